```python
import jax, jax.numpy as jnp
from jax import lax
import numpy as np

D_MODEL = 1024
BATCH = 8
SEQ = 4096
DEPTH = 4

EPS = 1e-6
PLE_DIM = 256
D_FF = 2816
SC_WIDTH = D_MODEL
SC_KERNEL = 3
SSM_INNER = 2 * D_MODEL
SSM_HEADDIM = 64
SSM_HEADS = SSM_INNER // SSM_HEADDIM
SSM_GROUPS = 4
SSM_STATE = 128
SSM_CONV = 4
SSM_CHUNK = 128
SSM_CONV_DIM = SSM_INNER + 2 * SSM_GROUPS * SSM_STATE
PROJ_SIZES = (SC_WIDTH, SC_WIDTH, SC_WIDTH,
              SSM_INNER, SSM_CONV_DIM, SSM_HEADS,
              D_MODEL, D_MODEL)
PROJ_DIM = sum(PROJ_SIZES)
PROJ_SPLITS = tuple(int(v) for v in np.cumsum(PROJ_SIZES)[:-1])

kernel_name = "hybrid_shortconv_ssd_macaron_block"


def rmsnorm(x, g):
    xf = x.astype(jnp.float32)
    y = xf * lax.rsqrt(jnp.mean(xf * xf, axis=-1, keepdims=True) + EPS)
    return y.astype(x.dtype) * g


def grouped_rmsnorm(x, g, groups):
    shp = x.shape
    xf = x.astype(jnp.float32).reshape(shp[:-1] + (groups, shp[-1] // groups))
    y = xf * lax.rsqrt(jnp.mean(xf * xf, axis=-1, keepdims=True) + EPS)
    return y.reshape(shp).astype(x.dtype) * g


def swiglu(x, wg, wu, wd):
    return (jax.nn.silu(x @ wg) * (x @ wu)) @ wd


def causal_depthwise_conv(x, w):
    k, c = w.shape
    return lax.conv_general_dilated(
        x, w.reshape(k, 1, c).astype(x.dtype), window_strides=(1,), padding=[(k - 1, 0)],
        dimension_numbers=("NWC", "WIO", "NWC"), feature_group_count=c)


def ssd_chunked(xdt, a, bm, cm):
    b, s, h, p = xdt.shape
    g, n = bm.shape[2], bm.shape[3]
    r = h // g
    nc, L = s // SSM_CHUNK, SSM_CHUNK
    dt_ = xdt.dtype
    X = xdt.reshape(b, nc, L, g, r, p)
    A = a.reshape(b, nc, L, g, r).astype(jnp.float32)
    Bc = bm.reshape(b, nc, L, g, n)
    Cc = cm.reshape(b, nc, L, g, n)
    a_cum = jnp.cumsum(A, axis=2)
    causal = jnp.tril(jnp.ones((L, L), dtype=bool))[None, None, :, :, None, None]
    diff = a_cum[:, :, :, None] - a_cum[:, :, None, :]
    decay = jnp.exp(jnp.where(causal, diff, -jnp.inf)).astype(dt_)
    cb = jnp.einsum("bclgn,bcsgn->bclsg", Cc, Bc)
    y_diag = jnp.einsum("bclsg,bclsgr,bcsgrp->bclgrp", cb, decay, X)
    decay_to_end = jnp.exp(a_cum[:, :, -1:] - a_cum).astype(dt_)
    states = jnp.einsum("bclgn,bclgr,bclgrp->bcgrpn", Bc, decay_to_end, X)
    chunk_decay = jnp.exp(a_cum[:, :, -1]).astype(dt_)

    def step(carry, inp):
        st, dec = inp
        return carry * dec[..., None, None] + st, carry

    init = jnp.zeros((b, g, r, p, n), dtype=states.dtype)
    _, prev = lax.scan(step, init, (jnp.swapaxes(states, 0, 1), jnp.swapaxes(chunk_decay, 0, 1)))
    prev = jnp.swapaxes(prev, 0, 1)
    y_off = jnp.einsum("bclgn,bcgrpn,bclgr->bclgrp", Cc, prev, jnp.exp(a_cum).astype(dt_))
    return (y_diag + y_off).reshape(b, s, h, p)


def hybrid_mixer(u, w_in, sc_conv_w, sc_w_out, m_conv_w, m_conv_b, m_dt_bias, m_A_log, m_D,
                 m_norm, m_w_out, w_o):
    b, s, _ = u.shape
    proj = u @ w_in
    sc_b, sc_c, sc_x, m_z, m_xbc, m_dt, gate_a, gate_m = jnp.split(proj, PROJ_SPLITS, axis=-1)
    y_a = (sc_b * causal_depthwise_conv(sc_c * sc_x, sc_conv_w)) @ sc_w_out
    xbc = jax.nn.silu(causal_depthwise_conv(m_xbc, m_conv_w) + m_conv_b)
    xs, bm, cm = jnp.split(xbc, (SSM_INNER, SSM_INNER + SSM_GROUPS * SSM_STATE), axis=-1)
    dt = jax.nn.softplus((m_dt + m_dt_bias).astype(jnp.float32))
    A = -jnp.exp(m_A_log.astype(jnp.float32))
    xh = xs.reshape(b, s, SSM_HEADS, SSM_HEADDIM)
    y = ssd_chunked(xh * dt.astype(xh.dtype)[..., None], A * dt,
                    bm.reshape(b, s, SSM_GROUPS, SSM_STATE), cm.reshape(b, s, SSM_GROUPS, SSM_STATE))
    y = (y + m_D[:, None] * xh).reshape(b, s, SSM_INNER)
    y_m = grouped_rmsnorm(y * jax.nn.silu(m_z), m_norm, SSM_GROUPS) @ m_w_out
    merged = jax.nn.sigmoid(gate_a) * y_a + jax.nn.sigmoid(gate_m) * y_m
    return merged @ w_o


def _fwd_setup_inputs(seed: int = 0) -> dict:
    key = jax.random.key(seed)
    ks = iter(jax.random.split(key, 40))

    def nrm(shape, fan_in):
        return jax.random.normal(next(ks), shape, jnp.float32) * (fan_in ** -0.5)

    def gain(shape):
        return 1.0 + 0.05 * jax.random.normal(next(ks), shape, jnp.float32)

    dt0 = jnp.exp(jax.random.uniform(next(ks), (DEPTH, SSM_HEADS), jnp.float32)
                  * (np.log(0.1) - np.log(0.001)) + np.log(0.001))
    return {
        "x": jax.random.normal(next(ks), (BATCH, SEQ, D_MODEL), jnp.float32),
        "p": jax.random.normal(next(ks), (DEPTH, BATCH, SEQ, PLE_DIM), jnp.float32),
        "ffn1_norm": gain((DEPTH, D_MODEL)),
        "ffn1_wg": nrm((DEPTH, D_MODEL, D_FF), D_MODEL),
        "ffn1_wu": nrm((DEPTH, D_MODEL, D_FF), D_MODEL),
        "ffn1_wd": nrm((DEPTH, D_FF, D_MODEL), D_FF),
        "mix_norm": gain((DEPTH, D_MODEL)),
        "w_in": nrm((DEPTH, D_MODEL, PROJ_DIM), D_MODEL),
        "sc_conv_w": nrm((DEPTH, SC_KERNEL, SC_WIDTH), SC_KERNEL),
        "sc_w_out": nrm((DEPTH, SC_WIDTH, D_MODEL), SC_WIDTH),
        "m_conv_w": nrm((DEPTH, SSM_CONV, SSM_CONV_DIM), SSM_CONV),
        "m_conv_b": 0.02 * jax.random.normal(next(ks), (DEPTH, SSM_CONV_DIM), jnp.float32),
        "m_dt_bias": dt0 + jnp.log(-jnp.expm1(-dt0)),
        "m_A_log": jnp.log(jax.random.uniform(next(ks), (DEPTH, SSM_HEADS), jnp.float32, 1.0, 16.0)),
        "m_D": gain((DEPTH, SSM_HEADS)),
        "m_norm": gain((DEPTH, SSM_INNER)),
        "m_w_out": nrm((DEPTH, SSM_INNER, D_MODEL), SSM_INNER),
        "w_o": nrm((DEPTH, D_MODEL, D_MODEL), D_MODEL),
        "ffn2_norm": gain((DEPTH, D_MODEL)),
        "ffn2_wg": nrm((DEPTH, D_MODEL, D_FF), D_MODEL),
        "ffn2_wu": nrm((DEPTH, D_MODEL, D_FF), D_MODEL),
        "ffn2_wd": nrm((DEPTH, D_FF, D_MODEL), D_FF),
        "ple_norm": gain((DEPTH, D_MODEL)),
        "ple_w_gate": nrm((DEPTH, D_MODEL, D_MODEL), D_MODEL),
        "ple_w_proj": nrm((DEPTH, PLE_DIM, D_MODEL), PLE_DIM),
        "final_norm": gain((D_MODEL,)),
    }


def _fwd_reference(x, p, ffn1_norm, ffn1_wg, ffn1_wu, ffn1_wd, mix_norm, w_in, sc_conv_w, sc_w_out,
              m_conv_w, m_conv_b, m_dt_bias, m_A_log, m_D, m_norm, m_w_out, w_o,
              ffn2_norm, ffn2_wg, ffn2_wu, ffn2_wd, ple_norm, ple_w_gate, ple_w_proj, final_norm):
    h = x
    for i in range(DEPTH):
        h = h + 0.5 * swiglu(rmsnorm(h, ffn1_norm[i]), ffn1_wg[i], ffn1_wu[i], ffn1_wd[i])
        h = h + hybrid_mixer(rmsnorm(h, mix_norm[i]), w_in[i], sc_conv_w[i], sc_w_out[i],
                             m_conv_w[i], m_conv_b[i], m_dt_bias[i], m_A_log[i], m_D[i],
                             m_norm[i], m_w_out[i], w_o[i])
        h = h + 0.5 * swiglu(rmsnorm(h, ffn2_norm[i]), ffn2_wg[i], ffn2_wu[i], ffn2_wd[i])
        gate = jax.nn.sigmoid(rmsnorm(h, ple_norm[i]) @ ple_w_gate[i])
        h = h + gate * (p[i] @ ple_w_proj[i])
    return rmsnorm(h, final_norm)


import jax as _jax
import jax.numpy as _jnp

TWIN_FORMAT = 'train_step'
FWD_PARAMS = ['x', 'p', 'ffn1_norm', 'ffn1_wg', 'ffn1_wu', 'ffn1_wd', 'mix_norm', 'w_in', 'sc_conv_w', 'sc_w_out', 'm_conv_w', 'm_conv_b', 'm_dt_bias', 'm_A_log', 'm_D', 'm_norm', 'm_w_out', 'w_o', 'ffn2_norm', 'ffn2_wg', 'ffn2_wu', 'ffn2_wd', 'ple_norm', 'ple_w_gate', 'ple_w_proj', 'final_norm']
TWIN_WEIGHTS = ['ffn1_norm', 'ffn1_wg', 'ffn1_wu', 'ffn1_wd', 'mix_norm', 'w_in', 'sc_conv_w', 'sc_w_out', 'm_conv_w', 'm_conv_b', 'm_dt_bias', 'm_A_log', 'm_D', 'm_norm', 'm_w_out', 'w_o', 'ffn2_norm', 'ffn2_wg', 'ffn2_wu', 'ffn2_wd', 'ple_norm', 'ple_w_gate', 'ple_w_proj', 'final_norm']
TWIN_DIFF_INPUT = 'x'
TWIN_INPUTS = ['x', 'p', 'ffn1_norm', 'ffn1_wg', 'ffn1_wu', 'ffn1_wd', 'mix_norm', 'w_in', 'sc_conv_w', 'sc_w_out', 'm_conv_w', 'm_conv_b', 'm_dt_bias', 'm_A_log', 'm_D', 'm_norm', 'm_w_out', 'w_o', 'ffn2_norm', 'ffn2_wg', 'ffn2_wu', 'ffn2_wd', 'ple_norm', 'ple_w_gate', 'ple_w_proj', 'final_norm', 'loss_target', 'm_ffn1_norm', 'm_ffn1_wg', 'm_ffn1_wu', 'm_ffn1_wd', 'm_mix_norm', 'm_w_in', 'm_sc_conv_w', 'm_sc_w_out', 'm_m_conv_w', 'm_m_conv_b', 'm_m_dt_bias', 'm_m_A_log', 'm_m_D', 'm_m_norm', 'm_m_w_out', 'm_w_o', 'm_ffn2_norm', 'm_ffn2_wg', 'm_ffn2_wu', 'm_ffn2_wd', 'm_ple_norm', 'm_ple_w_gate', 'm_ple_w_proj', 'm_final_norm', 'v_ffn1_norm', 'v_ffn1_wg', 'v_ffn1_wu', 'v_ffn1_wd', 'v_mix_norm', 'v_w_in', 'v_sc_conv_w', 'v_sc_w_out', 'v_m_conv_w', 'v_m_conv_b', 'v_m_dt_bias', 'v_m_A_log', 'v_m_D', 'v_m_norm', 'v_m_w_out', 'v_w_o', 'v_ffn2_norm', 'v_ffn2_wg', 'v_ffn2_wu', 'v_ffn2_wd', 'v_ple_norm', 'v_ple_w_gate', 'v_ple_w_proj', 'v_final_norm']
TWIN_OUTPUTS = ['loss', 'grad_x', 'grad_ffn1_norm', 'grad_ffn1_wg', 'grad_ffn1_wu', 'grad_ffn1_wd', 'grad_mix_norm', 'grad_w_in', 'grad_sc_conv_w', 'grad_sc_w_out', 'grad_m_conv_w', 'grad_m_conv_b', 'grad_m_dt_bias', 'grad_m_A_log', 'grad_m_D', 'grad_m_norm', 'grad_m_w_out', 'grad_w_o', 'grad_ffn2_norm', 'grad_ffn2_wg', 'grad_ffn2_wu', 'grad_ffn2_wd', 'grad_ple_norm', 'grad_ple_w_gate', 'grad_ple_w_proj', 'grad_final_norm', 'delta_ffn1_norm', 'delta_ffn1_wg', 'delta_ffn1_wu', 'delta_ffn1_wd', 'delta_mix_norm', 'delta_w_in', 'delta_sc_conv_w', 'delta_sc_w_out', 'delta_m_conv_w', 'delta_m_conv_b', 'delta_m_dt_bias', 'delta_m_A_log', 'delta_m_D', 'delta_m_norm', 'delta_m_w_out', 'delta_w_o', 'delta_ffn2_norm', 'delta_ffn2_wg', 'delta_ffn2_wu', 'delta_ffn2_wd', 'delta_ple_norm', 'delta_ple_w_gate', 'delta_ple_w_proj', 'delta_final_norm', 'new_m_ffn1_norm', 'new_m_ffn1_wg', 'new_m_ffn1_wu', 'new_m_ffn1_wd', 'new_m_mix_norm', 'new_m_w_in', 'new_m_sc_conv_w', 'new_m_sc_w_out', 'new_m_m_conv_w', 'new_m_m_conv_b', 'new_m_m_dt_bias', 'new_m_m_A_log', 'new_m_m_D', 'new_m_m_norm', 'new_m_m_w_out', 'new_m_w_o', 'new_m_ffn2_norm', 'new_m_ffn2_wg', 'new_m_ffn2_wu', 'new_m_ffn2_wd', 'new_m_ple_norm', 'new_m_ple_w_gate', 'new_m_ple_w_proj', 'new_m_final_norm', 'new_v_ffn1_norm', 'new_v_ffn1_wg', 'new_v_ffn1_wu', 'new_v_ffn1_wd', 'new_v_mix_norm', 'new_v_w_in', 'new_v_sc_conv_w', 'new_v_sc_w_out', 'new_v_m_conv_w', 'new_v_m_conv_b', 'new_v_m_dt_bias', 'new_v_m_A_log', 'new_v_m_D', 'new_v_m_norm', 'new_v_m_w_out', 'new_v_w_o', 'new_v_ffn2_norm', 'new_v_ffn2_wg', 'new_v_ffn2_wu', 'new_v_ffn2_wd', 'new_v_ple_norm', 'new_v_ple_w_gate', 'new_v_ple_w_proj', 'new_v_final_norm']
TWIN_LEAF_KINDS = {'loss': 'loss', 'grad_x': 'grad_x', 'grad_ffn1_norm': 'grad_w', 'grad_ffn1_wg': 'grad_w', 'grad_ffn1_wu': 'grad_w', 'grad_ffn1_wd': 'grad_w', 'grad_mix_norm': 'grad_w', 'grad_w_in': 'grad_w', 'grad_sc_conv_w': 'grad_w', 'grad_sc_w_out': 'grad_w', 'grad_m_conv_w': 'grad_w', 'grad_m_conv_b': 'grad_w', 'grad_m_dt_bias': 'grad_w', 'grad_m_A_log': 'grad_w', 'grad_m_D': 'grad_w', 'grad_m_norm': 'grad_w', 'grad_m_w_out': 'grad_w', 'grad_w_o': 'grad_w', 'grad_ffn2_norm': 'grad_w', 'grad_ffn2_wg': 'grad_w', 'grad_ffn2_wu': 'grad_w', 'grad_ffn2_wd': 'grad_w', 'grad_ple_norm': 'grad_w', 'grad_ple_w_gate': 'grad_w', 'grad_ple_w_proj': 'grad_w', 'grad_final_norm': 'grad_w', 'delta_ffn1_norm': 'delta_w', 'delta_ffn1_wg': 'delta_w', 'delta_ffn1_wu': 'delta_w', 'delta_ffn1_wd': 'delta_w', 'delta_mix_norm': 'delta_w', 'delta_w_in': 'delta_w', 'delta_sc_conv_w': 'delta_w', 'delta_sc_w_out': 'delta_w', 'delta_m_conv_w': 'delta_w', 'delta_m_conv_b': 'delta_w', 'delta_m_dt_bias': 'delta_w', 'delta_m_A_log': 'delta_w', 'delta_m_D': 'delta_w', 'delta_m_norm': 'delta_w', 'delta_m_w_out': 'delta_w', 'delta_w_o': 'delta_w', 'delta_ffn2_norm': 'delta_w', 'delta_ffn2_wg': 'delta_w', 'delta_ffn2_wu': 'delta_w', 'delta_ffn2_wd': 'delta_w', 'delta_ple_norm': 'delta_w', 'delta_ple_w_gate': 'delta_w', 'delta_ple_w_proj': 'delta_w', 'delta_final_norm': 'delta_w', 'new_m_ffn1_norm': 'new_m', 'new_m_ffn1_wg': 'new_m', 'new_m_ffn1_wu': 'new_m', 'new_m_ffn1_wd': 'new_m', 'new_m_mix_norm': 'new_m', 'new_m_w_in': 'new_m', 'new_m_sc_conv_w': 'new_m', 'new_m_sc_w_out': 'new_m', 'new_m_m_conv_w': 'new_m', 'new_m_m_conv_b': 'new_m', 'new_m_m_dt_bias': 'new_m', 'new_m_m_A_log': 'new_m', 'new_m_m_D': 'new_m', 'new_m_m_norm': 'new_m', 'new_m_m_w_out': 'new_m', 'new_m_w_o': 'new_m', 'new_m_ffn2_norm': 'new_m', 'new_m_ffn2_wg': 'new_m', 'new_m_ffn2_wu': 'new_m', 'new_m_ffn2_wd': 'new_m', 'new_m_ple_norm': 'new_m', 'new_m_ple_w_gate': 'new_m', 'new_m_ple_w_proj': 'new_m', 'new_m_final_norm': 'new_m', 'new_v_ffn1_norm': 'new_v', 'new_v_ffn1_wg': 'new_v', 'new_v_ffn1_wu': 'new_v', 'new_v_ffn1_wd': 'new_v', 'new_v_mix_norm': 'new_v', 'new_v_w_in': 'new_v', 'new_v_sc_conv_w': 'new_v', 'new_v_sc_w_out': 'new_v', 'new_v_m_conv_w': 'new_v', 'new_v_m_conv_b': 'new_v', 'new_v_m_dt_bias': 'new_v', 'new_v_m_A_log': 'new_v', 'new_v_m_D': 'new_v', 'new_v_m_norm': 'new_v', 'new_v_m_w_out': 'new_v', 'new_v_w_o': 'new_v', 'new_v_ffn2_norm': 'new_v', 'new_v_ffn2_wg': 'new_v', 'new_v_ffn2_wu': 'new_v', 'new_v_ffn2_wd': 'new_v', 'new_v_ple_norm': 'new_v', 'new_v_ple_w_gate': 'new_v', 'new_v_ple_w_proj': 'new_v', 'new_v_final_norm': 'new_v'}


def _forward(args):
    return _fwd_reference(*[args[k] for k in FWD_PARAMS])


def _output_shape():
    out = _jax.eval_shape(lambda: _forward(_fwd_setup_inputs(0)))
    return out.shape, out.dtype

N_MICROBATCH = 1
ADAM_LR = 0.001
ADAM_B1 = 0.9
ADAM_B2 = 0.999
ADAM_EPS = 1e-08
ADAM_WD = 0.01
ADAM_STEP = 10
PER_EXAMPLE_BATCH_AXIS = {'x': 0, 'p': 1, 'loss_target': 0}
SHARED_INPUTS = []
_WEIGHT_DTYPES = {'ffn1_norm': _jnp.float32, 'ffn1_wg': _jnp.float32, 'ffn1_wu': _jnp.float32, 'ffn1_wd': _jnp.float32, 'mix_norm': _jnp.float32, 'w_in': _jnp.float32, 'sc_conv_w': _jnp.float32, 'sc_w_out': _jnp.float32, 'm_conv_w': _jnp.float32, 'm_conv_b': _jnp.float32, 'm_dt_bias': _jnp.float32, 'm_A_log': _jnp.float32, 'm_D': _jnp.float32, 'm_norm': _jnp.float32, 'm_w_out': _jnp.float32, 'w_o': _jnp.float32, 'ffn2_norm': _jnp.float32, 'ffn2_wg': _jnp.float32, 'ffn2_wu': _jnp.float32, 'ffn2_wd': _jnp.float32, 'ple_norm': _jnp.float32, 'ple_w_gate': _jnp.float32, 'ple_w_proj': _jnp.float32, 'final_norm': _jnp.float32}
MOMENT_SCALE = {'ffn1_norm': 8.258934e-02, 'ffn1_wg': 3.557035e-02, 'ffn1_wu': 3.446121e-02, 'ffn1_wd': 5.714113e-02, 'mix_norm': 1.828042e-01, 'w_in': 5.354436e-02, 'sc_conv_w': 7.317987e-02, 'sc_w_out': 7.199167e-02, 'm_conv_w': 4.527008e-02, 'm_conv_b': 6.227937e-02, 'm_dt_bias': 1.028519e-01, 'm_A_log': 1.346648e-01, 'm_D': 2.833362e-01, 'm_norm': 5.278604e-02, 'm_w_out': 7.217518e-02, 'w_o': 1.021984e-01, 'ffn2_norm': 5.540587e-02, 'ffn2_wg': 2.370514e-02, 'ffn2_wu': 2.297349e-02, 'ffn2_wd': 3.814509e-02, 'ple_norm': 2.652762e-02, 'ple_w_gate': 2.680512e-02, 'ple_w_proj': 6.825797e-02, 'final_norm': 3.204335e+01}


def _to_microbatches(a, axis):
    t = _jnp.moveaxis(a, axis, 0)
    t = t.reshape((N_MICROBATCH, t.shape[0] // N_MICROBATCH) + t.shape[1:])
    return _jnp.moveaxis(t, 1, axis + 1)


def setup_inputs(seed: int = 0) -> dict:
    inp = _fwd_setup_inputs(seed)
    key = _jax.random.fold_in(_jax.random.key(seed), 7919)
    shape, _ = _output_shape()
    out = dict(inp)
    out["loss_target"] = _jax.random.normal(_jax.random.fold_in(key, 0), shape, _jnp.float32)
    for i, name in enumerate(TWIN_WEIGHTS):
        w = inp[name].astype(_jnp.float32)
        if MOMENT_SCALE is None:
            s = _jnp.sqrt(_jnp.mean(_jnp.square(w)) + 1e-30)
        else:
            s = MOMENT_SCALE[name]
        km, kv = _jax.random.split(_jax.random.fold_in(key, i + 1))
        out[name] = w
        out["m_" + name] = s * _jax.random.normal(km, w.shape, _jnp.float32)
        out["v_" + name] = (s * s) * _jax.random.uniform(kv, w.shape, _jnp.float32, 0.5, 1.5)
    if N_MICROBATCH > 1:
        for name, axis in PER_EXAMPLE_BATCH_AXIS.items():
            out[name] = _to_microbatches(out[name], axis)
    return {'x': out['x'], 'p': out['p'], 'ffn1_norm': out['ffn1_norm'], 'ffn1_wg': out['ffn1_wg'], 'ffn1_wu': out['ffn1_wu'], 'ffn1_wd': out['ffn1_wd'], 'mix_norm': out['mix_norm'], 'w_in': out['w_in'], 'sc_conv_w': out['sc_conv_w'], 'sc_w_out': out['sc_w_out'], 'm_conv_w': out['m_conv_w'], 'm_conv_b': out['m_conv_b'], 'm_dt_bias': out['m_dt_bias'], 'm_A_log': out['m_A_log'], 'm_D': out['m_D'], 'm_norm': out['m_norm'], 'm_w_out': out['m_w_out'], 'w_o': out['w_o'], 'ffn2_norm': out['ffn2_norm'], 'ffn2_wg': out['ffn2_wg'], 'ffn2_wu': out['ffn2_wu'], 'ffn2_wd': out['ffn2_wd'], 'ple_norm': out['ple_norm'], 'ple_w_gate': out['ple_w_gate'], 'ple_w_proj': out['ple_w_proj'], 'final_norm': out['final_norm'], 'loss_target': out['loss_target'], 'm_ffn1_norm': out['m_ffn1_norm'], 'm_ffn1_wg': out['m_ffn1_wg'], 'm_ffn1_wu': out['m_ffn1_wu'], 'm_ffn1_wd': out['m_ffn1_wd'], 'm_mix_norm': out['m_mix_norm'], 'm_w_in': out['m_w_in'], 'm_sc_conv_w': out['m_sc_conv_w'], 'm_sc_w_out': out['m_sc_w_out'], 'm_m_conv_w': out['m_m_conv_w'], 'm_m_conv_b': out['m_m_conv_b'], 'm_m_dt_bias': out['m_m_dt_bias'], 'm_m_A_log': out['m_m_A_log'], 'm_m_D': out['m_m_D'], 'm_m_norm': out['m_m_norm'], 'm_m_w_out': out['m_m_w_out'], 'm_w_o': out['m_w_o'], 'm_ffn2_norm': out['m_ffn2_norm'], 'm_ffn2_wg': out['m_ffn2_wg'], 'm_ffn2_wu': out['m_ffn2_wu'], 'm_ffn2_wd': out['m_ffn2_wd'], 'm_ple_norm': out['m_ple_norm'], 'm_ple_w_gate': out['m_ple_w_gate'], 'm_ple_w_proj': out['m_ple_w_proj'], 'm_final_norm': out['m_final_norm'], 'v_ffn1_norm': out['v_ffn1_norm'], 'v_ffn1_wg': out['v_ffn1_wg'], 'v_ffn1_wu': out['v_ffn1_wu'], 'v_ffn1_wd': out['v_ffn1_wd'], 'v_mix_norm': out['v_mix_norm'], 'v_w_in': out['v_w_in'], 'v_sc_conv_w': out['v_sc_conv_w'], 'v_sc_w_out': out['v_sc_w_out'], 'v_m_conv_w': out['v_m_conv_w'], 'v_m_conv_b': out['v_m_conv_b'], 'v_m_dt_bias': out['v_m_dt_bias'], 'v_m_A_log': out['v_m_A_log'], 'v_m_D': out['v_m_D'], 'v_m_norm': out['v_m_norm'], 'v_m_w_out': out['v_m_w_out'], 'v_w_o': out['v_w_o'], 'v_ffn2_norm': out['v_ffn2_norm'], 'v_ffn2_wg': out['v_ffn2_wg'], 'v_ffn2_wu': out['v_ffn2_wu'], 'v_ffn2_wd': out['v_ffn2_wd'], 'v_ple_norm': out['v_ple_norm'], 'v_ple_w_gate': out['v_ple_w_gate'], 'v_ple_w_proj': out['v_ple_w_proj'], 'v_final_norm': out['v_final_norm']}


def _loss(weights, diff, rest, loss_target):
    with _jax.named_scope("forward"):
        args = {**rest, TWIN_DIFF_INPUT: diff, **{k: w.astype(_WEIGHT_DTYPES[k]) for k, w in weights.items()}}
        y = _forward(args)
    with _jax.named_scope("loss_head"):
        err = _jnp.square(y.astype(_jnp.float32) - loss_target)
        return 0.5 * _jnp.sum(_jnp.mean(err, axis=-1)) if err.ndim else 0.5 * err


def _adamw(w, g, m, v):
    m = ADAM_B1 * m + (1.0 - ADAM_B1) * g
    v = ADAM_B2 * v + (1.0 - ADAM_B2) * _jnp.square(g)
    m_hat = m / (1.0 - ADAM_B1 ** ADAM_STEP)
    v_hat = v / (1.0 - ADAM_B2 ** ADAM_STEP)
    delta = -ADAM_LR * (m_hat / (_jnp.sqrt(v_hat) + ADAM_EPS) + ADAM_WD * w)
    return delta, m, v


def reference(x, p, ffn1_norm, ffn1_wg, ffn1_wu, ffn1_wd, mix_norm, w_in, sc_conv_w, sc_w_out, m_conv_w, m_conv_b, m_dt_bias, m_A_log, m_D, m_norm, m_w_out, w_o, ffn2_norm, ffn2_wg, ffn2_wu, ffn2_wd, ple_norm, ple_w_gate, ple_w_proj, final_norm, loss_target, m_ffn1_norm, m_ffn1_wg, m_ffn1_wu, m_ffn1_wd, m_mix_norm, m_w_in, m_sc_conv_w, m_sc_w_out, m_m_conv_w, m_m_conv_b, m_m_dt_bias, m_m_A_log, m_m_D, m_m_norm, m_m_w_out, m_w_o, m_ffn2_norm, m_ffn2_wg, m_ffn2_wu, m_ffn2_wd, m_ple_norm, m_ple_w_gate, m_ple_w_proj, m_final_norm, v_ffn1_norm, v_ffn1_wg, v_ffn1_wu, v_ffn1_wd, v_mix_norm, v_w_in, v_sc_conv_w, v_sc_w_out, v_m_conv_w, v_m_conv_b, v_m_dt_bias, v_m_A_log, v_m_D, v_m_norm, v_m_w_out, v_w_o, v_ffn2_norm, v_ffn2_wg, v_ffn2_wu, v_ffn2_wd, v_ple_norm, v_ple_w_gate, v_ple_w_proj, v_final_norm):
    given = dict(x=x, p=p, ffn1_norm=ffn1_norm, ffn1_wg=ffn1_wg, ffn1_wu=ffn1_wu, ffn1_wd=ffn1_wd, mix_norm=mix_norm, w_in=w_in, sc_conv_w=sc_conv_w, sc_w_out=sc_w_out, m_conv_w=m_conv_w, m_conv_b=m_conv_b, m_dt_bias=m_dt_bias, m_A_log=m_A_log, m_D=m_D, m_norm=m_norm, m_w_out=m_w_out, w_o=w_o, ffn2_norm=ffn2_norm, ffn2_wg=ffn2_wg, ffn2_wu=ffn2_wu, ffn2_wd=ffn2_wd, ple_norm=ple_norm, ple_w_gate=ple_w_gate, ple_w_proj=ple_w_proj, final_norm=final_norm, loss_target=loss_target, m_ffn1_norm=m_ffn1_norm, m_ffn1_wg=m_ffn1_wg, m_ffn1_wu=m_ffn1_wu, m_ffn1_wd=m_ffn1_wd, m_mix_norm=m_mix_norm, m_w_in=m_w_in, m_sc_conv_w=m_sc_conv_w, m_sc_w_out=m_sc_w_out, m_m_conv_w=m_m_conv_w, m_m_conv_b=m_m_conv_b, m_m_dt_bias=m_m_dt_bias, m_m_A_log=m_m_A_log, m_m_D=m_m_D, m_m_norm=m_m_norm, m_m_w_out=m_m_w_out, m_w_o=m_w_o, m_ffn2_norm=m_ffn2_norm, m_ffn2_wg=m_ffn2_wg, m_ffn2_wu=m_ffn2_wu, m_ffn2_wd=m_ffn2_wd, m_ple_norm=m_ple_norm, m_ple_w_gate=m_ple_w_gate, m_ple_w_proj=m_ple_w_proj, m_final_norm=m_final_norm, v_ffn1_norm=v_ffn1_norm, v_ffn1_wg=v_ffn1_wg, v_ffn1_wu=v_ffn1_wu, v_ffn1_wd=v_ffn1_wd, v_mix_norm=v_mix_norm, v_w_in=v_w_in, v_sc_conv_w=v_sc_conv_w, v_sc_w_out=v_sc_w_out, v_m_conv_w=v_m_conv_w, v_m_conv_b=v_m_conv_b, v_m_dt_bias=v_m_dt_bias, v_m_A_log=v_m_A_log, v_m_D=v_m_D, v_m_norm=v_m_norm, v_m_w_out=v_m_w_out, v_w_o=v_w_o, v_ffn2_norm=v_ffn2_norm, v_ffn2_wg=v_ffn2_wg, v_ffn2_wu=v_ffn2_wu, v_ffn2_wd=v_ffn2_wd, v_ple_norm=v_ple_norm, v_ple_w_gate=v_ple_w_gate, v_ple_w_proj=v_ple_w_proj, v_final_norm=v_final_norm)
    weights = {n: given[n] for n in TWIN_WEIGHTS}
    shared = {n: given[n] for n in SHARED_INPUTS}
    per_example = {n: given[n] for n in ['x', 'p']}
    grad_fn = _jax.value_and_grad(_loss, argnums=(0, 1))

    def one_microbatch(ex, loss_target):
        ex = dict(ex)
        diff = ex.pop(TWIN_DIFF_INPUT)
        return grad_fn(weights, diff, {**shared, **ex}, loss_target)

    if N_MICROBATCH == 1:
        loss, (grad_w, grad_x) = one_microbatch(per_example, given["loss_target"])
    else:
        def body(carry, xs):
            loss_sum, grad_sum = carry
            l_k, (gw_k, gx_k) = one_microbatch(xs[0], xs[1])
            with _jax.named_scope("update"):
                return (loss_sum + l_k, _jax.tree.map(_jnp.add, grad_sum, gw_k)), gx_k

        init = (_jnp.zeros((), _jnp.float32), _jax.tree.map(_jnp.zeros_like, weights))
        (loss, grad_w), grad_x = _jax.lax.scan(body, init, (per_example, given["loss_target"]))
    with _jax.named_scope("update"):
        delta_w, new_m, new_v = {}, {}, {}
        for n in TWIN_WEIGHTS:
            delta_w[n], new_m[n], new_v[n] = _adamw(weights[n], grad_w[n], given["m_" + n], given["v_" + n])
    return (loss, grad_x, *[grad_w[n] for n in TWIN_WEIGHTS], *[delta_w[n] for n in TWIN_WEIGHTS],
            *[new_m[n] for n in TWIN_WEIGHTS], *[new_v[n] for n in TWIN_WEIGHTS])
```

```python
import functools

import jax
import jax.numpy as jnp
from jax import lax
from jax.experimental import pallas as pl
from jax.experimental.pallas import tpu as pltpu

BF = jnp.bfloat16
F32 = jnp.float32

EPS = 1e-6
N_DEV = 8
LANES = 128
SSM_GROUPS = 4
SSM_HEADDIM = 64
SSM_CHUNK = 128
HALO = 16
VMEM_LIMIT = 56 * 1024 * 1024
FLAT_ROW_TILE = 2048

ADAM_LR = 0.001
ADAM_B1 = 0.9
ADAM_B2 = 0.999
ADAM_EPS = 1e-08
ADAM_WD = 0.01
ADAM_STEP = 10

MESH = pl.DeviceIdType.MESH

ARG_NAMES = ['x', 'p', 'ffn1_norm', 'ffn1_wg', 'ffn1_wu', 'ffn1_wd', 'mix_norm', 'w_in', 'sc_conv_w', 'sc_w_out', 'm_conv_w', 'm_conv_b', 'm_dt_bias', 'm_A_log', 'm_D', 'm_norm', 'm_w_out', 'w_o', 'ffn2_norm', 'ffn2_wg', 'ffn2_wu', 'ffn2_wd', 'ple_norm', 'ple_w_gate', 'ple_w_proj', 'final_norm', 'loss_target']
WEIGHTS = ARG_NAMES[2:26]
BIG = [('ffn1_wg', 'col'), ('ffn1_wu', 'col'), ('ffn1_wd', 'row'), ('w_in', 'col'), ('sc_w_out', 'row'),
       ('m_w_out', 'row'), ('w_o', 'row'), ('ffn2_wg', 'col'), ('ffn2_wu', 'col'), ('ffn2_wd', 'row'),
       ('ple_w_gate', 'row'), ('ple_w_proj', 'col')]
CONVW = ['sc_conv_w', 'm_conv_w']
SMALL = ['ffn1_norm', 'mix_norm', 'm_conv_b', 'm_dt_bias', 'm_A_log', 'm_D', 'm_norm', 'ffn2_norm', 'ple_norm']


def _pick(n, cands):
    for c in cands:
        if n % c == 0:
            return c
    return n


def _cp(sem):
    return pltpu.CompilerParams(dimension_semantics=sem, vmem_limit_bytes=VMEM_LIMIT)


def _sigmoid(x):
    return 1.0 / (1.0 + jnp.exp(-x))


def _softplus(x):
    return jnp.maximum(x, 0.0) + jnp.log(1.0 + jnp.exp(-jnp.abs(x)))


def _exchange(name, x, gather):
    slab = x.shape if gather else x.shape[1:]

    def body(x_ref, o_ref, send_sems, recv_sems, local_sem):
        mx, my, mc = lax.axis_index("x"), lax.axis_index("y"), lax.axis_index("c")
        me = 4 * mx + 2 * my + mc

        def src_for(k):
            return x_ref if gather else x_ref.at[k]

        local = pltpu.make_async_copy(src_for(me), o_ref.at[me], local_sem)
        local.start()
        sends = []
        peers = []
        for r in range(1, N_DEV):
            px = (mx + ((r >> 2) & 1)) % 2
            py = (my + ((r >> 1) & 1)) % 2
            pc = (mc + (r & 1)) % 2
            peer = 4 * px + 2 * py + pc
            peers.append(peer)
            cp = pltpu.make_async_remote_copy(
                src_ref=src_for(peer), dst_ref=o_ref.at[me], send_sem=send_sems.at[r - 1], recv_sem=recv_sems.at[r - 1],
                device_id=(px, py, pc), device_id_type=MESH)
            cp.start()
            sends.append(cp)
        for r in range(1, N_DEV):
            peer = peers[r - 1]
            pltpu.make_async_remote_copy(
                src_ref=src_for(peer), dst_ref=o_ref.at[peer], send_sem=send_sems.at[r - 1], recv_sem=recv_sems.at[r - 1],
                device_id=(mx, my, mc), device_id_type=MESH).wait_recv()
        for cp in sends:
            cp.wait_send()
        local.wait()

    return pl.pallas_call(
        body, name=name,
        out_shape=jax.ShapeDtypeStruct((N_DEV,) + tuple(slab), x.dtype),
        in_specs=[pl.BlockSpec(memory_space=pltpu.HBM)],
        out_specs=pl.BlockSpec(memory_space=pltpu.HBM),
        scratch_shapes=[pltpu.SemaphoreType.DMA((N_DEV - 1,)), pltpu.SemaphoreType.DMA((N_DEV - 1,)), pltpu.SemaphoreType.DMA],
    )(x)


def _mm(name, a, b, *, ta=False, tb=False, out_dtype=None, res=None, alpha=1.0):
    out_dtype = out_dtype or BF
    M, K = (a.shape[1], a.shape[0]) if ta else a.shape
    N = b.shape[0] if tb else b.shape[1]
    assert (b.shape[1] if tb else b.shape[0]) == K, (name, a.shape, b.shape)
    tm = _pick(M, (1024, 512, 256, 128))
    tn = _pick(N, (1024, 512, 256, 128))
    tk = _pick(K, (512, 256, 128))
    nk = K // tk
    a_spec = pl.BlockSpec((tk, tm), lambda i, j, k: (k, i)) if ta else pl.BlockSpec((tm, tk), lambda i, j, k: (i, k))
    b_spec = pl.BlockSpec((tn, tk), lambda i, j, k: (j, k)) if tb else pl.BlockSpec((tk, tn), lambda i, j, k: (k, j))
    dn = (((0 if ta else 1,), (1 if tb else 0,)), ((), ()))
    has_res = res is not None

    def body(*refs):
        if has_res:
            a_ref, b_ref, r_ref, o_ref, acc = refs
        else:
            a_ref, b_ref, o_ref, acc = refs
        k = pl.program_id(2)

        @pl.when(k == 0)
        def _():
            acc[...] = jnp.zeros_like(acc)

        acc[...] += lax.dot_general(a_ref[...].astype(BF), b_ref[...].astype(BF), dn, preferred_element_type=F32)

        @pl.when(k == nk - 1)
        def _():
            v = acc[...] * alpha if alpha != 1.0 else acc[...]
            if has_res:
                v = r_ref[...] + v
            o_ref[...] = v.astype(o_ref.dtype)

    in_specs = [a_spec, b_spec]
    args = [a, b]
    if has_res:
        in_specs.append(pl.BlockSpec((tm, tn), lambda i, j, k: (i, j)))
        args.append(res)
    return pl.pallas_call(
        body, name=name, grid=(M // tm, N // tn, nk),
        in_specs=in_specs, out_specs=pl.BlockSpec((tm, tn), lambda i, j, k: (i, j)),
        out_shape=jax.ShapeDtypeStruct((M, N), out_dtype),
        scratch_shapes=[pltpu.VMEM((tm, tn), F32)],
        compiler_params=_cp(("parallel", "parallel", "arbitrary")),
    )(*args)


def _ew(name, fn, tiled, params, outs, accs=(), tile=256):
    S = tiled[0].shape[0]
    T = _pick(S, (tile, 128, 64, 32, 16))

    def body(*refs):
        fn(pl.program_id(0) == 0, *refs)

    in_specs = [pl.BlockSpec((T, t.shape[1]), lambda i: (i, 0)) for t in tiled]
    in_specs += [pl.BlockSpec(p.shape, lambda i: (0, 0)) for p in params]
    out_specs = [pl.BlockSpec((T, w), lambda i: (i, 0)) for w, _ in outs]
    out_specs += [pl.BlockSpec(shp, lambda i: (0, 0)) for shp, _ in accs]
    out_shape = [jax.ShapeDtypeStruct((S, w), dt) for w, dt in outs]
    out_shape += [jax.ShapeDtypeStruct(shp, dt) for shp, dt in accs]
    res = pl.pallas_call(
        body, name=name, grid=(S // T,), in_specs=in_specs, out_specs=out_specs, out_shape=out_shape,
        compiler_params=_cp(("arbitrary",)),
    )(*tiled, *params)
    return res


def _rms_fwd(name, h, g):
    def fn(first, h_ref, g_ref, o_ref):
        x = h_ref[...]
        r = lax.rsqrt(jnp.mean(x * x, axis=-1, keepdims=True) + EPS)
        o_ref[...] = (x * r * g_ref[...]).astype(o_ref.dtype)

    return _ew(name, fn, [h], [g.reshape(1, -1)], [(h.shape[1], BF)])[0]


def _rms_bwd(name, dxn, h, g, res):
    D = h.shape[1]

    def fn(first, d_ref, h_ref, r_ref, g_ref, o_ref, dg_ref):
        x = h_ref[...]
        d = d_ref[...].astype(F32)
        r = lax.rsqrt(jnp.mean(x * x, axis=-1, keepdims=True) + EPS)
        xhat = x * r
        dxhat = d * g_ref[...]
        dh = r * (dxhat - xhat * jnp.mean(dxhat * xhat, axis=-1, keepdims=True))
        o_ref[...] = r_ref[...] + dh

        @pl.when(first)
        def _():
            dg_ref[...] = jnp.zeros_like(dg_ref)

        dg_ref[...] += jnp.sum(d * xhat, axis=0, keepdims=True)

    return _ew(name, fn, [dxn, h, res], [g.reshape(1, -1)], [(D, F32)], [((1, D), F32)])


def _swiglu_fwd(name, ab):
    FF = ab.shape[1] // 2

    def fn(first, ab_ref, o_ref):
        a = ab_ref[:, :FF].astype(F32)
        b = ab_ref[:, FF:].astype(F32)
        o_ref[...] = (a * _sigmoid(a) * b).astype(o_ref.dtype)

    return _ew(name, fn, [ab], [], [(FF, BF)])[0]


def _swiglu_bwd(name, dhmid, ab):
    FF = ab.shape[1] // 2

    def fn(first, d_ref, ab_ref, o_ref):
        a = ab_ref[:, :FF].astype(F32)
        b = ab_ref[:, FF:].astype(F32)
        d = d_ref[...].astype(F32)
        s = _sigmoid(a)
        o_ref[:, :FF] = (d * b * (s * (1.0 + a * (1.0 - s)))).astype(o_ref.dtype)
        o_ref[:, FF:] = (d * a * s).astype(o_ref.dtype)

    return _ew(name, fn, [dhmid, ab], [], [(2 * FF, BF)])[0]


def _merge_fwd(name, ga, gm, ya, ym):
    def fn(first, ga_ref, gm_ref, ya_ref, ym_ref, o_ref):
        o = _sigmoid(ga_ref[...].astype(F32)) * ya_ref[...].astype(F32) + _sigmoid(gm_ref[...].astype(F32)) * ym_ref[...].astype(F32)
        o_ref[...] = o.astype(o_ref.dtype)

    return _ew(name, fn, [ga, gm, ya, ym], [], [(ga.shape[1], BF)])[0]


def _merge_bwd(name, dmerged, ga, gm, ya, ym):
    W = ga.shape[1]

    def fn(first, d_ref, ga_ref, gm_ref, ya_ref, ym_ref, dga_ref, dgm_ref, dya_ref, dym_ref):
        d = d_ref[...].astype(F32)
        sa = _sigmoid(ga_ref[...].astype(F32))
        sm = _sigmoid(gm_ref[...].astype(F32))
        dga_ref[...] = (d * ya_ref[...].astype(F32) * sa * (1.0 - sa)).astype(BF)
        dgm_ref[...] = (d * ym_ref[...].astype(F32) * sm * (1.0 - sm)).astype(BF)
        dya_ref[...] = (d * sa).astype(BF)
        dym_ref[...] = (d * sm).astype(BF)

    return _ew(name, fn, [dmerged, ga, gm, ya, ym], [], [(W, BF)] * 4)


def _gnorm_fwd(name, y, z, w):
    W = y.shape[1]
    gw = W // SSM_GROUPS

    def fn(first, y_ref, z_ref, w_ref, o_ref):
        for g in range(SSM_GROUPS):
            sl = slice(g * gw, (g + 1) * gw)
            zz = z_ref[:, sl].astype(F32)
            t = y_ref[:, sl].astype(F32) * (zz * _sigmoid(zz))
            r = lax.rsqrt(jnp.mean(t * t, axis=-1, keepdims=True) + EPS)
            o_ref[:, sl] = (t * r * w_ref[:, sl]).astype(o_ref.dtype)

    return _ew(name, fn, [y, z], [w.reshape(1, -1)], [(W, BF)])[0]


def _gnorm_bwd(name, dyn, y, z, w):
    W = y.shape[1]
    gw = W // SSM_GROUPS

    def fn(first, d_ref, y_ref, z_ref, w_ref, dy_ref, dz_ref, dw_ref):
        @pl.when(first)
        def _():
            dw_ref[...] = jnp.zeros_like(dw_ref)

        for g in range(SSM_GROUPS):
            sl = slice(g * gw, (g + 1) * gw)
            zz = z_ref[:, sl].astype(F32)
            yy = y_ref[:, sl].astype(F32)
            d = d_ref[:, sl].astype(F32)
            s = _sigmoid(zz)
            sz = zz * s
            t = yy * sz
            r = lax.rsqrt(jnp.mean(t * t, axis=-1, keepdims=True) + EPS)
            that = t * r
            dthat = d * w_ref[:, sl]
            dt = r * (dthat - that * jnp.mean(dthat * that, axis=-1, keepdims=True))
            dw_ref[:, sl] += jnp.sum(d * that, axis=0, keepdims=True)
            dy_ref[:, sl] = (dt * sz).astype(BF)
            dz_ref[:, sl] = (dt * yy * (s * (1.0 + zz * (1.0 - s)))).astype(BF)

    return _ew(name, fn, [dyn, y, z], [w.reshape(1, -1)], [(W, BF), (W, BF)], [((1, W), F32)])


def _ple_fwd(name, h, gpre, pp):
    def fn(first, h_ref, g_ref, p_ref, o_ref):
        o_ref[...] = h_ref[...] + _sigmoid(g_ref[...].astype(F32)) * p_ref[...].astype(F32)

    return _ew(name, fn, [h, gpre, pp], [], [(h.shape[1], F32)])[0]


def _ple_bwd(name, dh, gpre, pp):
    W = dh.shape[1]

    def fn(first, d_ref, g_ref, p_ref, dg_ref, dp_ref):
        d = d_ref[...]
        s = _sigmoid(g_ref[...].astype(F32))
        dg_ref[...] = (d * p_ref[...].astype(F32) * s * (1.0 - s)).astype(BF)
        dp_ref[...] = (d * s).astype(BF)

    return _ew(name, fn, [dh, gpre, pp], [], [(W, BF), (W, BF)])


def _loss_head(name, h, g, target):
    D = h.shape[1]

    def fn(first, h_ref, t_ref, g_ref, dh_ref, loss_ref, dg_ref):
        x = h_ref[...]
        r = lax.rsqrt(jnp.mean(x * x, axis=-1, keepdims=True) + EPS)
        xhat = x * r
        err = xhat * g_ref[...] - t_ref[...]
        part = 0.5 * jnp.sum(jnp.mean(err * err, axis=-1, keepdims=True), axis=0, keepdims=True)
        dy = err * (1.0 / D)
        dxhat = dy * g_ref[...]
        dh_ref[...] = r * (dxhat - xhat * jnp.mean(dxhat * xhat, axis=-1, keepdims=True))

        @pl.when(first)
        def _():
            loss_ref[...] = jnp.zeros_like(loss_ref)
            dg_ref[...] = jnp.zeros_like(dg_ref)

        loss_ref[...] += jnp.broadcast_to(part, loss_ref.shape)
        dg_ref[...] += jnp.sum(dy * xhat, axis=0, keepdims=True)

    return _ew(name, fn, [h, target], [g.reshape(1, -1)], [(D, F32)], [((1, LANES), F32), ((1, D), F32)])


def _conv_specs(S, C):
    T = _pick(S, (512, 256, 128, 64, 32, 16))
    Ct = _pick(C, (512, 256, 128))
    per = T // HALO
    last = S // HALO - 1
    cur = pl.BlockSpec((T, Ct), lambda j, i: (i, j))
    prev = pl.BlockSpec((HALO, Ct), lambda j, i: (jnp.maximum(i * per - 1, 0), j))
    nxt = pl.BlockSpec((HALO, Ct), lambda j, i: (jnp.minimum((i + 1) * per, last), j))
    wspec = pl.BlockSpec((8, Ct), lambda j, i: (0, j))
    return T, Ct, cur, prev, nxt, wspec


def _pad_taps(w):
    return jnp.concatenate([w.astype(F32), jnp.zeros((8 - w.shape[0], w.shape[1]), F32)], axis=0)


def _causal(cat, w_ref, K, T, lead):
    out = None
    for k in range(K):
        o = lead - (K - 1) + k
        term = w_ref[k:k + 1, :] * cat[o:o + T]
        out = term if out is None else out + term
    return out


def _anticausal(cat, w_ref, K, T):
    out = None
    for k in range(K):
        o = K - 1 - k
        term = w_ref[k:k + 1, :] * cat[o:o + T]
        out = term if out is None else out + term
    return out


def _scconv_fwd(name, scb, scc, scx, w):
    S, C = scb.shape
    K = w.shape[0]
    T, Ct, cur, prev, nxt, wspec = _conv_specs(S, C)

    def body(b_ref, c_ref, x_ref, cp_ref, xp_ref, w_ref, o_ref):
        i = pl.program_id(1)
        q = c_ref[...].astype(F32) * x_ref[...].astype(F32)
        qp = jnp.where(i == 0, 0.0, cp_ref[...].astype(F32) * xp_ref[...].astype(F32))
        cat = jnp.concatenate([qp, q], axis=0)
        o_ref[...] = (b_ref[...].astype(F32) * _causal(cat, w_ref, K, T, HALO)).astype(o_ref.dtype)

    return pl.pallas_call(
        body, name=name, grid=(C // Ct, S // T),
        in_specs=[cur, cur, cur, prev, prev, wspec], out_specs=cur,
        out_shape=jax.ShapeDtypeStruct((S, C), BF), compiler_params=_cp(("parallel", "arbitrary")),
    )(scb, scc, scx, scc, scx, _pad_taps(w))


def _scconv_bwd(name, dv, scb, scc, scx, w):
    S, C = scb.shape
    K = w.shape[0]
    T, Ct, cur, prev, nxt, wspec = _conv_specs(S, C)
    n_t = S // T

    def body(d_ref, b_ref, c_ref, x_ref, dn_ref, bn_ref, cp_ref, xp_ref, w_ref, db_ref, dc_ref, dx_ref, dw_ref):
        i = pl.program_id(1)
        c = c_ref[...].astype(F32)
        x = x_ref[...].astype(F32)
        d = d_ref[...].astype(F32)
        q = c * x
        qp = jnp.where(i == 0, 0.0, cp_ref[...].astype(F32) * xp_ref[...].astype(F32))
        catq = jnp.concatenate([qp, q], axis=0)
        cv = _causal(catq, w_ref, K, T, HALO)
        db_ref[...] = (d * cv).astype(BF)
        dcv = d * b_ref[...].astype(F32)
        dcvn = jnp.where(i == n_t - 1, 0.0, dn_ref[...].astype(F32) * bn_ref[...].astype(F32))
        catd = jnp.concatenate([dcv, dcvn], axis=0)
        dq = _anticausal(catd, w_ref, K, T)
        dc_ref[...] = (dq * x).astype(BF)
        dx_ref[...] = (dq * c).astype(BF)

        @pl.when(i == 0)
        def _():
            dw_ref[...] = jnp.zeros_like(dw_ref)

        for k in range(K):
            o = HALO - (K - 1) + k
            dw_ref[k:k + 1, :] += jnp.sum(dcv * catq[o:o + T], axis=0, keepdims=True)

    return pl.pallas_call(
        body, name=name, grid=(C // Ct, n_t),
        in_specs=[cur, cur, cur, cur, nxt, nxt, prev, prev, wspec],
        out_specs=[cur, cur, cur, wspec],
        out_shape=[jax.ShapeDtypeStruct((S, C), BF)] * 3 + [jax.ShapeDtypeStruct((8, C), F32)],
        compiler_params=_cp(("parallel", "arbitrary")),
    )(dv, scb, scc, scx, dv, scb, scc, scx, _pad_taps(w))


def _mconv_fwd(name, x, w, b):
    S, C = x.shape
    K = w.shape[0]
    T, Ct, cur, prev, nxt, wspec = _conv_specs(S, C)
    bspec = pl.BlockSpec((1, Ct), lambda j, i: (0, j))

    def body(x_ref, xp_ref, w_ref, b_ref, o_ref):
        i = pl.program_id(1)
        xp = jnp.where(i == 0, 0.0, xp_ref[...].astype(F32))
        cat = jnp.concatenate([xp, x_ref[...].astype(F32)], axis=0)
        pre = _causal(cat, w_ref, K, T, HALO) + b_ref[...]
        o_ref[...] = (pre * _sigmoid(pre)).astype(o_ref.dtype)

    return pl.pallas_call(
        body, name=name, grid=(C // Ct, S // T),
        in_specs=[cur, prev, wspec, bspec], out_specs=cur,
        out_shape=jax.ShapeDtypeStruct((S, C), BF), compiler_params=_cp(("parallel", "arbitrary")),
    )(x, x, _pad_taps(w), b.reshape(1, -1).astype(F32))


def _mconv_bwd(name, dout, x, w, b):
    S, C = x.shape
    K = w.shape[0]
    T, Ct, cur, prev, nxt, wspec = _conv_specs(S, C)
    n_t = S // T
    bspec = pl.BlockSpec((1, Ct), lambda j, i: (0, j))

    def body(d_ref, dn_ref, x_ref, xp_ref, xn_ref, w_ref, b_ref, dx_ref, dw_ref, db_ref):
        i = pl.program_id(1)
        xp = jnp.where(i == 0, 0.0, xp_ref[...].astype(F32))
        cat3 = jnp.concatenate([xp, x_ref[...].astype(F32), xn_ref[...].astype(F32)], axis=0)
        pre = _causal(cat3, w_ref, K, T + HALO, HALO) + b_ref[...]
        dn = jnp.where(i == n_t - 1, 0.0, dn_ref[...].astype(F32))
        dext = jnp.concatenate([d_ref[...].astype(F32), dn], axis=0)
        s = _sigmoid(pre)
        dpre = dext * (s * (1.0 + pre * (1.0 - s)))
        dx_ref[...] = _anticausal(dpre, w_ref, K, T).astype(BF)
        dcur = dpre[:T]

        @pl.when(i == 0)
        def _():
            dw_ref[...] = jnp.zeros_like(dw_ref)
            db_ref[...] = jnp.zeros_like(db_ref)

        db_ref[...] += jnp.sum(dcur, axis=0, keepdims=True)
        for k in range(K):
            o = HALO - (K - 1) + k
            dw_ref[k:k + 1, :] += jnp.sum(dcur * cat3[o:o + T], axis=0, keepdims=True)

    return pl.pallas_call(
        body, name=name, grid=(C // Ct, n_t),
        in_specs=[cur, nxt, cur, prev, nxt, wspec, bspec],
        out_specs=[cur, wspec, bspec],
        out_shape=[jax.ShapeDtypeStruct((S, C), BF), jax.ShapeDtypeStruct((8, C), F32), jax.ShapeDtypeStruct((1, C), F32)],
        compiler_params=_cp(("parallel", "arbitrary")),
    )(dout, dout, x, x, x, _pad_taps(w), b.reshape(1, -1).astype(F32))


def _col(v, idx, lane):
    return jnp.sum(jnp.where(lane == idx, v, 0.0), axis=1, keepdims=True)


def _row(v, idx, sub):
    return jnp.sum(jnp.where(sub == idx, v, 0.0), axis=0, keepdims=True)


def _tri_matmul(tri_bf, v):
    hi = v.astype(BF)
    r1 = v - hi.astype(F32)
    mid = r1.astype(BF)
    lo = (r1 - mid.astype(F32)).astype(BF)
    dot = functools.partial(jnp.dot, preferred_element_type=F32)
    return dot(tri_bf, hi) + dot(tri_bf, mid) + dot(tri_bf, lo)


def _dot_nt(a, b):
    return lax.dot_general(a, b, (((1,), (1,)), ((), ())), preferred_element_type=F32)


def _dot_tn(a, b):
    return lax.dot_general(a, b, (((0,), (0,)), ((), ())), preferred_element_type=F32)


def _dot_nn(a, b):
    return jnp.dot(a, b, preferred_element_type=F32)


def _ssd_chunk_scalars(dtr_ref, par_ref, L):
    row_i = lax.broadcasted_iota(jnp.int32, (L, L), 0)
    col_i = lax.broadcasted_iota(jnp.int32, (L, L), 1)
    tri = row_i >= col_i
    pre = dtr_ref[...] + par_ref[0:1, :]
    dt_all = _softplus(pre)
    A_row = -jnp.exp(par_ref[1:2, :])
    a_all = dt_all * A_row
    acum_all = _tri_matmul(tri.astype(BF), a_all)
    return tri, pre, dt_all, A_row, a_all, acum_all, acum_all.T


def _ssd_dims(xbc, heads):
    S, conv_dim = xbc.shape
    inner = heads * SSM_HEADDIM
    N = (conv_dim - inner) // (2 * SSM_GROUPS)
    gw = inner // SSM_GROUPS
    PP = gw // LANES
    L = min(SSM_CHUNK, S)
    assert N == LANES and gw % LANES == 0 and inner % N == 0 and S % L == 0
    return S, inner, N, gw, PP, L, S // L


def _ssd_params(dt_bias, A_log, Dp):
    def padrow(v):
        return jnp.concatenate([v.astype(F32), jnp.zeros((LANES - v.shape[0],), F32)]).reshape(1, LANES)

    return jnp.concatenate([padrow(dt_bias), padrow(A_log), padrow(Dp), jnp.zeros((5, LANES), F32)], axis=0)


def _ssd_fwd(name, xbc, dt_raw, dt_bias, A_log, Dp):
    heads = dt_bias.shape[0]
    S, inner, N, gw, PP, L, nc = _ssd_dims(xbc, heads)
    G = SSM_GROUPS
    boff = inner // N

    def body(x_ref, b_ref, c_ref, dtr_ref, par_ref, y_ref, st_out_ref, st_ref):
        c = pl.program_id(0)
        g = pl.program_id(1)

        @pl.when(c == 0)
        def _():
            st_ref[pl.ds(g * PP, PP)] = jnp.zeros((PP, LANES, N), F32)

        tri, pre, dt_all, A_row, a_all, acum_all, acumT = _ssd_chunk_scalars(dtr_ref, par_ref, L)
        lane = lax.broadcasted_iota(jnp.int32, (L, LANES), 1)
        lane1 = lax.broadcasted_iota(jnp.int32, (1, LANES), 1)
        sub = lax.broadcasted_iota(jnp.int32, (LANES, L), 0)
        subp = lax.broadcasted_iota(jnp.int32, (LANES, 1), 0)
        rowl = lax.broadcasted_iota(jnp.int32, (L, 1), 0)
        lo = lane < SSM_HEADDIM
        lo1 = lane1 < SSM_HEADDIM
        Bb = b_ref[...]
        Cb = c_ref[...]
        Gm = _dot_nt(Cb, Bb)
        for j in range(PP):
            h0 = (g * PP + j) * 2
            h1 = h0 + 1
            x = x_ref[:, j * LANES:(j + 1) * LANES].astype(F32)
            dt_l = jnp.where(lo, _col(dt_all, h0, lane), _col(dt_all, h1, lane))
            ac0 = _col(acum_all, h0, lane)
            ac1 = _col(acum_all, h1, lane)
            ac_l = jnp.where(lo, ac0, ac1)
            E0 = jnp.exp(jnp.where(tri, ac0 - _row(acumT, h0, sub), -1e30))
            E1 = jnp.exp(jnp.where(tri, ac1 - _row(acumT, h1, sub), -1e30))
            xd = x * dt_l
            xdb = xd.astype(BF)
            yd = jnp.where(lo, _dot_nn((Gm * E0).astype(BF), xdb), _dot_nn((Gm * E1).astype(BF), xdb))
            prev = st_ref[g * PP + j]
            st_out_ref[0, j] = prev
            P = _dot_nt(Cb, prev.astype(BF))
            D_l = jnp.where(lo1, _col(par_ref[2:3, :], h0, lane1), _col(par_ref[2:3, :], h1, lane1))
            y_ref[:, j * LANES:(j + 1) * LANES] = (yd + P * jnp.exp(ac_l) + D_l * x).astype(y_ref.dtype)
            al0 = jnp.sum(jnp.where(rowl == L - 1, ac0, 0.0), axis=0, keepdims=True)
            al1 = jnp.sum(jnp.where(rowl == L - 1, ac1, 0.0), axis=0, keepdims=True)
            Wm = xd * jnp.exp(jnp.where(lo, al0, al1) - ac_l)
            eal = jnp.where(subp < SSM_HEADDIM, jnp.exp(al0), jnp.exp(al1))
            st_ref[g * PP + j] = eal * prev + _dot_tn(Wm.astype(BF), Bb)

    xspec = pl.BlockSpec((L, gw), lambda c, g: (c, g))
    return pl.pallas_call(
        body, name=name, grid=(nc, G),
        in_specs=[xspec, pl.BlockSpec((L, N), lambda c, g: (c, boff + g)), pl.BlockSpec((L, N), lambda c, g: (c, boff + G + g)),
                  pl.BlockSpec((L, LANES), lambda c, g: (c, 0)), pl.BlockSpec((8, LANES), lambda c, g: (0, 0))],
        out_specs=[xspec, pl.BlockSpec((1, PP, LANES, N), lambda c, g: (c, g, 0, 0))],
        out_shape=[jax.ShapeDtypeStruct((S, inner), BF), jax.ShapeDtypeStruct((nc, G * PP, LANES, N), F32)],
        scratch_shapes=[pltpu.VMEM((G * PP, LANES, N), F32)],
        compiler_params=_cp(("arbitrary", "arbitrary")),
    )(xbc, xbc, xbc, dt_raw, _ssd_params(dt_bias, A_log, Dp))


def _ssd_bwd(name, dy, xbc, dt_raw, states, dt_bias, A_log, Dp):
    heads = dt_bias.shape[0]
    S, inner, N, gw, PP, L, nc = _ssd_dims(xbc, heads)
    G = SSM_GROUPS
    boff = inner // N

    def body(dy_ref, x_ref, b_ref, c_ref, dtr_ref, par_ref, st_in_ref, dx_ref, dB_ref, dC_ref, ddt_ref, dpar_ref, dst_ref):
        c = pl.program_id(0)
        g = pl.program_id(1)

        @pl.when(c == 0)
        def _():
            dst_ref[pl.ds(g * PP, PP)] = jnp.zeros((PP, LANES, N), F32)

        @pl.when((c == 0) & (g == 0))
        def _():
            dpar_ref[...] = jnp.zeros_like(dpar_ref)

        tri, pre, dt_all, A_row, a_all, acum_all, acumT = _ssd_chunk_scalars(dtr_ref, par_ref, L)
        lane = lax.broadcasted_iota(jnp.int32, (L, LANES), 1)
        lane1 = lax.broadcasted_iota(jnp.int32, (1, LANES), 1)
        sub = lax.broadcasted_iota(jnp.int32, (LANES, L), 0)
        subp = lax.broadcasted_iota(jnp.int32, (LANES, 1), 0)
        rowl = lax.broadcasted_iota(jnp.int32, (L, 1), 0)
        lo = lane < SSM_HEADDIM
        lo1 = lane1 < SSM_HEADDIM
        Bb = b_ref[...]
        Cb = c_ref[...]
        Gm = _dot_nt(Cb, Bb)
        dG = jnp.zeros((L, L), F32)
        dBacc = jnp.zeros((L, N), F32)
        dCacc = jnp.zeros((L, N), F32)
        dac_all = jnp.zeros((L, LANES), F32)
        xds_all = jnp.zeros((L, LANES), F32)
        dD_row = jnp.zeros((1, LANES), F32)

        def rsum(v):
            return jnp.sum(v, axis=1, keepdims=True)

        def total(v):
            return jnp.sum(jnp.sum(v, axis=1, keepdims=True), axis=0, keepdims=True)

        for j in range(PP):
            h0 = (g * PP + j) * 2
            h1 = h0 + 1
            sl = slice(j * LANES, (j + 1) * LANES)
            x = x_ref[:, sl].astype(F32)
            dyv = dy_ref[:, sl].astype(F32)
            dt_l = jnp.where(lo, _col(dt_all, h0, lane), _col(dt_all, h1, lane))
            ac0 = _col(acum_all, h0, lane)
            ac1 = _col(acum_all, h1, lane)
            ac_l = jnp.where(lo, ac0, ac1)
            E0 = jnp.exp(jnp.where(tri, ac0 - _row(acumT, h0, sub), -1e30))
            E1 = jnp.exp(jnp.where(tri, ac1 - _row(acumT, h1, sub), -1e30))
            xd = x * dt_l
            xdb = xd.astype(BF)
            M0 = Gm * E0
            M1 = Gm * E1
            ea_l = jnp.exp(ac_l)
            al0 = jnp.sum(jnp.where(rowl == L - 1, ac0, 0.0), axis=0, keepdims=True)
            al1 = jnp.sum(jnp.where(rowl == L - 1, ac1, 0.0), axis=0, keepdims=True)
            dte_l = jnp.exp(jnp.where(lo, al0, al1) - ac_l)
            Wm = xd * dte_l
            prev = st_in_ref[0, j]
            prevb = prev.astype(BF)
            P = _dot_nt(Cb, prevb)
            D_l = jnp.where(lo1, _col(par_ref[2:3, :], h0, lane1), _col(par_ref[2:3, :], h1, lane1))
            dx = D_l * dyv
            s_l = jnp.sum(dyv * x, axis=0, keepdims=True)
            dD0 = rsum(jnp.where(lo1, s_l, 0.0))
            dD1 = rsum(jnp.where(lo1, 0.0, s_l))
            dyb = dyv.astype(BF)
            dM0 = _dot_nt(jnp.where(lo, dyv, 0.0).astype(BF), xdb)
            dM1 = _dot_nt(jnp.where(lo, 0.0, dyv).astype(BF), xdb)
            dxd = jnp.where(lo, _dot_tn(M0.astype(BF), dyb), _dot_tn(M1.astype(BF), dyb))
            dG = dG + dM0 * E0 + dM1 * E1
            Q0 = dM0 * M0
            Q1 = dM1 * M1
            dac0 = rsum(Q0) - rsum(Q0.T)
            dac1 = rsum(Q1) - rsum(Q1.T)
            dP = dyv * ea_l
            dPb = dP.astype(BF)
            dCacc = dCacc + _dot_nn(dPb, prevb)
            dprev = _dot_tn(dPb, Cb)
            t = dP * P
            dac0 = dac0 + rsum(jnp.where(lo, t, 0.0))
            dac1 = dac1 + rsum(jnp.where(lo, 0.0, t))
            dnew = dst_ref[g * PP + j]
            dnewb = dnew.astype(BF)
            e0 = jnp.exp(al0)
            e1 = jnp.exp(al1)
            dprev = dprev + jnp.where(subp < SSM_HEADDIM, e0, e1) * dnew
            u = dnew * prev
            dal0 = total(jnp.where(subp < SSM_HEADDIM, u, 0.0)) * e0
            dal1 = total(jnp.where(subp < SSM_HEADDIM, 0.0, u)) * e1
            dW = _dot_nt(Bb, dnewb)
            dBacc = dBacc + _dot_nn(Wm.astype(BF), dnewb)
            dxd = dxd + dW * dte_l
            tt = dW * Wm
            t0 = rsum(jnp.where(lo, tt, 0.0))
            t1 = rsum(jnp.where(lo, 0.0, tt))
            dal0 = dal0 + jnp.sum(t0, axis=0, keepdims=True)
            dal1 = dal1 + jnp.sum(t1, axis=0, keepdims=True)
            dac0 = dac0 - t0 + jnp.where(rowl == L - 1, dal0, 0.0)
            dac1 = dac1 - t1 + jnp.where(rowl == L - 1, dal1, 0.0)
            dx = dx + dxd * dt_l
            q = dxd * x
            dst_ref[g * PP + j] = dprev
            dx_ref[:, sl] = dx.astype(dx_ref.dtype)
            dac_all = dac_all + jnp.where(lane == h0, dac0, 0.0) + jnp.where(lane == h1, dac1, 0.0)
            xds_all = (xds_all + jnp.where(lane == h0, rsum(jnp.where(lo, q, 0.0)), 0.0)
                       + jnp.where(lane == h1, rsum(jnp.where(lo, 0.0, q)), 0.0))
            dD_row = dD_row + jnp.where(lane1 == h0, dD0, 0.0) + jnp.where(lane1 == h1, dD1, 0.0)

        dGb = dG.astype(BF)
        dC_ref[...] = (dCacc + _dot_nn(dGb, Bb)).astype(dC_ref.dtype)
        dB_ref[...] = (dBacc + _dot_tn(dGb, Cb)).astype(dB_ref.dtype)
        row_i = lax.broadcasted_iota(jnp.int32, (L, L), 0)
        col_i = lax.broadcasted_iota(jnp.int32, (L, L), 1)
        da_all = _tri_matmul((row_i <= col_i).astype(BF), dac_all)
        mine = (lane >= g * (2 * PP)) & (lane < (g + 1) * (2 * PP))
        ddt_all = da_all * A_row + xds_all
        draw = jnp.where(mine, ddt_all * _sigmoid(pre), 0.0)

        @pl.when(g == 0)
        def _():
            ddt_ref[...] = draw

        @pl.when(g != 0)
        def _():
            ddt_ref[...] += draw

        dbias_row = jnp.sum(draw, axis=0, keepdims=True)
        dalog_row = jnp.sum(jnp.where(mine, da_all * a_all, 0.0), axis=0, keepdims=True)
        dpar_ref[0:1, :] += dbias_row
        dpar_ref[1:2, :] += dalog_row
        dpar_ref[2:3, :] += dD_row

    xspec = pl.BlockSpec((L, gw), lambda c, g: (nc - 1 - c, g))
    nspec = pl.BlockSpec((L, N), lambda c, g: (nc - 1 - c, g))
    return pl.pallas_call(
        body, name=name, grid=(nc, G),
        in_specs=[xspec, xspec, pl.BlockSpec((L, N), lambda c, g: (nc - 1 - c, boff + g)),
                  pl.BlockSpec((L, N), lambda c, g: (nc - 1 - c, boff + G + g)),
                  pl.BlockSpec((L, LANES), lambda c, g: (nc - 1 - c, 0)), pl.BlockSpec((8, LANES), lambda c, g: (0, 0)),
                  pl.BlockSpec((1, PP, LANES, N), lambda c, g: (nc - 1 - c, g, 0, 0))],
        out_specs=[xspec, nspec, nspec, pl.BlockSpec((L, LANES), lambda c, g: (nc - 1 - c, 0)),
                   pl.BlockSpec((8, LANES), lambda c, g: (0, 0))],
        out_shape=[jax.ShapeDtypeStruct((S, inner), BF), jax.ShapeDtypeStruct((S, G * N), BF), jax.ShapeDtypeStruct((S, G * N), BF),
                   jax.ShapeDtypeStruct((S, LANES), F32), jax.ShapeDtypeStruct((8, LANES), F32)],
        scratch_shapes=[pltpu.VMEM((G * PP, LANES, N), F32)],
        compiler_params=_cp(("arbitrary", "arbitrary")),
    )(dy, xbc, xbc, xbc, dt_raw, _ssd_params(dt_bias, A_log, Dp), states)


def _adamw(g, w, m, v):
    m2 = ADAM_B1 * m + (1.0 - ADAM_B1) * g
    v2 = ADAM_B2 * v + (1.0 - ADAM_B2) * (g * g)
    m_hat = m2 / (1.0 - ADAM_B1 ** ADAM_STEP)
    v_hat = v2 / (1.0 - ADAM_B2 ** ADAM_STEP)
    delta = -ADAM_LR * (m_hat / (jnp.sqrt(v_hat) + ADAM_EPS) + ADAM_WD * w)
    return delta, m2, v2


def _flat_tile(R):
    return _pick(R, (FLAT_ROW_TILE, 1024, 512, 256, 128, 64, 32, 16, 8))


def _sum_adam(name, parts, w, m, v):
    R = w.shape[0]
    TR = _flat_tile(R)

    def body(p_ref, w_ref, m_ref, v_ref, g_ref, d_ref, m2_ref, v2_ref):
        g = p_ref[0].astype(F32)
        for k in range(1, N_DEV):
            g = g + p_ref[k].astype(F32)
        g_ref[...] = g
        d_ref[...], m2_ref[...], v2_ref[...] = _adamw(g, w_ref[...], m_ref[...], v_ref[...])

    spec = pl.BlockSpec((TR, LANES), lambda i: (i, 0))
    return pl.pallas_call(
        body, name=name, grid=(R // TR,),
        in_specs=[pl.BlockSpec((N_DEV, TR, LANES), lambda i: (0, i, 0)), spec, spec, spec],
        out_specs=[spec] * 4, out_shape=[jax.ShapeDtypeStruct((R, LANES), F32)] * 4,
        compiler_params=_cp(("parallel",)),
    )(parts, w, m, v)


def _sum8(name, parts):
    R = parts.shape[1]
    TR = _flat_tile(R)

    def body(p_ref, g_ref):
        g = p_ref[0]
        for k in range(1, N_DEV):
            g = g + p_ref[k]
        g_ref[...] = g

    return pl.pallas_call(
        body, name=name, grid=(R // TR,),
        in_specs=[pl.BlockSpec((N_DEV, TR, LANES), lambda i: (0, i, 0))],
        out_specs=pl.BlockSpec((TR, LANES), lambda i: (i, 0)), out_shape=jax.ShapeDtypeStruct((R, LANES), F32),
        compiler_params=_cp(("parallel",)),
    )(parts)


def _adam_flat(name, g, w, m, v):
    R = w.shape[0]
    TR = _flat_tile(R)

    def body(g_ref, w_ref, m_ref, v_ref, d_ref, m2_ref, v2_ref):
        d_ref[...], m2_ref[...], v2_ref[...] = _adamw(g_ref[...], w_ref[...], m_ref[...], v_ref[...])

    spec = pl.BlockSpec((TR, LANES), lambda i: (i, 0))
    return pl.pallas_call(
        body, name=name, grid=(R // TR,), in_specs=[spec] * 4, out_specs=[spec] * 3,
        out_shape=[jax.ShapeDtypeStruct((R, LANES), F32)] * 3, compiler_params=_cp(("parallel",)),
    )(g, w, m, v)


PART_ROWS = 16


def _nrows(shape):
    n = 1
    for s in shape:
        n *= s
    r = -(-n // LANES)
    return -(-r // PART_ROWS) * PART_ROWS


def _as_rows(a):
    n = a.size
    r = _nrows(a.shape)
    f = a.reshape(-1)
    if r * LANES != n:
        f = jnp.concatenate([f, jnp.zeros((r * LANES - n,), a.dtype)])
    return f.reshape(r, LANES)


def _pack(arrs, mult=PART_ROWS):
    cat = jnp.concatenate([_as_rows(a) for a in arrs], axis=0)
    pad = (-cat.shape[0]) % mult
    if pad:
        cat = jnp.concatenate([cat, jnp.zeros((pad, LANES), cat.dtype)], axis=0)
    return cat


def _unpack(flat, shapes):
    lead = flat.shape[:-2]
    out = []
    o = 0
    for shp in shapes:
        n = 1
        for s in shp:
            n *= s
        r = _nrows(shp)
        blk = flat[..., o:o + r, :].reshape(lead + (r * LANES,))
        out.append(blk[..., :n].reshape(lead + tuple(shp)))
        o += r
    return out


def _full_from_shards(st, kind):
    if kind == 'row':
        return st.reshape(st.shape[0] * st.shape[1], st.shape[2])
    return jnp.transpose(st, (1, 0, 2)).reshape(st.shape[1], st.shape[0] * st.shape[2])


def _shards_from_full(full, kind):
    if kind == 'row':
        return full.reshape(N_DEV, full.shape[0] // N_DEV, full.shape[1])
    return jnp.transpose(full.reshape(full.shape[0], N_DEV, full.shape[1] // N_DEV), (1, 0, 2))


def _ffn_fwd(tag, h, g, wgu, wd):
    xn = _rms_fwd(tag + "_rms", h, g)
    ab = _mm(tag + "_up", xn, wgu)
    hmid = _swiglu_fwd(tag + "_act", ab)
    hout = _mm(tag + "_down", hmid, wd, out_dtype=F32, res=h, alpha=0.5)
    return hout, (xn, ab, hmid)


def _ffn_bwd(tag, dh_out, h, g, wgu, wd, saved):
    xn, ab, hmid = saved
    dhmid = _mm(tag + "_d_hmid", dh_out, wd, tb=True, alpha=0.5)
    d_wd = _mm(tag + "_d_wd", hmid, dh_out, ta=True, alpha=0.5)
    dab = _swiglu_bwd(tag + "_d_act", dhmid, ab)
    d_wgu = _mm(tag + "_d_wgu", xn, dab, ta=True)
    dxn = _mm(tag + "_d_xn", dab, wgu, tb=True)
    dh, dg = _rms_bwd(tag + "_d_rms", dxn, h, g, dh_out)
    return dh, dg, d_wgu, d_wd


def _seg_layout(D, inner, conv_dim, H):
    names = ['scb', 'scc', 'scx', 'z', 'xbc', 'dt', 'ga', 'gm']
    widths = [D, D, D, inner, conv_dim, H, D, D]
    offs = {}
    o = 0
    for n, w in zip(names, widths):
        offs[n] = (o, w)
        o += w
    return names, offs


def _mixer_fwd(h, W):
    D = h.shape[1]
    H = W['m_dt_bias'].shape[0]
    inner = W['m_norm'].shape[0]
    conv_dim = W['m_conv_b'].shape[0]
    names, offs = _seg_layout(D, inner, conv_dim, H)
    w_in = W['w_in']
    u = _rms_fwd("mix_rms", h, W['mix_norm'])
    seg = {}
    for n in names:
        o, w = offs[n]
        if n == 'dt':
            wdt = jnp.concatenate([w_in[:, o:o + w], jnp.zeros((D, LANES - w), w_in.dtype)], axis=1)
            seg[n] = _mm("inproj_dt", u, wdt, out_dtype=F32)
        else:
            seg[n] = _mm("inproj_" + n, u, w_in[:, o:o + w])
    v = _scconv_fwd("scconv_f", seg['scb'], seg['scc'], seg['scx'], W['sc_conv_w'])
    ya = _mm("sc_out", v, W['sc_w_out'])
    xbc = _mconv_fwd("mconv_f", seg['xbc'], W['m_conv_w'], W['m_conv_b'])
    y, states = _ssd_fwd("ssd_f", xbc, seg['dt'], W['m_dt_bias'], W['m_A_log'], W['m_D'])
    yn = _gnorm_fwd("gnorm_f", y, seg['z'], W['m_norm'])
    ym = _mm("m_out", yn, W['m_w_out'])
    merged = _merge_fwd("merge_f", seg['ga'], seg['gm'], ya, ym)
    hout = _mm("w_o", merged, W['w_o'], out_dtype=F32, res=h)
    return hout, (u, seg, v, ya, xbc, y, states, yn, ym, merged)


def _mixer_bwd(dh_out, h, W, saved):
    u, seg, v, ya, xbc, y, states, yn, ym, merged = saved
    D = h.shape[1]
    H = W['m_dt_bias'].shape[0]
    inner = W['m_norm'].shape[0]
    conv_dim = W['m_conv_b'].shape[0]
    names, offs = _seg_layout(D, inner, conv_dim, H)
    w_in = W['w_in']
    g = {}
    dmerged = _mm("d_merged", dh_out, W['w_o'], tb=True)
    g['w_o'] = _mm("d_w_o", merged, dh_out, ta=True)
    dga, dgm, dya, dym = _merge_bwd("merge_b", dmerged, seg['ga'], seg['gm'], ya, ym)
    g['sc_w_out'] = _mm("d_sc_w_out", v, dya, ta=True)
    dv = _mm("d_v", dya, W['sc_w_out'], tb=True)
    g['m_w_out'] = _mm("d_m_w_out", yn, dym, ta=True)
    dyn = _mm("d_yn", dym, W['m_w_out'], tb=True)
    dy, dz, d_mnorm = _gnorm_bwd("gnorm_b", dyn, y, seg['z'], W['m_norm'])
    g['m_norm'] = d_mnorm.reshape(-1)
    dxs, dB, dC, ddt, dpar = _ssd_bwd("ssd_b", dy, xbc, seg['dt'], states, W['m_dt_bias'], W['m_A_log'], W['m_D'])
    g['m_dt_bias'] = dpar[0, :H]
    g['m_A_log'] = dpar[1, :H]
    g['m_D'] = dpar[2, :H]
    dxbc_post = jnp.concatenate([dxs, dB, dC], axis=1)
    K = W['m_conv_w'].shape[0]
    dxbc, d_mcw, d_mcb = _mconv_bwd("mconv_b", dxbc_post, seg['xbc'], W['m_conv_w'], W['m_conv_b'])
    g['m_conv_w'] = d_mcw[:K]
    g['m_conv_b'] = d_mcb.reshape(-1)
    dscb, dscc, dscx, d_scw = _scconv_bwd("scconv_b", dv, seg['scb'], seg['scc'], seg['scx'], W['sc_conv_w'])
    g['sc_conv_w'] = d_scw[:W['sc_conv_w'].shape[0]]
    o_dt, _ = offs['dt']
    dmain = jnp.concatenate([dscb, dscc, dscx, dz, dxbc, dga, dgm], axis=1)
    wmain = jnp.concatenate([w_in[:, :o_dt], w_in[:, o_dt + H:]], axis=1)
    wdt = jnp.concatenate([w_in[:, o_dt:o_dt + H], jnp.zeros((D, LANES - H), w_in.dtype)], axis=1)
    du = _mm("d_u_main", dmain, wmain, tb=True, out_dtype=F32)
    du = _mm("d_u_dt", ddt, wdt, tb=True, out_dtype=F32, res=du)
    d_wmain = _mm("d_w_in_main", u, dmain, ta=True)
    d_wdt = _mm("d_w_in_dt", u, ddt, ta=True)
    g['w_in'] = jnp.concatenate([d_wmain[:, :o_dt], d_wdt[:, :H], d_wmain[:, o_dt:]], axis=1)
    dh, dg = _rms_bwd("mix_d_rms", du, h, W['mix_norm'], dh_out)
    g['mix_norm'] = dg.reshape(-1)
    return dh, g


def _ple_layer_fwd(h, p_l, W):
    xn = _rms_fwd("ple_rms", h, W['ple_norm'])
    gpre = _mm("ple_gate", xn, W['ple_w_gate'])
    pp = _mm("ple_proj", p_l, W['ple_w_proj'])
    hout = _ple_fwd("ple_f", h, gpre, pp)
    return hout, (xn, gpre, pp)


def _ple_layer_bwd(dh_out, h, p_l, W, saved):
    xn, gpre, pp = saved
    g = {}
    dgpre, dpp = _ple_bwd("ple_b", dh_out, gpre, pp)
    g['ple_w_proj'] = _mm("d_ple_proj", p_l, dpp, ta=True)
    g['ple_w_gate'] = _mm("d_ple_gate", xn, dgpre, ta=True)
    dxn = _mm("d_ple_xn", dgpre, W['ple_w_gate'], tb=True)
    dh, dg = _rms_bwd("ple_d_rms", dxn, h, W['ple_norm'], dh_out)
    g['ple_norm'] = dg.reshape(-1)
    return dh, g


def kernel(x, p, ffn1_norm, ffn1_wg, ffn1_wu, ffn1_wd, mix_norm, w_in, sc_conv_w, sc_w_out, m_conv_w, m_conv_b, m_dt_bias, m_A_log, m_D, m_norm, m_w_out, w_o, ffn2_norm, ffn2_wg, ffn2_wu, ffn2_wd, ple_norm, ple_w_gate, ple_w_proj, final_norm, loss_target, m_ffn1_norm, m_ffn1_wg, m_ffn1_wu, m_ffn1_wd, m_mix_norm, m_w_in, m_sc_conv_w, m_sc_w_out, m_m_conv_w, m_m_conv_b, m_m_dt_bias, m_m_A_log, m_m_D, m_m_norm, m_m_w_out, m_w_o, m_ffn2_norm, m_ffn2_wg, m_ffn2_wu, m_ffn2_wd, m_ple_norm, m_ple_w_gate, m_ple_w_proj, m_final_norm, v_ffn1_norm, v_ffn1_wg, v_ffn1_wu, v_ffn1_wd, v_mix_norm, v_w_in, v_sc_conv_w, v_sc_w_out, v_m_conv_w, v_m_conv_b, v_m_dt_bias, v_m_A_log, v_m_D, v_m_norm, v_m_w_out, v_w_o, v_ffn2_norm, v_ffn2_wg, v_ffn2_wu, v_ffn2_wd, v_ple_norm, v_ple_w_gate, v_ple_w_proj, v_final_norm):
    args = (x, p, ffn1_norm, ffn1_wg, ffn1_wu, ffn1_wd, mix_norm, w_in, sc_conv_w, sc_w_out, m_conv_w, m_conv_b, m_dt_bias, m_A_log, m_D, m_norm, m_w_out, w_o, ffn2_norm, ffn2_wg, ffn2_wu, ffn2_wd, ple_norm, ple_w_gate, ple_w_proj, final_norm, loss_target, m_ffn1_norm, m_ffn1_wg, m_ffn1_wu, m_ffn1_wd, m_mix_norm, m_w_in, m_sc_conv_w, m_sc_w_out, m_m_conv_w, m_m_conv_b, m_m_dt_bias, m_m_A_log, m_m_D, m_m_norm, m_m_w_out, m_w_o, m_ffn2_norm, m_ffn2_wg, m_ffn2_wu, m_ffn2_wd, m_ple_norm, m_ple_w_gate, m_ple_w_proj, m_final_norm, v_ffn1_norm, v_ffn1_wg, v_ffn1_wu, v_ffn1_wd, v_mix_norm, v_w_in, v_sc_conv_w, v_sc_w_out, v_m_conv_w, v_m_conv_b, v_m_dt_bias, v_m_A_log, v_m_D, v_m_norm, v_m_w_out, v_w_o, v_ffn2_norm, v_ffn2_wg, v_ffn2_wu, v_ffn2_wd, v_ple_norm, v_ple_w_gate, v_ple_w_proj, v_final_norm)
    names = ARG_NAMES + ['m_' + n for n in WEIGHTS] + ['v_' + n for n in WEIGHTS]
    A = dict(zip(names, args))
    depth = ffn1_norm.shape[0]
    me = 4 * lax.axis_index("x") + 2 * lax.axis_index("y") + lax.axis_index("c")

    big_shapes = [A[n].shape[1:] for n, _ in BIG]
    conv_shapes = [A[n].shape[1:] for n in CONVW]

    def pack_big(prefix):
        return jnp.concatenate([_pack([A[prefix + n][l] for n, _ in BIG], FLAT_ROW_TILE) for l in range(depth)], axis=0)

    w_flat = pack_big('')
    rp = w_flat.shape[0] // depth
    gathered = _exchange("gather_weights", w_flat.astype(BF), True).reshape(N_DEV, depth, rp, LANES)
    conv_flat = _pack([A[n][l] for l in range(depth) for n in CONVW])
    conv_g = _unpack(_exchange("gather_conv_taps", conv_flat, True), conv_shapes * depth)

    layers = []
    for l in range(depth):
        W = {}
        for (n, kind), st in zip(BIG, _unpack(gathered[:, l], big_shapes)):
            W[n] = _full_from_shards(st, kind)
        for i, n in enumerate(CONVW):
            W[n] = _full_from_shards(conv_g[l * len(CONVW) + i], 'col')
        for n in SMALL:
            W[n] = A[n][l]
        W['ffn1_wgu'] = jnp.concatenate([W['ffn1_wg'], W['ffn1_wu']], axis=1)
        W['ffn2_wgu'] = jnp.concatenate([W['ffn2_wg'], W['ffn2_wu']], axis=1)
        layers.append(W)

    h = x[0]
    saved = []
    for l in range(depth):
        W = layers[l]
        p_l = p[l, 0]
        h1, s1 = _ffn_fwd("ffn1", h, W['ffn1_norm'], W['ffn1_wgu'], W['ffn1_wd'])
        h2, s2 = _mixer_fwd(h1, W)
        h3, s3 = _ffn_fwd("ffn2", h2, W['ffn2_norm'], W['ffn2_wgu'], W['ffn2_wd'])
        h4, s4 = _ple_layer_fwd(h3, p_l, W)
        saved.append((h, h1, h2, h3, s1, s2, s3, s4))
        h = h4

    dh, loss_row, d_final = _loss_head("loss_head", h, final_norm, loss_target[0])
    loss = lax.psum(loss_row[0, 0], ("x", "y", "c"))

    grads = [None] * depth
    for l in reversed(range(depth)):
        W = layers[l]
        h0, h1, h2, h3, s1, s2, s3, s4 = saved[l]
        g = {}
        dh, g4 = _ple_layer_bwd(dh, h3, p[l, 0], W, s4)
        g.update(g4)
        dh, dg, d_wgu, d_wd = _ffn_bwd("ffn2", dh, h2, W['ffn2_norm'], W['ffn2_wgu'], W['ffn2_wd'], s3)
        ff = d_wgu.shape[1] // 2
        g.update(ffn2_norm=dg.reshape(-1), ffn2_wg=d_wgu[:, :ff], ffn2_wu=d_wgu[:, ff:], ffn2_wd=d_wd)
        dh, g2 = _mixer_bwd(dh, h1, W, s2)
        g.update(g2)
        dh, dg, d_wgu, d_wd = _ffn_bwd("ffn1", dh, h0, W['ffn1_norm'], W['ffn1_wgu'], W['ffn1_wd'], s1)
        g.update(ffn1_norm=dg.reshape(-1), ffn1_wg=d_wgu[:, :ff], ffn1_wu=d_wgu[:, ff:], ffn1_wd=d_wd)
        grads[l] = g
    grad_x = dh[None]

    def grad_slab(l):
        parts = []
        for n, kind in BIG:
            sh = _shards_from_full(grads[l][n].astype(BF), kind).reshape(N_DEV, -1, LANES)
            pad = _nrows(A[n].shape[1:]) - sh.shape[1]
            if pad:
                sh = jnp.concatenate([sh, jnp.zeros((N_DEV, pad, LANES), BF)], axis=1)
            parts.append(sh)
        cat = jnp.concatenate(parts, axis=1)
        pad = rp - cat.shape[1]
        if pad:
            cat = jnp.concatenate([cat, jnp.zeros((N_DEV, pad, LANES), BF)], axis=1)
        return cat

    g_send = jnp.concatenate([grad_slab(l) for l in range(depth)], axis=1)
    g_parts = _exchange("scatter_grads", g_send, False)
    big_out = _sum_adam("sum_adamw", g_parts, w_flat, pack_big('m_'), pack_big('v_'))
    big_res = []
    for flat in big_out:
        per_layer = [_unpack(flat.reshape(depth, rp, LANES)[l], big_shapes) for l in range(depth)]
        big_res.append({n: jnp.stack([per_layer[l][i] for l in range(depth)]) for i, (n, _) in enumerate(BIG)})

    small_names = SMALL + CONVW
    small_parts = [grads[l][n] for l in range(depth) for n in small_names] + [d_final.reshape(-1)]
    small_shapes = [a.shape for a in small_parts]
    small_sum = _sum8("sum_small", _exchange("gather_small_grads", _pack(small_parts), True))
    small_g = _unpack(small_sum, small_shapes)
    sg = {}
    for i, n in enumerate(small_names):
        full = jnp.stack([small_g[l * len(small_names) + i] for l in range(depth)])
        if n in CONVW:
            c = A[n].shape[-1]
            full = lax.dynamic_slice_in_dim(full, me * c, c, axis=2)
        sg[n] = full
    sg['final_norm'] = small_g[-1]
    s_order = small_names + ['final_norm']
    s_shapes = [sg[n].shape for n in s_order]
    s_out = _adam_flat("adamw_small", _pack([sg[n] for n in s_order]), _pack([A[n] for n in s_order]),
                       _pack([A['m_' + n] for n in s_order]), _pack([A['v_' + n] for n in s_order]))
    small_res = [sg] + [dict(zip(s_order, _unpack(flat, s_shapes))) for flat in s_out]

    outs = [loss, grad_x]
    for k in range(4):
        for n in WEIGHTS:
            outs.append(big_res[k][n] if n in big_res[k] else small_res[k][n])
    return tuple(outs)
```

```python
import functools

import jax
import jax.numpy as jnp
from jax import lax
from jax.experimental import pallas as pl
from jax.experimental.pallas import tpu as pltpu

BF = jnp.bfloat16
F32 = jnp.float32

EPS = 1e-6
N_DEV = 8
LANES = 128
SSM_GROUPS = 4
SSM_HEADDIM = 64
SSM_CHUNK = 128
HALO = 16
VMEM_LIMIT = 56 * 1024 * 1024
FLAT_ROW_TILE = 2048

ADAM_LR = 0.001
ADAM_B1 = 0.9
ADAM_B2 = 0.999
ADAM_EPS = 1e-08
ADAM_WD = 0.01
ADAM_STEP = 10

MESH = pl.DeviceIdType.MESH

ARG_NAMES = ['x', 'p', 'ffn1_norm', 'ffn1_wg', 'ffn1_wu', 'ffn1_wd', 'mix_norm', 'w_in', 'sc_conv_w', 'sc_w_out', 'm_conv_w', 'm_conv_b', 'm_dt_bias', 'm_A_log', 'm_D', 'm_norm', 'm_w_out', 'w_o', 'ffn2_norm', 'ffn2_wg', 'ffn2_wu', 'ffn2_wd', 'ple_norm', 'ple_w_gate', 'ple_w_proj', 'final_norm', 'loss_target']
WEIGHTS = ARG_NAMES[2:26]
BIG = [('ffn1_wg', 'col'), ('ffn1_wu', 'col'), ('ffn1_wd', 'row'), ('w_in', 'col'), ('sc_w_out', 'row'),
       ('m_w_out', 'row'), ('w_o', 'row'), ('ffn2_wg', 'col'), ('ffn2_wu', 'col'), ('ffn2_wd', 'row'),
       ('ple_w_gate', 'row'), ('ple_w_proj', 'col')]
CONVW = ['sc_conv_w', 'm_conv_w']
SMALL = ['ffn1_norm', 'mix_norm', 'm_conv_b', 'm_dt_bias', 'm_A_log', 'm_D', 'm_norm', 'ffn2_norm', 'ple_norm']


def _pick(n, cands):
    for c in cands:
        if n % c == 0:
            return c
    return n


def _cp(sem):
    return pltpu.CompilerParams(dimension_semantics=sem, vmem_limit_bytes=VMEM_LIMIT)


def _sigmoid(x):
    return 1.0 / (1.0 + jnp.exp(-x))


def _softplus(x):
    return jnp.maximum(x, 0.0) + jnp.log(1.0 + jnp.exp(-jnp.abs(x)))


def _exchange(name, x, gather):
    slab = x.shape if gather else x.shape[1:]

    def body(x_ref, o_ref, send_sems, recv_sems, local_sem):
        mx, my, mc = lax.axis_index("x"), lax.axis_index("y"), lax.axis_index("c")
        me = 4 * mx + 2 * my + mc

        def src_for(k):
            return x_ref if gather else x_ref.at[k]

        local = pltpu.make_async_copy(src_for(me), o_ref.at[me], local_sem)
        local.start()
        sends = []
        peers = []
        for r in range(1, N_DEV):
            px = (mx + ((r >> 2) & 1)) % 2
            py = (my + ((r >> 1) & 1)) % 2
            pc = (mc + (r & 1)) % 2
            peer = 4 * px + 2 * py + pc
            peers.append(peer)
            cp = pltpu.make_async_remote_copy(
                src_ref=src_for(peer), dst_ref=o_ref.at[me], send_sem=send_sems.at[r - 1], recv_sem=recv_sems.at[r - 1],
                device_id=(px, py, pc), device_id_type=MESH)
            cp.start()
            sends.append(cp)
        for r in range(1, N_DEV):
            peer = peers[r - 1]
            pltpu.make_async_remote_copy(
                src_ref=src_for(peer), dst_ref=o_ref.at[peer], send_sem=send_sems.at[r - 1], recv_sem=recv_sems.at[r - 1],
                device_id=(mx, my, mc), device_id_type=MESH).wait_recv()
        for cp in sends:
            cp.wait_send()
        local.wait()

    return pl.pallas_call(
        body, name=name,
        out_shape=jax.ShapeDtypeStruct((N_DEV,) + tuple(slab), x.dtype),
        in_specs=[pl.BlockSpec(memory_space=pltpu.HBM)],
        out_specs=pl.BlockSpec(memory_space=pltpu.HBM),
        scratch_shapes=[pltpu.SemaphoreType.DMA((N_DEV - 1,)), pltpu.SemaphoreType.DMA((N_DEV - 1,)), pltpu.SemaphoreType.DMA],
    )(x)


GROUPS = [('wd1', ['ffn1_wd']), ('wd2', ['ffn2_wd']), ('row', ['sc_w_out', 'w_o', 'ple_w_gate', 'm_w_out']),
          ('col', ['ffn1_wg', 'ffn1_wu', 'ffn2_wg', 'ffn2_wu']), ('win', ['w_in']), ('pp', ['ple_w_proj'])]
_HBM = pl.BlockSpec(memory_space=pltpu.HBM)
_SEM = pl.BlockSpec(memory_space=pltpu.SEMAPHORE)
_ANY = pl.BlockSpec(memory_space=pl.ANY)
_EFFECT = pltpu.SideEffectType.DATAFLOW_SIDE_EFFECTING


def _peer_list():
    mx, my, mc = lax.axis_index("x"), lax.axis_index("y"), lax.axis_index("c")
    out = []
    for r in range(1, N_DEV):
        px = (mx + ((r >> 2) & 1)) % 2
        py = (my + ((r >> 1) & 1)) % 2
        pc = (mc + (r & 1)) % 2
        out.append((px, py, pc, 4 * px + 2 * py + pc))
    return 4 * mx + 2 * my + mc, out


def _xchg_copy(src_refs, land_refs, send_sems, recv_sems, layer, i, r, peer, dst_slab):
    px, py, pc, pidx = peer
    n = len(src_refs)
    src = src_refs[i].at[layer] if layer is not None else src_refs[i].at[pidx]
    return pltpu.make_async_remote_copy(
        src_ref=src, dst_ref=land_refs[i].at[dst_slab], send_sem=send_sems.at[r * n + i], recv_sem=recv_sems.at[r * n + i],
        device_id=(px, py, pc), device_id_type=MESH)


def _xchg_begin(name, srcs, layer, dep):
    n = len(srcs)
    slabs = [tuple(s.shape[1:]) for s in srcs]
    ncp = n * (N_DEV - 1)

    def body(*refs):
        src_refs, land_refs = refs[:n], refs[n:2 * n]
        send_sems, recv_sems = refs[2 * n + 1], refs[2 * n + 2]
        token = refs[-1]
        me, peers = _peer_list()
        for r, peer in enumerate(peers):
            for i in range(n):
                _xchg_copy(src_refs, land_refs, send_sems, recv_sems, layer, i, r, peer, me).start()
        token[...] = jnp.zeros_like(token)

    lands = [pltpu.with_memory_space_constraint(lax.empty((N_DEV,) + sl, s.dtype), pltpu.HBM) for sl, s in zip(slabs, srcs)]
    out = pl.pallas_call(
        body, name=name,
        out_shape=(pltpu.SemaphoreType.DMA((ncp,)), pltpu.SemaphoreType.DMA((ncp,)),
                   *[pltpu.HBM((N_DEV,) + sl, s.dtype) for sl, s in zip(slabs, srcs)], jax.ShapeDtypeStruct((8, LANES), F32)),
        in_specs=[_HBM] * (2 * n) + [_ANY],
        out_specs=(_SEM, _SEM, *[_HBM] * n, pl.BlockSpec(memory_space=pltpu.VMEM)),
        input_output_aliases={n + i: 2 + i for i in range(n)},
        compiler_params=pltpu.CompilerParams(has_side_effects=_EFFECT),
    )(*[pltpu.with_memory_space_constraint(s, pltpu.HBM) for s in srcs], *lands, dep)
    return (out[0], out[1]), list(out[2:2 + n]), out[-1]


def _xchg_end(name, srcs, lands, sems, layer, after):
    n = len(srcs)

    def body(*refs):
        src_refs, land_refs = refs[:n], refs[n:2 * n]
        send_sems, recv_sems = refs[2 * n], refs[2 * n + 1]
        me, peers = _peer_list()
        for r, peer in enumerate(peers):
            for i in range(n):
                cp = _xchg_copy(src_refs, land_refs, send_sems, recv_sems, layer, i, r, peer, peer[3])
                cp.wait_send()
                cp.wait_recv()

    out = pl.pallas_call(
        body, name=name,
        out_shape=tuple(pltpu.HBM(l.shape, l.dtype) for l in lands),
        in_specs=[_HBM] * (2 * n) + [_SEM, _SEM, _ANY], out_specs=tuple([_HBM] * n),
        input_output_aliases={n + i: i for i in range(n)},
        compiler_params=pltpu.CompilerParams(has_side_effects=_EFFECT),
    )(*[pltpu.with_memory_space_constraint(s, pltpu.HBM) for s in srcs], *lands, sems[0], sems[1], after)
    return list(out)


def _mm(name, a, b, *, ta=False, tb=False, out_dtype=None, res=None, alpha=1.0):
    out_dtype = out_dtype or BF
    M, K = (a.shape[1], a.shape[0]) if ta else a.shape
    N = b.shape[0] if tb else b.shape[1]
    assert (b.shape[1] if tb else b.shape[0]) == K, (name, a.shape, b.shape)
    tm = _pick(M, (1024, 512, 256, 128))
    tn = _pick(N, (1024, 512, 256, 128))
    tk = _pick(K, (512, 256, 128))
    nk = K // tk
    a_spec = pl.BlockSpec((tk, tm), lambda i, j, k: (k, i)) if ta else pl.BlockSpec((tm, tk), lambda i, j, k: (i, k))
    b_spec = pl.BlockSpec((tn, tk), lambda i, j, k: (j, k)) if tb else pl.BlockSpec((tk, tn), lambda i, j, k: (k, j))
    dn = (((0 if ta else 1,), (1 if tb else 0,)), ((), ()))
    has_res = res is not None

    def body(*refs):
        if has_res:
            a_ref, b_ref, r_ref, o_ref, acc = refs
        else:
            a_ref, b_ref, o_ref, acc = refs
        k = pl.program_id(2)

        @pl.when(k == 0)
        def _():
            acc[...] = jnp.zeros_like(acc)

        acc[...] += lax.dot_general(a_ref[...].astype(BF), b_ref[...].astype(BF), dn, preferred_element_type=F32)

        @pl.when(k == nk - 1)
        def _():
            v = acc[...] * alpha if alpha != 1.0 else acc[...]
            if has_res:
                v = r_ref[...] + v
            o_ref[...] = v.astype(o_ref.dtype)

    in_specs = [a_spec, b_spec]
    args = [a, b]
    if has_res:
        in_specs.append(pl.BlockSpec((tm, tn), lambda i, j, k: (i, j)))
        args.append(res)
    return pl.pallas_call(
        body, name=name, grid=(M // tm, N // tn, nk),
        in_specs=in_specs, out_specs=pl.BlockSpec((tm, tn), lambda i, j, k: (i, j)),
        out_shape=jax.ShapeDtypeStruct((M, N), out_dtype),
        scratch_shapes=[pltpu.VMEM((tm, tn), F32)],
        compiler_params=_cp(("parallel", "parallel", "arbitrary")),
    )(*args)


def _ew(name, fn, tiled, params, outs, accs=(), tile=256, dep=None):
    tiled = [t if isinstance(t, tuple) else (t, t.shape[1], 0) for t in tiled]
    params = [q if isinstance(q, tuple) else (q, None) for q in params]
    S = tiled[0][0].shape[0]
    T = _pick(S, (tile, 128, 64, 32, 16))
    n_in = len(tiled) + len(params)
    n_dep = 0 if dep is None else 1

    def body(*refs):
        fn(pl.program_id(0) == 0, *refs[:n_in], *refs[n_in + n_dep:])

    in_specs = [pl.BlockSpec((T, w), lambda i, cb=cb: (i, cb)) for _, w, cb in tiled]
    for q, row in params:
        if row is None:
            in_specs.append(pl.BlockSpec(q.shape, lambda i: (0, 0)))
        else:
            in_specs.append(pl.BlockSpec((None, 1, q.shape[2]), lambda i, row=row: (row, 0, 0)))
    args = [t[0] for t in tiled] + [q[0] for q in params]
    if dep is not None:
        in_specs.append(pl.BlockSpec(memory_space=pl.ANY))
        args.append(dep)
    out_specs = [pl.BlockSpec((T, w), lambda i: (i, 0)) for w, _ in outs]
    out_specs += [pl.BlockSpec(shp, lambda i: (0, 0)) for shp, _ in accs]
    out_shape = [jax.ShapeDtypeStruct((S, w), dt) for w, dt in outs]
    out_shape += [jax.ShapeDtypeStruct(shp, dt) for shp, dt in accs]
    res = pl.pallas_call(
        body, name=name, grid=(S // T,), in_specs=in_specs, out_specs=out_specs, out_shape=out_shape,
        compiler_params=_cp(("arbitrary",)),
    )(*args)
    return res


def _prow(g):
    return g if isinstance(g, tuple) else g.reshape(1, -1)


def _rms_fwd(name, h, g, dep=None):
    def fn(first, h_ref, g_ref, o_ref):
        x = h_ref[...]
        r = lax.rsqrt(jnp.mean(x * x, axis=-1, keepdims=True) + EPS)
        o_ref[...] = (x * r * g_ref[...]).astype(o_ref.dtype)

    return _ew(name, fn, [h], [_prow(g)], [(h.shape[1], BF)], dep=dep)[0]


def _rms_bwd(name, dxn, h, g, res):
    D = h.shape[1]

    def fn(first, d_ref, h_ref, r_ref, g_ref, o_ref, dg_ref):
        x = h_ref[...]
        d = d_ref[...].astype(F32)
        r = lax.rsqrt(jnp.mean(x * x, axis=-1, keepdims=True) + EPS)
        xhat = x * r
        dxhat = d * g_ref[...]
        dh = r * (dxhat - xhat * jnp.mean(dxhat * xhat, axis=-1, keepdims=True))
        o_ref[...] = r_ref[...] + dh

        @pl.when(first)
        def _():
            dg_ref[...] = jnp.zeros_like(dg_ref)

        dg_ref[...] += jnp.sum(d * xhat, axis=0, keepdims=True)

    return _ew(name, fn, [dxn, h, res], [_prow(g)], [(D, F32)], [((1, D), F32)])


def _swiglu_fwd(name, ab):
    FF = ab.shape[1] // 2

    def fn(first, ab_ref, o_ref):
        a = ab_ref[:, :FF].astype(F32)
        b = ab_ref[:, FF:].astype(F32)
        o_ref[...] = (a * _sigmoid(a) * b).astype(o_ref.dtype)

    return _ew(name, fn, [ab], [], [(FF, BF)])[0]


def _swiglu_bwd(name, dhmid, ab):
    FF = ab.shape[1] // 2

    def fn(first, d_ref, ab_ref, o_ref):
        a = ab_ref[:, :FF].astype(F32)
        b = ab_ref[:, FF:].astype(F32)
        d = d_ref[...].astype(F32)
        s = _sigmoid(a)
        o_ref[:, :FF] = (d * b * (s * (1.0 + a * (1.0 - s)))).astype(o_ref.dtype)
        o_ref[:, FF:] = (d * a * s).astype(o_ref.dtype)

    return _ew(name, fn, [dhmid, ab], [], [(2 * FF, BF)])[0]


def _merge_fwd(name, ga, gm, ya, ym):
    def fn(first, ga_ref, gm_ref, ya_ref, ym_ref, o_ref):
        o = _sigmoid(ga_ref[...].astype(F32)) * ya_ref[...].astype(F32) + _sigmoid(gm_ref[...].astype(F32)) * ym_ref[...].astype(F32)
        o_ref[...] = o.astype(o_ref.dtype)

    return _ew(name, fn, [ga, gm, ya, ym], [], [(ya.shape[1], BF)])[0]


def _merge_bwd(name, dmerged, ga, gm, ya, ym):
    W = ya.shape[1]

    def fn(first, d_ref, ga_ref, gm_ref, ya_ref, ym_ref, dga_ref, dgm_ref, dya_ref, dym_ref):
        d = d_ref[...].astype(F32)
        sa = _sigmoid(ga_ref[...].astype(F32))
        sm = _sigmoid(gm_ref[...].astype(F32))
        dga_ref[...] = (d * ya_ref[...].astype(F32) * sa * (1.0 - sa)).astype(BF)
        dgm_ref[...] = (d * ym_ref[...].astype(F32) * sm * (1.0 - sm)).astype(BF)
        dya_ref[...] = (d * sa).astype(BF)
        dym_ref[...] = (d * sm).astype(BF)

    return _ew(name, fn, [dmerged, ga, gm, ya, ym], [], [(W, BF)] * 4)


def _gnorm_fwd(name, y, z, w):
    W = y.shape[1]
    gw = W // SSM_GROUPS

    def fn(first, y_ref, z_ref, w_ref, o_ref):
        for g in range(SSM_GROUPS):
            sl = slice(g * gw, (g + 1) * gw)
            zz = z_ref[:, sl].astype(F32)
            t = y_ref[:, sl].astype(F32) * (zz * _sigmoid(zz))
            r = lax.rsqrt(jnp.mean(t * t, axis=-1, keepdims=True) + EPS)
            o_ref[:, sl] = (t * r * w_ref[:, sl]).astype(o_ref.dtype)

    return _ew(name, fn, [y, z], [_prow(w)], [(W, BF)])[0]


def _gnorm_bwd(name, dyn, y, z, w):
    W = y.shape[1]
    gw = W // SSM_GROUPS

    def fn(first, d_ref, y_ref, z_ref, w_ref, dy_ref, dz_ref, dw_ref):
        @pl.when(first)
        def _():
            dw_ref[...] = jnp.zeros_like(dw_ref)

        for g in range(SSM_GROUPS):
            sl = slice(g * gw, (g + 1) * gw)
            zz = z_ref[:, sl].astype(F32)
            yy = y_ref[:, sl].astype(F32)
            d = d_ref[:, sl].astype(F32)
            s = _sigmoid(zz)
            sz = zz * s
            t = yy * sz
            r = lax.rsqrt(jnp.mean(t * t, axis=-1, keepdims=True) + EPS)
            that = t * r
            dthat = d * w_ref[:, sl]
            dt = r * (dthat - that * jnp.mean(dthat * that, axis=-1, keepdims=True))
            dw_ref[:, sl] += jnp.sum(d * that, axis=0, keepdims=True)
            dy_ref[:, sl] = (dt * sz).astype(BF)
            dz_ref[:, sl] = (dt * yy * (s * (1.0 + zz * (1.0 - s)))).astype(BF)

    return _ew(name, fn, [dyn, y, z], [_prow(w)], [(W, BF), (W, BF)], [((1, W), F32)])


def _ple_fwd(name, h, gpre, pp):
    def fn(first, h_ref, g_ref, p_ref, o_ref):
        o_ref[...] = h_ref[...] + _sigmoid(g_ref[...].astype(F32)) * p_ref[...].astype(F32)

    return _ew(name, fn, [h, gpre, pp], [], [(h.shape[1], F32)])[0]


def _ple_bwd(name, dh, gpre, pp, dep=None):
    W = dh.shape[1]

    def fn(first, d_ref, g_ref, p_ref, dg_ref, dp_ref):
        d = d_ref[...]
        s = _sigmoid(g_ref[...].astype(F32))
        dg_ref[...] = (d * p_ref[...].astype(F32) * s * (1.0 - s)).astype(BF)
        dp_ref[...] = (d * s).astype(BF)

    return _ew(name, fn, [dh, gpre, pp], [], [(W, BF), (W, BF)], dep=dep)


def _loss_head(name, h, g, target):
    D = h.shape[1]

    def fn(first, h_ref, t_ref, g_ref, dh_ref, loss_ref, dg_ref):
        x = h_ref[...]
        r = lax.rsqrt(jnp.mean(x * x, axis=-1, keepdims=True) + EPS)
        xhat = x * r
        err = xhat * g_ref[...] - t_ref[...]
        part = 0.5 * jnp.sum(jnp.mean(err * err, axis=-1, keepdims=True), axis=0, keepdims=True)
        dy = err * (1.0 / D)
        dxhat = dy * g_ref[...]
        dh_ref[...] = r * (dxhat - xhat * jnp.mean(dxhat * xhat, axis=-1, keepdims=True))

        @pl.when(first)
        def _():
            loss_ref[...] = jnp.zeros_like(loss_ref)
            dg_ref[...] = jnp.zeros_like(dg_ref)

        loss_ref[...] += jnp.broadcast_to(part, loss_ref.shape)
        dg_ref[...] += jnp.sum(dy * xhat, axis=0, keepdims=True)

    return _ew(name, fn, [h, target], [_prow(g)], [(D, F32)], [((1, LANES), F32), ((1, D), F32)])


def _conv_specs(S, C, offs, l):
    T = _pick(S, (512, 256, 128, 64, 32, 16))
    Ct = [c for c in (512, 256, 128) if C % c == 0 and all(o % c == 0 for o in offs)][0]
    per = T // HALO
    last = S // HALO - 1

    def cur(off=0):
        return pl.BlockSpec((T, Ct), lambda j, i: (i, off // Ct + j))

    def prev(off=0):
        return pl.BlockSpec((HALO, Ct), lambda j, i: (jnp.maximum(i * per - 1, 0), off // Ct + j))

    def nxt(off=0):
        return pl.BlockSpec((HALO, Ct), lambda j, i: (jnp.minimum((i + 1) * per, last), off // Ct + j))

    wspec = pl.BlockSpec((None, 8, Ct), lambda j, i: (l, 0, j))
    return T, Ct, cur, prev, nxt, wspec


def _pad_taps(w):
    return jnp.concatenate([w.astype(F32), jnp.zeros((w.shape[0], 8 - w.shape[1], w.shape[2]), F32)], axis=1)


def _causal(cat, w_ref, K, T, lead):
    out = None
    for k in range(K):
        o = lead - (K - 1) + k
        term = w_ref[k:k + 1, :] * cat[o:o + T]
        out = term if out is None else out + term
    return out


def _anticausal(cat, w_ref, K, T):
    out = None
    for k in range(K):
        o = K - 1 - k
        term = w_ref[k:k + 1, :] * cat[o:o + T]
        out = term if out is None else out + term
    return out


def _scconv_fwd(name, proj, ob, oc, ox, taps, K, l):
    S = proj.shape[0]
    C = taps.shape[2]
    T, Ct, cur, prev, nxt, wspec = _conv_specs(S, C, (ob, oc, ox), l)

    def body(b_ref, c_ref, x_ref, cp_ref, xp_ref, w_ref, o_ref):
        i = pl.program_id(1)
        q = c_ref[...].astype(F32) * x_ref[...].astype(F32)
        qp = jnp.where(i == 0, 0.0, cp_ref[...].astype(F32) * xp_ref[...].astype(F32))
        cat = jnp.concatenate([qp, q], axis=0)
        o_ref[...] = (b_ref[...].astype(F32) * _causal(cat, w_ref, K, T, HALO)).astype(o_ref.dtype)

    return pl.pallas_call(
        body, name=name, grid=(C // Ct, S // T),
        in_specs=[cur(ob), cur(oc), cur(ox), prev(oc), prev(ox), wspec], out_specs=cur(),
        out_shape=jax.ShapeDtypeStruct((S, C), BF), compiler_params=_cp(("parallel", "arbitrary")),
    )(proj, proj, proj, proj, proj, taps)


def _scconv_bwd(name, dv, proj, ob, oc, ox, taps, K, l):
    S = proj.shape[0]
    C = taps.shape[2]
    T, Ct, cur, prev, nxt, wspec = _conv_specs(S, C, (ob, oc, ox), l)
    n_t = S // T

    def body(d_ref, b_ref, c_ref, x_ref, dn_ref, bn_ref, cp_ref, xp_ref, w_ref, db_ref, dc_ref, dx_ref, dw_ref):
        i = pl.program_id(1)
        c = c_ref[...].astype(F32)
        x = x_ref[...].astype(F32)
        d = d_ref[...].astype(F32)
        q = c * x
        qp = jnp.where(i == 0, 0.0, cp_ref[...].astype(F32) * xp_ref[...].astype(F32))
        catq = jnp.concatenate([qp, q], axis=0)
        cv = _causal(catq, w_ref, K, T, HALO)
        db_ref[...] = (d * cv).astype(BF)
        dcv = d * b_ref[...].astype(F32)
        dcvn = jnp.where(i == n_t - 1, 0.0, dn_ref[...].astype(F32) * bn_ref[...].astype(F32))
        catd = jnp.concatenate([dcv, dcvn], axis=0)
        dq = _anticausal(catd, w_ref, K, T)
        dc_ref[...] = (dq * x).astype(BF)
        dx_ref[...] = (dq * c).astype(BF)

        @pl.when(i == 0)
        def _():
            dw_ref[...] = jnp.zeros_like(dw_ref)

        for k in range(K):
            o = HALO - (K - 1) + k
            dw_ref[k:k + 1, :] += jnp.sum(dcv * catq[o:o + T], axis=0, keepdims=True)

    return pl.pallas_call(
        body, name=name, grid=(C // Ct, n_t),
        in_specs=[cur(), cur(ob), cur(oc), cur(ox), nxt(), nxt(ob), prev(oc), prev(ox), wspec],
        out_specs=[cur(), cur(), cur(), pl.BlockSpec((8, Ct), lambda j, i: (0, j))],
        out_shape=[jax.ShapeDtypeStruct((S, C), BF)] * 3 + [jax.ShapeDtypeStruct((8, C), F32)],
        compiler_params=_cp(("parallel", "arbitrary")),
    )(dv, proj, proj, proj, dv, proj, proj, proj, taps)


def _mconv_fwd(name, proj, ox, taps, K, bias, l):
    S = proj.shape[0]
    C = taps.shape[2]
    T, Ct, cur, prev, nxt, wspec = _conv_specs(S, C, (ox,), l)
    bspec = pl.BlockSpec((None, 1, Ct), lambda j, i: (l, 0, j))

    def body(x_ref, xp_ref, w_ref, b_ref, o_ref):
        i = pl.program_id(1)
        xp = jnp.where(i == 0, 0.0, xp_ref[...].astype(F32))
        cat = jnp.concatenate([xp, x_ref[...].astype(F32)], axis=0)
        pre = _causal(cat, w_ref, K, T, HALO) + b_ref[...]
        o_ref[...] = (pre * _sigmoid(pre)).astype(o_ref.dtype)

    return pl.pallas_call(
        body, name=name, grid=(C // Ct, S // T),
        in_specs=[cur(ox), prev(ox), wspec, bspec], out_specs=cur(),
        out_shape=jax.ShapeDtypeStruct((S, C), BF), compiler_params=_cp(("parallel", "arbitrary")),
    )(proj, proj, taps, bias)


def _mconv_bwd(name, dout, proj, ox, taps, K, bias, l):
    S = proj.shape[0]
    C = taps.shape[2]
    T, Ct, cur, prev, nxt, wspec = _conv_specs(S, C, (ox,), l)
    n_t = S // T
    bspec = pl.BlockSpec((None, 1, Ct), lambda j, i: (l, 0, j))

    def body(d_ref, dn_ref, x_ref, xp_ref, xn_ref, w_ref, b_ref, dx_ref, dw_ref, db_ref):
        i = pl.program_id(1)
        xp = jnp.where(i == 0, 0.0, xp_ref[...].astype(F32))
        cat3 = jnp.concatenate([xp, x_ref[...].astype(F32), xn_ref[...].astype(F32)], axis=0)
        pre = _causal(cat3, w_ref, K, T + HALO, HALO) + b_ref[...]
        dn = jnp.where(i == n_t - 1, 0.0, dn_ref[...].astype(F32))
        dext = jnp.concatenate([d_ref[...].astype(F32), dn], axis=0)
        s = _sigmoid(pre)
        dpre = dext * (s * (1.0 + pre * (1.0 - s)))
        dx_ref[...] = _anticausal(dpre, w_ref, K, T).astype(BF)
        dcur = dpre[:T]

        @pl.when(i == 0)
        def _():
            dw_ref[...] = jnp.zeros_like(dw_ref)
            db_ref[...] = jnp.zeros_like(db_ref)

        db_ref[...] += jnp.sum(dcur, axis=0, keepdims=True)
        for k in range(K):
            o = HALO - (K - 1) + k
            dw_ref[k:k + 1, :] += jnp.sum(dcur * cat3[o:o + T], axis=0, keepdims=True)

    return pl.pallas_call(
        body, name=name, grid=(C // Ct, n_t),
        in_specs=[cur(), nxt(), cur(ox), prev(ox), nxt(ox), wspec, bspec],
        out_specs=[cur(), pl.BlockSpec((8, Ct), lambda j, i: (0, j)), pl.BlockSpec((1, Ct), lambda j, i: (0, j))],
        out_shape=[jax.ShapeDtypeStruct((S, C), BF), jax.ShapeDtypeStruct((8, C), F32), jax.ShapeDtypeStruct((1, C), F32)],
        compiler_params=_cp(("parallel", "arbitrary")),
    )(dout, dout, proj, proj, proj, taps, bias)


def _col(v, idx, lane):
    return jnp.sum(jnp.where(lane == idx, v, 0.0), axis=1, keepdims=True)


def _row(v, idx, sub):
    return jnp.sum(jnp.where(sub == idx, v, 0.0), axis=0, keepdims=True)


def _tri_matmul(tri_bf, v):
    hi = v.astype(BF)
    r1 = v - hi.astype(F32)
    mid = r1.astype(BF)
    lo = (r1 - mid.astype(F32)).astype(BF)
    dot = functools.partial(jnp.dot, preferred_element_type=F32)
    return dot(tri_bf, hi) + dot(tri_bf, mid) + dot(tri_bf, lo)


def _dot_nt(a, b):
    return lax.dot_general(a, b, (((1,), (1,)), ((), ())), preferred_element_type=F32)


def _dot_tn(a, b):
    return lax.dot_general(a, b, (((0,), (0,)), ((), ())), preferred_element_type=F32)


def _dot_nn(a, b):
    return jnp.dot(a, b, preferred_element_type=F32)


def _ssd_chunk_scalars(dtr_ref, par_ref, L):
    row_i = lax.broadcasted_iota(jnp.int32, (L, L), 0)
    col_i = lax.broadcasted_iota(jnp.int32, (L, L), 1)
    tri = row_i >= col_i
    pre = dtr_ref[...] + par_ref[0:1, :]
    dt_all = _softplus(pre)
    A_row = -jnp.exp(par_ref[1:2, :])
    a_all = dt_all * A_row
    acum_all = _tri_matmul(tri.astype(BF), a_all)
    return tri, pre, dt_all, A_row, a_all, acum_all, acum_all.T


def _ssd_dims(xbc, heads):
    S, conv_dim = xbc.shape
    inner = heads * SSM_HEADDIM
    N = (conv_dim - inner) // (2 * SSM_GROUPS)
    gw = inner // SSM_GROUPS
    PP = gw // LANES
    L = min(SSM_CHUNK, S)
    assert N == LANES and gw % LANES == 0 and inner % N == 0 and S % L == 0
    return S, inner, N, gw, PP, L, S // L


def _ssd_params(dt_bias, A_log, Dp):
    depth, H = dt_bias.shape
    rows = jnp.stack([dt_bias, A_log, Dp], axis=1).astype(F32)
    rows = jnp.concatenate([rows, jnp.zeros((depth, 3, LANES - H), F32)], axis=2)
    return jnp.concatenate([rows, jnp.zeros((depth, 5, LANES), F32)], axis=1)


def _ssd_fwd(name, xbc, dt_raw, par, l, heads):
    S, inner, N, gw, PP, L, nc = _ssd_dims(xbc, heads)
    G = SSM_GROUPS
    boff = inner // N

    def body(x_ref, b_ref, c_ref, dtr_ref, par_ref, y_ref, st_out_ref, st_ref):
        c = pl.program_id(0)
        g = pl.program_id(1)

        @pl.when(c == 0)
        def _():
            st_ref[pl.ds(g * PP, PP)] = jnp.zeros((PP, LANES, N), F32)

        tri, pre, dt_all, A_row, a_all, acum_all, acumT = _ssd_chunk_scalars(dtr_ref, par_ref, L)
        lane = lax.broadcasted_iota(jnp.int32, (L, LANES), 1)
        lane1 = lax.broadcasted_iota(jnp.int32, (1, LANES), 1)
        sub = lax.broadcasted_iota(jnp.int32, (LANES, L), 0)
        subp = lax.broadcasted_iota(jnp.int32, (LANES, 1), 0)
        rowl = lax.broadcasted_iota(jnp.int32, (L, 1), 0)
        lo = lane < SSM_HEADDIM
        lo1 = lane1 < SSM_HEADDIM
        Bb = b_ref[...]
        Cb = c_ref[...]
        Gm = _dot_nt(Cb, Bb)
        for j in range(PP):
            h0 = (g * PP + j) * 2
            h1 = h0 + 1
            x = x_ref[:, j * LANES:(j + 1) * LANES].astype(F32)
            dt_l = jnp.where(lo, _col(dt_all, h0, lane), _col(dt_all, h1, lane))
            ac0 = _col(acum_all, h0, lane)
            ac1 = _col(acum_all, h1, lane)
            ac_l = jnp.where(lo, ac0, ac1)
            E0 = jnp.exp(jnp.where(tri, ac0 - _row(acumT, h0, sub), -1e30))
            E1 = jnp.exp(jnp.where(tri, ac1 - _row(acumT, h1, sub), -1e30))
            xd = x * dt_l
            xdb = xd.astype(BF)
            yd = jnp.where(lo, _dot_nn((Gm * E0).astype(BF), xdb), _dot_nn((Gm * E1).astype(BF), xdb))
            prev = st_ref[g * PP + j]
            st_out_ref[0, j] = prev
            P = _dot_nt(Cb, prev.astype(BF))
            D_l = jnp.where(lo1, _col(par_ref[2:3, :], h0, lane1), _col(par_ref[2:3, :], h1, lane1))
            y_ref[:, j * LANES:(j + 1) * LANES] = (yd + P * jnp.exp(ac_l) + D_l * x).astype(y_ref.dtype)
            al0 = jnp.sum(jnp.where(rowl == L - 1, ac0, 0.0), axis=0, keepdims=True)
            al1 = jnp.sum(jnp.where(rowl == L - 1, ac1, 0.0), axis=0, keepdims=True)
            Wm = xd * jnp.exp(jnp.where(lo, al0, al1) - ac_l)
            eal = jnp.where(subp < SSM_HEADDIM, jnp.exp(al0), jnp.exp(al1))
            st_ref[g * PP + j] = eal * prev + _dot_tn(Wm.astype(BF), Bb)

    xspec = pl.BlockSpec((L, gw), lambda c, g: (c, g))
    return pl.pallas_call(
        body, name=name, grid=(nc, G),
        in_specs=[xspec, pl.BlockSpec((L, N), lambda c, g: (c, boff + g)), pl.BlockSpec((L, N), lambda c, g: (c, boff + G + g)),
                  pl.BlockSpec((L, LANES), lambda c, g: (c, 0)), pl.BlockSpec((None, 8, LANES), lambda c, g: (l, 0, 0))],
        out_specs=[xspec, pl.BlockSpec((1, PP, LANES, N), lambda c, g: (c, g, 0, 0))],
        out_shape=[jax.ShapeDtypeStruct((S, inner), BF), jax.ShapeDtypeStruct((nc, G * PP, LANES, N), F32)],
        scratch_shapes=[pltpu.VMEM((G * PP, LANES, N), F32)],
        compiler_params=_cp(("arbitrary", "arbitrary")),
    )(xbc, xbc, xbc, dt_raw, par)


def _ssd_bwd(name, dy, xbc, dt_raw, states, par, l, heads):
    S, inner, N, gw, PP, L, nc = _ssd_dims(xbc, heads)
    G = SSM_GROUPS
    boff = inner // N

    def body(dy_ref, x_ref, b_ref, c_ref, dtr_ref, par_ref, st_in_ref, dx_ref, dB_ref, dC_ref, ddt_ref, dpar_ref, dst_ref):
        c = pl.program_id(0)
        g = pl.program_id(1)

        @pl.when(c == 0)
        def _():
            dst_ref[pl.ds(g * PP, PP)] = jnp.zeros((PP, LANES, N), F32)

        @pl.when((c == 0) & (g == 0))
        def _():
            dpar_ref[...] = jnp.zeros_like(dpar_ref)

        tri, pre, dt_all, A_row, a_all, acum_all, acumT = _ssd_chunk_scalars(dtr_ref, par_ref, L)
        lane = lax.broadcasted_iota(jnp.int32, (L, LANES), 1)
        lane1 = lax.broadcasted_iota(jnp.int32, (1, LANES), 1)
        sub = lax.broadcasted_iota(jnp.int32, (LANES, L), 0)
        subp = lax.broadcasted_iota(jnp.int32, (LANES, 1), 0)
        rowl = lax.broadcasted_iota(jnp.int32, (L, 1), 0)
        lo = lane < SSM_HEADDIM
        lo1 = lane1 < SSM_HEADDIM
        Bb = b_ref[...]
        Cb = c_ref[...]
        Gm = _dot_nt(Cb, Bb)
        dG = jnp.zeros((L, L), F32)
        dBacc = jnp.zeros((L, N), F32)
        dCacc = jnp.zeros((L, N), F32)
        dac_all = jnp.zeros((L, LANES), F32)
        xds_all = jnp.zeros((L, LANES), F32)
        dD_row = jnp.zeros((1, LANES), F32)

        def rsum(v):
            return jnp.sum(v, axis=1, keepdims=True)

        def total(v):
            return jnp.sum(jnp.sum(v, axis=1, keepdims=True), axis=0, keepdims=True)

        for j in range(PP):
            h0 = (g * PP + j) * 2
            h1 = h0 + 1
            sl = slice(j * LANES, (j + 1) * LANES)
            x = x_ref[:, sl].astype(F32)
            dyv = dy_ref[:, sl].astype(F32)
            dt_l = jnp.where(lo, _col(dt_all, h0, lane), _col(dt_all, h1, lane))
            ac0 = _col(acum_all, h0, lane)
            ac1 = _col(acum_all, h1, lane)
            ac_l = jnp.where(lo, ac0, ac1)
            E0 = jnp.exp(jnp.where(tri, ac0 - _row(acumT, h0, sub), -1e30))
            E1 = jnp.exp(jnp.where(tri, ac1 - _row(acumT, h1, sub), -1e30))
            xd = x * dt_l
            xdb = xd.astype(BF)
            M0 = Gm * E0
            M1 = Gm * E1
            ea_l = jnp.exp(ac_l)
            al0 = jnp.sum(jnp.where(rowl == L - 1, ac0, 0.0), axis=0, keepdims=True)
            al1 = jnp.sum(jnp.where(rowl == L - 1, ac1, 0.0), axis=0, keepdims=True)
            dte_l = jnp.exp(jnp.where(lo, al0, al1) - ac_l)
            Wm = xd * dte_l
            prev = st_in_ref[0, j]
            prevb = prev.astype(BF)
            P = _dot_nt(Cb, prevb)
            D_l = jnp.where(lo1, _col(par_ref[2:3, :], h0, lane1), _col(par_ref[2:3, :], h1, lane1))
            dx = D_l * dyv
            s_l = jnp.sum(dyv * x, axis=0, keepdims=True)
            dD0 = rsum(jnp.where(lo1, s_l, 0.0))
            dD1 = rsum(jnp.where(lo1, 0.0, s_l))
            dyb = dyv.astype(BF)
            dM0 = _dot_nt(jnp.where(lo, dyv, 0.0).astype(BF), xdb)
            dM1 = _dot_nt(jnp.where(lo, 0.0, dyv).astype(BF), xdb)
            dxd = jnp.where(lo, _dot_tn(M0.astype(BF), dyb), _dot_tn(M1.astype(BF), dyb))
            dG = dG + dM0 * E0 + dM1 * E1
            Q0 = dM0 * M0
            Q1 = dM1 * M1
            dac0 = rsum(Q0) - rsum(Q0.T)
            dac1 = rsum(Q1) - rsum(Q1.T)
            dP = dyv * ea_l
            dPb = dP.astype(BF)
            dCacc = dCacc + _dot_nn(dPb, prevb)
            dprev = _dot_tn(dPb, Cb)
            t = dP * P
            dac0 = dac0 + rsum(jnp.where(lo, t, 0.0))
            dac1 = dac1 + rsum(jnp.where(lo, 0.0, t))
            dnew = dst_ref[g * PP + j]
            dnewb = dnew.astype(BF)
            e0 = jnp.exp(al0)
            e1 = jnp.exp(al1)
            dprev = dprev + jnp.where(subp < SSM_HEADDIM, e0, e1) * dnew
            u = dnew * prev
            dal0 = total(jnp.where(subp < SSM_HEADDIM, u, 0.0)) * e0
            dal1 = total(jnp.where(subp < SSM_HEADDIM, 0.0, u)) * e1
            dW = _dot_nt(Bb, dnewb)
            dBacc = dBacc + _dot_nn(Wm.astype(BF), dnewb)
            dxd = dxd + dW * dte_l
            tt = dW * Wm
            t0 = rsum(jnp.where(lo, tt, 0.0))
            t1 = rsum(jnp.where(lo, 0.0, tt))
            dal0 = dal0 + jnp.sum(t0, axis=0, keepdims=True)
            dal1 = dal1 + jnp.sum(t1, axis=0, keepdims=True)
            dac0 = dac0 - t0 + jnp.where(rowl == L - 1, dal0, 0.0)
            dac1 = dac1 - t1 + jnp.where(rowl == L - 1, dal1, 0.0)
            dx = dx + dxd * dt_l
            q = dxd * x
            dst_ref[g * PP + j] = dprev
            dx_ref[:, sl] = dx.astype(dx_ref.dtype)
            dac_all = dac_all + jnp.where(lane == h0, dac0, 0.0) + jnp.where(lane == h1, dac1, 0.0)
            xds_all = (xds_all + jnp.where(lane == h0, rsum(jnp.where(lo, q, 0.0)), 0.0)
                       + jnp.where(lane == h1, rsum(jnp.where(lo, 0.0, q)), 0.0))
            dD_row = dD_row + jnp.where(lane1 == h0, dD0, 0.0) + jnp.where(lane1 == h1, dD1, 0.0)

        dGb = dG.astype(BF)
        dC_ref[...] = (dCacc + _dot_nn(dGb, Bb)).astype(dC_ref.dtype)
        dB_ref[...] = (dBacc + _dot_tn(dGb, Cb)).astype(dB_ref.dtype)
        row_i = lax.broadcasted_iota(jnp.int32, (L, L), 0)
        col_i = lax.broadcasted_iota(jnp.int32, (L, L), 1)
        da_all = _tri_matmul((row_i <= col_i).astype(BF), dac_all)
        mine = (lane >= g * (2 * PP)) & (lane < (g + 1) * (2 * PP))
        ddt_all = da_all * A_row + xds_all
        draw = jnp.where(mine, ddt_all * _sigmoid(pre), 0.0)

        @pl.when(g == 0)
        def _():
            ddt_ref[...] = draw

        @pl.when(g != 0)
        def _():
            ddt_ref[...] += draw

        dbias_row = jnp.sum(draw, axis=0, keepdims=True)
        dalog_row = jnp.sum(jnp.where(mine, da_all * a_all, 0.0), axis=0, keepdims=True)
        dpar_ref[0:1, :] += dbias_row
        dpar_ref[1:2, :] += dalog_row
        dpar_ref[2:3, :] += dD_row

    xspec = pl.BlockSpec((L, gw), lambda c, g: (nc - 1 - c, g))
    nspec = pl.BlockSpec((L, N), lambda c, g: (nc - 1 - c, g))
    return pl.pallas_call(
        body, name=name, grid=(nc, G),
        in_specs=[xspec, xspec, pl.BlockSpec((L, N), lambda c, g: (nc - 1 - c, boff + g)),
                  pl.BlockSpec((L, N), lambda c, g: (nc - 1 - c, boff + G + g)),
                  pl.BlockSpec((L, LANES), lambda c, g: (nc - 1 - c, 0)), pl.BlockSpec((None, 8, LANES), lambda c, g: (l, 0, 0)),
                  pl.BlockSpec((1, PP, LANES, N), lambda c, g: (nc - 1 - c, g, 0, 0))],
        out_specs=[xspec, nspec, nspec, pl.BlockSpec((L, LANES), lambda c, g: (nc - 1 - c, 0)),
                   pl.BlockSpec((8, LANES), lambda c, g: (0, 0))],
        out_shape=[jax.ShapeDtypeStruct((S, inner), BF), jax.ShapeDtypeStruct((S, G * N), BF), jax.ShapeDtypeStruct((S, G * N), BF),
                   jax.ShapeDtypeStruct((S, LANES), F32), jax.ShapeDtypeStruct((8, LANES), F32)],
        scratch_shapes=[pltpu.VMEM((G * PP, LANES, N), F32)],
        compiler_params=_cp(("arbitrary", "arbitrary")),
    )(dy, xbc, xbc, xbc, dt_raw, par, states)


def _adamw(g, w, m, v):
    m2 = ADAM_B1 * m + (1.0 - ADAM_B1) * g
    v2 = ADAM_B2 * v + (1.0 - ADAM_B2) * (g * g)
    m_hat = m2 / (1.0 - ADAM_B1 ** ADAM_STEP)
    v_hat = v2 / (1.0 - ADAM_B2 ** ADAM_STEP)
    delta = -ADAM_LR * (m_hat / (jnp.sqrt(v_hat) + ADAM_EPS) + ADAM_WD * w)
    return delta, m2, v2


def _flat_tile(R):
    return _pick(R, (FLAT_ROW_TILE, 1024, 512, 256, 128, 64, 32, 16, 8))


def _sum_adam(name, lands, off, w, m, v):
    depth, r, c = w.shape
    cap = max(16, (4 * 1024 * 1024) // (N_DEV * c * 2))
    tr = [t for t in (512, 256, 128, 64, 32, 16) if r % t == 0 and off % t == 0 and t <= cap][0]
    ob = off // tr

    def body(*refs):
        land_refs = refs[:depth]
        w_ref, m_ref, v_ref, g_ref, d_ref, m2_ref, v2_ref = refs[depth:]
        l = pl.program_id(0)
        for i in range(depth):
            @pl.when(l == i)
            def _(i=i):
                g = land_refs[i][0].astype(F32)
                for k in range(1, N_DEV):
                    g = g + land_refs[i][k].astype(F32)
                g_ref[...] = g
                d_ref[...], m2_ref[...], v2_ref[...] = _adamw(g, w_ref[...], m_ref[...], v_ref[...])

    spec = pl.BlockSpec((None, tr, c), lambda l, t: (l, t, 0))
    land_specs = [pl.BlockSpec((N_DEV, tr, c), lambda l, t, i=i: (0, jnp.where(l == i, ob + t, ob), 0)) for i in range(depth)]
    return pl.pallas_call(
        body, name=name, grid=(depth, r // tr),
        in_specs=land_specs + [spec, spec, spec],
        out_specs=[spec] * 4, out_shape=[jax.ShapeDtypeStruct((depth, r, c), F32)] * 4,
        compiler_params=_cp(("arbitrary", "arbitrary")),
    )(*lands, w, m, v)


def _sum8(name, parts):
    R = parts.shape[1]
    TR = _flat_tile(R)

    def body(p_ref, g_ref):
        g = p_ref[0]
        for k in range(1, N_DEV):
            g = g + p_ref[k]
        g_ref[...] = g

    return pl.pallas_call(
        body, name=name, grid=(R // TR,),
        in_specs=[pl.BlockSpec((N_DEV, TR, LANES), lambda i: (0, i, 0))],
        out_specs=pl.BlockSpec((TR, LANES), lambda i: (i, 0)), out_shape=jax.ShapeDtypeStruct((R, LANES), F32),
        compiler_params=_cp(("parallel",)),
    )(parts)


def _adam_flat(name, g, w, m, v):
    R = w.shape[0]
    TR = _flat_tile(R)

    def body(g_ref, w_ref, m_ref, v_ref, d_ref, m2_ref, v2_ref):
        d_ref[...], m2_ref[...], v2_ref[...] = _adamw(g_ref[...], w_ref[...], m_ref[...], v_ref[...])

    spec = pl.BlockSpec((TR, LANES), lambda i: (i, 0))
    return pl.pallas_call(
        body, name=name, grid=(R // TR,), in_specs=[spec] * 4, out_specs=[spec] * 3,
        out_shape=[jax.ShapeDtypeStruct((R, LANES), F32)] * 3, compiler_params=_cp(("parallel",)),
    )(g, w, m, v)


PART_ROWS = 16


def _nrows(shape):
    n = 1
    for s in shape:
        n *= s
    r = -(-n // LANES)
    return -(-r // PART_ROWS) * PART_ROWS


def _as_rows(a):
    n = a.size
    r = _nrows(a.shape)
    f = a.reshape(-1)
    if r * LANES != n:
        f = jnp.concatenate([f, jnp.zeros((r * LANES - n,), a.dtype)])
    return f.reshape(r, LANES)


def _pack(arrs, mult=PART_ROWS):
    cat = jnp.concatenate([_as_rows(a) for a in arrs], axis=0)
    pad = (-cat.shape[0]) % mult
    if pad:
        cat = jnp.concatenate([cat, jnp.zeros((pad, LANES), cat.dtype)], axis=0)
    return cat


def _unpack(flat, shapes):
    lead = flat.shape[:-2]
    out = []
    o = 0
    for shp in shapes:
        n = 1
        for s in shp:
            n *= s
        r = _nrows(shp)
        blk = flat[..., o:o + r, :].reshape(lead + (r * LANES,))
        out.append(blk[..., :n].reshape(lead + tuple(shp)))
        o += r
    return out


def _full_from_shards(st, kind):
    if kind == 'row':
        return st.reshape(st.shape[0] * st.shape[1], st.shape[2])
    return jnp.transpose(st, (1, 0, 2)).reshape(st.shape[1], st.shape[0] * st.shape[2])


def _shards_from_full(full, kind):
    if kind == 'row':
        return full.reshape(N_DEV, full.shape[0] // N_DEV, full.shape[1])
    return jnp.transpose(full.reshape(full.shape[0], N_DEV, full.shape[1] // N_DEV), (1, 0, 2))


def _ffn_fwd(tag, h, g, wgu, wd, dep=None):
    xn = _rms_fwd(tag + "_rms", h, g, dep=dep)
    ab = _mm(tag + "_up", xn, wgu)
    hmid = _swiglu_fwd(tag + "_act", ab)
    hout = _mm(tag + "_down", hmid, wd, out_dtype=F32, res=h, alpha=0.5)
    return hout, (xn, ab, hmid)


def _ffn_bwd(tag, dh_out, h, g, wgu, wd, saved):
    xn, ab, hmid = saved
    dhmid = _mm(tag + "_d_hmid", dh_out, wd, tb=True, alpha=0.5)
    d_wd = _mm(tag + "_d_wd", hmid, dh_out, ta=True, alpha=0.5)
    dab = _swiglu_bwd(tag + "_d_act", dhmid, ab)
    d_wgu = _mm(tag + "_d_wgu", xn, dab, ta=True)
    dxn = _mm(tag + "_d_xn", dab, wgu, tb=True)
    dh, dg = _rms_bwd(tag + "_d_rms", dxn, h, g, dh_out)
    return dh, dg, d_wgu, d_wd


SEG_NAMES = ['scb', 'scc', 'scx', 'z', 'xbc', 'dt', 'ga', 'gm']
PERM = ['z', 'scb', 'scc', 'scx', 'ga', 'gm', 'xbc']


def _seg_layout(dims):
    D, inner, conv_dim, H = dims[:4]
    widths = dict(zip(SEG_NAMES, [D, D, D, inner, conv_dim, H, D, D]))
    offs, o = {}, 0
    for n in SEG_NAMES:
        offs[n] = (o, widths[n])
        o += widths[n]
    poffs, o = {}, 0
    for n in PERM:
        poffs[n] = (o, widths[n])
        o += widths[n]
    return offs, poffs


def _perm_w_in(w_in, dims):
    offs, _ = _seg_layout(dims)
    wp = jnp.concatenate([w_in[:, offs[n][0]:offs[n][0] + offs[n][1]] for n in PERM], axis=1)
    o, w = offs['dt']
    wdt = jnp.concatenate([w_in[:, o:o + w], jnp.zeros((w_in.shape[0], LANES - w), w_in.dtype)], axis=1)
    return wp, wdt


def _unperm_d_w_in(d_wp, d_wdt, dims):
    offs, poffs = _seg_layout(dims)
    H = dims[3]
    return jnp.concatenate([d_wdt[:, :H] if n == 'dt' else d_wp[:, poffs[n][0]:poffs[n][0] + poffs[n][1]] for n in SEG_NAMES], axis=1)


def _mixer_fwd(h, W, dims):
    H, Ksc, Km = dims[3:]
    l = W['l']
    _, poffs = _seg_layout(dims)

    def seg(n):
        o, w = poffs[n]
        assert o % w == 0
        return (proj, w, o // w)

    u = _rms_fwd("mix_rms", h, W['mix_norm'])
    proj = _mm("inproj", u, W['w_in_p'])
    dt_raw = _mm("inproj_dt", u, W['w_dt'], out_dtype=F32)
    v = _scconv_fwd("scconv_f", proj, poffs['scb'][0], poffs['scc'][0], poffs['scx'][0], W['sc_taps'], Ksc, l)
    ya = _mm("sc_out", v, W['sc_w_out'])
    xbc = _mconv_fwd("mconv_f", proj, poffs['xbc'][0], W['m_taps'], Km, W['m_conv_b'], l)
    y, states = _ssd_fwd("ssd_f", xbc, dt_raw, W['ssd_par'], l, H)
    yn = _gnorm_fwd("gnorm_f", y, seg('z'), W['m_norm'])
    ym = _mm("m_out", yn, W['m_w_out'])
    merged = _merge_fwd("merge_f", seg('ga'), seg('gm'), ya, ym)
    hout = _mm("w_o", merged, W['w_o'], out_dtype=F32, res=h)
    return hout, (u, proj, dt_raw, v, ya, xbc, y, states, yn, ym, merged)


def _mixer_bwd(dh_out, h, W, dims, saved):
    u, proj, dt_raw, v, ya, xbc, y, states, yn, ym, merged = saved
    H, Ksc, Km = dims[3:]
    l = W['l']
    _, poffs = _seg_layout(dims)

    def seg(n):
        o, w = poffs[n]
        return (proj, w, o // w)

    g = {}
    dmerged = _mm("d_merged", dh_out, W['w_o'], tb=True)
    g['w_o'] = _mm("d_w_o", merged, dh_out, ta=True)
    dga, dgm, dya, dym = _merge_bwd("merge_b", dmerged, seg('ga'), seg('gm'), ya, ym)
    g['sc_w_out'] = _mm("d_sc_w_out", v, dya, ta=True)
    dv = _mm("d_v", dya, W['sc_w_out'], tb=True)
    g['m_w_out'] = _mm("d_m_w_out", yn, dym, ta=True)
    dyn = _mm("d_yn", dym, W['m_w_out'], tb=True)
    dy, dz, d_mnorm = _gnorm_bwd("gnorm_b", dyn, y, seg('z'), W['m_norm'])
    g['m_norm'] = d_mnorm.reshape(-1)
    dxs, dB, dC, ddt, dpar = _ssd_bwd("ssd_b", dy, xbc, dt_raw, states, W['ssd_par'], l, H)
    g['m_dt_bias'] = dpar[0, :H]
    g['m_A_log'] = dpar[1, :H]
    g['m_D'] = dpar[2, :H]
    dxbc_post = jnp.concatenate([dxs, dB, dC], axis=1)
    dxbc, d_mcw, d_mcb = _mconv_bwd("mconv_b", dxbc_post, proj, poffs['xbc'][0], W['m_taps'], Km, W['m_conv_b'], l)
    g['m_conv_w'] = d_mcw[:Km]
    g['m_conv_b'] = d_mcb.reshape(-1)
    dscb, dscc, dscx, d_scw = _scconv_bwd("scconv_b", dv, proj, poffs['scb'][0], poffs['scc'][0], poffs['scx'][0],
                                          W['sc_taps'], Ksc, l)
    g['sc_conv_w'] = d_scw[:Ksc]
    dproj = jnp.concatenate([dz, dscb, dscc, dscx, dga, dgm, dxbc], axis=1)
    du = _mm("d_u_main", dproj, W['w_in_p'], tb=True, out_dtype=F32)
    du = _mm("d_u_dt", ddt, W['w_dt'], tb=True, out_dtype=F32, res=du)
    d_wp = _mm("d_w_in_main", u, dproj, ta=True)
    d_wdt = _mm("d_w_in_dt", u, ddt, ta=True)
    g['w_in'] = _unperm_d_w_in(d_wp, d_wdt, dims)
    dh, dg = _rms_bwd("mix_d_rms", du, h, W['mix_norm'], dh_out)
    g['mix_norm'] = dg.reshape(-1)
    return dh, g


def _ple_layer_fwd(h, p_l, W):
    xn = _rms_fwd("ple_rms", h, W['ple_norm'])
    gpre = _mm("ple_gate", xn, W['ple_w_gate'])
    pp = _mm("ple_proj", p_l, W['ple_w_proj'])
    hout = _ple_fwd("ple_f", h, gpre, pp)
    return hout, (xn, gpre, pp)


def _ple_layer_bwd(dh_out, h, p_l, W, saved, dep=None):
    xn, gpre, pp = saved
    g = {}
    dgpre, dpp = _ple_bwd("ple_b", dh_out, gpre, pp, dep=dep)
    g['ple_w_proj'] = _mm("d_ple_proj", p_l, dpp, ta=True)
    g['ple_w_gate'] = _mm("d_ple_gate", xn, dgpre, ta=True)
    dxn = _mm("d_ple_xn", dgpre, W['ple_w_gate'], tb=True)
    dh, dg = _rms_bwd("ple_d_rms", dxn, h, W['ple_norm'], dh_out)
    g['ple_norm'] = dg.reshape(-1)
    return dh, g


def kernel(x, p, ffn1_norm, ffn1_wg, ffn1_wu, ffn1_wd, mix_norm, w_in, sc_conv_w, sc_w_out, m_conv_w, m_conv_b, m_dt_bias, m_A_log, m_D, m_norm, m_w_out, w_o, ffn2_norm, ffn2_wg, ffn2_wu, ffn2_wd, ple_norm, ple_w_gate, ple_w_proj, final_norm, loss_target, m_ffn1_norm, m_ffn1_wg, m_ffn1_wu, m_ffn1_wd, m_mix_norm, m_w_in, m_sc_conv_w, m_sc_w_out, m_m_conv_w, m_m_conv_b, m_m_dt_bias, m_m_A_log, m_m_D, m_m_norm, m_m_w_out, m_w_o, m_ffn2_norm, m_ffn2_wg, m_ffn2_wu, m_ffn2_wd, m_ple_norm, m_ple_w_gate, m_ple_w_proj, m_final_norm, v_ffn1_norm, v_ffn1_wg, v_ffn1_wu, v_ffn1_wd, v_mix_norm, v_w_in, v_sc_conv_w, v_sc_w_out, v_m_conv_w, v_m_conv_b, v_m_dt_bias, v_m_A_log, v_m_D, v_m_norm, v_m_w_out, v_w_o, v_ffn2_norm, v_ffn2_wg, v_ffn2_wu, v_ffn2_wd, v_ple_norm, v_ple_w_gate, v_ple_w_proj, v_final_norm):
    args = (x, p, ffn1_norm, ffn1_wg, ffn1_wu, ffn1_wd, mix_norm, w_in, sc_conv_w, sc_w_out, m_conv_w, m_conv_b, m_dt_bias, m_A_log, m_D, m_norm, m_w_out, w_o, ffn2_norm, ffn2_wg, ffn2_wu, ffn2_wd, ple_norm, ple_w_gate, ple_w_proj, final_norm, loss_target, m_ffn1_norm, m_ffn1_wg, m_ffn1_wu, m_ffn1_wd, m_mix_norm, m_w_in, m_sc_conv_w, m_sc_w_out, m_m_conv_w, m_m_conv_b, m_m_dt_bias, m_m_A_log, m_m_D, m_m_norm, m_m_w_out, m_w_o, m_ffn2_norm, m_ffn2_wg, m_ffn2_wu, m_ffn2_wd, m_ple_norm, m_ple_w_gate, m_ple_w_proj, m_final_norm, v_ffn1_norm, v_ffn1_wg, v_ffn1_wu, v_ffn1_wd, v_mix_norm, v_w_in, v_sc_conv_w, v_sc_w_out, v_m_conv_w, v_m_conv_b, v_m_dt_bias, v_m_A_log, v_m_D, v_m_norm, v_m_w_out, v_w_o, v_ffn2_norm, v_ffn2_wg, v_ffn2_wu, v_ffn2_wd, v_ple_norm, v_ple_w_gate, v_ple_w_proj, v_final_norm)
    names = ARG_NAMES + ['m_' + n for n in WEIGHTS] + ['v_' + n for n in WEIGHTS]
    A = dict(zip(names, args))
    depth = ffn1_norm.shape[0]
    me = 4 * lax.axis_index("x") + 2 * lax.axis_index("y") + lax.axis_index("c")

    dims = (x.shape[-1], m_norm.shape[1], m_conv_b.shape[1], m_dt_bias.shape[1], sc_conv_w.shape[1], m_conv_w.shape[1])
    kind = dict(BIG)
    dev = lax.broadcasted_iota(jnp.int32, (N_DEV, 1, 1), 0)

    wb = {n: A[n].astype(BF) for n, _ in BIG}
    srcs = [wb[ms[0]] if len(ms) == 1 else jnp.concatenate([wb[n] for n in ms], axis=1) for _, ms in GROUPS]
    conv_g = _unpack(_exchange("gather_conv_taps", _pack([A[n] for n in CONVW]), True), [A[n].shape for n in CONVW])
    taps = {}
    for n, st in zip(CONVW, conv_g):
        taps[n] = _pad_taps(jnp.transpose(st, (1, 2, 0, 3)).reshape(depth, st.shape[2], N_DEV * st.shape[3]))
    ssd_par = _ssd_params(m_dt_bias, m_A_log, m_D)
    small3 = {n: A[n].reshape(depth, 1, -1) for n in SMALL}

    def layer_weights(l, lands):
        W = {n: (small3[n], l) for n in SMALL}
        W.update(l=l, sc_taps=taps['sc_conv_w'], m_taps=taps['m_conv_w'], m_conv_b=small3['m_conv_b'], ssd_par=ssd_par)
        for (_, ms), land, src in zip(GROUPS, lands, srcs):
            off = 0
            for n in ms:
                r = A[n].shape[1]
                st = jnp.where(dev == me, src[l, off:off + r][None], land[:, off:off + r])
                W[n] = _full_from_shards(st, kind[n])
                off += r
        W['ffn1_wgu'] = jnp.concatenate([W.pop('ffn1_wg'), W.pop('ffn1_wu')], axis=1)
        W['ffn2_wgu'] = jnp.concatenate([W.pop('ffn2_wg'), W.pop('ffn2_wu')], axis=1)
        W['w_in_p'], W['w_dt'] = _perm_w_in(W.pop('w_in'), dims)
        return W

    sems, lands, tok = _xchg_begin("gather_begin0", srcs, 0, x)
    lands = _xchg_end("gather_end0", srcs, lands, sems, 0, tok)
    h = x[0]
    saved = []
    layers = []
    for l in range(depth):
        W = layer_weights(l, lands)
        layers.append(W)
        tok = None
        if l + 1 < depth:
            sems, lands, tok = _xchg_begin(f"gather_begin{l + 1}", srcs, l + 1, lands[0])
        h1, s1 = _ffn_fwd("ffn1", h, W['ffn1_norm'], W['ffn1_wgu'], W['ffn1_wd'], dep=tok)
        h2, s2 = _mixer_fwd(h1, W, dims)
        h3, s3 = _ffn_fwd("ffn2", h2, W['ffn2_norm'], W['ffn2_wgu'], W['ffn2_wd'])
        h4, s4 = _ple_layer_fwd(h3, p[l, 0], W)
        saved.append((h, h1, h2, h3, s1, s2, s3, s4))
        h = h4
        if l + 1 < depth:
            lands = _xchg_end(f"gather_end{l + 1}", srcs, lands, sems, l + 1, h)

    dh, loss_row, d_final = _loss_head("loss_head", h, final_norm, loss_target[0])
    loss = lax.psum(loss_row[0, 0], ("x", "y", "c"))

    grads = [None] * depth
    pending = []
    tok = None
    for l in reversed(range(depth)):
        W = layers[l]
        h0, h1, h2, h3, s1, s2, s3, s4 = saved[l]
        g = {}
        dh, g4 = _ple_layer_bwd(dh, h3, p[l, 0], W, s4, dep=tok)
        g.update(g4)
        dh, dg, d_wgu, d_wd = _ffn_bwd("ffn2", dh, h2, W['ffn2_norm'], W['ffn2_wgu'], W['ffn2_wd'], s3)
        ff = d_wgu.shape[1] // 2
        g.update(ffn2_norm=dg.reshape(-1), ffn2_wg=d_wgu[:, :ff], ffn2_wu=d_wgu[:, ff:], ffn2_wd=d_wd)
        dh, g2 = _mixer_bwd(dh, h1, W, dims, s2)
        g.update(g2)
        dh, dg, d_wgu, d_wd = _ffn_bwd("ffn1", dh, h0, W['ffn1_norm'], W['ffn1_wgu'], W['ffn1_wd'], s1)
        g.update(ffn1_norm=dg.reshape(-1), ffn1_wg=d_wgu[:, :ff], ffn1_wu=d_wgu[:, ff:], ffn1_wd=d_wd)
        grads[l] = g
        send = [jnp.concatenate([_shards_from_full(g[n], kind[n]) for n in ms], axis=1) if len(ms) > 1
                else _shards_from_full(g[ms[0]], kind[ms[0]]) for _, ms in GROUPS]
        sems, lands, tok = _xchg_begin(f"scatter_begin{l}", send, None, dh)
        pending.append((l, send, lands, sems))
    grad_x = dh[None]

    g_lands = [None] * depth
    after = dh
    for l, send, lands, sems in pending:
        got = _xchg_end(f"scatter_end{l}", send, lands, sems, None, after)
        after = got[0]
        g_lands[l] = [lax.dynamic_update_slice(o, lax.dynamic_slice_in_dim(s, me, 1, axis=0), (me, 0, 0)) for o, s in zip(got, send)]

    big_res = [{}, {}, {}, {}]
    for gi, (_, ms) in enumerate(GROUPS):
        off = 0
        for n in ms:
            res = _sum_adam("adamw_" + n, [g_lands[l][gi] for l in range(depth)], off, A[n], A['m_' + n], A['v_' + n])
            for k in range(4):
                big_res[k][n] = res[k]
            off += A[n].shape[1]

    small_names = SMALL + CONVW
    small_parts = [jnp.stack([grads[l][n] for l in range(depth)]) for n in small_names] + [d_final.reshape(-1)]
    small_sum = _sum8("sum_small", _exchange("gather_small_grads", _pack(small_parts), True))
    sg = dict(zip(small_names + ['final_norm'], _unpack(small_sum, [a.shape for a in small_parts])))
    for n in CONVW:
        c = A[n].shape[-1]
        sg[n] = lax.dynamic_slice_in_dim(sg[n], me * c, c, axis=2)
    s_order = small_names + ['final_norm']
    s_shapes = [sg[n].shape for n in s_order]
    s_out = _adam_flat("adamw_small", _pack([sg[n] for n in s_order]), _pack([A[n] for n in s_order]),
                       _pack([A['m_' + n] for n in s_order]), _pack([A['v_' + n] for n in s_order]))
    small_res = [sg] + [dict(zip(s_order, _unpack(flat, s_shapes))) for flat in s_out]

    outs = [loss, grad_x]
    for k in range(4):
        for n in WEIGHTS:
            outs.append(big_res[k][n] if n in big_res[k] else small_res[k][n])
    return tuple(outs)
```

```python
import functools

import jax
import jax.numpy as jnp
from jax import lax
from jax.experimental import pallas as pl
from jax.experimental.pallas import tpu as pltpu

BF = jnp.bfloat16
F32 = jnp.float32

EPS = 1e-6
N_DEV = 8
LANES = 128
SSM_GROUPS = 4
SSM_HEADDIM = 64
SSM_CHUNK = 128
HALO = 16
VMEM_LIMIT = 56 * 1024 * 1024
FLAT_ROW_TILE = 2048

ADAM_LR = 0.001
ADAM_B1 = 0.9
ADAM_B2 = 0.999
ADAM_EPS = 1e-08
ADAM_WD = 0.01
ADAM_STEP = 10

MESH = pl.DeviceIdType.MESH

ARG_NAMES = ['x', 'p', 'ffn1_norm', 'ffn1_wg', 'ffn1_wu', 'ffn1_wd', 'mix_norm', 'w_in', 'sc_conv_w', 'sc_w_out', 'm_conv_w', 'm_conv_b', 'm_dt_bias', 'm_A_log', 'm_D', 'm_norm', 'm_w_out', 'w_o', 'ffn2_norm', 'ffn2_wg', 'ffn2_wu', 'ffn2_wd', 'ple_norm', 'ple_w_gate', 'ple_w_proj', 'final_norm', 'loss_target']
WEIGHTS = ARG_NAMES[2:26]
BIG = [('ffn1_wg', 'col'), ('ffn1_wu', 'col'), ('ffn1_wd', 'row'), ('w_in', 'col'), ('sc_w_out', 'row'),
       ('m_w_out', 'row'), ('w_o', 'row'), ('ffn2_wg', 'col'), ('ffn2_wu', 'col'), ('ffn2_wd', 'row'),
       ('ple_w_gate', 'row'), ('ple_w_proj', 'col')]
CONVW = ['sc_conv_w', 'm_conv_w']
SMALL = ['ffn1_norm', 'mix_norm', 'm_conv_b', 'm_dt_bias', 'm_A_log', 'm_D', 'm_norm', 'ffn2_norm', 'ple_norm']


def _pick(n, cands):
    for c in cands:
        if n % c == 0:
            return c
    return n


def _cp(sem):
    return pltpu.CompilerParams(dimension_semantics=sem, vmem_limit_bytes=VMEM_LIMIT)


def _sigmoid(x):
    return 1.0 / (1.0 + jnp.exp(-x))


def _softplus(x):
    return jnp.maximum(x, 0.0) + jnp.log(1.0 + jnp.exp(-jnp.abs(x)))


def _exchange(name, x, gather):
    slab = x.shape if gather else x.shape[1:]

    def body(x_ref, o_ref, send_sems, recv_sems, local_sem):
        mx, my, mc = lax.axis_index("x"), lax.axis_index("y"), lax.axis_index("c")
        me = 4 * mx + 2 * my + mc

        def src_for(k):
            return x_ref if gather else x_ref.at[k]

        local = pltpu.make_async_copy(src_for(me), o_ref.at[me], local_sem)
        local.start()
        sends = []
        peers = []
        for r in range(1, N_DEV):
            px = (mx + ((r >> 2) & 1)) % 2
            py = (my + ((r >> 1) & 1)) % 2
            pc = (mc + (r & 1)) % 2
            peer = 4 * px + 2 * py + pc
            peers.append(peer)
            cp = pltpu.make_async_remote_copy(
                src_ref=src_for(peer), dst_ref=o_ref.at[me], send_sem=send_sems.at[r - 1], recv_sem=recv_sems.at[r - 1],
                device_id=(px, py, pc), device_id_type=MESH)
            cp.start()
            sends.append(cp)
        for r in range(1, N_DEV):
            peer = peers[r - 1]
            pltpu.make_async_remote_copy(
                src_ref=src_for(peer), dst_ref=o_ref.at[peer], send_sem=send_sems.at[r - 1], recv_sem=recv_sems.at[r - 1],
                device_id=(mx, my, mc), device_id_type=MESH).wait_recv()
        for cp in sends:
            cp.wait_send()
        local.wait()

    return pl.pallas_call(
        body, name=name,
        out_shape=jax.ShapeDtypeStruct((N_DEV,) + tuple(slab), x.dtype),
        in_specs=[pl.BlockSpec(memory_space=pltpu.HBM)],
        out_specs=pl.BlockSpec(memory_space=pltpu.HBM),
        scratch_shapes=[pltpu.SemaphoreType.DMA((N_DEV - 1,)), pltpu.SemaphoreType.DMA((N_DEV - 1,)), pltpu.SemaphoreType.DMA],
    )(x)


STAGES = [[['ffn1_wd'], ['ffn1_wg', 'ffn1_wu']],
          [['ffn2_wd'], ['ffn2_wg', 'ffn2_wu'], ['sc_w_out', 'w_o', 'ple_w_gate', 'm_w_out'], ['w_in'], ['ple_w_proj']]]
_HBM = pl.BlockSpec(memory_space=pltpu.HBM)
_SEM = pl.BlockSpec(memory_space=pltpu.SEMAPHORE)
_ANY = pl.BlockSpec(memory_space=pl.ANY)
_EFFECT = pltpu.SideEffectType.DATAFLOW_SIDE_EFFECTING


def _peer_list():
    mx, my, mc = lax.axis_index("x"), lax.axis_index("y"), lax.axis_index("c")
    out = []
    for r in range(1, N_DEV):
        px = (mx + ((r >> 2) & 1)) % 2
        py = (my + ((r >> 1) & 1)) % 2
        pc = (mc + (r & 1)) % 2
        out.append((px, py, pc, 4 * px + 2 * py + pc))
    return 4 * mx + 2 * my + mc, out


def _xchg_copy(src_refs, land_refs, send_sems, recv_sems, layer, i, r, peer, dst_slab):
    px, py, pc, pidx = peer
    n = len(src_refs)
    src = src_refs[i].at[layer] if layer is not None else src_refs[i].at[pidx]
    return pltpu.make_async_remote_copy(
        src_ref=src, dst_ref=land_refs[i].at[dst_slab], send_sem=send_sems.at[r * n + i], recv_sem=recv_sems.at[r * n + i],
        device_id=(px, py, pc), device_id_type=MESH)


def _xchg_begin(name, srcs, layer, dep):
    n = len(srcs)
    slabs = [tuple(s.shape[1:]) for s in srcs]
    ncp = n * (N_DEV - 1)

    def body(*refs):
        src_refs, land_refs = refs[:n], refs[n:2 * n]
        send_sems, recv_sems = refs[2 * n + 1], refs[2 * n + 2]
        token = refs[-1]
        me, peers = _peer_list()
        for r, peer in enumerate(peers):
            for i in range(n):
                _xchg_copy(src_refs, land_refs, send_sems, recv_sems, layer, i, r, peer, me).start()
        token[...] = jnp.zeros_like(token)

    lands = [pltpu.with_memory_space_constraint(lax.empty((N_DEV,) + sl, s.dtype), pltpu.HBM) for sl, s in zip(slabs, srcs)]
    out = pl.pallas_call(
        body, name=name,
        out_shape=(pltpu.SemaphoreType.DMA((ncp,)), pltpu.SemaphoreType.DMA((ncp,)),
                   *[pltpu.HBM((N_DEV,) + sl, s.dtype) for sl, s in zip(slabs, srcs)], jax.ShapeDtypeStruct((8, LANES), F32)),
        in_specs=[_HBM] * (2 * n) + [_ANY],
        out_specs=(_SEM, _SEM, *[_HBM] * n, pl.BlockSpec(memory_space=pltpu.VMEM)),
        input_output_aliases={n + i: 2 + i for i in range(n)},
        compiler_params=pltpu.CompilerParams(has_side_effects=_EFFECT),
    )(*[pltpu.with_memory_space_constraint(s, pltpu.HBM) for s in srcs], *lands, dep)
    return (out[0], out[1]), list(out[2:2 + n]), out[-1]


def _xchg_end(name, srcs, lands, sems, layer, after):
    n = len(srcs)

    def body(*refs):
        src_refs, land_refs = refs[:n], refs[n:2 * n]
        send_sems, recv_sems = refs[2 * n], refs[2 * n + 1]
        me, peers = _peer_list()
        for r, peer in enumerate(peers):
            for i in range(n):
                cp = _xchg_copy(src_refs, land_refs, send_sems, recv_sems, layer, i, r, peer, peer[3])
                cp.wait_send()
                cp.wait_recv()

    out = pl.pallas_call(
        body, name=name,
        out_shape=tuple(pltpu.HBM(l.shape, l.dtype) for l in lands),
        in_specs=[_HBM] * (2 * n) + [_SEM, _SEM, _ANY], out_specs=tuple([_HBM] * n),
        input_output_aliases={n + i: i for i in range(n)},
        compiler_params=pltpu.CompilerParams(has_side_effects=_EFFECT),
    )(*[pltpu.with_memory_space_constraint(s, pltpu.HBM) for s in srcs], *lands, sems[0], sems[1], after)
    return list(out)


MM_TILES = (1024, 1408, 512, 256, 128)
MM_OPERAND_BYTES = 24 * 1024 * 1024


def _mm(name, a, b, *, ta=False, tb=False, out_dtype=None, res=None, alpha=1.0, dep=None):
    out_dtype = out_dtype or BF
    M, K = (a.shape[1], a.shape[0]) if ta else a.shape
    N = b.shape[0] if tb else b.shape[1]
    assert (b.shape[1] if tb else b.shape[0]) == K, (name, a.shape, b.shape)
    tm = _pick(M, MM_TILES)
    tn = _pick(N, MM_TILES)
    per_k = 2 * (tm * a.dtype.itemsize + tn * b.dtype.itemsize)
    tk = [t for t in sorted({K, 4096, 2816, 2560, 2048, 1408, 1024, 512, 256, 128}, reverse=True)
          if K % t == 0 and (t * per_k <= MM_OPERAND_BYTES or t == 128)][0]
    nk = K // tk
    a_spec = pl.BlockSpec((tk, tm), lambda i, j, k: (k, i)) if ta else pl.BlockSpec((tm, tk), lambda i, j, k: (i, k))
    b_spec = pl.BlockSpec((tn, tk), lambda i, j, k: (j, k)) if tb else pl.BlockSpec((tk, tn), lambda i, j, k: (k, j))
    dn = (((0 if ta else 1,), (1 if tb else 0,)), ((), ()))
    has_res = res is not None
    n_dep = 0 if dep is None else 1

    def body(*refs):
        a_ref, b_ref = refs[:2]
        r_ref = refs[2] if has_res else None
        o_ref = refs[2 + has_res + n_dep]

        def finish(v):
            if alpha != 1.0:
                v = v * alpha
            if has_res:
                v = r_ref[...] + v
            o_ref[...] = v.astype(o_ref.dtype)

        part = lax.dot_general(a_ref[...].astype(BF), b_ref[...].astype(BF), dn, preferred_element_type=F32)
        if nk == 1:
            finish(part)
            return
        acc = refs[-1]
        k = pl.program_id(2)

        @pl.when(k == 0)
        def _():
            acc[...] = part

        @pl.when((k > 0) & (k < nk - 1))
        def _():
            acc[...] += part

        @pl.when(k == nk - 1)
        def _():
            finish(acc[...] + part)

    in_specs = [a_spec, b_spec]
    args = [a, b]
    if has_res:
        in_specs.append(pl.BlockSpec((tm, tn), lambda i, j, k: (i, j)))
        args.append(res)
    if dep is not None:
        in_specs.append(_ANY)
        args.append(dep)
    return pl.pallas_call(
        body, name=name, grid=(M // tm, N // tn, nk),
        in_specs=in_specs, out_specs=pl.BlockSpec((tm, tn), lambda i, j, k: (i, j)),
        out_shape=jax.ShapeDtypeStruct((M, N), out_dtype),
        scratch_shapes=[pltpu.VMEM((tm, tn), F32)] if nk > 1 else [],
        compiler_params=_cp(("parallel", "parallel", "arbitrary")),
    )(*args)


def _ew(name, fn, tiled, params, outs, accs=(), tile=256, dep=None):
    tiled = [t if isinstance(t, tuple) else (t, t.shape[1], 0) for t in tiled]
    params = [q if isinstance(q, tuple) else (q, None) for q in params]
    S = tiled[0][0].shape[0]
    T = _pick(S, (tile, 128, 64, 32, 16))
    n_in = len(tiled) + len(params)
    n_dep = 0 if dep is None else 1

    def body(*refs):
        fn(pl.program_id(0) == 0, *refs[:n_in], *refs[n_in + n_dep:])

    in_specs = [pl.BlockSpec((T, w), lambda i, cb=cb: (i, cb)) for _, w, cb in tiled]
    for q, row in params:
        if row is None:
            in_specs.append(pl.BlockSpec(q.shape, lambda i: (0, 0)))
        else:
            in_specs.append(pl.BlockSpec((None, 1, q.shape[2]), lambda i, row=row: (row, 0, 0)))
    args = [t[0] for t in tiled] + [q[0] for q in params]
    if dep is not None:
        in_specs.append(pl.BlockSpec(memory_space=pl.ANY))
        args.append(dep)
    out_specs = [pl.BlockSpec((T, w), lambda i: (i, 0)) for w, _ in outs]
    out_specs += [pl.BlockSpec(shp, lambda i: (0, 0)) for shp, _ in accs]
    out_shape = [jax.ShapeDtypeStruct((S, w), dt) for w, dt in outs]
    out_shape += [jax.ShapeDtypeStruct(shp, dt) for shp, dt in accs]
    res = pl.pallas_call(
        body, name=name, grid=(S // T,), in_specs=in_specs, out_specs=out_specs, out_shape=out_shape,
        compiler_params=_cp(("arbitrary",)),
    )(*args)
    return res


def _prow(g):
    return g if isinstance(g, tuple) else g.reshape(1, -1)


def _rms_fwd(name, h, g, dep=None):
    def fn(first, h_ref, g_ref, o_ref):
        x = h_ref[...]
        r = lax.rsqrt(jnp.mean(x * x, axis=-1, keepdims=True) + EPS)
        o_ref[...] = (x * r * g_ref[...]).astype(o_ref.dtype)

    return _ew(name, fn, [h], [_prow(g)], [(h.shape[1], BF)], dep=dep)[0]


def _rms_bwd(name, dxn, h, g, res):
    D = h.shape[1]

    def fn(first, d_ref, h_ref, r_ref, g_ref, o_ref, dg_ref):
        x = h_ref[...]
        d = d_ref[...].astype(F32)
        r = lax.rsqrt(jnp.mean(x * x, axis=-1, keepdims=True) + EPS)
        xhat = x * r
        dxhat = d * g_ref[...]
        dh = r * (dxhat - xhat * jnp.mean(dxhat * xhat, axis=-1, keepdims=True))
        o_ref[...] = r_ref[...] + dh

        @pl.when(first)
        def _():
            dg_ref[...] = jnp.zeros_like(dg_ref)

        dg_ref[...] += jnp.sum(d * xhat, axis=0, keepdims=True)

    return _ew(name, fn, [dxn, h, res], [_prow(g)], [(D, F32)], [((1, D), F32)])


def _swiglu_fwd(name, ab):
    FF = ab.shape[1] // 2

    def fn(first, ab_ref, o_ref):
        a = ab_ref[:, :FF].astype(F32)
        b = ab_ref[:, FF:].astype(F32)
        o_ref[...] = (a * _sigmoid(a) * b).astype(o_ref.dtype)

    return _ew(name, fn, [ab], [], [(FF, BF)])[0]


def _swiglu_bwd(name, dhmid, ab):
    FF = ab.shape[1] // 2

    def fn(first, d_ref, ab_ref, o_ref):
        a = ab_ref[:, :FF].astype(F32)
        b = ab_ref[:, FF:].astype(F32)
        d = d_ref[...].astype(F32)
        s = _sigmoid(a)
        o_ref[:, :FF] = (d * b * (s * (1.0 + a * (1.0 - s)))).astype(o_ref.dtype)
        o_ref[:, FF:] = (d * a * s).astype(o_ref.dtype)

    return _ew(name, fn, [dhmid, ab], [], [(2 * FF, BF)])[0]


def _merge_fwd(name, ga, gm, ya, ym):
    def fn(first, ga_ref, gm_ref, ya_ref, ym_ref, o_ref):
        o = _sigmoid(ga_ref[...].astype(F32)) * ya_ref[...].astype(F32) + _sigmoid(gm_ref[...].astype(F32)) * ym_ref[...].astype(F32)
        o_ref[...] = o.astype(o_ref.dtype)

    return _ew(name, fn, [ga, gm, ya, ym], [], [(ya.shape[1], BF)])[0]


def _merge_bwd(name, dmerged, ga, gm, ya, ym):
    W = ya.shape[1]

    def fn(first, d_ref, ga_ref, gm_ref, ya_ref, ym_ref, dga_ref, dgm_ref, dya_ref, dym_ref):
        d = d_ref[...].astype(F32)
        sa = _sigmoid(ga_ref[...].astype(F32))
        sm = _sigmoid(gm_ref[...].astype(F32))
        dga_ref[...] = (d * ya_ref[...].astype(F32) * sa * (1.0 - sa)).astype(BF)
        dgm_ref[...] = (d * ym_ref[...].astype(F32) * sm * (1.0 - sm)).astype(BF)
        dya_ref[...] = (d * sa).astype(BF)
        dym_ref[...] = (d * sm).astype(BF)

    return _ew(name, fn, [dmerged, ga, gm, ya, ym], [], [(W, BF)] * 4)


def _gnorm_fwd(name, y, z, w):
    W = y.shape[1]
    gw = W // SSM_GROUPS

    def fn(first, y_ref, z_ref, w_ref, o_ref):
        for g in range(SSM_GROUPS):
            sl = slice(g * gw, (g + 1) * gw)
            zz = z_ref[:, sl].astype(F32)
            t = y_ref[:, sl].astype(F32) * (zz * _sigmoid(zz))
            r = lax.rsqrt(jnp.mean(t * t, axis=-1, keepdims=True) + EPS)
            o_ref[:, sl] = (t * r * w_ref[:, sl]).astype(o_ref.dtype)

    return _ew(name, fn, [y, z], [_prow(w)], [(W, BF)])[0]


def _gnorm_bwd(name, dyn, y, z, w):
    W = y.shape[1]
    gw = W // SSM_GROUPS

    def fn(first, d_ref, y_ref, z_ref, w_ref, dy_ref, dz_ref, dw_ref):
        @pl.when(first)
        def _():
            dw_ref[...] = jnp.zeros_like(dw_ref)

        for g in range(SSM_GROUPS):
            sl = slice(g * gw, (g + 1) * gw)
            zz = z_ref[:, sl].astype(F32)
            yy = y_ref[:, sl].astype(F32)
            d = d_ref[:, sl].astype(F32)
            s = _sigmoid(zz)
            sz = zz * s
            t = yy * sz
            r = lax.rsqrt(jnp.mean(t * t, axis=-1, keepdims=True) + EPS)
            that = t * r
            dthat = d * w_ref[:, sl]
            dt = r * (dthat - that * jnp.mean(dthat * that, axis=-1, keepdims=True))
            dw_ref[:, sl] += jnp.sum(d * that, axis=0, keepdims=True)
            dy_ref[:, sl] = (dt * sz).astype(BF)
            dz_ref[:, sl] = (dt * yy * (s * (1.0 + zz * (1.0 - s)))).astype(BF)

    return _ew(name, fn, [dyn, y, z], [_prow(w)], [(W, BF), (W, BF)], [((1, W), F32)])


def _ple_fwd(name, h, gpre, pp):
    def fn(first, h_ref, g_ref, p_ref, o_ref):
        o_ref[...] = h_ref[...] + _sigmoid(g_ref[...].astype(F32)) * p_ref[...].astype(F32)

    return _ew(name, fn, [h, gpre, pp], [], [(h.shape[1], F32)])[0]


def _ple_bwd(name, dh, gpre, pp, dep=None):
    W = dh.shape[1]

    def fn(first, d_ref, g_ref, p_ref, dg_ref, dp_ref):
        d = d_ref[...]
        s = _sigmoid(g_ref[...].astype(F32))
        dg_ref[...] = (d * p_ref[...].astype(F32) * s * (1.0 - s)).astype(BF)
        dp_ref[...] = (d * s).astype(BF)

    return _ew(name, fn, [dh, gpre, pp], [], [(W, BF), (W, BF)], dep=dep)


def _loss_head(name, h, g, target):
    D = h.shape[1]

    def fn(first, h_ref, t_ref, g_ref, dh_ref, loss_ref, dg_ref):
        x = h_ref[...]
        r = lax.rsqrt(jnp.mean(x * x, axis=-1, keepdims=True) + EPS)
        xhat = x * r
        err = xhat * g_ref[...] - t_ref[...]
        part = 0.5 * jnp.sum(jnp.mean(err * err, axis=-1, keepdims=True), axis=0, keepdims=True)
        dy = err * (1.0 / D)
        dxhat = dy * g_ref[...]
        dh_ref[...] = r * (dxhat - xhat * jnp.mean(dxhat * xhat, axis=-1, keepdims=True))

        @pl.when(first)
        def _():
            loss_ref[...] = jnp.zeros_like(loss_ref)
            dg_ref[...] = jnp.zeros_like(dg_ref)

        loss_ref[...] += jnp.broadcast_to(part, loss_ref.shape)
        dg_ref[...] += jnp.sum(dy * xhat, axis=0, keepdims=True)

    return _ew(name, fn, [h, target], [_prow(g)], [(D, F32)], [((1, LANES), F32), ((1, D), F32)])


def _conv_specs(S, C, offs, l):
    T = _pick(S, (512, 256, 128, 64, 32, 16))
    Ct = [c for c in (512, 256, 128) if C % c == 0 and all(o % c == 0 for o in offs)][0]
    per = T // HALO
    last = S // HALO - 1

    def cur(off=0):
        return pl.BlockSpec((T, Ct), lambda j, i: (i, off // Ct + j))

    def prev(off=0):
        return pl.BlockSpec((HALO, Ct), lambda j, i: (jnp.maximum(i * per - 1, 0), off // Ct + j))

    def nxt(off=0):
        return pl.BlockSpec((HALO, Ct), lambda j, i: (jnp.minimum((i + 1) * per, last), off // Ct + j))

    wspec = pl.BlockSpec((None, 8, Ct), lambda j, i: (l, 0, j))
    return T, Ct, cur, prev, nxt, wspec


def _pad_taps(w):
    return jnp.concatenate([w.astype(F32), jnp.zeros((w.shape[0], 8 - w.shape[1], w.shape[2]), F32)], axis=1)


def _causal(cat, w_ref, K, T, lead):
    out = None
    for k in range(K):
        o = lead - (K - 1) + k
        term = w_ref[k:k + 1, :] * cat[o:o + T]
        out = term if out is None else out + term
    return out


def _anticausal(cat, w_ref, K, T):
    out = None
    for k in range(K):
        o = K - 1 - k
        term = w_ref[k:k + 1, :] * cat[o:o + T]
        out = term if out is None else out + term
    return out


def _scconv_fwd(name, proj, ob, oc, ox, taps, K, l):
    S = proj.shape[0]
    C = taps.shape[2]
    T, Ct, cur, prev, nxt, wspec = _conv_specs(S, C, (ob, oc, ox), l)

    def body(b_ref, c_ref, x_ref, cp_ref, xp_ref, w_ref, o_ref):
        i = pl.program_id(1)
        q = c_ref[...].astype(F32) * x_ref[...].astype(F32)
        qp = jnp.where(i == 0, 0.0, cp_ref[...].astype(F32) * xp_ref[...].astype(F32))
        cat = jnp.concatenate([qp, q], axis=0)
        o_ref[...] = (b_ref[...].astype(F32) * _causal(cat, w_ref, K, T, HALO)).astype(o_ref.dtype)

    return pl.pallas_call(
        body, name=name, grid=(C // Ct, S // T),
        in_specs=[cur(ob), cur(oc), cur(ox), prev(oc), prev(ox), wspec], out_specs=cur(),
        out_shape=jax.ShapeDtypeStruct((S, C), BF), compiler_params=_cp(("parallel", "arbitrary")),
    )(proj, proj, proj, proj, proj, taps)


def _scconv_bwd(name, dv, proj, ob, oc, ox, taps, K, l):
    S = proj.shape[0]
    C = taps.shape[2]
    T, Ct, cur, prev, nxt, wspec = _conv_specs(S, C, (ob, oc, ox), l)
    n_t = S // T

    def body(d_ref, b_ref, c_ref, x_ref, dn_ref, bn_ref, cp_ref, xp_ref, w_ref, db_ref, dc_ref, dx_ref, dw_ref):
        i = pl.program_id(1)
        c = c_ref[...].astype(F32)
        x = x_ref[...].astype(F32)
        d = d_ref[...].astype(F32)
        q = c * x
        qp = jnp.where(i == 0, 0.0, cp_ref[...].astype(F32) * xp_ref[...].astype(F32))
        catq = jnp.concatenate([qp, q], axis=0)
        cv = _causal(catq, w_ref, K, T, HALO)
        db_ref[...] = (d * cv).astype(BF)
        dcv = d * b_ref[...].astype(F32)
        dcvn = jnp.where(i == n_t - 1, 0.0, dn_ref[...].astype(F32) * bn_ref[...].astype(F32))
        catd = jnp.concatenate([dcv, dcvn], axis=0)
        dq = _anticausal(catd, w_ref, K, T)
        dc_ref[...] = (dq * x).astype(BF)
        dx_ref[...] = (dq * c).astype(BF)

        @pl.when(i == 0)
        def _():
            dw_ref[...] = jnp.zeros_like(dw_ref)

        for k in range(K):
            o = HALO - (K - 1) + k
            dw_ref[k:k + 1, :] += jnp.sum(dcv * catq[o:o + T], axis=0, keepdims=True)

    return pl.pallas_call(
        body, name=name, grid=(C // Ct, n_t),
        in_specs=[cur(), cur(ob), cur(oc), cur(ox), nxt(), nxt(ob), prev(oc), prev(ox), wspec],
        out_specs=[cur(), cur(), cur(), pl.BlockSpec((8, Ct), lambda j, i: (0, j))],
        out_shape=[jax.ShapeDtypeStruct((S, C), BF)] * 3 + [jax.ShapeDtypeStruct((8, C), F32)],
        compiler_params=_cp(("parallel", "arbitrary")),
    )(dv, proj, proj, proj, dv, proj, proj, proj, taps)


def _mconv_fwd(name, proj, ox, taps, K, bias, l):
    S = proj.shape[0]
    C = taps.shape[2]
    T, Ct, cur, prev, nxt, wspec = _conv_specs(S, C, (ox,), l)
    bspec = pl.BlockSpec((None, 1, Ct), lambda j, i: (l, 0, j))

    def body(x_ref, xp_ref, w_ref, b_ref, o_ref):
        i = pl.program_id(1)
        xp = jnp.where(i == 0, 0.0, xp_ref[...].astype(F32))
        cat = jnp.concatenate([xp, x_ref[...].astype(F32)], axis=0)
        pre = _causal(cat, w_ref, K, T, HALO) + b_ref[...]
        o_ref[...] = (pre * _sigmoid(pre)).astype(o_ref.dtype)

    return pl.pallas_call(
        body, name=name, grid=(C // Ct, S // T),
        in_specs=[cur(ox), prev(ox), wspec, bspec], out_specs=cur(),
        out_shape=jax.ShapeDtypeStruct((S, C), BF), compiler_params=_cp(("parallel", "arbitrary")),
    )(proj, proj, taps, bias)


def _mconv_bwd(name, dout, proj, ox, taps, K, bias, l):
    S = proj.shape[0]
    C = taps.shape[2]
    T, Ct, cur, prev, nxt, wspec = _conv_specs(S, C, (ox,), l)
    n_t = S // T
    bspec = pl.BlockSpec((None, 1, Ct), lambda j, i: (l, 0, j))

    def body(d_ref, dn_ref, x_ref, xp_ref, xn_ref, w_ref, b_ref, dx_ref, dw_ref, db_ref):
        i = pl.program_id(1)
        xp = jnp.where(i == 0, 0.0, xp_ref[...].astype(F32))
        cat3 = jnp.concatenate([xp, x_ref[...].astype(F32), xn_ref[...].astype(F32)], axis=0)
        pre = _causal(cat3, w_ref, K, T + HALO, HALO) + b_ref[...]
        dn = jnp.where(i == n_t - 1, 0.0, dn_ref[...].astype(F32))
        dext = jnp.concatenate([d_ref[...].astype(F32), dn], axis=0)
        s = _sigmoid(pre)
        dpre = dext * (s * (1.0 + pre * (1.0 - s)))
        dx_ref[...] = _anticausal(dpre, w_ref, K, T).astype(BF)
        dcur = dpre[:T]

        @pl.when(i == 0)
        def _():
            dw_ref[...] = jnp.zeros_like(dw_ref)
            db_ref[...] = jnp.zeros_like(db_ref)

        db_ref[...] += jnp.sum(dcur, axis=0, keepdims=True)
        for k in range(K):
            o = HALO - (K - 1) + k
            dw_ref[k:k + 1, :] += jnp.sum(dcur * cat3[o:o + T], axis=0, keepdims=True)

    return pl.pallas_call(
        body, name=name, grid=(C // Ct, n_t),
        in_specs=[cur(), nxt(), cur(ox), prev(ox), nxt(ox), wspec, bspec],
        out_specs=[cur(), pl.BlockSpec((8, Ct), lambda j, i: (0, j)), pl.BlockSpec((1, Ct), lambda j, i: (0, j))],
        out_shape=[jax.ShapeDtypeStruct((S, C), BF), jax.ShapeDtypeStruct((8, C), F32), jax.ShapeDtypeStruct((1, C), F32)],
        compiler_params=_cp(("parallel", "arbitrary")),
    )(dout, dout, proj, proj, proj, taps, bias)


def _col(v, idx, lane):
    return jnp.sum(jnp.where(lane == idx, v, 0.0), axis=1, keepdims=True)


def _row(v, idx, sub):
    return jnp.sum(jnp.where(sub == idx, v, 0.0), axis=0, keepdims=True)


def _tri_matmul(tri_bf, v):
    hi = v.astype(BF)
    r1 = v - hi.astype(F32)
    mid = r1.astype(BF)
    lo = (r1 - mid.astype(F32)).astype(BF)
    dot = functools.partial(jnp.dot, preferred_element_type=F32)
    return dot(tri_bf, hi) + dot(tri_bf, mid) + dot(tri_bf, lo)


def _dot_nt(a, b):
    return lax.dot_general(a, b, (((1,), (1,)), ((), ())), preferred_element_type=F32)


def _dot_tn(a, b):
    return lax.dot_general(a, b, (((0,), (0,)), ((), ())), preferred_element_type=F32)


def _dot_nn(a, b):
    return jnp.dot(a, b, preferred_element_type=F32)


def _ssd_chunk_scalars(dtr_ref, par_ref, L):
    row_i = lax.broadcasted_iota(jnp.int32, (L, L), 0)
    col_i = lax.broadcasted_iota(jnp.int32, (L, L), 1)
    tri = row_i >= col_i
    pre = dtr_ref[...] + par_ref[0:1, :]
    dt_all = _softplus(pre)
    A_row = -jnp.exp(par_ref[1:2, :])
    a_all = dt_all * A_row
    acum_all = _tri_matmul(tri.astype(BF), a_all)
    return tri, pre, dt_all, A_row, a_all, acum_all, acum_all.T


def _ssd_dims(xbc, heads):
    S, conv_dim = xbc.shape
    inner = heads * SSM_HEADDIM
    N = (conv_dim - inner) // (2 * SSM_GROUPS)
    gw = inner // SSM_GROUPS
    PP = gw // LANES
    L = min(SSM_CHUNK, S)
    assert N == LANES and gw % LANES == 0 and inner % N == 0 and S % L == 0
    return S, inner, N, gw, PP, L, S // L


def _ssd_params(dt_bias, A_log, Dp):
    depth, H = dt_bias.shape
    rows = jnp.stack([dt_bias, A_log, Dp], axis=1).astype(F32)
    rows = jnp.concatenate([rows, jnp.zeros((depth, 3, LANES - H), F32)], axis=2)
    return jnp.concatenate([rows, jnp.zeros((depth, 5, LANES), F32)], axis=1)


def _ssd_fwd(name, xbc, dt_raw, par, l, heads):
    S, inner, N, gw, PP, L, nc = _ssd_dims(xbc, heads)
    G = SSM_GROUPS
    boff = inner // N

    def body(x_ref, b_ref, c_ref, dtr_ref, par_ref, y_ref, st_out_ref, st_ref):
        c = pl.program_id(0)
        g = pl.program_id(1)

        @pl.when(c == 0)
        def _():
            st_ref[pl.ds(g * PP, PP)] = jnp.zeros((PP, LANES, N), F32)

        tri, pre, dt_all, A_row, a_all, acum_all, acumT = _ssd_chunk_scalars(dtr_ref, par_ref, L)
        lane = lax.broadcasted_iota(jnp.int32, (L, LANES), 1)
        lane1 = lax.broadcasted_iota(jnp.int32, (1, LANES), 1)
        sub = lax.broadcasted_iota(jnp.int32, (LANES, L), 0)
        subp = lax.broadcasted_iota(jnp.int32, (LANES, 1), 0)
        rowl = lax.broadcasted_iota(jnp.int32, (L, 1), 0)
        lo = lane < SSM_HEADDIM
        lo1 = lane1 < SSM_HEADDIM
        Bb = b_ref[...]
        Cb = c_ref[...]
        Gm = _dot_nt(Cb, Bb)
        for j in range(PP):
            h0 = (g * PP + j) * 2
            h1 = h0 + 1
            x = x_ref[:, j * LANES:(j + 1) * LANES].astype(F32)
            dt_l = jnp.where(lo, _col(dt_all, h0, lane), _col(dt_all, h1, lane))
            ac0 = _col(acum_all, h0, lane)
            ac1 = _col(acum_all, h1, lane)
            ac_l = jnp.where(lo, ac0, ac1)
            E0 = jnp.exp(jnp.where(tri, ac0 - _row(acumT, h0, sub), -1e30))
            E1 = jnp.exp(jnp.where(tri, ac1 - _row(acumT, h1, sub), -1e30))
            xd = x * dt_l
            xdb = xd.astype(BF)
            yd = jnp.where(lo, _dot_nn((Gm * E0).astype(BF), xdb), _dot_nn((Gm * E1).astype(BF), xdb))
            prev = st_ref[g * PP + j]
            st_out_ref[0, j] = prev
            P = _dot_nt(Cb, prev.astype(BF))
            D_l = jnp.where(lo1, _col(par_ref[2:3, :], h0, lane1), _col(par_ref[2:3, :], h1, lane1))
            y_ref[:, j * LANES:(j + 1) * LANES] = (yd + P * jnp.exp(ac_l) + D_l * x).astype(y_ref.dtype)
            al0 = jnp.sum(jnp.where(rowl == L - 1, ac0, 0.0), axis=0, keepdims=True)
            al1 = jnp.sum(jnp.where(rowl == L - 1, ac1, 0.0), axis=0, keepdims=True)
            Wm = xd * jnp.exp(jnp.where(lo, al0, al1) - ac_l)
            eal = jnp.where(subp < SSM_HEADDIM, jnp.exp(al0), jnp.exp(al1))
            st_ref[g * PP + j] = eal * prev + _dot_tn(Wm.astype(BF), Bb)

    xspec = pl.BlockSpec((L, gw), lambda c, g: (c, g))
    return pl.pallas_call(
        body, name=name, grid=(nc, G),
        in_specs=[xspec, pl.BlockSpec((L, N), lambda c, g: (c, boff + g)), pl.BlockSpec((L, N), lambda c, g: (c, boff + G + g)),
                  pl.BlockSpec((L, LANES), lambda c, g: (c, 0)), pl.BlockSpec((None, 8, LANES), lambda c, g: (l, 0, 0))],
        out_specs=[xspec, pl.BlockSpec((1, PP, LANES, N), lambda c, g: (c, g, 0, 0))],
        out_shape=[jax.ShapeDtypeStruct((S, inner), BF), jax.ShapeDtypeStruct((nc, G * PP, LANES, N), F32)],
        scratch_shapes=[pltpu.VMEM((G * PP, LANES, N), F32)],
        compiler_params=_cp(("arbitrary", "arbitrary")),
    )(xbc, xbc, xbc, dt_raw, par)


def _ssd_bwd(name, dy, xbc, dt_raw, states, par, l, heads):
    S, inner, N, gw, PP, L, nc = _ssd_dims(xbc, heads)
    G = SSM_GROUPS
    boff = inner // N

    def body(dy_ref, x_ref, b_ref, c_ref, dtr_ref, par_ref, st_in_ref, dx_ref, dB_ref, dC_ref, ddt_ref, dpar_ref, dst_ref):
        c = pl.program_id(0)
        g = pl.program_id(1)

        @pl.when(c == 0)
        def _():
            dst_ref[pl.ds(g * PP, PP)] = jnp.zeros((PP, LANES, N), F32)

        @pl.when((c == 0) & (g == 0))
        def _():
            dpar_ref[...] = jnp.zeros_like(dpar_ref)

        tri, pre, dt_all, A_row, a_all, acum_all, acumT = _ssd_chunk_scalars(dtr_ref, par_ref, L)
        lane = lax.broadcasted_iota(jnp.int32, (L, LANES), 1)
        lane1 = lax.broadcasted_iota(jnp.int32, (1, LANES), 1)
        sub = lax.broadcasted_iota(jnp.int32, (LANES, L), 0)
        subp = lax.broadcasted_iota(jnp.int32, (LANES, 1), 0)
        rowl = lax.broadcasted_iota(jnp.int32, (L, 1), 0)
        lo = lane < SSM_HEADDIM
        lo1 = lane1 < SSM_HEADDIM
        Bb = b_ref[...]
        Cb = c_ref[...]
        Gm = _dot_nt(Cb, Bb)
        dG = jnp.zeros((L, L), F32)
        dBacc = jnp.zeros((L, N), F32)
        dCacc = jnp.zeros((L, N), F32)
        dac_all = jnp.zeros((L, LANES), F32)
        xds_all = jnp.zeros((L, LANES), F32)
        dD_row = jnp.zeros((1, LANES), F32)

        def rsum(v):
            return jnp.sum(v, axis=1, keepdims=True)

        def total(v):
            return jnp.sum(jnp.sum(v, axis=1, keepdims=True), axis=0, keepdims=True)

        for j in range(PP):
            h0 = (g * PP + j) * 2
            h1 = h0 + 1
            sl = slice(j * LANES, (j + 1) * LANES)
            x = x_ref[:, sl].astype(F32)
            dyv = dy_ref[:, sl].astype(F32)
            dt_l = jnp.where(lo, _col(dt_all, h0, lane), _col(dt_all, h1, lane))
            ac0 = _col(acum_all, h0, lane)
            ac1 = _col(acum_all, h1, lane)
            ac_l = jnp.where(lo, ac0, ac1)
            E0 = jnp.exp(jnp.where(tri, ac0 - _row(acumT, h0, sub), -1e30))
            E1 = jnp.exp(jnp.where(tri, ac1 - _row(acumT, h1, sub), -1e30))
            xd = x * dt_l
            xdb = xd.astype(BF)
            M0 = Gm * E0
            M1 = Gm * E1
            ea_l = jnp.exp(ac_l)
            al0 = jnp.sum(jnp.where(rowl == L - 1, ac0, 0.0), axis=0, keepdims=True)
            al1 = jnp.sum(jnp.where(rowl == L - 1, ac1, 0.0), axis=0, keepdims=True)
            dte_l = jnp.exp(jnp.where(lo, al0, al1) - ac_l)
            Wm = xd * dte_l
            prev = st_in_ref[0, j]
            prevb = prev.astype(BF)
            P = _dot_nt(Cb, prevb)
            D_l = jnp.where(lo1, _col(par_ref[2:3, :], h0, lane1), _col(par_ref[2:3, :], h1, lane1))
            dx = D_l * dyv
            s_l = jnp.sum(dyv * x, axis=0, keepdims=True)
            dD0 = rsum(jnp.where(lo1, s_l, 0.0))
            dD1 = rsum(jnp.where(lo1, 0.0, s_l))
            dyb = dyv.astype(BF)
            dM0 = _dot_nt(jnp.where(lo, dyv, 0.0).astype(BF), xdb)
            dM1 = _dot_nt(jnp.where(lo, 0.0, dyv).astype(BF), xdb)
            dxd = jnp.where(lo, _dot_tn(M0.astype(BF), dyb), _dot_tn(M1.astype(BF), dyb))
            dG = dG + dM0 * E0 + dM1 * E1
            Q0 = dM0 * M0
            Q1 = dM1 * M1
            dac0 = rsum(Q0) - rsum(Q0.T)
            dac1 = rsum(Q1) - rsum(Q1.T)
            dP = dyv * ea_l
            dPb = dP.astype(BF)
            dCacc = dCacc + _dot_nn(dPb, prevb)
            dprev = _dot_tn(dPb, Cb)
            t = dP * P
            dac0 = dac0 + rsum(jnp.where(lo, t, 0.0))
            dac1 = dac1 + rsum(jnp.where(lo, 0.0, t))
            dnew = dst_ref[g * PP + j]
            dnewb = dnew.astype(BF)
            e0 = jnp.exp(al0)
            e1 = jnp.exp(al1)
            dprev = dprev + jnp.where(subp < SSM_HEADDIM, e0, e1) * dnew
            u = dnew * prev
            dal0 = total(jnp.where(subp < SSM_HEADDIM, u, 0.0)) * e0
            dal1 = total(jnp.where(subp < SSM_HEADDIM, 0.0, u)) * e1
            dW = _dot_nt(Bb, dnewb)
            dBacc = dBacc + _dot_nn(Wm.astype(BF), dnewb)
            dxd = dxd + dW * dte_l
            tt = dW * Wm
            t0 = rsum(jnp.where(lo, tt, 0.0))
            t1 = rsum(jnp.where(lo, 0.0, tt))
            dal0 = dal0 + jnp.sum(t0, axis=0, keepdims=True)
            dal1 = dal1 + jnp.sum(t1, axis=0, keepdims=True)
            dac0 = dac0 - t0 + jnp.where(rowl == L - 1, dal0, 0.0)
            dac1 = dac1 - t1 + jnp.where(rowl == L - 1, dal1, 0.0)
            dx = dx + dxd * dt_l
            q = dxd * x
            dst_ref[g * PP + j] = dprev
            dx_ref[:, sl] = dx.astype(dx_ref.dtype)
            dac_all = dac_all + jnp.where(lane == h0, dac0, 0.0) + jnp.where(lane == h1, dac1, 0.0)
            xds_all = (xds_all + jnp.where(lane == h0, rsum(jnp.where(lo, q, 0.0)), 0.0)
                       + jnp.where(lane == h1, rsum(jnp.where(lo, 0.0, q)), 0.0))
            dD_row = dD_row + jnp.where(lane1 == h0, dD0, 0.0) + jnp.where(lane1 == h1, dD1, 0.0)

        dGb = dG.astype(BF)
        dC_ref[...] = (dCacc + _dot_nn(dGb, Bb)).astype(dC_ref.dtype)
        dB_ref[...] = (dBacc + _dot_tn(dGb, Cb)).astype(dB_ref.dtype)
        row_i = lax.broadcasted_iota(jnp.int32, (L, L), 0)
        col_i = lax.broadcasted_iota(jnp.int32, (L, L), 1)
        da_all = _tri_matmul((row_i <= col_i).astype(BF), dac_all)
        mine = (lane >= g * (2 * PP)) & (lane < (g + 1) * (2 * PP))
        ddt_all = da_all * A_row + xds_all
        draw = jnp.where(mine, ddt_all * _sigmoid(pre), 0.0)

        @pl.when(g == 0)
        def _():
            ddt_ref[...] = draw

        @pl.when(g != 0)
        def _():
            ddt_ref[...] += draw

        dbias_row = jnp.sum(draw, axis=0, keepdims=True)
        dalog_row = jnp.sum(jnp.where(mine, da_all * a_all, 0.0), axis=0, keepdims=True)
        dpar_ref[0:1, :] += dbias_row
        dpar_ref[1:2, :] += dalog_row
        dpar_ref[2:3, :] += dD_row

    xspec = pl.BlockSpec((L, gw), lambda c, g: (nc - 1 - c, g))
    nspec = pl.BlockSpec((L, N), lambda c, g: (nc - 1 - c, g))
    return pl.pallas_call(
        body, name=name, grid=(nc, G),
        in_specs=[xspec, xspec, pl.BlockSpec((L, N), lambda c, g: (nc - 1 - c, boff + g)),
                  pl.BlockSpec((L, N), lambda c, g: (nc - 1 - c, boff + G + g)),
                  pl.BlockSpec((L, LANES), lambda c, g: (nc - 1 - c, 0)), pl.BlockSpec((None, 8, LANES), lambda c, g: (l, 0, 0)),
                  pl.BlockSpec((1, PP, LANES, N), lambda c, g: (nc - 1 - c, g, 0, 0))],
        out_specs=[xspec, nspec, nspec, pl.BlockSpec((L, LANES), lambda c, g: (nc - 1 - c, 0)),
                   pl.BlockSpec((8, LANES), lambda c, g: (0, 0))],
        out_shape=[jax.ShapeDtypeStruct((S, inner), BF), jax.ShapeDtypeStruct((S, G * N), BF), jax.ShapeDtypeStruct((S, G * N), BF),
                   jax.ShapeDtypeStruct((S, LANES), F32), jax.ShapeDtypeStruct((8, LANES), F32)],
        scratch_shapes=[pltpu.VMEM((G * PP, LANES, N), F32)],
        compiler_params=_cp(("arbitrary", "arbitrary")),
    )(dy, xbc, xbc, xbc, dt_raw, par, states)


def _adamw(g, w, m, v):
    m2 = ADAM_B1 * m + (1.0 - ADAM_B1) * g
    v2 = ADAM_B2 * v + (1.0 - ADAM_B2) * (g * g)
    m_hat = m2 / (1.0 - ADAM_B1 ** ADAM_STEP)
    v_hat = v2 / (1.0 - ADAM_B2 ** ADAM_STEP)
    delta = -ADAM_LR * (m_hat / (jnp.sqrt(v_hat) + ADAM_EPS) + ADAM_WD * w)
    return delta, m2, v2


def _flat_tile(R):
    return _pick(R, (FLAT_ROW_TILE, 1024, 512, 256, 128, 64, 32, 16, 8))


def _sum_adam(name, lands, off, w, m, v):
    depth, r, c = w.shape
    cap = max(16, (4 * 1024 * 1024) // (N_DEV * c * 2))
    tr = [t for t in (512, 256, 128, 64, 32, 16) if r % t == 0 and off % t == 0 and t <= cap][0]
    ob = off // tr

    def body(*refs):
        land_refs = refs[:depth]
        w_ref, m_ref, v_ref, g_ref, d_ref, m2_ref, v2_ref = refs[depth:]
        l = pl.program_id(0)
        for i in range(depth):
            @pl.when(l == i)
            def _(i=i):
                g = land_refs[i][0].astype(F32)
                for k in range(1, N_DEV):
                    g = g + land_refs[i][k].astype(F32)
                g_ref[...] = g
                d_ref[...], m2_ref[...], v2_ref[...] = _adamw(g, w_ref[...], m_ref[...], v_ref[...])

    spec = pl.BlockSpec((None, tr, c), lambda l, t: (l, t, 0))
    land_specs = [pl.BlockSpec((N_DEV, tr, c), lambda l, t, i=i: (0, jnp.where(l == i, ob + t, ob), 0)) for i in range(depth)]
    return pl.pallas_call(
        body, name=name, grid=(depth, r // tr),
        in_specs=land_specs + [spec, spec, spec],
        out_specs=[spec] * 4, out_shape=[jax.ShapeDtypeStruct((depth, r, c), F32)] * 4,
        compiler_params=_cp(("arbitrary", "arbitrary")),
    )(*lands, w, m, v)


def _sum8(name, parts):
    R = parts.shape[1]
    TR = _flat_tile(R)

    def body(p_ref, g_ref):
        g = p_ref[0]
        for k in range(1, N_DEV):
            g = g + p_ref[k]
        g_ref[...] = g

    return pl.pallas_call(
        body, name=name, grid=(R // TR,),
        in_specs=[pl.BlockSpec((N_DEV, TR, LANES), lambda i: (0, i, 0))],
        out_specs=pl.BlockSpec((TR, LANES), lambda i: (i, 0)), out_shape=jax.ShapeDtypeStruct((R, LANES), F32),
        compiler_params=_cp(("parallel",)),
    )(parts)


def _adam_flat(name, g, w, m, v):
    R = w.shape[0]
    TR = _flat_tile(R)

    def body(g_ref, w_ref, m_ref, v_ref, d_ref, m2_ref, v2_ref):
        d_ref[...], m2_ref[...], v2_ref[...] = _adamw(g_ref[...], w_ref[...], m_ref[...], v_ref[...])

    spec = pl.BlockSpec((TR, LANES), lambda i: (i, 0))
    return pl.pallas_call(
        body, name=name, grid=(R // TR,), in_specs=[spec] * 4, out_specs=[spec] * 3,
        out_shape=[jax.ShapeDtypeStruct((R, LANES), F32)] * 3, compiler_params=_cp(("parallel",)),
    )(g, w, m, v)


PART_ROWS = 16


def _nrows(shape):
    n = 1
    for s in shape:
        n *= s
    r = -(-n // LANES)
    return -(-r // PART_ROWS) * PART_ROWS


def _as_rows(a):
    n = a.size
    r = _nrows(a.shape)
    f = a.reshape(-1)
    if r * LANES != n:
        f = jnp.concatenate([f, jnp.zeros((r * LANES - n,), a.dtype)])
    return f.reshape(r, LANES)


def _pack(arrs, mult=PART_ROWS):
    cat = jnp.concatenate([_as_rows(a) for a in arrs], axis=0)
    pad = (-cat.shape[0]) % mult
    if pad:
        cat = jnp.concatenate([cat, jnp.zeros((pad, LANES), cat.dtype)], axis=0)
    return cat


def _unpack(flat, shapes):
    lead = flat.shape[:-2]
    out = []
    o = 0
    for shp in shapes:
        n = 1
        for s in shp:
            n *= s
        r = _nrows(shp)
        blk = flat[..., o:o + r, :].reshape(lead + (r * LANES,))
        out.append(blk[..., :n].reshape(lead + tuple(shp)))
        o += r
    return out


def _full_from_shards(st, kind):
    if kind == 'row':
        return st.reshape(st.shape[0] * st.shape[1], st.shape[2])
    return jnp.transpose(st, (1, 0, 2)).reshape(st.shape[1], st.shape[0] * st.shape[2])


def _shards_from_full(full, kind):
    if kind == 'row':
        return full.reshape(N_DEV, full.shape[0] // N_DEV, full.shape[1])
    return jnp.transpose(full.reshape(full.shape[0], N_DEV, full.shape[1] // N_DEV), (1, 0, 2))


def _ffn_fwd(tag, h, g, wgu, wd, dep=None):
    xn = _rms_fwd(tag + "_rms", h, g, dep=dep)
    ab = _mm(tag + "_up", xn, wgu)
    hmid = _swiglu_fwd(tag + "_act", ab)
    hout = _mm(tag + "_down", hmid, wd, out_dtype=F32, res=h, alpha=0.5)
    return hout, (xn, ab, hmid)


def _ffn_bwd(tag, dh_out, h, g, wgu, wd, saved, dep=None):
    xn, ab, hmid = saved
    dhmid = _mm(tag + "_d_hmid", dh_out, wd, tb=True, alpha=0.5, dep=dep)
    d_wd = _mm(tag + "_d_wd", hmid, dh_out, ta=True, alpha=0.5)
    dab = _swiglu_bwd(tag + "_d_act", dhmid, ab)
    d_wgu = _mm(tag + "_d_wgu", xn, dab, ta=True)
    dxn = _mm(tag + "_d_xn", dab, wgu, tb=True)
    dh, dg = _rms_bwd(tag + "_d_rms", dxn, h, g, dh_out)
    return dh, dg, d_wgu, d_wd


SEG_NAMES = ['scb', 'scc', 'scx', 'z', 'xbc', 'dt', 'ga', 'gm']
PERM = ['z', 'scb', 'scc', 'scx', 'ga', 'gm', 'xbc']


def _seg_layout(dims):
    D, inner, conv_dim, H = dims[:4]
    widths = dict(zip(SEG_NAMES, [D, D, D, inner, conv_dim, H, D, D]))
    offs, o = {}, 0
    for n in SEG_NAMES:
        offs[n] = (o, widths[n])
        o += widths[n]
    poffs, o = {}, 0
    for n in PERM:
        poffs[n] = (o, widths[n])
        o += widths[n]
    return offs, poffs


def _perm_w_in(w_in, dims):
    offs, _ = _seg_layout(dims)
    wp = jnp.concatenate([w_in[:, offs[n][0]:offs[n][0] + offs[n][1]] for n in PERM], axis=1)
    o, w = offs['dt']
    wdt = jnp.concatenate([w_in[:, o:o + w], jnp.zeros((w_in.shape[0], LANES - w), w_in.dtype)], axis=1)
    return wp, wdt


def _unperm_d_w_in(d_wp, d_wdt, dims):
    offs, poffs = _seg_layout(dims)
    H = dims[3]
    return jnp.concatenate([d_wdt[:, :H] if n == 'dt' else d_wp[:, poffs[n][0]:poffs[n][0] + poffs[n][1]] for n in SEG_NAMES], axis=1)


def _mixer_fwd(h, W, dims, dep=None):
    H, Ksc, Km = dims[3:]
    l = W['l']
    _, poffs = _seg_layout(dims)

    def seg(n):
        o, w = poffs[n]
        assert o % w == 0
        return (proj, w, o // w)

    u = _rms_fwd("mix_rms", h, W['mix_norm'], dep=dep)
    proj = _mm("inproj", u, W['w_in_p'])
    dt_raw = _mm("inproj_dt", u, W['w_dt'], out_dtype=F32)
    v = _scconv_fwd("scconv_f", proj, poffs['scb'][0], poffs['scc'][0], poffs['scx'][0], W['sc_taps'], Ksc, l)
    ya = _mm("sc_out", v, W['sc_w_out'])
    xbc = _mconv_fwd("mconv_f", proj, poffs['xbc'][0], W['m_taps'], Km, W['m_conv_b'], l)
    y, states = _ssd_fwd("ssd_f", xbc, dt_raw, W['ssd_par'], l, H)
    yn = _gnorm_fwd("gnorm_f", y, seg('z'), W['m_norm'])
    ym = _mm("m_out", yn, W['m_w_out'])
    merged = _merge_fwd("merge_f", seg('ga'), seg('gm'), ya, ym)
    hout = _mm("w_o", merged, W['w_o'], out_dtype=F32, res=h)
    return hout, (u, proj, dt_raw, v, ya, xbc, y, states, yn, ym, merged)


def _mixer_bwd(dh_out, h, W, dims, saved):
    u, proj, dt_raw, v, ya, xbc, y, states, yn, ym, merged = saved
    H, Ksc, Km = dims[3:]
    l = W['l']
    _, poffs = _seg_layout(dims)

    def seg(n):
        o, w = poffs[n]
        return (proj, w, o // w)

    g = {}
    dmerged = _mm("d_merged", dh_out, W['w_o'], tb=True)
    g['w_o'] = _mm("d_w_o", merged, dh_out, ta=True)
    dga, dgm, dya, dym = _merge_bwd("merge_b", dmerged, seg('ga'), seg('gm'), ya, ym)
    g['sc_w_out'] = _mm("d_sc_w_out", v, dya, ta=True)
    dv = _mm("d_v", dya, W['sc_w_out'], tb=True)
    g['m_w_out'] = _mm("d_m_w_out", yn, dym, ta=True)
    dyn = _mm("d_yn", dym, W['m_w_out'], tb=True)
    dy, dz, d_mnorm = _gnorm_bwd("gnorm_b", dyn, y, seg('z'), W['m_norm'])
    g['m_norm'] = d_mnorm.reshape(-1)
    dxs, dB, dC, ddt, dpar = _ssd_bwd("ssd_b", dy, xbc, dt_raw, states, W['ssd_par'], l, H)
    g['m_dt_bias'] = dpar[0, :H]
    g['m_A_log'] = dpar[1, :H]
    g['m_D'] = dpar[2, :H]
    dxbc_post = jnp.concatenate([dxs, dB, dC], axis=1)
    dxbc, d_mcw, d_mcb = _mconv_bwd("mconv_b", dxbc_post, proj, poffs['xbc'][0], W['m_taps'], Km, W['m_conv_b'], l)
    g['m_conv_w'] = d_mcw[:Km]
    g['m_conv_b'] = d_mcb.reshape(-1)
    dscb, dscc, dscx, d_scw = _scconv_bwd("scconv_b", dv, proj, poffs['scb'][0], poffs['scc'][0], poffs['scx'][0],
                                          W['sc_taps'], Ksc, l)
    g['sc_conv_w'] = d_scw[:Ksc]
    dproj = jnp.concatenate([dz, dscb, dscc, dscx, dga, dgm, dxbc], axis=1)
    du = _mm("d_u_main", dproj, W['w_in_p'], tb=True, out_dtype=F32)
    du = _mm("d_u_dt", ddt, W['w_dt'], tb=True, out_dtype=F32, res=du)
    d_wp = _mm("d_w_in_main", u, dproj, ta=True)
    d_wdt = _mm("d_w_in_dt", u, ddt, ta=True)
    g['w_in'] = _unperm_d_w_in(d_wp, d_wdt, dims)
    dh, dg = _rms_bwd("mix_d_rms", du, h, W['mix_norm'], dh_out)
    g['mix_norm'] = dg.reshape(-1)
    return dh, g


def _ple_layer_fwd(h, p_l, W):
    xn = _rms_fwd("ple_rms", h, W['ple_norm'])
    gpre = _mm("ple_gate", xn, W['ple_w_gate'])
    pp = _mm("ple_proj", p_l, W['ple_w_proj'])
    hout = _ple_fwd("ple_f", h, gpre, pp)
    return hout, (xn, gpre, pp)


def _ple_layer_bwd(dh_out, h, p_l, W, saved, dep=None):
    xn, gpre, pp = saved
    g = {}
    dgpre, dpp = _ple_bwd("ple_b", dh_out, gpre, pp, dep=dep)
    g['ple_w_proj'] = _mm("d_ple_proj", p_l, dpp, ta=True)
    g['ple_w_gate'] = _mm("d_ple_gate", xn, dgpre, ta=True)
    dxn = _mm("d_ple_xn", dgpre, W['ple_w_gate'], tb=True)
    dh, dg = _rms_bwd("ple_d_rms", dxn, h, W['ple_norm'], dh_out)
    g['ple_norm'] = dg.reshape(-1)
    return dh, g


def kernel(x, p, ffn1_norm, ffn1_wg, ffn1_wu, ffn1_wd, mix_norm, w_in, sc_conv_w, sc_w_out, m_conv_w, m_conv_b, m_dt_bias, m_A_log, m_D, m_norm, m_w_out, w_o, ffn2_norm, ffn2_wg, ffn2_wu, ffn2_wd, ple_norm, ple_w_gate, ple_w_proj, final_norm, loss_target, m_ffn1_norm, m_ffn1_wg, m_ffn1_wu, m_ffn1_wd, m_mix_norm, m_w_in, m_sc_conv_w, m_sc_w_out, m_m_conv_w, m_m_conv_b, m_m_dt_bias, m_m_A_log, m_m_D, m_m_norm, m_m_w_out, m_w_o, m_ffn2_norm, m_ffn2_wg, m_ffn2_wu, m_ffn2_wd, m_ple_norm, m_ple_w_gate, m_ple_w_proj, m_final_norm, v_ffn1_norm, v_ffn1_wg, v_ffn1_wu, v_ffn1_wd, v_mix_norm, v_w_in, v_sc_conv_w, v_sc_w_out, v_m_conv_w, v_m_conv_b, v_m_dt_bias, v_m_A_log, v_m_D, v_m_norm, v_m_w_out, v_w_o, v_ffn2_norm, v_ffn2_wg, v_ffn2_wu, v_ffn2_wd, v_ple_norm, v_ple_w_gate, v_ple_w_proj, v_final_norm):
    args = (x, p, ffn1_norm, ffn1_wg, ffn1_wu, ffn1_wd, mix_norm, w_in, sc_conv_w, sc_w_out, m_conv_w, m_conv_b, m_dt_bias, m_A_log, m_D, m_norm, m_w_out, w_o, ffn2_norm, ffn2_wg, ffn2_wu, ffn2_wd, ple_norm, ple_w_gate, ple_w_proj, final_norm, loss_target, m_ffn1_norm, m_ffn1_wg, m_ffn1_wu, m_ffn1_wd, m_mix_norm, m_w_in, m_sc_conv_w, m_sc_w_out, m_m_conv_w, m_m_conv_b, m_m_dt_bias, m_m_A_log, m_m_D, m_m_norm, m_m_w_out, m_w_o, m_ffn2_norm, m_ffn2_wg, m_ffn2_wu, m_ffn2_wd, m_ple_norm, m_ple_w_gate, m_ple_w_proj, m_final_norm, v_ffn1_norm, v_ffn1_wg, v_ffn1_wu, v_ffn1_wd, v_mix_norm, v_w_in, v_sc_conv_w, v_sc_w_out, v_m_conv_w, v_m_conv_b, v_m_dt_bias, v_m_A_log, v_m_D, v_m_norm, v_m_w_out, v_w_o, v_ffn2_norm, v_ffn2_wg, v_ffn2_wu, v_ffn2_wd, v_ple_norm, v_ple_w_gate, v_ple_w_proj, v_final_norm)
    names = ARG_NAMES + ['m_' + n for n in WEIGHTS] + ['v_' + n for n in WEIGHTS]
    A = dict(zip(names, args))
    depth = ffn1_norm.shape[0]
    me = 4 * lax.axis_index("x") + 2 * lax.axis_index("y") + lax.axis_index("c")

    dims = (x.shape[-1], m_norm.shape[1], m_conv_b.shape[1], m_dt_bias.shape[1], sc_conv_w.shape[1], m_conv_w.shape[1])
    kind = dict(BIG)
    dev = lax.broadcasted_iota(jnp.int32, (N_DEV, 1, 1), 0)

    wb = {n: A[n].astype(BF) for n, _ in BIG}
    srcs = [[wb[ms[0]] if len(ms) == 1 else jnp.concatenate([wb[n] for n in ms], axis=1) for ms in stage] for stage in STAGES]
    conv_g = _unpack(_exchange("gather_conv_taps", _pack([A[n] for n in CONVW]), True), [A[n].shape for n in CONVW])
    taps = {}
    for n, st in zip(CONVW, conv_g):
        taps[n] = _pad_taps(jnp.transpose(st, (1, 2, 0, 3)).reshape(depth, st.shape[2], N_DEV * st.shape[3]))
    ssd_par = _ssd_params(m_dt_bias, m_A_log, m_D)
    small3 = {n: A[n].reshape(depth, 1, -1) for n in SMALL}

    def stage_weights(W, s, l, lands):
        for ms, land, src in zip(STAGES[s], lands, srcs[s]):
            off = 0
            for n in ms:
                r = A[n].shape[1]
                st = jnp.where(dev == me, src[l, off:off + r][None], land[:, off:off + r])
                W[n] = _full_from_shards(st, kind[n])
                off += r
        if s == 0:
            W['ffn1_wgu'] = jnp.concatenate([W.pop('ffn1_wg'), W.pop('ffn1_wu')], axis=1)
        else:
            W['ffn2_wgu'] = jnp.concatenate([W.pop('ffn2_wg'), W.pop('ffn2_wu')], axis=1)
            W['w_in_p'], W['w_dt'] = _perm_w_in(W.pop('w_in'), dims)

    flight = [None, None]
    sems, lands, tok = _xchg_begin("gather_begin0a", srcs[0], 0, x)
    flight[0] = (sems, lands)
    sems, lands, tok = _xchg_begin("gather_begin0b", srcs[1], 0, tok)
    flight[1] = (sems, lands)
    lands_a = _xchg_end("gather_end0a", srcs[0], flight[0][1], flight[0][0], 0, tok)
    h = x[0]
    saved = []
    layers = []
    for l in range(depth):
        W = {n: (small3[n], l) for n in SMALL}
        W.update(l=l, sc_taps=taps['sc_conv_w'], m_taps=taps['m_conv_w'], m_conv_b=small3['m_conv_b'], ssd_par=ssd_par)
        layers.append(W)
        stage_weights(W, 0, l, lands_a)
        h1, s1 = _ffn_fwd("ffn1", h, W['ffn1_norm'], W['ffn1_wgu'], W['ffn1_wd'])
        lands_b = _xchg_end(f"gather_end{l}b", srcs[1], flight[1][1], flight[1][0], l, h1)
        stage_weights(W, 1, l, lands_b)
        tok = None
        if l + 1 < depth:
            sems, lands, tok = _xchg_begin(f"gather_begin{l + 1}a", srcs[0], l + 1, lands_b[0])
            flight[0] = (sems, lands)
            sems, lands, tok = _xchg_begin(f"gather_begin{l + 1}b", srcs[1], l + 1, tok)
            flight[1] = (sems, lands)
        h2, s2 = _mixer_fwd(h1, W, dims, dep=tok)
        h3, s3 = _ffn_fwd("ffn2", h2, W['ffn2_norm'], W['ffn2_wgu'], W['ffn2_wd'])
        h4, s4 = _ple_layer_fwd(h3, p[l, 0], W)
        saved.append((h, h1, h2, h3, s1, s2, s3, s4))
        h = h4
        if l + 1 < depth:
            lands_a = _xchg_end(f"gather_end{l + 1}a", srcs[0], flight[0][1], flight[0][0], l + 1, h)

    dh, loss_row, d_final = _loss_head("loss_head", h, final_norm, loss_target[0])
    loss = lax.psum(loss_row[0, 0], ("x", "y", "c"))

    def send_bufs(g, s):
        return [jnp.concatenate([_shards_from_full(g[n], kind[n]) for n in ms], axis=1) if len(ms) > 1
                else _shards_from_full(g[ms[0]], kind[ms[0]]) for ms in STAGES[s]]

    grads = [None] * depth
    pending = []
    tok = None
    for l in reversed(range(depth)):
        W = layers[l]
        h0, h1, h2, h3, s1, s2, s3, s4 = saved[l]
        g = {}
        dh, g4 = _ple_layer_bwd(dh, h3, p[l, 0], W, s4, dep=tok)
        g.update(g4)
        dh, dg, d_wgu, d_wd = _ffn_bwd("ffn2", dh, h2, W['ffn2_norm'], W['ffn2_wgu'], W['ffn2_wd'], s3)
        ff = d_wgu.shape[1] // 2
        g.update(ffn2_norm=dg.reshape(-1), ffn2_wg=d_wgu[:, :ff], ffn2_wu=d_wgu[:, ff:], ffn2_wd=d_wd)
        dh, g2 = _mixer_bwd(dh, h1, W, dims, s2)
        g.update(g2)
        send = send_bufs(g, 1)
        sems, lands, tok = _xchg_begin(f"scatter_begin{l}b", send, None, dh)
        pending.append((l, 1, send, lands, sems))
        dh, dg, d_wgu, d_wd = _ffn_bwd("ffn1", dh, h0, W['ffn1_norm'], W['ffn1_wgu'], W['ffn1_wd'], s1, dep=tok)
        g.update(ffn1_norm=dg.reshape(-1), ffn1_wg=d_wgu[:, :ff], ffn1_wu=d_wgu[:, ff:], ffn1_wd=d_wd)
        grads[l] = g
        send = send_bufs(g, 0)
        sems, lands, tok = _xchg_begin(f"scatter_begin{l}a", send, None, dh)
        pending.append((l, 0, send, lands, sems))
    grad_x = dh[None]

    g_lands = [[None, None] for _ in range(depth)]
    after = dh
    for l, s, send, lands, sems in pending:
        got = _xchg_end(f"scatter_end{l}{'ab'[s]}", send, lands, sems, None, after)
        after = got[0]
        g_lands[l][s] = [lax.dynamic_update_slice(o, lax.dynamic_slice_in_dim(b, me, 1, axis=0), (me, 0, 0)) for o, b in zip(got, send)]

    big_res = [{}, {}, {}, {}]
    for s, stage in enumerate(STAGES):
        for gi, ms in enumerate(stage):
            off = 0
            for n in ms:
                res = _sum_adam("adamw_" + n, [g_lands[l][s][gi] for l in range(depth)], off, A[n], A['m_' + n], A['v_' + n])
                for k in range(4):
                    big_res[k][n] = res[k]
                off += A[n].shape[1]

    small_names = SMALL + CONVW
    small_parts = [jnp.stack([grads[l][n] for l in range(depth)]) for n in small_names] + [d_final.reshape(-1)]
    small_sum = _sum8("sum_small", _exchange("gather_small_grads", _pack(small_parts), True))
    sg = dict(zip(small_names + ['final_norm'], _unpack(small_sum, [a.shape for a in small_parts])))
    for n in CONVW:
        c = A[n].shape[-1]
        sg[n] = lax.dynamic_slice_in_dim(sg[n], me * c, c, axis=2)
    s_order = small_names + ['final_norm']
    s_shapes = [sg[n].shape for n in s_order]
    s_out = _adam_flat("adamw_small", _pack([sg[n] for n in s_order]), _pack([A[n] for n in s_order]),
                       _pack([A['m_' + n] for n in s_order]), _pack([A['v_' + n] for n in s_order]))
    small_res = [sg] + [dict(zip(s_order, _unpack(flat, s_shapes))) for flat in s_out]

    outs = [loss, grad_x]
    for k in range(4):
        for n in WEIGHTS:
            outs.append(big_res[k][n] if n in big_res[k] else small_res[k][n])
    return tuple(outs)
```

```python
import functools

import jax
import jax.numpy as jnp
from jax import lax
from jax.experimental import pallas as pl
from jax.experimental.pallas import tpu as pltpu

BF = jnp.bfloat16
F32 = jnp.float32

EPS = 1e-6
N_DEV = 8
LANES = 128
SSM_GROUPS = 4
SSM_HEADDIM = 64
SSM_CHUNK = 128
HALO = 16
VMEM_LIMIT = 56 * 1024 * 1024
FLAT_ROW_TILE = 2048

ADAM_LR = 0.001
ADAM_B1 = 0.9
ADAM_B2 = 0.999
ADAM_EPS = 1e-08
ADAM_WD = 0.01
ADAM_STEP = 10

MESH = pl.DeviceIdType.MESH

ARG_NAMES = ['x', 'p', 'ffn1_norm', 'ffn1_wg', 'ffn1_wu', 'ffn1_wd', 'mix_norm', 'w_in', 'sc_conv_w', 'sc_w_out', 'm_conv_w', 'm_conv_b', 'm_dt_bias', 'm_A_log', 'm_D', 'm_norm', 'm_w_out', 'w_o', 'ffn2_norm', 'ffn2_wg', 'ffn2_wu', 'ffn2_wd', 'ple_norm', 'ple_w_gate', 'ple_w_proj', 'final_norm', 'loss_target']
WEIGHTS = ARG_NAMES[2:26]
BIG = [('ffn1_wg', 'col'), ('ffn1_wu', 'col'), ('ffn1_wd', 'row'), ('w_in', 'col'), ('sc_w_out', 'row'),
       ('m_w_out', 'row'), ('w_o', 'row'), ('ffn2_wg', 'col'), ('ffn2_wu', 'col'), ('ffn2_wd', 'row'),
       ('ple_w_gate', 'row'), ('ple_w_proj', 'col')]
CONVW = ['sc_conv_w', 'm_conv_w']
SMALL = ['ffn1_norm', 'mix_norm', 'm_conv_b', 'm_dt_bias', 'm_A_log', 'm_D', 'm_norm', 'ffn2_norm', 'ple_norm']


def _pick(n, cands):
    for c in cands:
        if n % c == 0:
            return c
    return n


def _cp(sem):
    return pltpu.CompilerParams(dimension_semantics=sem, vmem_limit_bytes=VMEM_LIMIT)


def _sigmoid(x):
    return 1.0 / (1.0 + jnp.exp(-x))


def _softplus(x):
    return jnp.maximum(x, 0.0) + jnp.log(1.0 + jnp.exp(-jnp.abs(x)))


def _exchange(name, x, gather, dep=None):
    slab = x.shape if gather else x.shape[1:]

    def body(x_ref, *rest):
        o_ref, send_sems, recv_sems, local_sem = rest[-4:]
        mx, my, mc = lax.axis_index("x"), lax.axis_index("y"), lax.axis_index("c")
        me = 4 * mx + 2 * my + mc

        def src_for(k):
            return x_ref if gather else x_ref.at[k]

        local = pltpu.make_async_copy(src_for(me), o_ref.at[me], local_sem)
        local.start()
        sends = []
        peers = []
        for r in range(1, N_DEV):
            px = (mx + ((r >> 2) & 1)) % 2
            py = (my + ((r >> 1) & 1)) % 2
            pc = (mc + (r & 1)) % 2
            peer = 4 * px + 2 * py + pc
            peers.append(peer)
            cp = pltpu.make_async_remote_copy(
                src_ref=src_for(peer), dst_ref=o_ref.at[me], send_sem=send_sems.at[r - 1], recv_sem=recv_sems.at[r - 1],
                device_id=(px, py, pc), device_id_type=MESH)
            cp.start()
            sends.append(cp)
        for r in range(1, N_DEV):
            peer = peers[r - 1]
            pltpu.make_async_remote_copy(
                src_ref=src_for(peer), dst_ref=o_ref.at[peer], send_sem=send_sems.at[r - 1], recv_sem=recv_sems.at[r - 1],
                device_id=(mx, my, mc), device_id_type=MESH).wait_recv()
        for cp in sends:
            cp.wait_send()
        local.wait()

    return pl.pallas_call(
        body, name=name,
        out_shape=jax.ShapeDtypeStruct((N_DEV,) + tuple(slab), x.dtype),
        in_specs=[pl.BlockSpec(memory_space=pltpu.HBM)] + ([] if dep is None else [pl.BlockSpec(memory_space=pl.ANY)]),
        out_specs=pl.BlockSpec(memory_space=pltpu.HBM),
        scratch_shapes=[pltpu.SemaphoreType.DMA((N_DEV - 1,)), pltpu.SemaphoreType.DMA((N_DEV - 1,)), pltpu.SemaphoreType.DMA],
    )(*([x] if dep is None else [x, dep]))


STAGES = [[['ffn1_wd'], ['ffn1_wg', 'ffn1_wu']],
          [['ffn2_wd'], ['ffn2_wg', 'ffn2_wu'], ['sc_w_out', 'w_o', 'ple_w_gate', 'm_w_out'], ['w_in'], ['ple_w_proj']]]
_HBM = pl.BlockSpec(memory_space=pltpu.HBM)
_SEM = pl.BlockSpec(memory_space=pltpu.SEMAPHORE)
_ANY = pl.BlockSpec(memory_space=pl.ANY)
_EFFECT = pltpu.SideEffectType.DATAFLOW_SIDE_EFFECTING


def _peer_list():
    mx, my, mc = lax.axis_index("x"), lax.axis_index("y"), lax.axis_index("c")
    out = []
    for r in range(1, N_DEV):
        px = (mx + ((r >> 2) & 1)) % 2
        py = (my + ((r >> 1) & 1)) % 2
        pc = (mc + (r & 1)) % 2
        out.append((px, py, pc, 4 * px + 2 * py + pc))
    return 4 * mx + 2 * my + mc, out


def _xchg_copy(src_refs, land_refs, send_sems, recv_sems, layer, i, r, peer, dst_slab):
    px, py, pc, pidx = peer
    n = len(src_refs)
    src = src_refs[i].at[layer] if layer is not None else src_refs[i].at[pidx]
    return pltpu.make_async_remote_copy(
        src_ref=src, dst_ref=land_refs[i].at[dst_slab], send_sem=send_sems.at[r * n + i], recv_sem=recv_sems.at[r * n + i],
        device_id=(px, py, pc), device_id_type=MESH)


def _xchg_begin(name, srcs, layer, dep):
    n = len(srcs)
    slabs = [tuple(s.shape[1:]) for s in srcs]
    ncp = n * (N_DEV - 1)

    def body(*refs):
        src_refs, land_refs = refs[:n], refs[n:2 * n]
        send_sems, recv_sems = refs[2 * n + 1], refs[2 * n + 2]
        token = refs[-1]
        me, peers = _peer_list()
        for r, peer in enumerate(peers):
            for i in range(n):
                _xchg_copy(src_refs, land_refs, send_sems, recv_sems, layer, i, r, peer, me).start()
        token[...] = jnp.zeros_like(token)

    lands = [pltpu.with_memory_space_constraint(lax.empty((N_DEV,) + sl, s.dtype), pltpu.HBM) for sl, s in zip(slabs, srcs)]
    out = pl.pallas_call(
        body, name=name,
        out_shape=(pltpu.SemaphoreType.DMA((ncp,)), pltpu.SemaphoreType.DMA((ncp,)),
                   *[pltpu.HBM((N_DEV,) + sl, s.dtype) for sl, s in zip(slabs, srcs)], jax.ShapeDtypeStruct((8, LANES), F32)),
        in_specs=[_HBM] * (2 * n) + [_ANY],
        out_specs=(_SEM, _SEM, *[_HBM] * n, pl.BlockSpec(memory_space=pltpu.VMEM)),
        input_output_aliases={n + i: 2 + i for i in range(n)},
        compiler_params=pltpu.CompilerParams(has_side_effects=_EFFECT),
    )(*[pltpu.with_memory_space_constraint(s, pltpu.HBM) for s in srcs], *lands, dep)
    return (out[0], out[1]), list(out[2:2 + n]), out[-1]


def _xchg_end(name, srcs, lands, sems, layer, after):
    n = len(srcs)

    def body(*refs):
        src_refs, land_refs = refs[:n], refs[n:2 * n]
        send_sems, recv_sems = refs[2 * n], refs[2 * n + 1]
        me, peers = _peer_list()
        for r, peer in enumerate(peers):
            for i in range(n):
                cp = _xchg_copy(src_refs, land_refs, send_sems, recv_sems, layer, i, r, peer, peer[3])
                cp.wait_send()
                cp.wait_recv()

    out = pl.pallas_call(
        body, name=name,
        out_shape=tuple(pltpu.HBM(l.shape, l.dtype) for l in lands),
        in_specs=[_HBM] * (2 * n) + [_SEM, _SEM, _ANY], out_specs=tuple([_HBM] * n),
        input_output_aliases={n + i: i for i in range(n)},
        compiler_params=pltpu.CompilerParams(has_side_effects=_EFFECT),
    )(*[pltpu.with_memory_space_constraint(s, pltpu.HBM) for s in srcs], *lands, sems[0], sems[1], after)
    return list(out)


MM_TILES = (1024, 1408, 512, 256, 128)
MM_OPERAND_BYTES = 24 * 1024 * 1024


def _mm(name, a, b, *, ta=False, tb=False, out_dtype=None, res=None, alpha=1.0, dep=None):
    out_dtype = out_dtype or BF
    M, K = (a.shape[1], a.shape[0]) if ta else a.shape
    N = b.shape[0] if tb else b.shape[1]
    assert (b.shape[1] if tb else b.shape[0]) == K, (name, a.shape, b.shape)
    tm = _pick(M, MM_TILES)
    tn = _pick(N, MM_TILES)
    per_k = 2 * (tm * a.dtype.itemsize + tn * b.dtype.itemsize)
    tk = [t for t in sorted({K, 4096, 2816, 2560, 2048, 1408, 1024, 512, 256, 128}, reverse=True)
          if K % t == 0 and (t * per_k <= MM_OPERAND_BYTES or t == 128)][0]
    nk = K // tk
    a_spec = pl.BlockSpec((tk, tm), lambda i, j, k: (k, i)) if ta else pl.BlockSpec((tm, tk), lambda i, j, k: (i, k))
    b_spec = pl.BlockSpec((tn, tk), lambda i, j, k: (j, k)) if tb else pl.BlockSpec((tk, tn), lambda i, j, k: (k, j))
    dn = (((0 if ta else 1,), (1 if tb else 0,)), ((), ()))
    has_res = res is not None
    n_dep = 0 if dep is None else 1

    def body(*refs):
        a_ref, b_ref = refs[:2]
        r_ref = refs[2] if has_res else None
        o_ref = refs[2 + has_res + n_dep]

        def finish(v):
            if alpha != 1.0:
                v = v * alpha
            if has_res:
                v = r_ref[...] + v
            o_ref[...] = v.astype(o_ref.dtype)

        part = lax.dot_general(a_ref[...].astype(BF), b_ref[...].astype(BF), dn, preferred_element_type=F32)
        if nk == 1:
            finish(part)
            return
        acc = refs[-1]
        k = pl.program_id(2)

        @pl.when(k == 0)
        def _():
            acc[...] = part

        @pl.when((k > 0) & (k < nk - 1))
        def _():
            acc[...] += part

        @pl.when(k == nk - 1)
        def _():
            finish(acc[...] + part)

    in_specs = [a_spec, b_spec]
    args = [a, b]
    if has_res:
        in_specs.append(pl.BlockSpec((tm, tn), lambda i, j, k: (i, j)))
        args.append(res)
    if dep is not None:
        in_specs.append(_ANY)
        args.append(dep)
    return pl.pallas_call(
        body, name=name, grid=(M // tm, N // tn, nk),
        in_specs=in_specs, out_specs=pl.BlockSpec((tm, tn), lambda i, j, k: (i, j)),
        out_shape=jax.ShapeDtypeStruct((M, N), out_dtype),
        scratch_shapes=[pltpu.VMEM((tm, tn), F32)] if nk > 1 else [],
        compiler_params=_cp(("parallel", "parallel", "arbitrary")),
    )(*args)


def _ew(name, fn, tiled, params, outs, accs=(), tile=256, dep=None):
    tiled = [t if isinstance(t, tuple) else (t, t.shape[1], 0) for t in tiled]
    params = [q if isinstance(q, tuple) else (q, None) for q in params]
    S = tiled[0][0].shape[0]
    T = _pick(S, (tile, 128, 64, 32, 16))
    n_in = len(tiled) + len(params)
    n_dep = 0 if dep is None else 1

    def body(*refs):
        fn(pl.program_id(0) == 0, *refs[:n_in], *refs[n_in + n_dep:])

    in_specs = [pl.BlockSpec((T, w), lambda i, cb=cb: (i, cb)) for _, w, cb in tiled]
    for q, row in params:
        if row is None:
            in_specs.append(pl.BlockSpec(q.shape, lambda i: (0, 0)))
        else:
            in_specs.append(pl.BlockSpec((None, 1, q.shape[2]), lambda i, row=row: (row, 0, 0)))
    args = [t[0] for t in tiled] + [q[0] for q in params]
    if dep is not None:
        in_specs.append(pl.BlockSpec(memory_space=pl.ANY))
        args.append(dep)
    out_specs = [pl.BlockSpec((T, w), lambda i: (i, 0)) for w, _ in outs]
    out_specs += [pl.BlockSpec(shp, lambda i: (0, 0)) for shp, _ in accs]
    out_shape = [jax.ShapeDtypeStruct((S, w), dt) for w, dt in outs]
    out_shape += [jax.ShapeDtypeStruct(shp, dt) for shp, dt in accs]
    res = pl.pallas_call(
        body, name=name, grid=(S // T,), in_specs=in_specs, out_specs=out_specs, out_shape=out_shape,
        compiler_params=_cp(("arbitrary",)),
    )(*args)
    return res


def _prow(g):
    return g if isinstance(g, tuple) else g.reshape(1, -1)


def _rms_fwd(name, h, g, dep=None):
    def fn(first, h_ref, g_ref, o_ref):
        x = h_ref[...]
        r = lax.rsqrt(jnp.mean(x * x, axis=-1, keepdims=True) + EPS)
        o_ref[...] = (x * r * g_ref[...]).astype(o_ref.dtype)

    return _ew(name, fn, [h], [_prow(g)], [(h.shape[1], BF)], dep=dep)[0]


def _rms_bwd(name, dxn, h, g, res):
    D = h.shape[1]

    def fn(first, d_ref, h_ref, r_ref, g_ref, o_ref, dg_ref):
        x = h_ref[...]
        d = d_ref[...].astype(F32)
        r = lax.rsqrt(jnp.mean(x * x, axis=-1, keepdims=True) + EPS)
        xhat = x * r
        dxhat = d * g_ref[...]
        dh = r * (dxhat - xhat * jnp.mean(dxhat * xhat, axis=-1, keepdims=True))
        o_ref[...] = r_ref[...] + dh

        @pl.when(first)
        def _():
            dg_ref[...] = jnp.zeros_like(dg_ref)

        dg_ref[...] += jnp.sum(d * xhat, axis=0, keepdims=True)

    return _ew(name, fn, [dxn, h, res], [_prow(g)], [(D, F32)], [((1, D), F32)])


def _swiglu_fwd(name, ab):
    FF = ab.shape[1] // 2

    def fn(first, ab_ref, o_ref):
        a = ab_ref[:, :FF].astype(F32)
        b = ab_ref[:, FF:].astype(F32)
        o_ref[...] = (a * _sigmoid(a) * b).astype(o_ref.dtype)

    return _ew(name, fn, [ab], [], [(FF, BF)])[0]


def _swiglu_bwd(name, dhmid, ab):
    FF = ab.shape[1] // 2

    def fn(first, d_ref, ab_ref, o_ref):
        a = ab_ref[:, :FF].astype(F32)
        b = ab_ref[:, FF:].astype(F32)
        d = d_ref[...].astype(F32)
        s = _sigmoid(a)
        o_ref[:, :FF] = (d * b * (s * (1.0 + a * (1.0 - s)))).astype(o_ref.dtype)
        o_ref[:, FF:] = (d * a * s).astype(o_ref.dtype)

    return _ew(name, fn, [dhmid, ab], [], [(2 * FF, BF)])[0]


def _merge_fwd(name, ga, gm, ya, ym):
    def fn(first, ga_ref, gm_ref, ya_ref, ym_ref, o_ref):
        o = _sigmoid(ga_ref[...].astype(F32)) * ya_ref[...].astype(F32) + _sigmoid(gm_ref[...].astype(F32)) * ym_ref[...].astype(F32)
        o_ref[...] = o.astype(o_ref.dtype)

    return _ew(name, fn, [ga, gm, ya, ym], [], [(ya.shape[1], BF)])[0]


def _merge_bwd(name, dmerged, ga, gm, ya, ym):
    W = ya.shape[1]

    def fn(first, d_ref, ga_ref, gm_ref, ya_ref, ym_ref, dga_ref, dgm_ref, dya_ref, dym_ref):
        d = d_ref[...].astype(F32)
        sa = _sigmoid(ga_ref[...].astype(F32))
        sm = _sigmoid(gm_ref[...].astype(F32))
        dga_ref[...] = (d * ya_ref[...].astype(F32) * sa * (1.0 - sa)).astype(BF)
        dgm_ref[...] = (d * ym_ref[...].astype(F32) * sm * (1.0 - sm)).astype(BF)
        dya_ref[...] = (d * sa).astype(BF)
        dym_ref[...] = (d * sm).astype(BF)

    return _ew(name, fn, [dmerged, ga, gm, ya, ym], [], [(W, BF)] * 4)


def _gnorm_fwd(name, y, z, w):
    W = y.shape[1]
    gw = W // SSM_GROUPS

    def fn(first, y_ref, z_ref, w_ref, o_ref):
        for g in range(SSM_GROUPS):
            sl = slice(g * gw, (g + 1) * gw)
            zz = z_ref[:, sl].astype(F32)
            t = y_ref[:, sl].astype(F32) * (zz * _sigmoid(zz))
            r = lax.rsqrt(jnp.mean(t * t, axis=-1, keepdims=True) + EPS)
            o_ref[:, sl] = (t * r * w_ref[:, sl]).astype(o_ref.dtype)

    return _ew(name, fn, [y, z], [_prow(w)], [(W, BF)])[0]


def _gnorm_bwd(name, dyn, y, z, w):
    W = y.shape[1]
    gw = W // SSM_GROUPS

    def fn(first, d_ref, y_ref, z_ref, w_ref, dy_ref, dz_ref, dw_ref):
        @pl.when(first)
        def _():
            dw_ref[...] = jnp.zeros_like(dw_ref)

        for g in range(SSM_GROUPS):
            sl = slice(g * gw, (g + 1) * gw)
            zz = z_ref[:, sl].astype(F32)
            yy = y_ref[:, sl].astype(F32)
            d = d_ref[:, sl].astype(F32)
            s = _sigmoid(zz)
            sz = zz * s
            t = yy * sz
            r = lax.rsqrt(jnp.mean(t * t, axis=-1, keepdims=True) + EPS)
            that = t * r
            dthat = d * w_ref[:, sl]
            dt = r * (dthat - that * jnp.mean(dthat * that, axis=-1, keepdims=True))
            dw_ref[:, sl] += jnp.sum(d * that, axis=0, keepdims=True)
            dy_ref[:, sl] = (dt * sz).astype(BF)
            dz_ref[:, sl] = (dt * yy * (s * (1.0 + zz * (1.0 - s)))).astype(BF)

    return _ew(name, fn, [dyn, y, z], [_prow(w)], [(W, BF), (W, BF)], [((1, W), F32)])


def _ple_fwd(name, h, gpre, pp):
    def fn(first, h_ref, g_ref, p_ref, o_ref):
        o_ref[...] = h_ref[...] + _sigmoid(g_ref[...].astype(F32)) * p_ref[...].astype(F32)

    return _ew(name, fn, [h, gpre, pp], [], [(h.shape[1], F32)])[0]


def _ple_bwd(name, dh, gpre, pp, dep=None):
    W = dh.shape[1]

    def fn(first, d_ref, g_ref, p_ref, dg_ref, dp_ref):
        d = d_ref[...]
        s = _sigmoid(g_ref[...].astype(F32))
        dg_ref[...] = (d * p_ref[...].astype(F32) * s * (1.0 - s)).astype(BF)
        dp_ref[...] = (d * s).astype(BF)

    return _ew(name, fn, [dh, gpre, pp], [], [(W, BF), (W, BF)], dep=dep)


def _loss_head(name, h, g, target):
    D = h.shape[1]

    def fn(first, h_ref, t_ref, g_ref, dh_ref, loss_ref, dg_ref):
        x = h_ref[...]
        r = lax.rsqrt(jnp.mean(x * x, axis=-1, keepdims=True) + EPS)
        xhat = x * r
        err = xhat * g_ref[...] - t_ref[...]
        part = 0.5 * jnp.sum(jnp.mean(err * err, axis=-1, keepdims=True), axis=0, keepdims=True)
        dy = err * (1.0 / D)
        dxhat = dy * g_ref[...]
        dh_ref[...] = r * (dxhat - xhat * jnp.mean(dxhat * xhat, axis=-1, keepdims=True))

        @pl.when(first)
        def _():
            loss_ref[...] = jnp.zeros_like(loss_ref)
            dg_ref[...] = jnp.zeros_like(dg_ref)

        loss_ref[...] += jnp.broadcast_to(part, loss_ref.shape)
        dg_ref[...] += jnp.sum(dy * xhat, axis=0, keepdims=True)

    return _ew(name, fn, [h, target], [_prow(g)], [(D, F32)], [((1, LANES), F32), ((1, D), F32)])


def _conv_specs(S, C, offs, l):
    T = _pick(S, (512, 256, 128, 64, 32, 16))
    Ct = [c for c in (512, 256, 128) if C % c == 0 and all(o % c == 0 for o in offs)][0]
    per = T // HALO
    last = S // HALO - 1

    def cur(off=0):
        return pl.BlockSpec((T, Ct), lambda j, i: (i, off // Ct + j))

    def prev(off=0):
        return pl.BlockSpec((HALO, Ct), lambda j, i: (jnp.maximum(i * per - 1, 0), off // Ct + j))

    def nxt(off=0):
        return pl.BlockSpec((HALO, Ct), lambda j, i: (jnp.minimum((i + 1) * per, last), off // Ct + j))

    wspec = pl.BlockSpec((None, 8, Ct), lambda j, i: (l, 0, j))
    return T, Ct, cur, prev, nxt, wspec


def _pad_taps(w):
    return jnp.concatenate([w.astype(F32), jnp.zeros((w.shape[0], 8 - w.shape[1], w.shape[2]), F32)], axis=1)


def _causal(cat, w_ref, K, T, lead):
    out = None
    for k in range(K):
        o = lead - (K - 1) + k
        term = w_ref[k:k + 1, :] * cat[o:o + T]
        out = term if out is None else out + term
    return out


def _anticausal(cat, w_ref, K, T):
    out = None
    for k in range(K):
        o = K - 1 - k
        term = w_ref[k:k + 1, :] * cat[o:o + T]
        out = term if out is None else out + term
    return out


def _scconv_fwd(name, proj, ob, oc, ox, taps, K, l):
    S = proj.shape[0]
    C = taps.shape[2]
    T, Ct, cur, prev, nxt, wspec = _conv_specs(S, C, (ob, oc, ox), l)

    def body(b_ref, c_ref, x_ref, cp_ref, xp_ref, w_ref, o_ref):
        i = pl.program_id(1)
        q = c_ref[...].astype(F32) * x_ref[...].astype(F32)
        qp = jnp.where(i == 0, 0.0, cp_ref[...].astype(F32) * xp_ref[...].astype(F32))
        cat = jnp.concatenate([qp, q], axis=0)
        o_ref[...] = (b_ref[...].astype(F32) * _causal(cat, w_ref, K, T, HALO)).astype(o_ref.dtype)

    return pl.pallas_call(
        body, name=name, grid=(C // Ct, S // T),
        in_specs=[cur(ob), cur(oc), cur(ox), prev(oc), prev(ox), wspec], out_specs=cur(),
        out_shape=jax.ShapeDtypeStruct((S, C), BF), compiler_params=_cp(("parallel", "arbitrary")),
    )(proj, proj, proj, proj, proj, taps)


def _scconv_bwd(name, dv, proj, ob, oc, ox, taps, K, l):
    S = proj.shape[0]
    C = taps.shape[2]
    T, Ct, cur, prev, nxt, wspec = _conv_specs(S, C, (ob, oc, ox), l)
    n_t = S // T

    def body(d_ref, b_ref, c_ref, x_ref, dn_ref, bn_ref, cp_ref, xp_ref, w_ref, db_ref, dc_ref, dx_ref, dw_ref):
        i = pl.program_id(1)
        c = c_ref[...].astype(F32)
        x = x_ref[...].astype(F32)
        d = d_ref[...].astype(F32)
        q = c * x
        qp = jnp.where(i == 0, 0.0, cp_ref[...].astype(F32) * xp_ref[...].astype(F32))
        catq = jnp.concatenate([qp, q], axis=0)
        cv = _causal(catq, w_ref, K, T, HALO)
        db_ref[...] = (d * cv).astype(BF)
        dcv = d * b_ref[...].astype(F32)
        dcvn = jnp.where(i == n_t - 1, 0.0, dn_ref[...].astype(F32) * bn_ref[...].astype(F32))
        catd = jnp.concatenate([dcv, dcvn], axis=0)
        dq = _anticausal(catd, w_ref, K, T)
        dc_ref[...] = (dq * x).astype(BF)
        dx_ref[...] = (dq * c).astype(BF)

        @pl.when(i == 0)
        def _():
            dw_ref[...] = jnp.zeros_like(dw_ref)

        for k in range(K):
            o = HALO - (K - 1) + k
            dw_ref[k:k + 1, :] += jnp.sum(dcv * catq[o:o + T], axis=0, keepdims=True)

    return pl.pallas_call(
        body, name=name, grid=(C // Ct, n_t),
        in_specs=[cur(), cur(ob), cur(oc), cur(ox), nxt(), nxt(ob), prev(oc), prev(ox), wspec],
        out_specs=[cur(), cur(), cur(), pl.BlockSpec((8, Ct), lambda j, i: (0, j))],
        out_shape=[jax.ShapeDtypeStruct((S, C), BF)] * 3 + [jax.ShapeDtypeStruct((8, C), F32)],
        compiler_params=_cp(("parallel", "arbitrary")),
    )(dv, proj, proj, proj, dv, proj, proj, proj, taps)


def _mconv_fwd(name, proj, ox, taps, K, bias, l):
    S = proj.shape[0]
    C = taps.shape[2]
    T, Ct, cur, prev, nxt, wspec = _conv_specs(S, C, (ox,), l)
    bspec = pl.BlockSpec((None, 1, Ct), lambda j, i: (l, 0, j))

    def body(x_ref, xp_ref, w_ref, b_ref, o_ref):
        i = pl.program_id(1)
        xp = jnp.where(i == 0, 0.0, xp_ref[...].astype(F32))
        cat = jnp.concatenate([xp, x_ref[...].astype(F32)], axis=0)
        pre = _causal(cat, w_ref, K, T, HALO) + b_ref[...]
        o_ref[...] = (pre * _sigmoid(pre)).astype(o_ref.dtype)

    return pl.pallas_call(
        body, name=name, grid=(C // Ct, S // T),
        in_specs=[cur(ox), prev(ox), wspec, bspec], out_specs=cur(),
        out_shape=jax.ShapeDtypeStruct((S, C), BF), compiler_params=_cp(("parallel", "arbitrary")),
    )(proj, proj, taps, bias)


def _mconv_bwd(name, dout, proj, ox, taps, K, bias, l):
    S = proj.shape[0]
    C = taps.shape[2]
    T, Ct, cur, prev, nxt, wspec = _conv_specs(S, C, (ox,), l)
    n_t = S // T
    bspec = pl.BlockSpec((None, 1, Ct), lambda j, i: (l, 0, j))

    def body(d_ref, dn_ref, x_ref, xp_ref, xn_ref, w_ref, b_ref, dx_ref, dw_ref, db_ref):
        i = pl.program_id(1)
        xp = jnp.where(i == 0, 0.0, xp_ref[...].astype(F32))
        cat3 = jnp.concatenate([xp, x_ref[...].astype(F32), xn_ref[...].astype(F32)], axis=0)
        pre = _causal(cat3, w_ref, K, T + HALO, HALO) + b_ref[...]
        dn = jnp.where(i == n_t - 1, 0.0, dn_ref[...].astype(F32))
        dext = jnp.concatenate([d_ref[...].astype(F32), dn], axis=0)
        s = _sigmoid(pre)
        dpre = dext * (s * (1.0 + pre * (1.0 - s)))
        dx_ref[...] = _anticausal(dpre, w_ref, K, T).astype(BF)
        dcur = dpre[:T]

        @pl.when(i == 0)
        def _():
            dw_ref[...] = jnp.zeros_like(dw_ref)
            db_ref[...] = jnp.zeros_like(db_ref)

        db_ref[...] += jnp.sum(dcur, axis=0, keepdims=True)
        for k in range(K):
            o = HALO - (K - 1) + k
            dw_ref[k:k + 1, :] += jnp.sum(dcur * cat3[o:o + T], axis=0, keepdims=True)

    return pl.pallas_call(
        body, name=name, grid=(C // Ct, n_t),
        in_specs=[cur(), nxt(), cur(ox), prev(ox), nxt(ox), wspec, bspec],
        out_specs=[cur(), pl.BlockSpec((8, Ct), lambda j, i: (0, j)), pl.BlockSpec((1, Ct), lambda j, i: (0, j))],
        out_shape=[jax.ShapeDtypeStruct((S, C), BF), jax.ShapeDtypeStruct((8, C), F32), jax.ShapeDtypeStruct((1, C), F32)],
        compiler_params=_cp(("parallel", "arbitrary")),
    )(dout, dout, proj, proj, proj, taps, bias)


def _col(v, idx, lane):
    return jnp.sum(jnp.where(lane == idx, v, 0.0), axis=1, keepdims=True)


def _row(v, idx, sub):
    return jnp.sum(jnp.where(sub == idx, v, 0.0), axis=0, keepdims=True)


def _tri_matmul(tri_bf, v):
    hi = v.astype(BF)
    r1 = v - hi.astype(F32)
    mid = r1.astype(BF)
    lo = (r1 - mid.astype(F32)).astype(BF)
    dot = functools.partial(jnp.dot, preferred_element_type=F32)
    return dot(tri_bf, hi) + dot(tri_bf, mid) + dot(tri_bf, lo)


def _dot_nt(a, b):
    return lax.dot_general(a, b, (((1,), (1,)), ((), ())), preferred_element_type=F32)


def _dot_tn(a, b):
    return lax.dot_general(a, b, (((0,), (0,)), ((), ())), preferred_element_type=F32)


def _dot_nn(a, b):
    return jnp.dot(a, b, preferred_element_type=F32)


def _ssd_chunk_scalars(dtr_ref, par_ref, L):
    row_i = lax.broadcasted_iota(jnp.int32, (L, L), 0)
    col_i = lax.broadcasted_iota(jnp.int32, (L, L), 1)
    tri = row_i >= col_i
    pre = dtr_ref[...] + par_ref[0:1, :]
    dt_all = _softplus(pre)
    A_row = -jnp.exp(par_ref[1:2, :])
    a_all = dt_all * A_row
    acum_all = _tri_matmul(tri.astype(BF), a_all)
    return tri, pre, dt_all, A_row, a_all, acum_all, acum_all.T


def _ssd_dims(xbc, heads):
    S, conv_dim = xbc.shape
    inner = heads * SSM_HEADDIM
    N = (conv_dim - inner) // (2 * SSM_GROUPS)
    gw = inner // SSM_GROUPS
    PP = gw // LANES
    L = min(SSM_CHUNK, S)
    assert N == LANES and gw % LANES == 0 and inner % N == 0 and S % L == 0
    return S, inner, N, gw, PP, L, S // L


def _ssd_params(dt_bias, A_log, Dp):
    depth, H = dt_bias.shape
    rows = jnp.stack([dt_bias, A_log, Dp], axis=1).astype(F32)
    rows = jnp.concatenate([rows, jnp.zeros((depth, 3, LANES - H), F32)], axis=2)
    return jnp.concatenate([rows, jnp.zeros((depth, 5, LANES), F32)], axis=1)


def _ssd_fwd(name, xbc, dt_raw, par, l, heads):
    S, inner, N, gw, PP, L, nc = _ssd_dims(xbc, heads)
    G = SSM_GROUPS
    boff = inner // N

    def body(x_ref, b_ref, c_ref, dtr_ref, par_ref, y_ref, st_out_ref, st_ref):
        c = pl.program_id(0)
        g = pl.program_id(1)

        @pl.when(c == 0)
        def _():
            st_ref[pl.ds(g * PP, PP)] = jnp.zeros((PP, LANES, N), F32)

        tri, pre, dt_all, A_row, a_all, acum_all, acumT = _ssd_chunk_scalars(dtr_ref, par_ref, L)
        lane = lax.broadcasted_iota(jnp.int32, (L, LANES), 1)
        lane1 = lax.broadcasted_iota(jnp.int32, (1, LANES), 1)
        sub = lax.broadcasted_iota(jnp.int32, (LANES, L), 0)
        subp = lax.broadcasted_iota(jnp.int32, (LANES, 1), 0)
        rowl = lax.broadcasted_iota(jnp.int32, (L, 1), 0)
        lo = lane < SSM_HEADDIM
        lo1 = lane1 < SSM_HEADDIM
        Bb = b_ref[...]
        Cb = c_ref[...]
        Gm = _dot_nt(Cb, Bb)
        for j in range(PP):
            h0 = (g * PP + j) * 2
            h1 = h0 + 1
            x = x_ref[:, j * LANES:(j + 1) * LANES].astype(F32)
            dt_l = jnp.where(lo, _col(dt_all, h0, lane), _col(dt_all, h1, lane))
            ac0 = _col(acum_all, h0, lane)
            ac1 = _col(acum_all, h1, lane)
            ac_l = jnp.where(lo, ac0, ac1)
            E0 = jnp.exp(jnp.where(tri, ac0 - _row(acumT, h0, sub), -1e30))
            E1 = jnp.exp(jnp.where(tri, ac1 - _row(acumT, h1, sub), -1e30))
            xd = x * dt_l
            xdb = xd.astype(BF)
            yd = jnp.where(lo, _dot_nn((Gm * E0).astype(BF), xdb), _dot_nn((Gm * E1).astype(BF), xdb))
            prev = st_ref[g * PP + j]
            st_out_ref[0, j] = prev
            P = _dot_nt(Cb, prev.astype(BF))
            D_l = jnp.where(lo1, _col(par_ref[2:3, :], h0, lane1), _col(par_ref[2:3, :], h1, lane1))
            y_ref[:, j * LANES:(j + 1) * LANES] = (yd + P * jnp.exp(ac_l) + D_l * x).astype(y_ref.dtype)
            al0 = jnp.sum(jnp.where(rowl == L - 1, ac0, 0.0), axis=0, keepdims=True)
            al1 = jnp.sum(jnp.where(rowl == L - 1, ac1, 0.0), axis=0, keepdims=True)
            Wm = xd * jnp.exp(jnp.where(lo, al0, al1) - ac_l)
            eal = jnp.where(subp < SSM_HEADDIM, jnp.exp(al0), jnp.exp(al1))
            st_ref[g * PP + j] = eal * prev + _dot_tn(Wm.astype(BF), Bb)

    xspec = pl.BlockSpec((L, gw), lambda c, g: (c, g))
    return pl.pallas_call(
        body, name=name, grid=(nc, G),
        in_specs=[xspec, pl.BlockSpec((L, N), lambda c, g: (c, boff + g)), pl.BlockSpec((L, N), lambda c, g: (c, boff + G + g)),
                  pl.BlockSpec((L, LANES), lambda c, g: (c, 0)), pl.BlockSpec((None, 8, LANES), lambda c, g: (l, 0, 0))],
        out_specs=[xspec, pl.BlockSpec((1, PP, LANES, N), lambda c, g: (c, g, 0, 0))],
        out_shape=[jax.ShapeDtypeStruct((S, inner), BF), jax.ShapeDtypeStruct((nc, G * PP, LANES, N), F32)],
        scratch_shapes=[pltpu.VMEM((G * PP, LANES, N), F32)],
        compiler_params=_cp(("arbitrary", "arbitrary")),
    )(xbc, xbc, xbc, dt_raw, par)


def _ssd_bwd(name, dy, xbc, dt_raw, states, par, l, heads):
    S, inner, N, gw, PP, L, nc = _ssd_dims(xbc, heads)
    G = SSM_GROUPS
    boff = inner // N

    def body(dy_ref, x_ref, b_ref, c_ref, dtr_ref, par_ref, st_in_ref, dx_ref, dB_ref, dC_ref, ddt_ref, dpar_ref, dst_ref):
        c = pl.program_id(0)
        g = pl.program_id(1)

        @pl.when(c == 0)
        def _():
            dst_ref[pl.ds(g * PP, PP)] = jnp.zeros((PP, LANES, N), F32)

        @pl.when((c == 0) & (g == 0))
        def _():
            dpar_ref[...] = jnp.zeros_like(dpar_ref)

        tri, pre, dt_all, A_row, a_all, acum_all, acumT = _ssd_chunk_scalars(dtr_ref, par_ref, L)
        lane = lax.broadcasted_iota(jnp.int32, (L, LANES), 1)
        lane1 = lax.broadcasted_iota(jnp.int32, (1, LANES), 1)
        sub = lax.broadcasted_iota(jnp.int32, (LANES, L), 0)
        subp = lax.broadcasted_iota(jnp.int32, (LANES, 1), 0)
        rowl = lax.broadcasted_iota(jnp.int32, (L, 1), 0)
        lo = lane < SSM_HEADDIM
        lo1 = lane1 < SSM_HEADDIM
        Bb = b_ref[...]
        Cb = c_ref[...]
        Gm = _dot_nt(Cb, Bb)
        dG = jnp.zeros((L, L), F32)
        dBacc = jnp.zeros((L, N), F32)
        dCacc = jnp.zeros((L, N), F32)
        dac_all = jnp.zeros((L, LANES), F32)
        xds_all = jnp.zeros((L, LANES), F32)
        dD_row = jnp.zeros((1, LANES), F32)

        def rsum(v):
            return jnp.sum(v, axis=1, keepdims=True)

        def total(v):
            return jnp.sum(jnp.sum(v, axis=1, keepdims=True), axis=0, keepdims=True)

        for j in range(PP):
            h0 = (g * PP + j) * 2
            h1 = h0 + 1
            sl = slice(j * LANES, (j + 1) * LANES)
            x = x_ref[:, sl].astype(F32)
            dyv = dy_ref[:, sl].astype(F32)
            dt_l = jnp.where(lo, _col(dt_all, h0, lane), _col(dt_all, h1, lane))
            ac0 = _col(acum_all, h0, lane)
            ac1 = _col(acum_all, h1, lane)
            ac_l = jnp.where(lo, ac0, ac1)
            E0 = jnp.exp(jnp.where(tri, ac0 - _row(acumT, h0, sub), -1e30))
            E1 = jnp.exp(jnp.where(tri, ac1 - _row(acumT, h1, sub), -1e30))
            xd = x * dt_l
            xdb = xd.astype(BF)
            M0 = Gm * E0
            M1 = Gm * E1
            ea_l = jnp.exp(ac_l)
            al0 = jnp.sum(jnp.where(rowl == L - 1, ac0, 0.0), axis=0, keepdims=True)
            al1 = jnp.sum(jnp.where(rowl == L - 1, ac1, 0.0), axis=0, keepdims=True)
            dte_l = jnp.exp(jnp.where(lo, al0, al1) - ac_l)
            Wm = xd * dte_l
            prev = st_in_ref[0, j]
            prevb = prev.astype(BF)
            P = _dot_nt(Cb, prevb)
            D_l = jnp.where(lo1, _col(par_ref[2:3, :], h0, lane1), _col(par_ref[2:3, :], h1, lane1))
            dx = D_l * dyv
            s_l = jnp.sum(dyv * x, axis=0, keepdims=True)
            dD0 = rsum(jnp.where(lo1, s_l, 0.0))
            dD1 = rsum(jnp.where(lo1, 0.0, s_l))
            dyb = dyv.astype(BF)
            dM0 = _dot_nt(jnp.where(lo, dyv, 0.0).astype(BF), xdb)
            dM1 = _dot_nt(jnp.where(lo, 0.0, dyv).astype(BF), xdb)
            dxd = jnp.where(lo, _dot_tn(M0.astype(BF), dyb), _dot_tn(M1.astype(BF), dyb))
            dG = dG + dM0 * E0 + dM1 * E1
            Q0 = dM0 * M0
            Q1 = dM1 * M1
            dac0 = rsum(Q0) - rsum(Q0.T)
            dac1 = rsum(Q1) - rsum(Q1.T)
            dP = dyv * ea_l
            dPb = dP.astype(BF)
            dCacc = dCacc + _dot_nn(dPb, prevb)
            dprev = _dot_tn(dPb, Cb)
            t = dP * P
            dac0 = dac0 + rsum(jnp.where(lo, t, 0.0))
            dac1 = dac1 + rsum(jnp.where(lo, 0.0, t))
            dnew = dst_ref[g * PP + j]
            dnewb = dnew.astype(BF)
            e0 = jnp.exp(al0)
            e1 = jnp.exp(al1)
            dprev = dprev + jnp.where(subp < SSM_HEADDIM, e0, e1) * dnew
            u = dnew * prev
            dal0 = total(jnp.where(subp < SSM_HEADDIM, u, 0.0)) * e0
            dal1 = total(jnp.where(subp < SSM_HEADDIM, 0.0, u)) * e1
            dW = _dot_nt(Bb, dnewb)
            dBacc = dBacc + _dot_nn(Wm.astype(BF), dnewb)
            dxd = dxd + dW * dte_l
            tt = dW * Wm
            t0 = rsum(jnp.where(lo, tt, 0.0))
            t1 = rsum(jnp.where(lo, 0.0, tt))
            dal0 = dal0 + jnp.sum(t0, axis=0, keepdims=True)
            dal1 = dal1 + jnp.sum(t1, axis=0, keepdims=True)
            dac0 = dac0 - t0 + jnp.where(rowl == L - 1, dal0, 0.0)
            dac1 = dac1 - t1 + jnp.where(rowl == L - 1, dal1, 0.0)
            dx = dx + dxd * dt_l
            q = dxd * x
            dst_ref[g * PP + j] = dprev
            dx_ref[:, sl] = dx.astype(dx_ref.dtype)
            dac_all = dac_all + jnp.where(lane == h0, dac0, 0.0) + jnp.where(lane == h1, dac1, 0.0)
            xds_all = (xds_all + jnp.where(lane == h0, rsum(jnp.where(lo, q, 0.0)), 0.0)
                       + jnp.where(lane == h1, rsum(jnp.where(lo, 0.0, q)), 0.0))
            dD_row = dD_row + jnp.where(lane1 == h0, dD0, 0.0) + jnp.where(lane1 == h1, dD1, 0.0)

        dGb = dG.astype(BF)
        dC_ref[...] = (dCacc + _dot_nn(dGb, Bb)).astype(dC_ref.dtype)
        dB_ref[...] = (dBacc + _dot_tn(dGb, Cb)).astype(dB_ref.dtype)
        row_i = lax.broadcasted_iota(jnp.int32, (L, L), 0)
        col_i = lax.broadcasted_iota(jnp.int32, (L, L), 1)
        da_all = _tri_matmul((row_i <= col_i).astype(BF), dac_all)
        mine = (lane >= g * (2 * PP)) & (lane < (g + 1) * (2 * PP))
        ddt_all = da_all * A_row + xds_all
        draw = jnp.where(mine, ddt_all * _sigmoid(pre), 0.0)

        @pl.when(g == 0)
        def _():
            ddt_ref[...] = draw

        @pl.when(g != 0)
        def _():
            ddt_ref[...] += draw

        dbias_row = jnp.sum(draw, axis=0, keepdims=True)
        dalog_row = jnp.sum(jnp.where(mine, da_all * a_all, 0.0), axis=0, keepdims=True)
        dpar_ref[0:1, :] += dbias_row
        dpar_ref[1:2, :] += dalog_row
        dpar_ref[2:3, :] += dD_row

    xspec = pl.BlockSpec((L, gw), lambda c, g: (nc - 1 - c, g))
    nspec = pl.BlockSpec((L, N), lambda c, g: (nc - 1 - c, g))
    return pl.pallas_call(
        body, name=name, grid=(nc, G),
        in_specs=[xspec, xspec, pl.BlockSpec((L, N), lambda c, g: (nc - 1 - c, boff + g)),
                  pl.BlockSpec((L, N), lambda c, g: (nc - 1 - c, boff + G + g)),
                  pl.BlockSpec((L, LANES), lambda c, g: (nc - 1 - c, 0)), pl.BlockSpec((None, 8, LANES), lambda c, g: (l, 0, 0)),
                  pl.BlockSpec((1, PP, LANES, N), lambda c, g: (nc - 1 - c, g, 0, 0))],
        out_specs=[xspec, nspec, nspec, pl.BlockSpec((L, LANES), lambda c, g: (nc - 1 - c, 0)),
                   pl.BlockSpec((8, LANES), lambda c, g: (0, 0))],
        out_shape=[jax.ShapeDtypeStruct((S, inner), BF), jax.ShapeDtypeStruct((S, G * N), BF), jax.ShapeDtypeStruct((S, G * N), BF),
                   jax.ShapeDtypeStruct((S, LANES), F32), jax.ShapeDtypeStruct((8, LANES), F32)],
        scratch_shapes=[pltpu.VMEM((G * PP, LANES, N), F32)],
        compiler_params=_cp(("arbitrary", "arbitrary")),
    )(dy, xbc, xbc, xbc, dt_raw, par, states)


def _adamw(g, w, m, v):
    m2 = ADAM_B1 * m + (1.0 - ADAM_B1) * g
    v2 = ADAM_B2 * v + (1.0 - ADAM_B2) * (g * g)
    m_hat = m2 / (1.0 - ADAM_B1 ** ADAM_STEP)
    v_hat = v2 / (1.0 - ADAM_B2 ** ADAM_STEP)
    delta = -ADAM_LR * (m_hat / (jnp.sqrt(v_hat) + ADAM_EPS) + ADAM_WD * w)
    return delta, m2, v2


def _flat_tile(R):
    return _pick(R, (FLAT_ROW_TILE, 1024, 512, 256, 128, 64, 32, 16, 8))


def _sum_adam(name, lands, off, w, m, v):
    depth, r, c = w.shape
    cap = max(16, (4 * 1024 * 1024) // (N_DEV * c * 2))
    tr = [t for t in (512, 256, 128, 64, 32, 16) if r % t == 0 and off % t == 0 and t <= cap][0]
    ob = off // tr

    def body(*refs):
        land_refs = refs[:depth]
        w_ref, m_ref, v_ref, g_ref, d_ref, m2_ref, v2_ref = refs[depth:]
        l = pl.program_id(0)
        for i in range(depth):
            @pl.when(l == i)
            def _(i=i):
                g = land_refs[i][0].astype(F32)
                for k in range(1, N_DEV):
                    g = g + land_refs[i][k].astype(F32)
                g_ref[...] = g
                d_ref[...], m2_ref[...], v2_ref[...] = _adamw(g, w_ref[...], m_ref[...], v_ref[...])

    spec = pl.BlockSpec((None, tr, c), lambda l, t: (l, t, 0))
    land_specs = [pl.BlockSpec((N_DEV, tr, c), lambda l, t, i=i: (0, jnp.where(l == i, ob + t, ob), 0)) for i in range(depth)]
    return pl.pallas_call(
        body, name=name, grid=(depth, r // tr),
        in_specs=land_specs + [spec, spec, spec],
        out_specs=[spec] * 4, out_shape=[jax.ShapeDtypeStruct((depth, r, c), F32)] * 4,
        compiler_params=_cp(("arbitrary", "arbitrary")),
    )(*lands, w, m, v)


def _sum8(name, parts):
    R = parts.shape[1]
    TR = _flat_tile(R)

    def body(p_ref, g_ref):
        g = p_ref[0]
        for k in range(1, N_DEV):
            g = g + p_ref[k]
        g_ref[...] = g

    return pl.pallas_call(
        body, name=name, grid=(R // TR,),
        in_specs=[pl.BlockSpec((N_DEV, TR, LANES), lambda i: (0, i, 0))],
        out_specs=pl.BlockSpec((TR, LANES), lambda i: (i, 0)), out_shape=jax.ShapeDtypeStruct((R, LANES), F32),
        compiler_params=_cp(("parallel",)),
    )(parts)


def _adam_flat(name, g, w, m, v):
    R = w.shape[0]
    TR = _flat_tile(R)

    def body(g_ref, w_ref, m_ref, v_ref, d_ref, m2_ref, v2_ref):
        d_ref[...], m2_ref[...], v2_ref[...] = _adamw(g_ref[...], w_ref[...], m_ref[...], v_ref[...])

    spec = pl.BlockSpec((TR, LANES), lambda i: (i, 0))
    return pl.pallas_call(
        body, name=name, grid=(R // TR,), in_specs=[spec] * 4, out_specs=[spec] * 3,
        out_shape=[jax.ShapeDtypeStruct((R, LANES), F32)] * 3, compiler_params=_cp(("parallel",)),
    )(g, w, m, v)


PART_ROWS = 16


def _nrows(shape):
    n = 1
    for s in shape:
        n *= s
    r = -(-n // LANES)
    return -(-r // PART_ROWS) * PART_ROWS


def _as_rows(a):
    n = a.size
    r = _nrows(a.shape)
    f = a.reshape(-1)
    if r * LANES != n:
        f = jnp.concatenate([f, jnp.zeros((r * LANES - n,), a.dtype)])
    return f.reshape(r, LANES)


def _pack(arrs, mult=PART_ROWS):
    cat = jnp.concatenate([_as_rows(a) for a in arrs], axis=0)
    pad = (-cat.shape[0]) % mult
    if pad:
        cat = jnp.concatenate([cat, jnp.zeros((pad, LANES), cat.dtype)], axis=0)
    return cat


def _unpack(flat, shapes):
    lead = flat.shape[:-2]
    out = []
    o = 0
    for shp in shapes:
        n = 1
        for s in shp:
            n *= s
        r = _nrows(shp)
        blk = flat[..., o:o + r, :].reshape(lead + (r * LANES,))
        out.append(blk[..., :n].reshape(lead + tuple(shp)))
        o += r
    return out


def _full_from_shards(st, kind):
    if kind == 'row':
        return st.reshape(st.shape[0] * st.shape[1], st.shape[2])
    return jnp.transpose(st, (1, 0, 2)).reshape(st.shape[1], st.shape[0] * st.shape[2])


def _shards_from_full(full, kind):
    if kind == 'row':
        return full.reshape(N_DEV, full.shape[0] // N_DEV, full.shape[1])
    return jnp.transpose(full.reshape(full.shape[0], N_DEV, full.shape[1] // N_DEV), (1, 0, 2))


def _ffn_fwd(tag, h, g, wgu, wd, dep=None):
    xn = _rms_fwd(tag + "_rms", h, g, dep=dep)
    ab = _mm(tag + "_up", xn, wgu)
    hmid = _swiglu_fwd(tag + "_act", ab)
    hout = _mm(tag + "_down", hmid, wd, out_dtype=F32, res=h, alpha=0.5)
    return hout, (xn, ab, hmid)


def _ffn_bwd(tag, dh_out, h, g, wgu, wd, saved, dep=None):
    xn, ab, hmid = saved
    dhmid = _mm(tag + "_d_hmid", dh_out, wd, tb=True, alpha=0.5, dep=dep)
    d_wd = _mm(tag + "_d_wd", hmid, dh_out, ta=True, alpha=0.5)
    dab = _swiglu_bwd(tag + "_d_act", dhmid, ab)
    d_wgu = _mm(tag + "_d_wgu", xn, dab, ta=True)
    dxn = _mm(tag + "_d_xn", dab, wgu, tb=True)
    dh, dg = _rms_bwd(tag + "_d_rms", dxn, h, g, dh_out)
    return dh, dg, d_wgu, d_wd


SEG_NAMES = ['scb', 'scc', 'scx', 'z', 'xbc', 'dt', 'ga', 'gm']
PERM = ['z', 'scb', 'scc', 'scx', 'ga', 'gm', 'xbc']


def _seg_layout(dims):
    D, inner, conv_dim, H = dims[:4]
    widths = dict(zip(SEG_NAMES, [D, D, D, inner, conv_dim, H, D, D]))
    offs, o = {}, 0
    for n in SEG_NAMES:
        offs[n] = (o, widths[n])
        o += widths[n]
    poffs, o = {}, 0
    for n in PERM:
        poffs[n] = (o, widths[n])
        o += widths[n]
    return offs, poffs


def _perm_w_in(w_in, dims):
    offs, _ = _seg_layout(dims)
    wp = jnp.concatenate([w_in[:, offs[n][0]:offs[n][0] + offs[n][1]] for n in PERM], axis=1)
    o, w = offs['dt']
    wdt = jnp.concatenate([w_in[:, o:o + w], jnp.zeros((w_in.shape[0], LANES - w), w_in.dtype)], axis=1)
    return wp, wdt


def _unperm_d_w_in(d_wp, d_wdt, dims):
    offs, poffs = _seg_layout(dims)
    H = dims[3]
    return jnp.concatenate([d_wdt[:, :H] if n == 'dt' else d_wp[:, poffs[n][0]:poffs[n][0] + poffs[n][1]] for n in SEG_NAMES], axis=1)


def _mixer_fwd(h, W, dims, dep=None):
    H, Ksc, Km = dims[3:]
    l = W['l']
    _, poffs = _seg_layout(dims)

    def seg(n):
        o, w = poffs[n]
        assert o % w == 0
        return (proj, w, o // w)

    u = _rms_fwd("mix_rms", h, W['mix_norm'], dep=dep)
    proj = _mm("inproj", u, W['w_in_p'])
    dt_raw = _mm("inproj_dt", u, W['w_dt'], out_dtype=F32)
    v = _scconv_fwd("scconv_f", proj, poffs['scb'][0], poffs['scc'][0], poffs['scx'][0], W['sc_taps'], Ksc, l)
    ya = _mm("sc_out", v, W['sc_w_out'])
    xbc = _mconv_fwd("mconv_f", proj, poffs['xbc'][0], W['m_taps'], Km, W['m_conv_b'], l)
    y, states = _ssd_fwd("ssd_f", xbc, dt_raw, W['ssd_par'], l, H)
    yn = _gnorm_fwd("gnorm_f", y, seg('z'), W['m_norm'])
    ym = _mm("m_out", yn, W['m_w_out'])
    merged = _merge_fwd("merge_f", seg('ga'), seg('gm'), ya, ym)
    hout = _mm("w_o", merged, W['w_o'], out_dtype=F32, res=h)
    return hout, (u, proj, dt_raw, v, ya, xbc, y, states, yn, ym, merged)


def _mixer_bwd(dh_out, h, W, dims, saved):
    u, proj, dt_raw, v, ya, xbc, y, states, yn, ym, merged = saved
    H, Ksc, Km = dims[3:]
    l = W['l']
    _, poffs = _seg_layout(dims)

    def seg(n):
        o, w = poffs[n]
        return (proj, w, o // w)

    g = {}
    dmerged = _mm("d_merged", dh_out, W['w_o'], tb=True)
    g['w_o'] = _mm("d_w_o", merged, dh_out, ta=True)
    dga, dgm, dya, dym = _merge_bwd("merge_b", dmerged, seg('ga'), seg('gm'), ya, ym)
    g['sc_w_out'] = _mm("d_sc_w_out", v, dya, ta=True)
    dv = _mm("d_v", dya, W['sc_w_out'], tb=True)
    g['m_w_out'] = _mm("d_m_w_out", yn, dym, ta=True)
    dyn = _mm("d_yn", dym, W['m_w_out'], tb=True)
    dy, dz, d_mnorm = _gnorm_bwd("gnorm_b", dyn, y, seg('z'), W['m_norm'])
    g['m_norm'] = d_mnorm.reshape(-1)
    dxs, dB, dC, ddt, dpar = _ssd_bwd("ssd_b", dy, xbc, dt_raw, states, W['ssd_par'], l, H)
    g['m_dt_bias'] = dpar[0, :H]
    g['m_A_log'] = dpar[1, :H]
    g['m_D'] = dpar[2, :H]
    dxbc_post = jnp.concatenate([dxs, dB, dC], axis=1)
    dxbc, d_mcw, d_mcb = _mconv_bwd("mconv_b", dxbc_post, proj, poffs['xbc'][0], W['m_taps'], Km, W['m_conv_b'], l)
    g['m_conv_w'] = d_mcw[:Km]
    g['m_conv_b'] = d_mcb.reshape(-1)
    dscb, dscc, dscx, d_scw = _scconv_bwd("scconv_b", dv, proj, poffs['scb'][0], poffs['scc'][0], poffs['scx'][0],
                                          W['sc_taps'], Ksc, l)
    g['sc_conv_w'] = d_scw[:Ksc]
    dproj = jnp.concatenate([dz, dscb, dscc, dscx, dga, dgm, dxbc], axis=1)
    du = _mm("d_u_main", dproj, W['w_in_p'], tb=True, out_dtype=F32)
    du = _mm("d_u_dt", ddt, W['w_dt'], tb=True, out_dtype=F32, res=du)
    d_wp = _mm("d_w_in_main", u, dproj, ta=True)
    d_wdt = _mm("d_w_in_dt", u, ddt, ta=True)
    g['w_in'] = _unperm_d_w_in(d_wp, d_wdt, dims)
    dh, dg = _rms_bwd("mix_d_rms", du, h, W['mix_norm'], dh_out)
    g['mix_norm'] = dg.reshape(-1)
    return dh, g


def _ple_layer_fwd(h, p_l, W):
    xn = _rms_fwd("ple_rms", h, W['ple_norm'])
    gpre = _mm("ple_gate", xn, W['ple_w_gate'])
    pp = _mm("ple_proj", p_l, W['ple_w_proj'])
    hout = _ple_fwd("ple_f", h, gpre, pp)
    return hout, (xn, gpre, pp)


def _ple_layer_bwd(dh_out, h, p_l, W, saved, dep=None):
    xn, gpre, pp = saved
    g = {}
    dgpre, dpp = _ple_bwd("ple_b", dh_out, gpre, pp, dep=dep)
    g['ple_w_proj'] = _mm("d_ple_proj", p_l, dpp, ta=True)
    g['ple_w_gate'] = _mm("d_ple_gate", xn, dgpre, ta=True)
    dxn = _mm("d_ple_xn", dgpre, W['ple_w_gate'], tb=True)
    dh, dg = _rms_bwd("ple_d_rms", dxn, h, W['ple_norm'], dh_out)
    g['ple_norm'] = dg.reshape(-1)
    return dh, g


def kernel(x, p, ffn1_norm, ffn1_wg, ffn1_wu, ffn1_wd, mix_norm, w_in, sc_conv_w, sc_w_out, m_conv_w, m_conv_b, m_dt_bias, m_A_log, m_D, m_norm, m_w_out, w_o, ffn2_norm, ffn2_wg, ffn2_wu, ffn2_wd, ple_norm, ple_w_gate, ple_w_proj, final_norm, loss_target, m_ffn1_norm, m_ffn1_wg, m_ffn1_wu, m_ffn1_wd, m_mix_norm, m_w_in, m_sc_conv_w, m_sc_w_out, m_m_conv_w, m_m_conv_b, m_m_dt_bias, m_m_A_log, m_m_D, m_m_norm, m_m_w_out, m_w_o, m_ffn2_norm, m_ffn2_wg, m_ffn2_wu, m_ffn2_wd, m_ple_norm, m_ple_w_gate, m_ple_w_proj, m_final_norm, v_ffn1_norm, v_ffn1_wg, v_ffn1_wu, v_ffn1_wd, v_mix_norm, v_w_in, v_sc_conv_w, v_sc_w_out, v_m_conv_w, v_m_conv_b, v_m_dt_bias, v_m_A_log, v_m_D, v_m_norm, v_m_w_out, v_w_o, v_ffn2_norm, v_ffn2_wg, v_ffn2_wu, v_ffn2_wd, v_ple_norm, v_ple_w_gate, v_ple_w_proj, v_final_norm):
    args = (x, p, ffn1_norm, ffn1_wg, ffn1_wu, ffn1_wd, mix_norm, w_in, sc_conv_w, sc_w_out, m_conv_w, m_conv_b, m_dt_bias, m_A_log, m_D, m_norm, m_w_out, w_o, ffn2_norm, ffn2_wg, ffn2_wu, ffn2_wd, ple_norm, ple_w_gate, ple_w_proj, final_norm, loss_target, m_ffn1_norm, m_ffn1_wg, m_ffn1_wu, m_ffn1_wd, m_mix_norm, m_w_in, m_sc_conv_w, m_sc_w_out, m_m_conv_w, m_m_conv_b, m_m_dt_bias, m_m_A_log, m_m_D, m_m_norm, m_m_w_out, m_w_o, m_ffn2_norm, m_ffn2_wg, m_ffn2_wu, m_ffn2_wd, m_ple_norm, m_ple_w_gate, m_ple_w_proj, m_final_norm, v_ffn1_norm, v_ffn1_wg, v_ffn1_wu, v_ffn1_wd, v_mix_norm, v_w_in, v_sc_conv_w, v_sc_w_out, v_m_conv_w, v_m_conv_b, v_m_dt_bias, v_m_A_log, v_m_D, v_m_norm, v_m_w_out, v_w_o, v_ffn2_norm, v_ffn2_wg, v_ffn2_wu, v_ffn2_wd, v_ple_norm, v_ple_w_gate, v_ple_w_proj, v_final_norm)
    names = ARG_NAMES + ['m_' + n for n in WEIGHTS] + ['v_' + n for n in WEIGHTS]
    A = dict(zip(names, args))
    depth = ffn1_norm.shape[0]
    me = 4 * lax.axis_index("x") + 2 * lax.axis_index("y") + lax.axis_index("c")

    dims = (x.shape[-1], m_norm.shape[1], m_conv_b.shape[1], m_dt_bias.shape[1], sc_conv_w.shape[1], m_conv_w.shape[1])
    kind = dict(BIG)
    dev = lax.broadcasted_iota(jnp.int32, (N_DEV, 1, 1), 0)

    wb = {n: A[n].astype(BF) for n, _ in BIG}
    srcs = [[wb[ms[0]] if len(ms) == 1 else jnp.concatenate([wb[n] for n in ms], axis=1) for ms in stage] for stage in STAGES]
    conv_g = _unpack(_exchange("gather_conv_taps", _pack([A[n] for n in CONVW]), True), [A[n].shape for n in CONVW])
    taps = {}
    for n, st in zip(CONVW, conv_g):
        taps[n] = _pad_taps(jnp.transpose(st, (1, 2, 0, 3)).reshape(depth, st.shape[2], N_DEV * st.shape[3]))
    ssd_par = _ssd_params(m_dt_bias, m_A_log, m_D)
    small3 = {n: A[n].reshape(depth, 1, -1) for n in SMALL}

    def stage_weights(W, s, l, lands):
        for ms, land, src in zip(STAGES[s], lands, srcs[s]):
            off = 0
            for n in ms:
                r = A[n].shape[1]
                st = jnp.where(dev == me, src[l, off:off + r][None], land[:, off:off + r])
                W[n] = _full_from_shards(st, kind[n])
                off += r
        if s == 0:
            W['ffn1_wgu'] = jnp.concatenate([W.pop('ffn1_wg'), W.pop('ffn1_wu')], axis=1)
        else:
            W['ffn2_wgu'] = jnp.concatenate([W.pop('ffn2_wg'), W.pop('ffn2_wu')], axis=1)
            W['w_in_p'], W['w_dt'] = _perm_w_in(W.pop('w_in'), dims)

    flight = [None, None]
    sems, lands, tok = _xchg_begin("gather_begin0a", srcs[0], 0, taps['sc_conv_w'])
    flight[0] = (sems, lands)
    sems, lands, tok = _xchg_begin("gather_begin0b", srcs[1], 0, tok)
    flight[1] = (sems, lands)
    lands_a = _xchg_end("gather_end0a", srcs[0], flight[0][1], flight[0][0], 0, tok)
    h = x[0]
    saved = []
    layers = []
    for l in range(depth):
        W = {n: (small3[n], l) for n in SMALL}
        W.update(l=l, sc_taps=taps['sc_conv_w'], m_taps=taps['m_conv_w'], m_conv_b=small3['m_conv_b'], ssd_par=ssd_par)
        layers.append(W)
        stage_weights(W, 0, l, lands_a)
        h1, s1 = _ffn_fwd("ffn1", h, W['ffn1_norm'], W['ffn1_wgu'], W['ffn1_wd'])
        lands_b = _xchg_end(f"gather_end{l}b", srcs[1], flight[1][1], flight[1][0], l, h1)
        stage_weights(W, 1, l, lands_b)
        tok = None
        if l + 1 < depth:
            sems, lands, tok = _xchg_begin(f"gather_begin{l + 1}a", srcs[0], l + 1, lands_b[0])
            flight[0] = (sems, lands)
            sems, lands, tok = _xchg_begin(f"gather_begin{l + 1}b", srcs[1], l + 1, tok)
            flight[1] = (sems, lands)
        h2, s2 = _mixer_fwd(h1, W, dims, dep=tok)
        h3, s3 = _ffn_fwd("ffn2", h2, W['ffn2_norm'], W['ffn2_wgu'], W['ffn2_wd'])
        h4, s4 = _ple_layer_fwd(h3, p[l, 0], W)
        saved.append((h, h1, h2, h3, s1, s2, s3, s4))
        h = h4
        if l + 1 < depth:
            lands_a = _xchg_end(f"gather_end{l + 1}a", srcs[0], flight[0][1], flight[0][0], l + 1, h)

    dh, loss_row, d_final = _loss_head("loss_head", h, final_norm, loss_target[0])
    loss = lax.psum(loss_row[0, 0], ("x", "y", "c"))

    def send_bufs(g, s):
        return [jnp.concatenate([_shards_from_full(g[n], kind[n]) for n in ms], axis=1) if len(ms) > 1
                else _shards_from_full(g[ms[0]], kind[ms[0]]) for ms in STAGES[s]]

    grads = [None] * depth
    pending = []
    tok = loss.reshape(1, 1)
    for l in reversed(range(depth)):
        W = layers[l]
        h0, h1, h2, h3, s1, s2, s3, s4 = saved[l]
        g = {}
        dh, g4 = _ple_layer_bwd(dh, h3, p[l, 0], W, s4, dep=tok)
        g.update(g4)
        dh, dg, d_wgu, d_wd = _ffn_bwd("ffn2", dh, h2, W['ffn2_norm'], W['ffn2_wgu'], W['ffn2_wd'], s3)
        ff = d_wgu.shape[1] // 2
        g.update(ffn2_norm=dg.reshape(-1), ffn2_wg=d_wgu[:, :ff], ffn2_wu=d_wgu[:, ff:], ffn2_wd=d_wd)
        dh, g2 = _mixer_bwd(dh, h1, W, dims, s2)
        g.update(g2)
        send = send_bufs(g, 1)
        sems, lands, tok = _xchg_begin(f"scatter_begin{l}b", send, None, dh)
        pending.append((l, 1, send, lands, sems))
        dh, dg, d_wgu, d_wd = _ffn_bwd("ffn1", dh, h0, W['ffn1_norm'], W['ffn1_wgu'], W['ffn1_wd'], s1, dep=tok)
        g.update(ffn1_norm=dg.reshape(-1), ffn1_wg=d_wgu[:, :ff], ffn1_wu=d_wgu[:, ff:], ffn1_wd=d_wd)
        grads[l] = g
        send = send_bufs(g, 0)
        sems, lands, tok = _xchg_begin(f"scatter_begin{l}a", send, None, dh)
        pending.append((l, 0, send, lands, sems))
    grad_x = dh[None]

    g_lands = [[None, None] for _ in range(depth)]
    after = dh
    for l, s, send, lands, sems in pending:
        got = _xchg_end(f"scatter_end{l}{'ab'[s]}", send, lands, sems, None, after)
        after = got[0]
        g_lands[l][s] = [lax.dynamic_update_slice(o, lax.dynamic_slice_in_dim(b, me, 1, axis=0), (me, 0, 0)) for o, b in zip(got, send)]

    big_res = [{}, {}, {}, {}]
    for s, stage in enumerate(STAGES):
        for gi, ms in enumerate(stage):
            off = 0
            for n in ms:
                res = _sum_adam("adamw_" + n, [g_lands[l][s][gi] for l in range(depth)], off, A[n], A['m_' + n], A['v_' + n])
                for k in range(4):
                    big_res[k][n] = res[k]
                off += A[n].shape[1]

    small_names = SMALL + CONVW
    small_parts = [jnp.stack([grads[l][n] for l in range(depth)]) for n in small_names] + [d_final.reshape(-1)]
    small_sum = _sum8("sum_small", _exchange("gather_small_grads", _pack(small_parts), True, dep=after))
    sg = dict(zip(small_names + ['final_norm'], _unpack(small_sum, [a.shape for a in small_parts])))
    for n in CONVW:
        c = A[n].shape[-1]
        sg[n] = lax.dynamic_slice_in_dim(sg[n], me * c, c, axis=2)
    s_order = small_names + ['final_norm']
    s_shapes = [sg[n].shape for n in s_order]
    s_out = _adam_flat("adamw_small", _pack([sg[n] for n in s_order]), _pack([A[n] for n in s_order]),
                       _pack([A['m_' + n] for n in s_order]), _pack([A['v_' + n] for n in s_order]))
    small_res = [sg] + [dict(zip(s_order, _unpack(flat, s_shapes))) for flat in s_out]

    outs = [loss, grad_x]
    for k in range(4):
        for n in WEIGHTS:
            outs.append(big_res[k][n] if n in big_res[k] else small_res[k][n])
    return tuple(outs)
```

```python
import functools

import jax
import jax.numpy as jnp
from jax import lax
from jax.experimental import pallas as pl
from jax.experimental.pallas import tpu as pltpu

BF = jnp.bfloat16
F32 = jnp.float32

EPS = 1e-6
N_DEV = 8
LANES = 128
SSM_GROUPS = 4
SSM_HEADDIM = 64
SSM_CHUNK = 128
HALO = 16
VMEM_LIMIT = 56 * 1024 * 1024
FLAT_ROW_TILE = 2048

ADAM_LR = 0.001
ADAM_B1 = 0.9
ADAM_B2 = 0.999
ADAM_EPS = 1e-08
ADAM_WD = 0.01
ADAM_STEP = 10

MESH = pl.DeviceIdType.MESH

ARG_NAMES = ['x', 'p', 'ffn1_norm', 'ffn1_wg', 'ffn1_wu', 'ffn1_wd', 'mix_norm', 'w_in', 'sc_conv_w', 'sc_w_out', 'm_conv_w', 'm_conv_b', 'm_dt_bias', 'm_A_log', 'm_D', 'm_norm', 'm_w_out', 'w_o', 'ffn2_norm', 'ffn2_wg', 'ffn2_wu', 'ffn2_wd', 'ple_norm', 'ple_w_gate', 'ple_w_proj', 'final_norm', 'loss_target']
WEIGHTS = ARG_NAMES[2:26]
BIG = [('ffn1_wg', 'col'), ('ffn1_wu', 'col'), ('ffn1_wd', 'row'), ('w_in', 'col'), ('sc_w_out', 'row'),
       ('m_w_out', 'row'), ('w_o', 'row'), ('ffn2_wg', 'col'), ('ffn2_wu', 'col'), ('ffn2_wd', 'row'),
       ('ple_w_gate', 'row'), ('ple_w_proj', 'col')]
CONVW = ['sc_conv_w', 'm_conv_w']
SMALL = ['ffn1_norm', 'mix_norm', 'm_conv_b', 'm_dt_bias', 'm_A_log', 'm_D', 'm_norm', 'ffn2_norm', 'ple_norm']


def _pick(n, cands):
    for c in cands:
        if n % c == 0:
            return c
    return n


def _cp(sem):
    return pltpu.CompilerParams(dimension_semantics=sem, vmem_limit_bytes=VMEM_LIMIT)


def _sigmoid(x):
    return 1.0 / (1.0 + jnp.exp(-x))


def _softplus(x):
    return jnp.maximum(x, 0.0) + jnp.log(1.0 + jnp.exp(-jnp.abs(x)))


def _exchange(name, x, gather, dep=None):
    slab = x.shape if gather else x.shape[1:]

    def body(x_ref, *rest):
        o_ref, send_sems, recv_sems, local_sem = rest[-4:]
        mx, my, mc = lax.axis_index("x"), lax.axis_index("y"), lax.axis_index("c")
        me = 4 * mx + 2 * my + mc

        def src_for(k):
            return x_ref if gather else x_ref.at[k]

        local = pltpu.make_async_copy(src_for(me), o_ref.at[me], local_sem)
        local.start()
        sends = []
        peers = []
        for r in range(1, N_DEV):
            px = (mx + ((r >> 2) & 1)) % 2
            py = (my + ((r >> 1) & 1)) % 2
            pc = (mc + (r & 1)) % 2
            peer = 4 * px + 2 * py + pc
            peers.append(peer)
            cp = pltpu.make_async_remote_copy(
                src_ref=src_for(peer), dst_ref=o_ref.at[me], send_sem=send_sems.at[r - 1], recv_sem=recv_sems.at[r - 1],
                device_id=(px, py, pc), device_id_type=MESH)
            cp.start()
            sends.append(cp)
        for r in range(1, N_DEV):
            peer = peers[r - 1]
            pltpu.make_async_remote_copy(
                src_ref=src_for(peer), dst_ref=o_ref.at[peer], send_sem=send_sems.at[r - 1], recv_sem=recv_sems.at[r - 1],
                device_id=(mx, my, mc), device_id_type=MESH).wait_recv()
        for cp in sends:
            cp.wait_send()
        local.wait()

    return pl.pallas_call(
        body, name=name,
        out_shape=jax.ShapeDtypeStruct((N_DEV,) + tuple(slab), x.dtype),
        in_specs=[pl.BlockSpec(memory_space=pltpu.HBM)] + ([] if dep is None else [pl.BlockSpec(memory_space=pl.ANY)]),
        out_specs=pl.BlockSpec(memory_space=pltpu.HBM),
        scratch_shapes=[pltpu.SemaphoreType.DMA((N_DEV - 1,)), pltpu.SemaphoreType.DMA((N_DEV - 1,)), pltpu.SemaphoreType.DMA],
    )(*([x] if dep is None else [x, dep]))


STAGES = [[['ffn1_wd'], ['ffn1_wg'], ['ffn1_wu']],
          [['ffn2_wd'], ['ffn2_wg'], ['ffn2_wu'], ['sc_w_out', 'w_o', 'ple_w_gate', 'm_w_out'], ['w_in'], ['ple_w_proj']]]
_HBM = pl.BlockSpec(memory_space=pltpu.HBM)
_SEM = pl.BlockSpec(memory_space=pltpu.SEMAPHORE)
_ANY = pl.BlockSpec(memory_space=pl.ANY)
_EFFECT = pltpu.SideEffectType.DATAFLOW_SIDE_EFFECTING


def _peer_list():
    mx, my, mc = lax.axis_index("x"), lax.axis_index("y"), lax.axis_index("c")
    out = []
    for r in range(1, N_DEV):
        px = (mx + ((r >> 2) & 1)) % 2
        py = (my + ((r >> 1) & 1)) % 2
        pc = (mc + (r & 1)) % 2
        out.append((px, py, pc, 4 * px + 2 * py + pc))
    return 4 * mx + 2 * my + mc, out


def _xchg_copy(src_refs, land_refs, send_sems, recv_sems, layer, i, r, peer, dst_slab):
    px, py, pc, pidx = peer
    n = len(src_refs)
    src = src_refs[i].at[layer] if layer is not None else src_refs[i].at[pidx]
    return pltpu.make_async_remote_copy(
        src_ref=src, dst_ref=land_refs[i].at[dst_slab], send_sem=send_sems.at[r * n + i], recv_sem=recv_sems.at[r * n + i],
        device_id=(px, py, pc), device_id_type=MESH)


def _xchg_begin(name, srcs, layer, dep):
    n = len(srcs)
    slabs = [tuple(s.shape[1:]) for s in srcs]
    ncp = n * (N_DEV - 1)

    def body(*refs):
        src_refs, land_refs = refs[:n], refs[n:2 * n]
        send_sems, recv_sems = refs[2 * n + 1], refs[2 * n + 2]
        token = refs[-1]
        me, peers = _peer_list()
        for r, peer in enumerate(peers):
            for i in range(n):
                _xchg_copy(src_refs, land_refs, send_sems, recv_sems, layer, i, r, peer, me).start()
        token[...] = jnp.zeros_like(token)

    lands = [pltpu.with_memory_space_constraint(lax.empty((N_DEV,) + sl, s.dtype), pltpu.HBM) for sl, s in zip(slabs, srcs)]
    out = pl.pallas_call(
        body, name=name,
        out_shape=(pltpu.SemaphoreType.DMA((ncp,)), pltpu.SemaphoreType.DMA((ncp,)),
                   *[pltpu.HBM((N_DEV,) + sl, s.dtype) for sl, s in zip(slabs, srcs)], jax.ShapeDtypeStruct((8, LANES), F32)),
        in_specs=[_HBM] * (2 * n) + [_ANY],
        out_specs=(_SEM, _SEM, *[_HBM] * n, pl.BlockSpec(memory_space=pltpu.VMEM)),
        input_output_aliases={n + i: 2 + i for i in range(n)},
        compiler_params=pltpu.CompilerParams(has_side_effects=_EFFECT),
    )(*[pltpu.with_memory_space_constraint(s, pltpu.HBM) for s in srcs], *lands, dep)
    return (out[0], out[1]), list(out[2:2 + n]), out[-1]


def _xchg_end(name, srcs, lands, sems, layer, after):
    n = len(srcs)

    def body(*refs):
        src_refs, land_refs = refs[:n], refs[n:2 * n]
        send_sems, recv_sems = refs[2 * n], refs[2 * n + 1]
        me, peers = _peer_list()
        for r, peer in enumerate(peers):
            for i in range(n):
                cp = _xchg_copy(src_refs, land_refs, send_sems, recv_sems, layer, i, r, peer, peer[3])
                cp.wait_send()
                cp.wait_recv()

    out = pl.pallas_call(
        body, name=name,
        out_shape=tuple(pltpu.HBM(l.shape, l.dtype) for l in lands),
        in_specs=[_HBM] * (2 * n) + [_SEM, _SEM, _ANY], out_specs=tuple([_HBM] * n),
        input_output_aliases={n + i: i for i in range(n)},
        compiler_params=pltpu.CompilerParams(has_side_effects=_EFFECT),
    )(*[pltpu.with_memory_space_constraint(s, pltpu.HBM) for s in srcs], *lands, sems[0], sems[1], after)
    return list(out)


MM_TILES = (1024, 1408, 512, 256, 128)
MM_OPERAND_BYTES = 24 * 1024 * 1024


def _mm(name, a, b, *, ta=False, tb=False, out_dtype=None, res=None, alpha=1.0, dep=None):
    out_dtype = out_dtype or BF
    M, K = (a.shape[1], a.shape[0]) if ta else a.shape
    N = b.shape[0] if tb else b.shape[1]
    assert (b.shape[1] if tb else b.shape[0]) == K, (name, a.shape, b.shape)
    tm = _pick(M, MM_TILES)
    tn = _pick(N, MM_TILES)
    per_k = 2 * (tm * a.dtype.itemsize + tn * b.dtype.itemsize)
    tk = [t for t in sorted({K, 4096, 2816, 2560, 2048, 1408, 1024, 512, 256, 128}, reverse=True)
          if K % t == 0 and (t * per_k <= MM_OPERAND_BYTES or t == 128)][0]
    nk = K // tk
    a_spec = pl.BlockSpec((tk, tm), lambda i, j, k: (k, i)) if ta else pl.BlockSpec((tm, tk), lambda i, j, k: (i, k))
    b_spec = pl.BlockSpec((tn, tk), lambda i, j, k: (j, k)) if tb else pl.BlockSpec((tk, tn), lambda i, j, k: (k, j))
    dn = (((0 if ta else 1,), (1 if tb else 0,)), ((), ()))
    has_res = res is not None
    n_dep = 0 if dep is None else 1

    def body(*refs):
        a_ref, b_ref = refs[:2]
        r_ref = refs[2] if has_res else None
        o_ref = refs[2 + has_res + n_dep]

        def finish(v):
            if alpha != 1.0:
                v = v * alpha
            if has_res:
                v = r_ref[...] + v
            o_ref[...] = v.astype(o_ref.dtype)

        part = lax.dot_general(a_ref[...].astype(BF), b_ref[...].astype(BF), dn, preferred_element_type=F32)
        if nk == 1:
            finish(part)
            return
        acc = refs[-1]
        k = pl.program_id(2)

        @pl.when(k == 0)
        def _():
            acc[...] = part

        @pl.when((k > 0) & (k < nk - 1))
        def _():
            acc[...] += part

        @pl.when(k == nk - 1)
        def _():
            finish(acc[...] + part)

    in_specs = [a_spec, b_spec]
    args = [a, b]
    if has_res:
        in_specs.append(pl.BlockSpec((tm, tn), lambda i, j, k: (i, j)))
        args.append(res)
    if dep is not None:
        in_specs.append(_ANY)
        args.append(dep)
    return pl.pallas_call(
        body, name=name, grid=(M // tm, N // tn, nk),
        in_specs=in_specs, out_specs=pl.BlockSpec((tm, tn), lambda i, j, k: (i, j)),
        out_shape=jax.ShapeDtypeStruct((M, N), out_dtype),
        scratch_shapes=[pltpu.VMEM((tm, tn), F32)] if nk > 1 else [],
        compiler_params=_cp(("parallel", "parallel", "arbitrary")),
    )(*args)


def _ew(name, fn, tiled, params, outs, accs=(), tile=256, dep=None):
    tiled = [t if isinstance(t, tuple) else (t, t.shape[1], 0) for t in tiled]
    params = [q if isinstance(q, tuple) else (q, None) for q in params]
    S = tiled[0][0].shape[0]
    T = _pick(S, (tile, 128, 64, 32, 16))
    n_in = len(tiled) + len(params)
    n_dep = 0 if dep is None else 1

    def body(*refs):
        fn(pl.program_id(0) == 0, *refs[:n_in], *refs[n_in + n_dep:])

    in_specs = [pl.BlockSpec((T, w), lambda i, cb=cb: (i, cb)) for _, w, cb in tiled]
    for q, row in params:
        if row is None:
            in_specs.append(pl.BlockSpec(q.shape, lambda i: (0, 0)))
        else:
            in_specs.append(pl.BlockSpec((None, 1, q.shape[2]), lambda i, row=row: (row, 0, 0)))
    args = [t[0] for t in tiled] + [q[0] for q in params]
    if dep is not None:
        in_specs.append(pl.BlockSpec(memory_space=pl.ANY))
        args.append(dep)
    out_specs = [pl.BlockSpec((T, w), lambda i: (i, 0)) for w, _ in outs]
    out_specs += [pl.BlockSpec(shp, lambda i: (0, 0)) for shp, _ in accs]
    out_shape = [jax.ShapeDtypeStruct((S, w), dt) for w, dt in outs]
    out_shape += [jax.ShapeDtypeStruct(shp, dt) for shp, dt in accs]
    res = pl.pallas_call(
        body, name=name, grid=(S // T,), in_specs=in_specs, out_specs=out_specs, out_shape=out_shape,
        compiler_params=_cp(("arbitrary",)),
    )(*args)
    return res


def _prow(g):
    return g if isinstance(g, tuple) else g.reshape(1, -1)


def _rms_fwd(name, h, g, dep=None):
    def fn(first, h_ref, g_ref, o_ref):
        x = h_ref[...]
        r = lax.rsqrt(jnp.mean(x * x, axis=-1, keepdims=True) + EPS)
        o_ref[...] = (x * r * g_ref[...]).astype(o_ref.dtype)

    return _ew(name, fn, [h], [_prow(g)], [(h.shape[1], BF)], dep=dep)[0]


def _rms_bwd(name, dxn, h, g, res):
    D = h.shape[1]

    def fn(first, d_ref, h_ref, r_ref, g_ref, o_ref, dg_ref):
        x = h_ref[...]
        d = d_ref[...].astype(F32)
        r = lax.rsqrt(jnp.mean(x * x, axis=-1, keepdims=True) + EPS)
        xhat = x * r
        dxhat = d * g_ref[...]
        dh = r * (dxhat - xhat * jnp.mean(dxhat * xhat, axis=-1, keepdims=True))
        o_ref[...] = r_ref[...] + dh

        @pl.when(first)
        def _():
            dg_ref[...] = jnp.zeros_like(dg_ref)

        dg_ref[...] += jnp.sum(d * xhat, axis=0, keepdims=True)

    return _ew(name, fn, [dxn, h, res], [_prow(g)], [(D, F32)], [((1, D), F32)])


FFN_TOKEN_TILE = 512


def _ffn_up(name, xn, wgT, wuT, dep=None):
    S, D = xn.shape
    FF = wgT.shape[0]
    tm = _pick(S, (FFN_TOKEN_TILE, 256, 128))
    tn = _pick(FF, MM_TILES)
    n_dep = 0 if dep is None else 1

    def body(x_ref, g_ref, u_ref, *rest):
        a_ref, b_ref, h_ref = rest[n_dep:]
        x = x_ref[...]
        a = _dot_nt(x, g_ref[...])
        b = _dot_nt(x, u_ref[...])
        a_ref[...] = a.astype(BF)
        b_ref[...] = b.astype(BF)
        h_ref[...] = (a * _sigmoid(a) * b).astype(BF)

    wspec = pl.BlockSpec((tn, D), lambda j, i: (j, 0))
    ospec = pl.BlockSpec((tm, tn), lambda j, i: (i, j))
    return pl.pallas_call(
        body, name=name, grid=(FF // tn, S // tm),
        in_specs=[pl.BlockSpec((tm, D), lambda j, i: (i, 0)), wspec, wspec] + ([] if dep is None else [_ANY]),
        out_specs=[ospec] * 3, out_shape=[jax.ShapeDtypeStruct((S, FF), BF)] * 3,
        compiler_params=_cp(("parallel", "arbitrary")),
    )(*([xn, wgT, wuT] + ([] if dep is None else [dep])))


def _ffn_dact(name, dh, wd, a, b, dep=None):
    S, D = dh.shape
    FF = wd.shape[0]
    tm = _pick(S, (FFN_TOKEN_TILE, 256, 128))
    tn = _pick(FF, MM_TILES)
    n_dep = 0 if dep is None else 1

    def body(d_ref, w_ref, a_ref, b_ref, *rest):
        da_ref, db_ref = rest[n_dep:]
        d = 0.5 * _dot_nt(d_ref[...].astype(BF), w_ref[...])
        av = a_ref[...].astype(F32)
        s = _sigmoid(av)
        da_ref[...] = (d * b_ref[...].astype(F32) * (s * (1.0 + av * (1.0 - s)))).astype(BF)
        db_ref[...] = (d * av * s).astype(BF)

    tspec = pl.BlockSpec((tm, tn), lambda j, i: (i, j))
    return pl.pallas_call(
        body, name=name, grid=(FF // tn, S // tm),
        in_specs=[pl.BlockSpec((tm, D), lambda j, i: (i, 0)), pl.BlockSpec((tn, D), lambda j, i: (j, 0)), tspec, tspec]
        + ([] if dep is None else [_ANY]),
        out_specs=[tspec] * 2, out_shape=[jax.ShapeDtypeStruct((S, FF), BF)] * 2,
        compiler_params=_cp(("parallel", "arbitrary")),
    )(*([dh, wd, a, b] + ([] if dep is None else [dep])))


def _merge_fwd(name, ga, gm, ya, ym):
    def fn(first, ga_ref, gm_ref, ya_ref, ym_ref, o_ref):
        o = _sigmoid(ga_ref[...].astype(F32)) * ya_ref[...].astype(F32) + _sigmoid(gm_ref[...].astype(F32)) * ym_ref[...].astype(F32)
        o_ref[...] = o.astype(o_ref.dtype)

    return _ew(name, fn, [ga, gm, ya, ym], [], [(ya.shape[1], BF)])[0]


def _merge_bwd(name, dmerged, ga, gm, ya, ym):
    W = ya.shape[1]

    def fn(first, d_ref, ga_ref, gm_ref, ya_ref, ym_ref, dga_ref, dgm_ref, dya_ref, dym_ref):
        d = d_ref[...].astype(F32)
        sa = _sigmoid(ga_ref[...].astype(F32))
        sm = _sigmoid(gm_ref[...].astype(F32))
        dga_ref[...] = (d * ya_ref[...].astype(F32) * sa * (1.0 - sa)).astype(BF)
        dgm_ref[...] = (d * ym_ref[...].astype(F32) * sm * (1.0 - sm)).astype(BF)
        dya_ref[...] = (d * sa).astype(BF)
        dym_ref[...] = (d * sm).astype(BF)

    return _ew(name, fn, [dmerged, ga, gm, ya, ym], [], [(W, BF)] * 4)


def _gnorm_fwd(name, y, z, w):
    W = y.shape[1]
    gw = W // SSM_GROUPS

    def fn(first, y_ref, z_ref, w_ref, o_ref):
        for g in range(SSM_GROUPS):
            sl = slice(g * gw, (g + 1) * gw)
            zz = z_ref[:, sl].astype(F32)
            t = y_ref[:, sl].astype(F32) * (zz * _sigmoid(zz))
            r = lax.rsqrt(jnp.mean(t * t, axis=-1, keepdims=True) + EPS)
            o_ref[:, sl] = (t * r * w_ref[:, sl]).astype(o_ref.dtype)

    return _ew(name, fn, [y, z], [_prow(w)], [(W, BF)])[0]


def _gnorm_bwd(name, dyn, y, z, w):
    W = y.shape[1]
    gw = W // SSM_GROUPS

    def fn(first, d_ref, y_ref, z_ref, w_ref, dy_ref, dz_ref, dw_ref):
        @pl.when(first)
        def _():
            dw_ref[...] = jnp.zeros_like(dw_ref)

        for g in range(SSM_GROUPS):
            sl = slice(g * gw, (g + 1) * gw)
            zz = z_ref[:, sl].astype(F32)
            yy = y_ref[:, sl].astype(F32)
            d = d_ref[:, sl].astype(F32)
            s = _sigmoid(zz)
            sz = zz * s
            t = yy * sz
            r = lax.rsqrt(jnp.mean(t * t, axis=-1, keepdims=True) + EPS)
            that = t * r
            dthat = d * w_ref[:, sl]
            dt = r * (dthat - that * jnp.mean(dthat * that, axis=-1, keepdims=True))
            dw_ref[:, sl] += jnp.sum(d * that, axis=0, keepdims=True)
            dy_ref[:, sl] = (dt * sz).astype(BF)
            dz_ref[:, sl] = (dt * yy * (s * (1.0 + zz * (1.0 - s)))).astype(BF)

    return _ew(name, fn, [dyn, y, z], [_prow(w)], [(W, BF), (W, BF)], [((1, W), F32)])


def _ple_fwd(name, h, gpre, pp):
    def fn(first, h_ref, g_ref, p_ref, o_ref):
        o_ref[...] = h_ref[...] + _sigmoid(g_ref[...].astype(F32)) * p_ref[...].astype(F32)

    return _ew(name, fn, [h, gpre, pp], [], [(h.shape[1], F32)])[0]


def _ple_bwd(name, dh, gpre, pp, dep=None):
    W = dh.shape[1]

    def fn(first, d_ref, g_ref, p_ref, dg_ref, dp_ref):
        d = d_ref[...]
        s = _sigmoid(g_ref[...].astype(F32))
        dg_ref[...] = (d * p_ref[...].astype(F32) * s * (1.0 - s)).astype(BF)
        dp_ref[...] = (d * s).astype(BF)

    return _ew(name, fn, [dh, gpre, pp], [], [(W, BF), (W, BF)], dep=dep)


def _loss_head(name, h, g, target):
    D = h.shape[1]

    def fn(first, h_ref, t_ref, g_ref, dh_ref, loss_ref, dg_ref):
        x = h_ref[...]
        r = lax.rsqrt(jnp.mean(x * x, axis=-1, keepdims=True) + EPS)
        xhat = x * r
        err = xhat * g_ref[...] - t_ref[...]
        part = 0.5 * jnp.sum(jnp.mean(err * err, axis=-1, keepdims=True), axis=0, keepdims=True)
        dy = err * (1.0 / D)
        dxhat = dy * g_ref[...]
        dh_ref[...] = r * (dxhat - xhat * jnp.mean(dxhat * xhat, axis=-1, keepdims=True))

        @pl.when(first)
        def _():
            loss_ref[...] = jnp.zeros_like(loss_ref)
            dg_ref[...] = jnp.zeros_like(dg_ref)

        loss_ref[...] += jnp.broadcast_to(part, loss_ref.shape)
        dg_ref[...] += jnp.sum(dy * xhat, axis=0, keepdims=True)

    return _ew(name, fn, [h, target], [_prow(g)], [(D, F32)], [((1, LANES), F32), ((1, D), F32)])


def _conv_specs(S, C, offs, l):
    T = _pick(S, (512, 256, 128, 64, 32, 16))
    Ct = [c for c in (512, 256, 128) if C % c == 0 and all(o % c == 0 for o in offs)][0]
    per = T // HALO
    last = S // HALO - 1

    def cur(off=0):
        return pl.BlockSpec((T, Ct), lambda j, i: (i, off // Ct + j))

    def prev(off=0):
        return pl.BlockSpec((HALO, Ct), lambda j, i: (jnp.maximum(i * per - 1, 0), off // Ct + j))

    def nxt(off=0):
        return pl.BlockSpec((HALO, Ct), lambda j, i: (jnp.minimum((i + 1) * per, last), off // Ct + j))

    wspec = pl.BlockSpec((None, 8, Ct), lambda j, i: (l, 0, j))
    return T, Ct, cur, prev, nxt, wspec


def _pad_taps(w):
    return jnp.concatenate([w.astype(F32), jnp.zeros((w.shape[0], 8 - w.shape[1], w.shape[2]), F32)], axis=1)


def _causal(cat, w_ref, K, T, lead):
    out = None
    for k in range(K):
        o = lead - (K - 1) + k
        term = w_ref[k:k + 1, :] * cat[o:o + T]
        out = term if out is None else out + term
    return out


def _anticausal(cat, w_ref, K, T):
    out = None
    for k in range(K):
        o = K - 1 - k
        term = w_ref[k:k + 1, :] * cat[o:o + T]
        out = term if out is None else out + term
    return out


def _scconv_fwd(name, proj, ob, oc, ox, taps, K, l):
    S = proj.shape[0]
    C = taps.shape[2]
    T, Ct, cur, prev, nxt, wspec = _conv_specs(S, C, (ob, oc, ox), l)

    def body(b_ref, c_ref, x_ref, cp_ref, xp_ref, w_ref, o_ref):
        i = pl.program_id(1)
        q = c_ref[...].astype(F32) * x_ref[...].astype(F32)
        qp = jnp.where(i == 0, 0.0, cp_ref[...].astype(F32) * xp_ref[...].astype(F32))
        cat = jnp.concatenate([qp, q], axis=0)
        o_ref[...] = (b_ref[...].astype(F32) * _causal(cat, w_ref, K, T, HALO)).astype(o_ref.dtype)

    return pl.pallas_call(
        body, name=name, grid=(C // Ct, S // T),
        in_specs=[cur(ob), cur(oc), cur(ox), prev(oc), prev(ox), wspec], out_specs=cur(),
        out_shape=jax.ShapeDtypeStruct((S, C), BF), compiler_params=_cp(("parallel", "arbitrary")),
    )(proj, proj, proj, proj, proj, taps)


def _scconv_bwd(name, dv, proj, ob, oc, ox, taps, K, l):
    S = proj.shape[0]
    C = taps.shape[2]
    T, Ct, cur, prev, nxt, wspec = _conv_specs(S, C, (ob, oc, ox), l)
    n_t = S // T

    def body(d_ref, b_ref, c_ref, x_ref, dn_ref, bn_ref, cp_ref, xp_ref, w_ref, db_ref, dc_ref, dx_ref, dw_ref):
        i = pl.program_id(1)
        c = c_ref[...].astype(F32)
        x = x_ref[...].astype(F32)
        d = d_ref[...].astype(F32)
        q = c * x
        qp = jnp.where(i == 0, 0.0, cp_ref[...].astype(F32) * xp_ref[...].astype(F32))
        catq = jnp.concatenate([qp, q], axis=0)
        cv = _causal(catq, w_ref, K, T, HALO)
        db_ref[...] = (d * cv).astype(BF)
        dcv = d * b_ref[...].astype(F32)
        dcvn = jnp.where(i == n_t - 1, 0.0, dn_ref[...].astype(F32) * bn_ref[...].astype(F32))
        catd = jnp.concatenate([dcv, dcvn], axis=0)
        dq = _anticausal(catd, w_ref, K, T)
        dc_ref[...] = (dq * x).astype(BF)
        dx_ref[...] = (dq * c).astype(BF)

        @pl.when(i == 0)
        def _():
            dw_ref[...] = jnp.zeros_like(dw_ref)

        for k in range(K):
            o = HALO - (K - 1) + k
            dw_ref[k:k + 1, :] += jnp.sum(dcv * catq[o:o + T], axis=0, keepdims=True)

    return pl.pallas_call(
        body, name=name, grid=(C // Ct, n_t),
        in_specs=[cur(), cur(ob), cur(oc), cur(ox), nxt(), nxt(ob), prev(oc), prev(ox), wspec],
        out_specs=[cur(), cur(), cur(), pl.BlockSpec((8, Ct), lambda j, i: (0, j))],
        out_shape=[jax.ShapeDtypeStruct((S, C), BF)] * 3 + [jax.ShapeDtypeStruct((8, C), F32)],
        compiler_params=_cp(("parallel", "arbitrary")),
    )(dv, proj, proj, proj, dv, proj, proj, proj, taps)


def _mconv_fwd(name, proj, ox, taps, K, bias, l):
    S = proj.shape[0]
    C = taps.shape[2]
    T, Ct, cur, prev, nxt, wspec = _conv_specs(S, C, (ox,), l)
    bspec = pl.BlockSpec((None, 1, Ct), lambda j, i: (l, 0, j))

    def body(x_ref, xp_ref, w_ref, b_ref, o_ref):
        i = pl.program_id(1)
        xp = jnp.where(i == 0, 0.0, xp_ref[...].astype(F32))
        cat = jnp.concatenate([xp, x_ref[...].astype(F32)], axis=0)
        pre = _causal(cat, w_ref, K, T, HALO) + b_ref[...]
        o_ref[...] = (pre * _sigmoid(pre)).astype(o_ref.dtype)

    return pl.pallas_call(
        body, name=name, grid=(C // Ct, S // T),
        in_specs=[cur(ox), prev(ox), wspec, bspec], out_specs=cur(),
        out_shape=jax.ShapeDtypeStruct((S, C), BF), compiler_params=_cp(("parallel", "arbitrary")),
    )(proj, proj, taps, bias)


def _mconv_bwd(name, dout, proj, ox, taps, K, bias, l):
    S = proj.shape[0]
    C = taps.shape[2]
    T, Ct, cur, prev, nxt, wspec = _conv_specs(S, C, (ox,), l)
    n_t = S // T
    bspec = pl.BlockSpec((None, 1, Ct), lambda j, i: (l, 0, j))

    def body(d_ref, dn_ref, x_ref, xp_ref, xn_ref, w_ref, b_ref, dx_ref, dw_ref, db_ref):
        i = pl.program_id(1)
        xp = jnp.where(i == 0, 0.0, xp_ref[...].astype(F32))
        cat3 = jnp.concatenate([xp, x_ref[...].astype(F32), xn_ref[...].astype(F32)], axis=0)
        pre = _causal(cat3, w_ref, K, T + HALO, HALO) + b_ref[...]
        dn = jnp.where(i == n_t - 1, 0.0, dn_ref[...].astype(F32))
        dext = jnp.concatenate([d_ref[...].astype(F32), dn], axis=0)
        s = _sigmoid(pre)
        dpre = dext * (s * (1.0 + pre * (1.0 - s)))
        dx_ref[...] = _anticausal(dpre, w_ref, K, T).astype(BF)
        dcur = dpre[:T]

        @pl.when(i == 0)
        def _():
            dw_ref[...] = jnp.zeros_like(dw_ref)
            db_ref[...] = jnp.zeros_like(db_ref)

        db_ref[...] += jnp.sum(dcur, axis=0, keepdims=True)
        for k in range(K):
            o = HALO - (K - 1) + k
            dw_ref[k:k + 1, :] += jnp.sum(dcur * cat3[o:o + T], axis=0, keepdims=True)

    return pl.pallas_call(
        body, name=name, grid=(C // Ct, n_t),
        in_specs=[cur(), nxt(), cur(ox), prev(ox), nxt(ox), wspec, bspec],
        out_specs=[cur(), pl.BlockSpec((8, Ct), lambda j, i: (0, j)), pl.BlockSpec((1, Ct), lambda j, i: (0, j))],
        out_shape=[jax.ShapeDtypeStruct((S, C), BF), jax.ShapeDtypeStruct((8, C), F32), jax.ShapeDtypeStruct((1, C), F32)],
        compiler_params=_cp(("parallel", "arbitrary")),
    )(dout, dout, proj, proj, proj, taps, bias)


def _col(v, idx, lane):
    return jnp.sum(jnp.where(lane == idx, v, 0.0), axis=1, keepdims=True)


def _row(v, idx, sub):
    return jnp.sum(jnp.where(sub == idx, v, 0.0), axis=0, keepdims=True)


def _tri_matmul(tri_bf, v):
    hi = v.astype(BF)
    r1 = v - hi.astype(F32)
    mid = r1.astype(BF)
    lo = (r1 - mid.astype(F32)).astype(BF)
    dot = functools.partial(jnp.dot, preferred_element_type=F32)
    return dot(tri_bf, hi) + dot(tri_bf, mid) + dot(tri_bf, lo)


def _dot_nt(a, b):
    return lax.dot_general(a, b, (((1,), (1,)), ((), ())), preferred_element_type=F32)


def _dot_tn(a, b):
    return lax.dot_general(a, b, (((0,), (0,)), ((), ())), preferred_element_type=F32)


def _dot_nn(a, b):
    return jnp.dot(a, b, preferred_element_type=F32)


def _ssd_chunk_scalars(dtr_ref, par_ref, L):
    row_i = lax.broadcasted_iota(jnp.int32, (L, L), 0)
    col_i = lax.broadcasted_iota(jnp.int32, (L, L), 1)
    tri = row_i >= col_i
    pre = dtr_ref[...] + par_ref[0:1, :]
    dt_all = _softplus(pre)
    A_row = -jnp.exp(par_ref[1:2, :])
    a_all = dt_all * A_row
    acum_all = _tri_matmul(tri.astype(BF), a_all)
    return tri, pre, dt_all, A_row, a_all, acum_all, acum_all.T


def _ssd_dims(xbc, heads):
    S, conv_dim = xbc.shape
    inner = heads * SSM_HEADDIM
    N = (conv_dim - inner) // (2 * SSM_GROUPS)
    gw = inner // SSM_GROUPS
    PP = gw // LANES
    L = min(SSM_CHUNK, S)
    assert N == LANES and gw % LANES == 0 and inner % N == 0 and S % L == 0
    return S, inner, N, gw, PP, L, S // L


def _ssd_params(dt_bias, A_log, Dp):
    depth, H = dt_bias.shape
    rows = jnp.stack([dt_bias, A_log, Dp], axis=1).astype(F32)
    rows = jnp.concatenate([rows, jnp.zeros((depth, 3, LANES - H), F32)], axis=2)
    return jnp.concatenate([rows, jnp.zeros((depth, 5, LANES), F32)], axis=1)


def _ssd_fwd(name, xbc, dt_raw, par, l, heads):
    S, inner, N, gw, PP, L, nc = _ssd_dims(xbc, heads)
    G = SSM_GROUPS
    boff = inner // N

    def body(x_ref, b_ref, c_ref, dtr_ref, par_ref, y_ref, st_out_ref, st_ref):
        c = pl.program_id(0)
        g = pl.program_id(1)

        @pl.when(c == 0)
        def _():
            st_ref[pl.ds(g * PP, PP)] = jnp.zeros((PP, LANES, N), F32)

        tri, pre, dt_all, A_row, a_all, acum_all, acumT = _ssd_chunk_scalars(dtr_ref, par_ref, L)
        lane = lax.broadcasted_iota(jnp.int32, (L, LANES), 1)
        lane1 = lax.broadcasted_iota(jnp.int32, (1, LANES), 1)
        sub = lax.broadcasted_iota(jnp.int32, (LANES, L), 0)
        subp = lax.broadcasted_iota(jnp.int32, (LANES, 1), 0)
        rowl = lax.broadcasted_iota(jnp.int32, (L, 1), 0)
        lo = lane < SSM_HEADDIM
        lo1 = lane1 < SSM_HEADDIM
        Bb = b_ref[...]
        Cb = c_ref[...]
        Gm = _dot_nt(Cb, Bb)
        for j in range(PP):
            h0 = (g * PP + j) * 2
            h1 = h0 + 1
            x = x_ref[:, j * LANES:(j + 1) * LANES].astype(F32)
            dt_l = jnp.where(lo, _col(dt_all, h0, lane), _col(dt_all, h1, lane))
            ac0 = _col(acum_all, h0, lane)
            ac1 = _col(acum_all, h1, lane)
            ac_l = jnp.where(lo, ac0, ac1)
            E0 = jnp.exp(jnp.where(tri, ac0 - _row(acumT, h0, sub), -1e30))
            E1 = jnp.exp(jnp.where(tri, ac1 - _row(acumT, h1, sub), -1e30))
            xd = x * dt_l
            xdb = xd.astype(BF)
            yd = jnp.where(lo, _dot_nn((Gm * E0).astype(BF), xdb), _dot_nn((Gm * E1).astype(BF), xdb))
            prev = st_ref[g * PP + j]
            st_out_ref[0, j] = prev
            P = _dot_nt(Cb, prev.astype(BF))
            D_l = jnp.where(lo1, _col(par_ref[2:3, :], h0, lane1), _col(par_ref[2:3, :], h1, lane1))
            y_ref[:, j * LANES:(j + 1) * LANES] = (yd + P * jnp.exp(ac_l) + D_l * x).astype(y_ref.dtype)
            al0 = jnp.sum(jnp.where(rowl == L - 1, ac0, 0.0), axis=0, keepdims=True)
            al1 = jnp.sum(jnp.where(rowl == L - 1, ac1, 0.0), axis=0, keepdims=True)
            Wm = xd * jnp.exp(jnp.where(lo, al0, al1) - ac_l)
            eal = jnp.where(subp < SSM_HEADDIM, jnp.exp(al0), jnp.exp(al1))
            st_ref[g * PP + j] = eal * prev + _dot_tn(Wm.astype(BF), Bb)

    xspec = pl.BlockSpec((L, gw), lambda c, g: (c, g))
    return pl.pallas_call(
        body, name=name, grid=(nc, G),
        in_specs=[xspec, pl.BlockSpec((L, N), lambda c, g: (c, boff + g)), pl.BlockSpec((L, N), lambda c, g: (c, boff + G + g)),
                  pl.BlockSpec((L, LANES), lambda c, g: (c, 0)), pl.BlockSpec((None, 8, LANES), lambda c, g: (l, 0, 0))],
        out_specs=[xspec, pl.BlockSpec((1, PP, LANES, N), lambda c, g: (c, g, 0, 0))],
        out_shape=[jax.ShapeDtypeStruct((S, inner), BF), jax.ShapeDtypeStruct((nc, G * PP, LANES, N), F32)],
        scratch_shapes=[pltpu.VMEM((G * PP, LANES, N), F32)],
        compiler_params=_cp(("arbitrary", "arbitrary")),
    )(xbc, xbc, xbc, dt_raw, par)


def _ssd_bwd(name, dy, xbc, dt_raw, states, par, l, heads):
    S, inner, N, gw, PP, L, nc = _ssd_dims(xbc, heads)
    G = SSM_GROUPS
    boff = inner // N

    def body(dy_ref, x_ref, b_ref, c_ref, dtr_ref, par_ref, st_in_ref, dx_ref, dB_ref, dC_ref, ddt_ref, dpar_ref, dst_ref):
        c = pl.program_id(0)
        g = pl.program_id(1)

        @pl.when(c == 0)
        def _():
            dst_ref[pl.ds(g * PP, PP)] = jnp.zeros((PP, LANES, N), F32)

        @pl.when((c == 0) & (g == 0))
        def _():
            dpar_ref[...] = jnp.zeros_like(dpar_ref)

        tri, pre, dt_all, A_row, a_all, acum_all, acumT = _ssd_chunk_scalars(dtr_ref, par_ref, L)
        lane = lax.broadcasted_iota(jnp.int32, (L, LANES), 1)
        lane1 = lax.broadcasted_iota(jnp.int32, (1, LANES), 1)
        sub = lax.broadcasted_iota(jnp.int32, (LANES, L), 0)
        subp = lax.broadcasted_iota(jnp.int32, (LANES, 1), 0)
        rowl = lax.broadcasted_iota(jnp.int32, (L, 1), 0)
        lo = lane < SSM_HEADDIM
        lo1 = lane1 < SSM_HEADDIM
        Bb = b_ref[...]
        Cb = c_ref[...]
        Gm = _dot_nt(Cb, Bb)
        dG = jnp.zeros((L, L), F32)
        dBacc = jnp.zeros((L, N), F32)
        dCacc = jnp.zeros((L, N), F32)
        dac_all = jnp.zeros((L, LANES), F32)
        xds_all = jnp.zeros((L, LANES), F32)
        dD_row = jnp.zeros((1, LANES), F32)

        def rsum(v):
            return jnp.sum(v, axis=1, keepdims=True)

        def total(v):
            return jnp.sum(jnp.sum(v, axis=1, keepdims=True), axis=0, keepdims=True)

        for j in range(PP):
            h0 = (g * PP + j) * 2
            h1 = h0 + 1
            sl = slice(j * LANES, (j + 1) * LANES)
            x = x_ref[:, sl].astype(F32)
            dyv = dy_ref[:, sl].astype(F32)
            dt_l = jnp.where(lo, _col(dt_all, h0, lane), _col(dt_all, h1, lane))
            ac0 = _col(acum_all, h0, lane)
            ac1 = _col(acum_all, h1, lane)
            ac_l = jnp.where(lo, ac0, ac1)
            E0 = jnp.exp(jnp.where(tri, ac0 - _row(acumT, h0, sub), -1e30))
            E1 = jnp.exp(jnp.where(tri, ac1 - _row(acumT, h1, sub), -1e30))
            xd = x * dt_l
            xdb = xd.astype(BF)
            M0 = Gm * E0
            M1 = Gm * E1
            ea_l = jnp.exp(ac_l)
            al0 = jnp.sum(jnp.where(rowl == L - 1, ac0, 0.0), axis=0, keepdims=True)
            al1 = jnp.sum(jnp.where(rowl == L - 1, ac1, 0.0), axis=0, keepdims=True)
            dte_l = jnp.exp(jnp.where(lo, al0, al1) - ac_l)
            Wm = xd * dte_l
            prev = st_in_ref[0, j]
            prevb = prev.astype(BF)
            P = _dot_nt(Cb, prevb)
            D_l = jnp.where(lo1, _col(par_ref[2:3, :], h0, lane1), _col(par_ref[2:3, :], h1, lane1))
            dx = D_l * dyv
            s_l = jnp.sum(dyv * x, axis=0, keepdims=True)
            dD0 = rsum(jnp.where(lo1, s_l, 0.0))
            dD1 = rsum(jnp.where(lo1, 0.0, s_l))
            dyb = dyv.astype(BF)
            dM0 = _dot_nt(jnp.where(lo, dyv, 0.0).astype(BF), xdb)
            dM1 = _dot_nt(jnp.where(lo, 0.0, dyv).astype(BF), xdb)
            dxd = jnp.where(lo, _dot_tn(M0.astype(BF), dyb), _dot_tn(M1.astype(BF), dyb))
            dG = dG + dM0 * E0 + dM1 * E1
            Q0 = dM0 * M0
            Q1 = dM1 * M1
            dac0 = rsum(Q0) - rsum(Q0.T)
            dac1 = rsum(Q1) - rsum(Q1.T)
            dP = dyv * ea_l
            dPb = dP.astype(BF)
            dCacc = dCacc + _dot_nn(dPb, prevb)
            dprev = _dot_tn(dPb, Cb)
            t = dP * P
            dac0 = dac0 + rsum(jnp.where(lo, t, 0.0))
            dac1 = dac1 + rsum(jnp.where(lo, 0.0, t))
            dnew = dst_ref[g * PP + j]
            dnewb = dnew.astype(BF)
            e0 = jnp.exp(al0)
            e1 = jnp.exp(al1)
            dprev = dprev + jnp.where(subp < SSM_HEADDIM, e0, e1) * dnew
            u = dnew * prev
            dal0 = total(jnp.where(subp < SSM_HEADDIM, u, 0.0)) * e0
            dal1 = total(jnp.where(subp < SSM_HEADDIM, 0.0, u)) * e1
            dW = _dot_nt(Bb, dnewb)
            dBacc = dBacc + _dot_nn(Wm.astype(BF), dnewb)
            dxd = dxd + dW * dte_l
            tt = dW * Wm
            t0 = rsum(jnp.where(lo, tt, 0.0))
            t1 = rsum(jnp.where(lo, 0.0, tt))
            dal0 = dal0 + jnp.sum(t0, axis=0, keepdims=True)
            dal1 = dal1 + jnp.sum(t1, axis=0, keepdims=True)
            dac0 = dac0 - t0 + jnp.where(rowl == L - 1, dal0, 0.0)
            dac1 = dac1 - t1 + jnp.where(rowl == L - 1, dal1, 0.0)
            dx = dx + dxd * dt_l
            q = dxd * x
            dst_ref[g * PP + j] = dprev
            dx_ref[:, sl] = dx.astype(dx_ref.dtype)
            dac_all = dac_all + jnp.where(lane == h0, dac0, 0.0) + jnp.where(lane == h1, dac1, 0.0)
            xds_all = (xds_all + jnp.where(lane == h0, rsum(jnp.where(lo, q, 0.0)), 0.0)
                       + jnp.where(lane == h1, rsum(jnp.where(lo, 0.0, q)), 0.0))
            dD_row = dD_row + jnp.where(lane1 == h0, dD0, 0.0) + jnp.where(lane1 == h1, dD1, 0.0)

        dGb = dG.astype(BF)
        dC_ref[...] = (dCacc + _dot_nn(dGb, Bb)).astype(dC_ref.dtype)
        dB_ref[...] = (dBacc + _dot_tn(dGb, Cb)).astype(dB_ref.dtype)
        row_i = lax.broadcasted_iota(jnp.int32, (L, L), 0)
        col_i = lax.broadcasted_iota(jnp.int32, (L, L), 1)
        da_all = _tri_matmul((row_i <= col_i).astype(BF), dac_all)
        mine = (lane >= g * (2 * PP)) & (lane < (g + 1) * (2 * PP))
        ddt_all = da_all * A_row + xds_all
        draw = jnp.where(mine, ddt_all * _sigmoid(pre), 0.0)

        @pl.when(g == 0)
        def _():
            ddt_ref[...] = draw

        @pl.when(g != 0)
        def _():
            ddt_ref[...] += draw

        dbias_row = jnp.sum(draw, axis=0, keepdims=True)
        dalog_row = jnp.sum(jnp.where(mine, da_all * a_all, 0.0), axis=0, keepdims=True)
        dpar_ref[0:1, :] += dbias_row
        dpar_ref[1:2, :] += dalog_row
        dpar_ref[2:3, :] += dD_row

    xspec = pl.BlockSpec((L, gw), lambda c, g: (nc - 1 - c, g))
    nspec = pl.BlockSpec((L, N), lambda c, g: (nc - 1 - c, g))
    return pl.pallas_call(
        body, name=name, grid=(nc, G),
        in_specs=[xspec, xspec, pl.BlockSpec((L, N), lambda c, g: (nc - 1 - c, boff + g)),
                  pl.BlockSpec((L, N), lambda c, g: (nc - 1 - c, boff + G + g)),
                  pl.BlockSpec((L, LANES), lambda c, g: (nc - 1 - c, 0)), pl.BlockSpec((None, 8, LANES), lambda c, g: (l, 0, 0)),
                  pl.BlockSpec((1, PP, LANES, N), lambda c, g: (nc - 1 - c, g, 0, 0))],
        out_specs=[xspec, nspec, nspec, pl.BlockSpec((L, LANES), lambda c, g: (nc - 1 - c, 0)),
                   pl.BlockSpec((8, LANES), lambda c, g: (0, 0))],
        out_shape=[jax.ShapeDtypeStruct((S, inner), BF), jax.ShapeDtypeStruct((S, G * N), BF), jax.ShapeDtypeStruct((S, G * N), BF),
                   jax.ShapeDtypeStruct((S, LANES), F32), jax.ShapeDtypeStruct((8, LANES), F32)],
        scratch_shapes=[pltpu.VMEM((G * PP, LANES, N), F32)],
        compiler_params=_cp(("arbitrary", "arbitrary")),
    )(dy, xbc, xbc, xbc, dt_raw, par, states)


def _adamw(g, w, m, v):
    m2 = ADAM_B1 * m + (1.0 - ADAM_B1) * g
    v2 = ADAM_B2 * v + (1.0 - ADAM_B2) * (g * g)
    m_hat = m2 / (1.0 - ADAM_B1 ** ADAM_STEP)
    v_hat = v2 / (1.0 - ADAM_B2 ** ADAM_STEP)
    delta = -ADAM_LR * (m_hat / (jnp.sqrt(v_hat) + ADAM_EPS) + ADAM_WD * w)
    return delta, m2, v2


def _flat_tile(R):
    return _pick(R, (FLAT_ROW_TILE, 1024, 512, 256, 128, 64, 32, 16, 8))


def _sum_adam(name, lands, off, w, m, v):
    depth, r, c = w.shape
    cap = max(16, (4 * 1024 * 1024) // (N_DEV * c * 2))
    row_tiles = [t for t in (512, 256, 128, 64, 32, 16) if r % t == 0 and off % t == 0 and t <= cap]
    if row_tiles:
        tr, tc = row_tiles[0], c
        ob = off // tr
        n_t = r // tr
        spec = pl.BlockSpec((None, tr, c), lambda l, t: (l, t, 0))
        land_specs = [pl.BlockSpec((N_DEV, tr, c), lambda l, t, i=i: (0, jnp.where(l == i, ob + t, ob), 0)) for i in range(depth)]
    else:
        assert off == 0 and lands[0].shape[1] == r and c % LANES == 0
        tc = LANES
        n_t = c // tc
        spec = pl.BlockSpec((None, r, tc), lambda l, t: (l, 0, t))
        land_specs = [pl.BlockSpec((N_DEV, r, tc), lambda l, t, i=i: (0, 0, jnp.where(l == i, t, 0))) for i in range(depth)]

    def body(*refs):
        land_refs = refs[:depth]
        w_ref, m_ref, v_ref, g_ref, d_ref, m2_ref, v2_ref = refs[depth:]
        l = pl.program_id(0)
        for i in range(depth):
            @pl.when(l == i)
            def _(i=i):
                g = land_refs[i][0].astype(F32)
                for k in range(1, N_DEV):
                    g = g + land_refs[i][k].astype(F32)
                g_ref[...] = g
                d_ref[...], m2_ref[...], v2_ref[...] = _adamw(g, w_ref[...], m_ref[...], v_ref[...])

    return pl.pallas_call(
        body, name=name, grid=(depth, n_t),
        in_specs=land_specs + [spec, spec, spec],
        out_specs=[spec] * 4, out_shape=[jax.ShapeDtypeStruct((depth, r, c), F32)] * 4,
        compiler_params=_cp(("arbitrary", "arbitrary")),
    )(*lands, w, m, v)


def _sum8(name, parts):
    R = parts.shape[1]
    TR = _flat_tile(R)

    def body(p_ref, g_ref):
        g = p_ref[0]
        for k in range(1, N_DEV):
            g = g + p_ref[k]
        g_ref[...] = g

    return pl.pallas_call(
        body, name=name, grid=(R // TR,),
        in_specs=[pl.BlockSpec((N_DEV, TR, LANES), lambda i: (0, i, 0))],
        out_specs=pl.BlockSpec((TR, LANES), lambda i: (i, 0)), out_shape=jax.ShapeDtypeStruct((R, LANES), F32),
        compiler_params=_cp(("parallel",)),
    )(parts)


def _adam_flat(name, g, w, m, v):
    R = w.shape[0]
    TR = _flat_tile(R)

    def body(g_ref, w_ref, m_ref, v_ref, d_ref, m2_ref, v2_ref):
        d_ref[...], m2_ref[...], v2_ref[...] = _adamw(g_ref[...], w_ref[...], m_ref[...], v_ref[...])

    spec = pl.BlockSpec((TR, LANES), lambda i: (i, 0))
    return pl.pallas_call(
        body, name=name, grid=(R // TR,), in_specs=[spec] * 4, out_specs=[spec] * 3,
        out_shape=[jax.ShapeDtypeStruct((R, LANES), F32)] * 3, compiler_params=_cp(("parallel",)),
    )(g, w, m, v)


PART_ROWS = 16


def _nrows(shape):
    n = 1
    for s in shape:
        n *= s
    r = -(-n // LANES)
    return -(-r // PART_ROWS) * PART_ROWS


def _as_rows(a):
    n = a.size
    r = _nrows(a.shape)
    f = a.reshape(-1)
    if r * LANES != n:
        f = jnp.concatenate([f, jnp.zeros((r * LANES - n,), a.dtype)])
    return f.reshape(r, LANES)


def _pack(arrs, mult=PART_ROWS):
    cat = jnp.concatenate([_as_rows(a) for a in arrs], axis=0)
    pad = (-cat.shape[0]) % mult
    if pad:
        cat = jnp.concatenate([cat, jnp.zeros((pad, LANES), cat.dtype)], axis=0)
    return cat


def _unpack(flat, shapes):
    lead = flat.shape[:-2]
    out = []
    o = 0
    for shp in shapes:
        n = 1
        for s in shp:
            n *= s
        r = _nrows(shp)
        blk = flat[..., o:o + r, :].reshape(lead + (r * LANES,))
        out.append(blk[..., :n].reshape(lead + tuple(shp)))
        o += r
    return out


def _full_from_shards(st):
    return st.reshape(st.shape[0] * st.shape[1], st.shape[2])


def _shards_from_full(full):
    return full.reshape(N_DEV, full.shape[0] // N_DEV, full.shape[1])


def _ffn_fwd(tag, h, g, wgT, wuT, wd, dep=None):
    xn = _rms_fwd(tag + "_rms", h, g, dep=dep)
    a, b, hmid = _ffn_up(tag + "_up", xn, wgT, wuT)
    hout = _mm(tag + "_down", hmid, wd, out_dtype=F32, res=h, alpha=0.5)
    return hout, (xn, a, b, hmid)


def _ffn_bwd(tag, dh_out, h, g, wgT, wuT, wd, saved, dep=None):
    xn, a, b, hmid = saved
    da, db = _ffn_dact(tag + "_d_act", dh_out, wd, a, b, dep=dep)
    d_wd = _mm(tag + "_d_wd", hmid, dh_out, ta=True, alpha=0.5)
    d_wgT = _mm(tag + "_d_wg", da, xn, ta=True)
    d_wuT = _mm(tag + "_d_wu", db, xn, ta=True)
    dxn = _mm(tag + "_d_xn_g", da, wgT, out_dtype=F32)
    dxn = _mm(tag + "_d_xn_u", db, wuT, out_dtype=F32, res=dxn)
    dh, dg = _rms_bwd(tag + "_d_rms", dxn, h, g, dh_out)
    return dh, dg, d_wgT, d_wuT, d_wd


SEG_NAMES = ['scb', 'scc', 'scx', 'z', 'xbc', 'dt', 'ga', 'gm']
PERM = ['z', 'scb', 'scc', 'scx', 'ga', 'gm', 'xbc']


def _seg_layout(dims):
    D, inner, conv_dim, H = dims[:4]
    widths = dict(zip(SEG_NAMES, [D, D, D, inner, conv_dim, H, D, D]))
    offs, o = {}, 0
    for n in SEG_NAMES:
        offs[n] = (o, widths[n])
        o += widths[n]
    poffs, o = {}, 0
    for n in PERM:
        poffs[n] = (o, widths[n])
        o += widths[n]
    return offs, poffs


def _perm_w_in(w_inT, dims):
    offs, _ = _seg_layout(dims)
    wp = jnp.concatenate([w_inT[offs[n][0]:offs[n][0] + offs[n][1]] for n in PERM], axis=0)
    o, w = offs['dt']
    wdt = jnp.concatenate([w_inT[o:o + w], jnp.zeros((LANES - w, w_inT.shape[1]), w_inT.dtype)], axis=0)
    return wp, wdt


def _unperm_d_w_in(d_wp, d_wdt, dims):
    offs, poffs = _seg_layout(dims)
    H = dims[3]
    return jnp.concatenate([d_wdt[:H] if n == 'dt' else d_wp[poffs[n][0]:poffs[n][0] + poffs[n][1]] for n in SEG_NAMES], axis=0)


def _mixer_fwd(h, W, dims, dep=None):
    H, Ksc, Km = dims[3:]
    l = W['l']
    _, poffs = _seg_layout(dims)

    def seg(n):
        o, w = poffs[n]
        assert o % w == 0
        return (proj, w, o // w)

    u = _rms_fwd("mix_rms", h, W['mix_norm'], dep=dep)
    proj = _mm("inproj", u, W['w_in_p'], tb=True)
    dt_raw = _mm("inproj_dt", u, W['w_dt'], tb=True, out_dtype=F32)
    v = _scconv_fwd("scconv_f", proj, poffs['scb'][0], poffs['scc'][0], poffs['scx'][0], W['sc_taps'], Ksc, l)
    ya = _mm("sc_out", v, W['sc_w_out'])
    xbc = _mconv_fwd("mconv_f", proj, poffs['xbc'][0], W['m_taps'], Km, W['m_conv_b'], l)
    y, states = _ssd_fwd("ssd_f", xbc, dt_raw, W['ssd_par'], l, H)
    yn = _gnorm_fwd("gnorm_f", y, seg('z'), W['m_norm'])
    ym = _mm("m_out", yn, W['m_w_out'])
    merged = _merge_fwd("merge_f", seg('ga'), seg('gm'), ya, ym)
    hout = _mm("w_o", merged, W['w_o'], out_dtype=F32, res=h)
    return hout, (u, proj, dt_raw, v, ya, xbc, y, states, yn, ym, merged)


def _mixer_bwd(dh_out, h, W, dims, saved):
    u, proj, dt_raw, v, ya, xbc, y, states, yn, ym, merged = saved
    H, Ksc, Km = dims[3:]
    l = W['l']
    _, poffs = _seg_layout(dims)

    def seg(n):
        o, w = poffs[n]
        return (proj, w, o // w)

    g = {}
    dmerged = _mm("d_merged", dh_out, W['w_o'], tb=True)
    g['w_o'] = _mm("d_w_o", merged, dh_out, ta=True)
    dga, dgm, dya, dym = _merge_bwd("merge_b", dmerged, seg('ga'), seg('gm'), ya, ym)
    g['sc_w_out'] = _mm("d_sc_w_out", v, dya, ta=True)
    dv = _mm("d_v", dya, W['sc_w_out'], tb=True)
    g['m_w_out'] = _mm("d_m_w_out", yn, dym, ta=True)
    dyn = _mm("d_yn", dym, W['m_w_out'], tb=True)
    dy, dz, d_mnorm = _gnorm_bwd("gnorm_b", dyn, y, seg('z'), W['m_norm'])
    g['m_norm'] = d_mnorm.reshape(-1)
    dxs, dB, dC, ddt, dpar = _ssd_bwd("ssd_b", dy, xbc, dt_raw, states, W['ssd_par'], l, H)
    g['m_dt_bias'] = dpar[0, :H]
    g['m_A_log'] = dpar[1, :H]
    g['m_D'] = dpar[2, :H]
    dxbc_post = jnp.concatenate([dxs, dB, dC], axis=1)
    dxbc, d_mcw, d_mcb = _mconv_bwd("mconv_b", dxbc_post, proj, poffs['xbc'][0], W['m_taps'], Km, W['m_conv_b'], l)
    g['m_conv_w'] = d_mcw[:Km]
    g['m_conv_b'] = d_mcb.reshape(-1)
    dscb, dscc, dscx, d_scw = _scconv_bwd("scconv_b", dv, proj, poffs['scb'][0], poffs['scc'][0], poffs['scx'][0],
                                          W['sc_taps'], Ksc, l)
    g['sc_conv_w'] = d_scw[:Ksc]
    dproj = jnp.concatenate([dz, dscb, dscc, dscx, dga, dgm, dxbc], axis=1)
    du = _mm("d_u_main", dproj, W['w_in_p'], out_dtype=F32)
    du = _mm("d_u_dt", ddt, W['w_dt'], out_dtype=F32, res=du)
    d_wp = _mm("d_w_in_main", dproj, u, ta=True)
    d_wdt = _mm("d_w_in_dt", ddt, u, ta=True)
    g['w_in'] = _unperm_d_w_in(d_wp, d_wdt, dims)
    dh, dg = _rms_bwd("mix_d_rms", du, h, W['mix_norm'], dh_out)
    g['mix_norm'] = dg.reshape(-1)
    return dh, g


def _ple_layer_fwd(h, p_l, W):
    xn = _rms_fwd("ple_rms", h, W['ple_norm'])
    gpre = _mm("ple_gate", xn, W['ple_w_gate'])
    pp = _mm("ple_proj", p_l, W['ple_w_proj'], tb=True)
    hout = _ple_fwd("ple_f", h, gpre, pp)
    return hout, (xn, gpre, pp)


def _ple_layer_bwd(dh_out, h, p_l, W, saved, dep=None):
    xn, gpre, pp = saved
    g = {}
    dgpre, dpp = _ple_bwd("ple_b", dh_out, gpre, pp, dep=dep)
    g['ple_w_proj'] = _mm("d_ple_proj", dpp, p_l, ta=True)
    g['ple_w_gate'] = _mm("d_ple_gate", xn, dgpre, ta=True)
    dxn = _mm("d_ple_xn", dgpre, W['ple_w_gate'], tb=True)
    dh, dg = _rms_bwd("ple_d_rms", dxn, h, W['ple_norm'], dh_out)
    g['ple_norm'] = dg.reshape(-1)
    return dh, g


def kernel(x, p, ffn1_norm, ffn1_wg, ffn1_wu, ffn1_wd, mix_norm, w_in, sc_conv_w, sc_w_out, m_conv_w, m_conv_b, m_dt_bias, m_A_log, m_D, m_norm, m_w_out, w_o, ffn2_norm, ffn2_wg, ffn2_wu, ffn2_wd, ple_norm, ple_w_gate, ple_w_proj, final_norm, loss_target, m_ffn1_norm, m_ffn1_wg, m_ffn1_wu, m_ffn1_wd, m_mix_norm, m_w_in, m_sc_conv_w, m_sc_w_out, m_m_conv_w, m_m_conv_b, m_m_dt_bias, m_m_A_log, m_m_D, m_m_norm, m_m_w_out, m_w_o, m_ffn2_norm, m_ffn2_wg, m_ffn2_wu, m_ffn2_wd, m_ple_norm, m_ple_w_gate, m_ple_w_proj, m_final_norm, v_ffn1_norm, v_ffn1_wg, v_ffn1_wu, v_ffn1_wd, v_mix_norm, v_w_in, v_sc_conv_w, v_sc_w_out, v_m_conv_w, v_m_conv_b, v_m_dt_bias, v_m_A_log, v_m_D, v_m_norm, v_m_w_out, v_w_o, v_ffn2_norm, v_ffn2_wg, v_ffn2_wu, v_ffn2_wd, v_ple_norm, v_ple_w_gate, v_ple_w_proj, v_final_norm):
    args = (x, p, ffn1_norm, ffn1_wg, ffn1_wu, ffn1_wd, mix_norm, w_in, sc_conv_w, sc_w_out, m_conv_w, m_conv_b, m_dt_bias, m_A_log, m_D, m_norm, m_w_out, w_o, ffn2_norm, ffn2_wg, ffn2_wu, ffn2_wd, ple_norm, ple_w_gate, ple_w_proj, final_norm, loss_target, m_ffn1_norm, m_ffn1_wg, m_ffn1_wu, m_ffn1_wd, m_mix_norm, m_w_in, m_sc_conv_w, m_sc_w_out, m_m_conv_w, m_m_conv_b, m_m_dt_bias, m_m_A_log, m_m_D, m_m_norm, m_m_w_out, m_w_o, m_ffn2_norm, m_ffn2_wg, m_ffn2_wu, m_ffn2_wd, m_ple_norm, m_ple_w_gate, m_ple_w_proj, m_final_norm, v_ffn1_norm, v_ffn1_wg, v_ffn1_wu, v_ffn1_wd, v_mix_norm, v_w_in, v_sc_conv_w, v_sc_w_out, v_m_conv_w, v_m_conv_b, v_m_dt_bias, v_m_A_log, v_m_D, v_m_norm, v_m_w_out, v_w_o, v_ffn2_norm, v_ffn2_wg, v_ffn2_wu, v_ffn2_wd, v_ple_norm, v_ple_w_gate, v_ple_w_proj, v_final_norm)
    names = ARG_NAMES + ['m_' + n for n in WEIGHTS] + ['v_' + n for n in WEIGHTS]
    A = dict(zip(names, args))
    depth = ffn1_norm.shape[0]
    me = 4 * lax.axis_index("x") + 2 * lax.axis_index("y") + lax.axis_index("c")

    dims = (x.shape[-1], m_norm.shape[1], m_conv_b.shape[1], m_dt_bias.shape[1], sc_conv_w.shape[1], m_conv_w.shape[1])
    kind = dict(BIG)

    def work(n, prefix=''):
        return jnp.swapaxes(A[prefix + n], 1, 2) if kind[n] == 'col' else A[prefix + n]

    wb = {n: work(n).astype(BF) for n, _ in BIG}
    srcs = [[wb[ms[0]] if len(ms) == 1 else jnp.concatenate([wb[n] for n in ms], axis=1) for ms in stage] for stage in STAGES]
    conv_g = _unpack(_exchange("gather_conv_taps", _pack([A[n] for n in CONVW]), True), [A[n].shape for n in CONVW])
    taps = {}
    for n, st in zip(CONVW, conv_g):
        taps[n] = _pad_taps(jnp.transpose(st, (1, 2, 0, 3)).reshape(depth, st.shape[2], N_DEV * st.shape[3]))
    ssd_par = _ssd_params(m_dt_bias, m_A_log, m_D)
    small3 = {n: A[n].reshape(depth, 1, -1) for n in SMALL}

    def stage_weights(W, s, l, lands):
        for ms, land, src in zip(STAGES[s], lands, srcs[s]):
            land = lax.dynamic_update_slice(land, src[l][None], (me, 0, 0))
            off = 0
            for n in ms:
                r = wb[n].shape[1]
                W[n] = _full_from_shards(land if len(ms) == 1 else land[:, off:off + r])
                off += r
        if s == 1:
            W['w_in_p'], W['w_dt'] = _perm_w_in(W.pop('w_in'), dims)

    flight = [None, None]
    sems, lands, tok = _xchg_begin("gather_begin0a", srcs[0], 0, taps['sc_conv_w'])
    flight[0] = (sems, lands)
    sems, lands, tok = _xchg_begin("gather_begin0b", srcs[1], 0, tok)
    flight[1] = (sems, lands)
    lands_a = _xchg_end("gather_end0a", srcs[0], flight[0][1], flight[0][0], 0, tok)
    h = x[0]
    saved = []
    layers = []
    for l in range(depth):
        W = {n: (small3[n], l) for n in SMALL}
        W.update(l=l, sc_taps=taps['sc_conv_w'], m_taps=taps['m_conv_w'], m_conv_b=small3['m_conv_b'], ssd_par=ssd_par)
        layers.append(W)
        stage_weights(W, 0, l, lands_a)
        h1, s1 = _ffn_fwd("ffn1", h, W['ffn1_norm'], W['ffn1_wg'], W['ffn1_wu'], W['ffn1_wd'])
        lands_b = _xchg_end(f"gather_end{l}b", srcs[1], flight[1][1], flight[1][0], l, h1)
        stage_weights(W, 1, l, lands_b)
        tok = None
        if l + 1 < depth:
            sems, lands, tok = _xchg_begin(f"gather_begin{l + 1}a", srcs[0], l + 1, lands_b[0])
            flight[0] = (sems, lands)
            sems, lands, tok = _xchg_begin(f"gather_begin{l + 1}b", srcs[1], l + 1, tok)
            flight[1] = (sems, lands)
        h2, s2 = _mixer_fwd(h1, W, dims, dep=tok)
        h3, s3 = _ffn_fwd("ffn2", h2, W['ffn2_norm'], W['ffn2_wg'], W['ffn2_wu'], W['ffn2_wd'])
        h4, s4 = _ple_layer_fwd(h3, p[l, 0], W)
        saved.append((h, h1, h2, h3, s1, s2, s3, s4))
        h = h4
        if l + 1 < depth:
            lands_a = _xchg_end(f"gather_end{l + 1}a", srcs[0], flight[0][1], flight[0][0], l + 1, h)

    dh, loss_row, d_final = _loss_head("loss_head", h, final_norm, loss_target[0])
    loss = lax.psum(loss_row[0, 0], ("x", "y", "c"))

    def send_bufs(g, s):
        return [jnp.concatenate([_shards_from_full(g[n]) for n in ms], axis=1) if len(ms) > 1
                else _shards_from_full(g[ms[0]]) for ms in STAGES[s]]

    grads = [None] * depth
    pending = []
    tok = loss.reshape(1, 1)
    for l in reversed(range(depth)):
        W = layers[l]
        h0, h1, h2, h3, s1, s2, s3, s4 = saved[l]
        g = {}
        dh, g4 = _ple_layer_bwd(dh, h3, p[l, 0], W, s4, dep=tok)
        g.update(g4)
        dh, dg, d_wg, d_wu, d_wd = _ffn_bwd("ffn2", dh, h2, W['ffn2_norm'], W['ffn2_wg'], W['ffn2_wu'], W['ffn2_wd'], s3)
        g.update(ffn2_norm=dg.reshape(-1), ffn2_wg=d_wg, ffn2_wu=d_wu, ffn2_wd=d_wd)
        dh, g2 = _mixer_bwd(dh, h1, W, dims, s2)
        g.update(g2)
        send = send_bufs(g, 1)
        sems, lands, tok = _xchg_begin(f"scatter_begin{l}b", send, None, dh)
        pending.append((l, 1, send, lands, sems))
        dh, dg, d_wg, d_wu, d_wd = _ffn_bwd("ffn1", dh, h0, W['ffn1_norm'], W['ffn1_wg'], W['ffn1_wu'], W['ffn1_wd'], s1, dep=tok)
        g.update(ffn1_norm=dg.reshape(-1), ffn1_wg=d_wg, ffn1_wu=d_wu, ffn1_wd=d_wd)
        grads[l] = g
        send = send_bufs(g, 0)
        sems, lands, tok = _xchg_begin(f"scatter_begin{l}a", send, None, dh)
        pending.append((l, 0, send, lands, sems))
    grad_x = dh[None]

    g_lands = [[None, None] for _ in range(depth)]
    after = dh
    for l, s, send, lands, sems in pending:
        got = _xchg_end(f"scatter_end{l}{'ab'[s]}", send, lands, sems, None, after)
        after = got[0]
        g_lands[l][s] = [lax.dynamic_update_slice(o, lax.dynamic_slice_in_dim(b, me, 1, axis=0), (me, 0, 0)) for o, b in zip(got, send)]

    big_res = [{}, {}, {}, {}]
    for s, stage in enumerate(STAGES):
        for gi, ms in enumerate(stage):
            off = 0
            for n in ms:
                res = _sum_adam("adamw_" + n, [g_lands[l][s][gi] for l in range(depth)], off, work(n), work(n, 'm_'), work(n, 'v_'))
                for k in range(4):
                    big_res[k][n] = jnp.swapaxes(res[k], 1, 2) if kind[n] == 'col' else res[k]
                off += wb[n].shape[1]

    small_names = SMALL + CONVW
    small_parts = [jnp.stack([grads[l][n] for l in range(depth)]) for n in small_names] + [d_final.reshape(-1)]
    small_sum = _sum8("sum_small", _exchange("gather_small_grads", _pack(small_parts), True, dep=after))
    sg = dict(zip(small_names + ['final_norm'], _unpack(small_sum, [a.shape for a in small_parts])))
    for n in CONVW:
        c = A[n].shape[-1]
        sg[n] = lax.dynamic_slice_in_dim(sg[n], me * c, c, axis=2)
    s_order = small_names + ['final_norm']
    s_shapes = [sg[n].shape for n in s_order]
    s_out = _adam_flat("adamw_small", _pack([sg[n] for n in s_order]), _pack([A[n] for n in s_order]),
                       _pack([A['m_' + n] for n in s_order]), _pack([A['v_' + n] for n in s_order]))
    small_res = [sg] + [dict(zip(s_order, _unpack(flat, s_shapes))) for flat in s_out]

    outs = [loss, grad_x]
    for k in range(4):
        for n in WEIGHTS:
            outs.append(big_res[k][n] if n in big_res[k] else small_res[k][n])
    return tuple(outs)
```

```python
import functools

import jax
import jax.numpy as jnp
from jax import lax
from jax.experimental import pallas as pl
from jax.experimental.pallas import tpu as pltpu

BF = jnp.bfloat16
F32 = jnp.float32

EPS = 1e-6
N_DEV = 8
LANES = 128
SSM_GROUPS = 4
SSM_HEADDIM = 64
SSM_CHUNK = 128
HALO = 16
VMEM_LIMIT = 56 * 1024 * 1024
FLAT_ROW_TILE = 2048

ADAM_LR = 0.001
ADAM_B1 = 0.9
ADAM_B2 = 0.999
ADAM_EPS = 1e-08
ADAM_WD = 0.01
ADAM_STEP = 10

MESH = pl.DeviceIdType.MESH

ARG_NAMES = ['x', 'p', 'ffn1_norm', 'ffn1_wg', 'ffn1_wu', 'ffn1_wd', 'mix_norm', 'w_in', 'sc_conv_w', 'sc_w_out', 'm_conv_w', 'm_conv_b', 'm_dt_bias', 'm_A_log', 'm_D', 'm_norm', 'm_w_out', 'w_o', 'ffn2_norm', 'ffn2_wg', 'ffn2_wu', 'ffn2_wd', 'ple_norm', 'ple_w_gate', 'ple_w_proj', 'final_norm', 'loss_target']
WEIGHTS = ARG_NAMES[2:26]
BIG = [('ffn1_wg', 'col'), ('ffn1_wu', 'col'), ('ffn1_wd', 'row'), ('w_in', 'col'), ('sc_w_out', 'row'),
       ('m_w_out', 'row'), ('w_o', 'row'), ('ffn2_wg', 'col'), ('ffn2_wu', 'col'), ('ffn2_wd', 'row'),
       ('ple_w_gate', 'row'), ('ple_w_proj', 'col')]
CONVW = ['sc_conv_w', 'm_conv_w']
SMALL = ['ffn1_norm', 'mix_norm', 'm_conv_b', 'm_dt_bias', 'm_A_log', 'm_D', 'm_norm', 'ffn2_norm', 'ple_norm']


def _pick(n, cands):
    for c in cands:
        if n % c == 0:
            return c
    return n


def _cp(sem):
    return pltpu.CompilerParams(dimension_semantics=sem, vmem_limit_bytes=VMEM_LIMIT)


def _sigmoid(x):
    return 1.0 / (1.0 + jnp.exp(-x))


def _softplus(x):
    return jnp.maximum(x, 0.0) + jnp.log(1.0 + jnp.exp(-jnp.abs(x)))


def _exchange(name, x, gather, dep=None):
    slab = x.shape if gather else x.shape[1:]

    def body(x_ref, *rest):
        o_ref, send_sems, recv_sems, local_sem = rest[-4:]
        mx, my, mc = lax.axis_index("x"), lax.axis_index("y"), lax.axis_index("c")
        me = 4 * mx + 2 * my + mc

        def src_for(k):
            return x_ref if gather else x_ref.at[k]

        local = pltpu.make_async_copy(src_for(me), o_ref.at[me], local_sem)
        local.start()
        sends = []
        peers = []
        for r in range(1, N_DEV):
            px = (mx + ((r >> 2) & 1)) % 2
            py = (my + ((r >> 1) & 1)) % 2
            pc = (mc + (r & 1)) % 2
            peer = 4 * px + 2 * py + pc
            peers.append(peer)
            cp = pltpu.make_async_remote_copy(
                src_ref=src_for(peer), dst_ref=o_ref.at[me], send_sem=send_sems.at[r - 1], recv_sem=recv_sems.at[r - 1],
                device_id=(px, py, pc), device_id_type=MESH)
            cp.start()
            sends.append(cp)
        for r in range(1, N_DEV):
            peer = peers[r - 1]
            pltpu.make_async_remote_copy(
                src_ref=src_for(peer), dst_ref=o_ref.at[peer], send_sem=send_sems.at[r - 1], recv_sem=recv_sems.at[r - 1],
                device_id=(mx, my, mc), device_id_type=MESH).wait_recv()
        for cp in sends:
            cp.wait_send()
        local.wait()

    return pl.pallas_call(
        body, name=name,
        out_shape=jax.ShapeDtypeStruct((N_DEV,) + tuple(slab), x.dtype),
        in_specs=[pl.BlockSpec(memory_space=pltpu.HBM)] + ([] if dep is None else [pl.BlockSpec(memory_space=pl.ANY)]),
        out_specs=pl.BlockSpec(memory_space=pltpu.HBM),
        scratch_shapes=[pltpu.SemaphoreType.DMA((N_DEV - 1,)), pltpu.SemaphoreType.DMA((N_DEV - 1,)), pltpu.SemaphoreType.DMA],
    )(*([x] if dep is None else [x, dep]))


STAGES = [[['ffn1_wd'], ['ffn1_wg'], ['ffn1_wu']],
          [['w_in'], ['sc_w_out', 'w_o', 'ple_w_gate', 'm_w_out']],
          [['ffn2_wd'], ['ffn2_wg'], ['ffn2_wu'], ['ple_w_proj']]]
_HBM = pl.BlockSpec(memory_space=pltpu.HBM)
_SEM = pl.BlockSpec(memory_space=pltpu.SEMAPHORE)
_ANY = pl.BlockSpec(memory_space=pl.ANY)
_EFFECT = pltpu.SideEffectType.DATAFLOW_SIDE_EFFECTING


def _peer_list():
    mx, my, mc = lax.axis_index("x"), lax.axis_index("y"), lax.axis_index("c")
    out = []
    for r in range(1, N_DEV):
        px = (mx + ((r >> 2) & 1)) % 2
        py = (my + ((r >> 1) & 1)) % 2
        pc = (mc + (r & 1)) % 2
        out.append((px, py, pc, 4 * px + 2 * py + pc))
    return 4 * mx + 2 * my + mc, out


def _xchg_copy(src_refs, land_refs, send_sems, recv_sems, layer, i, r, peer, dst_slab):
    px, py, pc, pidx = peer
    n = len(src_refs)
    src = src_refs[i].at[layer] if layer is not None else src_refs[i].at[pidx]
    return pltpu.make_async_remote_copy(
        src_ref=src, dst_ref=land_refs[i].at[dst_slab], send_sem=send_sems.at[r * n + i], recv_sem=recv_sems.at[r * n + i],
        device_id=(px, py, pc), device_id_type=MESH)


def _xchg_begin(name, srcs, layer, dep):
    n = len(srcs)
    slabs = [tuple(s.shape[1:]) for s in srcs]
    ncp = n * (N_DEV - 1)

    def body(*refs):
        src_refs, land_refs = refs[:n], refs[n:2 * n]
        send_sems, recv_sems = refs[2 * n + 1], refs[2 * n + 2]
        token = refs[-1]
        me, peers = _peer_list()
        for r, peer in enumerate(peers):
            for i in range(n):
                _xchg_copy(src_refs, land_refs, send_sems, recv_sems, layer, i, r, peer, me).start()
        token[...] = jnp.zeros_like(token)

    lands = [pltpu.with_memory_space_constraint(lax.empty((N_DEV,) + sl, s.dtype), pltpu.HBM) for sl, s in zip(slabs, srcs)]
    out = pl.pallas_call(
        body, name=name,
        out_shape=(pltpu.SemaphoreType.DMA((ncp,)), pltpu.SemaphoreType.DMA((ncp,)),
                   *[pltpu.HBM((N_DEV,) + sl, s.dtype) for sl, s in zip(slabs, srcs)], jax.ShapeDtypeStruct((8, LANES), F32)),
        in_specs=[_HBM] * (2 * n) + [_ANY],
        out_specs=(_SEM, _SEM, *[_HBM] * n, pl.BlockSpec(memory_space=pltpu.VMEM)),
        input_output_aliases={n + i: 2 + i for i in range(n)},
        compiler_params=pltpu.CompilerParams(has_side_effects=_EFFECT),
    )(*[pltpu.with_memory_space_constraint(s, pltpu.HBM) for s in srcs], *lands, dep)
    return (out[0], out[1]), list(out[2:2 + n]), out[-1]


def _xchg_end(name, srcs, lands, sems, layer, after):
    n = len(srcs)

    def body(*refs):
        src_refs, land_refs = refs[:n], refs[n:2 * n]
        send_sems, recv_sems = refs[2 * n], refs[2 * n + 1]
        me, peers = _peer_list()
        for r, peer in enumerate(peers):
            for i in range(n):
                cp = _xchg_copy(src_refs, land_refs, send_sems, recv_sems, layer, i, r, peer, peer[3])
                cp.wait_send()
                cp.wait_recv()

    out = pl.pallas_call(
        body, name=name,
        out_shape=tuple(pltpu.HBM(l.shape, l.dtype) for l in lands),
        in_specs=[_HBM] * (2 * n) + [_SEM, _SEM, _ANY], out_specs=tuple([_HBM] * n),
        input_output_aliases={n + i: i for i in range(n)},
        compiler_params=pltpu.CompilerParams(has_side_effects=_EFFECT),
    )(*[pltpu.with_memory_space_constraint(s, pltpu.HBM) for s in srcs], *lands, sems[0], sems[1], after)
    return list(out)


MM_TILES = (1024, 1408, 512, 256, 128)
MM_OPERAND_BYTES = 24 * 1024 * 1024


def _mm(name, a, b, *, ta=False, tb=False, out_dtype=None, res=None, alpha=1.0, dep=None):
    out_dtype = out_dtype or BF
    M, K = (a.shape[1], a.shape[0]) if ta else a.shape
    N = b.shape[0] if tb else b.shape[1]
    assert (b.shape[1] if tb else b.shape[0]) == K, (name, a.shape, b.shape)
    tm = _pick(M, MM_TILES)
    tn = _pick(N, MM_TILES)
    per_k = 2 * (tm * a.dtype.itemsize + tn * b.dtype.itemsize)
    tk = [t for t in sorted({K, 4096, 2816, 2560, 2048, 1408, 1024, 512, 256, 128}, reverse=True)
          if K % t == 0 and (t * per_k <= MM_OPERAND_BYTES or t == 128)][0]
    nk = K // tk
    a_spec = pl.BlockSpec((tk, tm), lambda i, j, k: (k, i)) if ta else pl.BlockSpec((tm, tk), lambda i, j, k: (i, k))
    b_spec = pl.BlockSpec((tn, tk), lambda i, j, k: (j, k)) if tb else pl.BlockSpec((tk, tn), lambda i, j, k: (k, j))
    dn = (((0 if ta else 1,), (1 if tb else 0,)), ((), ()))
    has_res = res is not None
    n_dep = 0 if dep is None else 1

    def body(*refs):
        a_ref, b_ref = refs[:2]
        r_ref = refs[2] if has_res else None
        o_ref = refs[2 + has_res + n_dep]

        def finish(v):
            if alpha != 1.0:
                v = v * alpha
            if has_res:
                v = r_ref[...] + v
            o_ref[...] = v.astype(o_ref.dtype)

        part = lax.dot_general(a_ref[...].astype(BF), b_ref[...].astype(BF), dn, preferred_element_type=F32)
        if nk == 1:
            finish(part)
            return
        acc = refs[-1]
        k = pl.program_id(2)

        @pl.when(k == 0)
        def _():
            acc[...] = part

        @pl.when((k > 0) & (k < nk - 1))
        def _():
            acc[...] += part

        @pl.when(k == nk - 1)
        def _():
            finish(acc[...] + part)

    in_specs = [a_spec, b_spec]
    args = [a, b]
    if has_res:
        in_specs.append(pl.BlockSpec((tm, tn), lambda i, j, k: (i, j)))
        args.append(res)
    if dep is not None:
        in_specs.append(_ANY)
        args.append(dep)
    return pl.pallas_call(
        body, name=name, grid=(M // tm, N // tn, nk),
        in_specs=in_specs, out_specs=pl.BlockSpec((tm, tn), lambda i, j, k: (i, j)),
        out_shape=jax.ShapeDtypeStruct((M, N), out_dtype),
        scratch_shapes=[pltpu.VMEM((tm, tn), F32)] if nk > 1 else [],
        compiler_params=_cp(("parallel", "parallel", "arbitrary")),
    )(*args)


def _ew(name, fn, tiled, params, outs, accs=(), tile=256, dep=None):
    tiled = [t if isinstance(t, tuple) else (t, t.shape[1], 0) for t in tiled]
    params = [q if isinstance(q, tuple) else (q, None) for q in params]
    S = tiled[0][0].shape[0]
    T = _pick(S, (tile, 128, 64, 32, 16))
    n_in = len(tiled) + len(params)
    n_dep = 0 if dep is None else 1

    def body(*refs):
        fn(pl.program_id(0) == 0, *refs[:n_in], *refs[n_in + n_dep:])

    in_specs = [pl.BlockSpec((T, w), lambda i, cb=cb: (i, cb)) for _, w, cb in tiled]
    for q, row in params:
        if row is None:
            in_specs.append(pl.BlockSpec(q.shape, lambda i: (0, 0)))
        else:
            in_specs.append(pl.BlockSpec((None, 1, q.shape[2]), lambda i, row=row: (row, 0, 0)))
    args = [t[0] for t in tiled] + [q[0] for q in params]
    if dep is not None:
        in_specs.append(pl.BlockSpec(memory_space=pl.ANY))
        args.append(dep)
    out_specs = [pl.BlockSpec((T, w), lambda i: (i, 0)) for w, _ in outs]
    out_specs += [pl.BlockSpec(shp, lambda i: (0, 0)) for shp, _ in accs]
    out_shape = [jax.ShapeDtypeStruct((S, w), dt) for w, dt in outs]
    out_shape += [jax.ShapeDtypeStruct(shp, dt) for shp, dt in accs]
    res = pl.pallas_call(
        body, name=name, grid=(S // T,), in_specs=in_specs, out_specs=out_specs, out_shape=out_shape,
        compiler_params=_cp(("arbitrary",)),
    )(*args)
    return res


def _prow(g):
    return g if isinstance(g, tuple) else g.reshape(1, -1)


def _rms_fwd(name, h, g, dep=None):
    def fn(first, h_ref, g_ref, o_ref):
        x = h_ref[...]
        r = lax.rsqrt(jnp.mean(x * x, axis=-1, keepdims=True) + EPS)
        o_ref[...] = (x * r * g_ref[...]).astype(o_ref.dtype)

    return _ew(name, fn, [h], [_prow(g)], [(h.shape[1], BF)], dep=dep)[0]


def _rms_bwd(name, dxn, h, g, res):
    D = h.shape[1]

    def fn(first, d_ref, h_ref, r_ref, g_ref, o_ref, dg_ref):
        x = h_ref[...]
        d = d_ref[...].astype(F32)
        r = lax.rsqrt(jnp.mean(x * x, axis=-1, keepdims=True) + EPS)
        xhat = x * r
        dxhat = d * g_ref[...]
        dh = r * (dxhat - xhat * jnp.mean(dxhat * xhat, axis=-1, keepdims=True))
        o_ref[...] = r_ref[...] + dh

        @pl.when(first)
        def _():
            dg_ref[...] = jnp.zeros_like(dg_ref)

        dg_ref[...] += jnp.sum(d * xhat, axis=0, keepdims=True)

    return _ew(name, fn, [dxn, h, res], [_prow(g)], [(D, F32)], [((1, D), F32)])


FFN_TOKEN_TILE = 512


def _ffn_up(name, xn, wgT, wuT, dep=None):
    S, D = xn.shape
    FF = wgT.shape[0]
    tm = _pick(S, (FFN_TOKEN_TILE, 256, 128))
    tn = _pick(FF, MM_TILES)
    n_dep = 0 if dep is None else 1

    def body(x_ref, g_ref, u_ref, *rest):
        a_ref, b_ref, h_ref = rest[n_dep:]
        x = x_ref[...]
        a = _dot_nt(x, g_ref[...])
        b = _dot_nt(x, u_ref[...])
        a_ref[...] = a.astype(BF)
        b_ref[...] = b.astype(BF)
        h_ref[...] = (a * _sigmoid(a) * b).astype(BF)

    wspec = pl.BlockSpec((tn, D), lambda j, i: (j, 0))
    ospec = pl.BlockSpec((tm, tn), lambda j, i: (i, j))
    return pl.pallas_call(
        body, name=name, grid=(FF // tn, S // tm),
        in_specs=[pl.BlockSpec((tm, D), lambda j, i: (i, 0)), wspec, wspec] + ([] if dep is None else [_ANY]),
        out_specs=[ospec] * 3, out_shape=[jax.ShapeDtypeStruct((S, FF), BF)] * 3,
        compiler_params=_cp(("parallel", "arbitrary")),
    )(*([xn, wgT, wuT] + ([] if dep is None else [dep])))


def _ffn_dact(name, dh, wd, a, b, dep=None):
    S, D = dh.shape
    FF = wd.shape[0]
    tm = _pick(S, (FFN_TOKEN_TILE, 256, 128))
    tn = _pick(FF, MM_TILES)
    n_dep = 0 if dep is None else 1

    def body(d_ref, w_ref, a_ref, b_ref, *rest):
        da_ref, db_ref = rest[n_dep:]
        d = 0.5 * _dot_nt(d_ref[...].astype(BF), w_ref[...])
        av = a_ref[...].astype(F32)
        s = _sigmoid(av)
        da_ref[...] = (d * b_ref[...].astype(F32) * (s * (1.0 + av * (1.0 - s)))).astype(BF)
        db_ref[...] = (d * av * s).astype(BF)

    tspec = pl.BlockSpec((tm, tn), lambda j, i: (i, j))
    return pl.pallas_call(
        body, name=name, grid=(FF // tn, S // tm),
        in_specs=[pl.BlockSpec((tm, D), lambda j, i: (i, 0)), pl.BlockSpec((tn, D), lambda j, i: (j, 0)), tspec, tspec]
        + ([] if dep is None else [_ANY]),
        out_specs=[tspec] * 2, out_shape=[jax.ShapeDtypeStruct((S, FF), BF)] * 2,
        compiler_params=_cp(("parallel", "arbitrary")),
    )(*([dh, wd, a, b] + ([] if dep is None else [dep])))


def _merge_fwd(name, ga, gm, ya, ym):
    def fn(first, ga_ref, gm_ref, ya_ref, ym_ref, o_ref):
        o = _sigmoid(ga_ref[...].astype(F32)) * ya_ref[...].astype(F32) + _sigmoid(gm_ref[...].astype(F32)) * ym_ref[...].astype(F32)
        o_ref[...] = o.astype(o_ref.dtype)

    return _ew(name, fn, [ga, gm, ya, ym], [], [(ya.shape[1], BF)])[0]


def _merge_bwd(name, dmerged, ga, gm, ya, ym):
    W = ya.shape[1]

    def fn(first, d_ref, ga_ref, gm_ref, ya_ref, ym_ref, dga_ref, dgm_ref, dya_ref, dym_ref):
        d = d_ref[...].astype(F32)
        sa = _sigmoid(ga_ref[...].astype(F32))
        sm = _sigmoid(gm_ref[...].astype(F32))
        dga_ref[...] = (d * ya_ref[...].astype(F32) * sa * (1.0 - sa)).astype(BF)
        dgm_ref[...] = (d * ym_ref[...].astype(F32) * sm * (1.0 - sm)).astype(BF)
        dya_ref[...] = (d * sa).astype(BF)
        dym_ref[...] = (d * sm).astype(BF)

    return _ew(name, fn, [dmerged, ga, gm, ya, ym], [], [(W, BF)] * 4)


def _gnorm_fwd(name, y, z, w):
    W = y.shape[1]
    gw = W // SSM_GROUPS

    def fn(first, y_ref, z_ref, w_ref, o_ref):
        for g in range(SSM_GROUPS):
            sl = slice(g * gw, (g + 1) * gw)
            zz = z_ref[:, sl].astype(F32)
            t = y_ref[:, sl].astype(F32) * (zz * _sigmoid(zz))
            r = lax.rsqrt(jnp.mean(t * t, axis=-1, keepdims=True) + EPS)
            o_ref[:, sl] = (t * r * w_ref[:, sl]).astype(o_ref.dtype)

    return _ew(name, fn, [y, z], [_prow(w)], [(W, BF)])[0]


def _gnorm_bwd(name, dyn, y, z, w):
    W = y.shape[1]
    gw = W // SSM_GROUPS

    def fn(first, d_ref, y_ref, z_ref, w_ref, dy_ref, dz_ref, dw_ref):
        @pl.when(first)
        def _():
            dw_ref[...] = jnp.zeros_like(dw_ref)

        for g in range(SSM_GROUPS):
            sl = slice(g * gw, (g + 1) * gw)
            zz = z_ref[:, sl].astype(F32)
            yy = y_ref[:, sl].astype(F32)
            d = d_ref[:, sl].astype(F32)
            s = _sigmoid(zz)
            sz = zz * s
            t = yy * sz
            r = lax.rsqrt(jnp.mean(t * t, axis=-1, keepdims=True) + EPS)
            that = t * r
            dthat = d * w_ref[:, sl]
            dt = r * (dthat - that * jnp.mean(dthat * that, axis=-1, keepdims=True))
            dw_ref[:, sl] += jnp.sum(d * that, axis=0, keepdims=True)
            dy_ref[:, sl] = (dt * sz).astype(BF)
            dz_ref[:, sl] = (dt * yy * (s * (1.0 + zz * (1.0 - s)))).astype(BF)

    return _ew(name, fn, [dyn, y, z], [_prow(w)], [(W, BF), (W, BF)], [((1, W), F32)])


def _ple_fwd(name, h, gpre, pp):
    def fn(first, h_ref, g_ref, p_ref, o_ref):
        o_ref[...] = h_ref[...] + _sigmoid(g_ref[...].astype(F32)) * p_ref[...].astype(F32)

    return _ew(name, fn, [h, gpre, pp], [], [(h.shape[1], F32)])[0]


def _ple_bwd(name, dh, gpre, pp, dep=None):
    W = dh.shape[1]

    def fn(first, d_ref, g_ref, p_ref, dg_ref, dp_ref):
        d = d_ref[...]
        s = _sigmoid(g_ref[...].astype(F32))
        dg_ref[...] = (d * p_ref[...].astype(F32) * s * (1.0 - s)).astype(BF)
        dp_ref[...] = (d * s).astype(BF)

    return _ew(name, fn, [dh, gpre, pp], [], [(W, BF), (W, BF)], dep=dep)


def _loss_head(name, h, g, target):
    D = h.shape[1]

    def fn(first, h_ref, t_ref, g_ref, dh_ref, loss_ref, dg_ref):
        x = h_ref[...]
        r = lax.rsqrt(jnp.mean(x * x, axis=-1, keepdims=True) + EPS)
        xhat = x * r
        err = xhat * g_ref[...] - t_ref[...]
        part = 0.5 * jnp.sum(jnp.mean(err * err, axis=-1, keepdims=True), axis=0, keepdims=True)
        dy = err * (1.0 / D)
        dxhat = dy * g_ref[...]
        dh_ref[...] = r * (dxhat - xhat * jnp.mean(dxhat * xhat, axis=-1, keepdims=True))

        @pl.when(first)
        def _():
            loss_ref[...] = jnp.zeros_like(loss_ref)
            dg_ref[...] = jnp.zeros_like(dg_ref)

        loss_ref[...] += jnp.broadcast_to(part, loss_ref.shape)
        dg_ref[...] += jnp.sum(dy * xhat, axis=0, keepdims=True)

    return _ew(name, fn, [h, target], [_prow(g)], [(D, F32)], [((1, LANES), F32), ((1, D), F32)])


def _conv_specs(S, C, offs, l):
    T = _pick(S, (512, 256, 128, 64, 32, 16))
    Ct = [c for c in (512, 256, 128) if C % c == 0 and all(o % c == 0 for o in offs)][0]
    per = T // HALO
    last = S // HALO - 1

    def cur(off=0):
        return pl.BlockSpec((T, Ct), lambda j, i: (i, off // Ct + j))

    def prev(off=0):
        return pl.BlockSpec((HALO, Ct), lambda j, i: (jnp.maximum(i * per - 1, 0), off // Ct + j))

    def nxt(off=0):
        return pl.BlockSpec((HALO, Ct), lambda j, i: (jnp.minimum((i + 1) * per, last), off // Ct + j))

    wspec = pl.BlockSpec((None, 8, Ct), lambda j, i: (l, 0, j))
    return T, Ct, cur, prev, nxt, wspec


def _pad_taps(w):
    return jnp.concatenate([w.astype(F32), jnp.zeros((w.shape[0], 8 - w.shape[1], w.shape[2]), F32)], axis=1)


def _causal(cat, w_ref, K, T, lead):
    out = None
    for k in range(K):
        o = lead - (K - 1) + k
        term = w_ref[k:k + 1, :] * cat[o:o + T]
        out = term if out is None else out + term
    return out


def _anticausal(cat, w_ref, K, T):
    out = None
    for k in range(K):
        o = K - 1 - k
        term = w_ref[k:k + 1, :] * cat[o:o + T]
        out = term if out is None else out + term
    return out


def _scconv_fwd(name, proj, ob, oc, ox, taps, K, l):
    S = proj.shape[0]
    C = taps.shape[2]
    T, Ct, cur, prev, nxt, wspec = _conv_specs(S, C, (ob, oc, ox), l)

    def body(b_ref, c_ref, x_ref, cp_ref, xp_ref, w_ref, o_ref):
        i = pl.program_id(1)
        q = c_ref[...].astype(F32) * x_ref[...].astype(F32)
        qp = jnp.where(i == 0, 0.0, cp_ref[...].astype(F32) * xp_ref[...].astype(F32))
        cat = jnp.concatenate([qp, q], axis=0)
        o_ref[...] = (b_ref[...].astype(F32) * _causal(cat, w_ref, K, T, HALO)).astype(o_ref.dtype)

    return pl.pallas_call(
        body, name=name, grid=(C // Ct, S // T),
        in_specs=[cur(ob), cur(oc), cur(ox), prev(oc), prev(ox), wspec], out_specs=cur(),
        out_shape=jax.ShapeDtypeStruct((S, C), BF), compiler_params=_cp(("parallel", "arbitrary")),
    )(proj, proj, proj, proj, proj, taps)


def _scconv_bwd(name, dv, proj, ob, oc, ox, taps, K, l):
    S = proj.shape[0]
    C = taps.shape[2]
    T, Ct, cur, prev, nxt, wspec = _conv_specs(S, C, (ob, oc, ox), l)
    n_t = S // T

    def body(d_ref, b_ref, c_ref, x_ref, dn_ref, bn_ref, cp_ref, xp_ref, w_ref, db_ref, dc_ref, dx_ref, dw_ref):
        i = pl.program_id(1)
        c = c_ref[...].astype(F32)
        x = x_ref[...].astype(F32)
        d = d_ref[...].astype(F32)
        q = c * x
        qp = jnp.where(i == 0, 0.0, cp_ref[...].astype(F32) * xp_ref[...].astype(F32))
        catq = jnp.concatenate([qp, q], axis=0)
        cv = _causal(catq, w_ref, K, T, HALO)
        db_ref[...] = (d * cv).astype(BF)
        dcv = d * b_ref[...].astype(F32)
        dcvn = jnp.where(i == n_t - 1, 0.0, dn_ref[...].astype(F32) * bn_ref[...].astype(F32))
        catd = jnp.concatenate([dcv, dcvn], axis=0)
        dq = _anticausal(catd, w_ref, K, T)
        dc_ref[...] = (dq * x).astype(BF)
        dx_ref[...] = (dq * c).astype(BF)

        @pl.when(i == 0)
        def _():
            dw_ref[...] = jnp.zeros_like(dw_ref)

        for k in range(K):
            o = HALO - (K - 1) + k
            dw_ref[k:k + 1, :] += jnp.sum(dcv * catq[o:o + T], axis=0, keepdims=True)

    return pl.pallas_call(
        body, name=name, grid=(C // Ct, n_t),
        in_specs=[cur(), cur(ob), cur(oc), cur(ox), nxt(), nxt(ob), prev(oc), prev(ox), wspec],
        out_specs=[cur(), cur(), cur(), pl.BlockSpec((8, Ct), lambda j, i: (0, j))],
        out_shape=[jax.ShapeDtypeStruct((S, C), BF)] * 3 + [jax.ShapeDtypeStruct((8, C), F32)],
        compiler_params=_cp(("parallel", "arbitrary")),
    )(dv, proj, proj, proj, dv, proj, proj, proj, taps)


def _mconv_fwd(name, proj, ox, taps, K, bias, l):
    S = proj.shape[0]
    C = taps.shape[2]
    T, Ct, cur, prev, nxt, wspec = _conv_specs(S, C, (ox,), l)
    bspec = pl.BlockSpec((None, 1, Ct), lambda j, i: (l, 0, j))

    def body(x_ref, xp_ref, w_ref, b_ref, o_ref):
        i = pl.program_id(1)
        xp = jnp.where(i == 0, 0.0, xp_ref[...].astype(F32))
        cat = jnp.concatenate([xp, x_ref[...].astype(F32)], axis=0)
        pre = _causal(cat, w_ref, K, T, HALO) + b_ref[...]
        o_ref[...] = (pre * _sigmoid(pre)).astype(o_ref.dtype)

    return pl.pallas_call(
        body, name=name, grid=(C // Ct, S // T),
        in_specs=[cur(ox), prev(ox), wspec, bspec], out_specs=cur(),
        out_shape=jax.ShapeDtypeStruct((S, C), BF), compiler_params=_cp(("parallel", "arbitrary")),
    )(proj, proj, taps, bias)


def _mconv_bwd(name, dout, proj, ox, taps, K, bias, l):
    S = proj.shape[0]
    C = taps.shape[2]
    T, Ct, cur, prev, nxt, wspec = _conv_specs(S, C, (ox,), l)
    n_t = S // T
    bspec = pl.BlockSpec((None, 1, Ct), lambda j, i: (l, 0, j))

    def body(d_ref, dn_ref, x_ref, xp_ref, xn_ref, w_ref, b_ref, dx_ref, dw_ref, db_ref):
        i = pl.program_id(1)
        xp = jnp.where(i == 0, 0.0, xp_ref[...].astype(F32))
        cat3 = jnp.concatenate([xp, x_ref[...].astype(F32), xn_ref[...].astype(F32)], axis=0)
        pre = _causal(cat3, w_ref, K, T + HALO, HALO) + b_ref[...]
        dn = jnp.where(i == n_t - 1, 0.0, dn_ref[...].astype(F32))
        dext = jnp.concatenate([d_ref[...].astype(F32), dn], axis=0)
        s = _sigmoid(pre)
        dpre = dext * (s * (1.0 + pre * (1.0 - s)))
        dx_ref[...] = _anticausal(dpre, w_ref, K, T).astype(BF)
        dcur = dpre[:T]

        @pl.when(i == 0)
        def _():
            dw_ref[...] = jnp.zeros_like(dw_ref)
            db_ref[...] = jnp.zeros_like(db_ref)

        db_ref[...] += jnp.sum(dcur, axis=0, keepdims=True)
        for k in range(K):
            o = HALO - (K - 1) + k
            dw_ref[k:k + 1, :] += jnp.sum(dcur * cat3[o:o + T], axis=0, keepdims=True)

    return pl.pallas_call(
        body, name=name, grid=(C // Ct, n_t),
        in_specs=[cur(), nxt(), cur(ox), prev(ox), nxt(ox), wspec, bspec],
        out_specs=[cur(), pl.BlockSpec((8, Ct), lambda j, i: (0, j)), pl.BlockSpec((1, Ct), lambda j, i: (0, j))],
        out_shape=[jax.ShapeDtypeStruct((S, C), BF), jax.ShapeDtypeStruct((8, C), F32), jax.ShapeDtypeStruct((1, C), F32)],
        compiler_params=_cp(("parallel", "arbitrary")),
    )(dout, dout, proj, proj, proj, taps, bias)


def _tri_matmul(tri_bf, v):
    hi = v.astype(BF)
    r1 = v - hi.astype(F32)
    mid = r1.astype(BF)
    lo = (r1 - mid.astype(F32)).astype(BF)
    dot = functools.partial(jnp.dot, preferred_element_type=F32)
    return dot(tri_bf, hi) + dot(tri_bf, mid) + dot(tri_bf, lo)


def _dot_nt(a, b):
    return lax.dot_general(a, b, (((1,), (1,)), ((), ())), preferred_element_type=F32)


def _dot_tn(a, b):
    return lax.dot_general(a, b, (((0,), (0,)), ((), ())), preferred_element_type=F32)


def _dot_nn(a, b):
    return jnp.dot(a, b, preferred_element_type=F32)


def _ssd_chunk_scalars(dtr_ref, par_ref, L):
    row_i = lax.broadcasted_iota(jnp.int32, (L, L), 0)
    col_i = lax.broadcasted_iota(jnp.int32, (L, L), 1)
    tri = row_i >= col_i
    pre = dtr_ref[...] + par_ref[0:1, :]
    dt_all = _softplus(pre)
    A_row = -jnp.exp(par_ref[1:2, :])
    a_all = dt_all * A_row
    acum_all = _tri_matmul(tri.astype(BF), a_all)
    return tri, pre, dt_all, A_row, a_all, acum_all, acum_all.T


def _ssd_dims(xbc, heads):
    S, conv_dim = xbc.shape
    inner = heads * SSM_HEADDIM
    N = (conv_dim - inner) // (2 * SSM_GROUPS)
    gw = inner // SSM_GROUPS
    PP = gw // LANES
    L = min(SSM_CHUNK, S)
    assert N == LANES and gw % LANES == 0 and inner % (SSM_GROUPS * N) == 0 and S % L == 0
    return S, inner, N, gw, PP, L, S // L


def _ssd_params(dt_bias, A_log, Dp):
    depth, H = dt_bias.shape
    rows = jnp.stack([dt_bias, A_log, Dp], axis=1).astype(F32)
    rows = jnp.concatenate([rows, jnp.zeros((depth, 3, LANES - H), F32)], axis=2)
    return jnp.concatenate([rows, jnp.zeros((depth, 5, LANES), F32)], axis=1)


def _ssd_fwd(name, xbc, dt_raw, par, l, heads):
    S, inner, N, gw, PP, L, nc = _ssd_dims(xbc, heads)
    G = SSM_GROUPS
    boff = inner // N

    def body(x_ref, b_ref, c_ref, dtr_ref, par_ref, y_ref, st_out_ref, st_ref):
        @pl.when(pl.program_id(0) == 0)
        def _():
            st_ref[...] = jnp.zeros_like(st_ref)

        tri, pre, dt_all, A_row, a_all, acum_all, acumT = _ssd_chunk_scalars(dtr_ref, par_ref, L)
        lane = lax.broadcasted_iota(jnp.int32, (L, LANES), 1)
        lane1 = lax.broadcasted_iota(jnp.int32, (1, LANES), 1)
        subp = lax.broadcasted_iota(jnp.int32, (LANES, 1), 0)
        lo = lane < SSM_HEADDIM
        lo1 = lane1 < SSM_HEADDIM
        for g in range(G):
            Bb = b_ref[:, g * N:(g + 1) * N]
            Cb = c_ref[:, g * N:(g + 1) * N]
            Gm = _dot_nt(Cb, Bb)
            for j in range(PP):
                pj = g * PP + j
                h0, h1 = 2 * pj, 2 * pj + 1
                cols = slice(pj * LANES, (pj + 1) * LANES)
                x = x_ref[:, cols].astype(F32)
                dt_l = jnp.where(lo, dt_all[:, h0:h0 + 1], dt_all[:, h1:h1 + 1])
                ac0 = acum_all[:, h0:h0 + 1]
                ac1 = acum_all[:, h1:h1 + 1]
                ac_l = jnp.where(lo, ac0, ac1)
                E0 = jnp.exp(jnp.where(tri, ac0 - acumT[h0:h0 + 1, :], -1e30))
                E1 = jnp.exp(jnp.where(tri, ac1 - acumT[h1:h1 + 1, :], -1e30))
                xd = x * dt_l
                xdb = xd.astype(BF)
                yd = jnp.where(lo, _dot_nn((Gm * E0).astype(BF), xdb), _dot_nn((Gm * E1).astype(BF), xdb))
                prev = st_ref[pj]
                st_out_ref[0, pj] = prev
                P = _dot_nt(Cb, prev.astype(BF))
                D_l = jnp.where(lo1, par_ref[2:3, h0:h0 + 1], par_ref[2:3, h1:h1 + 1])
                y_ref[:, cols] = (yd + P * jnp.exp(ac_l) + D_l * x).astype(y_ref.dtype)
                al0 = ac0[L - 1:L, :]
                al1 = ac1[L - 1:L, :]
                Wm = xd * jnp.exp(jnp.where(lo, al0, al1) - ac_l)
                eal = jnp.where(subp < SSM_HEADDIM, jnp.exp(al0), jnp.exp(al1))
                st_ref[pj] = eal * prev + _dot_tn(Wm.astype(BF), Bb)

    gn = G * N
    return pl.pallas_call(
        body, name=name, grid=(nc,),
        in_specs=[pl.BlockSpec((L, inner), lambda c: (c, 0)), pl.BlockSpec((L, gn), lambda c: (c, inner // gn)),
                  pl.BlockSpec((L, gn), lambda c: (c, inner // gn + 1)),
                  pl.BlockSpec((L, LANES), lambda c: (c, 0)), pl.BlockSpec((None, 8, LANES), lambda c: (l, 0, 0))],
        out_specs=[pl.BlockSpec((L, inner), lambda c: (c, 0)), pl.BlockSpec((1, G * PP, LANES, N), lambda c: (c, 0, 0, 0))],
        out_shape=[jax.ShapeDtypeStruct((S, inner), BF), jax.ShapeDtypeStruct((nc, G * PP, LANES, N), F32)],
        scratch_shapes=[pltpu.VMEM((G * PP, LANES, N), F32)],
        compiler_params=_cp(("arbitrary",)),
    )(xbc, xbc, xbc, dt_raw, par)


def _ssd_bwd(name, dy, xbc, dt_raw, states, par, l, heads):
    S, inner, N, gw, PP, L, nc = _ssd_dims(xbc, heads)
    G = SSM_GROUPS

    def body(dy_ref, x_ref, b_ref, c_ref, dtr_ref, par_ref, st_in_ref, d_ref, ddt_ref, dpar_ref, dst_ref):
        @pl.when(pl.program_id(0) == 0)
        def _():
            dst_ref[...] = jnp.zeros_like(dst_ref)
            dpar_ref[...] = jnp.zeros_like(dpar_ref)

        tri, pre, dt_all, A_row, a_all, acum_all, acumT = _ssd_chunk_scalars(dtr_ref, par_ref, L)
        lane = lax.broadcasted_iota(jnp.int32, (L, LANES), 1)
        lane1 = lax.broadcasted_iota(jnp.int32, (1, LANES), 1)
        subp = lax.broadcasted_iota(jnp.int32, (LANES, 1), 0)
        rowl = lax.broadcasted_iota(jnp.int32, (L, 1), 0)
        lo = lane < SSM_HEADDIM
        lo1 = lane1 < SSM_HEADDIM
        dac_all = jnp.zeros((L, LANES), F32)
        xds_all = jnp.zeros((L, LANES), F32)
        dD_row = jnp.zeros((1, LANES), F32)

        def rsum(v):
            return jnp.sum(v, axis=1, keepdims=True)

        def total(v):
            return jnp.sum(jnp.sum(v, axis=1, keepdims=True), axis=0, keepdims=True)

        for pj in range(G * PP):
            g, j = divmod(pj, PP)
            if j == 0:
                Bb = b_ref[:, g * N:(g + 1) * N]
                Cb = c_ref[:, g * N:(g + 1) * N]
                Gm = _dot_nt(Cb, Bb)
                dG = jnp.zeros((L, L), F32)
                dBacc = jnp.zeros((L, N), F32)
                dCacc = jnp.zeros((L, N), F32)
            h0, h1 = 2 * pj, 2 * pj + 1
            sl = slice(pj * LANES, (pj + 1) * LANES)
            x = x_ref[:, sl].astype(F32)
            dyv = dy_ref[:, sl].astype(F32)
            dt_l = jnp.where(lo, dt_all[:, h0:h0 + 1], dt_all[:, h1:h1 + 1])
            ac0 = acum_all[:, h0:h0 + 1]
            ac1 = acum_all[:, h1:h1 + 1]
            ac_l = jnp.where(lo, ac0, ac1)
            E0 = jnp.exp(jnp.where(tri, ac0 - acumT[h0:h0 + 1, :], -1e30))
            E1 = jnp.exp(jnp.where(tri, ac1 - acumT[h1:h1 + 1, :], -1e30))
            xd = x * dt_l
            xdb = xd.astype(BF)
            M0 = Gm * E0
            M1 = Gm * E1
            ea_l = jnp.exp(ac_l)
            al0 = ac0[L - 1:L, :]
            al1 = ac1[L - 1:L, :]
            dte_l = jnp.exp(jnp.where(lo, al0, al1) - ac_l)
            Wm = xd * dte_l
            prev = st_in_ref[0, pj]
            prevb = prev.astype(BF)
            P = _dot_nt(Cb, prevb)
            D_l = jnp.where(lo1, par_ref[2:3, h0:h0 + 1], par_ref[2:3, h1:h1 + 1])
            dx = D_l * dyv
            s_l = jnp.sum(dyv * x, axis=0, keepdims=True)
            dD0 = rsum(jnp.where(lo1, s_l, 0.0))
            dD1 = rsum(jnp.where(lo1, 0.0, s_l))
            dyb = dyv.astype(BF)
            dM0 = _dot_nt(jnp.where(lo, dyv, 0.0).astype(BF), xdb)
            dM1 = _dot_nt(jnp.where(lo, 0.0, dyv).astype(BF), xdb)
            dxd = jnp.where(lo, _dot_tn(M0.astype(BF), dyb), _dot_tn(M1.astype(BF), dyb))
            dG = dG + dM0 * E0 + dM1 * E1
            Q0 = dM0 * M0
            Q1 = dM1 * M1
            dac0 = rsum(Q0) - rsum(Q0.T)
            dac1 = rsum(Q1) - rsum(Q1.T)
            dP = dyv * ea_l
            dPb = dP.astype(BF)
            dCacc = dCacc + _dot_nn(dPb, prevb)
            dprev = _dot_tn(dPb, Cb)
            t = dP * P
            dac0 = dac0 + rsum(jnp.where(lo, t, 0.0))
            dac1 = dac1 + rsum(jnp.where(lo, 0.0, t))
            dnew = dst_ref[pj]
            dnewb = dnew.astype(BF)
            e0 = jnp.exp(al0)
            e1 = jnp.exp(al1)
            dprev = dprev + jnp.where(subp < SSM_HEADDIM, e0, e1) * dnew
            u = dnew * prev
            dal0 = total(jnp.where(subp < SSM_HEADDIM, u, 0.0)) * e0
            dal1 = total(jnp.where(subp < SSM_HEADDIM, 0.0, u)) * e1
            dW = _dot_nt(Bb, dnewb)
            dBacc = dBacc + _dot_nn(Wm.astype(BF), dnewb)
            dxd = dxd + dW * dte_l
            tt = dW * Wm
            t0 = rsum(jnp.where(lo, tt, 0.0))
            t1 = rsum(jnp.where(lo, 0.0, tt))
            dal0 = dal0 + jnp.sum(t0, axis=0, keepdims=True)
            dal1 = dal1 + jnp.sum(t1, axis=0, keepdims=True)
            dac0 = dac0 - t0 + jnp.where(rowl == L - 1, dal0, 0.0)
            dac1 = dac1 - t1 + jnp.where(rowl == L - 1, dal1, 0.0)
            dx = dx + dxd * dt_l
            q = dxd * x
            dst_ref[pj] = dprev
            d_ref[:, sl] = dx.astype(d_ref.dtype)
            dac_all = dac_all + jnp.where(lane == h0, dac0, 0.0) + jnp.where(lane == h1, dac1, 0.0)
            xds_all = (xds_all + jnp.where(lane == h0, rsum(jnp.where(lo, q, 0.0)), 0.0)
                       + jnp.where(lane == h1, rsum(jnp.where(lo, 0.0, q)), 0.0))
            dD_row = dD_row + jnp.where(lane1 == h0, dD0, 0.0) + jnp.where(lane1 == h1, dD1, 0.0)
            if j == PP - 1:
                dGb = dG.astype(BF)
                d_ref[:, inner + g * N:inner + (g + 1) * N] = (dBacc + _dot_tn(dGb, Cb)).astype(d_ref.dtype)
                d_ref[:, inner + (G + g) * N:inner + (G + g + 1) * N] = (dCacc + _dot_nn(dGb, Bb)).astype(d_ref.dtype)

        row_i = lax.broadcasted_iota(jnp.int32, (L, L), 0)
        col_i = lax.broadcasted_iota(jnp.int32, (L, L), 1)
        da_all = _tri_matmul((row_i <= col_i).astype(BF), dac_all)
        real = lane < heads
        ddt_all = da_all * A_row + xds_all
        draw = jnp.where(real, ddt_all * _sigmoid(pre), 0.0)
        ddt_ref[...] = draw
        dpar_ref[0:1, :] += jnp.sum(draw, axis=0, keepdims=True)
        dpar_ref[1:2, :] += jnp.sum(jnp.where(real, da_all * a_all, 0.0), axis=0, keepdims=True)
        dpar_ref[2:3, :] += dD_row

    gn = G * N
    conv_dim = xbc.shape[1]
    rev = lambda c: nc - 1 - c
    return pl.pallas_call(
        body, name=name, grid=(nc,),
        in_specs=[pl.BlockSpec((L, inner), lambda c: (rev(c), 0)), pl.BlockSpec((L, inner), lambda c: (rev(c), 0)),
                  pl.BlockSpec((L, gn), lambda c: (rev(c), inner // gn)), pl.BlockSpec((L, gn), lambda c: (rev(c), inner // gn + 1)),
                  pl.BlockSpec((L, LANES), lambda c: (rev(c), 0)), pl.BlockSpec((None, 8, LANES), lambda c: (l, 0, 0)),
                  pl.BlockSpec((1, G * PP, LANES, N), lambda c: (rev(c), 0, 0, 0))],
        out_specs=[pl.BlockSpec((L, conv_dim), lambda c: (rev(c), 0)), pl.BlockSpec((L, LANES), lambda c: (rev(c), 0)),
                   pl.BlockSpec((8, LANES), lambda c: (0, 0))],
        out_shape=[jax.ShapeDtypeStruct((S, conv_dim), BF), jax.ShapeDtypeStruct((S, LANES), F32), jax.ShapeDtypeStruct((8, LANES), F32)],
        scratch_shapes=[pltpu.VMEM((G * PP, LANES, N), F32)],
        compiler_params=_cp(("arbitrary",)),
    )(dy, xbc, xbc, xbc, dt_raw, par, states)


def _adamw(g, w, m, v):
    m2 = ADAM_B1 * m + (1.0 - ADAM_B1) * g
    v2 = ADAM_B2 * v + (1.0 - ADAM_B2) * (g * g)
    m_hat = m2 / (1.0 - ADAM_B1 ** ADAM_STEP)
    v_hat = v2 / (1.0 - ADAM_B2 ** ADAM_STEP)
    delta = -ADAM_LR * (m_hat / (jnp.sqrt(v_hat) + ADAM_EPS) + ADAM_WD * w)
    return delta, m2, v2


def _flat_tile(R):
    return _pick(R, (FLAT_ROW_TILE, 1024, 512, 256, 128, 64, 32, 16, 8))


def _sum_adam(name, lands, off, w, m, v):
    depth, r, c = w.shape
    cap = max(16, (4 * 1024 * 1024) // (N_DEV * c * 2))
    row_tiles = [t for t in (512, 256, 128, 64, 32, 16) if r % t == 0 and off % t == 0 and t <= cap]
    if row_tiles:
        tr, tc = row_tiles[0], c
        ob = off // tr
        n_t = r // tr
        spec = pl.BlockSpec((None, tr, c), lambda l, t: (l, t, 0))
        land_specs = [pl.BlockSpec((N_DEV, tr, c), lambda l, t, i=i: (0, jnp.where(l == i, ob + t, ob), 0)) for i in range(depth)]
    else:
        assert off == 0 and lands[0].shape[1] == r and c % LANES == 0
        tc = LANES
        n_t = c // tc
        spec = pl.BlockSpec((None, r, tc), lambda l, t: (l, 0, t))
        land_specs = [pl.BlockSpec((N_DEV, r, tc), lambda l, t, i=i: (0, 0, jnp.where(l == i, t, 0))) for i in range(depth)]

    def body(*refs):
        land_refs = refs[:depth]
        w_ref, m_ref, v_ref, g_ref, d_ref, m2_ref, v2_ref = refs[depth:]
        l = pl.program_id(0)
        for i in range(depth):
            @pl.when(l == i)
            def _(i=i):
                g = land_refs[i][0].astype(F32)
                for k in range(1, N_DEV):
                    g = g + land_refs[i][k].astype(F32)
                g_ref[...] = g
                d_ref[...], m2_ref[...], v2_ref[...] = _adamw(g, w_ref[...], m_ref[...], v_ref[...])

    return pl.pallas_call(
        body, name=name, grid=(depth, n_t),
        in_specs=land_specs + [spec, spec, spec],
        out_specs=[spec] * 4, out_shape=[jax.ShapeDtypeStruct((depth, r, c), F32)] * 4,
        compiler_params=_cp(("arbitrary", "arbitrary")),
    )(*lands, w, m, v)


def _sum8(name, parts):
    R = parts.shape[1]
    TR = _flat_tile(R)

    def body(p_ref, g_ref):
        g = p_ref[0]
        for k in range(1, N_DEV):
            g = g + p_ref[k]
        g_ref[...] = g

    return pl.pallas_call(
        body, name=name, grid=(R // TR,),
        in_specs=[pl.BlockSpec((N_DEV, TR, LANES), lambda i: (0, i, 0))],
        out_specs=pl.BlockSpec((TR, LANES), lambda i: (i, 0)), out_shape=jax.ShapeDtypeStruct((R, LANES), F32),
        compiler_params=_cp(("parallel",)),
    )(parts)


def _adam_flat(name, g, w, m, v):
    R = w.shape[0]
    TR = _flat_tile(R)

    def body(g_ref, w_ref, m_ref, v_ref, d_ref, m2_ref, v2_ref):
        d_ref[...], m2_ref[...], v2_ref[...] = _adamw(g_ref[...], w_ref[...], m_ref[...], v_ref[...])

    spec = pl.BlockSpec((TR, LANES), lambda i: (i, 0))
    return pl.pallas_call(
        body, name=name, grid=(R // TR,), in_specs=[spec] * 4, out_specs=[spec] * 3,
        out_shape=[jax.ShapeDtypeStruct((R, LANES), F32)] * 3, compiler_params=_cp(("parallel",)),
    )(g, w, m, v)


PART_ROWS = 16


def _nrows(shape):
    n = 1
    for s in shape:
        n *= s
    r = -(-n // LANES)
    return -(-r // PART_ROWS) * PART_ROWS


def _as_rows(a):
    n = a.size
    r = _nrows(a.shape)
    f = a.reshape(-1)
    if r * LANES != n:
        f = jnp.concatenate([f, jnp.zeros((r * LANES - n,), a.dtype)])
    return f.reshape(r, LANES)


def _pack(arrs, mult=PART_ROWS):
    cat = jnp.concatenate([_as_rows(a) for a in arrs], axis=0)
    pad = (-cat.shape[0]) % mult
    if pad:
        cat = jnp.concatenate([cat, jnp.zeros((pad, LANES), cat.dtype)], axis=0)
    return cat


def _unpack(flat, shapes):
    lead = flat.shape[:-2]
    out = []
    o = 0
    for shp in shapes:
        n = 1
        for s in shp:
            n *= s
        r = _nrows(shp)
        blk = flat[..., o:o + r, :].reshape(lead + (r * LANES,))
        out.append(blk[..., :n].reshape(lead + tuple(shp)))
        o += r
    return out


def _full_from_shards(st):
    return st.reshape(st.shape[0] * st.shape[1], st.shape[2])


def _shards_from_full(full):
    return full.reshape(N_DEV, full.shape[0] // N_DEV, full.shape[1])


def _ffn_fwd(tag, h, g, wgT, wuT, wd, dep=None):
    xn = _rms_fwd(tag + "_rms", h, g, dep=dep)
    a, b, hmid = _ffn_up(tag + "_up", xn, wgT, wuT)
    hout = _mm(tag + "_down", hmid, wd, out_dtype=F32, res=h, alpha=0.5)
    return hout, (xn, a, b, hmid)


def _ffn_bwd(tag, dh_out, h, g, wgT, wuT, wd, saved, dep=None):
    xn, a, b, hmid = saved
    da, db = _ffn_dact(tag + "_d_act", dh_out, wd, a, b, dep=dep)
    d_wd = _mm(tag + "_d_wd", hmid, dh_out, ta=True, alpha=0.5)
    d_wgT = _mm(tag + "_d_wg", da, xn, ta=True)
    d_wuT = _mm(tag + "_d_wu", db, xn, ta=True)
    dxn = _mm(tag + "_d_xn_g", da, wgT, out_dtype=F32)
    dxn = _mm(tag + "_d_xn_u", db, wuT, out_dtype=F32, res=dxn)
    dh, dg = _rms_bwd(tag + "_d_rms", dxn, h, g, dh_out)
    return dh, dg, d_wgT, d_wuT, d_wd


SEG_NAMES = ['scb', 'scc', 'scx', 'z', 'xbc', 'dt', 'ga', 'gm']
PERM = ['z', 'scb', 'scc', 'scx', 'ga', 'gm', 'xbc']


def _seg_layout(dims):
    D, inner, conv_dim, H = dims[:4]
    widths = dict(zip(SEG_NAMES, [D, D, D, inner, conv_dim, H, D, D]))
    offs, o = {}, 0
    for n in SEG_NAMES:
        offs[n] = (o, widths[n])
        o += widths[n]
    poffs, o = {}, 0
    for n in PERM:
        poffs[n] = (o, widths[n])
        o += widths[n]
    return offs, poffs


def _perm_w_in(w_inT, dims):
    offs, _ = _seg_layout(dims)
    wp = jnp.concatenate([w_inT[offs[n][0]:offs[n][0] + offs[n][1]] for n in PERM], axis=0)
    o, w = offs['dt']
    wdt = jnp.concatenate([w_inT[o:o + w], jnp.zeros((LANES - w, w_inT.shape[1]), w_inT.dtype)], axis=0)
    return wp, wdt


def _unperm_d_w_in(d_wp, d_wdt, dims):
    offs, poffs = _seg_layout(dims)
    H = dims[3]
    return jnp.concatenate([d_wdt[:H] if n == 'dt' else d_wp[poffs[n][0]:poffs[n][0] + poffs[n][1]] for n in SEG_NAMES], axis=0)


def _mixer_fwd(h, W, dims, dep=None):
    H, Ksc, Km = dims[3:]
    l = W['l']
    _, poffs = _seg_layout(dims)

    def seg(n):
        o, w = poffs[n]
        assert o % w == 0
        return (proj, w, o // w)

    u = _rms_fwd("mix_rms", h, W['mix_norm'], dep=dep)
    proj = _mm("inproj", u, W['w_in_p'], tb=True)
    dt_raw = _mm("inproj_dt", u, W['w_dt'], tb=True, out_dtype=F32)
    v = _scconv_fwd("scconv_f", proj, poffs['scb'][0], poffs['scc'][0], poffs['scx'][0], W['sc_taps'], Ksc, l)
    ya = _mm("sc_out", v, W['sc_w_out'])
    xbc = _mconv_fwd("mconv_f", proj, poffs['xbc'][0], W['m_taps'], Km, W['m_conv_b'], l)
    y, states = _ssd_fwd("ssd_f", xbc, dt_raw, W['ssd_par'], l, H)
    yn = _gnorm_fwd("gnorm_f", y, seg('z'), W['m_norm'])
    ym = _mm("m_out", yn, W['m_w_out'])
    merged = _merge_fwd("merge_f", seg('ga'), seg('gm'), ya, ym)
    hout = _mm("w_o", merged, W['w_o'], out_dtype=F32, res=h)
    return hout, (u, proj, dt_raw, v, ya, xbc, y, states, yn, ym, merged)


def _mixer_bwd(dh_out, h, W, dims, saved, dep=None):
    u, proj, dt_raw, v, ya, xbc, y, states, yn, ym, merged = saved
    H, Ksc, Km = dims[3:]
    l = W['l']
    _, poffs = _seg_layout(dims)

    def seg(n):
        o, w = poffs[n]
        return (proj, w, o // w)

    g = {}
    dmerged = _mm("d_merged", dh_out, W['w_o'], tb=True, dep=dep)
    g['w_o'] = _mm("d_w_o", merged, dh_out, ta=True)
    dga, dgm, dya, dym = _merge_bwd("merge_b", dmerged, seg('ga'), seg('gm'), ya, ym)
    g['sc_w_out'] = _mm("d_sc_w_out", v, dya, ta=True)
    dv = _mm("d_v", dya, W['sc_w_out'], tb=True)
    g['m_w_out'] = _mm("d_m_w_out", yn, dym, ta=True)
    dyn = _mm("d_yn", dym, W['m_w_out'], tb=True)
    dy, dz, d_mnorm = _gnorm_bwd("gnorm_b", dyn, y, seg('z'), W['m_norm'])
    g['m_norm'] = d_mnorm.reshape(-1)
    dxbc_post, ddt, dpar = _ssd_bwd("ssd_b", dy, xbc, dt_raw, states, W['ssd_par'], l, H)
    g['m_dt_bias'] = dpar[0, :H]
    g['m_A_log'] = dpar[1, :H]
    g['m_D'] = dpar[2, :H]
    dxbc, d_mcw, d_mcb = _mconv_bwd("mconv_b", dxbc_post, proj, poffs['xbc'][0], W['m_taps'], Km, W['m_conv_b'], l)
    g['m_conv_w'] = d_mcw[:Km]
    g['m_conv_b'] = d_mcb.reshape(-1)
    dscb, dscc, dscx, d_scw = _scconv_bwd("scconv_b", dv, proj, poffs['scb'][0], poffs['scc'][0], poffs['scx'][0],
                                          W['sc_taps'], Ksc, l)
    g['sc_conv_w'] = d_scw[:Ksc]
    dproj = jnp.concatenate([dz, dscb, dscc, dscx, dga, dgm, dxbc], axis=1)
    du = _mm("d_u_main", dproj, W['w_in_p'], out_dtype=F32)
    du = _mm("d_u_dt", ddt, W['w_dt'], out_dtype=F32, res=du)
    d_wp = _mm("d_w_in_main", dproj, u, ta=True)
    d_wdt = _mm("d_w_in_dt", ddt, u, ta=True)
    g['w_in'] = _unperm_d_w_in(d_wp, d_wdt, dims)
    dh, dg = _rms_bwd("mix_d_rms", du, h, W['mix_norm'], dh_out)
    g['mix_norm'] = dg.reshape(-1)
    return dh, g


def _ple_layer_fwd(h, p_l, W):
    xn = _rms_fwd("ple_rms", h, W['ple_norm'])
    gpre = _mm("ple_gate", xn, W['ple_w_gate'])
    pp = _mm("ple_proj", p_l, W['ple_w_proj'], tb=True)
    hout = _ple_fwd("ple_f", h, gpre, pp)
    return hout, (xn, gpre, pp)


def _ple_layer_bwd(dh_out, h, p_l, W, saved, dep=None):
    xn, gpre, pp = saved
    g = {}
    dgpre, dpp = _ple_bwd("ple_b", dh_out, gpre, pp, dep=dep)
    g['ple_w_proj'] = _mm("d_ple_proj", dpp, p_l, ta=True)
    g['ple_w_gate'] = _mm("d_ple_gate", xn, dgpre, ta=True)
    dxn = _mm("d_ple_xn", dgpre, W['ple_w_gate'], tb=True)
    dh, dg = _rms_bwd("ple_d_rms", dxn, h, W['ple_norm'], dh_out)
    g['ple_norm'] = dg.reshape(-1)
    return dh, g


def kernel(x, p, ffn1_norm, ffn1_wg, ffn1_wu, ffn1_wd, mix_norm, w_in, sc_conv_w, sc_w_out, m_conv_w, m_conv_b, m_dt_bias, m_A_log, m_D, m_norm, m_w_out, w_o, ffn2_norm, ffn2_wg, ffn2_wu, ffn2_wd, ple_norm, ple_w_gate, ple_w_proj, final_norm, loss_target, m_ffn1_norm, m_ffn1_wg, m_ffn1_wu, m_ffn1_wd, m_mix_norm, m_w_in, m_sc_conv_w, m_sc_w_out, m_m_conv_w, m_m_conv_b, m_m_dt_bias, m_m_A_log, m_m_D, m_m_norm, m_m_w_out, m_w_o, m_ffn2_norm, m_ffn2_wg, m_ffn2_wu, m_ffn2_wd, m_ple_norm, m_ple_w_gate, m_ple_w_proj, m_final_norm, v_ffn1_norm, v_ffn1_wg, v_ffn1_wu, v_ffn1_wd, v_mix_norm, v_w_in, v_sc_conv_w, v_sc_w_out, v_m_conv_w, v_m_conv_b, v_m_dt_bias, v_m_A_log, v_m_D, v_m_norm, v_m_w_out, v_w_o, v_ffn2_norm, v_ffn2_wg, v_ffn2_wu, v_ffn2_wd, v_ple_norm, v_ple_w_gate, v_ple_w_proj, v_final_norm):
    args = (x, p, ffn1_norm, ffn1_wg, ffn1_wu, ffn1_wd, mix_norm, w_in, sc_conv_w, sc_w_out, m_conv_w, m_conv_b, m_dt_bias, m_A_log, m_D, m_norm, m_w_out, w_o, ffn2_norm, ffn2_wg, ffn2_wu, ffn2_wd, ple_norm, ple_w_gate, ple_w_proj, final_norm, loss_target, m_ffn1_norm, m_ffn1_wg, m_ffn1_wu, m_ffn1_wd, m_mix_norm, m_w_in, m_sc_conv_w, m_sc_w_out, m_m_conv_w, m_m_conv_b, m_m_dt_bias, m_m_A_log, m_m_D, m_m_norm, m_m_w_out, m_w_o, m_ffn2_norm, m_ffn2_wg, m_ffn2_wu, m_ffn2_wd, m_ple_norm, m_ple_w_gate, m_ple_w_proj, m_final_norm, v_ffn1_norm, v_ffn1_wg, v_ffn1_wu, v_ffn1_wd, v_mix_norm, v_w_in, v_sc_conv_w, v_sc_w_out, v_m_conv_w, v_m_conv_b, v_m_dt_bias, v_m_A_log, v_m_D, v_m_norm, v_m_w_out, v_w_o, v_ffn2_norm, v_ffn2_wg, v_ffn2_wu, v_ffn2_wd, v_ple_norm, v_ple_w_gate, v_ple_w_proj, v_final_norm)
    names = ARG_NAMES + ['m_' + n for n in WEIGHTS] + ['v_' + n for n in WEIGHTS]
    A = dict(zip(names, args))
    depth = ffn1_norm.shape[0]
    me = 4 * lax.axis_index("x") + 2 * lax.axis_index("y") + lax.axis_index("c")

    dims = (x.shape[-1], m_norm.shape[1], m_conv_b.shape[1], m_dt_bias.shape[1], sc_conv_w.shape[1], m_conv_w.shape[1])
    kind = dict(BIG)

    def work(n, prefix=''):
        return jnp.swapaxes(A[prefix + n], 1, 2) if kind[n] == 'col' else A[prefix + n]

    wb = {n: work(n).astype(BF) for n, _ in BIG}
    srcs = [[wb[ms[0]] if len(ms) == 1 else jnp.concatenate([wb[n] for n in ms], axis=1) for ms in stage] for stage in STAGES]
    conv_g = _unpack(_exchange("gather_conv_taps", _pack([A[n] for n in CONVW]), True), [A[n].shape for n in CONVW])
    taps = {}
    for n, st in zip(CONVW, conv_g):
        taps[n] = _pad_taps(jnp.transpose(st, (1, 2, 0, 3)).reshape(depth, st.shape[2], N_DEV * st.shape[3]))
    ssd_par = _ssd_params(m_dt_bias, m_A_log, m_D)
    small3 = {n: A[n].reshape(depth, 1, -1) for n in SMALL}

    def stage_weights(W, s, l, lands):
        for ms, land, src in zip(STAGES[s], lands, srcs[s]):
            land = lax.dynamic_update_slice(land, src[l][None], (me, 0, 0))
            off = 0
            for n in ms:
                r = wb[n].shape[1]
                W[n] = _full_from_shards(land if len(ms) == 1 else land[:, off:off + r])
                off += r
        if s == 1:
            W['w_in_p'], W['w_dt'] = _perm_w_in(W.pop('w_in'), dims)

    flight = {}

    def begin_layer(l, dep):
        for s in range(len(STAGES)):
            sems, lands, dep = _xchg_begin(f"gather_begin{l}{'abc'[s]}", srcs[s], l, dep)
            flight[(l, s)] = (sems, lands)
        return dep

    def end_stage(W, l, s, after):
        sems, lands = flight.pop((l, s))
        stage_weights(W, s, l, _xchg_end(f"gather_end{l}{'abc'[s]}", srcs[s], lands, sems, l, after))

    tok = begin_layer(0, taps['sc_conv_w'])
    h = x[0]
    saved = []
    layers = []
    for l in range(depth):
        W = {n: (small3[n], l) for n in SMALL}
        W.update(l=l, sc_taps=taps['sc_conv_w'], m_taps=taps['m_conv_w'], m_conv_b=small3['m_conv_b'], ssd_par=ssd_par)
        layers.append(W)
        end_stage(W, l, 0, tok if l == 0 else h)
        h1, s1 = _ffn_fwd("ffn1", h, W['ffn1_norm'], W['ffn1_wg'], W['ffn1_wu'], W['ffn1_wd'])
        end_stage(W, l, 1, h1)
        tok = begin_layer(l + 1, W['w_dt']) if l + 1 < depth else None
        h2, s2 = _mixer_fwd(h1, W, dims, dep=tok)
        end_stage(W, l, 2, h2)
        h3, s3 = _ffn_fwd("ffn2", h2, W['ffn2_norm'], W['ffn2_wg'], W['ffn2_wu'], W['ffn2_wd'])
        h4, s4 = _ple_layer_fwd(h3, p[l, 0], W)
        saved.append((h, h1, h2, h3, s1, s2, s3, s4))
        h = h4

    dh, loss_row, d_final = _loss_head("loss_head", h, final_norm, loss_target[0])
    loss = lax.psum(loss_row[0, 0], ("x", "y", "c"))

    def send_bufs(g, s):
        return [jnp.concatenate([_shards_from_full(g[n]) for n in ms], axis=1) if len(ms) > 1
                else _shards_from_full(g[ms[0]]) for ms in STAGES[s]]

    grads = [None] * depth
    pending = []

    def send_stage(g, l, s, dep):
        send = send_bufs(g, s)
        sems, lands, tok = _xchg_begin(f"scatter_begin{l}{'abc'[s]}", send, None, dep)
        pending.append((l, s, send, lands, sems))
        return tok

    tok = loss.reshape(1, 1)
    for l in reversed(range(depth)):
        W = layers[l]
        h0, h1, h2, h3, s1, s2, s3, s4 = saved[l]
        g = {}
        dh, g4 = _ple_layer_bwd(dh, h3, p[l, 0], W, s4, dep=tok)
        g.update(g4)
        dh, dg, d_wg, d_wu, d_wd = _ffn_bwd("ffn2", dh, h2, W['ffn2_norm'], W['ffn2_wg'], W['ffn2_wu'], W['ffn2_wd'], s3)
        g.update(ffn2_norm=dg.reshape(-1), ffn2_wg=d_wg, ffn2_wu=d_wu, ffn2_wd=d_wd)
        tok = send_stage(g, l, 2, dh)
        dh, g2 = _mixer_bwd(dh, h1, W, dims, s2, dep=tok)
        g.update(g2)
        tok = send_stage(g, l, 1, dh)
        dh, dg, d_wg, d_wu, d_wd = _ffn_bwd("ffn1", dh, h0, W['ffn1_norm'], W['ffn1_wg'], W['ffn1_wu'], W['ffn1_wd'], s1, dep=tok)
        g.update(ffn1_norm=dg.reshape(-1), ffn1_wg=d_wg, ffn1_wu=d_wu, ffn1_wd=d_wd)
        grads[l] = g
        tok = send_stage(g, l, 0, dh)
    grad_x = dh[None]

    g_lands = [[None] * len(STAGES) for _ in range(depth)]
    after = dh
    for l, s, send, lands, sems in pending:
        got = _xchg_end(f"scatter_end{l}{'abc'[s]}", send, lands, sems, None, after)
        after = got[0]
        g_lands[l][s] = [lax.dynamic_update_slice(o, lax.dynamic_slice_in_dim(b, me, 1, axis=0), (me, 0, 0)) for o, b in zip(got, send)]

    big_res = [{}, {}, {}, {}]
    for s, stage in enumerate(STAGES):
        for gi, ms in enumerate(stage):
            off = 0
            for n in ms:
                res = _sum_adam("adamw_" + n, [g_lands[l][s][gi] for l in range(depth)], off, work(n), work(n, 'm_'), work(n, 'v_'))
                for k in range(4):
                    big_res[k][n] = jnp.swapaxes(res[k], 1, 2) if kind[n] == 'col' else res[k]
                off += wb[n].shape[1]

    small_names = SMALL + CONVW
    small_parts = [jnp.stack([grads[l][n] for l in range(depth)]) for n in small_names] + [d_final.reshape(-1)]
    small_sum = _sum8("sum_small", _exchange("gather_small_grads", _pack(small_parts), True, dep=after))
    sg = dict(zip(small_names + ['final_norm'], _unpack(small_sum, [a.shape for a in small_parts])))
    for n in CONVW:
        c = A[n].shape[-1]
        sg[n] = lax.dynamic_slice_in_dim(sg[n], me * c, c, axis=2)
    s_order = small_names + ['final_norm']
    s_shapes = [sg[n].shape for n in s_order]
    s_out = _adam_flat("adamw_small", _pack([sg[n] for n in s_order]), _pack([A[n] for n in s_order]),
                       _pack([A['m_' + n] for n in s_order]), _pack([A['v_' + n] for n in s_order]))
    small_res = [sg] + [dict(zip(s_order, _unpack(flat, s_shapes))) for flat in s_out]

    outs = [loss, grad_x]
    for k in range(4):
        for n in WEIGHTS:
            outs.append(big_res[k][n] if n in big_res[k] else small_res[k][n])
    return tuple(outs)
```

```python
import functools

import jax
import jax.numpy as jnp
from jax import lax
from jax.experimental import pallas as pl
from jax.experimental.pallas import tpu as pltpu

BF = jnp.bfloat16
F32 = jnp.float32

EPS = 1e-6
N_DEV = 8
LANES = 128
SSM_GROUPS = 4
SSM_HEADDIM = 64
SSM_CHUNK = 128
HALO = 16
VMEM_LIMIT = 56 * 1024 * 1024
FLAT_ROW_TILE = 2048

ADAM_LR = 0.001
ADAM_B1 = 0.9
ADAM_B2 = 0.999
ADAM_EPS = 1e-08
ADAM_WD = 0.01
ADAM_STEP = 10

MESH = pl.DeviceIdType.MESH

ARG_NAMES = ['x', 'p', 'ffn1_norm', 'ffn1_wg', 'ffn1_wu', 'ffn1_wd', 'mix_norm', 'w_in', 'sc_conv_w', 'sc_w_out', 'm_conv_w', 'm_conv_b', 'm_dt_bias', 'm_A_log', 'm_D', 'm_norm', 'm_w_out', 'w_o', 'ffn2_norm', 'ffn2_wg', 'ffn2_wu', 'ffn2_wd', 'ple_norm', 'ple_w_gate', 'ple_w_proj', 'final_norm', 'loss_target']
WEIGHTS = ARG_NAMES[2:26]
BIG = [('ffn1_wg', 'col'), ('ffn1_wu', 'col'), ('ffn1_wd', 'row'), ('w_in', 'col'), ('sc_w_out', 'row'),
       ('m_w_out', 'row'), ('w_o', 'row'), ('ffn2_wg', 'col'), ('ffn2_wu', 'col'), ('ffn2_wd', 'row'),
       ('ple_w_gate', 'row'), ('ple_w_proj', 'col')]
CONVW = ['sc_conv_w', 'm_conv_w']
SMALL = ['ffn1_norm', 'mix_norm', 'm_conv_b', 'm_dt_bias', 'm_A_log', 'm_D', 'm_norm', 'ffn2_norm', 'ple_norm']


def _pick(n, cands):
    for c in cands:
        if n % c == 0:
            return c
    return n


def _cp(sem):
    return pltpu.CompilerParams(dimension_semantics=sem, vmem_limit_bytes=VMEM_LIMIT)


def _sigmoid(x):
    return 1.0 / (1.0 + jnp.exp(-x))


def _softplus(x):
    return jnp.maximum(x, 0.0) + jnp.log(1.0 + jnp.exp(-jnp.abs(x)))


def _exchange(name, x, gather, dep=None):
    slab = x.shape if gather else x.shape[1:]

    def body(x_ref, *rest):
        o_ref, send_sems, recv_sems, local_sem = rest[-4:]
        mx, my, mc = lax.axis_index("x"), lax.axis_index("y"), lax.axis_index("c")
        me = 4 * mx + 2 * my + mc

        def src_for(k):
            return x_ref if gather else x_ref.at[k]

        local = pltpu.make_async_copy(src_for(me), o_ref.at[me], local_sem)
        local.start()
        sends = []
        peers = []
        for r in range(1, N_DEV):
            px = (mx + ((r >> 2) & 1)) % 2
            py = (my + ((r >> 1) & 1)) % 2
            pc = (mc + (r & 1)) % 2
            peer = 4 * px + 2 * py + pc
            peers.append(peer)
            cp = pltpu.make_async_remote_copy(
                src_ref=src_for(peer), dst_ref=o_ref.at[me], send_sem=send_sems.at[r - 1], recv_sem=recv_sems.at[r - 1],
                device_id=(px, py, pc), device_id_type=MESH)
            cp.start()
            sends.append(cp)
        for r in range(1, N_DEV):
            peer = peers[r - 1]
            pltpu.make_async_remote_copy(
                src_ref=src_for(peer), dst_ref=o_ref.at[peer], send_sem=send_sems.at[r - 1], recv_sem=recv_sems.at[r - 1],
                device_id=(mx, my, mc), device_id_type=MESH).wait_recv()
        for cp in sends:
            cp.wait_send()
        local.wait()

    return pl.pallas_call(
        body, name=name,
        out_shape=jax.ShapeDtypeStruct((N_DEV,) + tuple(slab), x.dtype),
        in_specs=[pl.BlockSpec(memory_space=pltpu.HBM)] + ([] if dep is None else [pl.BlockSpec(memory_space=pl.ANY)]),
        out_specs=pl.BlockSpec(memory_space=pltpu.HBM),
        scratch_shapes=[pltpu.SemaphoreType.DMA((N_DEV - 1,)), pltpu.SemaphoreType.DMA((N_DEV - 1,)), pltpu.SemaphoreType.DMA],
    )(*([x] if dep is None else [x, dep]))


STAGES = [[['ffn1_wd'], ['ffn1_wg'], ['ffn1_wu']],
          [['w_in'], ['sc_w_out', 'w_o', 'ple_w_gate', 'm_w_out']],
          [['ffn2_wd'], ['ffn2_wg'], ['ffn2_wu'], ['ple_w_proj']]]
_HBM = pl.BlockSpec(memory_space=pltpu.HBM)
_SEM = pl.BlockSpec(memory_space=pltpu.SEMAPHORE)
_ANY = pl.BlockSpec(memory_space=pl.ANY)
_EFFECT = pltpu.SideEffectType.DATAFLOW_SIDE_EFFECTING


def _peer_list():
    mx, my, mc = lax.axis_index("x"), lax.axis_index("y"), lax.axis_index("c")
    out = []
    for r in range(1, N_DEV):
        px = (mx + ((r >> 2) & 1)) % 2
        py = (my + ((r >> 1) & 1)) % 2
        pc = (mc + (r & 1)) % 2
        out.append((px, py, pc, 4 * px + 2 * py + pc))
    return 4 * mx + 2 * my + mc, out


def _xchg_copy(src_refs, land_refs, send_sems, recv_sems, layer, i, r, peer, dst_slab):
    px, py, pc, pidx = peer
    n = len(src_refs)
    src = src_refs[i].at[layer] if layer is not None else src_refs[i].at[pidx]
    return pltpu.make_async_remote_copy(
        src_ref=src, dst_ref=land_refs[i].at[dst_slab], send_sem=send_sems.at[r * n + i], recv_sem=recv_sems.at[r * n + i],
        device_id=(px, py, pc), device_id_type=MESH)


def _xchg_begin(name, srcs, layer, dep):
    n = len(srcs)
    slabs = [tuple(s.shape[1:]) for s in srcs]
    ncp = n * (N_DEV - 1)

    def body(*refs):
        src_refs, land_refs = refs[:n], refs[n:2 * n]
        send_sems, recv_sems = refs[2 * n + 1], refs[2 * n + 2]
        token = refs[-1]
        me, peers = _peer_list()
        for r, peer in enumerate(peers):
            for i in range(n):
                _xchg_copy(src_refs, land_refs, send_sems, recv_sems, layer, i, r, peer, me).start()
        token[...] = jnp.zeros_like(token)

    lands = [pltpu.with_memory_space_constraint(lax.empty((N_DEV,) + sl, s.dtype), pltpu.HBM) for sl, s in zip(slabs, srcs)]
    out = pl.pallas_call(
        body, name=name,
        out_shape=(pltpu.SemaphoreType.DMA((ncp,)), pltpu.SemaphoreType.DMA((ncp,)),
                   *[pltpu.HBM((N_DEV,) + sl, s.dtype) for sl, s in zip(slabs, srcs)], jax.ShapeDtypeStruct((8, LANES), F32)),
        in_specs=[_HBM] * (2 * n) + [_ANY],
        out_specs=(_SEM, _SEM, *[_HBM] * n, pl.BlockSpec(memory_space=pltpu.VMEM)),
        input_output_aliases={n + i: 2 + i for i in range(n)},
        compiler_params=pltpu.CompilerParams(has_side_effects=_EFFECT),
    )(*[pltpu.with_memory_space_constraint(s, pltpu.HBM) for s in srcs], *lands, dep)
    return (out[0], out[1]), list(out[2:2 + n]), out[-1]


def _xchg_end(name, srcs, lands, sems, layer, after):
    n = len(srcs)

    def body(*refs):
        src_refs, land_refs = refs[:n], refs[n:2 * n]
        send_sems, recv_sems = refs[2 * n], refs[2 * n + 1]
        me, peers = _peer_list()
        for r, peer in enumerate(peers):
            for i in range(n):
                cp = _xchg_copy(src_refs, land_refs, send_sems, recv_sems, layer, i, r, peer, peer[3])
                cp.wait_send()
                cp.wait_recv()

    out = pl.pallas_call(
        body, name=name,
        out_shape=tuple(pltpu.HBM(l.shape, l.dtype) for l in lands),
        in_specs=[_HBM] * (2 * n) + [_SEM, _SEM, _ANY], out_specs=tuple([_HBM] * n),
        input_output_aliases={n + i: i for i in range(n)},
        compiler_params=pltpu.CompilerParams(has_side_effects=_EFFECT),
    )(*[pltpu.with_memory_space_constraint(s, pltpu.HBM) for s in srcs], *lands, sems[0], sems[1], after)
    return list(out)


MM_TILES = (1024, 1408, 512, 256, 128)
MM_OPERAND_BYTES = 24 * 1024 * 1024


def _mm(name, a, b, *, ta=False, tb=False, out_dtype=None, res=None, alpha=1.0, dep=None):
    out_dtype = out_dtype or BF
    M, K = (a.shape[1], a.shape[0]) if ta else a.shape
    N = b.shape[0] if tb else b.shape[1]
    assert (b.shape[1] if tb else b.shape[0]) == K, (name, a.shape, b.shape)
    tm = _pick(M, MM_TILES)
    tn = _pick(N, MM_TILES)
    per_k = 2 * (tm * a.dtype.itemsize + tn * b.dtype.itemsize)
    tk = [t for t in sorted({K, 4096, 2816, 2560, 2048, 1408, 1024, 512, 256, 128}, reverse=True)
          if K % t == 0 and (t * per_k <= MM_OPERAND_BYTES or t == 128)][0]
    nk = K // tk
    a_spec = pl.BlockSpec((tk, tm), lambda i, j, k: (k, i)) if ta else pl.BlockSpec((tm, tk), lambda i, j, k: (i, k))
    b_spec = pl.BlockSpec((tn, tk), lambda i, j, k: (j, k)) if tb else pl.BlockSpec((tk, tn), lambda i, j, k: (k, j))
    dn = (((0 if ta else 1,), (1 if tb else 0,)), ((), ()))
    has_res = res is not None
    n_dep = 0 if dep is None else 1

    def body(*refs):
        a_ref, b_ref = refs[:2]
        r_ref = refs[2] if has_res else None
        o_ref = refs[2 + has_res + n_dep]

        def finish(v):
            if alpha != 1.0:
                v = v * alpha
            if has_res:
                v = r_ref[...] + v
            o_ref[...] = v.astype(o_ref.dtype)

        part = lax.dot_general(a_ref[...].astype(BF), b_ref[...].astype(BF), dn, preferred_element_type=F32)
        if nk == 1:
            finish(part)
            return
        acc = refs[-1]
        k = pl.program_id(2)

        @pl.when(k == 0)
        def _():
            acc[...] = part

        @pl.when((k > 0) & (k < nk - 1))
        def _():
            acc[...] += part

        @pl.when(k == nk - 1)
        def _():
            finish(acc[...] + part)

    in_specs = [a_spec, b_spec]
    args = [a, b]
    if has_res:
        in_specs.append(pl.BlockSpec((tm, tn), lambda i, j, k: (i, j)))
        args.append(res)
    if dep is not None:
        in_specs.append(_ANY)
        args.append(dep)
    return pl.pallas_call(
        body, name=name, grid=(M // tm, N // tn, nk),
        in_specs=in_specs, out_specs=pl.BlockSpec((tm, tn), lambda i, j, k: (i, j)),
        out_shape=jax.ShapeDtypeStruct((M, N), out_dtype),
        scratch_shapes=[pltpu.VMEM((tm, tn), F32)] if nk > 1 else [],
        compiler_params=_cp(("parallel", "parallel", "arbitrary")),
    )(*args)


def _ew(name, fn, tiled, params, outs, accs=(), tile=256, dep=None):
    tiled = [t if isinstance(t, tuple) else (t, t.shape[1], 0) for t in tiled]
    params = [q if isinstance(q, tuple) else (q, None) for q in params]
    S = tiled[0][0].shape[0]
    T = _pick(S, (tile, 128, 64, 32, 16))
    n_in = len(tiled) + len(params)
    n_dep = 0 if dep is None else 1

    def body(*refs):
        fn(pl.program_id(0) == 0, *refs[:n_in], *refs[n_in + n_dep:])

    in_specs = [pl.BlockSpec((T, w), lambda i, cb=cb: (i, cb)) for _, w, cb in tiled]
    for q, row in params:
        if row is None:
            in_specs.append(pl.BlockSpec(q.shape, lambda i: (0, 0)))
        else:
            in_specs.append(pl.BlockSpec((None, 1, q.shape[2]), lambda i, row=row: (row, 0, 0)))
    args = [t[0] for t in tiled] + [q[0] for q in params]
    if dep is not None:
        in_specs.append(pl.BlockSpec(memory_space=pl.ANY))
        args.append(dep)
    out_specs = [pl.BlockSpec((T, w), lambda i: (i, 0)) for w, _ in outs]
    out_specs += [pl.BlockSpec(shp, lambda i: (0, 0)) for shp, _ in accs]
    out_shape = [jax.ShapeDtypeStruct((S, w), dt) for w, dt in outs]
    out_shape += [jax.ShapeDtypeStruct(shp, dt) for shp, dt in accs]
    res = pl.pallas_call(
        body, name=name, grid=(S // T,), in_specs=in_specs, out_specs=out_specs, out_shape=out_shape,
        compiler_params=_cp(("arbitrary",)),
    )(*args)
    return res


def _prow(g):
    return g if isinstance(g, tuple) else g.reshape(1, -1)


def _rms_fwd(name, h, g, dep=None):
    def fn(first, h_ref, g_ref, o_ref):
        x = h_ref[...]
        r = lax.rsqrt(jnp.mean(x * x, axis=-1, keepdims=True) + EPS)
        o_ref[...] = (x * r * g_ref[...]).astype(o_ref.dtype)

    return _ew(name, fn, [h], [_prow(g)], [(h.shape[1], BF)], dep=dep)[0]


def _rms_bwd(name, dxn, h, g, res):
    D = h.shape[1]

    def fn(first, d_ref, h_ref, r_ref, g_ref, o_ref, dg_ref):
        x = h_ref[...]
        d = d_ref[...].astype(F32)
        r = lax.rsqrt(jnp.mean(x * x, axis=-1, keepdims=True) + EPS)
        xhat = x * r
        dxhat = d * g_ref[...]
        dh = r * (dxhat - xhat * jnp.mean(dxhat * xhat, axis=-1, keepdims=True))
        o_ref[...] = r_ref[...] + dh

        @pl.when(first)
        def _():
            dg_ref[...] = jnp.zeros_like(dg_ref)

        dg_ref[...] += jnp.sum(d * xhat, axis=0, keepdims=True)

    return _ew(name, fn, [dxn, h, res], [_prow(g)], [(D, F32)], [((1, D), F32)])


FFN_TOKEN_TILE = 512


def _ffn_up(name, xn, wgT, wuT, dep=None):
    S, D = xn.shape
    FF = wgT.shape[0]
    tm = _pick(S, (FFN_TOKEN_TILE, 256, 128))
    tn = _pick(FF, MM_TILES)
    n_dep = 0 if dep is None else 1

    def body(x_ref, g_ref, u_ref, *rest):
        a_ref, b_ref, h_ref = rest[n_dep:]
        x = x_ref[...]
        a = _dot_nt(x, g_ref[...])
        b = _dot_nt(x, u_ref[...])
        a_ref[...] = a.astype(BF)
        b_ref[...] = b.astype(BF)
        h_ref[...] = (a * _sigmoid(a) * b).astype(BF)

    wspec = pl.BlockSpec((tn, D), lambda j, i: (j, 0))
    ospec = pl.BlockSpec((tm, tn), lambda j, i: (i, j))
    return pl.pallas_call(
        body, name=name, grid=(FF // tn, S // tm),
        in_specs=[pl.BlockSpec((tm, D), lambda j, i: (i, 0)), wspec, wspec] + ([] if dep is None else [_ANY]),
        out_specs=[ospec] * 3, out_shape=[jax.ShapeDtypeStruct((S, FF), BF)] * 3,
        compiler_params=_cp(("parallel", "arbitrary")),
    )(*([xn, wgT, wuT] + ([] if dep is None else [dep])))


def _ffn_dact(name, dh, wd, a, b, dep=None):
    S, D = dh.shape
    FF = wd.shape[0]
    tm = _pick(S, (FFN_TOKEN_TILE, 256, 128))
    tn = _pick(FF, MM_TILES)
    n_dep = 0 if dep is None else 1

    def body(d_ref, w_ref, a_ref, b_ref, *rest):
        da_ref, db_ref = rest[n_dep:]
        d = 0.5 * _dot_nt(d_ref[...].astype(BF), w_ref[...])
        av = a_ref[...].astype(F32)
        s = _sigmoid(av)
        da_ref[...] = (d * b_ref[...].astype(F32) * (s * (1.0 + av * (1.0 - s)))).astype(BF)
        db_ref[...] = (d * av * s).astype(BF)

    tspec = pl.BlockSpec((tm, tn), lambda j, i: (i, j))
    return pl.pallas_call(
        body, name=name, grid=(FF // tn, S // tm),
        in_specs=[pl.BlockSpec((tm, D), lambda j, i: (i, 0)), pl.BlockSpec((tn, D), lambda j, i: (j, 0)), tspec, tspec]
        + ([] if dep is None else [_ANY]),
        out_specs=[tspec] * 2, out_shape=[jax.ShapeDtypeStruct((S, FF), BF)] * 2,
        compiler_params=_cp(("parallel", "arbitrary")),
    )(*([dh, wd, a, b] + ([] if dep is None else [dep])))


def _merge_fwd(name, ga, gm, ya, ym):
    def fn(first, ga_ref, gm_ref, ya_ref, ym_ref, o_ref):
        o = _sigmoid(ga_ref[...].astype(F32)) * ya_ref[...].astype(F32) + _sigmoid(gm_ref[...].astype(F32)) * ym_ref[...].astype(F32)
        o_ref[...] = o.astype(o_ref.dtype)

    return _ew(name, fn, [ga, gm, ya, ym], [], [(ya.shape[1], BF)])[0]


def _merge_bwd(name, dmerged, ga, gm, ya, ym):
    W = ya.shape[1]

    def fn(first, d_ref, ga_ref, gm_ref, ya_ref, ym_ref, dga_ref, dgm_ref, dya_ref, dym_ref):
        d = d_ref[...].astype(F32)
        sa = _sigmoid(ga_ref[...].astype(F32))
        sm = _sigmoid(gm_ref[...].astype(F32))
        dga_ref[...] = (d * ya_ref[...].astype(F32) * sa * (1.0 - sa)).astype(BF)
        dgm_ref[...] = (d * ym_ref[...].astype(F32) * sm * (1.0 - sm)).astype(BF)
        dya_ref[...] = (d * sa).astype(BF)
        dym_ref[...] = (d * sm).astype(BF)

    return _ew(name, fn, [dmerged, ga, gm, ya, ym], [], [(W, BF)] * 4)


def _gnorm_fwd(name, y, z, w):
    W = y.shape[1]
    gw = W // SSM_GROUPS

    def fn(first, y_ref, z_ref, w_ref, o_ref):
        for g in range(SSM_GROUPS):
            sl = slice(g * gw, (g + 1) * gw)
            zz = z_ref[:, sl].astype(F32)
            t = y_ref[:, sl].astype(F32) * (zz * _sigmoid(zz))
            r = lax.rsqrt(jnp.mean(t * t, axis=-1, keepdims=True) + EPS)
            o_ref[:, sl] = (t * r * w_ref[:, sl]).astype(o_ref.dtype)

    return _ew(name, fn, [y, z], [_prow(w)], [(W, BF)])[0]


def _gnorm_bwd(name, dyn, y, z, w):
    W = y.shape[1]
    gw = W // SSM_GROUPS

    def fn(first, d_ref, y_ref, z_ref, w_ref, dy_ref, dz_ref, dw_ref):
        @pl.when(first)
        def _():
            dw_ref[...] = jnp.zeros_like(dw_ref)

        for g in range(SSM_GROUPS):
            sl = slice(g * gw, (g + 1) * gw)
            zz = z_ref[:, sl].astype(F32)
            yy = y_ref[:, sl].astype(F32)
            d = d_ref[:, sl].astype(F32)
            s = _sigmoid(zz)
            sz = zz * s
            t = yy * sz
            r = lax.rsqrt(jnp.mean(t * t, axis=-1, keepdims=True) + EPS)
            that = t * r
            dthat = d * w_ref[:, sl]
            dt = r * (dthat - that * jnp.mean(dthat * that, axis=-1, keepdims=True))
            dw_ref[:, sl] += jnp.sum(d * that, axis=0, keepdims=True)
            dy_ref[:, sl] = (dt * sz).astype(BF)
            dz_ref[:, sl] = (dt * yy * (s * (1.0 + zz * (1.0 - s)))).astype(BF)

    return _ew(name, fn, [dyn, y, z], [_prow(w)], [(W, BF), (W, BF)], [((1, W), F32)])


def _ple_fwd(name, h, gpre, pp):
    def fn(first, h_ref, g_ref, p_ref, o_ref):
        o_ref[...] = h_ref[...] + _sigmoid(g_ref[...].astype(F32)) * p_ref[...].astype(F32)

    return _ew(name, fn, [h, gpre, pp], [], [(h.shape[1], F32)])[0]


def _ple_bwd(name, dh, gpre, pp, dep=None):
    W = dh.shape[1]

    def fn(first, d_ref, g_ref, p_ref, dg_ref, dp_ref):
        d = d_ref[...]
        s = _sigmoid(g_ref[...].astype(F32))
        dg_ref[...] = (d * p_ref[...].astype(F32) * s * (1.0 - s)).astype(BF)
        dp_ref[...] = (d * s).astype(BF)

    return _ew(name, fn, [dh, gpre, pp], [], [(W, BF), (W, BF)], dep=dep)


def _loss_head(name, h, g, target):
    D = h.shape[1]

    def fn(first, h_ref, t_ref, g_ref, dh_ref, loss_ref, dg_ref):
        x = h_ref[...]
        r = lax.rsqrt(jnp.mean(x * x, axis=-1, keepdims=True) + EPS)
        xhat = x * r
        err = xhat * g_ref[...] - t_ref[...]
        part = 0.5 * jnp.sum(jnp.mean(err * err, axis=-1, keepdims=True), axis=0, keepdims=True)
        dy = err * (1.0 / D)
        dxhat = dy * g_ref[...]
        dh_ref[...] = r * (dxhat - xhat * jnp.mean(dxhat * xhat, axis=-1, keepdims=True))

        @pl.when(first)
        def _():
            loss_ref[...] = jnp.zeros_like(loss_ref)
            dg_ref[...] = jnp.zeros_like(dg_ref)

        loss_ref[...] += jnp.broadcast_to(part, loss_ref.shape)
        dg_ref[...] += jnp.sum(dy * xhat, axis=0, keepdims=True)

    return _ew(name, fn, [h, target], [_prow(g)], [(D, F32)], [((1, LANES), F32), ((1, D), F32)])


def _conv_specs(S, C, offs, l):
    T = _pick(S, (512, 256, 128, 64, 32, 16))
    Ct = [c for c in (512, 256, 128) if C % c == 0 and all(o % c == 0 for o in offs)][0]
    per = T // HALO
    last = S // HALO - 1

    def cur(off=0):
        return pl.BlockSpec((T, Ct), lambda j, i: (i, off // Ct + j))

    def prev(off=0):
        return pl.BlockSpec((HALO, Ct), lambda j, i: (jnp.maximum(i * per - 1, 0), off // Ct + j))

    def nxt(off=0):
        return pl.BlockSpec((HALO, Ct), lambda j, i: (jnp.minimum((i + 1) * per, last), off // Ct + j))

    wspec = pl.BlockSpec((None, 8, Ct), lambda j, i: (l, 0, j))
    return T, Ct, cur, prev, nxt, wspec


def _pad_taps(w):
    return jnp.concatenate([w.astype(F32), jnp.zeros((w.shape[0], 8 - w.shape[1], w.shape[2]), F32)], axis=1)


def _causal(cat, w_ref, K, T, lead):
    out = None
    for k in range(K):
        o = lead - (K - 1) + k
        term = w_ref[k:k + 1, :] * cat[o:o + T]
        out = term if out is None else out + term
    return out


def _anticausal(cat, w_ref, K, T):
    out = None
    for k in range(K):
        o = K - 1 - k
        term = w_ref[k:k + 1, :] * cat[o:o + T]
        out = term if out is None else out + term
    return out


def _scconv_fwd(name, proj, ob, oc, ox, taps, K, l):
    S = proj.shape[0]
    C = taps.shape[2]
    T, Ct, cur, prev, nxt, wspec = _conv_specs(S, C, (ob, oc, ox), l)

    def body(b_ref, c_ref, x_ref, cp_ref, xp_ref, w_ref, o_ref):
        i = pl.program_id(1)
        q = c_ref[...].astype(F32) * x_ref[...].astype(F32)
        qp = jnp.where(i == 0, 0.0, cp_ref[...].astype(F32) * xp_ref[...].astype(F32))
        cat = jnp.concatenate([qp, q], axis=0)
        o_ref[...] = (b_ref[...].astype(F32) * _causal(cat, w_ref, K, T, HALO)).astype(o_ref.dtype)

    return pl.pallas_call(
        body, name=name, grid=(C // Ct, S // T),
        in_specs=[cur(ob), cur(oc), cur(ox), prev(oc), prev(ox), wspec], out_specs=cur(),
        out_shape=jax.ShapeDtypeStruct((S, C), BF), compiler_params=_cp(("parallel", "arbitrary")),
    )(proj, proj, proj, proj, proj, taps)


def _scconv_bwd(name, dv, proj, ob, oc, ox, taps, K, l):
    S = proj.shape[0]
    C = taps.shape[2]
    T, Ct, cur, prev, nxt, wspec = _conv_specs(S, C, (ob, oc, ox), l)
    n_t = S // T

    def body(d_ref, b_ref, c_ref, x_ref, dn_ref, bn_ref, cp_ref, xp_ref, w_ref, db_ref, dc_ref, dx_ref, dw_ref):
        i = pl.program_id(1)
        c = c_ref[...].astype(F32)
        x = x_ref[...].astype(F32)
        d = d_ref[...].astype(F32)
        q = c * x
        qp = jnp.where(i == 0, 0.0, cp_ref[...].astype(F32) * xp_ref[...].astype(F32))
        catq = jnp.concatenate([qp, q], axis=0)
        cv = _causal(catq, w_ref, K, T, HALO)
        db_ref[...] = (d * cv).astype(BF)
        dcv = d * b_ref[...].astype(F32)
        dcvn = jnp.where(i == n_t - 1, 0.0, dn_ref[...].astype(F32) * bn_ref[...].astype(F32))
        catd = jnp.concatenate([dcv, dcvn], axis=0)
        dq = _anticausal(catd, w_ref, K, T)
        dc_ref[...] = (dq * x).astype(BF)
        dx_ref[...] = (dq * c).astype(BF)

        @pl.when(i == 0)
        def _():
            dw_ref[...] = jnp.zeros_like(dw_ref)

        for k in range(K):
            o = HALO - (K - 1) + k
            dw_ref[k:k + 1, :] += jnp.sum(dcv * catq[o:o + T], axis=0, keepdims=True)

    return pl.pallas_call(
        body, name=name, grid=(C // Ct, n_t),
        in_specs=[cur(), cur(ob), cur(oc), cur(ox), nxt(), nxt(ob), prev(oc), prev(ox), wspec],
        out_specs=[cur(), cur(), cur(), pl.BlockSpec((8, Ct), lambda j, i: (0, j))],
        out_shape=[jax.ShapeDtypeStruct((S, C), BF)] * 3 + [jax.ShapeDtypeStruct((8, C), F32)],
        compiler_params=_cp(("parallel", "arbitrary")),
    )(dv, proj, proj, proj, dv, proj, proj, proj, taps)


def _mconv_fwd(name, proj, ox, taps, K, bias, l):
    S = proj.shape[0]
    C = taps.shape[2]
    T, Ct, cur, prev, nxt, wspec = _conv_specs(S, C, (ox,), l)
    bspec = pl.BlockSpec((None, 1, Ct), lambda j, i: (l, 0, j))

    def body(x_ref, xp_ref, w_ref, b_ref, o_ref):
        i = pl.program_id(1)
        xp = jnp.where(i == 0, 0.0, xp_ref[...].astype(F32))
        cat = jnp.concatenate([xp, x_ref[...].astype(F32)], axis=0)
        pre = _causal(cat, w_ref, K, T, HALO) + b_ref[...]
        o_ref[...] = (pre * _sigmoid(pre)).astype(o_ref.dtype)

    return pl.pallas_call(
        body, name=name, grid=(C // Ct, S // T),
        in_specs=[cur(ox), prev(ox), wspec, bspec], out_specs=cur(),
        out_shape=jax.ShapeDtypeStruct((S, C), BF), compiler_params=_cp(("parallel", "arbitrary")),
    )(proj, proj, taps, bias)


def _mconv_bwd(name, dout, proj, ox, taps, K, bias, l):
    S = proj.shape[0]
    C = taps.shape[2]
    T, Ct, cur, prev, nxt, wspec = _conv_specs(S, C, (ox,), l)
    n_t = S // T
    bspec = pl.BlockSpec((None, 1, Ct), lambda j, i: (l, 0, j))

    def body(d_ref, dn_ref, x_ref, xp_ref, xn_ref, w_ref, b_ref, dx_ref, dw_ref, db_ref):
        i = pl.program_id(1)
        xp = jnp.where(i == 0, 0.0, xp_ref[...].astype(F32))
        cat3 = jnp.concatenate([xp, x_ref[...].astype(F32), xn_ref[...].astype(F32)], axis=0)
        pre = _causal(cat3, w_ref, K, T + HALO, HALO) + b_ref[...]
        dn = jnp.where(i == n_t - 1, 0.0, dn_ref[...].astype(F32))
        dext = jnp.concatenate([d_ref[...].astype(F32), dn], axis=0)
        s = _sigmoid(pre)
        dpre = dext * (s * (1.0 + pre * (1.0 - s)))
        dx_ref[...] = _anticausal(dpre, w_ref, K, T).astype(BF)
        dcur = dpre[:T]

        @pl.when(i == 0)
        def _():
            dw_ref[...] = jnp.zeros_like(dw_ref)
            db_ref[...] = jnp.zeros_like(db_ref)

        db_ref[...] += jnp.sum(dcur, axis=0, keepdims=True)
        for k in range(K):
            o = HALO - (K - 1) + k
            dw_ref[k:k + 1, :] += jnp.sum(dcur * cat3[o:o + T], axis=0, keepdims=True)

    return pl.pallas_call(
        body, name=name, grid=(C // Ct, n_t),
        in_specs=[cur(), nxt(), cur(ox), prev(ox), nxt(ox), wspec, bspec],
        out_specs=[cur(), pl.BlockSpec((8, Ct), lambda j, i: (0, j)), pl.BlockSpec((1, Ct), lambda j, i: (0, j))],
        out_shape=[jax.ShapeDtypeStruct((S, C), BF), jax.ShapeDtypeStruct((8, C), F32), jax.ShapeDtypeStruct((1, C), F32)],
        compiler_params=_cp(("parallel", "arbitrary")),
    )(dout, dout, proj, proj, proj, taps, bias)


def _tri_matmul(tri_bf, v):
    hi = v.astype(BF)
    r1 = v - hi.astype(F32)
    mid = r1.astype(BF)
    lo = (r1 - mid.astype(F32)).astype(BF)
    dot = functools.partial(jnp.dot, preferred_element_type=F32)
    return dot(tri_bf, hi) + dot(tri_bf, mid) + dot(tri_bf, lo)


def _dot_nt(a, b):
    return lax.dot_general(a, b, (((1,), (1,)), ((), ())), preferred_element_type=F32)


def _dot_tn(a, b):
    return lax.dot_general(a, b, (((0,), (0,)), ((), ())), preferred_element_type=F32)


def _dot_nn(a, b):
    return jnp.dot(a, b, preferred_element_type=F32)


def _ssd_chunk_scalars(dtr_ref, par_ref, L):
    row_i = lax.broadcasted_iota(jnp.int32, (L, L), 0)
    col_i = lax.broadcasted_iota(jnp.int32, (L, L), 1)
    tri = row_i >= col_i
    pre = dtr_ref[...] + par_ref[0:1, :]
    dt_all = _softplus(pre)
    A_row = -jnp.exp(par_ref[1:2, :])
    a_all = dt_all * A_row
    acum_all = _tri_matmul(tri.astype(BF), a_all)
    return tri, pre, dt_all, A_row, a_all, acum_all, acum_all.T


def _ssd_dims(xbc, heads):
    S, conv_dim = xbc.shape
    inner = heads * SSM_HEADDIM
    N = (conv_dim - inner) // (2 * SSM_GROUPS)
    gw = inner // SSM_GROUPS
    PP = gw // LANES
    L = min(SSM_CHUNK, S)
    assert N == LANES and gw % LANES == 0 and inner % (SSM_GROUPS * N) == 0 and S % L == 0
    return S, inner, N, gw, PP, L, S // L


def _ssd_params(dt_bias, A_log, Dp):
    depth, H = dt_bias.shape
    rows = jnp.stack([dt_bias, A_log, Dp], axis=1).astype(F32)
    rows = jnp.concatenate([rows, jnp.zeros((depth, 3, LANES - H), F32)], axis=2)
    return jnp.concatenate([rows, jnp.zeros((depth, 5, LANES), F32)], axis=1)


def _ssd_fwd(name, xbc, dt_raw, par, l, heads):
    S, inner, N, gw, PP, L, nc = _ssd_dims(xbc, heads)
    G = SSM_GROUPS

    def body(x_ref, b_ref, c_ref, dtr_ref, par_ref, y_ref, st_out_ref, st_ref):
        @pl.when(pl.program_id(0) == 0)
        def _():
            st_ref[...] = jnp.zeros_like(st_ref)

        tri, pre, dt_all, A_row, a_all, acum_all, acumT = _ssd_chunk_scalars(dtr_ref, par_ref, L)
        lane = lax.broadcasted_iota(jnp.int32, (L, LANES), 1)
        lane1 = lax.broadcasted_iota(jnp.int32, (1, LANES), 1)
        lo = lane < SSM_HEADDIM
        lo1 = lane1 < SSM_HEADDIM
        for g in range(G):
            Bb = b_ref[:, g * N:(g + 1) * N]
            Cb = c_ref[:, g * N:(g + 1) * N]
            BbT = Bb.astype(F32).T.astype(BF)
            Gm = _dot_nt(Cb, Bb)
            for j in range(PP):
                pj = g * PP + j
                h0, h1 = 2 * pj, 2 * pj + 1
                cols = slice(pj * LANES, (pj + 1) * LANES)
                x = x_ref[:, cols].astype(F32)
                dt_l = jnp.where(lo, dt_all[:, h0:h0 + 1], dt_all[:, h1:h1 + 1])
                ac0 = acum_all[:, h0:h0 + 1]
                ac1 = acum_all[:, h1:h1 + 1]
                ac_l = jnp.where(lo, ac0, ac1)
                E0 = jnp.exp(jnp.where(tri, ac0 - acumT[h0:h0 + 1, :], -1e30))
                E1 = jnp.exp(jnp.where(tri, ac1 - acumT[h1:h1 + 1, :], -1e30))
                xd = x * dt_l
                xdb = xd.astype(BF)
                yd = jnp.where(lo, _dot_nn((Gm * E0).astype(BF), xdb), _dot_nn((Gm * E1).astype(BF), xdb))
                prevT = st_ref[pj]
                st_out_ref[0, pj] = prevT
                P = _dot_nn(Cb, prevT.astype(BF))
                D_l = jnp.where(lo1, par_ref[2:3, h0:h0 + 1], par_ref[2:3, h1:h1 + 1])
                y_ref[:, cols] = (yd + P * jnp.exp(ac_l) + D_l * x).astype(y_ref.dtype)
                al0 = ac0[L - 1:L, :]
                al1 = ac1[L - 1:L, :]
                Wm = xd * jnp.exp(jnp.where(lo, al0, al1) - ac_l)
                eal = jnp.where(lo1, jnp.exp(al0), jnp.exp(al1))
                st_ref[pj] = eal * prevT + _dot_nn(BbT, Wm.astype(BF))

    gn = G * N
    return pl.pallas_call(
        body, name=name, grid=(nc,),
        in_specs=[pl.BlockSpec((L, inner), lambda c: (c, 0)), pl.BlockSpec((L, gn), lambda c: (c, inner // gn)),
                  pl.BlockSpec((L, gn), lambda c: (c, inner // gn + 1)),
                  pl.BlockSpec((L, LANES), lambda c: (c, 0)), pl.BlockSpec((None, 8, LANES), lambda c: (l, 0, 0))],
        out_specs=[pl.BlockSpec((L, inner), lambda c: (c, 0)), pl.BlockSpec((1, G * PP, N, LANES), lambda c: (c, 0, 0, 0))],
        out_shape=[jax.ShapeDtypeStruct((S, inner), BF), jax.ShapeDtypeStruct((nc, G * PP, N, LANES), F32)],
        scratch_shapes=[pltpu.VMEM((G * PP, N, LANES), F32)],
        compiler_params=_cp(("arbitrary",)),
    )(xbc, xbc, xbc, dt_raw, par)


def _ssd_bwd(name, dy, xbc, dt_raw, states, par, l, heads):
    S, inner, N, gw, PP, L, nc = _ssd_dims(xbc, heads)
    G = SSM_GROUPS

    def body(dy_ref, x_ref, b_ref, c_ref, dtr_ref, par_ref, st_in_ref, d_ref, ddt_ref, dpar_ref, dst_ref):
        @pl.when(pl.program_id(0) == 0)
        def _():
            dst_ref[...] = jnp.zeros_like(dst_ref)
            dpar_ref[...] = jnp.zeros_like(dpar_ref)

        tri, pre, dt_all, A_row, a_all, acum_all, acumT = _ssd_chunk_scalars(dtr_ref, par_ref, L)
        lane = lax.broadcasted_iota(jnp.int32, (L, LANES), 1)
        lane1 = lax.broadcasted_iota(jnp.int32, (1, LANES), 1)
        rowl = lax.broadcasted_iota(jnp.int32, (L, LANES), 0)
        lo = lane < SSM_HEADDIM
        lo1 = lane1 < SSM_HEADDIM
        triT = lax.broadcasted_iota(jnp.int32, (L, L), 0) <= lax.broadcasted_iota(jnp.int32, (L, L), 1)
        sel_r = lax.broadcasted_iota(jnp.int32, (3 * LANES, LANES), 0)
        sel_c = lax.broadcasted_iota(jnp.int32, (3 * LANES, LANES), 1)
        dac_all = jnp.zeros((L, LANES), F32)
        xds_all = jnp.zeros((L, LANES), F32)
        dD_row = jnp.zeros((1, LANES), F32)

        def half_sums(v):
            return (jnp.sum(jnp.where(lo1, v, 0.0), axis=1, keepdims=True), jnp.sum(jnp.where(lo1, 0.0, v), axis=1, keepdims=True))

        def dot2(v, sel):
            hi = v.astype(BF)
            return _dot_nn(hi, sel) + _dot_nn((v - hi.astype(F32)).astype(BF), sel)

        for pj in range(G * PP):
            g, j = divmod(pj, PP)
            if j == 0:
                Bb = b_ref[:, g * N:(g + 1) * N]
                Cb = c_ref[:, g * N:(g + 1) * N]
                CbT = Cb.astype(F32).T.astype(BF)
                Gm = _dot_nt(Cb, Bb)
                GmT = _dot_nt(Bb, Cb)
                dG = jnp.zeros((L, L), F32)
                dGT = jnp.zeros((L, L), F32)
                dBacc = jnp.zeros((L, N), F32)
                dCacc = jnp.zeros((L, N), F32)
            h0, h1 = 2 * pj, 2 * pj + 1
            to_h0 = (sel_r < LANES) | ((sel_r >= 2 * LANES) & (sel_r < 2 * LANES + SSM_HEADDIM))
            sel3 = jnp.where(sel_c == jnp.where(to_h0, h0, h1), 1.0, 0.0).astype(BF)
            sel1 = sel3[2 * LANES:]
            sl = slice(pj * LANES, (pj + 1) * LANES)
            x = x_ref[:, sl].astype(F32)
            dyv = dy_ref[:, sl].astype(F32)
            dt_l = jnp.where(lo, dt_all[:, h0:h0 + 1], dt_all[:, h1:h1 + 1])
            ac0 = acum_all[:, h0:h0 + 1]
            ac1 = acum_all[:, h1:h1 + 1]
            r0 = acumT[h0:h0 + 1, :]
            r1 = acumT[h1:h1 + 1, :]
            ac_l = jnp.where(lo, ac0, ac1)
            E0 = jnp.exp(jnp.where(tri, ac0 - r0, -1e30))
            E1 = jnp.exp(jnp.where(tri, ac1 - r1, -1e30))
            E0T = jnp.exp(jnp.where(triT, r0 - ac0, -1e30))
            E1T = jnp.exp(jnp.where(triT, r1 - ac1, -1e30))
            xd = x * dt_l
            xdb = xd.astype(BF)
            M0 = Gm * E0
            M1 = Gm * E1
            ea_l = jnp.exp(ac_l)
            al0 = ac0[L - 1:L, :]
            al1 = ac1[L - 1:L, :]
            dte_l = jnp.exp(jnp.where(lo, al0, al1) - ac_l)
            Wm = xd * dte_l
            prevT = st_in_ref[0, pj]
            prevTb = prevT.astype(BF)
            P = _dot_nn(Cb, prevTb)
            D_l = jnp.where(lo1, par_ref[2:3, h0:h0 + 1], par_ref[2:3, h1:h1 + 1])
            dx = D_l * dyv
            dD0, dD1 = half_sums(jnp.sum(dyv * x, axis=0, keepdims=True))
            dyb = dyv.astype(BF)
            dy0b = jnp.where(lo, dyv, 0.0).astype(BF)
            dy1b = jnp.where(lo, 0.0, dyv).astype(BF)
            dM0 = _dot_nt(dy0b, xdb)
            dM1 = _dot_nt(dy1b, xdb)
            dM0T = _dot_nt(xdb, dy0b)
            dM1T = _dot_nt(xdb, dy1b)
            M0T = GmT * E0T
            M1T = GmT * E1T
            dxd = jnp.where(lo, _dot_nn(M0T.astype(BF), dyb), _dot_nn(M1T.astype(BF), dyb))
            dG = dG + dM0 * E0 + dM1 * E1
            dGT = dGT + dM0T * E0T + dM1T * E1T
            z0 = dM0 * M0 - dM0T * M0T
            z1 = dM1 * M1 - dM1T * M1T
            dP = dyv * ea_l
            dPb = dP.astype(BF)
            dCacc = dCacc + _dot_nt(dPb, prevTb)
            dprevT = _dot_nn(CbT, dPb)
            dnewT = dst_ref[pj]
            dnewTb = dnewT.astype(BF)
            e0 = jnp.exp(al0)
            e1 = jnp.exp(al1)
            dprevT = dprevT + jnp.where(lo1, e0, e1) * dnewT
            u0, u1 = half_sums(jnp.sum(dnewT * prevT, axis=0, keepdims=True))
            dW = _dot_nn(Bb, dnewTb)
            dBacc = dBacc + _dot_nt(Wm.astype(BF), dnewTb)
            dxd = dxd + dW * dte_l
            tt = dW * Wm
            t0, t1 = half_sums(jnp.sum(tt, axis=0, keepdims=True))
            dal0 = u0 * e0 + t0
            dal1 = u1 * e1 + t1
            dac_all = dac_all + dot2(jnp.concatenate([z0, z1, dP * P - tt], axis=1), sel3)
            dac_all = dac_all + jnp.where(rowl == L - 1, jnp.where(lane == h0, dal0, 0.0) + jnp.where(lane == h1, dal1, 0.0), 0.0)
            dx = dx + dxd * dt_l
            xds_all = xds_all + dot2(dxd * x, sel1)
            dst_ref[pj] = dprevT
            d_ref[:, sl] = dx.astype(d_ref.dtype)
            dD_row = dD_row + jnp.where(lane1 == h0, dD0, 0.0) + jnp.where(lane1 == h1, dD1, 0.0)
            if j == PP - 1:
                d_ref[:, inner + g * N:inner + (g + 1) * N] = (dBacc + _dot_nn(dGT.astype(BF), Cb)).astype(d_ref.dtype)
                d_ref[:, inner + (G + g) * N:inner + (G + g + 1) * N] = (dCacc + _dot_nn(dG.astype(BF), Bb)).astype(d_ref.dtype)

        row_i = lax.broadcasted_iota(jnp.int32, (L, L), 0)
        col_i = lax.broadcasted_iota(jnp.int32, (L, L), 1)
        da_all = _tri_matmul((row_i <= col_i).astype(BF), dac_all)
        real = lane < heads
        ddt_all = da_all * A_row + xds_all
        draw = jnp.where(real, ddt_all * _sigmoid(pre), 0.0)
        ddt_ref[...] = draw
        dpar_ref[0:1, :] += jnp.sum(draw, axis=0, keepdims=True)
        dpar_ref[1:2, :] += jnp.sum(jnp.where(real, da_all * a_all, 0.0), axis=0, keepdims=True)
        dpar_ref[2:3, :] += dD_row

    gn = G * N
    conv_dim = xbc.shape[1]
    rev = lambda c: nc - 1 - c
    return pl.pallas_call(
        body, name=name, grid=(nc,),
        in_specs=[pl.BlockSpec((L, inner), lambda c: (rev(c), 0)), pl.BlockSpec((L, inner), lambda c: (rev(c), 0)),
                  pl.BlockSpec((L, gn), lambda c: (rev(c), inner // gn)), pl.BlockSpec((L, gn), lambda c: (rev(c), inner // gn + 1)),
                  pl.BlockSpec((L, LANES), lambda c: (rev(c), 0)), pl.BlockSpec((None, 8, LANES), lambda c: (l, 0, 0)),
                  pl.BlockSpec((1, G * PP, N, LANES), lambda c: (rev(c), 0, 0, 0))],
        out_specs=[pl.BlockSpec((L, conv_dim), lambda c: (rev(c), 0)), pl.BlockSpec((L, LANES), lambda c: (rev(c), 0)),
                   pl.BlockSpec((8, LANES), lambda c: (0, 0))],
        out_shape=[jax.ShapeDtypeStruct((S, conv_dim), BF), jax.ShapeDtypeStruct((S, LANES), F32), jax.ShapeDtypeStruct((8, LANES), F32)],
        scratch_shapes=[pltpu.VMEM((G * PP, N, LANES), F32)],
        compiler_params=_cp(("arbitrary",)),
    )(dy, xbc, xbc, xbc, dt_raw, par, states)


def _adamw(g, w, m, v):
    m2 = ADAM_B1 * m + (1.0 - ADAM_B1) * g
    v2 = ADAM_B2 * v + (1.0 - ADAM_B2) * (g * g)
    m_hat = m2 / (1.0 - ADAM_B1 ** ADAM_STEP)
    v_hat = v2 / (1.0 - ADAM_B2 ** ADAM_STEP)
    delta = -ADAM_LR * (m_hat / (jnp.sqrt(v_hat) + ADAM_EPS) + ADAM_WD * w)
    return delta, m2, v2


def _flat_tile(R):
    return _pick(R, (FLAT_ROW_TILE, 1024, 512, 256, 128, 64, 32, 16, 8))


def _sum_adam(name, lands, off, w, m, v):
    depth, r, c = w.shape
    cap = max(16, (4 * 1024 * 1024) // (N_DEV * c * 2))
    row_tiles = [t for t in (512, 256, 128, 64, 32, 16) if r % t == 0 and off % t == 0 and t <= cap]
    if row_tiles:
        tr, tc = row_tiles[0], c
        ob = off // tr
        n_t = r // tr
        spec = pl.BlockSpec((None, tr, c), lambda l, t: (l, t, 0))
        land_specs = [pl.BlockSpec((N_DEV, tr, c), lambda l, t, i=i: (0, jnp.where(l == i, ob + t, ob), 0)) for i in range(depth)]
    else:
        assert off == 0 and lands[0].shape[1] == r and c % LANES == 0
        tc = LANES
        n_t = c // tc
        spec = pl.BlockSpec((None, r, tc), lambda l, t: (l, 0, t))
        land_specs = [pl.BlockSpec((N_DEV, r, tc), lambda l, t, i=i: (0, 0, jnp.where(l == i, t, 0))) for i in range(depth)]

    def body(*refs):
        land_refs = refs[:depth]
        w_ref, m_ref, v_ref, g_ref, d_ref, m2_ref, v2_ref = refs[depth:]
        l = pl.program_id(0)
        for i in range(depth):
            @pl.when(l == i)
            def _(i=i):
                g = land_refs[i][0].astype(F32)
                for k in range(1, N_DEV):
                    g = g + land_refs[i][k].astype(F32)
                g_ref[...] = g
                d_ref[...], m2_ref[...], v2_ref[...] = _adamw(g, w_ref[...], m_ref[...], v_ref[...])

    return pl.pallas_call(
        body, name=name, grid=(depth, n_t),
        in_specs=land_specs + [spec, spec, spec],
        out_specs=[spec] * 4, out_shape=[jax.ShapeDtypeStruct((depth, r, c), F32)] * 4,
        compiler_params=_cp(("arbitrary", "arbitrary")),
    )(*lands, w, m, v)


def _sum8(name, parts):
    R = parts.shape[1]
    TR = _flat_tile(R)

    def body(p_ref, g_ref):
        g = p_ref[0]
        for k in range(1, N_DEV):
            g = g + p_ref[k]
        g_ref[...] = g

    return pl.pallas_call(
        body, name=name, grid=(R // TR,),
        in_specs=[pl.BlockSpec((N_DEV, TR, LANES), lambda i: (0, i, 0))],
        out_specs=pl.BlockSpec((TR, LANES), lambda i: (i, 0)), out_shape=jax.ShapeDtypeStruct((R, LANES), F32),
        compiler_params=_cp(("parallel",)),
    )(parts)


def _adam_flat(name, g, w, m, v):
    R = w.shape[0]
    TR = _flat_tile(R)

    def body(g_ref, w_ref, m_ref, v_ref, d_ref, m2_ref, v2_ref):
        d_ref[...], m2_ref[...], v2_ref[...] = _adamw(g_ref[...], w_ref[...], m_ref[...], v_ref[...])

    spec = pl.BlockSpec((TR, LANES), lambda i: (i, 0))
    return pl.pallas_call(
        body, name=name, grid=(R // TR,), in_specs=[spec] * 4, out_specs=[spec] * 3,
        out_shape=[jax.ShapeDtypeStruct((R, LANES), F32)] * 3, compiler_params=_cp(("parallel",)),
    )(g, w, m, v)


PART_ROWS = 16


def _nrows(shape):
    n = 1
    for s in shape:
        n *= s
    r = -(-n // LANES)
    return -(-r // PART_ROWS) * PART_ROWS


def _as_rows(a):
    n = a.size
    r = _nrows(a.shape)
    f = a.reshape(-1)
    if r * LANES != n:
        f = jnp.concatenate([f, jnp.zeros((r * LANES - n,), a.dtype)])
    return f.reshape(r, LANES)


def _pack(arrs, mult=PART_ROWS):
    cat = jnp.concatenate([_as_rows(a) for a in arrs], axis=0)
    pad = (-cat.shape[0]) % mult
    if pad:
        cat = jnp.concatenate([cat, jnp.zeros((pad, LANES), cat.dtype)], axis=0)
    return cat


def _unpack(flat, shapes):
    lead = flat.shape[:-2]
    out = []
    o = 0
    for shp in shapes:
        n = 1
        for s in shp:
            n *= s
        r = _nrows(shp)
        blk = flat[..., o:o + r, :].reshape(lead + (r * LANES,))
        out.append(blk[..., :n].reshape(lead + tuple(shp)))
        o += r
    return out


def _full_from_shards(st):
    return st.reshape(st.shape[0] * st.shape[1], st.shape[2])


def _shards_from_full(full):
    return full.reshape(N_DEV, full.shape[0] // N_DEV, full.shape[1])


def _ffn_fwd(tag, h, g, wgT, wuT, wd, dep=None):
    xn = _rms_fwd(tag + "_rms", h, g, dep=dep)
    a, b, hmid = _ffn_up(tag + "_up", xn, wgT, wuT)
    hout = _mm(tag + "_down", hmid, wd, out_dtype=F32, res=h, alpha=0.5)
    return hout, (xn, a, b, hmid)


def _ffn_bwd(tag, dh_out, h, g, wgT, wuT, wd, saved, dep=None):
    xn, a, b, hmid = saved
    da, db = _ffn_dact(tag + "_d_act", dh_out, wd, a, b, dep=dep)
    d_wd = _mm(tag + "_d_wd", hmid, dh_out, ta=True, alpha=0.5)
    d_wgT = _mm(tag + "_d_wg", da, xn, ta=True)
    d_wuT = _mm(tag + "_d_wu", db, xn, ta=True)
    dxn = _mm(tag + "_d_xn_g", da, wgT, out_dtype=F32)
    dxn = _mm(tag + "_d_xn_u", db, wuT, out_dtype=F32, res=dxn)
    dh, dg = _rms_bwd(tag + "_d_rms", dxn, h, g, dh_out)
    return dh, dg, d_wgT, d_wuT, d_wd


SEG_NAMES = ['scb', 'scc', 'scx', 'z', 'xbc', 'dt', 'ga', 'gm']
PERM = ['z', 'scb', 'scc', 'scx', 'ga', 'gm', 'xbc']


def _seg_layout(dims):
    D, inner, conv_dim, H = dims[:4]
    widths = dict(zip(SEG_NAMES, [D, D, D, inner, conv_dim, H, D, D]))
    offs, o = {}, 0
    for n in SEG_NAMES:
        offs[n] = (o, widths[n])
        o += widths[n]
    poffs, o = {}, 0
    for n in PERM:
        poffs[n] = (o, widths[n])
        o += widths[n]
    return offs, poffs


def _perm_w_in(w_inT, dims):
    offs, _ = _seg_layout(dims)
    wp = jnp.concatenate([w_inT[offs[n][0]:offs[n][0] + offs[n][1]] for n in PERM], axis=0)
    o, w = offs['dt']
    wdt = jnp.concatenate([w_inT[o:o + w], jnp.zeros((LANES - w, w_inT.shape[1]), w_inT.dtype)], axis=0)
    return wp, wdt


def _unperm_d_w_in(d_wp, d_wdt, dims):
    offs, poffs = _seg_layout(dims)
    H = dims[3]
    return jnp.concatenate([d_wdt[:H] if n == 'dt' else d_wp[poffs[n][0]:poffs[n][0] + poffs[n][1]] for n in SEG_NAMES], axis=0)


def _mixer_fwd(h, W, dims, dep=None):
    H, Ksc, Km = dims[3:]
    l = W['l']
    _, poffs = _seg_layout(dims)

    def seg(n):
        o, w = poffs[n]
        assert o % w == 0
        return (proj, w, o // w)

    u = _rms_fwd("mix_rms", h, W['mix_norm'], dep=dep)
    proj = _mm("inproj", u, W['w_in_p'], tb=True)
    dt_raw = _mm("inproj_dt", u, W['w_dt'], tb=True, out_dtype=F32)
    v = _scconv_fwd("scconv_f", proj, poffs['scb'][0], poffs['scc'][0], poffs['scx'][0], W['sc_taps'], Ksc, l)
    ya = _mm("sc_out", v, W['sc_w_out'])
    xbc = _mconv_fwd("mconv_f", proj, poffs['xbc'][0], W['m_taps'], Km, W['m_conv_b'], l)
    y, states = _ssd_fwd("ssd_f", xbc, dt_raw, W['ssd_par'], l, H)
    yn = _gnorm_fwd("gnorm_f", y, seg('z'), W['m_norm'])
    ym = _mm("m_out", yn, W['m_w_out'])
    merged = _merge_fwd("merge_f", seg('ga'), seg('gm'), ya, ym)
    hout = _mm("w_o", merged, W['w_o'], out_dtype=F32, res=h)
    return hout, (u, proj, dt_raw, v, ya, xbc, y, states, yn, ym, merged)


def _mixer_bwd(dh_out, h, W, dims, saved, dep=None):
    u, proj, dt_raw, v, ya, xbc, y, states, yn, ym, merged = saved
    H, Ksc, Km = dims[3:]
    l = W['l']
    _, poffs = _seg_layout(dims)

    def seg(n):
        o, w = poffs[n]
        return (proj, w, o // w)

    g = {}
    dmerged = _mm("d_merged", dh_out, W['w_o'], tb=True, dep=dep)
    g['w_o'] = _mm("d_w_o", merged, dh_out, ta=True)
    dga, dgm, dya, dym = _merge_bwd("merge_b", dmerged, seg('ga'), seg('gm'), ya, ym)
    g['sc_w_out'] = _mm("d_sc_w_out", v, dya, ta=True)
    dv = _mm("d_v", dya, W['sc_w_out'], tb=True)
    g['m_w_out'] = _mm("d_m_w_out", yn, dym, ta=True)
    dyn = _mm("d_yn", dym, W['m_w_out'], tb=True)
    dy, dz, d_mnorm = _gnorm_bwd("gnorm_b", dyn, y, seg('z'), W['m_norm'])
    g['m_norm'] = d_mnorm.reshape(-1)
    dxbc_post, ddt, dpar = _ssd_bwd("ssd_b", dy, xbc, dt_raw, states, W['ssd_par'], l, H)
    g['m_dt_bias'] = dpar[0, :H]
    g['m_A_log'] = dpar[1, :H]
    g['m_D'] = dpar[2, :H]
    dxbc, d_mcw, d_mcb = _mconv_bwd("mconv_b", dxbc_post, proj, poffs['xbc'][0], W['m_taps'], Km, W['m_conv_b'], l)
    g['m_conv_w'] = d_mcw[:Km]
    g['m_conv_b'] = d_mcb.reshape(-1)
    dscb, dscc, dscx, d_scw = _scconv_bwd("scconv_b", dv, proj, poffs['scb'][0], poffs['scc'][0], poffs['scx'][0],
                                          W['sc_taps'], Ksc, l)
    g['sc_conv_w'] = d_scw[:Ksc]
    dproj = jnp.concatenate([dz, dscb, dscc, dscx, dga, dgm, dxbc], axis=1)
    du = _mm("d_u_main", dproj, W['w_in_p'], out_dtype=F32)
    du = _mm("d_u_dt", ddt, W['w_dt'], out_dtype=F32, res=du)
    d_wp = _mm("d_w_in_main", dproj, u, ta=True)
    d_wdt = _mm("d_w_in_dt", ddt, u, ta=True)
    g['w_in'] = _unperm_d_w_in(d_wp, d_wdt, dims)
    dh, dg = _rms_bwd("mix_d_rms", du, h, W['mix_norm'], dh_out)
    g['mix_norm'] = dg.reshape(-1)
    return dh, g


def _ple_layer_fwd(h, p_l, W):
    xn = _rms_fwd("ple_rms", h, W['ple_norm'])
    gpre = _mm("ple_gate", xn, W['ple_w_gate'])
    pp = _mm("ple_proj", p_l, W['ple_w_proj'], tb=True)
    hout = _ple_fwd("ple_f", h, gpre, pp)
    return hout, (xn, gpre, pp)


def _ple_layer_bwd(dh_out, h, p_l, W, saved, dep=None):
    xn, gpre, pp = saved
    g = {}
    dgpre, dpp = _ple_bwd("ple_b", dh_out, gpre, pp, dep=dep)
    g['ple_w_proj'] = _mm("d_ple_proj", dpp, p_l, ta=True)
    g['ple_w_gate'] = _mm("d_ple_gate", xn, dgpre, ta=True)
    dxn = _mm("d_ple_xn", dgpre, W['ple_w_gate'], tb=True)
    dh, dg = _rms_bwd("ple_d_rms", dxn, h, W['ple_norm'], dh_out)
    g['ple_norm'] = dg.reshape(-1)
    return dh, g


def kernel(x, p, ffn1_norm, ffn1_wg, ffn1_wu, ffn1_wd, mix_norm, w_in, sc_conv_w, sc_w_out, m_conv_w, m_conv_b, m_dt_bias, m_A_log, m_D, m_norm, m_w_out, w_o, ffn2_norm, ffn2_wg, ffn2_wu, ffn2_wd, ple_norm, ple_w_gate, ple_w_proj, final_norm, loss_target, m_ffn1_norm, m_ffn1_wg, m_ffn1_wu, m_ffn1_wd, m_mix_norm, m_w_in, m_sc_conv_w, m_sc_w_out, m_m_conv_w, m_m_conv_b, m_m_dt_bias, m_m_A_log, m_m_D, m_m_norm, m_m_w_out, m_w_o, m_ffn2_norm, m_ffn2_wg, m_ffn2_wu, m_ffn2_wd, m_ple_norm, m_ple_w_gate, m_ple_w_proj, m_final_norm, v_ffn1_norm, v_ffn1_wg, v_ffn1_wu, v_ffn1_wd, v_mix_norm, v_w_in, v_sc_conv_w, v_sc_w_out, v_m_conv_w, v_m_conv_b, v_m_dt_bias, v_m_A_log, v_m_D, v_m_norm, v_m_w_out, v_w_o, v_ffn2_norm, v_ffn2_wg, v_ffn2_wu, v_ffn2_wd, v_ple_norm, v_ple_w_gate, v_ple_w_proj, v_final_norm):
    args = (x, p, ffn1_norm, ffn1_wg, ffn1_wu, ffn1_wd, mix_norm, w_in, sc_conv_w, sc_w_out, m_conv_w, m_conv_b, m_dt_bias, m_A_log, m_D, m_norm, m_w_out, w_o, ffn2_norm, ffn2_wg, ffn2_wu, ffn2_wd, ple_norm, ple_w_gate, ple_w_proj, final_norm, loss_target, m_ffn1_norm, m_ffn1_wg, m_ffn1_wu, m_ffn1_wd, m_mix_norm, m_w_in, m_sc_conv_w, m_sc_w_out, m_m_conv_w, m_m_conv_b, m_m_dt_bias, m_m_A_log, m_m_D, m_m_norm, m_m_w_out, m_w_o, m_ffn2_norm, m_ffn2_wg, m_ffn2_wu, m_ffn2_wd, m_ple_norm, m_ple_w_gate, m_ple_w_proj, m_final_norm, v_ffn1_norm, v_ffn1_wg, v_ffn1_wu, v_ffn1_wd, v_mix_norm, v_w_in, v_sc_conv_w, v_sc_w_out, v_m_conv_w, v_m_conv_b, v_m_dt_bias, v_m_A_log, v_m_D, v_m_norm, v_m_w_out, v_w_o, v_ffn2_norm, v_ffn2_wg, v_ffn2_wu, v_ffn2_wd, v_ple_norm, v_ple_w_gate, v_ple_w_proj, v_final_norm)
    names = ARG_NAMES + ['m_' + n for n in WEIGHTS] + ['v_' + n for n in WEIGHTS]
    A = dict(zip(names, args))
    depth = ffn1_norm.shape[0]
    me = 4 * lax.axis_index("x") + 2 * lax.axis_index("y") + lax.axis_index("c")

    dims = (x.shape[-1], m_norm.shape[1], m_conv_b.shape[1], m_dt_bias.shape[1], sc_conv_w.shape[1], m_conv_w.shape[1])
    kind = dict(BIG)

    def work(n, prefix=''):
        return jnp.swapaxes(A[prefix + n], 1, 2) if kind[n] == 'col' else A[prefix + n]

    wb = {n: work(n).astype(BF) for n, _ in BIG}
    srcs = [[wb[ms[0]] if len(ms) == 1 else jnp.concatenate([wb[n] for n in ms], axis=1) for ms in stage] for stage in STAGES]
    conv_g = _unpack(_exchange("gather_conv_taps", _pack([A[n] for n in CONVW]), True), [A[n].shape for n in CONVW])
    taps = {}
    for n, st in zip(CONVW, conv_g):
        taps[n] = _pad_taps(jnp.transpose(st, (1, 2, 0, 3)).reshape(depth, st.shape[2], N_DEV * st.shape[3]))
    ssd_par = _ssd_params(m_dt_bias, m_A_log, m_D)
    small3 = {n: A[n].reshape(depth, 1, -1) for n in SMALL}

    def stage_weights(W, s, l, lands):
        for ms, land, src in zip(STAGES[s], lands, srcs[s]):
            land = lax.dynamic_update_slice(land, src[l][None], (me, 0, 0))
            off = 0
            for n in ms:
                r = wb[n].shape[1]
                W[n] = _full_from_shards(land if len(ms) == 1 else land[:, off:off + r])
                off += r
        if s == 1:
            W['w_in_p'], W['w_dt'] = _perm_w_in(W.pop('w_in'), dims)

    flight = {}

    def begin_layer(l, dep):
        for s in range(len(STAGES)):
            sems, lands, dep = _xchg_begin(f"gather_begin{l}{'abc'[s]}", srcs[s], l, dep)
            flight[(l, s)] = (sems, lands)
        return dep

    def end_stage(W, l, s, after):
        sems, lands = flight.pop((l, s))
        stage_weights(W, s, l, _xchg_end(f"gather_end{l}{'abc'[s]}", srcs[s], lands, sems, l, after))

    tok = begin_layer(0, taps['sc_conv_w'])
    h = x[0]
    saved = []
    layers = []
    for l in range(depth):
        W = {n: (small3[n], l) for n in SMALL}
        W.update(l=l, sc_taps=taps['sc_conv_w'], m_taps=taps['m_conv_w'], m_conv_b=small3['m_conv_b'], ssd_par=ssd_par)
        layers.append(W)
        end_stage(W, l, 0, tok if l == 0 else h)
        h1, s1 = _ffn_fwd("ffn1", h, W['ffn1_norm'], W['ffn1_wg'], W['ffn1_wu'], W['ffn1_wd'])
        end_stage(W, l, 1, h1)
        tok = begin_layer(l + 1, W['w_dt']) if l + 1 < depth else None
        h2, s2 = _mixer_fwd(h1, W, dims, dep=tok)
        end_stage(W, l, 2, h2)
        h3, s3 = _ffn_fwd("ffn2", h2, W['ffn2_norm'], W['ffn2_wg'], W['ffn2_wu'], W['ffn2_wd'])
        h4, s4 = _ple_layer_fwd(h3, p[l, 0], W)
        saved.append((h, h1, h2, h3, s1, s2, s3, s4))
        h = h4

    dh, loss_row, d_final = _loss_head("loss_head", h, final_norm, loss_target[0])
    loss = lax.psum(loss_row[0, 0], ("x", "y", "c"))

    def send_bufs(g, s):
        return [jnp.concatenate([_shards_from_full(g[n]) for n in ms], axis=1) if len(ms) > 1
                else _shards_from_full(g[ms[0]]) for ms in STAGES[s]]

    grads = [None] * depth
    pending = []

    def send_stage(g, l, s, dep):
        send = send_bufs(g, s)
        sems, lands, tok = _xchg_begin(f"scatter_begin{l}{'abc'[s]}", send, None, dep)
        pending.append((l, s, send, lands, sems))
        return tok

    tok = loss.reshape(1, 1)
    for l in reversed(range(depth)):
        W = layers[l]
        h0, h1, h2, h3, s1, s2, s3, s4 = saved[l]
        g = {}
        dh, g4 = _ple_layer_bwd(dh, h3, p[l, 0], W, s4, dep=tok)
        g.update(g4)
        dh, dg, d_wg, d_wu, d_wd = _ffn_bwd("ffn2", dh, h2, W['ffn2_norm'], W['ffn2_wg'], W['ffn2_wu'], W['ffn2_wd'], s3)
        g.update(ffn2_norm=dg.reshape(-1), ffn2_wg=d_wg, ffn2_wu=d_wu, ffn2_wd=d_wd)
        tok = send_stage(g, l, 2, dh)
        dh, g2 = _mixer_bwd(dh, h1, W, dims, s2, dep=tok)
        g.update(g2)
        tok = send_stage(g, l, 1, dh)
        dh, dg, d_wg, d_wu, d_wd = _ffn_bwd("ffn1", dh, h0, W['ffn1_norm'], W['ffn1_wg'], W['ffn1_wu'], W['ffn1_wd'], s1, dep=tok)
        g.update(ffn1_norm=dg.reshape(-1), ffn1_wg=d_wg, ffn1_wu=d_wu, ffn1_wd=d_wd)
        grads[l] = g
        tok = send_stage(g, l, 0, dh)
    grad_x = dh[None]

    g_lands = [[None] * len(STAGES) for _ in range(depth)]
    big_res = [{}, {}, {}, {}]

    def finish(entries, after):
        for l, s, send, lands, sems in entries:
            got = _xchg_end(f"scatter_end{l}{'abc'[s]}", send, lands, sems, None, after)
            after = got[0]
            g_lands[l][s] = [lax.dynamic_update_slice(o, lax.dynamic_slice_in_dim(b, me, 1, axis=0), (me, 0, 0)) for o, b in zip(got, send)]
        return after

    def adam_stages(stages):
        res = None
        for s in stages:
            for gi, ms in enumerate(STAGES[s]):
                off = 0
                for n in ms:
                    res = _sum_adam("adamw_" + n, [g_lands[l][s][gi] for l in range(depth)], off, work(n), work(n, 'm_'), work(n, 'v_'))
                    for k in range(4):
                        big_res[k][n] = jnp.swapaxes(res[k], 1, 2) if kind[n] == 'col' else res[k]
                    off += wb[n].shape[1]
        return res[0]

    after = finish(pending[:-1], dh)
    after = adam_stages(range(1, len(STAGES)))
    after = finish(pending[-1:], after)
    adam_stages([0])

    small_names = SMALL + CONVW
    small_parts = [jnp.stack([grads[l][n] for l in range(depth)]) for n in small_names] + [d_final.reshape(-1)]
    small_sum = _sum8("sum_small", _exchange("gather_small_grads", _pack(small_parts), True, dep=after))
    sg = dict(zip(small_names + ['final_norm'], _unpack(small_sum, [a.shape for a in small_parts])))
    for n in CONVW:
        c = A[n].shape[-1]
        sg[n] = lax.dynamic_slice_in_dim(sg[n], me * c, c, axis=2)
    s_order = small_names + ['final_norm']
    s_shapes = [sg[n].shape for n in s_order]
    s_out = _adam_flat("adamw_small", _pack([sg[n] for n in s_order]), _pack([A[n] for n in s_order]),
                       _pack([A['m_' + n] for n in s_order]), _pack([A['v_' + n] for n in s_order]))
    small_res = [sg] + [dict(zip(s_order, _unpack(flat, s_shapes))) for flat in s_out]

    outs = [loss, grad_x]
    for k in range(4):
        for n in WEIGHTS:
            outs.append(big_res[k][n] if n in big_res[k] else small_res[k][n])
    return tuple(outs)
```

```python
import functools

import jax
import jax.numpy as jnp
from jax import lax
from jax.experimental import pallas as pl
from jax.experimental.pallas import tpu as pltpu

BF = jnp.bfloat16
F32 = jnp.float32

EPS = 1e-6
N_DEV = 8
LANES = 128
SSM_GROUPS = 4
SSM_HEADDIM = 64
SSM_CHUNK = 128
HALO = 16
VMEM_LIMIT = 56 * 1024 * 1024
FLAT_ROW_TILE = 2048

ADAM_LR = 0.001
ADAM_B1 = 0.9
ADAM_B2 = 0.999
ADAM_EPS = 1e-08
ADAM_WD = 0.01
ADAM_STEP = 10

MESH = pl.DeviceIdType.MESH

ARG_NAMES = ['x', 'p', 'ffn1_norm', 'ffn1_wg', 'ffn1_wu', 'ffn1_wd', 'mix_norm', 'w_in', 'sc_conv_w', 'sc_w_out', 'm_conv_w', 'm_conv_b', 'm_dt_bias', 'm_A_log', 'm_D', 'm_norm', 'm_w_out', 'w_o', 'ffn2_norm', 'ffn2_wg', 'ffn2_wu', 'ffn2_wd', 'ple_norm', 'ple_w_gate', 'ple_w_proj', 'final_norm', 'loss_target']
WEIGHTS = ARG_NAMES[2:26]
BIG = [('ffn1_wg', 'col'), ('ffn1_wu', 'col'), ('ffn1_wd', 'row'), ('w_in', 'col'), ('sc_w_out', 'row'),
       ('m_w_out', 'row'), ('w_o', 'row'), ('ffn2_wg', 'col'), ('ffn2_wu', 'col'), ('ffn2_wd', 'row'),
       ('ple_w_gate', 'row'), ('ple_w_proj', 'col')]
CONVW = ['sc_conv_w', 'm_conv_w']
SMALL = ['ffn1_norm', 'mix_norm', 'm_conv_b', 'm_dt_bias', 'm_A_log', 'm_D', 'm_norm', 'ffn2_norm', 'ple_norm']


def _pick(n, cands):
    for c in cands:
        if n % c == 0:
            return c
    return n


def _cp(sem):
    return pltpu.CompilerParams(dimension_semantics=sem, vmem_limit_bytes=VMEM_LIMIT)


def _sigmoid(x):
    return 1.0 / (1.0 + jnp.exp(-x))


def _softplus(x):
    return jnp.maximum(x, 0.0) + jnp.log(1.0 + jnp.exp(-jnp.abs(x)))


def _exchange(name, x, gather, dep=None):
    slab = x.shape if gather else x.shape[1:]

    def body(x_ref, *rest):
        o_ref, send_sems, recv_sems, local_sem = rest[-4:]
        mx, my, mc = lax.axis_index("x"), lax.axis_index("y"), lax.axis_index("c")
        me = 4 * mx + 2 * my + mc

        def src_for(k):
            return x_ref if gather else x_ref.at[k]

        local = pltpu.make_async_copy(src_for(me), o_ref.at[me], local_sem)
        local.start()
        sends = []
        peers = []
        for r in range(1, N_DEV):
            px = (mx + ((r >> 2) & 1)) % 2
            py = (my + ((r >> 1) & 1)) % 2
            pc = (mc + (r & 1)) % 2
            peer = 4 * px + 2 * py + pc
            peers.append(peer)
            cp = pltpu.make_async_remote_copy(
                src_ref=src_for(peer), dst_ref=o_ref.at[me], send_sem=send_sems.at[r - 1], recv_sem=recv_sems.at[r - 1],
                device_id=(px, py, pc), device_id_type=MESH)
            cp.start()
            sends.append(cp)
        for r in range(1, N_DEV):
            peer = peers[r - 1]
            pltpu.make_async_remote_copy(
                src_ref=src_for(peer), dst_ref=o_ref.at[peer], send_sem=send_sems.at[r - 1], recv_sem=recv_sems.at[r - 1],
                device_id=(mx, my, mc), device_id_type=MESH).wait_recv()
        for cp in sends:
            cp.wait_send()
        local.wait()

    return pl.pallas_call(
        body, name=name,
        out_shape=jax.ShapeDtypeStruct((N_DEV,) + tuple(slab), x.dtype),
        in_specs=[pl.BlockSpec(memory_space=pltpu.HBM)] + ([] if dep is None else [pl.BlockSpec(memory_space=pl.ANY)]),
        out_specs=pl.BlockSpec(memory_space=pltpu.HBM),
        scratch_shapes=[pltpu.SemaphoreType.DMA((N_DEV - 1,)), pltpu.SemaphoreType.DMA((N_DEV - 1,)), pltpu.SemaphoreType.DMA],
    )(*([x] if dep is None else [x, dep]))


STAGES = [[['ffn1_wd'], ['ffn1_wg'], ['ffn1_wu']],
          [['w_in'], ['sc_w_out', 'w_o', 'ple_w_gate', 'm_w_out']],
          [['ffn2_wd'], ['ffn2_wg'], ['ffn2_wu'], ['ple_w_proj']]]
_HBM = pl.BlockSpec(memory_space=pltpu.HBM)
_SEM = pl.BlockSpec(memory_space=pltpu.SEMAPHORE)
_ANY = pl.BlockSpec(memory_space=pl.ANY)
_EFFECT = pltpu.SideEffectType.DATAFLOW_SIDE_EFFECTING


def _peer_list():
    mx, my, mc = lax.axis_index("x"), lax.axis_index("y"), lax.axis_index("c")
    out = []
    for r in range(1, N_DEV):
        px = (mx + ((r >> 2) & 1)) % 2
        py = (my + ((r >> 1) & 1)) % 2
        pc = (mc + (r & 1)) % 2
        out.append((px, py, pc, 4 * px + 2 * py + pc))
    return 4 * mx + 2 * my + mc, out


def _xchg_copy(src_refs, land_refs, send_sems, recv_sems, layer, i, r, peer, dst_slab):
    px, py, pc, pidx = peer
    n = len(src_refs)
    src = src_refs[i].at[layer] if layer is not None else src_refs[i].at[pidx]
    return pltpu.make_async_remote_copy(
        src_ref=src, dst_ref=land_refs[i].at[dst_slab], send_sem=send_sems.at[r * n + i], recv_sem=recv_sems.at[r * n + i],
        device_id=(px, py, pc), device_id_type=MESH)


def _xchg_begin(name, srcs, layer, dep):
    n = len(srcs)
    slabs = [tuple(s.shape[1:]) for s in srcs]
    ncp = n * (N_DEV - 1)

    def body(*refs):
        src_refs, land_refs = refs[:n], refs[n:2 * n]
        send_sems, recv_sems = refs[2 * n + 1], refs[2 * n + 2]
        token = refs[-1]
        me, peers = _peer_list()
        for r, peer in enumerate(peers):
            for i in range(n):
                _xchg_copy(src_refs, land_refs, send_sems, recv_sems, layer, i, r, peer, me).start()
        token[...] = jnp.zeros_like(token)

    lands = [pltpu.with_memory_space_constraint(lax.empty((N_DEV,) + sl, s.dtype), pltpu.HBM) for sl, s in zip(slabs, srcs)]
    out = pl.pallas_call(
        body, name=name,
        out_shape=(pltpu.SemaphoreType.DMA((ncp,)), pltpu.SemaphoreType.DMA((ncp,)),
                   *[pltpu.HBM((N_DEV,) + sl, s.dtype) for sl, s in zip(slabs, srcs)], jax.ShapeDtypeStruct((8, LANES), F32)),
        in_specs=[_HBM] * (2 * n) + [_ANY],
        out_specs=(_SEM, _SEM, *[_HBM] * n, pl.BlockSpec(memory_space=pltpu.VMEM)),
        input_output_aliases={n + i: 2 + i for i in range(n)},
        compiler_params=pltpu.CompilerParams(has_side_effects=_EFFECT),
    )(*[pltpu.with_memory_space_constraint(s, pltpu.HBM) for s in srcs], *lands, dep)
    return (out[0], out[1]), list(out[2:2 + n]), out[-1]


def _xchg_end(name, srcs, lands, sems, layer, after):
    n = len(srcs)

    def body(*refs):
        src_refs, land_refs = refs[:n], refs[n:2 * n]
        send_sems, recv_sems = refs[2 * n], refs[2 * n + 1]
        me, peers = _peer_list()
        for r, peer in enumerate(peers):
            for i in range(n):
                cp = _xchg_copy(src_refs, land_refs, send_sems, recv_sems, layer, i, r, peer, peer[3])
                cp.wait_send()
                cp.wait_recv()

    out = pl.pallas_call(
        body, name=name,
        out_shape=tuple(pltpu.HBM(l.shape, l.dtype) for l in lands),
        in_specs=[_HBM] * (2 * n) + [_SEM, _SEM, _ANY], out_specs=tuple([_HBM] * n),
        input_output_aliases={n + i: i for i in range(n)},
        compiler_params=pltpu.CompilerParams(has_side_effects=_EFFECT),
    )(*[pltpu.with_memory_space_constraint(s, pltpu.HBM) for s in srcs], *lands, sems[0], sems[1], after)
    return list(out)


MM_TILES = (1024, 1408, 512, 256, 128)
MM_OPERAND_BYTES = 24 * 1024 * 1024


def _mm(name, a, b, *, ta=False, tb=False, out_dtype=None, res=None, alpha=1.0, dep=None):
    out_dtype = out_dtype or BF
    M, K = (a.shape[1], a.shape[0]) if ta else a.shape
    N = b.shape[0] if tb else b.shape[1]
    assert (b.shape[1] if tb else b.shape[0]) == K, (name, a.shape, b.shape)
    tm = _pick(M, MM_TILES)
    tn = _pick(N, MM_TILES)
    per_k = 2 * (tm * a.dtype.itemsize + tn * b.dtype.itemsize)
    tk = [t for t in sorted({K, 4096, 2816, 2560, 2048, 1408, 1024, 512, 256, 128}, reverse=True)
          if K % t == 0 and (t * per_k <= MM_OPERAND_BYTES or t == 128)][0]
    nk = K // tk
    a_spec = pl.BlockSpec((tk, tm), lambda i, j, k: (k, i)) if ta else pl.BlockSpec((tm, tk), lambda i, j, k: (i, k))
    b_spec = pl.BlockSpec((tn, tk), lambda i, j, k: (j, k)) if tb else pl.BlockSpec((tk, tn), lambda i, j, k: (k, j))
    dn = (((0 if ta else 1,), (1 if tb else 0,)), ((), ()))
    has_res = res is not None
    n_dep = 0 if dep is None else 1

    def body(*refs):
        a_ref, b_ref = refs[:2]
        r_ref = refs[2] if has_res else None
        o_ref = refs[2 + has_res + n_dep]

        def finish(v):
            if alpha != 1.0:
                v = v * alpha
            if has_res:
                v = r_ref[...] + v
            o_ref[...] = v.astype(o_ref.dtype)

        part = lax.dot_general(a_ref[...].astype(BF), b_ref[...].astype(BF), dn, preferred_element_type=F32)
        if nk == 1:
            finish(part)
            return
        acc = refs[-1]
        k = pl.program_id(2)

        @pl.when(k == 0)
        def _():
            acc[...] = part

        @pl.when((k > 0) & (k < nk - 1))
        def _():
            acc[...] += part

        @pl.when(k == nk - 1)
        def _():
            finish(acc[...] + part)

    in_specs = [a_spec, b_spec]
    args = [a, b]
    if has_res:
        in_specs.append(pl.BlockSpec((tm, tn), lambda i, j, k: (i, j)))
        args.append(res)
    if dep is not None:
        in_specs.append(_ANY)
        args.append(dep)
    return pl.pallas_call(
        body, name=name, grid=(M // tm, N // tn, nk),
        in_specs=in_specs, out_specs=pl.BlockSpec((tm, tn), lambda i, j, k: (i, j)),
        out_shape=jax.ShapeDtypeStruct((M, N), out_dtype),
        scratch_shapes=[pltpu.VMEM((tm, tn), F32)] if nk > 1 else [],
        compiler_params=_cp(("parallel", "parallel", "arbitrary")),
    )(*args)


def _ew(name, fn, tiled, params, outs, accs=(), tile=256, dep=None):
    tiled = [t if isinstance(t, tuple) else (t, t.shape[1], 0) for t in tiled]
    params = [q if isinstance(q, tuple) else (q, None) for q in params]
    S = tiled[0][0].shape[0]
    T = _pick(S, (tile, 128, 64, 32, 16))
    n_in = len(tiled) + len(params)
    n_dep = 0 if dep is None else 1

    def body(*refs):
        fn(pl.program_id(0) == 0, *refs[:n_in], *refs[n_in + n_dep:])

    in_specs = [pl.BlockSpec((T, w), lambda i, cb=cb: (i, cb)) for _, w, cb in tiled]
    for q, row in params:
        if row is None:
            in_specs.append(pl.BlockSpec(q.shape, lambda i: (0, 0)))
        else:
            in_specs.append(pl.BlockSpec((None, 1, q.shape[2]), lambda i, row=row: (row, 0, 0)))
    args = [t[0] for t in tiled] + [q[0] for q in params]
    if dep is not None:
        in_specs.append(pl.BlockSpec(memory_space=pl.ANY))
        args.append(dep)
    out_specs = [pl.BlockSpec((T, w), lambda i: (i, 0)) for w, _ in outs]
    out_specs += [pl.BlockSpec(shp, lambda i: (0, 0)) for shp, _ in accs]
    out_shape = [jax.ShapeDtypeStruct((S, w), dt) for w, dt in outs]
    out_shape += [jax.ShapeDtypeStruct(shp, dt) for shp, dt in accs]
    res = pl.pallas_call(
        body, name=name, grid=(S // T,), in_specs=in_specs, out_specs=out_specs, out_shape=out_shape,
        compiler_params=_cp(("arbitrary",)),
    )(*args)
    return res


def _prow(g):
    return g if isinstance(g, tuple) else g.reshape(1, -1)


def _rms_fwd(name, h, g, dep=None):
    def fn(first, h_ref, g_ref, o_ref):
        x = h_ref[...]
        r = lax.rsqrt(jnp.mean(x * x, axis=-1, keepdims=True) + EPS)
        o_ref[...] = (x * r * g_ref[...]).astype(o_ref.dtype)

    return _ew(name, fn, [h], [_prow(g)], [(h.shape[1], BF)], dep=dep)[0]


def _rms_bwd(name, dxn, h, g, res):
    D = h.shape[1]

    def fn(first, d_ref, h_ref, r_ref, g_ref, o_ref, dg_ref):
        x = h_ref[...]
        d = d_ref[...].astype(F32)
        r = lax.rsqrt(jnp.mean(x * x, axis=-1, keepdims=True) + EPS)
        xhat = x * r
        dxhat = d * g_ref[...]
        dh = r * (dxhat - xhat * jnp.mean(dxhat * xhat, axis=-1, keepdims=True))
        o_ref[...] = r_ref[...] + dh

        @pl.when(first)
        def _():
            dg_ref[...] = jnp.zeros_like(dg_ref)

        dg_ref[...] += jnp.sum(d * xhat, axis=0, keepdims=True)

    return _ew(name, fn, [dxn, h, res], [_prow(g)], [(D, F32)], [((1, D), F32)])


FFN_TOKEN_TILE = 512


def _ffn_up(name, xn, wgT, wuT, dep=None):
    S, D = xn.shape
    FF = wgT.shape[0]
    tm = _pick(S, (FFN_TOKEN_TILE, 256, 128))
    tn = _pick(FF, MM_TILES)
    n_dep = 0 if dep is None else 1

    def body(x_ref, g_ref, u_ref, *rest):
        a_ref, b_ref, h_ref = rest[n_dep:]
        x = x_ref[...]
        a = _dot_nt(x, g_ref[...])
        b = _dot_nt(x, u_ref[...])
        a_ref[...] = a.astype(BF)
        b_ref[...] = b.astype(BF)
        h_ref[...] = (a * _sigmoid(a) * b).astype(BF)

    wspec = pl.BlockSpec((tn, D), lambda j, i: (j, 0))
    ospec = pl.BlockSpec((tm, tn), lambda j, i: (i, j))
    return pl.pallas_call(
        body, name=name, grid=(FF // tn, S // tm),
        in_specs=[pl.BlockSpec((tm, D), lambda j, i: (i, 0)), wspec, wspec] + ([] if dep is None else [_ANY]),
        out_specs=[ospec] * 3, out_shape=[jax.ShapeDtypeStruct((S, FF), BF)] * 3,
        compiler_params=_cp(("parallel", "arbitrary")),
    )(*([xn, wgT, wuT] + ([] if dep is None else [dep])))


def _ffn_dact(name, dh, wd, a, b, dep=None):
    S, D = dh.shape
    FF = wd.shape[0]
    tm = _pick(S, (FFN_TOKEN_TILE, 256, 128))
    tn = _pick(FF, MM_TILES)
    n_dep = 0 if dep is None else 1

    def body(d_ref, w_ref, a_ref, b_ref, *rest):
        da_ref, db_ref = rest[n_dep:]
        d = 0.5 * _dot_nt(d_ref[...].astype(BF), w_ref[...])
        av = a_ref[...].astype(F32)
        s = _sigmoid(av)
        da_ref[...] = (d * b_ref[...].astype(F32) * (s * (1.0 + av * (1.0 - s)))).astype(BF)
        db_ref[...] = (d * av * s).astype(BF)

    tspec = pl.BlockSpec((tm, tn), lambda j, i: (i, j))
    return pl.pallas_call(
        body, name=name, grid=(FF // tn, S // tm),
        in_specs=[pl.BlockSpec((tm, D), lambda j, i: (i, 0)), pl.BlockSpec((tn, D), lambda j, i: (j, 0)), tspec, tspec]
        + ([] if dep is None else [_ANY]),
        out_specs=[tspec] * 2, out_shape=[jax.ShapeDtypeStruct((S, FF), BF)] * 2,
        compiler_params=_cp(("parallel", "arbitrary")),
    )(*([dh, wd, a, b] + ([] if dep is None else [dep])))


def _merge_fwd(name, ga, gm, ya, ym):
    def fn(first, ga_ref, gm_ref, ya_ref, ym_ref, o_ref):
        o = _sigmoid(ga_ref[...].astype(F32)) * ya_ref[...].astype(F32) + _sigmoid(gm_ref[...].astype(F32)) * ym_ref[...].astype(F32)
        o_ref[...] = o.astype(o_ref.dtype)

    return _ew(name, fn, [ga, gm, ya, ym], [], [(ya.shape[1], BF)])[0]


def _merge_bwd(name, dmerged, ga, gm, ya, ym):
    W = ya.shape[1]

    def fn(first, d_ref, ga_ref, gm_ref, ya_ref, ym_ref, dga_ref, dgm_ref, dya_ref, dym_ref):
        d = d_ref[...].astype(F32)
        sa = _sigmoid(ga_ref[...].astype(F32))
        sm = _sigmoid(gm_ref[...].astype(F32))
        dga_ref[...] = (d * ya_ref[...].astype(F32) * sa * (1.0 - sa)).astype(BF)
        dgm_ref[...] = (d * ym_ref[...].astype(F32) * sm * (1.0 - sm)).astype(BF)
        dya_ref[...] = (d * sa).astype(BF)
        dym_ref[...] = (d * sm).astype(BF)

    return _ew(name, fn, [dmerged, ga, gm, ya, ym], [], [(W, BF)] * 4)


def _gnorm_fwd(name, y, z, w):
    W = y.shape[1]
    gw = W // SSM_GROUPS

    def fn(first, y_ref, z_ref, w_ref, o_ref):
        for g in range(SSM_GROUPS):
            sl = slice(g * gw, (g + 1) * gw)
            zz = z_ref[:, sl].astype(F32)
            t = y_ref[:, sl].astype(F32) * (zz * _sigmoid(zz))
            r = lax.rsqrt(jnp.mean(t * t, axis=-1, keepdims=True) + EPS)
            o_ref[:, sl] = (t * r * w_ref[:, sl]).astype(o_ref.dtype)

    return _ew(name, fn, [y, z], [_prow(w)], [(W, BF)])[0]


def _gnorm_bwd(name, dyn, y, z, w):
    W = y.shape[1]
    gw = W // SSM_GROUPS

    def fn(first, d_ref, y_ref, z_ref, w_ref, dy_ref, dz_ref, dw_ref):
        @pl.when(first)
        def _():
            dw_ref[...] = jnp.zeros_like(dw_ref)

        for g in range(SSM_GROUPS):
            sl = slice(g * gw, (g + 1) * gw)
            zz = z_ref[:, sl].astype(F32)
            yy = y_ref[:, sl].astype(F32)
            d = d_ref[:, sl].astype(F32)
            s = _sigmoid(zz)
            sz = zz * s
            t = yy * sz
            r = lax.rsqrt(jnp.mean(t * t, axis=-1, keepdims=True) + EPS)
            that = t * r
            dthat = d * w_ref[:, sl]
            dt = r * (dthat - that * jnp.mean(dthat * that, axis=-1, keepdims=True))
            dw_ref[:, sl] += jnp.sum(d * that, axis=0, keepdims=True)
            dy_ref[:, sl] = (dt * sz).astype(BF)
            dz_ref[:, sl] = (dt * yy * (s * (1.0 + zz * (1.0 - s)))).astype(BF)

    return _ew(name, fn, [dyn, y, z], [_prow(w)], [(W, BF), (W, BF)], [((1, W), F32)])


def _ple_fwd(name, h, gpre, pp):
    def fn(first, h_ref, g_ref, p_ref, o_ref):
        o_ref[...] = h_ref[...] + _sigmoid(g_ref[...].astype(F32)) * p_ref[...].astype(F32)

    return _ew(name, fn, [h, gpre, pp], [], [(h.shape[1], F32)])[0]


def _ple_bwd(name, dh, gpre, pp, dep=None):
    W = dh.shape[1]

    def fn(first, d_ref, g_ref, p_ref, dg_ref, dp_ref):
        d = d_ref[...]
        s = _sigmoid(g_ref[...].astype(F32))
        dg_ref[...] = (d * p_ref[...].astype(F32) * s * (1.0 - s)).astype(BF)
        dp_ref[...] = (d * s).astype(BF)

    return _ew(name, fn, [dh, gpre, pp], [], [(W, BF), (W, BF)], dep=dep)


def _loss_head(name, h, g, target):
    D = h.shape[1]

    def fn(first, h_ref, t_ref, g_ref, dh_ref, loss_ref, dg_ref):
        x = h_ref[...]
        r = lax.rsqrt(jnp.mean(x * x, axis=-1, keepdims=True) + EPS)
        xhat = x * r
        err = xhat * g_ref[...] - t_ref[...]
        part = 0.5 * jnp.sum(jnp.mean(err * err, axis=-1, keepdims=True), axis=0, keepdims=True)
        dy = err * (1.0 / D)
        dxhat = dy * g_ref[...]
        dh_ref[...] = r * (dxhat - xhat * jnp.mean(dxhat * xhat, axis=-1, keepdims=True))

        @pl.when(first)
        def _():
            loss_ref[...] = jnp.zeros_like(loss_ref)
            dg_ref[...] = jnp.zeros_like(dg_ref)

        loss_ref[...] += jnp.broadcast_to(part, loss_ref.shape)
        dg_ref[...] += jnp.sum(dy * xhat, axis=0, keepdims=True)

    return _ew(name, fn, [h, target], [_prow(g)], [(D, F32)], [((1, LANES), F32), ((1, D), F32)])


def _conv_specs(S, C, offs, l):
    T = _pick(S, (512, 256, 128, 64, 32, 16))
    Ct = [c for c in (512, 256, 128) if C % c == 0 and all(o % c == 0 for o in offs)][0]
    per = T // HALO
    last = S // HALO - 1

    def cur(off=0):
        return pl.BlockSpec((T, Ct), lambda j, i: (i, off // Ct + j))

    def prev(off=0):
        return pl.BlockSpec((HALO, Ct), lambda j, i: (jnp.maximum(i * per - 1, 0), off // Ct + j))

    def nxt(off=0):
        return pl.BlockSpec((HALO, Ct), lambda j, i: (jnp.minimum((i + 1) * per, last), off // Ct + j))

    wspec = pl.BlockSpec((None, 8, Ct), lambda j, i: (l, 0, j))
    return T, Ct, cur, prev, nxt, wspec


def _pad_taps(w):
    return jnp.concatenate([w.astype(F32), jnp.zeros((w.shape[0], 8 - w.shape[1], w.shape[2]), F32)], axis=1)


def _causal(cat, w_ref, K, T, lead):
    views = [cat[lead - (K - 1) + k:lead - (K - 1) + k + T] for k in range(K)]
    out = None
    for k in range(K):
        term = w_ref[k:k + 1, :] * views[k]
        out = term if out is None else out + term
    return out, views


def _anticausal(cat, w_ref, K, T):
    out = None
    for k in range(K):
        o = K - 1 - k
        term = w_ref[k:k + 1, :] * cat[o:o + T]
        out = term if out is None else out + term
    return out


def _scconv_fwd(name, proj, ob, oc, ox, taps, K, l):
    S = proj.shape[0]
    C = taps.shape[2]
    T, Ct, cur, prev, nxt, wspec = _conv_specs(S, C, (ob, oc, ox), l)

    def body(b_ref, c_ref, x_ref, cp_ref, xp_ref, w_ref, o_ref):
        i = pl.program_id(1)
        q = c_ref[...].astype(F32) * x_ref[...].astype(F32)
        qp = jnp.where(i == 0, 0.0, cp_ref[...].astype(F32) * xp_ref[...].astype(F32))
        cat = jnp.concatenate([qp, q], axis=0)
        o_ref[...] = (b_ref[...].astype(F32) * _causal(cat, w_ref, K, T, HALO)[0]).astype(o_ref.dtype)

    return pl.pallas_call(
        body, name=name, grid=(C // Ct, S // T),
        in_specs=[cur(ob), cur(oc), cur(ox), prev(oc), prev(ox), wspec], out_specs=cur(),
        out_shape=jax.ShapeDtypeStruct((S, C), BF), compiler_params=_cp(("parallel", "arbitrary")),
    )(proj, proj, proj, proj, proj, taps)


def _scconv_bwd(name, dv, proj, ob, oc, ox, taps, K, l):
    S = proj.shape[0]
    C = taps.shape[2]
    T, Ct, cur, prev, nxt, wspec = _conv_specs(S, C, (ob, oc, ox), l)
    n_t = S // T

    def body(d_ref, b_ref, c_ref, x_ref, dn_ref, bn_ref, cp_ref, xp_ref, w_ref, db_ref, dc_ref, dx_ref, dw_ref):
        i = pl.program_id(1)
        c = c_ref[...].astype(F32)
        x = x_ref[...].astype(F32)
        d = d_ref[...].astype(F32)
        q = c * x
        qp = jnp.where(i == 0, 0.0, cp_ref[...].astype(F32) * xp_ref[...].astype(F32))
        catq = jnp.concatenate([qp, q], axis=0)
        cv, q_views = _causal(catq, w_ref, K, T, HALO)
        db_ref[...] = (d * cv).astype(BF)
        dcv = d * b_ref[...].astype(F32)
        dcvn = jnp.where(i == n_t - 1, 0.0, dn_ref[...].astype(F32) * bn_ref[...].astype(F32))
        catd = jnp.concatenate([dcv, dcvn], axis=0)
        dq = _anticausal(catd, w_ref, K, T)
        dc_ref[...] = (dq * x).astype(BF)
        dx_ref[...] = (dq * c).astype(BF)

        @pl.when(i == 0)
        def _():
            dw_ref[...] = jnp.zeros_like(dw_ref)

        for k in range(K):
            dw_ref[k:k + 1, :] += jnp.sum(dcv * q_views[k], axis=0, keepdims=True)

    return pl.pallas_call(
        body, name=name, grid=(C // Ct, n_t),
        in_specs=[cur(), cur(ob), cur(oc), cur(ox), nxt(), nxt(ob), prev(oc), prev(ox), wspec],
        out_specs=[cur(), cur(), cur(), pl.BlockSpec((8, Ct), lambda j, i: (0, j))],
        out_shape=[jax.ShapeDtypeStruct((S, C), BF)] * 3 + [jax.ShapeDtypeStruct((8, C), F32)],
        compiler_params=_cp(("parallel", "arbitrary")),
    )(dv, proj, proj, proj, dv, proj, proj, proj, taps)


def _mconv_fwd(name, proj, ox, taps, K, bias, l):
    S = proj.shape[0]
    C = taps.shape[2]
    T, Ct, cur, prev, nxt, wspec = _conv_specs(S, C, (ox,), l)
    bspec = pl.BlockSpec((None, 1, Ct), lambda j, i: (l, 0, j))

    def body(x_ref, xp_ref, w_ref, b_ref, o_ref):
        i = pl.program_id(1)
        xp = jnp.where(i == 0, 0.0, xp_ref[...].astype(F32))
        cat = jnp.concatenate([xp, x_ref[...].astype(F32)], axis=0)
        pre = _causal(cat, w_ref, K, T, HALO)[0] + b_ref[...]
        o_ref[...] = (pre * _sigmoid(pre)).astype(o_ref.dtype)

    return pl.pallas_call(
        body, name=name, grid=(C // Ct, S // T),
        in_specs=[cur(ox), prev(ox), wspec, bspec], out_specs=cur(),
        out_shape=jax.ShapeDtypeStruct((S, C), BF), compiler_params=_cp(("parallel", "arbitrary")),
    )(proj, proj, taps, bias)


def _mconv_bwd(name, dout, proj, ox, taps, K, bias, l):
    S = proj.shape[0]
    C = taps.shape[2]
    T, Ct, cur, prev, nxt, wspec = _conv_specs(S, C, (ox,), l)
    n_t = S // T
    bspec = pl.BlockSpec((None, 1, Ct), lambda j, i: (l, 0, j))

    def body(d_ref, dn_ref, x_ref, xp_ref, xn_ref, w_ref, b_ref, dx_ref, dw_ref, db_ref):
        i = pl.program_id(1)
        xp = jnp.where(i == 0, 0.0, xp_ref[...].astype(F32))
        cat3 = jnp.concatenate([xp, x_ref[...].astype(F32), xn_ref[...].astype(F32)], axis=0)
        pre, x_views = _causal(cat3, w_ref, K, T + HALO, HALO)
        pre = pre + b_ref[...]
        dn = jnp.where(i == n_t - 1, 0.0, dn_ref[...].astype(F32))
        dext = jnp.concatenate([d_ref[...].astype(F32), dn], axis=0)
        s = _sigmoid(pre)
        dpre = dext * (s * (1.0 + pre * (1.0 - s)))
        dx_ref[...] = _anticausal(dpre, w_ref, K, T).astype(BF)
        dcur = dpre[:T]

        @pl.when(i == 0)
        def _():
            dw_ref[...] = jnp.zeros_like(dw_ref)
            db_ref[...] = jnp.zeros_like(db_ref)

        db_ref[...] += jnp.sum(dcur, axis=0, keepdims=True)
        for k in range(K):
            dw_ref[k:k + 1, :] += jnp.sum(dcur * x_views[k][:T], axis=0, keepdims=True)

    return pl.pallas_call(
        body, name=name, grid=(C // Ct, n_t),
        in_specs=[cur(), nxt(), cur(ox), prev(ox), nxt(ox), wspec, bspec],
        out_specs=[cur(), pl.BlockSpec((8, Ct), lambda j, i: (0, j)), pl.BlockSpec((1, Ct), lambda j, i: (0, j))],
        out_shape=[jax.ShapeDtypeStruct((S, C), BF), jax.ShapeDtypeStruct((8, C), F32), jax.ShapeDtypeStruct((1, C), F32)],
        compiler_params=_cp(("parallel", "arbitrary")),
    )(dout, dout, proj, proj, proj, taps, bias)


def _tri_matmul(tri_bf, v):
    hi = v.astype(BF)
    r1 = v - hi.astype(F32)
    mid = r1.astype(BF)
    lo = (r1 - mid.astype(F32)).astype(BF)
    dot = functools.partial(jnp.dot, preferred_element_type=F32)
    return dot(tri_bf, hi) + dot(tri_bf, mid) + dot(tri_bf, lo)


def _dot_nt(a, b):
    return lax.dot_general(a, b, (((1,), (1,)), ((), ())), preferred_element_type=F32)


def _dot_tn(a, b):
    return lax.dot_general(a, b, (((0,), (0,)), ((), ())), preferred_element_type=F32)


def _dot_nn(a, b):
    return jnp.dot(a, b, preferred_element_type=F32)


def _ssd_chunk_scalars(dtr_ref, par_ref, L):
    row_i = lax.broadcasted_iota(jnp.int32, (L, L), 0)
    col_i = lax.broadcasted_iota(jnp.int32, (L, L), 1)
    tri = row_i >= col_i
    pre = dtr_ref[...] + par_ref[0:1, :]
    dt_all = _softplus(pre)
    A_row = -jnp.exp(par_ref[1:2, :])
    a_all = dt_all * A_row
    acum_all = _tri_matmul(tri.astype(BF), a_all)
    return tri, pre, dt_all, A_row, a_all, acum_all, acum_all.T


def _ssd_dims(xbc, heads):
    S, conv_dim = xbc.shape
    inner = heads * SSM_HEADDIM
    N = (conv_dim - inner) // (2 * SSM_GROUPS)
    gw = inner // SSM_GROUPS
    PP = gw // LANES
    L = min(SSM_CHUNK, S)
    assert N == LANES and gw % LANES == 0 and inner % (SSM_GROUPS * N) == 0 and S % L == 0
    return S, inner, N, gw, PP, L, S // L


def _ssd_params(dt_bias, A_log, Dp):
    depth, H = dt_bias.shape
    rows = jnp.stack([dt_bias, A_log, Dp], axis=1).astype(F32)
    rows = jnp.concatenate([rows, jnp.zeros((depth, 3, LANES - H), F32)], axis=2)
    return jnp.concatenate([rows, jnp.zeros((depth, 5, LANES), F32)], axis=1)


def _ssd_fwd(name, xbc, dt_raw, par, l, heads):
    S, inner, N, gw, PP, L, nc = _ssd_dims(xbc, heads)
    G = SSM_GROUPS

    def body(x_ref, b_ref, c_ref, dtr_ref, par_ref, y_ref, st_out_ref, st_ref):
        @pl.when(pl.program_id(0) == 0)
        def _():
            st_ref[...] = jnp.zeros_like(st_ref)

        tri, pre, dt_all, A_row, a_all, acum_all, acumT = _ssd_chunk_scalars(dtr_ref, par_ref, L)
        lane = lax.broadcasted_iota(jnp.int32, (L, LANES), 1)
        lane1 = lax.broadcasted_iota(jnp.int32, (1, LANES), 1)
        lo = lane < SSM_HEADDIM
        lo1 = lane1 < SSM_HEADDIM
        for g in range(G):
            Bb = b_ref[:, g * N:(g + 1) * N]
            Cb = c_ref[:, g * N:(g + 1) * N]
            BbT = Bb.astype(F32).T.astype(BF)
            Gm = _dot_nt(Cb, Bb)
            for j in range(PP):
                pj = g * PP + j
                h0, h1 = 2 * pj, 2 * pj + 1
                cols = slice(pj * LANES, (pj + 1) * LANES)
                x = x_ref[:, cols].astype(F32)
                dt_l = jnp.where(lo, dt_all[:, h0:h0 + 1], dt_all[:, h1:h1 + 1])
                ac0 = acum_all[:, h0:h0 + 1]
                ac1 = acum_all[:, h1:h1 + 1]
                ac_l = jnp.where(lo, ac0, ac1)
                E0 = jnp.exp(jnp.where(tri, ac0 - acumT[h0:h0 + 1, :], -1e30))
                E1 = jnp.exp(jnp.where(tri, ac1 - acumT[h1:h1 + 1, :], -1e30))
                xd = x * dt_l
                xdb = xd.astype(BF)
                yd = jnp.where(lo, _dot_nn((Gm * E0).astype(BF), xdb), _dot_nn((Gm * E1).astype(BF), xdb))
                prevT = st_ref[pj]
                st_out_ref[0, pj] = prevT
                P = _dot_nn(Cb, prevT.astype(BF))
                D_l = jnp.where(lo1, par_ref[2:3, h0:h0 + 1], par_ref[2:3, h1:h1 + 1])
                y_ref[:, cols] = (yd + P * jnp.exp(ac_l) + D_l * x).astype(y_ref.dtype)
                al0 = ac0[L - 1:L, :]
                al1 = ac1[L - 1:L, :]
                Wm = xd * jnp.exp(jnp.where(lo, al0, al1) - ac_l)
                eal = jnp.where(lo1, jnp.exp(al0), jnp.exp(al1))
                st_ref[pj] = eal * prevT + _dot_nn(BbT, Wm.astype(BF))

    gn = G * N
    return pl.pallas_call(
        body, name=name, grid=(nc,),
        in_specs=[pl.BlockSpec((L, inner), lambda c: (c, 0)), pl.BlockSpec((L, gn), lambda c: (c, inner // gn)),
                  pl.BlockSpec((L, gn), lambda c: (c, inner // gn + 1)),
                  pl.BlockSpec((L, LANES), lambda c: (c, 0)), pl.BlockSpec((None, 8, LANES), lambda c: (l, 0, 0))],
        out_specs=[pl.BlockSpec((L, inner), lambda c: (c, 0)), pl.BlockSpec((1, G * PP, N, LANES), lambda c: (c, 0, 0, 0))],
        out_shape=[jax.ShapeDtypeStruct((S, inner), BF), jax.ShapeDtypeStruct((nc, G * PP, N, LANES), F32)],
        scratch_shapes=[pltpu.VMEM((G * PP, N, LANES), F32)],
        compiler_params=_cp(("arbitrary",)),
    )(xbc, xbc, xbc, dt_raw, par)


def _ssd_bwd(name, dy, xbc, dt_raw, states, par, l, heads):
    S, inner, N, gw, PP, L, nc = _ssd_dims(xbc, heads)
    G = SSM_GROUPS

    def body(dy_ref, x_ref, b_ref, c_ref, dtr_ref, par_ref, st_in_ref, d_ref, ddt_ref, dpar_ref, dst_ref):
        @pl.when(pl.program_id(0) == 0)
        def _():
            dst_ref[...] = jnp.zeros_like(dst_ref)
            dpar_ref[...] = jnp.zeros_like(dpar_ref)

        tri, pre, dt_all, A_row, a_all, acum_all, acumT = _ssd_chunk_scalars(dtr_ref, par_ref, L)
        lane = lax.broadcasted_iota(jnp.int32, (L, LANES), 1)
        lane1 = lax.broadcasted_iota(jnp.int32, (1, LANES), 1)
        rowl = lax.broadcasted_iota(jnp.int32, (L, LANES), 0)
        lo = lane < SSM_HEADDIM
        lo1 = lane1 < SSM_HEADDIM
        triT = lax.broadcasted_iota(jnp.int32, (L, L), 0) <= lax.broadcasted_iota(jnp.int32, (L, L), 1)
        sel_r = lax.broadcasted_iota(jnp.int32, (3 * LANES, LANES), 0)
        sel_c = lax.broadcasted_iota(jnp.int32, (3 * LANES, LANES), 1)
        dac_all = jnp.zeros((L, LANES), F32)
        xds_all = jnp.zeros((L, LANES), F32)
        dD_row = jnp.zeros((1, LANES), F32)

        def half_sums(v):
            return (jnp.sum(jnp.where(lo1, v, 0.0), axis=1, keepdims=True), jnp.sum(jnp.where(lo1, 0.0, v), axis=1, keepdims=True))

        def dot2(v, sel):
            hi = v.astype(BF)
            return _dot_nn(hi, sel) + _dot_nn((v - hi.astype(F32)).astype(BF), sel)

        for pj in range(G * PP):
            g, j = divmod(pj, PP)
            if j == 0:
                Bb = b_ref[:, g * N:(g + 1) * N]
                Cb = c_ref[:, g * N:(g + 1) * N]
                CbT = Cb.astype(F32).T.astype(BF)
                Gm = _dot_nt(Cb, Bb)
                GmT = _dot_nt(Bb, Cb)
                dG = jnp.zeros((L, L), F32)
                dGT = jnp.zeros((L, L), F32)
                dBacc = jnp.zeros((L, N), F32)
                dCacc = jnp.zeros((L, N), F32)
            h0, h1 = 2 * pj, 2 * pj + 1
            to_h0 = (sel_r < LANES) | ((sel_r >= 2 * LANES) & (sel_r < 2 * LANES + SSM_HEADDIM))
            sel3 = jnp.where(sel_c == jnp.where(to_h0, h0, h1), 1.0, 0.0).astype(BF)
            sel1 = sel3[2 * LANES:]
            sl = slice(pj * LANES, (pj + 1) * LANES)
            x = x_ref[:, sl].astype(F32)
            dyv = dy_ref[:, sl].astype(F32)
            dt_l = jnp.where(lo, dt_all[:, h0:h0 + 1], dt_all[:, h1:h1 + 1])
            ac0 = acum_all[:, h0:h0 + 1]
            ac1 = acum_all[:, h1:h1 + 1]
            r0 = acumT[h0:h0 + 1, :]
            r1 = acumT[h1:h1 + 1, :]
            ac_l = jnp.where(lo, ac0, ac1)
            E0 = jnp.exp(jnp.where(tri, ac0 - r0, -1e30))
            E1 = jnp.exp(jnp.where(tri, ac1 - r1, -1e30))
            E0T = jnp.exp(jnp.where(triT, r0 - ac0, -1e30))
            E1T = jnp.exp(jnp.where(triT, r1 - ac1, -1e30))
            xd = x * dt_l
            xdb = xd.astype(BF)
            M0 = Gm * E0
            M1 = Gm * E1
            ea_l = jnp.exp(ac_l)
            al0 = ac0[L - 1:L, :]
            al1 = ac1[L - 1:L, :]
            dte_l = jnp.exp(jnp.where(lo, al0, al1) - ac_l)
            Wm = xd * dte_l
            prevT = st_in_ref[0, pj]
            prevTb = prevT.astype(BF)
            P = _dot_nn(Cb, prevTb)
            D_l = jnp.where(lo1, par_ref[2:3, h0:h0 + 1], par_ref[2:3, h1:h1 + 1])
            dx = D_l * dyv
            dD0, dD1 = half_sums(jnp.sum(dyv * x, axis=0, keepdims=True))
            dyb = dyv.astype(BF)
            dy0b = jnp.where(lo, dyv, 0.0).astype(BF)
            dy1b = jnp.where(lo, 0.0, dyv).astype(BF)
            dM0 = _dot_nt(dy0b, xdb)
            dM1 = _dot_nt(dy1b, xdb)
            dM0T = _dot_nt(xdb, dy0b)
            dM1T = _dot_nt(xdb, dy1b)
            M0T = GmT * E0T
            M1T = GmT * E1T
            dxd = jnp.where(lo, _dot_nn(M0T.astype(BF), dyb), _dot_nn(M1T.astype(BF), dyb))
            dG = dG + dM0 * E0 + dM1 * E1
            dGT = dGT + dM0T * E0T + dM1T * E1T
            z0 = dM0 * M0 - dM0T * M0T
            z1 = dM1 * M1 - dM1T * M1T
            dP = dyv * ea_l
            dPb = dP.astype(BF)
            dCacc = dCacc + _dot_nt(dPb, prevTb)
            dprevT = _dot_nn(CbT, dPb)
            dnewT = dst_ref[pj]
            dnewTb = dnewT.astype(BF)
            e0 = jnp.exp(al0)
            e1 = jnp.exp(al1)
            dprevT = dprevT + jnp.where(lo1, e0, e1) * dnewT
            u0, u1 = half_sums(jnp.sum(dnewT * prevT, axis=0, keepdims=True))
            dW = _dot_nn(Bb, dnewTb)
            dBacc = dBacc + _dot_nt(Wm.astype(BF), dnewTb)
            dxd = dxd + dW * dte_l
            tt = dW * Wm
            t0, t1 = half_sums(jnp.sum(tt, axis=0, keepdims=True))
            dal0 = u0 * e0 + t0
            dal1 = u1 * e1 + t1
            dac_all = dac_all + dot2(jnp.concatenate([z0, z1, dP * P - tt], axis=1), sel3)
            dac_all = dac_all + jnp.where(rowl == L - 1, jnp.where(lane == h0, dal0, 0.0) + jnp.where(lane == h1, dal1, 0.0), 0.0)
            dx = dx + dxd * dt_l
            xds_all = xds_all + dot2(dxd * x, sel1)
            dst_ref[pj] = dprevT
            d_ref[:, sl] = dx.astype(d_ref.dtype)
            dD_row = dD_row + jnp.where(lane1 == h0, dD0, 0.0) + jnp.where(lane1 == h1, dD1, 0.0)
            if j == PP - 1:
                d_ref[:, inner + g * N:inner + (g + 1) * N] = (dBacc + _dot_nn(dGT.astype(BF), Cb)).astype(d_ref.dtype)
                d_ref[:, inner + (G + g) * N:inner + (G + g + 1) * N] = (dCacc + _dot_nn(dG.astype(BF), Bb)).astype(d_ref.dtype)

        row_i = lax.broadcasted_iota(jnp.int32, (L, L), 0)
        col_i = lax.broadcasted_iota(jnp.int32, (L, L), 1)
        da_all = _tri_matmul((row_i <= col_i).astype(BF), dac_all)
        real = lane < heads
        ddt_all = da_all * A_row + xds_all
        draw = jnp.where(real, ddt_all * _sigmoid(pre), 0.0)
        ddt_ref[...] = draw
        dpar_ref[0:1, :] += jnp.sum(draw, axis=0, keepdims=True)
        dpar_ref[1:2, :] += jnp.sum(jnp.where(real, da_all * a_all, 0.0), axis=0, keepdims=True)
        dpar_ref[2:3, :] += dD_row

    gn = G * N
    conv_dim = xbc.shape[1]
    rev = lambda c: nc - 1 - c
    return pl.pallas_call(
        body, name=name, grid=(nc,),
        in_specs=[pl.BlockSpec((L, inner), lambda c: (rev(c), 0)), pl.BlockSpec((L, inner), lambda c: (rev(c), 0)),
                  pl.BlockSpec((L, gn), lambda c: (rev(c), inner // gn)), pl.BlockSpec((L, gn), lambda c: (rev(c), inner // gn + 1)),
                  pl.BlockSpec((L, LANES), lambda c: (rev(c), 0)), pl.BlockSpec((None, 8, LANES), lambda c: (l, 0, 0)),
                  pl.BlockSpec((1, G * PP, N, LANES), lambda c: (rev(c), 0, 0, 0))],
        out_specs=[pl.BlockSpec((L, conv_dim), lambda c: (rev(c), 0)), pl.BlockSpec((L, LANES), lambda c: (rev(c), 0)),
                   pl.BlockSpec((8, LANES), lambda c: (0, 0))],
        out_shape=[jax.ShapeDtypeStruct((S, conv_dim), BF), jax.ShapeDtypeStruct((S, LANES), F32), jax.ShapeDtypeStruct((8, LANES), F32)],
        scratch_shapes=[pltpu.VMEM((G * PP, N, LANES), F32)],
        compiler_params=_cp(("arbitrary",)),
    )(dy, xbc, xbc, xbc, dt_raw, par, states)


def _adamw(g, w, m, v):
    m2 = ADAM_B1 * m + (1.0 - ADAM_B1) * g
    v2 = ADAM_B2 * v + (1.0 - ADAM_B2) * (g * g)
    m_hat = m2 / (1.0 - ADAM_B1 ** ADAM_STEP)
    v_hat = v2 / (1.0 - ADAM_B2 ** ADAM_STEP)
    delta = -ADAM_LR * (m_hat / (jnp.sqrt(v_hat) + ADAM_EPS) + ADAM_WD * w)
    return delta, m2, v2


def _flat_tile(R):
    return _pick(R, (FLAT_ROW_TILE, 1024, 512, 256, 128, 64, 32, 16, 8))


def _sum_adam(name, lands, off, w, m, v):
    depth, r, c = w.shape
    cap = max(16, (4 * 1024 * 1024) // (N_DEV * c * 2))
    row_tiles = [t for t in (512, 256, 128, 64, 32, 16) if r % t == 0 and off % t == 0 and t <= cap]
    if row_tiles:
        tr, tc = row_tiles[0], c
        ob = off // tr
        n_t = r // tr
        spec = pl.BlockSpec((None, tr, c), lambda l, t: (l, t, 0))
        land_specs = [pl.BlockSpec((N_DEV, tr, c), lambda l, t, i=i: (0, jnp.where(l == i, ob + t, ob), 0)) for i in range(depth)]
    else:
        assert off == 0 and lands[0].shape[1] == r and c % LANES == 0
        tc = LANES
        n_t = c // tc
        spec = pl.BlockSpec((None, r, tc), lambda l, t: (l, 0, t))
        land_specs = [pl.BlockSpec((N_DEV, r, tc), lambda l, t, i=i: (0, 0, jnp.where(l == i, t, 0))) for i in range(depth)]

    def body(*refs):
        land_refs = refs[:depth]
        w_ref, m_ref, v_ref, g_ref, d_ref, m2_ref, v2_ref = refs[depth:]
        l = pl.program_id(0)
        for i in range(depth):
            @pl.when(l == i)
            def _(i=i):
                g = land_refs[i][0].astype(F32)
                for k in range(1, N_DEV):
                    g = g + land_refs[i][k].astype(F32)
                g_ref[...] = g
                d_ref[...], m2_ref[...], v2_ref[...] = _adamw(g, w_ref[...], m_ref[...], v_ref[...])

    return pl.pallas_call(
        body, name=name, grid=(depth, n_t),
        in_specs=land_specs + [spec, spec, spec],
        out_specs=[spec] * 4, out_shape=[jax.ShapeDtypeStruct((depth, r, c), F32)] * 4,
        compiler_params=_cp(("arbitrary", "arbitrary")),
    )(*lands, w, m, v)


def _sum8(name, parts):
    R = parts.shape[1]
    TR = _flat_tile(R)

    def body(p_ref, g_ref):
        g = p_ref[0]
        for k in range(1, N_DEV):
            g = g + p_ref[k]
        g_ref[...] = g

    return pl.pallas_call(
        body, name=name, grid=(R // TR,),
        in_specs=[pl.BlockSpec((N_DEV, TR, LANES), lambda i: (0, i, 0))],
        out_specs=pl.BlockSpec((TR, LANES), lambda i: (i, 0)), out_shape=jax.ShapeDtypeStruct((R, LANES), F32),
        compiler_params=_cp(("parallel",)),
    )(parts)


def _adam_flat(name, g, w, m, v):
    R = w.shape[0]
    TR = _flat_tile(R)

    def body(g_ref, w_ref, m_ref, v_ref, d_ref, m2_ref, v2_ref):
        d_ref[...], m2_ref[...], v2_ref[...] = _adamw(g_ref[...], w_ref[...], m_ref[...], v_ref[...])

    spec = pl.BlockSpec((TR, LANES), lambda i: (i, 0))
    return pl.pallas_call(
        body, name=name, grid=(R // TR,), in_specs=[spec] * 4, out_specs=[spec] * 3,
        out_shape=[jax.ShapeDtypeStruct((R, LANES), F32)] * 3, compiler_params=_cp(("parallel",)),
    )(g, w, m, v)


PART_ROWS = 16


def _nrows(shape):
    n = 1
    for s in shape:
        n *= s
    r = -(-n // LANES)
    return -(-r // PART_ROWS) * PART_ROWS


def _as_rows(a):
    n = a.size
    r = _nrows(a.shape)
    f = a.reshape(-1)
    if r * LANES != n:
        f = jnp.concatenate([f, jnp.zeros((r * LANES - n,), a.dtype)])
    return f.reshape(r, LANES)


def _pack(arrs, mult=PART_ROWS):
    cat = jnp.concatenate([_as_rows(a) for a in arrs], axis=0)
    pad = (-cat.shape[0]) % mult
    if pad:
        cat = jnp.concatenate([cat, jnp.zeros((pad, LANES), cat.dtype)], axis=0)
    return cat


def _unpack(flat, shapes):
    lead = flat.shape[:-2]
    out = []
    o = 0
    for shp in shapes:
        n = 1
        for s in shp:
            n *= s
        r = _nrows(shp)
        blk = flat[..., o:o + r, :].reshape(lead + (r * LANES,))
        out.append(blk[..., :n].reshape(lead + tuple(shp)))
        o += r
    return out


def _full_from_shards(st):
    return st.reshape(st.shape[0] * st.shape[1], st.shape[2])


def _shards_from_full(full):
    return full.reshape(N_DEV, full.shape[0] // N_DEV, full.shape[1])


def _ffn_fwd(tag, h, g, wgT, wuT, wd, dep=None):
    xn = _rms_fwd(tag + "_rms", h, g, dep=dep)
    a, b, hmid = _ffn_up(tag + "_up", xn, wgT, wuT)
    hout = _mm(tag + "_down", hmid, wd, out_dtype=F32, res=h, alpha=0.5)
    return hout, (xn, a, b, hmid)


def _ffn_bwd(tag, dh_out, h, g, wgT, wuT, wd, saved, dep=None):
    xn, a, b, hmid = saved
    da, db = _ffn_dact(tag + "_d_act", dh_out, wd, a, b, dep=dep)
    d_wd = _mm(tag + "_d_wd", hmid, dh_out, ta=True, alpha=0.5)
    d_wgT = _mm(tag + "_d_wg", da, xn, ta=True)
    d_wuT = _mm(tag + "_d_wu", db, xn, ta=True)
    dxn = _mm(tag + "_d_xn_g", da, wgT, out_dtype=F32)
    dxn = _mm(tag + "_d_xn_u", db, wuT, out_dtype=F32, res=dxn)
    dh, dg = _rms_bwd(tag + "_d_rms", dxn, h, g, dh_out)
    return dh, dg, d_wgT, d_wuT, d_wd


SEG_NAMES = ['scb', 'scc', 'scx', 'z', 'xbc', 'dt', 'ga', 'gm']
PERM = ['z', 'scb', 'scc', 'scx', 'ga', 'gm', 'xbc']


def _seg_layout(dims):
    D, inner, conv_dim, H = dims[:4]
    widths = dict(zip(SEG_NAMES, [D, D, D, inner, conv_dim, H, D, D]))
    offs, o = {}, 0
    for n in SEG_NAMES:
        offs[n] = (o, widths[n])
        o += widths[n]
    poffs, o = {}, 0
    for n in PERM:
        poffs[n] = (o, widths[n])
        o += widths[n]
    return offs, poffs


def _perm_w_in(w_inT, dims):
    offs, _ = _seg_layout(dims)
    wp = jnp.concatenate([w_inT[offs[n][0]:offs[n][0] + offs[n][1]] for n in PERM], axis=0)
    o, w = offs['dt']
    wdt = jnp.concatenate([w_inT[o:o + w], jnp.zeros((LANES - w, w_inT.shape[1]), w_inT.dtype)], axis=0)
    return wp, wdt


def _unperm_d_w_in(d_wp, d_wdt, dims):
    offs, poffs = _seg_layout(dims)
    H = dims[3]
    return jnp.concatenate([d_wdt[:H] if n == 'dt' else d_wp[poffs[n][0]:poffs[n][0] + poffs[n][1]] for n in SEG_NAMES], axis=0)


def _mixer_fwd(h, W, dims, dep=None):
    H, Ksc, Km = dims[3:]
    l = W['l']
    _, poffs = _seg_layout(dims)

    def seg(n):
        o, w = poffs[n]
        assert o % w == 0
        return (proj, w, o // w)

    u = _rms_fwd("mix_rms", h, W['mix_norm'], dep=dep)
    proj = _mm("inproj", u, W['w_in_p'], tb=True)
    dt_raw = _mm("inproj_dt", u, W['w_dt'], tb=True, out_dtype=F32)
    v = _scconv_fwd("scconv_f", proj, poffs['scb'][0], poffs['scc'][0], poffs['scx'][0], W['sc_taps'], Ksc, l)
    ya = _mm("sc_out", v, W['sc_w_out'])
    xbc = _mconv_fwd("mconv_f", proj, poffs['xbc'][0], W['m_taps'], Km, W['m_conv_b'], l)
    y, states = _ssd_fwd("ssd_f", xbc, dt_raw, W['ssd_par'], l, H)
    yn = _gnorm_fwd("gnorm_f", y, seg('z'), W['m_norm'])
    ym = _mm("m_out", yn, W['m_w_out'])
    merged = _merge_fwd("merge_f", seg('ga'), seg('gm'), ya, ym)
    hout = _mm("w_o", merged, W['w_o'], out_dtype=F32, res=h)
    return hout, (u, proj, dt_raw, v, ya, xbc, y, states, yn, ym, merged)


def _mixer_bwd(dh_out, h, W, dims, saved, dep=None):
    u, proj, dt_raw, v, ya, xbc, y, states, yn, ym, merged = saved
    H, Ksc, Km = dims[3:]
    l = W['l']
    _, poffs = _seg_layout(dims)

    def seg(n):
        o, w = poffs[n]
        return (proj, w, o // w)

    g = {}
    dmerged = _mm("d_merged", dh_out, W['w_o'], tb=True, dep=dep)
    g['w_o'] = _mm("d_w_o", merged, dh_out, ta=True)
    dga, dgm, dya, dym = _merge_bwd("merge_b", dmerged, seg('ga'), seg('gm'), ya, ym)
    g['sc_w_out'] = _mm("d_sc_w_out", v, dya, ta=True)
    dv = _mm("d_v", dya, W['sc_w_out'], tb=True)
    g['m_w_out'] = _mm("d_m_w_out", yn, dym, ta=True)
    dyn = _mm("d_yn", dym, W['m_w_out'], tb=True)
    dy, dz, d_mnorm = _gnorm_bwd("gnorm_b", dyn, y, seg('z'), W['m_norm'])
    g['m_norm'] = d_mnorm.reshape(-1)
    dxbc_post, ddt, dpar = _ssd_bwd("ssd_b", dy, xbc, dt_raw, states, W['ssd_par'], l, H)
    g['m_dt_bias'] = dpar[0, :H]
    g['m_A_log'] = dpar[1, :H]
    g['m_D'] = dpar[2, :H]
    dxbc, d_mcw, d_mcb = _mconv_bwd("mconv_b", dxbc_post, proj, poffs['xbc'][0], W['m_taps'], Km, W['m_conv_b'], l)
    g['m_conv_w'] = d_mcw[:Km]
    g['m_conv_b'] = d_mcb.reshape(-1)
    dscb, dscc, dscx, d_scw = _scconv_bwd("scconv_b", dv, proj, poffs['scb'][0], poffs['scc'][0], poffs['scx'][0],
                                          W['sc_taps'], Ksc, l)
    g['sc_conv_w'] = d_scw[:Ksc]
    dproj = jnp.concatenate([dz, dscb, dscc, dscx, dga, dgm, dxbc], axis=1)
    du = _mm("d_u_main", dproj, W['w_in_p'], out_dtype=F32)
    du = _mm("d_u_dt", ddt, W['w_dt'], out_dtype=F32, res=du)
    d_wp = _mm("d_w_in_main", dproj, u, ta=True)
    d_wdt = _mm("d_w_in_dt", ddt, u, ta=True)
    g['w_in'] = _unperm_d_w_in(d_wp, d_wdt, dims)
    dh, dg = _rms_bwd("mix_d_rms", du, h, W['mix_norm'], dh_out)
    g['mix_norm'] = dg.reshape(-1)
    return dh, g


def _ple_layer_fwd(h, p_l, W):
    xn = _rms_fwd("ple_rms", h, W['ple_norm'])
    gpre = _mm("ple_gate", xn, W['ple_w_gate'])
    pp = _mm("ple_proj", p_l, W['ple_w_proj'], tb=True)
    hout = _ple_fwd("ple_f", h, gpre, pp)
    return hout, (xn, gpre, pp)


def _ple_layer_bwd(dh_out, h, p_l, W, saved, dep=None):
    xn, gpre, pp = saved
    g = {}
    dgpre, dpp = _ple_bwd("ple_b", dh_out, gpre, pp, dep=dep)
    g['ple_w_proj'] = _mm("d_ple_proj", dpp, p_l, ta=True)
    g['ple_w_gate'] = _mm("d_ple_gate", xn, dgpre, ta=True)
    dxn = _mm("d_ple_xn", dgpre, W['ple_w_gate'], tb=True)
    dh, dg = _rms_bwd("ple_d_rms", dxn, h, W['ple_norm'], dh_out)
    g['ple_norm'] = dg.reshape(-1)
    return dh, g


def kernel(x, p, ffn1_norm, ffn1_wg, ffn1_wu, ffn1_wd, mix_norm, w_in, sc_conv_w, sc_w_out, m_conv_w, m_conv_b, m_dt_bias, m_A_log, m_D, m_norm, m_w_out, w_o, ffn2_norm, ffn2_wg, ffn2_wu, ffn2_wd, ple_norm, ple_w_gate, ple_w_proj, final_norm, loss_target, m_ffn1_norm, m_ffn1_wg, m_ffn1_wu, m_ffn1_wd, m_mix_norm, m_w_in, m_sc_conv_w, m_sc_w_out, m_m_conv_w, m_m_conv_b, m_m_dt_bias, m_m_A_log, m_m_D, m_m_norm, m_m_w_out, m_w_o, m_ffn2_norm, m_ffn2_wg, m_ffn2_wu, m_ffn2_wd, m_ple_norm, m_ple_w_gate, m_ple_w_proj, m_final_norm, v_ffn1_norm, v_ffn1_wg, v_ffn1_wu, v_ffn1_wd, v_mix_norm, v_w_in, v_sc_conv_w, v_sc_w_out, v_m_conv_w, v_m_conv_b, v_m_dt_bias, v_m_A_log, v_m_D, v_m_norm, v_m_w_out, v_w_o, v_ffn2_norm, v_ffn2_wg, v_ffn2_wu, v_ffn2_wd, v_ple_norm, v_ple_w_gate, v_ple_w_proj, v_final_norm):
    args = (x, p, ffn1_norm, ffn1_wg, ffn1_wu, ffn1_wd, mix_norm, w_in, sc_conv_w, sc_w_out, m_conv_w, m_conv_b, m_dt_bias, m_A_log, m_D, m_norm, m_w_out, w_o, ffn2_norm, ffn2_wg, ffn2_wu, ffn2_wd, ple_norm, ple_w_gate, ple_w_proj, final_norm, loss_target, m_ffn1_norm, m_ffn1_wg, m_ffn1_wu, m_ffn1_wd, m_mix_norm, m_w_in, m_sc_conv_w, m_sc_w_out, m_m_conv_w, m_m_conv_b, m_m_dt_bias, m_m_A_log, m_m_D, m_m_norm, m_m_w_out, m_w_o, m_ffn2_norm, m_ffn2_wg, m_ffn2_wu, m_ffn2_wd, m_ple_norm, m_ple_w_gate, m_ple_w_proj, m_final_norm, v_ffn1_norm, v_ffn1_wg, v_ffn1_wu, v_ffn1_wd, v_mix_norm, v_w_in, v_sc_conv_w, v_sc_w_out, v_m_conv_w, v_m_conv_b, v_m_dt_bias, v_m_A_log, v_m_D, v_m_norm, v_m_w_out, v_w_o, v_ffn2_norm, v_ffn2_wg, v_ffn2_wu, v_ffn2_wd, v_ple_norm, v_ple_w_gate, v_ple_w_proj, v_final_norm)
    names = ARG_NAMES + ['m_' + n for n in WEIGHTS] + ['v_' + n for n in WEIGHTS]
    A = dict(zip(names, args))
    depth = ffn1_norm.shape[0]
    me = 4 * lax.axis_index("x") + 2 * lax.axis_index("y") + lax.axis_index("c")

    dims = (x.shape[-1], m_norm.shape[1], m_conv_b.shape[1], m_dt_bias.shape[1], sc_conv_w.shape[1], m_conv_w.shape[1])
    kind = dict(BIG)

    def work(n, prefix=''):
        return jnp.swapaxes(A[prefix + n], 1, 2) if kind[n] == 'col' else A[prefix + n]

    wb = {n: work(n).astype(BF) for n, _ in BIG}
    srcs = [[wb[ms[0]] if len(ms) == 1 else jnp.concatenate([wb[n] for n in ms], axis=1) for ms in stage] for stage in STAGES]
    conv_g = _unpack(_exchange("gather_conv_taps", _pack([A[n] for n in CONVW]), True), [A[n].shape for n in CONVW])
    taps = {}
    for n, st in zip(CONVW, conv_g):
        taps[n] = _pad_taps(jnp.transpose(st, (1, 2, 0, 3)).reshape(depth, st.shape[2], N_DEV * st.shape[3]))
    ssd_par = _ssd_params(m_dt_bias, m_A_log, m_D)
    small3 = {n: A[n].reshape(depth, 1, -1) for n in SMALL}

    def stage_weights(W, s, l, lands):
        for ms, land, src in zip(STAGES[s], lands, srcs[s]):
            land = lax.dynamic_update_slice(land, src[l][None], (me, 0, 0))
            off = 0
            for n in ms:
                r = wb[n].shape[1]
                W[n] = _full_from_shards(land if len(ms) == 1 else land[:, off:off + r])
                off += r
        if s == 1:
            W['w_in_p'], W['w_dt'] = _perm_w_in(W.pop('w_in'), dims)

    flight = {}

    def begin_layer(l, dep):
        for s in range(len(STAGES)):
            sems, lands, dep = _xchg_begin(f"gather_begin{l}{'abc'[s]}", srcs[s], l, dep)
            flight[(l, s)] = (sems, lands)
        return dep

    def end_stage(W, l, s, after):
        sems, lands = flight.pop((l, s))
        stage_weights(W, s, l, _xchg_end(f"gather_end{l}{'abc'[s]}", srcs[s], lands, sems, l, after))

    tok = begin_layer(0, taps['sc_conv_w'])
    h = x[0]
    saved = []
    layers = []
    for l in range(depth):
        W = {n: (small3[n], l) for n in SMALL}
        W.update(l=l, sc_taps=taps['sc_conv_w'], m_taps=taps['m_conv_w'], m_conv_b=small3['m_conv_b'], ssd_par=ssd_par)
        layers.append(W)
        end_stage(W, l, 0, tok if l == 0 else h)
        h1, s1 = _ffn_fwd("ffn1", h, W['ffn1_norm'], W['ffn1_wg'], W['ffn1_wu'], W['ffn1_wd'])
        end_stage(W, l, 1, h1)
        tok = begin_layer(l + 1, W['w_dt']) if l + 1 < depth else None
        h2, s2 = _mixer_fwd(h1, W, dims, dep=tok)
        end_stage(W, l, 2, h2)
        h3, s3 = _ffn_fwd("ffn2", h2, W['ffn2_norm'], W['ffn2_wg'], W['ffn2_wu'], W['ffn2_wd'])
        h4, s4 = _ple_layer_fwd(h3, p[l, 0], W)
        saved.append((h, h1, h2, h3, s1, s2, s3, s4))
        h = h4

    dh, loss_row, d_final = _loss_head("loss_head", h, final_norm, loss_target[0])
    loss = lax.psum(loss_row[0, 0], ("x", "y", "c"))

    def send_bufs(g, s):
        return [jnp.concatenate([_shards_from_full(g[n]) for n in ms], axis=1) if len(ms) > 1
                else _shards_from_full(g[ms[0]]) for ms in STAGES[s]]

    grads = [None] * depth
    pending = []

    def send_stage(g, l, s, dep):
        send = send_bufs(g, s)
        sems, lands, tok = _xchg_begin(f"scatter_begin{l}{'abc'[s]}", send, None, dep)
        pending.append((l, s, send, lands, sems))
        return tok

    tok = loss.reshape(1, 1)
    for l in reversed(range(depth)):
        W = layers[l]
        h0, h1, h2, h3, s1, s2, s3, s4 = saved[l]
        g = {}
        dh, g4 = _ple_layer_bwd(dh, h3, p[l, 0], W, s4, dep=tok)
        g.update(g4)
        dh, dg, d_wg, d_wu, d_wd = _ffn_bwd("ffn2", dh, h2, W['ffn2_norm'], W['ffn2_wg'], W['ffn2_wu'], W['ffn2_wd'], s3)
        g.update(ffn2_norm=dg.reshape(-1), ffn2_wg=d_wg, ffn2_wu=d_wu, ffn2_wd=d_wd)
        tok = send_stage(g, l, 2, dh)
        dh, g2 = _mixer_bwd(dh, h1, W, dims, s2, dep=tok)
        g.update(g2)
        tok = send_stage(g, l, 1, dh)
        dh, dg, d_wg, d_wu, d_wd = _ffn_bwd("ffn1", dh, h0, W['ffn1_norm'], W['ffn1_wg'], W['ffn1_wu'], W['ffn1_wd'], s1, dep=tok)
        g.update(ffn1_norm=dg.reshape(-1), ffn1_wg=d_wg, ffn1_wu=d_wu, ffn1_wd=d_wd)
        grads[l] = g
        tok = send_stage(g, l, 0, dh)
    grad_x = dh[None]

    g_lands = [[None] * len(STAGES) for _ in range(depth)]
    big_res = [{}, {}, {}, {}]

    def finish(entries, after):
        for l, s, send, lands, sems in entries:
            got = _xchg_end(f"scatter_end{l}{'abc'[s]}", send, lands, sems, None, after)
            after = got[0]
            g_lands[l][s] = [lax.dynamic_update_slice(o, lax.dynamic_slice_in_dim(b, me, 1, axis=0), (me, 0, 0)) for o, b in zip(got, send)]
        return after

    def adam_stages(stages):
        res = None
        for s in stages:
            for gi, ms in enumerate(STAGES[s]):
                off = 0
                for n in ms:
                    res = _sum_adam("adamw_" + n, [g_lands[l][s][gi] for l in range(depth)], off, work(n), work(n, 'm_'), work(n, 'v_'))
                    for k in range(4):
                        big_res[k][n] = jnp.swapaxes(res[k], 1, 2) if kind[n] == 'col' else res[k]
                    off += wb[n].shape[1]
        return res[0]

    after = finish(pending[:-1], tok)
    after = adam_stages(range(1, len(STAGES)))
    after = finish(pending[-1:], after)
    adam_stages([0])

    small_names = SMALL + CONVW
    small_parts = [jnp.stack([grads[l][n] for l in range(depth)]) for n in small_names] + [d_final.reshape(-1)]
    small_sum = _sum8("sum_small", _exchange("gather_small_grads", _pack(small_parts), True, dep=after))
    sg = dict(zip(small_names + ['final_norm'], _unpack(small_sum, [a.shape for a in small_parts])))
    for n in CONVW:
        c = A[n].shape[-1]
        sg[n] = lax.dynamic_slice_in_dim(sg[n], me * c, c, axis=2)
    s_order = small_names + ['final_norm']
    s_shapes = [sg[n].shape for n in s_order]
    s_out = _adam_flat("adamw_small", _pack([sg[n] for n in s_order]), _pack([A[n] for n in s_order]),
                       _pack([A['m_' + n] for n in s_order]), _pack([A['v_' + n] for n in s_order]))
    small_res = [sg] + [dict(zip(s_order, _unpack(flat, s_shapes))) for flat in s_out]

    outs = [loss, grad_x]
    for k in range(4):
        for n in WEIGHTS:
            outs.append(big_res[k][n] if n in big_res[k] else small_res[k][n])
    return tuple(outs)
```

```python
import functools

import jax
import jax.numpy as jnp
from jax import lax
from jax.experimental import pallas as pl
from jax.experimental.pallas import tpu as pltpu

BF = jnp.bfloat16
F32 = jnp.float32

EPS = 1e-6
N_DEV = 8
LANES = 128
SSM_GROUPS = 4
SSM_HEADDIM = 64
SSM_CHUNK = 128
HALO = 16
VMEM_LIMIT = 56 * 1024 * 1024
FLAT_ROW_TILE = 2048

ADAM_LR = 0.001
ADAM_B1 = 0.9
ADAM_B2 = 0.999
ADAM_EPS = 1e-08
ADAM_WD = 0.01
ADAM_STEP = 10

MESH = pl.DeviceIdType.MESH

ARG_NAMES = ['x', 'p', 'ffn1_norm', 'ffn1_wg', 'ffn1_wu', 'ffn1_wd', 'mix_norm', 'w_in', 'sc_conv_w', 'sc_w_out', 'm_conv_w', 'm_conv_b', 'm_dt_bias', 'm_A_log', 'm_D', 'm_norm', 'm_w_out', 'w_o', 'ffn2_norm', 'ffn2_wg', 'ffn2_wu', 'ffn2_wd', 'ple_norm', 'ple_w_gate', 'ple_w_proj', 'final_norm', 'loss_target']
WEIGHTS = ARG_NAMES[2:26]
BIG = [('ffn1_wg', 'col'), ('ffn1_wu', 'col'), ('ffn1_wd', 'row'), ('w_in', 'col'), ('sc_w_out', 'row'),
       ('m_w_out', 'row'), ('w_o', 'row'), ('ffn2_wg', 'col'), ('ffn2_wu', 'col'), ('ffn2_wd', 'row'),
       ('ple_w_gate', 'row'), ('ple_w_proj', 'col')]
CONVW = ['sc_conv_w', 'm_conv_w']
SMALL = ['ffn1_norm', 'mix_norm', 'm_conv_b', 'm_dt_bias', 'm_A_log', 'm_D', 'm_norm', 'ffn2_norm', 'ple_norm']


def _pick(n, cands):
    for c in cands:
        if n % c == 0:
            return c
    return n


def _cp(sem):
    return pltpu.CompilerParams(dimension_semantics=sem, vmem_limit_bytes=VMEM_LIMIT)


def _sigmoid(x):
    return 1.0 / (1.0 + jnp.exp(-x))


def _softplus(x):
    return jnp.maximum(x, 0.0) + jnp.log(1.0 + jnp.exp(-jnp.abs(x)))


def _exchange(name, x, gather, dep=None):
    slab = x.shape if gather else x.shape[1:]

    def body(x_ref, *rest):
        o_ref, send_sems, recv_sems, local_sem = rest[-4:]
        mx, my, mc = lax.axis_index("x"), lax.axis_index("y"), lax.axis_index("c")
        me = 4 * mx + 2 * my + mc

        def src_for(k):
            return x_ref if gather else x_ref.at[k]

        local = pltpu.make_async_copy(src_for(me), o_ref.at[me], local_sem)
        local.start()
        sends = []
        peers = []
        for r in range(1, N_DEV):
            px = (mx + ((r >> 2) & 1)) % 2
            py = (my + ((r >> 1) & 1)) % 2
            pc = (mc + (r & 1)) % 2
            peer = 4 * px + 2 * py + pc
            peers.append(peer)
            cp = pltpu.make_async_remote_copy(
                src_ref=src_for(peer), dst_ref=o_ref.at[me], send_sem=send_sems.at[r - 1], recv_sem=recv_sems.at[r - 1],
                device_id=(px, py, pc), device_id_type=MESH)
            cp.start()
            sends.append(cp)
        for r in range(1, N_DEV):
            peer = peers[r - 1]
            pltpu.make_async_remote_copy(
                src_ref=src_for(peer), dst_ref=o_ref.at[peer], send_sem=send_sems.at[r - 1], recv_sem=recv_sems.at[r - 1],
                device_id=(mx, my, mc), device_id_type=MESH).wait_recv()
        for cp in sends:
            cp.wait_send()
        local.wait()

    return pl.pallas_call(
        body, name=name,
        out_shape=jax.ShapeDtypeStruct((N_DEV,) + tuple(slab), x.dtype),
        in_specs=[pl.BlockSpec(memory_space=pltpu.HBM)] + ([] if dep is None else [pl.BlockSpec(memory_space=pl.ANY)]),
        out_specs=pl.BlockSpec(memory_space=pltpu.HBM),
        scratch_shapes=[pltpu.SemaphoreType.DMA((N_DEV - 1,)), pltpu.SemaphoreType.DMA((N_DEV - 1,)), pltpu.SemaphoreType.DMA],
    )(*([x] if dep is None else [x, dep]))


STAGES = [[['ffn1_wd'], ['ffn1_wg'], ['ffn1_wu']],
          [['w_in'], ['sc_w_out', 'w_o', 'ple_w_gate', 'm_w_out']],
          [['ffn2_wd'], ['ffn2_wg'], ['ffn2_wu'], ['ple_w_proj']]]
_HBM = pl.BlockSpec(memory_space=pltpu.HBM)
_SEM = pl.BlockSpec(memory_space=pltpu.SEMAPHORE)
_ANY = pl.BlockSpec(memory_space=pl.ANY)
_EFFECT = pltpu.SideEffectType.DATAFLOW_SIDE_EFFECTING


def _peer_list():
    mx, my, mc = lax.axis_index("x"), lax.axis_index("y"), lax.axis_index("c")
    out = []
    for r in range(1, N_DEV):
        px = (mx + ((r >> 2) & 1)) % 2
        py = (my + ((r >> 1) & 1)) % 2
        pc = (mc + (r & 1)) % 2
        out.append((px, py, pc, 4 * px + 2 * py + pc))
    return 4 * mx + 2 * my + mc, out


ALL_PEERS = tuple(range(1, N_DEV))
NEAR_PEERS = (1, 2, 4, 6)
RELAYED = (2, 4, 6)


def _xchg_copy(src_refs, land_refs, send_sems, recv_sems, layer, i, r, peer, dst_slab):
    px, py, pc, pidx = peer
    n = len(src_refs)
    src = src_refs[i].at[layer] if layer is not None else src_refs[i].at[pidx]
    return pltpu.make_async_remote_copy(
        src_ref=src, dst_ref=land_refs[i].at[dst_slab], send_sem=send_sems.at[r * n + i], recv_sem=recv_sems.at[r * n + i],
        device_id=(px, py, pc), device_id_type=MESH)


def _xchg_begin(name, srcs, layer, dep, rels=ALL_PEERS):
    n = len(srcs)
    slabs = [tuple(s.shape[1:]) for s in srcs]
    ncp = n * len(rels)

    def body(*refs):
        src_refs, land_refs = refs[:n], refs[n:2 * n]
        send_sems, recv_sems = refs[2 * n + 1], refs[2 * n + 2]
        token = refs[-1]
        me, peers = _peer_list()
        for ri, r in enumerate(rels):
            for i in range(n):
                _xchg_copy(src_refs, land_refs, send_sems, recv_sems, layer, i, ri, peers[r - 1], me).start()
        token[...] = jnp.zeros_like(token)

    lands = [pltpu.with_memory_space_constraint(lax.empty((N_DEV,) + sl, s.dtype), pltpu.HBM) for sl, s in zip(slabs, srcs)]
    out = pl.pallas_call(
        body, name=name,
        out_shape=(pltpu.SemaphoreType.DMA((ncp,)), pltpu.SemaphoreType.DMA((ncp,)),
                   *[pltpu.HBM((N_DEV,) + sl, s.dtype) for sl, s in zip(slabs, srcs)], jax.ShapeDtypeStruct((8, LANES), F32)),
        in_specs=[_HBM] * (2 * n) + [_ANY],
        out_specs=(_SEM, _SEM, *[_HBM] * n, pl.BlockSpec(memory_space=pltpu.VMEM)),
        input_output_aliases={n + i: 2 + i for i in range(n)},
        compiler_params=pltpu.CompilerParams(has_side_effects=_EFFECT),
    )(*[pltpu.with_memory_space_constraint(s, pltpu.HBM) for s in srcs], *lands, dep)
    return (out[0], out[1]), list(out[2:2 + n]), out[-1]


def _xchg_end(name, srcs, lands, sems, layer, after, rels=ALL_PEERS):
    n = len(srcs)

    def body(*refs):
        src_refs, land_refs = refs[:n], refs[n:2 * n]
        send_sems, recv_sems = refs[2 * n], refs[2 * n + 1]
        me, peers = _peer_list()
        for ri, r in enumerate(rels):
            for i in range(n):
                cp = _xchg_copy(src_refs, land_refs, send_sems, recv_sems, layer, i, ri, peers[r - 1], peers[r - 1][3])
                cp.wait_send()
                cp.wait_recv()

    out = pl.pallas_call(
        body, name=name,
        out_shape=tuple(pltpu.HBM(l.shape, l.dtype) for l in lands),
        in_specs=[_HBM] * (2 * n) + [_SEM, _SEM, _ANY], out_specs=tuple([_HBM] * n),
        input_output_aliases={n + i: i for i in range(n)},
        compiler_params=pltpu.CompilerParams(has_side_effects=_EFFECT),
    )(*[pltpu.with_memory_space_constraint(s, pltpu.HBM) for s in srcs], *lands, sems[0], sems[1], after)
    return list(out)


def _relay_copy(land_refs, send_sems, recv_sems, i, qi, slab, sibling):
    n = len(land_refs)
    return pltpu.make_async_remote_copy(
        src_ref=land_refs[i].at[slab], dst_ref=land_refs[i].at[slab], send_sem=send_sems.at[qi * n + i],
        recv_sem=recv_sems.at[qi * n + i], device_id=sibling[:3], device_id_type=MESH)


def _relay_begin(name, lands):
    n = len(lands)
    ncp = n * len(RELAYED)

    def body(*refs):
        land_refs = refs[:n]
        send_sems, recv_sems = refs[n], refs[n + 1]
        me, peers = _peer_list()
        for qi, q in enumerate(RELAYED):
            for i in range(n):
                _relay_copy(land_refs, send_sems, recv_sems, i, qi, peers[q - 1][3], peers[0]).start()

    out = pl.pallas_call(
        body, name=name,
        out_shape=(pltpu.SemaphoreType.DMA((ncp,)), pltpu.SemaphoreType.DMA((ncp,)), *[pltpu.HBM(l.shape, l.dtype) for l in lands]),
        in_specs=[_HBM] * n, out_specs=(_SEM, _SEM, *[_HBM] * n),
        input_output_aliases={i: 2 + i for i in range(n)},
        compiler_params=pltpu.CompilerParams(has_side_effects=_EFFECT),
    )(*lands)
    return (out[0], out[1]), list(out[2:])


def _relay_end(name, lands, sems, after):
    n = len(lands)

    def body(*refs):
        land_refs = refs[:n]
        send_sems, recv_sems = refs[n], refs[n + 1]
        me, peers = _peer_list()
        for qi, q in enumerate(RELAYED):
            for i in range(n):
                _relay_copy(land_refs, send_sems, recv_sems, i, qi, peers[q - 1][3], peers[0]).wait_send()
                _relay_copy(land_refs, send_sems, recv_sems, i, qi, peers[q][3], peers[0]).wait_recv()

    out = pl.pallas_call(
        body, name=name,
        out_shape=tuple(pltpu.HBM(l.shape, l.dtype) for l in lands),
        in_specs=[_HBM] * n + [_SEM, _SEM, _ANY], out_specs=tuple([_HBM] * n),
        input_output_aliases={i: i for i in range(n)},
        compiler_params=pltpu.CompilerParams(has_side_effects=_EFFECT),
    )(*lands, sems[0], sems[1], after)
    return list(out)


MM_TILES = (1024, 1408, 512, 256, 128)
MM_OPERAND_BYTES = 24 * 1024 * 1024


def _mm(name, a, b, *, ta=False, tb=False, out_dtype=None, res=None, alpha=1.0, dep=None):
    out_dtype = out_dtype or BF
    M, K = (a.shape[1], a.shape[0]) if ta else a.shape
    N = b.shape[0] if tb else b.shape[1]
    assert (b.shape[1] if tb else b.shape[0]) == K, (name, a.shape, b.shape)
    tm = _pick(M, MM_TILES)
    tn = _pick(N, MM_TILES)
    per_k = 2 * (tm * a.dtype.itemsize + tn * b.dtype.itemsize)
    tk = [t for t in sorted({K, 4096, 2816, 2560, 2048, 1408, 1024, 512, 256, 128}, reverse=True)
          if K % t == 0 and (t * per_k <= MM_OPERAND_BYTES or t == 128)][0]
    nk = K // tk
    a_spec = pl.BlockSpec((tk, tm), lambda i, j, k: (k, i)) if ta else pl.BlockSpec((tm, tk), lambda i, j, k: (i, k))
    b_spec = pl.BlockSpec((tn, tk), lambda i, j, k: (j, k)) if tb else pl.BlockSpec((tk, tn), lambda i, j, k: (k, j))
    dn = (((0 if ta else 1,), (1 if tb else 0,)), ((), ()))
    has_res = res is not None
    n_dep = 0 if dep is None else 1

    def body(*refs):
        a_ref, b_ref = refs[:2]
        r_ref = refs[2] if has_res else None
        o_ref = refs[2 + has_res + n_dep]

        def finish(v):
            if alpha != 1.0:
                v = v * alpha
            if has_res:
                v = r_ref[...] + v
            o_ref[...] = v.astype(o_ref.dtype)

        part = lax.dot_general(a_ref[...].astype(BF), b_ref[...].astype(BF), dn, preferred_element_type=F32)
        if nk == 1:
            finish(part)
            return
        acc = refs[-1]
        k = pl.program_id(2)

        @pl.when(k == 0)
        def _():
            acc[...] = part

        @pl.when((k > 0) & (k < nk - 1))
        def _():
            acc[...] += part

        @pl.when(k == nk - 1)
        def _():
            finish(acc[...] + part)

    in_specs = [a_spec, b_spec]
    args = [a, b]
    if has_res:
        in_specs.append(pl.BlockSpec((tm, tn), lambda i, j, k: (i, j)))
        args.append(res)
    if dep is not None:
        in_specs.append(_ANY)
        args.append(dep)
    return pl.pallas_call(
        body, name=name, grid=(M // tm, N // tn, nk),
        in_specs=in_specs, out_specs=pl.BlockSpec((tm, tn), lambda i, j, k: (i, j)),
        out_shape=jax.ShapeDtypeStruct((M, N), out_dtype),
        scratch_shapes=[pltpu.VMEM((tm, tn), F32)] if nk > 1 else [],
        compiler_params=_cp(("parallel", "parallel", "arbitrary")),
    )(*args)


def _ew(name, fn, tiled, params, outs, accs=(), tile=256, dep=None):
    tiled = [t if isinstance(t, tuple) else (t, t.shape[1], 0) for t in tiled]
    params = [q if isinstance(q, tuple) else (q, None) for q in params]
    S = tiled[0][0].shape[0]
    T = _pick(S, (tile, 128, 64, 32, 16))
    n_in = len(tiled) + len(params)
    n_dep = 0 if dep is None else 1

    def body(*refs):
        fn(pl.program_id(0) == 0, *refs[:n_in], *refs[n_in + n_dep:])

    in_specs = [pl.BlockSpec((T, w), lambda i, cb=cb: (i, cb)) for _, w, cb in tiled]
    for q, row in params:
        if row is None:
            in_specs.append(pl.BlockSpec(q.shape, lambda i: (0, 0)))
        else:
            in_specs.append(pl.BlockSpec((None, 1, q.shape[2]), lambda i, row=row: (row, 0, 0)))
    args = [t[0] for t in tiled] + [q[0] for q in params]
    if dep is not None:
        in_specs.append(pl.BlockSpec(memory_space=pl.ANY))
        args.append(dep)
    out_specs = [pl.BlockSpec((T, w), lambda i: (i, 0)) for w, _ in outs]
    out_specs += [pl.BlockSpec(shp, lambda i: (0, 0)) for shp, _ in accs]
    out_shape = [jax.ShapeDtypeStruct((S, w), dt) for w, dt in outs]
    out_shape += [jax.ShapeDtypeStruct(shp, dt) for shp, dt in accs]
    res = pl.pallas_call(
        body, name=name, grid=(S // T,), in_specs=in_specs, out_specs=out_specs, out_shape=out_shape,
        compiler_params=_cp(("arbitrary",)),
    )(*args)
    return res


def _prow(g):
    return g if isinstance(g, tuple) else g.reshape(1, -1)


def _rms_fwd(name, h, g, dep=None):
    def fn(first, h_ref, g_ref, o_ref):
        x = h_ref[...]
        r = lax.rsqrt(jnp.mean(x * x, axis=-1, keepdims=True) + EPS)
        o_ref[...] = (x * r * g_ref[...]).astype(o_ref.dtype)

    return _ew(name, fn, [h], [_prow(g)], [(h.shape[1], BF)], dep=dep)[0]


def _rms_bwd(name, dxn, h, g, res):
    D = h.shape[1]

    def fn(first, d_ref, h_ref, r_ref, g_ref, o_ref, dg_ref):
        x = h_ref[...]
        d = d_ref[...].astype(F32)
        r = lax.rsqrt(jnp.mean(x * x, axis=-1, keepdims=True) + EPS)
        xhat = x * r
        dxhat = d * g_ref[...]
        dh = r * (dxhat - xhat * jnp.mean(dxhat * xhat, axis=-1, keepdims=True))
        o_ref[...] = r_ref[...] + dh

        @pl.when(first)
        def _():
            dg_ref[...] = jnp.zeros_like(dg_ref)

        dg_ref[...] += jnp.sum(d * xhat, axis=0, keepdims=True)

    return _ew(name, fn, [dxn, h, res], [_prow(g)], [(D, F32)], [((1, D), F32)])


FFN_TOKEN_TILE = 512


def _ffn_up(name, xn, wgT, wuT, dep=None):
    S, D = xn.shape
    FF = wgT.shape[0]
    tm = _pick(S, (FFN_TOKEN_TILE, 256, 128))
    tn = _pick(FF, MM_TILES)
    n_dep = 0 if dep is None else 1

    def body(x_ref, g_ref, u_ref, *rest):
        a_ref, b_ref, h_ref = rest[n_dep:]
        x = x_ref[...]
        a = _dot_nt(x, g_ref[...])
        b = _dot_nt(x, u_ref[...])
        a_ref[...] = a.astype(BF)
        b_ref[...] = b.astype(BF)
        h_ref[...] = (a * _sigmoid(a) * b).astype(BF)

    wspec = pl.BlockSpec((tn, D), lambda j, i: (j, 0))
    ospec = pl.BlockSpec((tm, tn), lambda j, i: (i, j))
    return pl.pallas_call(
        body, name=name, grid=(FF // tn, S // tm),
        in_specs=[pl.BlockSpec((tm, D), lambda j, i: (i, 0)), wspec, wspec] + ([] if dep is None else [_ANY]),
        out_specs=[ospec] * 3, out_shape=[jax.ShapeDtypeStruct((S, FF), BF)] * 3,
        compiler_params=_cp(("parallel", "arbitrary")),
    )(*([xn, wgT, wuT] + ([] if dep is None else [dep])))


def _ffn_dact(name, dh, wd, a, b, dep=None):
    S, D = dh.shape
    FF = wd.shape[0]
    tm = _pick(S, (FFN_TOKEN_TILE, 256, 128))
    tn = _pick(FF, MM_TILES)
    n_dep = 0 if dep is None else 1

    def body(d_ref, w_ref, a_ref, b_ref, *rest):
        da_ref, db_ref = rest[n_dep:]
        d = 0.5 * _dot_nt(d_ref[...].astype(BF), w_ref[...])
        av = a_ref[...].astype(F32)
        s = _sigmoid(av)
        da_ref[...] = (d * b_ref[...].astype(F32) * (s * (1.0 + av * (1.0 - s)))).astype(BF)
        db_ref[...] = (d * av * s).astype(BF)

    tspec = pl.BlockSpec((tm, tn), lambda j, i: (i, j))
    return pl.pallas_call(
        body, name=name, grid=(FF // tn, S // tm),
        in_specs=[pl.BlockSpec((tm, D), lambda j, i: (i, 0)), pl.BlockSpec((tn, D), lambda j, i: (j, 0)), tspec, tspec]
        + ([] if dep is None else [_ANY]),
        out_specs=[tspec] * 2, out_shape=[jax.ShapeDtypeStruct((S, FF), BF)] * 2,
        compiler_params=_cp(("parallel", "arbitrary")),
    )(*([dh, wd, a, b] + ([] if dep is None else [dep])))


def _merge_fwd(name, ga, gm, ya, ym):
    def fn(first, ga_ref, gm_ref, ya_ref, ym_ref, o_ref):
        o = _sigmoid(ga_ref[...].astype(F32)) * ya_ref[...].astype(F32) + _sigmoid(gm_ref[...].astype(F32)) * ym_ref[...].astype(F32)
        o_ref[...] = o.astype(o_ref.dtype)

    return _ew(name, fn, [ga, gm, ya, ym], [], [(ya.shape[1], BF)])[0]


def _merge_bwd(name, dmerged, ga, gm, ya, ym):
    W = ya.shape[1]

    def fn(first, d_ref, ga_ref, gm_ref, ya_ref, ym_ref, dga_ref, dgm_ref, dya_ref, dym_ref):
        d = d_ref[...].astype(F32)
        sa = _sigmoid(ga_ref[...].astype(F32))
        sm = _sigmoid(gm_ref[...].astype(F32))
        dga_ref[...] = (d * ya_ref[...].astype(F32) * sa * (1.0 - sa)).astype(BF)
        dgm_ref[...] = (d * ym_ref[...].astype(F32) * sm * (1.0 - sm)).astype(BF)
        dya_ref[...] = (d * sa).astype(BF)
        dym_ref[...] = (d * sm).astype(BF)

    return _ew(name, fn, [dmerged, ga, gm, ya, ym], [], [(W, BF)] * 4)


def _gnorm_fwd(name, y, z, w):
    W = y.shape[1]
    gw = W // SSM_GROUPS

    def fn(first, y_ref, z_ref, w_ref, o_ref):
        for g in range(SSM_GROUPS):
            sl = slice(g * gw, (g + 1) * gw)
            zz = z_ref[:, sl].astype(F32)
            t = y_ref[:, sl].astype(F32) * (zz * _sigmoid(zz))
            r = lax.rsqrt(jnp.mean(t * t, axis=-1, keepdims=True) + EPS)
            o_ref[:, sl] = (t * r * w_ref[:, sl]).astype(o_ref.dtype)

    return _ew(name, fn, [y, z], [_prow(w)], [(W, BF)])[0]


def _gnorm_bwd(name, dyn, y, z, w):
    W = y.shape[1]
    gw = W // SSM_GROUPS

    def fn(first, d_ref, y_ref, z_ref, w_ref, dy_ref, dz_ref, dw_ref):
        @pl.when(first)
        def _():
            dw_ref[...] = jnp.zeros_like(dw_ref)

        for g in range(SSM_GROUPS):
            sl = slice(g * gw, (g + 1) * gw)
            zz = z_ref[:, sl].astype(F32)
            yy = y_ref[:, sl].astype(F32)
            d = d_ref[:, sl].astype(F32)
            s = _sigmoid(zz)
            sz = zz * s
            t = yy * sz
            r = lax.rsqrt(jnp.mean(t * t, axis=-1, keepdims=True) + EPS)
            that = t * r
            dthat = d * w_ref[:, sl]
            dt = r * (dthat - that * jnp.mean(dthat * that, axis=-1, keepdims=True))
            dw_ref[:, sl] += jnp.sum(d * that, axis=0, keepdims=True)
            dy_ref[:, sl] = (dt * sz).astype(BF)
            dz_ref[:, sl] = (dt * yy * (s * (1.0 + zz * (1.0 - s)))).astype(BF)

    return _ew(name, fn, [dyn, y, z], [_prow(w)], [(W, BF), (W, BF)], [((1, W), F32)])


def _ple_fwd(name, h, gpre, pp):
    def fn(first, h_ref, g_ref, p_ref, o_ref):
        o_ref[...] = h_ref[...] + _sigmoid(g_ref[...].astype(F32)) * p_ref[...].astype(F32)

    return _ew(name, fn, [h, gpre, pp], [], [(h.shape[1], F32)])[0]


def _ple_bwd(name, dh, gpre, pp, dep=None):
    W = dh.shape[1]

    def fn(first, d_ref, g_ref, p_ref, dg_ref, dp_ref):
        d = d_ref[...]
        s = _sigmoid(g_ref[...].astype(F32))
        dg_ref[...] = (d * p_ref[...].astype(F32) * s * (1.0 - s)).astype(BF)
        dp_ref[...] = (d * s).astype(BF)

    return _ew(name, fn, [dh, gpre, pp], [], [(W, BF), (W, BF)], dep=dep)


def _loss_head(name, h, g, target):
    D = h.shape[1]

    def fn(first, h_ref, t_ref, g_ref, dh_ref, loss_ref, dg_ref):
        x = h_ref[...]
        r = lax.rsqrt(jnp.mean(x * x, axis=-1, keepdims=True) + EPS)
        xhat = x * r
        err = xhat * g_ref[...] - t_ref[...]
        part = 0.5 * jnp.sum(jnp.mean(err * err, axis=-1, keepdims=True), axis=0, keepdims=True)
        dy = err * (1.0 / D)
        dxhat = dy * g_ref[...]
        dh_ref[...] = r * (dxhat - xhat * jnp.mean(dxhat * xhat, axis=-1, keepdims=True))

        @pl.when(first)
        def _():
            loss_ref[...] = jnp.zeros_like(loss_ref)
            dg_ref[...] = jnp.zeros_like(dg_ref)

        loss_ref[...] += jnp.broadcast_to(part, loss_ref.shape)
        dg_ref[...] += jnp.sum(dy * xhat, axis=0, keepdims=True)

    return _ew(name, fn, [h, target], [_prow(g)], [(D, F32)], [((1, LANES), F32), ((1, D), F32)])


def _conv_specs(S, C, offs, l):
    T = _pick(S, (512, 256, 128, 64, 32, 16))
    Ct = [c for c in (512, 256, 128) if C % c == 0 and all(o % c == 0 for o in offs)][0]
    per = T // HALO
    last = S // HALO - 1

    def cur(off=0):
        return pl.BlockSpec((T, Ct), lambda j, i: (i, off // Ct + j))

    def prev(off=0):
        return pl.BlockSpec((HALO, Ct), lambda j, i: (jnp.maximum(i * per - 1, 0), off // Ct + j))

    def nxt(off=0):
        return pl.BlockSpec((HALO, Ct), lambda j, i: (jnp.minimum((i + 1) * per, last), off // Ct + j))

    wspec = pl.BlockSpec((None, 8, Ct), lambda j, i: (l, 0, j))
    return T, Ct, cur, prev, nxt, wspec


def _pad_taps(w):
    return jnp.concatenate([w.astype(F32), jnp.zeros((w.shape[0], 8 - w.shape[1], w.shape[2]), F32)], axis=1)


def _causal(cat, w_ref, K, T, lead):
    views = [cat[lead - (K - 1) + k:lead - (K - 1) + k + T] for k in range(K)]
    out = None
    for k in range(K):
        term = w_ref[k:k + 1, :] * views[k]
        out = term if out is None else out + term
    return out, views


def _anticausal(cat, w_ref, K, T):
    out = None
    for k in range(K):
        o = K - 1 - k
        term = w_ref[k:k + 1, :] * cat[o:o + T]
        out = term if out is None else out + term
    return out


def _scconv_fwd(name, proj, ob, oc, ox, taps, K, l):
    S = proj.shape[0]
    C = taps.shape[2]
    T, Ct, cur, prev, nxt, wspec = _conv_specs(S, C, (ob, oc, ox), l)

    def body(b_ref, c_ref, x_ref, cp_ref, xp_ref, w_ref, o_ref):
        i = pl.program_id(1)
        q = c_ref[...].astype(F32) * x_ref[...].astype(F32)
        qp = jnp.where(i == 0, 0.0, cp_ref[...].astype(F32) * xp_ref[...].astype(F32))
        cat = jnp.concatenate([qp, q], axis=0)
        o_ref[...] = (b_ref[...].astype(F32) * _causal(cat, w_ref, K, T, HALO)[0]).astype(o_ref.dtype)

    return pl.pallas_call(
        body, name=name, grid=(C // Ct, S // T),
        in_specs=[cur(ob), cur(oc), cur(ox), prev(oc), prev(ox), wspec], out_specs=cur(),
        out_shape=jax.ShapeDtypeStruct((S, C), BF), compiler_params=_cp(("parallel", "arbitrary")),
    )(proj, proj, proj, proj, proj, taps)


def _scconv_bwd(name, dv, proj, ob, oc, ox, taps, K, l):
    S = proj.shape[0]
    C = taps.shape[2]
    T, Ct, cur, prev, nxt, wspec = _conv_specs(S, C, (ob, oc, ox), l)
    n_t = S // T

    def body(d_ref, b_ref, c_ref, x_ref, dn_ref, bn_ref, cp_ref, xp_ref, w_ref, db_ref, dc_ref, dx_ref, dw_ref):
        i = pl.program_id(1)
        c = c_ref[...].astype(F32)
        x = x_ref[...].astype(F32)
        d = d_ref[...].astype(F32)
        q = c * x
        qp = jnp.where(i == 0, 0.0, cp_ref[...].astype(F32) * xp_ref[...].astype(F32))
        catq = jnp.concatenate([qp, q], axis=0)
        cv, q_views = _causal(catq, w_ref, K, T, HALO)
        db_ref[...] = (d * cv).astype(BF)
        dcv = d * b_ref[...].astype(F32)
        dcvn = jnp.where(i == n_t - 1, 0.0, dn_ref[...].astype(F32) * bn_ref[...].astype(F32))
        catd = jnp.concatenate([dcv, dcvn], axis=0)
        dq = _anticausal(catd, w_ref, K, T)
        dc_ref[...] = (dq * x).astype(BF)
        dx_ref[...] = (dq * c).astype(BF)

        @pl.when(i == 0)
        def _():
            dw_ref[...] = jnp.zeros_like(dw_ref)

        for k in range(K):
            dw_ref[k:k + 1, :] += jnp.sum(dcv * q_views[k], axis=0, keepdims=True)

    return pl.pallas_call(
        body, name=name, grid=(C // Ct, n_t),
        in_specs=[cur(), cur(ob), cur(oc), cur(ox), nxt(), nxt(ob), prev(oc), prev(ox), wspec],
        out_specs=[cur(), cur(), cur(), pl.BlockSpec((8, Ct), lambda j, i: (0, j))],
        out_shape=[jax.ShapeDtypeStruct((S, C), BF)] * 3 + [jax.ShapeDtypeStruct((8, C), F32)],
        compiler_params=_cp(("parallel", "arbitrary")),
    )(dv, proj, proj, proj, dv, proj, proj, proj, taps)


def _mconv_fwd(name, proj, ox, taps, K, bias, l):
    S = proj.shape[0]
    C = taps.shape[2]
    T, Ct, cur, prev, nxt, wspec = _conv_specs(S, C, (ox,), l)
    bspec = pl.BlockSpec((None, 1, Ct), lambda j, i: (l, 0, j))

    def body(x_ref, xp_ref, w_ref, b_ref, o_ref):
        i = pl.program_id(1)
        xp = jnp.where(i == 0, 0.0, xp_ref[...].astype(F32))
        cat = jnp.concatenate([xp, x_ref[...].astype(F32)], axis=0)
        pre = _causal(cat, w_ref, K, T, HALO)[0] + b_ref[...]
        o_ref[...] = (pre * _sigmoid(pre)).astype(o_ref.dtype)

    return pl.pallas_call(
        body, name=name, grid=(C // Ct, S // T),
        in_specs=[cur(ox), prev(ox), wspec, bspec], out_specs=cur(),
        out_shape=jax.ShapeDtypeStruct((S, C), BF), compiler_params=_cp(("parallel", "arbitrary")),
    )(proj, proj, taps, bias)


def _mconv_bwd(name, dout, proj, ox, taps, K, bias, l):
    S = proj.shape[0]
    C = taps.shape[2]
    T, Ct, cur, prev, nxt, wspec = _conv_specs(S, C, (ox,), l)
    n_t = S // T
    bspec = pl.BlockSpec((None, 1, Ct), lambda j, i: (l, 0, j))

    def body(d_ref, dn_ref, x_ref, xp_ref, xn_ref, w_ref, b_ref, dx_ref, dw_ref, db_ref):
        i = pl.program_id(1)
        xp = jnp.where(i == 0, 0.0, xp_ref[...].astype(F32))
        cat3 = jnp.concatenate([xp, x_ref[...].astype(F32), xn_ref[...].astype(F32)], axis=0)
        pre, x_views = _causal(cat3, w_ref, K, T + HALO, HALO)
        pre = pre + b_ref[...]
        dn = jnp.where(i == n_t - 1, 0.0, dn_ref[...].astype(F32))
        dext = jnp.concatenate([d_ref[...].astype(F32), dn], axis=0)
        s = _sigmoid(pre)
        dpre = dext * (s * (1.0 + pre * (1.0 - s)))
        dx_ref[...] = _anticausal(dpre, w_ref, K, T).astype(BF)
        dcur = dpre[:T]

        @pl.when(i == 0)
        def _():
            dw_ref[...] = jnp.zeros_like(dw_ref)
            db_ref[...] = jnp.zeros_like(db_ref)

        db_ref[...] += jnp.sum(dcur, axis=0, keepdims=True)
        for k in range(K):
            dw_ref[k:k + 1, :] += jnp.sum(dcur * x_views[k][:T], axis=0, keepdims=True)

    return pl.pallas_call(
        body, name=name, grid=(C // Ct, n_t),
        in_specs=[cur(), nxt(), cur(ox), prev(ox), nxt(ox), wspec, bspec],
        out_specs=[cur(), pl.BlockSpec((8, Ct), lambda j, i: (0, j)), pl.BlockSpec((1, Ct), lambda j, i: (0, j))],
        out_shape=[jax.ShapeDtypeStruct((S, C), BF), jax.ShapeDtypeStruct((8, C), F32), jax.ShapeDtypeStruct((1, C), F32)],
        compiler_params=_cp(("parallel", "arbitrary")),
    )(dout, dout, proj, proj, proj, taps, bias)


def _tri_matmul(tri_bf, v):
    hi = v.astype(BF)
    r1 = v - hi.astype(F32)
    mid = r1.astype(BF)
    lo = (r1 - mid.astype(F32)).astype(BF)
    dot = functools.partial(jnp.dot, preferred_element_type=F32)
    return dot(tri_bf, hi) + dot(tri_bf, mid) + dot(tri_bf, lo)


def _dot_nt(a, b):
    return lax.dot_general(a, b, (((1,), (1,)), ((), ())), preferred_element_type=F32)


def _dot_tn(a, b):
    return lax.dot_general(a, b, (((0,), (0,)), ((), ())), preferred_element_type=F32)


def _dot_nn(a, b):
    return jnp.dot(a, b, preferred_element_type=F32)


def _ssd_chunk_scalars(dtr_ref, par_ref, L):
    row_i = lax.broadcasted_iota(jnp.int32, (L, L), 0)
    col_i = lax.broadcasted_iota(jnp.int32, (L, L), 1)
    tri = row_i >= col_i
    pre = dtr_ref[...] + par_ref[0:1, :]
    dt_all = _softplus(pre)
    A_row = -jnp.exp(par_ref[1:2, :])
    a_all = dt_all * A_row
    acum_all = _tri_matmul(tri.astype(BF), a_all)
    return tri, pre, dt_all, A_row, a_all, acum_all, acum_all.T


def _ssd_dims(xbc, heads):
    S, conv_dim = xbc.shape
    inner = heads * SSM_HEADDIM
    N = (conv_dim - inner) // (2 * SSM_GROUPS)
    gw = inner // SSM_GROUPS
    PP = gw // LANES
    L = min(SSM_CHUNK, S)
    assert N == LANES and gw % LANES == 0 and inner % (SSM_GROUPS * N) == 0 and S % L == 0
    return S, inner, N, gw, PP, L, S // L


def _ssd_params(dt_bias, A_log, Dp):
    depth, H = dt_bias.shape
    rows = jnp.stack([dt_bias, A_log, Dp], axis=1).astype(F32)
    rows = jnp.concatenate([rows, jnp.zeros((depth, 3, LANES - H), F32)], axis=2)
    return jnp.concatenate([rows, jnp.zeros((depth, 5, LANES), F32)], axis=1)


def _ssd_fwd(name, xbc, dt_raw, par, l, heads):
    S, inner, N, gw, PP, L, nc = _ssd_dims(xbc, heads)
    G = SSM_GROUPS

    def body(x_ref, b_ref, c_ref, dtr_ref, par_ref, y_ref, st_out_ref, st_ref):
        @pl.when(pl.program_id(0) == 0)
        def _():
            st_ref[...] = jnp.zeros_like(st_ref)

        tri, pre, dt_all, A_row, a_all, acum_all, acumT = _ssd_chunk_scalars(dtr_ref, par_ref, L)
        lane = lax.broadcasted_iota(jnp.int32, (L, LANES), 1)
        lane1 = lax.broadcasted_iota(jnp.int32, (1, LANES), 1)
        lo = lane < SSM_HEADDIM
        lo1 = lane1 < SSM_HEADDIM
        for g in range(G):
            Bb = b_ref[:, g * N:(g + 1) * N]
            Cb = c_ref[:, g * N:(g + 1) * N]
            BbT = Bb.astype(F32).T.astype(BF)
            Gm = _dot_nt(Cb, Bb)
            for j in range(PP):
                pj = g * PP + j
                h0, h1 = 2 * pj, 2 * pj + 1
                cols = slice(pj * LANES, (pj + 1) * LANES)
                x = x_ref[:, cols].astype(F32)
                dt_l = jnp.where(lo, dt_all[:, h0:h0 + 1], dt_all[:, h1:h1 + 1])
                ac0 = acum_all[:, h0:h0 + 1]
                ac1 = acum_all[:, h1:h1 + 1]
                ac_l = jnp.where(lo, ac0, ac1)
                E0 = jnp.exp(jnp.where(tri, ac0 - acumT[h0:h0 + 1, :], -1e30))
                E1 = jnp.exp(jnp.where(tri, ac1 - acumT[h1:h1 + 1, :], -1e30))
                xd = x * dt_l
                xdb = xd.astype(BF)
                yd = jnp.where(lo, _dot_nn((Gm * E0).astype(BF), xdb), _dot_nn((Gm * E1).astype(BF), xdb))
                prevT = st_ref[pj]
                st_out_ref[0, pj] = prevT
                P = _dot_nn(Cb, prevT.astype(BF))
                D_l = jnp.where(lo1, par_ref[2:3, h0:h0 + 1], par_ref[2:3, h1:h1 + 1])
                y_ref[:, cols] = (yd + P * jnp.exp(ac_l) + D_l * x).astype(y_ref.dtype)
                al0 = ac0[L - 1:L, :]
                al1 = ac1[L - 1:L, :]
                Wm = xd * jnp.exp(jnp.where(lo, al0, al1) - ac_l)
                eal = jnp.where(lo1, jnp.exp(al0), jnp.exp(al1))
                st_ref[pj] = eal * prevT + _dot_nn(BbT, Wm.astype(BF))

    gn = G * N
    return pl.pallas_call(
        body, name=name, grid=(nc,),
        in_specs=[pl.BlockSpec((L, inner), lambda c: (c, 0)), pl.BlockSpec((L, gn), lambda c: (c, inner // gn)),
                  pl.BlockSpec((L, gn), lambda c: (c, inner // gn + 1)),
                  pl.BlockSpec((L, LANES), lambda c: (c, 0)), pl.BlockSpec((None, 8, LANES), lambda c: (l, 0, 0))],
        out_specs=[pl.BlockSpec((L, inner), lambda c: (c, 0)), pl.BlockSpec((1, G * PP, N, LANES), lambda c: (c, 0, 0, 0))],
        out_shape=[jax.ShapeDtypeStruct((S, inner), BF), jax.ShapeDtypeStruct((nc, G * PP, N, LANES), F32)],
        scratch_shapes=[pltpu.VMEM((G * PP, N, LANES), F32)],
        compiler_params=_cp(("arbitrary",)),
    )(xbc, xbc, xbc, dt_raw, par)


def _ssd_bwd(name, dy, xbc, dt_raw, states, par, l, heads):
    S, inner, N, gw, PP, L, nc = _ssd_dims(xbc, heads)
    G = SSM_GROUPS

    def body(dy_ref, x_ref, b_ref, c_ref, dtr_ref, par_ref, st_in_ref, d_ref, ddt_ref, dpar_ref, dst_ref):
        @pl.when(pl.program_id(0) == 0)
        def _():
            dst_ref[...] = jnp.zeros_like(dst_ref)
            dpar_ref[...] = jnp.zeros_like(dpar_ref)

        tri, pre, dt_all, A_row, a_all, acum_all, acumT = _ssd_chunk_scalars(dtr_ref, par_ref, L)
        lane = lax.broadcasted_iota(jnp.int32, (L, LANES), 1)
        lane1 = lax.broadcasted_iota(jnp.int32, (1, LANES), 1)
        rowl = lax.broadcasted_iota(jnp.int32, (L, LANES), 0)
        lo = lane < SSM_HEADDIM
        lo1 = lane1 < SSM_HEADDIM
        triT = lax.broadcasted_iota(jnp.int32, (L, L), 0) <= lax.broadcasted_iota(jnp.int32, (L, L), 1)
        sel_r = lax.broadcasted_iota(jnp.int32, (3 * LANES, LANES), 0)
        sel_c = lax.broadcasted_iota(jnp.int32, (3 * LANES, LANES), 1)
        dac_all = jnp.zeros((L, LANES), F32)
        xds_all = jnp.zeros((L, LANES), F32)
        dD_row = jnp.zeros((1, LANES), F32)

        def half_sums(v):
            return (jnp.sum(jnp.where(lo1, v, 0.0), axis=1, keepdims=True), jnp.sum(jnp.where(lo1, 0.0, v), axis=1, keepdims=True))

        def dot2(v, sel):
            hi = v.astype(BF)
            return _dot_nn(hi, sel) + _dot_nn((v - hi.astype(F32)).astype(BF), sel)

        for pj in range(G * PP):
            g, j = divmod(pj, PP)
            if j == 0:
                Bb = b_ref[:, g * N:(g + 1) * N]
                Cb = c_ref[:, g * N:(g + 1) * N]
                CbT = Cb.astype(F32).T.astype(BF)
                Gm = _dot_nt(Cb, Bb)
                GmT = _dot_nt(Bb, Cb)
                dG = jnp.zeros((L, L), F32)
                dGT = jnp.zeros((L, L), F32)
                dBacc = jnp.zeros((L, N), F32)
                dCacc = jnp.zeros((L, N), F32)
            h0, h1 = 2 * pj, 2 * pj + 1
            to_h0 = (sel_r < LANES) | ((sel_r >= 2 * LANES) & (sel_r < 2 * LANES + SSM_HEADDIM))
            sel3 = jnp.where(sel_c == jnp.where(to_h0, h0, h1), 1.0, 0.0).astype(BF)
            sel1 = sel3[2 * LANES:]
            sl = slice(pj * LANES, (pj + 1) * LANES)
            x = x_ref[:, sl].astype(F32)
            dyv = dy_ref[:, sl].astype(F32)
            dt_l = jnp.where(lo, dt_all[:, h0:h0 + 1], dt_all[:, h1:h1 + 1])
            ac0 = acum_all[:, h0:h0 + 1]
            ac1 = acum_all[:, h1:h1 + 1]
            r0 = acumT[h0:h0 + 1, :]
            r1 = acumT[h1:h1 + 1, :]
            ac_l = jnp.where(lo, ac0, ac1)
            E0 = jnp.exp(jnp.where(tri, ac0 - r0, -1e30))
            E1 = jnp.exp(jnp.where(tri, ac1 - r1, -1e30))
            E0T = jnp.exp(jnp.where(triT, r0 - ac0, -1e30))
            E1T = jnp.exp(jnp.where(triT, r1 - ac1, -1e30))
            xd = x * dt_l
            xdb = xd.astype(BF)
            M0 = Gm * E0
            M1 = Gm * E1
            ea_l = jnp.exp(ac_l)
            al0 = ac0[L - 1:L, :]
            al1 = ac1[L - 1:L, :]
            dte_l = jnp.exp(jnp.where(lo, al0, al1) - ac_l)
            Wm = xd * dte_l
            prevT = st_in_ref[0, pj]
            prevTb = prevT.astype(BF)
            P = _dot_nn(Cb, prevTb)
            D_l = jnp.where(lo1, par_ref[2:3, h0:h0 + 1], par_ref[2:3, h1:h1 + 1])
            dx = D_l * dyv
            dD0, dD1 = half_sums(jnp.sum(dyv * x, axis=0, keepdims=True))
            dyb = dyv.astype(BF)
            dy0b = jnp.where(lo, dyv, 0.0).astype(BF)
            dy1b = jnp.where(lo, 0.0, dyv).astype(BF)
            dM0 = _dot_nt(dy0b, xdb)
            dM1 = _dot_nt(dy1b, xdb)
            dM0T = _dot_nt(xdb, dy0b)
            dM1T = _dot_nt(xdb, dy1b)
            M0T = GmT * E0T
            M1T = GmT * E1T
            dxd = jnp.where(lo, _dot_nn(M0T.astype(BF), dyb), _dot_nn(M1T.astype(BF), dyb))
            dG = dG + dM0 * E0 + dM1 * E1
            dGT = dGT + dM0T * E0T + dM1T * E1T
            z0 = dM0 * M0 - dM0T * M0T
            z1 = dM1 * M1 - dM1T * M1T
            dP = dyv * ea_l
            dPb = dP.astype(BF)
            dCacc = dCacc + _dot_nt(dPb, prevTb)
            dprevT = _dot_nn(CbT, dPb)
            dnewT = dst_ref[pj]
            dnewTb = dnewT.astype(BF)
            e0 = jnp.exp(al0)
            e1 = jnp.exp(al1)
            dprevT = dprevT + jnp.where(lo1, e0, e1) * dnewT
            u0, u1 = half_sums(jnp.sum(dnewT * prevT, axis=0, keepdims=True))
            dW = _dot_nn(Bb, dnewTb)
            dBacc = dBacc + _dot_nt(Wm.astype(BF), dnewTb)
            dxd = dxd + dW * dte_l
            tt = dW * Wm
            t0, t1 = half_sums(jnp.sum(tt, axis=0, keepdims=True))
            dal0 = u0 * e0 + t0
            dal1 = u1 * e1 + t1
            dac_all = dac_all + dot2(jnp.concatenate([z0, z1, dP * P - tt], axis=1), sel3)
            dac_all = dac_all + jnp.where(rowl == L - 1, jnp.where(lane == h0, dal0, 0.0) + jnp.where(lane == h1, dal1, 0.0), 0.0)
            dx = dx + dxd * dt_l
            xds_all = xds_all + dot2(dxd * x, sel1)
            dst_ref[pj] = dprevT
            d_ref[:, sl] = dx.astype(d_ref.dtype)
            dD_row = dD_row + jnp.where(lane1 == h0, dD0, 0.0) + jnp.where(lane1 == h1, dD1, 0.0)
            if j == PP - 1:
                d_ref[:, inner + g * N:inner + (g + 1) * N] = (dBacc + _dot_nn(dGT.astype(BF), Cb)).astype(d_ref.dtype)
                d_ref[:, inner + (G + g) * N:inner + (G + g + 1) * N] = (dCacc + _dot_nn(dG.astype(BF), Bb)).astype(d_ref.dtype)

        row_i = lax.broadcasted_iota(jnp.int32, (L, L), 0)
        col_i = lax.broadcasted_iota(jnp.int32, (L, L), 1)
        da_all = _tri_matmul((row_i <= col_i).astype(BF), dac_all)
        real = lane < heads
        ddt_all = da_all * A_row + xds_all
        draw = jnp.where(real, ddt_all * _sigmoid(pre), 0.0)
        ddt_ref[...] = draw
        dpar_ref[0:1, :] += jnp.sum(draw, axis=0, keepdims=True)
        dpar_ref[1:2, :] += jnp.sum(jnp.where(real, da_all * a_all, 0.0), axis=0, keepdims=True)
        dpar_ref[2:3, :] += dD_row

    gn = G * N
    conv_dim = xbc.shape[1]
    rev = lambda c: nc - 1 - c
    return pl.pallas_call(
        body, name=name, grid=(nc,),
        in_specs=[pl.BlockSpec((L, inner), lambda c: (rev(c), 0)), pl.BlockSpec((L, inner), lambda c: (rev(c), 0)),
                  pl.BlockSpec((L, gn), lambda c: (rev(c), inner // gn)), pl.BlockSpec((L, gn), lambda c: (rev(c), inner // gn + 1)),
                  pl.BlockSpec((L, LANES), lambda c: (rev(c), 0)), pl.BlockSpec((None, 8, LANES), lambda c: (l, 0, 0)),
                  pl.BlockSpec((1, G * PP, N, LANES), lambda c: (rev(c), 0, 0, 0))],
        out_specs=[pl.BlockSpec((L, conv_dim), lambda c: (rev(c), 0)), pl.BlockSpec((L, LANES), lambda c: (rev(c), 0)),
                   pl.BlockSpec((8, LANES), lambda c: (0, 0))],
        out_shape=[jax.ShapeDtypeStruct((S, conv_dim), BF), jax.ShapeDtypeStruct((S, LANES), F32), jax.ShapeDtypeStruct((8, LANES), F32)],
        scratch_shapes=[pltpu.VMEM((G * PP, N, LANES), F32)],
        compiler_params=_cp(("arbitrary",)),
    )(dy, xbc, xbc, xbc, dt_raw, par, states)


def _adamw(g, w, m, v):
    m2 = ADAM_B1 * m + (1.0 - ADAM_B1) * g
    v2 = ADAM_B2 * v + (1.0 - ADAM_B2) * (g * g)
    m_hat = m2 / (1.0 - ADAM_B1 ** ADAM_STEP)
    v_hat = v2 / (1.0 - ADAM_B2 ** ADAM_STEP)
    delta = -ADAM_LR * (m_hat / (jnp.sqrt(v_hat) + ADAM_EPS) + ADAM_WD * w)
    return delta, m2, v2


def _flat_tile(R):
    return _pick(R, (FLAT_ROW_TILE, 1024, 512, 256, 128, 64, 32, 16, 8))


def _sum_adam(name, lands, off, w, m, v):
    depth, r, c = w.shape
    cap = max(16, (4 * 1024 * 1024) // (N_DEV * c * 2))
    row_tiles = [t for t in (512, 256, 128, 64, 32, 16) if r % t == 0 and off % t == 0 and t <= cap]
    if row_tiles:
        tr, tc = row_tiles[0], c
        ob = off // tr
        n_t = r // tr
        spec = pl.BlockSpec((None, tr, c), lambda l, t: (l, t, 0))
        land_specs = [pl.BlockSpec((N_DEV, tr, c), lambda l, t, i=i: (0, jnp.where(l == i, ob + t, ob), 0)) for i in range(depth)]
    else:
        assert off == 0 and lands[0].shape[1] == r and c % LANES == 0
        tc = LANES
        n_t = c // tc
        spec = pl.BlockSpec((None, r, tc), lambda l, t: (l, 0, t))
        land_specs = [pl.BlockSpec((N_DEV, r, tc), lambda l, t, i=i: (0, 0, jnp.where(l == i, t, 0))) for i in range(depth)]

    def body(*refs):
        land_refs = refs[:depth]
        w_ref, m_ref, v_ref, g_ref, d_ref, m2_ref, v2_ref = refs[depth:]
        l = pl.program_id(0)
        for i in range(depth):
            @pl.when(l == i)
            def _(i=i):
                g = land_refs[i][0].astype(F32)
                for k in range(1, N_DEV):
                    g = g + land_refs[i][k].astype(F32)
                g_ref[...] = g
                d_ref[...], m2_ref[...], v2_ref[...] = _adamw(g, w_ref[...], m_ref[...], v_ref[...])

    return pl.pallas_call(
        body, name=name, grid=(depth, n_t),
        in_specs=land_specs + [spec, spec, spec],
        out_specs=[spec] * 4, out_shape=[jax.ShapeDtypeStruct((depth, r, c), F32)] * 4,
        compiler_params=_cp(("arbitrary", "arbitrary")),
    )(*lands, w, m, v)


def _sum8(name, parts):
    R = parts.shape[1]
    TR = _flat_tile(R)

    def body(p_ref, g_ref):
        g = p_ref[0]
        for k in range(1, N_DEV):
            g = g + p_ref[k]
        g_ref[...] = g

    return pl.pallas_call(
        body, name=name, grid=(R // TR,),
        in_specs=[pl.BlockSpec((N_DEV, TR, LANES), lambda i: (0, i, 0))],
        out_specs=pl.BlockSpec((TR, LANES), lambda i: (i, 0)), out_shape=jax.ShapeDtypeStruct((R, LANES), F32),
        compiler_params=_cp(("parallel",)),
    )(parts)


def _adam_flat(name, g, w, m, v):
    R = w.shape[0]
    TR = _flat_tile(R)

    def body(g_ref, w_ref, m_ref, v_ref, d_ref, m2_ref, v2_ref):
        d_ref[...], m2_ref[...], v2_ref[...] = _adamw(g_ref[...], w_ref[...], m_ref[...], v_ref[...])

    spec = pl.BlockSpec((TR, LANES), lambda i: (i, 0))
    return pl.pallas_call(
        body, name=name, grid=(R // TR,), in_specs=[spec] * 4, out_specs=[spec] * 3,
        out_shape=[jax.ShapeDtypeStruct((R, LANES), F32)] * 3, compiler_params=_cp(("parallel",)),
    )(g, w, m, v)


PART_ROWS = 16


def _nrows(shape):
    n = 1
    for s in shape:
        n *= s
    r = -(-n // LANES)
    return -(-r // PART_ROWS) * PART_ROWS


def _as_rows(a):
    n = a.size
    r = _nrows(a.shape)
    f = a.reshape(-1)
    if r * LANES != n:
        f = jnp.concatenate([f, jnp.zeros((r * LANES - n,), a.dtype)])
    return f.reshape(r, LANES)


def _pack(arrs, mult=PART_ROWS):
    cat = jnp.concatenate([_as_rows(a) for a in arrs], axis=0)
    pad = (-cat.shape[0]) % mult
    if pad:
        cat = jnp.concatenate([cat, jnp.zeros((pad, LANES), cat.dtype)], axis=0)
    return cat


def _unpack(flat, shapes):
    lead = flat.shape[:-2]
    out = []
    o = 0
    for shp in shapes:
        n = 1
        for s in shp:
            n *= s
        r = _nrows(shp)
        blk = flat[..., o:o + r, :].reshape(lead + (r * LANES,))
        out.append(blk[..., :n].reshape(lead + tuple(shp)))
        o += r
    return out


def _full_from_shards(st):
    return st.reshape(st.shape[0] * st.shape[1], st.shape[2])


def _shards_from_full(full):
    return full.reshape(N_DEV, full.shape[0] // N_DEV, full.shape[1])


def _ffn_fwd(tag, h, g, wgT, wuT, wd, dep=None):
    xn = _rms_fwd(tag + "_rms", h, g, dep=dep)
    a, b, hmid = _ffn_up(tag + "_up", xn, wgT, wuT)
    hout = _mm(tag + "_down", hmid, wd, out_dtype=F32, res=h, alpha=0.5)
    return hout, (xn, a, b, hmid)


def _ffn_bwd(tag, dh_out, h, g, wgT, wuT, wd, saved, dep=None):
    xn, a, b, hmid = saved
    da, db = _ffn_dact(tag + "_d_act", dh_out, wd, a, b, dep=dep)
    d_wd = _mm(tag + "_d_wd", hmid, dh_out, ta=True, alpha=0.5)
    d_wgT = _mm(tag + "_d_wg", da, xn, ta=True)
    d_wuT = _mm(tag + "_d_wu", db, xn, ta=True)
    dxn = _mm(tag + "_d_xn_g", da, wgT, out_dtype=F32)
    dxn = _mm(tag + "_d_xn_u", db, wuT, out_dtype=F32, res=dxn)
    dh, dg = _rms_bwd(tag + "_d_rms", dxn, h, g, dh_out)
    return dh, dg, d_wgT, d_wuT, d_wd


SEG_NAMES = ['scb', 'scc', 'scx', 'z', 'xbc', 'dt', 'ga', 'gm']
PERM = ['z', 'scb', 'scc', 'scx', 'ga', 'gm', 'xbc']


def _seg_layout(dims):
    D, inner, conv_dim, H = dims[:4]
    widths = dict(zip(SEG_NAMES, [D, D, D, inner, conv_dim, H, D, D]))
    offs, o = {}, 0
    for n in SEG_NAMES:
        offs[n] = (o, widths[n])
        o += widths[n]
    poffs, o = {}, 0
    for n in PERM:
        poffs[n] = (o, widths[n])
        o += widths[n]
    return offs, poffs


def _perm_w_in(w_inT, dims):
    offs, _ = _seg_layout(dims)
    wp = jnp.concatenate([w_inT[offs[n][0]:offs[n][0] + offs[n][1]] for n in PERM], axis=0)
    o, w = offs['dt']
    wdt = jnp.concatenate([w_inT[o:o + w], jnp.zeros((LANES - w, w_inT.shape[1]), w_inT.dtype)], axis=0)
    return wp, wdt


def _unperm_d_w_in(d_wp, d_wdt, dims):
    offs, poffs = _seg_layout(dims)
    H = dims[3]
    return jnp.concatenate([d_wdt[:H] if n == 'dt' else d_wp[poffs[n][0]:poffs[n][0] + poffs[n][1]] for n in SEG_NAMES], axis=0)


def _mixer_fwd(h, W, dims, dep=None):
    H, Ksc, Km = dims[3:]
    l = W['l']
    _, poffs = _seg_layout(dims)

    def seg(n):
        o, w = poffs[n]
        assert o % w == 0
        return (proj, w, o // w)

    u = _rms_fwd("mix_rms", h, W['mix_norm'], dep=dep)
    proj = _mm("inproj", u, W['w_in_p'], tb=True)
    dt_raw = _mm("inproj_dt", u, W['w_dt'], tb=True, out_dtype=F32)
    v = _scconv_fwd("scconv_f", proj, poffs['scb'][0], poffs['scc'][0], poffs['scx'][0], W['sc_taps'], Ksc, l)
    ya = _mm("sc_out", v, W['sc_w_out'])
    xbc = _mconv_fwd("mconv_f", proj, poffs['xbc'][0], W['m_taps'], Km, W['m_conv_b'], l)
    y, states = _ssd_fwd("ssd_f", xbc, dt_raw, W['ssd_par'], l, H)
    yn = _gnorm_fwd("gnorm_f", y, seg('z'), W['m_norm'])
    ym = _mm("m_out", yn, W['m_w_out'])
    merged = _merge_fwd("merge_f", seg('ga'), seg('gm'), ya, ym)
    hout = _mm("w_o", merged, W['w_o'], out_dtype=F32, res=h)
    return hout, (u, proj, dt_raw, v, ya, xbc, y, states, yn, ym, merged)


def _mixer_bwd(dh_out, h, W, dims, saved, dep=None):
    u, proj, dt_raw, v, ya, xbc, y, states, yn, ym, merged = saved
    H, Ksc, Km = dims[3:]
    l = W['l']
    _, poffs = _seg_layout(dims)

    def seg(n):
        o, w = poffs[n]
        return (proj, w, o // w)

    g = {}
    dmerged = _mm("d_merged", dh_out, W['w_o'], tb=True, dep=dep)
    g['w_o'] = _mm("d_w_o", merged, dh_out, ta=True)
    dga, dgm, dya, dym = _merge_bwd("merge_b", dmerged, seg('ga'), seg('gm'), ya, ym)
    g['sc_w_out'] = _mm("d_sc_w_out", v, dya, ta=True)
    dv = _mm("d_v", dya, W['sc_w_out'], tb=True)
    g['m_w_out'] = _mm("d_m_w_out", yn, dym, ta=True)
    dyn = _mm("d_yn", dym, W['m_w_out'], tb=True)
    dy, dz, d_mnorm = _gnorm_bwd("gnorm_b", dyn, y, seg('z'), W['m_norm'])
    g['m_norm'] = d_mnorm.reshape(-1)
    dxbc_post, ddt, dpar = _ssd_bwd("ssd_b", dy, xbc, dt_raw, states, W['ssd_par'], l, H)
    g['m_dt_bias'] = dpar[0, :H]
    g['m_A_log'] = dpar[1, :H]
    g['m_D'] = dpar[2, :H]
    dxbc, d_mcw, d_mcb = _mconv_bwd("mconv_b", dxbc_post, proj, poffs['xbc'][0], W['m_taps'], Km, W['m_conv_b'], l)
    g['m_conv_w'] = d_mcw[:Km]
    g['m_conv_b'] = d_mcb.reshape(-1)
    dscb, dscc, dscx, d_scw = _scconv_bwd("scconv_b", dv, proj, poffs['scb'][0], poffs['scc'][0], poffs['scx'][0],
                                          W['sc_taps'], Ksc, l)
    g['sc_conv_w'] = d_scw[:Ksc]
    dproj = jnp.concatenate([dz, dscb, dscc, dscx, dga, dgm, dxbc], axis=1)
    du = _mm("d_u_main", dproj, W['w_in_p'], out_dtype=F32)
    du = _mm("d_u_dt", ddt, W['w_dt'], out_dtype=F32, res=du)
    d_wp = _mm("d_w_in_main", dproj, u, ta=True)
    d_wdt = _mm("d_w_in_dt", ddt, u, ta=True)
    g['w_in'] = _unperm_d_w_in(d_wp, d_wdt, dims)
    dh, dg = _rms_bwd("mix_d_rms", du, h, W['mix_norm'], dh_out)
    g['mix_norm'] = dg.reshape(-1)
    return dh, g


def _ple_layer_fwd(h, p_l, W):
    xn = _rms_fwd("ple_rms", h, W['ple_norm'])
    gpre = _mm("ple_gate", xn, W['ple_w_gate'])
    pp = _mm("ple_proj", p_l, W['ple_w_proj'], tb=True)
    hout = _ple_fwd("ple_f", h, gpre, pp)
    return hout, (xn, gpre, pp)


def _ple_layer_bwd(dh_out, h, p_l, W, saved, dep=None):
    xn, gpre, pp = saved
    g = {}
    dgpre, dpp = _ple_bwd("ple_b", dh_out, gpre, pp, dep=dep)
    g['ple_w_proj'] = _mm("d_ple_proj", dpp, p_l, ta=True)
    g['ple_w_gate'] = _mm("d_ple_gate", xn, dgpre, ta=True)
    dxn = _mm("d_ple_xn", dgpre, W['ple_w_gate'], tb=True)
    dh, dg = _rms_bwd("ple_d_rms", dxn, h, W['ple_norm'], dh_out)
    g['ple_norm'] = dg.reshape(-1)
    return dh, g


def kernel(x, p, ffn1_norm, ffn1_wg, ffn1_wu, ffn1_wd, mix_norm, w_in, sc_conv_w, sc_w_out, m_conv_w, m_conv_b, m_dt_bias, m_A_log, m_D, m_norm, m_w_out, w_o, ffn2_norm, ffn2_wg, ffn2_wu, ffn2_wd, ple_norm, ple_w_gate, ple_w_proj, final_norm, loss_target, m_ffn1_norm, m_ffn1_wg, m_ffn1_wu, m_ffn1_wd, m_mix_norm, m_w_in, m_sc_conv_w, m_sc_w_out, m_m_conv_w, m_m_conv_b, m_m_dt_bias, m_m_A_log, m_m_D, m_m_norm, m_m_w_out, m_w_o, m_ffn2_norm, m_ffn2_wg, m_ffn2_wu, m_ffn2_wd, m_ple_norm, m_ple_w_gate, m_ple_w_proj, m_final_norm, v_ffn1_norm, v_ffn1_wg, v_ffn1_wu, v_ffn1_wd, v_mix_norm, v_w_in, v_sc_conv_w, v_sc_w_out, v_m_conv_w, v_m_conv_b, v_m_dt_bias, v_m_A_log, v_m_D, v_m_norm, v_m_w_out, v_w_o, v_ffn2_norm, v_ffn2_wg, v_ffn2_wu, v_ffn2_wd, v_ple_norm, v_ple_w_gate, v_ple_w_proj, v_final_norm):
    args = (x, p, ffn1_norm, ffn1_wg, ffn1_wu, ffn1_wd, mix_norm, w_in, sc_conv_w, sc_w_out, m_conv_w, m_conv_b, m_dt_bias, m_A_log, m_D, m_norm, m_w_out, w_o, ffn2_norm, ffn2_wg, ffn2_wu, ffn2_wd, ple_norm, ple_w_gate, ple_w_proj, final_norm, loss_target, m_ffn1_norm, m_ffn1_wg, m_ffn1_wu, m_ffn1_wd, m_mix_norm, m_w_in, m_sc_conv_w, m_sc_w_out, m_m_conv_w, m_m_conv_b, m_m_dt_bias, m_m_A_log, m_m_D, m_m_norm, m_m_w_out, m_w_o, m_ffn2_norm, m_ffn2_wg, m_ffn2_wu, m_ffn2_wd, m_ple_norm, m_ple_w_gate, m_ple_w_proj, m_final_norm, v_ffn1_norm, v_ffn1_wg, v_ffn1_wu, v_ffn1_wd, v_mix_norm, v_w_in, v_sc_conv_w, v_sc_w_out, v_m_conv_w, v_m_conv_b, v_m_dt_bias, v_m_A_log, v_m_D, v_m_norm, v_m_w_out, v_w_o, v_ffn2_norm, v_ffn2_wg, v_ffn2_wu, v_ffn2_wd, v_ple_norm, v_ple_w_gate, v_ple_w_proj, v_final_norm)
    names = ARG_NAMES + ['m_' + n for n in WEIGHTS] + ['v_' + n for n in WEIGHTS]
    A = dict(zip(names, args))
    depth = ffn1_norm.shape[0]
    me = 4 * lax.axis_index("x") + 2 * lax.axis_index("y") + lax.axis_index("c")

    dims = (x.shape[-1], m_norm.shape[1], m_conv_b.shape[1], m_dt_bias.shape[1], sc_conv_w.shape[1], m_conv_w.shape[1])
    kind = dict(BIG)

    def work(n, prefix=''):
        return jnp.swapaxes(A[prefix + n], 1, 2) if kind[n] == 'col' else A[prefix + n]

    wb = {n: work(n).astype(BF) for n, _ in BIG}
    srcs = [[wb[ms[0]] if len(ms) == 1 else jnp.concatenate([wb[n] for n in ms], axis=1) for ms in stage] for stage in STAGES]
    conv_g = _unpack(_exchange("gather_conv_taps", _pack([A[n] for n in CONVW]), True), [A[n].shape for n in CONVW])
    taps = {}
    for n, st in zip(CONVW, conv_g):
        taps[n] = _pad_taps(jnp.transpose(st, (1, 2, 0, 3)).reshape(depth, st.shape[2], N_DEV * st.shape[3]))
    ssd_par = _ssd_params(m_dt_bias, m_A_log, m_D)
    small3 = {n: A[n].reshape(depth, 1, -1) for n in SMALL}

    def stage_weights(W, s, l, lands):
        for ms, land, src in zip(STAGES[s], lands, srcs[s]):
            land = lax.dynamic_update_slice(land, src[l][None], (me, 0, 0))
            off = 0
            for n in ms:
                r = wb[n].shape[1]
                W[n] = _full_from_shards(land if len(ms) == 1 else land[:, off:off + r])
                off += r
        if s == 1:
            W['w_in_p'], W['w_dt'] = _perm_w_in(W.pop('w_in'), dims)

    flight = {}

    via_sibling = {(0, 0), (0, 1)}

    def begin_layer(l, dep):
        for s in range(len(STAGES)):
            rels = NEAR_PEERS if (l, s) in via_sibling else ALL_PEERS
            sems, lands, dep = _xchg_begin(f"gather_begin{l}{'abc'[s]}", srcs[s], l, dep, rels)
            flight[(l, s)] = (sems, lands)
        return dep

    def end_stage(W, l, s, after):
        sems, lands = flight.pop((l, s))
        tag = f"{l}{'abc'[s]}"
        if (l, s) in via_sibling:
            lands = _xchg_end("gather_end" + tag, srcs[s], lands, sems, l, after, NEAR_PEERS)
            rsems, lands = _relay_begin("gather_relay" + tag, lands)
            lands = _relay_end("gather_relayed" + tag, lands, rsems, after)
        else:
            lands = _xchg_end("gather_end" + tag, srcs[s], lands, sems, l, after)
        stage_weights(W, s, l, lands)

    tok = begin_layer(0, taps['sc_conv_w'])
    h = x[0]
    saved = []
    layers = []
    for l in range(depth):
        W = {n: (small3[n], l) for n in SMALL}
        W.update(l=l, sc_taps=taps['sc_conv_w'], m_taps=taps['m_conv_w'], m_conv_b=small3['m_conv_b'], ssd_par=ssd_par)
        layers.append(W)
        end_stage(W, l, 0, tok if l == 0 else h)
        h1, s1 = _ffn_fwd("ffn1", h, W['ffn1_norm'], W['ffn1_wg'], W['ffn1_wu'], W['ffn1_wd'])
        end_stage(W, l, 1, h1)
        tok = begin_layer(l + 1, W['w_dt']) if l + 1 < depth else None
        h2, s2 = _mixer_fwd(h1, W, dims, dep=tok)
        end_stage(W, l, 2, h2)
        h3, s3 = _ffn_fwd("ffn2", h2, W['ffn2_norm'], W['ffn2_wg'], W['ffn2_wu'], W['ffn2_wd'])
        h4, s4 = _ple_layer_fwd(h3, p[l, 0], W)
        saved.append((h, h1, h2, h3, s1, s2, s3, s4))
        h = h4

    dh, loss_row, d_final = _loss_head("loss_head", h, final_norm, loss_target[0])
    loss = lax.psum(loss_row[0, 0], ("x", "y", "c"))

    def send_bufs(g, s):
        return [jnp.concatenate([_shards_from_full(g[n]) for n in ms], axis=1) if len(ms) > 1
                else _shards_from_full(g[ms[0]]) for ms in STAGES[s]]

    grads = [None] * depth
    pending = []

    def send_stage(g, l, s, dep):
        send = send_bufs(g, s)
        sems, lands, tok = _xchg_begin(f"scatter_begin{l}{'abc'[s]}", send, None, dep)
        pending.append((l, s, send, lands, sems))
        return tok

    tok = loss.reshape(1, 1)
    for l in reversed(range(depth)):
        W = layers[l]
        h0, h1, h2, h3, s1, s2, s3, s4 = saved[l]
        g = {}
        dh, g4 = _ple_layer_bwd(dh, h3, p[l, 0], W, s4, dep=tok)
        g.update(g4)
        dh, dg, d_wg, d_wu, d_wd = _ffn_bwd("ffn2", dh, h2, W['ffn2_norm'], W['ffn2_wg'], W['ffn2_wu'], W['ffn2_wd'], s3)
        g.update(ffn2_norm=dg.reshape(-1), ffn2_wg=d_wg, ffn2_wu=d_wu, ffn2_wd=d_wd)
        tok = send_stage(g, l, 2, dh)
        dh, g2 = _mixer_bwd(dh, h1, W, dims, s2, dep=tok)
        g.update(g2)
        tok = send_stage(g, l, 1, dh)
        dh, dg, d_wg, d_wu, d_wd = _ffn_bwd("ffn1", dh, h0, W['ffn1_norm'], W['ffn1_wg'], W['ffn1_wu'], W['ffn1_wd'], s1, dep=tok)
        g.update(ffn1_norm=dg.reshape(-1), ffn1_wg=d_wg, ffn1_wu=d_wu, ffn1_wd=d_wd)
        grads[l] = g
        tok = send_stage(g, l, 0, dh)
    grad_x = dh[None]

    g_lands = [[None] * len(STAGES) for _ in range(depth)]
    big_res = [{}, {}, {}, {}]

    def finish(entries, after):
        for l, s, send, lands, sems in entries:
            got = _xchg_end(f"scatter_end{l}{'abc'[s]}", send, lands, sems, None, after)
            after = got[0]
            g_lands[l][s] = [lax.dynamic_update_slice(o, lax.dynamic_slice_in_dim(b, me, 1, axis=0), (me, 0, 0)) for o, b in zip(got, send)]
        return after

    def adam_stages(stages):
        res = None
        for s in stages:
            for gi, ms in enumerate(STAGES[s]):
                off = 0
                for n in ms:
                    res = _sum_adam("adamw_" + n, [g_lands[l][s][gi] for l in range(depth)], off, work(n), work(n, 'm_'), work(n, 'v_'))
                    for k in range(4):
                        big_res[k][n] = jnp.swapaxes(res[k], 1, 2) if kind[n] == 'col' else res[k]
                    off += wb[n].shape[1]
        return res[0]

    after = finish(pending[:-1], tok)
    after = adam_stages(range(1, len(STAGES)))
    after = finish(pending[-1:], after)
    adam_stages([0])

    small_names = SMALL + CONVW
    small_parts = [jnp.stack([grads[l][n] for l in range(depth)]) for n in small_names] + [d_final.reshape(-1)]
    small_sum = _sum8("sum_small", _exchange("gather_small_grads", _pack(small_parts), True, dep=after))
    sg = dict(zip(small_names + ['final_norm'], _unpack(small_sum, [a.shape for a in small_parts])))
    for n in CONVW:
        c = A[n].shape[-1]
        sg[n] = lax.dynamic_slice_in_dim(sg[n], me * c, c, axis=2)
    s_order = small_names + ['final_norm']
    s_shapes = [sg[n].shape for n in s_order]
    s_out = _adam_flat("adamw_small", _pack([sg[n] for n in s_order]), _pack([A[n] for n in s_order]),
                       _pack([A['m_' + n] for n in s_order]), _pack([A['v_' + n] for n in s_order]))
    small_res = [sg] + [dict(zip(s_order, _unpack(flat, s_shapes))) for flat in s_out]

    outs = [loss, grad_x]
    for k in range(4):
        for n in WEIGHTS:
            outs.append(big_res[k][n] if n in big_res[k] else small_res[k][n])
    return tuple(outs)
```

```python
import functools

import jax
import jax.numpy as jnp
from jax import lax
from jax.experimental import pallas as pl
from jax.experimental.pallas import tpu as pltpu

BF = jnp.bfloat16
F32 = jnp.float32

EPS = 1e-6
N_DEV = 8
LANES = 128
SSM_GROUPS = 4
SSM_HEADDIM = 64
SSM_CHUNK = 128
HALO = 16
VMEM_LIMIT = 56 * 1024 * 1024
FLAT_ROW_TILE = 2048

ADAM_LR = 0.001
ADAM_B1 = 0.9
ADAM_B2 = 0.999
ADAM_EPS = 1e-08
ADAM_WD = 0.01
ADAM_STEP = 10

MESH = pl.DeviceIdType.MESH

ARG_NAMES = ['x', 'p', 'ffn1_norm', 'ffn1_wg', 'ffn1_wu', 'ffn1_wd', 'mix_norm', 'w_in', 'sc_conv_w', 'sc_w_out', 'm_conv_w', 'm_conv_b', 'm_dt_bias', 'm_A_log', 'm_D', 'm_norm', 'm_w_out', 'w_o', 'ffn2_norm', 'ffn2_wg', 'ffn2_wu', 'ffn2_wd', 'ple_norm', 'ple_w_gate', 'ple_w_proj', 'final_norm', 'loss_target']
WEIGHTS = ARG_NAMES[2:26]
BIG = [('ffn1_wg', 'col'), ('ffn1_wu', 'col'), ('ffn1_wd', 'row'), ('w_in', 'col'), ('sc_w_out', 'row'),
       ('m_w_out', 'row'), ('w_o', 'row'), ('ffn2_wg', 'col'), ('ffn2_wu', 'col'), ('ffn2_wd', 'row'),
       ('ple_w_gate', 'row'), ('ple_w_proj', 'col')]
CONVW = ['sc_conv_w', 'm_conv_w']
SMALL = ['ffn1_norm', 'mix_norm', 'm_conv_b', 'm_dt_bias', 'm_A_log', 'm_D', 'm_norm', 'ffn2_norm', 'ple_norm']


def _pick(n, cands):
    for c in cands:
        if n % c == 0:
            return c
    return n


def _cp(sem):
    return pltpu.CompilerParams(dimension_semantics=sem, vmem_limit_bytes=VMEM_LIMIT)


def _sigmoid(x):
    return 1.0 / (1.0 + jnp.exp(-x))


def _softplus(x):
    return jnp.maximum(x, 0.0) + jnp.log(1.0 + jnp.exp(-jnp.abs(x)))


def _exchange(name, x, gather, dep=None):
    slab = x.shape if gather else x.shape[1:]

    def body(x_ref, *rest):
        o_ref, send_sems, recv_sems, local_sem = rest[-4:]
        mx, my, mc = lax.axis_index("x"), lax.axis_index("y"), lax.axis_index("c")
        me = 4 * mx + 2 * my + mc

        def src_for(k):
            return x_ref if gather else x_ref.at[k]

        local = pltpu.make_async_copy(src_for(me), o_ref.at[me], local_sem)
        local.start()
        sends = []
        peers = []
        for r in range(1, N_DEV):
            px = (mx + ((r >> 2) & 1)) % 2
            py = (my + ((r >> 1) & 1)) % 2
            pc = (mc + (r & 1)) % 2
            peer = 4 * px + 2 * py + pc
            peers.append(peer)
            cp = pltpu.make_async_remote_copy(
                src_ref=src_for(peer), dst_ref=o_ref.at[me], send_sem=send_sems.at[r - 1], recv_sem=recv_sems.at[r - 1],
                device_id=(px, py, pc), device_id_type=MESH)
            cp.start()
            sends.append(cp)
        for r in range(1, N_DEV):
            peer = peers[r - 1]
            pltpu.make_async_remote_copy(
                src_ref=src_for(peer), dst_ref=o_ref.at[peer], send_sem=send_sems.at[r - 1], recv_sem=recv_sems.at[r - 1],
                device_id=(mx, my, mc), device_id_type=MESH).wait_recv()
        for cp in sends:
            cp.wait_send()
        local.wait()

    return pl.pallas_call(
        body, name=name,
        out_shape=jax.ShapeDtypeStruct((N_DEV,) + tuple(slab), x.dtype),
        in_specs=[pl.BlockSpec(memory_space=pltpu.HBM)] + ([] if dep is None else [pl.BlockSpec(memory_space=pl.ANY)]),
        out_specs=pl.BlockSpec(memory_space=pltpu.HBM),
        scratch_shapes=[pltpu.SemaphoreType.DMA((N_DEV - 1,)), pltpu.SemaphoreType.DMA((N_DEV - 1,)), pltpu.SemaphoreType.DMA],
    )(*([x] if dep is None else [x, dep]))


STAGES = [[['ffn1_wd'], ['ffn1_wg'], ['ffn1_wu']],
          [['w_in'], ['sc_w_out', 'w_o', 'ple_w_gate', 'm_w_out']],
          [['ffn2_wd'], ['ffn2_wg'], ['ffn2_wu'], ['ple_w_proj']]]
_HBM = pl.BlockSpec(memory_space=pltpu.HBM)
_SEM = pl.BlockSpec(memory_space=pltpu.SEMAPHORE)
_ANY = pl.BlockSpec(memory_space=pl.ANY)
_EFFECT = pltpu.SideEffectType.DATAFLOW_SIDE_EFFECTING


def _peer_list():
    mx, my, mc = lax.axis_index("x"), lax.axis_index("y"), lax.axis_index("c")
    out = []
    for r in range(1, N_DEV):
        px = (mx + ((r >> 2) & 1)) % 2
        py = (my + ((r >> 1) & 1)) % 2
        pc = (mc + (r & 1)) % 2
        out.append((px, py, pc, 4 * px + 2 * py + pc))
    return 4 * mx + 2 * my + mc, out


ALL_PEERS = tuple(range(1, N_DEV))
NEAR_PEERS = (1, 2, 4, 6)
RELAYED = (2, 4, 6)


def _xchg_copy(src_refs, land_refs, send_sems, recv_sems, layer, i, r, peer, dst_slab):
    px, py, pc, pidx = peer
    n = len(src_refs)
    src = src_refs[i].at[layer] if layer is not None else src_refs[i].at[pidx]
    return pltpu.make_async_remote_copy(
        src_ref=src, dst_ref=land_refs[i].at[dst_slab], send_sem=send_sems.at[r * n + i], recv_sem=recv_sems.at[r * n + i],
        device_id=(px, py, pc), device_id_type=MESH)


def _xchg_begin(name, srcs, layer, dep, rels=ALL_PEERS):
    n = len(srcs)
    slabs = [tuple(s.shape[1:]) for s in srcs]
    ncp = n * len(rels)

    def body(*refs):
        src_refs, land_refs = refs[:n], refs[n:2 * n]
        send_sems, recv_sems = refs[2 * n + 1], refs[2 * n + 2]
        token = refs[-1]
        me, peers = _peer_list()
        for ri, r in enumerate(rels):
            for i in range(n):
                _xchg_copy(src_refs, land_refs, send_sems, recv_sems, layer, i, ri, peers[r - 1], me).start()
        token[...] = jnp.zeros_like(token)

    lands = [pltpu.with_memory_space_constraint(lax.empty((N_DEV,) + sl, s.dtype), pltpu.HBM) for sl, s in zip(slabs, srcs)]
    out = pl.pallas_call(
        body, name=name,
        out_shape=(pltpu.SemaphoreType.DMA((ncp,)), pltpu.SemaphoreType.DMA((ncp,)),
                   *[pltpu.HBM((N_DEV,) + sl, s.dtype) for sl, s in zip(slabs, srcs)], jax.ShapeDtypeStruct((8, LANES), F32)),
        in_specs=[_HBM] * (2 * n) + [_ANY],
        out_specs=(_SEM, _SEM, *[_HBM] * n, pl.BlockSpec(memory_space=pltpu.VMEM)),
        input_output_aliases={n + i: 2 + i for i in range(n)},
        compiler_params=pltpu.CompilerParams(has_side_effects=_EFFECT),
    )(*[pltpu.with_memory_space_constraint(s, pltpu.HBM) for s in srcs], *lands, dep)
    return (out[0], out[1]), list(out[2:2 + n]), out[-1]


def _xchg_end(name, srcs, lands, sems, layer, after, rels=ALL_PEERS):
    n = len(srcs)

    def body(*refs):
        src_refs, land_refs = refs[:n], refs[n:2 * n]
        send_sems, recv_sems = refs[2 * n], refs[2 * n + 1]
        me, peers = _peer_list()
        for ri, r in enumerate(rels):
            for i in range(n):
                cp = _xchg_copy(src_refs, land_refs, send_sems, recv_sems, layer, i, ri, peers[r - 1], peers[r - 1][3])
                cp.wait_send()
                cp.wait_recv()

    out = pl.pallas_call(
        body, name=name,
        out_shape=tuple(pltpu.HBM(l.shape, l.dtype) for l in lands),
        in_specs=[_HBM] * (2 * n) + [_SEM, _SEM, _ANY], out_specs=tuple([_HBM] * n),
        input_output_aliases={n + i: i for i in range(n)},
        compiler_params=pltpu.CompilerParams(has_side_effects=_EFFECT),
    )(*[pltpu.with_memory_space_constraint(s, pltpu.HBM) for s in srcs], *lands, sems[0], sems[1], after)
    return list(out)


def _relay_copy(land_refs, send_sems, recv_sems, i, qi, slab, sibling):
    n = len(land_refs)
    return pltpu.make_async_remote_copy(
        src_ref=land_refs[i].at[slab], dst_ref=land_refs[i].at[slab], send_sem=send_sems.at[qi * n + i],
        recv_sem=recv_sems.at[qi * n + i], device_id=sibling[:3], device_id_type=MESH)


def _relay_begin(name, lands):
    n = len(lands)
    ncp = n * len(RELAYED)

    def body(*refs):
        land_refs = refs[:n]
        send_sems, recv_sems = refs[n], refs[n + 1]
        me, peers = _peer_list()
        for qi, q in enumerate(RELAYED):
            for i in range(n):
                _relay_copy(land_refs, send_sems, recv_sems, i, qi, peers[q - 1][3], peers[0]).start()

    out = pl.pallas_call(
        body, name=name,
        out_shape=(pltpu.SemaphoreType.DMA((ncp,)), pltpu.SemaphoreType.DMA((ncp,)), *[pltpu.HBM(l.shape, l.dtype) for l in lands]),
        in_specs=[_HBM] * n, out_specs=(_SEM, _SEM, *[_HBM] * n),
        input_output_aliases={i: 2 + i for i in range(n)},
        compiler_params=pltpu.CompilerParams(has_side_effects=_EFFECT),
    )(*lands)
    return (out[0], out[1]), list(out[2:])


def _relay_end(name, lands, sems, after):
    n = len(lands)

    def body(*refs):
        land_refs = refs[:n]
        send_sems, recv_sems = refs[n], refs[n + 1]
        me, peers = _peer_list()
        for qi, q in enumerate(RELAYED):
            for i in range(n):
                _relay_copy(land_refs, send_sems, recv_sems, i, qi, peers[q - 1][3], peers[0]).wait_send()
                _relay_copy(land_refs, send_sems, recv_sems, i, qi, peers[q][3], peers[0]).wait_recv()

    out = pl.pallas_call(
        body, name=name,
        out_shape=tuple(pltpu.HBM(l.shape, l.dtype) for l in lands),
        in_specs=[_HBM] * n + [_SEM, _SEM, _ANY], out_specs=tuple([_HBM] * n),
        input_output_aliases={i: i for i in range(n)},
        compiler_params=pltpu.CompilerParams(has_side_effects=_EFFECT),
    )(*lands, sems[0], sems[1], after)
    return list(out)


MM_TILES = (1024, 1408, 512, 256, 128)
MM_OPERAND_BYTES = 24 * 1024 * 1024


def _mm(name, a, b, *, ta=False, tb=False, out_dtype=None, res=None, alpha=1.0, dep=None):
    out_dtype = out_dtype or BF
    M, K = (a.shape[1], a.shape[0]) if ta else a.shape
    N = b.shape[0] if tb else b.shape[1]
    assert (b.shape[1] if tb else b.shape[0]) == K, (name, a.shape, b.shape)
    tm = _pick(M, MM_TILES)
    tn = _pick(N, MM_TILES)
    per_k = 2 * (tm * a.dtype.itemsize + tn * b.dtype.itemsize)
    tk = [t for t in sorted({K, 4096, 2816, 2560, 2048, 1408, 1024, 512, 256, 128}, reverse=True)
          if K % t == 0 and (t * per_k <= MM_OPERAND_BYTES or t == 128)][0]
    nk = K // tk
    a_spec = pl.BlockSpec((tk, tm), lambda i, j, k: (k, i)) if ta else pl.BlockSpec((tm, tk), lambda i, j, k: (i, k))
    b_spec = pl.BlockSpec((tn, tk), lambda i, j, k: (j, k)) if tb else pl.BlockSpec((tk, tn), lambda i, j, k: (k, j))
    dn = (((0 if ta else 1,), (1 if tb else 0,)), ((), ()))
    has_res = res is not None
    n_dep = 0 if dep is None else 1

    def body(*refs):
        a_ref, b_ref = refs[:2]
        r_ref = refs[2] if has_res else None
        o_ref = refs[2 + has_res + n_dep]

        def finish(v):
            if alpha != 1.0:
                v = v * alpha
            if has_res:
                v = r_ref[...] + v
            o_ref[...] = v.astype(o_ref.dtype)

        part = lax.dot_general(a_ref[...].astype(BF), b_ref[...].astype(BF), dn, preferred_element_type=F32)
        if nk == 1:
            finish(part)
            return
        acc = refs[-1]
        k = pl.program_id(2)

        @pl.when(k == 0)
        def _():
            acc[...] = part

        @pl.when((k > 0) & (k < nk - 1))
        def _():
            acc[...] += part

        @pl.when(k == nk - 1)
        def _():
            finish(acc[...] + part)

    in_specs = [a_spec, b_spec]
    args = [a, b]
    if has_res:
        in_specs.append(pl.BlockSpec((tm, tn), lambda i, j, k: (i, j)))
        args.append(res)
    if dep is not None:
        in_specs.append(_ANY)
        args.append(dep)
    return pl.pallas_call(
        body, name=name, grid=(M // tm, N // tn, nk),
        in_specs=in_specs, out_specs=pl.BlockSpec((tm, tn), lambda i, j, k: (i, j)),
        out_shape=jax.ShapeDtypeStruct((M, N), out_dtype),
        scratch_shapes=[pltpu.VMEM((tm, tn), F32)] if nk > 1 else [],
        compiler_params=_cp(("parallel", "parallel", "arbitrary")),
    )(*args)


def _mm_rms_bwd(name, a, b, acc_in, h, g, dh_res, *, tb=False):
    M, K = a.shape
    N = b.shape[0] if tb else b.shape[1]
    assert (b.shape[1] if tb else b.shape[0]) == K and h.shape == (M, N), (name, a.shape, b.shape)
    tm = _pick(M, (FFN_TOKEN_TILE, 256, 128))
    dn = (((1,), (1 if tb else 0,)), ((), ()))
    has_acc = acc_in is not None
    g_arr, g_row = _prow(g) if isinstance(g, tuple) else (_prow(g), None)

    def body(*refs):
        a_ref, b_ref = refs[:2]
        c_ref = refs[2] if has_acc else None
        h_ref, g_ref, r_ref, o_ref, dg_ref = refs[2 + has_acc:]
        d = lax.dot_general(a_ref[...].astype(BF), b_ref[...].astype(BF), dn, preferred_element_type=F32)
        if has_acc:
            d = c_ref[...] + d
        x = h_ref[...]
        r = lax.rsqrt(jnp.mean(x * x, axis=-1, keepdims=True) + EPS)
        xhat = x * r
        dxhat = d * g_ref[...]
        o_ref[...] = r_ref[...] + r * (dxhat - xhat * jnp.mean(dxhat * xhat, axis=-1, keepdims=True))

        @pl.when(pl.program_id(0) == 0)
        def _():
            dg_ref[...] = jnp.zeros_like(dg_ref)

        dg_ref[...] += jnp.sum(d * xhat, axis=0, keepdims=True)

    row = pl.BlockSpec((tm, N), lambda i: (i, 0))
    gspec = pl.BlockSpec((1, N), lambda i: (0, 0)) if g_row is None else pl.BlockSpec((None, 1, N), lambda i: (g_row, 0, 0))
    in_specs = [pl.BlockSpec((tm, K), lambda i: (i, 0)), pl.BlockSpec(b.shape, lambda i: (0, 0))]
    args = [a, b]
    if has_acc:
        in_specs.append(row)
        args.append(acc_in)
    return pl.pallas_call(
        body, name=name, grid=(M // tm,),
        in_specs=in_specs + [row, gspec, row], out_specs=[row, pl.BlockSpec((1, N), lambda i: (0, 0))],
        out_shape=[jax.ShapeDtypeStruct((M, N), F32), jax.ShapeDtypeStruct((1, N), F32)],
        compiler_params=_cp(("arbitrary",)),
    )(*args, h, g_arr, dh_res)


def _ew(name, fn, tiled, params, outs, accs=(), tile=256, dep=None):
    tiled = [t if isinstance(t, tuple) else (t, t.shape[1], 0) for t in tiled]
    params = [q if isinstance(q, tuple) else (q, None) for q in params]
    S = tiled[0][0].shape[0]
    T = _pick(S, (tile, 128, 64, 32, 16))
    n_in = len(tiled) + len(params)
    n_dep = 0 if dep is None else 1

    def body(*refs):
        fn(pl.program_id(0) == 0, *refs[:n_in], *refs[n_in + n_dep:])

    in_specs = [pl.BlockSpec((T, w), lambda i, cb=cb: (i, cb)) for _, w, cb in tiled]
    for q, row in params:
        if row is None:
            in_specs.append(pl.BlockSpec(q.shape, lambda i: (0, 0)))
        else:
            in_specs.append(pl.BlockSpec((None, 1, q.shape[2]), lambda i, row=row: (row, 0, 0)))
    args = [t[0] for t in tiled] + [q[0] for q in params]
    if dep is not None:
        in_specs.append(pl.BlockSpec(memory_space=pl.ANY))
        args.append(dep)
    out_specs = [pl.BlockSpec((T, w), lambda i: (i, 0)) for w, _ in outs]
    out_specs += [pl.BlockSpec(shp, lambda i: (0, 0)) for shp, _ in accs]
    out_shape = [jax.ShapeDtypeStruct((S, w), dt) for w, dt in outs]
    out_shape += [jax.ShapeDtypeStruct(shp, dt) for shp, dt in accs]
    res = pl.pallas_call(
        body, name=name, grid=(S // T,), in_specs=in_specs, out_specs=out_specs, out_shape=out_shape,
        compiler_params=_cp(("arbitrary",)),
    )(*args)
    return res


def _prow(g):
    return g if isinstance(g, tuple) else g.reshape(1, -1)


def _rms_fwd(name, h, g, dep=None):
    def fn(first, h_ref, g_ref, o_ref):
        x = h_ref[...]
        r = lax.rsqrt(jnp.mean(x * x, axis=-1, keepdims=True) + EPS)
        o_ref[...] = (x * r * g_ref[...]).astype(o_ref.dtype)

    return _ew(name, fn, [h], [_prow(g)], [(h.shape[1], BF)], dep=dep)[0]


def _rms_bwd(name, dxn, h, g, res):
    D = h.shape[1]

    def fn(first, d_ref, h_ref, r_ref, g_ref, o_ref, dg_ref):
        x = h_ref[...]
        d = d_ref[...].astype(F32)
        r = lax.rsqrt(jnp.mean(x * x, axis=-1, keepdims=True) + EPS)
        xhat = x * r
        dxhat = d * g_ref[...]
        dh = r * (dxhat - xhat * jnp.mean(dxhat * xhat, axis=-1, keepdims=True))
        o_ref[...] = r_ref[...] + dh

        @pl.when(first)
        def _():
            dg_ref[...] = jnp.zeros_like(dg_ref)

        dg_ref[...] += jnp.sum(d * xhat, axis=0, keepdims=True)

    return _ew(name, fn, [dxn, h, res], [_prow(g)], [(D, F32)], [((1, D), F32)])


FFN_TOKEN_TILE = 512


def _ffn_up(name, xn, wgT, wuT, dep=None):
    S, D = xn.shape
    FF = wgT.shape[0]
    tm = _pick(S, (FFN_TOKEN_TILE, 256, 128))
    tn = _pick(FF, MM_TILES)
    n_dep = 0 if dep is None else 1

    def body(x_ref, g_ref, u_ref, *rest):
        a_ref, b_ref, h_ref = rest[n_dep:]
        x = x_ref[...]
        a = _dot_nt(x, g_ref[...])
        b = _dot_nt(x, u_ref[...])
        a_ref[...] = a.astype(BF)
        b_ref[...] = b.astype(BF)
        h_ref[...] = (a * _sigmoid(a) * b).astype(BF)

    wspec = pl.BlockSpec((tn, D), lambda j, i: (j, 0))
    ospec = pl.BlockSpec((tm, tn), lambda j, i: (i, j))
    return pl.pallas_call(
        body, name=name, grid=(FF // tn, S // tm),
        in_specs=[pl.BlockSpec((tm, D), lambda j, i: (i, 0)), wspec, wspec] + ([] if dep is None else [_ANY]),
        out_specs=[ospec] * 3, out_shape=[jax.ShapeDtypeStruct((S, FF), BF)] * 3,
        compiler_params=_cp(("parallel", "arbitrary")),
    )(*([xn, wgT, wuT] + ([] if dep is None else [dep])))


def _ffn_dact(name, dh, wd, a, b, dep=None):
    S, D = dh.shape
    FF = wd.shape[0]
    tm = _pick(S, (FFN_TOKEN_TILE, 256, 128))
    tn = _pick(FF, MM_TILES)
    n_dep = 0 if dep is None else 1

    def body(d_ref, w_ref, a_ref, b_ref, *rest):
        da_ref, db_ref = rest[n_dep:]
        d = 0.5 * _dot_nt(d_ref[...].astype(BF), w_ref[...])
        av = a_ref[...].astype(F32)
        s = _sigmoid(av)
        da_ref[...] = (d * b_ref[...].astype(F32) * (s * (1.0 + av * (1.0 - s)))).astype(BF)
        db_ref[...] = (d * av * s).astype(BF)

    tspec = pl.BlockSpec((tm, tn), lambda j, i: (i, j))
    return pl.pallas_call(
        body, name=name, grid=(FF // tn, S // tm),
        in_specs=[pl.BlockSpec((tm, D), lambda j, i: (i, 0)), pl.BlockSpec((tn, D), lambda j, i: (j, 0)), tspec, tspec]
        + ([] if dep is None else [_ANY]),
        out_specs=[tspec] * 2, out_shape=[jax.ShapeDtypeStruct((S, FF), BF)] * 2,
        compiler_params=_cp(("parallel", "arbitrary")),
    )(*([dh, wd, a, b] + ([] if dep is None else [dep])))


def _merge_fwd(name, ga, gm, ya, ym):
    def fn(first, ga_ref, gm_ref, ya_ref, ym_ref, o_ref):
        o = _sigmoid(ga_ref[...].astype(F32)) * ya_ref[...].astype(F32) + _sigmoid(gm_ref[...].astype(F32)) * ym_ref[...].astype(F32)
        o_ref[...] = o.astype(o_ref.dtype)

    return _ew(name, fn, [ga, gm, ya, ym], [], [(ya.shape[1], BF)])[0]


def _merge_bwd(name, dmerged, ga, gm, ya, ym):
    W = ya.shape[1]

    def fn(first, d_ref, ga_ref, gm_ref, ya_ref, ym_ref, dga_ref, dgm_ref, dya_ref, dym_ref):
        d = d_ref[...].astype(F32)
        sa = _sigmoid(ga_ref[...].astype(F32))
        sm = _sigmoid(gm_ref[...].astype(F32))
        dga_ref[...] = (d * ya_ref[...].astype(F32) * sa * (1.0 - sa)).astype(BF)
        dgm_ref[...] = (d * ym_ref[...].astype(F32) * sm * (1.0 - sm)).astype(BF)
        dya_ref[...] = (d * sa).astype(BF)
        dym_ref[...] = (d * sm).astype(BF)

    return _ew(name, fn, [dmerged, ga, gm, ya, ym], [], [(W, BF)] * 4)


def _gnorm_fwd(name, y, z, w):
    W = y.shape[1]
    gw = W // SSM_GROUPS

    def fn(first, y_ref, z_ref, w_ref, o_ref):
        for g in range(SSM_GROUPS):
            sl = slice(g * gw, (g + 1) * gw)
            zz = z_ref[:, sl].astype(F32)
            t = y_ref[:, sl].astype(F32) * (zz * _sigmoid(zz))
            r = lax.rsqrt(jnp.mean(t * t, axis=-1, keepdims=True) + EPS)
            o_ref[:, sl] = (t * r * w_ref[:, sl]).astype(o_ref.dtype)

    return _ew(name, fn, [y, z], [_prow(w)], [(W, BF)])[0]


def _gnorm_bwd(name, dyn, y, z, w):
    W = y.shape[1]
    gw = W // SSM_GROUPS

    def fn(first, d_ref, y_ref, z_ref, w_ref, dy_ref, dz_ref, dw_ref):
        @pl.when(first)
        def _():
            dw_ref[...] = jnp.zeros_like(dw_ref)

        for g in range(SSM_GROUPS):
            sl = slice(g * gw, (g + 1) * gw)
            zz = z_ref[:, sl].astype(F32)
            yy = y_ref[:, sl].astype(F32)
            d = d_ref[:, sl].astype(F32)
            s = _sigmoid(zz)
            sz = zz * s
            t = yy * sz
            r = lax.rsqrt(jnp.mean(t * t, axis=-1, keepdims=True) + EPS)
            that = t * r
            dthat = d * w_ref[:, sl]
            dt = r * (dthat - that * jnp.mean(dthat * that, axis=-1, keepdims=True))
            dw_ref[:, sl] += jnp.sum(d * that, axis=0, keepdims=True)
            dy_ref[:, sl] = (dt * sz).astype(BF)
            dz_ref[:, sl] = (dt * yy * (s * (1.0 + zz * (1.0 - s)))).astype(BF)

    return _ew(name, fn, [dyn, y, z], [_prow(w)], [(W, BF), (W, BF)], [((1, W), F32)])


def _ple_fwd(name, h, gpre, pp):
    def fn(first, h_ref, g_ref, p_ref, o_ref):
        o_ref[...] = h_ref[...] + _sigmoid(g_ref[...].astype(F32)) * p_ref[...].astype(F32)

    return _ew(name, fn, [h, gpre, pp], [], [(h.shape[1], F32)])[0]


def _ple_bwd(name, dh, gpre, pp, dep=None):
    W = dh.shape[1]

    def fn(first, d_ref, g_ref, p_ref, dg_ref, dp_ref):
        d = d_ref[...]
        s = _sigmoid(g_ref[...].astype(F32))
        dg_ref[...] = (d * p_ref[...].astype(F32) * s * (1.0 - s)).astype(BF)
        dp_ref[...] = (d * s).astype(BF)

    return _ew(name, fn, [dh, gpre, pp], [], [(W, BF), (W, BF)], dep=dep)


def _loss_head(name, h, g, target):
    D = h.shape[1]

    def fn(first, h_ref, t_ref, g_ref, dh_ref, loss_ref, dg_ref):
        x = h_ref[...]
        r = lax.rsqrt(jnp.mean(x * x, axis=-1, keepdims=True) + EPS)
        xhat = x * r
        err = xhat * g_ref[...] - t_ref[...]
        part = 0.5 * jnp.sum(jnp.mean(err * err, axis=-1, keepdims=True), axis=0, keepdims=True)
        dy = err * (1.0 / D)
        dxhat = dy * g_ref[...]
        dh_ref[...] = r * (dxhat - xhat * jnp.mean(dxhat * xhat, axis=-1, keepdims=True))

        @pl.when(first)
        def _():
            loss_ref[...] = jnp.zeros_like(loss_ref)
            dg_ref[...] = jnp.zeros_like(dg_ref)

        loss_ref[...] += jnp.broadcast_to(part, loss_ref.shape)
        dg_ref[...] += jnp.sum(dy * xhat, axis=0, keepdims=True)

    return _ew(name, fn, [h, target], [_prow(g)], [(D, F32)], [((1, LANES), F32), ((1, D), F32)])


def _conv_specs(S, C, offs, l):
    T = _pick(S, (512, 256, 128, 64, 32, 16))
    Ct = [c for c in (512, 256, 128) if C % c == 0 and all(o % c == 0 for o in offs)][0]
    per = T // HALO
    last = S // HALO - 1

    def cur(off=0):
        return pl.BlockSpec((T, Ct), lambda j, i: (i, off // Ct + j))

    def prev(off=0):
        return pl.BlockSpec((HALO, Ct), lambda j, i: (jnp.maximum(i * per - 1, 0), off // Ct + j))

    def nxt(off=0):
        return pl.BlockSpec((HALO, Ct), lambda j, i: (jnp.minimum((i + 1) * per, last), off // Ct + j))

    wspec = pl.BlockSpec((None, 8, Ct), lambda j, i: (l, 0, j))
    return T, Ct, cur, prev, nxt, wspec


def _pad_taps(w):
    return jnp.concatenate([w.astype(F32), jnp.zeros((w.shape[0], 8 - w.shape[1], w.shape[2]), F32)], axis=1)


def _causal(cat, w_ref, K, T, lead):
    views = [cat[lead - (K - 1) + k:lead - (K - 1) + k + T] for k in range(K)]
    out = None
    for k in range(K):
        term = w_ref[k:k + 1, :] * views[k]
        out = term if out is None else out + term
    return out, views


def _anticausal(cat, w_ref, K, T):
    out = None
    for k in range(K):
        o = K - 1 - k
        term = w_ref[k:k + 1, :] * cat[o:o + T]
        out = term if out is None else out + term
    return out


def _scconv_fwd(name, proj, ob, oc, ox, taps, K, l):
    S = proj.shape[0]
    C = taps.shape[2]
    T, Ct, cur, prev, nxt, wspec = _conv_specs(S, C, (ob, oc, ox), l)

    def body(b_ref, c_ref, x_ref, cp_ref, xp_ref, w_ref, o_ref):
        i = pl.program_id(1)
        q = c_ref[...].astype(F32) * x_ref[...].astype(F32)
        qp = jnp.where(i == 0, 0.0, cp_ref[...].astype(F32) * xp_ref[...].astype(F32))
        cat = jnp.concatenate([qp, q], axis=0)
        o_ref[...] = (b_ref[...].astype(F32) * _causal(cat, w_ref, K, T, HALO)[0]).astype(o_ref.dtype)

    return pl.pallas_call(
        body, name=name, grid=(C // Ct, S // T),
        in_specs=[cur(ob), cur(oc), cur(ox), prev(oc), prev(ox), wspec], out_specs=cur(),
        out_shape=jax.ShapeDtypeStruct((S, C), BF), compiler_params=_cp(("parallel", "arbitrary")),
    )(proj, proj, proj, proj, proj, taps)


def _scconv_bwd(name, dv, proj, ob, oc, ox, taps, K, l):
    S = proj.shape[0]
    C = taps.shape[2]
    T, Ct, cur, prev, nxt, wspec = _conv_specs(S, C, (ob, oc, ox), l)
    n_t = S // T

    def body(d_ref, b_ref, c_ref, x_ref, dn_ref, bn_ref, cp_ref, xp_ref, w_ref, db_ref, dc_ref, dx_ref, dw_ref):
        i = pl.program_id(1)
        c = c_ref[...].astype(F32)
        x = x_ref[...].astype(F32)
        d = d_ref[...].astype(F32)
        q = c * x
        qp = jnp.where(i == 0, 0.0, cp_ref[...].astype(F32) * xp_ref[...].astype(F32))
        catq = jnp.concatenate([qp, q], axis=0)
        cv, q_views = _causal(catq, w_ref, K, T, HALO)
        db_ref[...] = (d * cv).astype(BF)
        dcv = d * b_ref[...].astype(F32)
        dcvn = jnp.where(i == n_t - 1, 0.0, dn_ref[...].astype(F32) * bn_ref[...].astype(F32))
        catd = jnp.concatenate([dcv, dcvn], axis=0)
        dq = _anticausal(catd, w_ref, K, T)
        dc_ref[...] = (dq * x).astype(BF)
        dx_ref[...] = (dq * c).astype(BF)

        @pl.when(i == 0)
        def _():
            dw_ref[...] = jnp.zeros_like(dw_ref)

        for k in range(K):
            dw_ref[k:k + 1, :] += jnp.sum(dcv * q_views[k], axis=0, keepdims=True)

    return pl.pallas_call(
        body, name=name, grid=(C // Ct, n_t),
        in_specs=[cur(), cur(ob), cur(oc), cur(ox), nxt(), nxt(ob), prev(oc), prev(ox), wspec],
        out_specs=[cur(), cur(), cur(), pl.BlockSpec((8, Ct), lambda j, i: (0, j))],
        out_shape=[jax.ShapeDtypeStruct((S, C), BF)] * 3 + [jax.ShapeDtypeStruct((8, C), F32)],
        compiler_params=_cp(("parallel", "arbitrary")),
    )(dv, proj, proj, proj, dv, proj, proj, proj, taps)


def _mconv_fwd(name, proj, ox, taps, K, bias, l):
    S = proj.shape[0]
    C = taps.shape[2]
    T, Ct, cur, prev, nxt, wspec = _conv_specs(S, C, (ox,), l)
    bspec = pl.BlockSpec((None, 1, Ct), lambda j, i: (l, 0, j))

    def body(x_ref, xp_ref, w_ref, b_ref, o_ref):
        i = pl.program_id(1)
        xp = jnp.where(i == 0, 0.0, xp_ref[...].astype(F32))
        cat = jnp.concatenate([xp, x_ref[...].astype(F32)], axis=0)
        pre = _causal(cat, w_ref, K, T, HALO)[0] + b_ref[...]
        o_ref[...] = (pre * _sigmoid(pre)).astype(o_ref.dtype)

    return pl.pallas_call(
        body, name=name, grid=(C // Ct, S // T),
        in_specs=[cur(ox), prev(ox), wspec, bspec], out_specs=cur(),
        out_shape=jax.ShapeDtypeStruct((S, C), BF), compiler_params=_cp(("parallel", "arbitrary")),
    )(proj, proj, taps, bias)


def _mconv_bwd(name, dout, proj, ox, taps, K, bias, l):
    S = proj.shape[0]
    C = taps.shape[2]
    T, Ct, cur, prev, nxt, wspec = _conv_specs(S, C, (ox,), l)
    n_t = S // T
    bspec = pl.BlockSpec((None, 1, Ct), lambda j, i: (l, 0, j))

    def body(d_ref, dn_ref, x_ref, xp_ref, xn_ref, w_ref, b_ref, dx_ref, dw_ref, db_ref):
        i = pl.program_id(1)
        xp = jnp.where(i == 0, 0.0, xp_ref[...].astype(F32))
        cat3 = jnp.concatenate([xp, x_ref[...].astype(F32), xn_ref[...].astype(F32)], axis=0)
        pre, x_views = _causal(cat3, w_ref, K, T + HALO, HALO)
        pre = pre + b_ref[...]
        dn = jnp.where(i == n_t - 1, 0.0, dn_ref[...].astype(F32))
        dext = jnp.concatenate([d_ref[...].astype(F32), dn], axis=0)
        s = _sigmoid(pre)
        dpre = dext * (s * (1.0 + pre * (1.0 - s)))
        dx_ref[...] = _anticausal(dpre, w_ref, K, T).astype(BF)
        dcur = dpre[:T]

        @pl.when(i == 0)
        def _():
            dw_ref[...] = jnp.zeros_like(dw_ref)
            db_ref[...] = jnp.zeros_like(db_ref)

        db_ref[...] += jnp.sum(dcur, axis=0, keepdims=True)
        for k in range(K):
            dw_ref[k:k + 1, :] += jnp.sum(dcur * x_views[k][:T], axis=0, keepdims=True)

    return pl.pallas_call(
        body, name=name, grid=(C // Ct, n_t),
        in_specs=[cur(), nxt(), cur(ox), prev(ox), nxt(ox), wspec, bspec],
        out_specs=[cur(), pl.BlockSpec((8, Ct), lambda j, i: (0, j)), pl.BlockSpec((1, Ct), lambda j, i: (0, j))],
        out_shape=[jax.ShapeDtypeStruct((S, C), BF), jax.ShapeDtypeStruct((8, C), F32), jax.ShapeDtypeStruct((1, C), F32)],
        compiler_params=_cp(("parallel", "arbitrary")),
    )(dout, dout, proj, proj, proj, taps, bias)


def _tri_matmul(tri_bf, v):
    hi = v.astype(BF)
    r1 = v - hi.astype(F32)
    mid = r1.astype(BF)
    lo = (r1 - mid.astype(F32)).astype(BF)
    dot = functools.partial(jnp.dot, preferred_element_type=F32)
    return dot(tri_bf, hi) + dot(tri_bf, mid) + dot(tri_bf, lo)


def _dot_nt(a, b):
    return lax.dot_general(a, b, (((1,), (1,)), ((), ())), preferred_element_type=F32)


def _dot_tn(a, b):
    return lax.dot_general(a, b, (((0,), (0,)), ((), ())), preferred_element_type=F32)


def _dot_nn(a, b):
    return jnp.dot(a, b, preferred_element_type=F32)


def _ssd_chunk_scalars(dtr_ref, par_ref, L):
    row_i = lax.broadcasted_iota(jnp.int32, (L, L), 0)
    col_i = lax.broadcasted_iota(jnp.int32, (L, L), 1)
    tri = row_i >= col_i
    pre = dtr_ref[...] + par_ref[0:1, :]
    dt_all = _softplus(pre)
    A_row = -jnp.exp(par_ref[1:2, :])
    a_all = dt_all * A_row
    acum_all = _tri_matmul(tri.astype(BF), a_all)
    return tri, pre, dt_all, A_row, a_all, acum_all, acum_all.T


def _ssd_dims(xbc, heads):
    S, conv_dim = xbc.shape
    inner = heads * SSM_HEADDIM
    N = (conv_dim - inner) // (2 * SSM_GROUPS)
    gw = inner // SSM_GROUPS
    PP = gw // LANES
    L = min(SSM_CHUNK, S)
    assert N == LANES and gw % LANES == 0 and inner % (SSM_GROUPS * N) == 0 and S % L == 0
    return S, inner, N, gw, PP, L, S // L


def _ssd_params(dt_bias, A_log, Dp):
    depth, H = dt_bias.shape
    rows = jnp.stack([dt_bias, A_log, Dp], axis=1).astype(F32)
    rows = jnp.concatenate([rows, jnp.zeros((depth, 3, LANES - H), F32)], axis=2)
    return jnp.concatenate([rows, jnp.zeros((depth, 5, LANES), F32)], axis=1)


def _ssd_fwd(name, xbc, dt_raw, par, l, heads):
    S, inner, N, gw, PP, L, nc = _ssd_dims(xbc, heads)
    G = SSM_GROUPS

    def body(x_ref, b_ref, c_ref, dtr_ref, par_ref, y_ref, st_out_ref, st_ref):
        @pl.when(pl.program_id(0) == 0)
        def _():
            st_ref[...] = jnp.zeros_like(st_ref)

        tri, pre, dt_all, A_row, a_all, acum_all, acumT = _ssd_chunk_scalars(dtr_ref, par_ref, L)
        lane = lax.broadcasted_iota(jnp.int32, (L, LANES), 1)
        lane1 = lax.broadcasted_iota(jnp.int32, (1, LANES), 1)
        lo = lane < SSM_HEADDIM
        lo1 = lane1 < SSM_HEADDIM
        for g in range(G):
            Bb = b_ref[:, g * N:(g + 1) * N]
            Cb = c_ref[:, g * N:(g + 1) * N]
            BbT = Bb.astype(F32).T.astype(BF)
            Gm = _dot_nt(Cb, Bb)
            for j in range(PP):
                pj = g * PP + j
                h0, h1 = 2 * pj, 2 * pj + 1
                cols = slice(pj * LANES, (pj + 1) * LANES)
                x = x_ref[:, cols].astype(F32)
                dt_l = jnp.where(lo, dt_all[:, h0:h0 + 1], dt_all[:, h1:h1 + 1])
                ac0 = acum_all[:, h0:h0 + 1]
                ac1 = acum_all[:, h1:h1 + 1]
                ac_l = jnp.where(lo, ac0, ac1)
                E0 = jnp.exp(jnp.where(tri, ac0 - acumT[h0:h0 + 1, :], -1e30))
                E1 = jnp.exp(jnp.where(tri, ac1 - acumT[h1:h1 + 1, :], -1e30))
                xd = x * dt_l
                xdb = xd.astype(BF)
                yd = jnp.where(lo, _dot_nn((Gm * E0).astype(BF), xdb), _dot_nn((Gm * E1).astype(BF), xdb))
                prevT = st_ref[pj]
                st_out_ref[0, pj] = prevT
                P = _dot_nn(Cb, prevT.astype(BF))
                D_l = jnp.where(lo1, par_ref[2:3, h0:h0 + 1], par_ref[2:3, h1:h1 + 1])
                y_ref[:, cols] = (yd + P * jnp.exp(ac_l) + D_l * x).astype(y_ref.dtype)
                al0 = ac0[L - 1:L, :]
                al1 = ac1[L - 1:L, :]
                Wm = xd * jnp.exp(jnp.where(lo, al0, al1) - ac_l)
                eal = jnp.where(lo1, jnp.exp(al0), jnp.exp(al1))
                st_ref[pj] = eal * prevT + _dot_nn(BbT, Wm.astype(BF))

    gn = G * N
    return pl.pallas_call(
        body, name=name, grid=(nc,),
        in_specs=[pl.BlockSpec((L, inner), lambda c: (c, 0)), pl.BlockSpec((L, gn), lambda c: (c, inner // gn)),
                  pl.BlockSpec((L, gn), lambda c: (c, inner // gn + 1)),
                  pl.BlockSpec((L, LANES), lambda c: (c, 0)), pl.BlockSpec((None, 8, LANES), lambda c: (l, 0, 0))],
        out_specs=[pl.BlockSpec((L, inner), lambda c: (c, 0)), pl.BlockSpec((1, G * PP, N, LANES), lambda c: (c, 0, 0, 0))],
        out_shape=[jax.ShapeDtypeStruct((S, inner), BF), jax.ShapeDtypeStruct((nc, G * PP, N, LANES), F32)],
        scratch_shapes=[pltpu.VMEM((G * PP, N, LANES), F32)],
        compiler_params=_cp(("arbitrary",)),
    )(xbc, xbc, xbc, dt_raw, par)


def _ssd_bwd(name, dy, xbc, dt_raw, states, par, l, heads):
    S, inner, N, gw, PP, L, nc = _ssd_dims(xbc, heads)
    G = SSM_GROUPS

    def body(dy_ref, x_ref, b_ref, c_ref, dtr_ref, par_ref, st_in_ref, d_ref, ddt_ref, dpar_ref, dst_ref):
        @pl.when(pl.program_id(0) == 0)
        def _():
            dst_ref[...] = jnp.zeros_like(dst_ref)
            dpar_ref[...] = jnp.zeros_like(dpar_ref)

        tri, pre, dt_all, A_row, a_all, acum_all, acumT = _ssd_chunk_scalars(dtr_ref, par_ref, L)
        lane = lax.broadcasted_iota(jnp.int32, (L, LANES), 1)
        lane1 = lax.broadcasted_iota(jnp.int32, (1, LANES), 1)
        rowl = lax.broadcasted_iota(jnp.int32, (L, LANES), 0)
        lo = lane < SSM_HEADDIM
        lo1 = lane1 < SSM_HEADDIM
        triT = lax.broadcasted_iota(jnp.int32, (L, L), 0) <= lax.broadcasted_iota(jnp.int32, (L, L), 1)
        sel_r = lax.broadcasted_iota(jnp.int32, (3 * LANES, LANES), 0)
        sel_c = lax.broadcasted_iota(jnp.int32, (3 * LANES, LANES), 1)
        dac_all = jnp.zeros((L, LANES), F32)
        xds_all = jnp.zeros((L, LANES), F32)
        dD_row = jnp.zeros((1, LANES), F32)

        def half_sums(v):
            return (jnp.sum(jnp.where(lo1, v, 0.0), axis=1, keepdims=True), jnp.sum(jnp.where(lo1, 0.0, v), axis=1, keepdims=True))

        def dot2(v, sel):
            hi = v.astype(BF)
            return _dot_nn(hi, sel) + _dot_nn((v - hi.astype(F32)).astype(BF), sel)

        for pj in range(G * PP):
            g, j = divmod(pj, PP)
            if j == 0:
                Bb = b_ref[:, g * N:(g + 1) * N]
                Cb = c_ref[:, g * N:(g + 1) * N]
                CbT = Cb.astype(F32).T.astype(BF)
                Gm = _dot_nt(Cb, Bb)
                GmT = _dot_nt(Bb, Cb)
                dG = jnp.zeros((L, L), F32)
                dGT = jnp.zeros((L, L), F32)
                dBacc = jnp.zeros((L, N), F32)
                dCacc = jnp.zeros((L, N), F32)
            h0, h1 = 2 * pj, 2 * pj + 1
            to_h0 = (sel_r < LANES) | ((sel_r >= 2 * LANES) & (sel_r < 2 * LANES + SSM_HEADDIM))
            sel3 = jnp.where(sel_c == jnp.where(to_h0, h0, h1), 1.0, 0.0).astype(BF)
            sel1 = sel3[2 * LANES:]
            sl = slice(pj * LANES, (pj + 1) * LANES)
            x = x_ref[:, sl].astype(F32)
            dyv = dy_ref[:, sl].astype(F32)
            dt_l = jnp.where(lo, dt_all[:, h0:h0 + 1], dt_all[:, h1:h1 + 1])
            ac0 = acum_all[:, h0:h0 + 1]
            ac1 = acum_all[:, h1:h1 + 1]
            r0 = acumT[h0:h0 + 1, :]
            r1 = acumT[h1:h1 + 1, :]
            ac_l = jnp.where(lo, ac0, ac1)
            E0 = jnp.exp(jnp.where(tri, ac0 - r0, -1e30))
            E1 = jnp.exp(jnp.where(tri, ac1 - r1, -1e30))
            E0T = jnp.exp(jnp.where(triT, r0 - ac0, -1e30))
            E1T = jnp.exp(jnp.where(triT, r1 - ac1, -1e30))
            xd = x * dt_l
            xdb = xd.astype(BF)
            M0 = Gm * E0
            M1 = Gm * E1
            ea_l = jnp.exp(ac_l)
            al0 = ac0[L - 1:L, :]
            al1 = ac1[L - 1:L, :]
            dte_l = jnp.exp(jnp.where(lo, al0, al1) - ac_l)
            Wm = xd * dte_l
            prevT = st_in_ref[0, pj]
            prevTb = prevT.astype(BF)
            P = _dot_nn(Cb, prevTb)
            D_l = jnp.where(lo1, par_ref[2:3, h0:h0 + 1], par_ref[2:3, h1:h1 + 1])
            dx = D_l * dyv
            dD0, dD1 = half_sums(jnp.sum(dyv * x, axis=0, keepdims=True))
            dyb = dyv.astype(BF)
            dy0b = jnp.where(lo, dyv, 0.0).astype(BF)
            dy1b = jnp.where(lo, 0.0, dyv).astype(BF)
            dM0 = _dot_nt(dy0b, xdb)
            dM1 = _dot_nt(dy1b, xdb)
            dM0T = _dot_nt(xdb, dy0b)
            dM1T = _dot_nt(xdb, dy1b)
            M0T = GmT * E0T
            M1T = GmT * E1T
            dxd = jnp.where(lo, _dot_nn(M0T.astype(BF), dyb), _dot_nn(M1T.astype(BF), dyb))
            dG = dG + dM0 * E0 + dM1 * E1
            dGT = dGT + dM0T * E0T + dM1T * E1T
            z0 = dM0 * M0 - dM0T * M0T
            z1 = dM1 * M1 - dM1T * M1T
            dP = dyv * ea_l
            dPb = dP.astype(BF)
            dCacc = dCacc + _dot_nt(dPb, prevTb)
            dprevT = _dot_nn(CbT, dPb)
            dnewT = dst_ref[pj]
            dnewTb = dnewT.astype(BF)
            e0 = jnp.exp(al0)
            e1 = jnp.exp(al1)
            dprevT = dprevT + jnp.where(lo1, e0, e1) * dnewT
            u0, u1 = half_sums(jnp.sum(dnewT * prevT, axis=0, keepdims=True))
            dW = _dot_nn(Bb, dnewTb)
            dBacc = dBacc + _dot_nt(Wm.astype(BF), dnewTb)
            dxd = dxd + dW * dte_l
            tt = dW * Wm
            t0, t1 = half_sums(jnp.sum(tt, axis=0, keepdims=True))
            dal0 = u0 * e0 + t0
            dal1 = u1 * e1 + t1
            dac_all = dac_all + dot2(jnp.concatenate([z0, z1, dP * P - tt], axis=1), sel3)
            dac_all = dac_all + jnp.where(rowl == L - 1, jnp.where(lane == h0, dal0, 0.0) + jnp.where(lane == h1, dal1, 0.0), 0.0)
            dx = dx + dxd * dt_l
            xds_all = xds_all + dot2(dxd * x, sel1)
            dst_ref[pj] = dprevT
            d_ref[:, sl] = dx.astype(d_ref.dtype)
            dD_row = dD_row + jnp.where(lane1 == h0, dD0, 0.0) + jnp.where(lane1 == h1, dD1, 0.0)
            if j == PP - 1:
                d_ref[:, inner + g * N:inner + (g + 1) * N] = (dBacc + _dot_nn(dGT.astype(BF), Cb)).astype(d_ref.dtype)
                d_ref[:, inner + (G + g) * N:inner + (G + g + 1) * N] = (dCacc + _dot_nn(dG.astype(BF), Bb)).astype(d_ref.dtype)

        row_i = lax.broadcasted_iota(jnp.int32, (L, L), 0)
        col_i = lax.broadcasted_iota(jnp.int32, (L, L), 1)
        da_all = _tri_matmul((row_i <= col_i).astype(BF), dac_all)
        real = lane < heads
        ddt_all = da_all * A_row + xds_all
        draw = jnp.where(real, ddt_all * _sigmoid(pre), 0.0)
        ddt_ref[...] = draw
        dpar_ref[0:1, :] += jnp.sum(draw, axis=0, keepdims=True)
        dpar_ref[1:2, :] += jnp.sum(jnp.where(real, da_all * a_all, 0.0), axis=0, keepdims=True)
        dpar_ref[2:3, :] += dD_row

    gn = G * N
    conv_dim = xbc.shape[1]
    rev = lambda c: nc - 1 - c
    return pl.pallas_call(
        body, name=name, grid=(nc,),
        in_specs=[pl.BlockSpec((L, inner), lambda c: (rev(c), 0)), pl.BlockSpec((L, inner), lambda c: (rev(c), 0)),
                  pl.BlockSpec((L, gn), lambda c: (rev(c), inner // gn)), pl.BlockSpec((L, gn), lambda c: (rev(c), inner // gn + 1)),
                  pl.BlockSpec((L, LANES), lambda c: (rev(c), 0)), pl.BlockSpec((None, 8, LANES), lambda c: (l, 0, 0)),
                  pl.BlockSpec((1, G * PP, N, LANES), lambda c: (rev(c), 0, 0, 0))],
        out_specs=[pl.BlockSpec((L, conv_dim), lambda c: (rev(c), 0)), pl.BlockSpec((L, LANES), lambda c: (rev(c), 0)),
                   pl.BlockSpec((8, LANES), lambda c: (0, 0))],
        out_shape=[jax.ShapeDtypeStruct((S, conv_dim), BF), jax.ShapeDtypeStruct((S, LANES), F32), jax.ShapeDtypeStruct((8, LANES), F32)],
        scratch_shapes=[pltpu.VMEM((G * PP, N, LANES), F32)],
        compiler_params=_cp(("arbitrary",)),
    )(dy, xbc, xbc, xbc, dt_raw, par, states)


def _adamw(g, w, m, v):
    m2 = ADAM_B1 * m + (1.0 - ADAM_B1) * g
    v2 = ADAM_B2 * v + (1.0 - ADAM_B2) * (g * g)
    m_hat = m2 / (1.0 - ADAM_B1 ** ADAM_STEP)
    v_hat = v2 / (1.0 - ADAM_B2 ** ADAM_STEP)
    delta = -ADAM_LR * (m_hat / (jnp.sqrt(v_hat) + ADAM_EPS) + ADAM_WD * w)
    return delta, m2, v2


def _flat_tile(R):
    return _pick(R, (FLAT_ROW_TILE, 1024, 512, 256, 128, 64, 32, 16, 8))


def _sum_adam(name, lands, off, w, m, v):
    depth, r, c = w.shape
    cap = max(16, (4 * 1024 * 1024) // (N_DEV * c * 2))
    row_tiles = [t for t in (512, 256, 128, 64, 32, 16) if r % t == 0 and off % t == 0 and t <= cap]
    if row_tiles:
        tr, tc = row_tiles[0], c
        ob = off // tr
        n_t = r // tr
        spec = pl.BlockSpec((None, tr, c), lambda l, t: (l, t, 0))
        land_specs = [pl.BlockSpec((N_DEV, tr, c), lambda l, t, i=i: (0, jnp.where(l == i, ob + t, ob), 0)) for i in range(depth)]
    else:
        assert off == 0 and lands[0].shape[1] == r and c % LANES == 0
        tc = LANES
        n_t = c // tc
        spec = pl.BlockSpec((None, r, tc), lambda l, t: (l, 0, t))
        land_specs = [pl.BlockSpec((N_DEV, r, tc), lambda l, t, i=i: (0, 0, jnp.where(l == i, t, 0))) for i in range(depth)]

    def body(*refs):
        land_refs = refs[:depth]
        w_ref, m_ref, v_ref, g_ref, d_ref, m2_ref, v2_ref = refs[depth:]
        l = pl.program_id(0)
        for i in range(depth):
            @pl.when(l == i)
            def _(i=i):
                g = land_refs[i][0].astype(F32)
                for k in range(1, N_DEV):
                    g = g + land_refs[i][k].astype(F32)
                g_ref[...] = g
                d_ref[...], m2_ref[...], v2_ref[...] = _adamw(g, w_ref[...], m_ref[...], v_ref[...])

    return pl.pallas_call(
        body, name=name, grid=(depth, n_t),
        in_specs=land_specs + [spec, spec, spec],
        out_specs=[spec] * 4, out_shape=[jax.ShapeDtypeStruct((depth, r, c), F32)] * 4,
        compiler_params=_cp(("arbitrary", "arbitrary")),
    )(*lands, w, m, v)


def _sum8(name, parts):
    R = parts.shape[1]
    TR = _flat_tile(R)

    def body(p_ref, g_ref):
        g = p_ref[0]
        for k in range(1, N_DEV):
            g = g + p_ref[k]
        g_ref[...] = g

    return pl.pallas_call(
        body, name=name, grid=(R // TR,),
        in_specs=[pl.BlockSpec((N_DEV, TR, LANES), lambda i: (0, i, 0))],
        out_specs=pl.BlockSpec((TR, LANES), lambda i: (i, 0)), out_shape=jax.ShapeDtypeStruct((R, LANES), F32),
        compiler_params=_cp(("parallel",)),
    )(parts)


def _adam_flat(name, g, w, m, v):
    R = w.shape[0]
    TR = _flat_tile(R)

    def body(g_ref, w_ref, m_ref, v_ref, d_ref, m2_ref, v2_ref):
        d_ref[...], m2_ref[...], v2_ref[...] = _adamw(g_ref[...], w_ref[...], m_ref[...], v_ref[...])

    spec = pl.BlockSpec((TR, LANES), lambda i: (i, 0))
    return pl.pallas_call(
        body, name=name, grid=(R // TR,), in_specs=[spec] * 4, out_specs=[spec] * 3,
        out_shape=[jax.ShapeDtypeStruct((R, LANES), F32)] * 3, compiler_params=_cp(("parallel",)),
    )(g, w, m, v)


PART_ROWS = 16


def _nrows(shape):
    n = 1
    for s in shape:
        n *= s
    r = -(-n // LANES)
    return -(-r // PART_ROWS) * PART_ROWS


def _as_rows(a):
    n = a.size
    r = _nrows(a.shape)
    f = a.reshape(-1)
    if r * LANES != n:
        f = jnp.concatenate([f, jnp.zeros((r * LANES - n,), a.dtype)])
    return f.reshape(r, LANES)


def _pack(arrs, mult=PART_ROWS):
    cat = jnp.concatenate([_as_rows(a) for a in arrs], axis=0)
    pad = (-cat.shape[0]) % mult
    if pad:
        cat = jnp.concatenate([cat, jnp.zeros((pad, LANES), cat.dtype)], axis=0)
    return cat


def _unpack(flat, shapes):
    lead = flat.shape[:-2]
    out = []
    o = 0
    for shp in shapes:
        n = 1
        for s in shp:
            n *= s
        r = _nrows(shp)
        blk = flat[..., o:o + r, :].reshape(lead + (r * LANES,))
        out.append(blk[..., :n].reshape(lead + tuple(shp)))
        o += r
    return out


def _full_from_shards(st):
    return st.reshape(st.shape[0] * st.shape[1], st.shape[2])


def _shards_from_full(full):
    return full.reshape(N_DEV, full.shape[0] // N_DEV, full.shape[1])


def _ffn_fwd(tag, h, g, wgT, wuT, wd, dep=None):
    xn = _rms_fwd(tag + "_rms", h, g, dep=dep)
    a, b, hmid = _ffn_up(tag + "_up", xn, wgT, wuT)
    hout = _mm(tag + "_down", hmid, wd, out_dtype=F32, res=h, alpha=0.5)
    return hout, (xn, a, b, hmid)


def _ffn_bwd(tag, dh_out, h, g, wgT, wuT, wd, saved, dep=None):
    xn, a, b, hmid = saved
    da, db = _ffn_dact(tag + "_d_act", dh_out, wd, a, b, dep=dep)
    d_wd = _mm(tag + "_d_wd", hmid, dh_out, ta=True, alpha=0.5)
    d_wgT = _mm(tag + "_d_wg", da, xn, ta=True)
    d_wuT = _mm(tag + "_d_wu", db, xn, ta=True)
    dxn = _mm(tag + "_d_xn_g", da, wgT, out_dtype=F32)
    dh, dg = _mm_rms_bwd(tag + "_d_xn_u", db, wuT, dxn, h, g, dh_out)
    return dh, dg, d_wgT, d_wuT, d_wd


SEG_NAMES = ['scb', 'scc', 'scx', 'z', 'xbc', 'dt', 'ga', 'gm']
PERM = ['z', 'scb', 'scc', 'scx', 'ga', 'gm', 'xbc']


def _seg_layout(dims):
    D, inner, conv_dim, H = dims[:4]
    widths = dict(zip(SEG_NAMES, [D, D, D, inner, conv_dim, H, D, D]))
    offs, o = {}, 0
    for n in SEG_NAMES:
        offs[n] = (o, widths[n])
        o += widths[n]
    poffs, o = {}, 0
    for n in PERM:
        poffs[n] = (o, widths[n])
        o += widths[n]
    return offs, poffs


def _perm_w_in(w_inT, dims):
    offs, _ = _seg_layout(dims)
    wp = jnp.concatenate([w_inT[offs[n][0]:offs[n][0] + offs[n][1]] for n in PERM], axis=0)
    o, w = offs['dt']
    wdt = jnp.concatenate([w_inT[o:o + w], jnp.zeros((LANES - w, w_inT.shape[1]), w_inT.dtype)], axis=0)
    return wp, wdt


def _unperm_d_w_in(d_wp, d_wdt, dims):
    offs, poffs = _seg_layout(dims)
    H = dims[3]
    return jnp.concatenate([d_wdt[:H] if n == 'dt' else d_wp[poffs[n][0]:poffs[n][0] + poffs[n][1]] for n in SEG_NAMES], axis=0)


def _mixer_fwd(h, W, dims, dep=None):
    H, Ksc, Km = dims[3:]
    l = W['l']
    _, poffs = _seg_layout(dims)

    def seg(n):
        o, w = poffs[n]
        assert o % w == 0
        return (proj, w, o // w)

    u = _rms_fwd("mix_rms", h, W['mix_norm'], dep=dep)
    proj = _mm("inproj", u, W['w_in_p'], tb=True)
    dt_raw = _mm("inproj_dt", u, W['w_dt'], tb=True, out_dtype=F32)
    v = _scconv_fwd("scconv_f", proj, poffs['scb'][0], poffs['scc'][0], poffs['scx'][0], W['sc_taps'], Ksc, l)
    ya = _mm("sc_out", v, W['sc_w_out'])
    xbc = _mconv_fwd("mconv_f", proj, poffs['xbc'][0], W['m_taps'], Km, W['m_conv_b'], l)
    y, states = _ssd_fwd("ssd_f", xbc, dt_raw, W['ssd_par'], l, H)
    yn = _gnorm_fwd("gnorm_f", y, seg('z'), W['m_norm'])
    ym = _mm("m_out", yn, W['m_w_out'])
    merged = _merge_fwd("merge_f", seg('ga'), seg('gm'), ya, ym)
    hout = _mm("w_o", merged, W['w_o'], out_dtype=F32, res=h)
    return hout, (u, proj, dt_raw, v, ya, xbc, y, states, yn, ym, merged)


def _mixer_bwd(dh_out, h, W, dims, saved, dep=None):
    u, proj, dt_raw, v, ya, xbc, y, states, yn, ym, merged = saved
    H, Ksc, Km = dims[3:]
    l = W['l']
    _, poffs = _seg_layout(dims)

    def seg(n):
        o, w = poffs[n]
        return (proj, w, o // w)

    g = {}
    dmerged = _mm("d_merged", dh_out, W['w_o'], tb=True, dep=dep)
    g['w_o'] = _mm("d_w_o", merged, dh_out, ta=True)
    dga, dgm, dya, dym = _merge_bwd("merge_b", dmerged, seg('ga'), seg('gm'), ya, ym)
    g['sc_w_out'] = _mm("d_sc_w_out", v, dya, ta=True)
    dv = _mm("d_v", dya, W['sc_w_out'], tb=True)
    g['m_w_out'] = _mm("d_m_w_out", yn, dym, ta=True)
    dyn = _mm("d_yn", dym, W['m_w_out'], tb=True)
    dy, dz, d_mnorm = _gnorm_bwd("gnorm_b", dyn, y, seg('z'), W['m_norm'])
    g['m_norm'] = d_mnorm.reshape(-1)
    dxbc_post, ddt, dpar = _ssd_bwd("ssd_b", dy, xbc, dt_raw, states, W['ssd_par'], l, H)
    g['m_dt_bias'] = dpar[0, :H]
    g['m_A_log'] = dpar[1, :H]
    g['m_D'] = dpar[2, :H]
    dxbc, d_mcw, d_mcb = _mconv_bwd("mconv_b", dxbc_post, proj, poffs['xbc'][0], W['m_taps'], Km, W['m_conv_b'], l)
    g['m_conv_w'] = d_mcw[:Km]
    g['m_conv_b'] = d_mcb.reshape(-1)
    dscb, dscc, dscx, d_scw = _scconv_bwd("scconv_b", dv, proj, poffs['scb'][0], poffs['scc'][0], poffs['scx'][0],
                                          W['sc_taps'], Ksc, l)
    g['sc_conv_w'] = d_scw[:Ksc]
    dproj = jnp.concatenate([dz, dscb, dscc, dscx, dga, dgm, dxbc], axis=1)
    du = _mm("d_u_main", dproj, W['w_in_p'], out_dtype=F32)
    dh, dg = _mm_rms_bwd("d_u_dt", ddt, W['w_dt'], du, h, W['mix_norm'], dh_out)
    d_wp = _mm("d_w_in_main", dproj, u, ta=True)
    d_wdt = _mm("d_w_in_dt", ddt, u, ta=True)
    g['w_in'] = _unperm_d_w_in(d_wp, d_wdt, dims)
    g['mix_norm'] = dg.reshape(-1)
    return dh, g


def _ple_layer_fwd(h, p_l, W):
    xn = _rms_fwd("ple_rms", h, W['ple_norm'])
    gpre = _mm("ple_gate", xn, W['ple_w_gate'])
    pp = _mm("ple_proj", p_l, W['ple_w_proj'], tb=True)
    hout = _ple_fwd("ple_f", h, gpre, pp)
    return hout, (xn, gpre, pp)


def _ple_layer_bwd(dh_out, h, p_l, W, saved, dep=None):
    xn, gpre, pp = saved
    g = {}
    dgpre, dpp = _ple_bwd("ple_b", dh_out, gpre, pp, dep=dep)
    g['ple_w_proj'] = _mm("d_ple_proj", dpp, p_l, ta=True)
    g['ple_w_gate'] = _mm("d_ple_gate", xn, dgpre, ta=True)
    dh, dg = _mm_rms_bwd("d_ple_xn", dgpre, W['ple_w_gate'], None, h, W['ple_norm'], dh_out, tb=True)
    g['ple_norm'] = dg.reshape(-1)
    return dh, g


def kernel(x, p, ffn1_norm, ffn1_wg, ffn1_wu, ffn1_wd, mix_norm, w_in, sc_conv_w, sc_w_out, m_conv_w, m_conv_b, m_dt_bias, m_A_log, m_D, m_norm, m_w_out, w_o, ffn2_norm, ffn2_wg, ffn2_wu, ffn2_wd, ple_norm, ple_w_gate, ple_w_proj, final_norm, loss_target, m_ffn1_norm, m_ffn1_wg, m_ffn1_wu, m_ffn1_wd, m_mix_norm, m_w_in, m_sc_conv_w, m_sc_w_out, m_m_conv_w, m_m_conv_b, m_m_dt_bias, m_m_A_log, m_m_D, m_m_norm, m_m_w_out, m_w_o, m_ffn2_norm, m_ffn2_wg, m_ffn2_wu, m_ffn2_wd, m_ple_norm, m_ple_w_gate, m_ple_w_proj, m_final_norm, v_ffn1_norm, v_ffn1_wg, v_ffn1_wu, v_ffn1_wd, v_mix_norm, v_w_in, v_sc_conv_w, v_sc_w_out, v_m_conv_w, v_m_conv_b, v_m_dt_bias, v_m_A_log, v_m_D, v_m_norm, v_m_w_out, v_w_o, v_ffn2_norm, v_ffn2_wg, v_ffn2_wu, v_ffn2_wd, v_ple_norm, v_ple_w_gate, v_ple_w_proj, v_final_norm):
    args = (x, p, ffn1_norm, ffn1_wg, ffn1_wu, ffn1_wd, mix_norm, w_in, sc_conv_w, sc_w_out, m_conv_w, m_conv_b, m_dt_bias, m_A_log, m_D, m_norm, m_w_out, w_o, ffn2_norm, ffn2_wg, ffn2_wu, ffn2_wd, ple_norm, ple_w_gate, ple_w_proj, final_norm, loss_target, m_ffn1_norm, m_ffn1_wg, m_ffn1_wu, m_ffn1_wd, m_mix_norm, m_w_in, m_sc_conv_w, m_sc_w_out, m_m_conv_w, m_m_conv_b, m_m_dt_bias, m_m_A_log, m_m_D, m_m_norm, m_m_w_out, m_w_o, m_ffn2_norm, m_ffn2_wg, m_ffn2_wu, m_ffn2_wd, m_ple_norm, m_ple_w_gate, m_ple_w_proj, m_final_norm, v_ffn1_norm, v_ffn1_wg, v_ffn1_wu, v_ffn1_wd, v_mix_norm, v_w_in, v_sc_conv_w, v_sc_w_out, v_m_conv_w, v_m_conv_b, v_m_dt_bias, v_m_A_log, v_m_D, v_m_norm, v_m_w_out, v_w_o, v_ffn2_norm, v_ffn2_wg, v_ffn2_wu, v_ffn2_wd, v_ple_norm, v_ple_w_gate, v_ple_w_proj, v_final_norm)
    names = ARG_NAMES + ['m_' + n for n in WEIGHTS] + ['v_' + n for n in WEIGHTS]
    A = dict(zip(names, args))
    depth = ffn1_norm.shape[0]
    me = 4 * lax.axis_index("x") + 2 * lax.axis_index("y") + lax.axis_index("c")

    dims = (x.shape[-1], m_norm.shape[1], m_conv_b.shape[1], m_dt_bias.shape[1], sc_conv_w.shape[1], m_conv_w.shape[1])
    kind = dict(BIG)

    def work(n, prefix=''):
        return jnp.swapaxes(A[prefix + n], 1, 2) if kind[n] == 'col' else A[prefix + n]

    wb = {n: work(n).astype(BF) for n, _ in BIG}
    srcs = [[wb[ms[0]] if len(ms) == 1 else jnp.concatenate([wb[n] for n in ms], axis=1) for ms in stage] for stage in STAGES]
    conv_g = _unpack(_exchange("gather_conv_taps", _pack([A[n] for n in CONVW]), True), [A[n].shape for n in CONVW])
    taps = {}
    for n, st in zip(CONVW, conv_g):
        taps[n] = _pad_taps(jnp.transpose(st, (1, 2, 0, 3)).reshape(depth, st.shape[2], N_DEV * st.shape[3]))
    ssd_par = _ssd_params(m_dt_bias, m_A_log, m_D)
    small3 = {n: A[n].reshape(depth, 1, -1) for n in SMALL}

    def stage_weights(W, s, l, lands):
        for ms, land, src in zip(STAGES[s], lands, srcs[s]):
            land = lax.dynamic_update_slice(land, src[l][None], (me, 0, 0))
            off = 0
            for n in ms:
                r = wb[n].shape[1]
                W[n] = _full_from_shards(land if len(ms) == 1 else land[:, off:off + r])
                off += r
        if s == 1:
            W['w_in_p'], W['w_dt'] = _perm_w_in(W.pop('w_in'), dims)

    flight = {}

    via_sibling = {(0, 0), (0, 1)}

    def begin_layer(l, dep):
        for s in range(len(STAGES)):
            rels = NEAR_PEERS if (l, s) in via_sibling else ALL_PEERS
            sems, lands, dep = _xchg_begin(f"gather_begin{l}{'abc'[s]}", srcs[s], l, dep, rels)
            flight[(l, s)] = (sems, lands)
        return dep

    def end_stage(W, l, s, after):
        sems, lands = flight.pop((l, s))
        tag = f"{l}{'abc'[s]}"
        if (l, s) in via_sibling:
            lands = _xchg_end("gather_end" + tag, srcs[s], lands, sems, l, after, NEAR_PEERS)
            rsems, lands = _relay_begin("gather_relay" + tag, lands)
            lands = _relay_end("gather_relayed" + tag, lands, rsems, after)
        else:
            lands = _xchg_end("gather_end" + tag, srcs[s], lands, sems, l, after)
        stage_weights(W, s, l, lands)

    tok = begin_layer(0, taps['sc_conv_w'])
    h = x[0]
    saved = []
    layers = []
    for l in range(depth):
        W = {n: (small3[n], l) for n in SMALL}
        W.update(l=l, sc_taps=taps['sc_conv_w'], m_taps=taps['m_conv_w'], m_conv_b=small3['m_conv_b'], ssd_par=ssd_par)
        layers.append(W)
        end_stage(W, l, 0, tok if l == 0 else h)
        h1, s1 = _ffn_fwd("ffn1", h, W['ffn1_norm'], W['ffn1_wg'], W['ffn1_wu'], W['ffn1_wd'])
        end_stage(W, l, 1, h1)
        tok = begin_layer(l + 1, W['w_dt']) if l + 1 < depth else None
        h2, s2 = _mixer_fwd(h1, W, dims, dep=tok)
        end_stage(W, l, 2, h2)
        h3, s3 = _ffn_fwd("ffn2", h2, W['ffn2_norm'], W['ffn2_wg'], W['ffn2_wu'], W['ffn2_wd'])
        h4, s4 = _ple_layer_fwd(h3, p[l, 0], W)
        saved.append((h, h1, h2, h3, s1, s2, s3, s4))
        h = h4

    dh, loss_row, d_final = _loss_head("loss_head", h, final_norm, loss_target[0])
    loss = lax.psum(loss_row[0, 0], ("x", "y", "c"))

    def send_bufs(g, s):
        return [jnp.concatenate([_shards_from_full(g[n]) for n in ms], axis=1) if len(ms) > 1
                else _shards_from_full(g[ms[0]]) for ms in STAGES[s]]

    grads = [None] * depth
    pending = []

    def send_stage(g, l, s, dep):
        send = send_bufs(g, s)
        sems, lands, tok = _xchg_begin(f"scatter_begin{l}{'abc'[s]}", send, None, dep)
        pending.append((l, s, send, lands, sems))
        return tok

    tok = loss.reshape(1, 1)
    for l in reversed(range(depth)):
        W = layers[l]
        h0, h1, h2, h3, s1, s2, s3, s4 = saved[l]
        g = {}
        dh, g4 = _ple_layer_bwd(dh, h3, p[l, 0], W, s4, dep=tok)
        g.update(g4)
        dh, dg, d_wg, d_wu, d_wd = _ffn_bwd("ffn2", dh, h2, W['ffn2_norm'], W['ffn2_wg'], W['ffn2_wu'], W['ffn2_wd'], s3)
        g.update(ffn2_norm=dg.reshape(-1), ffn2_wg=d_wg, ffn2_wu=d_wu, ffn2_wd=d_wd)
        tok = send_stage(g, l, 2, dh)
        dh, g2 = _mixer_bwd(dh, h1, W, dims, s2, dep=tok)
        g.update(g2)
        tok = send_stage(g, l, 1, dh)
        dh, dg, d_wg, d_wu, d_wd = _ffn_bwd("ffn1", dh, h0, W['ffn1_norm'], W['ffn1_wg'], W['ffn1_wu'], W['ffn1_wd'], s1, dep=tok)
        g.update(ffn1_norm=dg.reshape(-1), ffn1_wg=d_wg, ffn1_wu=d_wu, ffn1_wd=d_wd)
        grads[l] = g
        tok = send_stage(g, l, 0, dh)
    grad_x = dh[None]

    g_lands = [[None] * len(STAGES) for _ in range(depth)]
    big_res = [{}, {}, {}, {}]

    def finish(entries, after):
        for l, s, send, lands, sems in entries:
            got = _xchg_end(f"scatter_end{l}{'abc'[s]}", send, lands, sems, None, after)
            after = got[0]
            g_lands[l][s] = [lax.dynamic_update_slice(o, lax.dynamic_slice_in_dim(b, me, 1, axis=0), (me, 0, 0)) for o, b in zip(got, send)]
        return after

    def adam_stages(stages):
        res = None
        for s in stages:
            for gi, ms in enumerate(STAGES[s]):
                off = 0
                for n in ms:
                    res = _sum_adam("adamw_" + n, [g_lands[l][s][gi] for l in range(depth)], off, work(n), work(n, 'm_'), work(n, 'v_'))
                    for k in range(4):
                        big_res[k][n] = jnp.swapaxes(res[k], 1, 2) if kind[n] == 'col' else res[k]
                    off += wb[n].shape[1]
        return res[0]

    after = finish(pending[:-1], tok)
    after = adam_stages(range(1, len(STAGES)))
    after = finish(pending[-1:], after)
    adam_stages([0])

    small_names = SMALL + CONVW
    small_parts = [jnp.stack([grads[l][n] for l in range(depth)]) for n in small_names] + [d_final.reshape(-1)]
    small_sum = _sum8("sum_small", _exchange("gather_small_grads", _pack(small_parts), True, dep=after))
    sg = dict(zip(small_names + ['final_norm'], _unpack(small_sum, [a.shape for a in small_parts])))
    for n in CONVW:
        c = A[n].shape[-1]
        sg[n] = lax.dynamic_slice_in_dim(sg[n], me * c, c, axis=2)
    s_order = small_names + ['final_norm']
    s_shapes = [sg[n].shape for n in s_order]
    s_out = _adam_flat("adamw_small", _pack([sg[n] for n in s_order]), _pack([A[n] for n in s_order]),
                       _pack([A['m_' + n] for n in s_order]), _pack([A['v_' + n] for n in s_order]))
    small_res = [sg] + [dict(zip(s_order, _unpack(flat, s_shapes))) for flat in s_out]

    outs = [loss, grad_x]
    for k in range(4):
        for n in WEIGHTS:
            outs.append(big_res[k][n] if n in big_res[k] else small_res[k][n])
    return tuple(outs)
```

```python
import functools

import jax
import jax.numpy as jnp
from jax import lax
from jax.experimental import pallas as pl
from jax.experimental.pallas import tpu as pltpu

BF = jnp.bfloat16
F32 = jnp.float32

EPS = 1e-6
N_DEV = 8
LANES = 128
SSM_GROUPS = 4
SSM_HEADDIM = 64
SSM_CHUNK = 128
HALO = 16
VMEM_LIMIT = 56 * 1024 * 1024
FLAT_ROW_TILE = 2048

ADAM_LR = 0.001
ADAM_B1 = 0.9
ADAM_B2 = 0.999
ADAM_EPS = 1e-08
ADAM_WD = 0.01
ADAM_STEP = 10

MESH = pl.DeviceIdType.MESH

ARG_NAMES = ['x', 'p', 'ffn1_norm', 'ffn1_wg', 'ffn1_wu', 'ffn1_wd', 'mix_norm', 'w_in', 'sc_conv_w', 'sc_w_out', 'm_conv_w', 'm_conv_b', 'm_dt_bias', 'm_A_log', 'm_D', 'm_norm', 'm_w_out', 'w_o', 'ffn2_norm', 'ffn2_wg', 'ffn2_wu', 'ffn2_wd', 'ple_norm', 'ple_w_gate', 'ple_w_proj', 'final_norm', 'loss_target']
WEIGHTS = ARG_NAMES[2:26]
BIG = [('ffn1_wg', 'col'), ('ffn1_wu', 'col'), ('ffn1_wd', 'row'), ('w_in', 'col'), ('sc_w_out', 'row'),
       ('m_w_out', 'row'), ('w_o', 'row'), ('ffn2_wg', 'col'), ('ffn2_wu', 'col'), ('ffn2_wd', 'row'),
       ('ple_w_gate', 'row'), ('ple_w_proj', 'col')]
CONVW = ['sc_conv_w', 'm_conv_w']
SMALL = ['ffn1_norm', 'mix_norm', 'm_conv_b', 'm_dt_bias', 'm_A_log', 'm_D', 'm_norm', 'ffn2_norm', 'ple_norm']


def _pick(n, cands):
    for c in cands:
        if n % c == 0:
            return c
    return n


def _cp(sem):
    return pltpu.CompilerParams(dimension_semantics=sem, vmem_limit_bytes=VMEM_LIMIT)


def _sigmoid(x):
    return 1.0 / (1.0 + jnp.exp(-x))


def _softplus(x):
    return jnp.maximum(x, 0.0) + jnp.log(1.0 + jnp.exp(-jnp.abs(x)))


def _exchange(name, x, gather, dep=None):
    slab = x.shape if gather else x.shape[1:]

    def body(x_ref, *rest):
        o_ref, send_sems, recv_sems, local_sem = rest[-4:]
        mx, my, mc = lax.axis_index("x"), lax.axis_index("y"), lax.axis_index("c")
        me = 4 * mx + 2 * my + mc

        def src_for(k):
            return x_ref if gather else x_ref.at[k]

        local = pltpu.make_async_copy(src_for(me), o_ref.at[me], local_sem)
        local.start()
        sends = []
        peers = []
        for r in range(1, N_DEV):
            px = (mx + ((r >> 2) & 1)) % 2
            py = (my + ((r >> 1) & 1)) % 2
            pc = (mc + (r & 1)) % 2
            peer = 4 * px + 2 * py + pc
            peers.append(peer)
            cp = pltpu.make_async_remote_copy(
                src_ref=src_for(peer), dst_ref=o_ref.at[me], send_sem=send_sems.at[r - 1], recv_sem=recv_sems.at[r - 1],
                device_id=(px, py, pc), device_id_type=MESH)
            cp.start()
            sends.append(cp)
        for r in range(1, N_DEV):
            peer = peers[r - 1]
            pltpu.make_async_remote_copy(
                src_ref=src_for(peer), dst_ref=o_ref.at[peer], send_sem=send_sems.at[r - 1], recv_sem=recv_sems.at[r - 1],
                device_id=(mx, my, mc), device_id_type=MESH).wait_recv()
        for cp in sends:
            cp.wait_send()
        local.wait()

    return pl.pallas_call(
        body, name=name,
        out_shape=jax.ShapeDtypeStruct((N_DEV,) + tuple(slab), x.dtype),
        in_specs=[pl.BlockSpec(memory_space=pltpu.HBM)] + ([] if dep is None else [pl.BlockSpec(memory_space=pl.ANY)]),
        out_specs=pl.BlockSpec(memory_space=pltpu.HBM),
        scratch_shapes=[pltpu.SemaphoreType.DMA((N_DEV - 1,)), pltpu.SemaphoreType.DMA((N_DEV - 1,)), pltpu.SemaphoreType.DMA],
    )(*([x] if dep is None else [x, dep]))


STAGES = [[['ffn1_wd'], ['ffn1_wg'], ['ffn1_wu']],
          [['w_in'], ['sc_w_out', 'w_o', 'ple_w_gate', 'm_w_out']],
          [['ffn2_wd'], ['ffn2_wg'], ['ffn2_wu'], ['ple_w_proj']]]
_HBM = pl.BlockSpec(memory_space=pltpu.HBM)
_SEM = pl.BlockSpec(memory_space=pltpu.SEMAPHORE)
_ANY = pl.BlockSpec(memory_space=pl.ANY)
_EFFECT = pltpu.SideEffectType.DATAFLOW_SIDE_EFFECTING


def _peer_list():
    mx, my, mc = lax.axis_index("x"), lax.axis_index("y"), lax.axis_index("c")
    out = []
    for r in range(1, N_DEV):
        px = (mx + ((r >> 2) & 1)) % 2
        py = (my + ((r >> 1) & 1)) % 2
        pc = (mc + (r & 1)) % 2
        out.append((px, py, pc, 4 * px + 2 * py + pc))
    return 4 * mx + 2 * my + mc, out


ALL_PEERS = tuple(range(1, N_DEV))
NEAR_PEERS = (1, 2, 4, 6)
RELAYED = (2, 4, 6)


def _xchg_copy(src_refs, land_refs, send_sems, recv_sems, layer, i, r, peer, dst_slab):
    px, py, pc, pidx = peer
    n = len(src_refs)
    src = src_refs[i].at[layer] if layer is not None else src_refs[i].at[pidx]
    return pltpu.make_async_remote_copy(
        src_ref=src, dst_ref=land_refs[i].at[dst_slab], send_sem=send_sems.at[r * n + i], recv_sem=recv_sems.at[r * n + i],
        device_id=(px, py, pc), device_id_type=MESH)


def _xchg_begin(name, srcs, layer, dep, rels=ALL_PEERS):
    n = len(srcs)
    slabs = [tuple(s.shape[1:]) for s in srcs]
    ncp = n * len(rels)

    def body(*refs):
        src_refs, land_refs = refs[:n], refs[n:2 * n]
        send_sems, recv_sems = refs[2 * n + 1], refs[2 * n + 2]
        token = refs[-1]
        me, peers = _peer_list()
        for ri, r in enumerate(rels):
            for i in range(n):
                _xchg_copy(src_refs, land_refs, send_sems, recv_sems, layer, i, ri, peers[r - 1], me).start()
        token[...] = jnp.zeros_like(token)

    lands = [pltpu.with_memory_space_constraint(lax.empty((N_DEV,) + sl, s.dtype), pltpu.HBM) for sl, s in zip(slabs, srcs)]
    out = pl.pallas_call(
        body, name=name,
        out_shape=(pltpu.SemaphoreType.DMA((ncp,)), pltpu.SemaphoreType.DMA((ncp,)),
                   *[pltpu.HBM((N_DEV,) + sl, s.dtype) for sl, s in zip(slabs, srcs)], jax.ShapeDtypeStruct((8, LANES), F32)),
        in_specs=[_HBM] * (2 * n) + [_ANY],
        out_specs=(_SEM, _SEM, *[_HBM] * n, pl.BlockSpec(memory_space=pltpu.VMEM)),
        input_output_aliases={n + i: 2 + i for i in range(n)},
        compiler_params=pltpu.CompilerParams(has_side_effects=_EFFECT),
    )(*[pltpu.with_memory_space_constraint(s, pltpu.HBM) for s in srcs], *lands, dep)
    return (out[0], out[1]), list(out[2:2 + n]), out[-1]


def _xchg_end(name, srcs, lands, sems, layer, after, rels=ALL_PEERS):
    n = len(srcs)

    def body(*refs):
        src_refs, land_refs = refs[:n], refs[n:2 * n]
        send_sems, recv_sems = refs[2 * n], refs[2 * n + 1]
        me, peers = _peer_list()
        for ri, r in enumerate(rels):
            for i in range(n):
                cp = _xchg_copy(src_refs, land_refs, send_sems, recv_sems, layer, i, ri, peers[r - 1], peers[r - 1][3])
                cp.wait_send()
                cp.wait_recv()

    out = pl.pallas_call(
        body, name=name,
        out_shape=tuple(pltpu.HBM(l.shape, l.dtype) for l in lands),
        in_specs=[_HBM] * (2 * n) + [_SEM, _SEM, _ANY], out_specs=tuple([_HBM] * n),
        input_output_aliases={n + i: i for i in range(n)},
        compiler_params=pltpu.CompilerParams(has_side_effects=_EFFECT),
    )(*[pltpu.with_memory_space_constraint(s, pltpu.HBM) for s in srcs], *lands, sems[0], sems[1], after)
    return list(out)


def _relay_copy(land_refs, send_sems, recv_sems, i, qi, slab, sibling):
    n = len(land_refs)
    return pltpu.make_async_remote_copy(
        src_ref=land_refs[i].at[slab], dst_ref=land_refs[i].at[slab], send_sem=send_sems.at[qi * n + i],
        recv_sem=recv_sems.at[qi * n + i], device_id=sibling[:3], device_id_type=MESH)


def _relay_begin(name, lands):
    n = len(lands)
    ncp = n * len(RELAYED)

    def body(*refs):
        land_refs = refs[:n]
        send_sems, recv_sems = refs[n], refs[n + 1]
        me, peers = _peer_list()
        for qi, q in enumerate(RELAYED):
            for i in range(n):
                _relay_copy(land_refs, send_sems, recv_sems, i, qi, peers[q - 1][3], peers[0]).start()

    out = pl.pallas_call(
        body, name=name,
        out_shape=(pltpu.SemaphoreType.DMA((ncp,)), pltpu.SemaphoreType.DMA((ncp,)), *[pltpu.HBM(l.shape, l.dtype) for l in lands]),
        in_specs=[_HBM] * n, out_specs=(_SEM, _SEM, *[_HBM] * n),
        input_output_aliases={i: 2 + i for i in range(n)},
        compiler_params=pltpu.CompilerParams(has_side_effects=_EFFECT),
    )(*lands)
    return (out[0], out[1]), list(out[2:])


def _relay_end(name, lands, sems, after):
    n = len(lands)

    def body(*refs):
        land_refs = refs[:n]
        send_sems, recv_sems = refs[n], refs[n + 1]
        me, peers = _peer_list()
        for qi, q in enumerate(RELAYED):
            for i in range(n):
                _relay_copy(land_refs, send_sems, recv_sems, i, qi, peers[q - 1][3], peers[0]).wait_send()
                _relay_copy(land_refs, send_sems, recv_sems, i, qi, peers[q][3], peers[0]).wait_recv()

    out = pl.pallas_call(
        body, name=name,
        out_shape=tuple(pltpu.HBM(l.shape, l.dtype) for l in lands),
        in_specs=[_HBM] * n + [_SEM, _SEM, _ANY], out_specs=tuple([_HBM] * n),
        input_output_aliases={i: i for i in range(n)},
        compiler_params=pltpu.CompilerParams(has_side_effects=_EFFECT),
    )(*lands, sems[0], sems[1], after)
    return list(out)


MM_TILES = (1024, 1408, 512, 256, 128)
MM_OPERAND_BYTES = 24 * 1024 * 1024


def _mm(name, a, b, *, ta=False, tb=False, out_dtype=None, res=None, alpha=1.0, dep=None):
    out_dtype = out_dtype or BF
    M, K = (a.shape[1], a.shape[0]) if ta else a.shape
    N = b.shape[0] if tb else b.shape[1]
    assert (b.shape[1] if tb else b.shape[0]) == K, (name, a.shape, b.shape)
    tm = _pick(M, MM_TILES)
    tn = _pick(N, MM_TILES)
    per_k = 2 * (tm * a.dtype.itemsize + tn * b.dtype.itemsize)
    tk = [t for t in sorted({K, 4096, 2816, 2560, 2048, 1408, 1024, 512, 256, 128}, reverse=True)
          if K % t == 0 and (t * per_k <= MM_OPERAND_BYTES or t == 128)][0]
    nk = K // tk
    a_spec = pl.BlockSpec((tk, tm), lambda i, j, k: (k, i)) if ta else pl.BlockSpec((tm, tk), lambda i, j, k: (i, k))
    b_spec = pl.BlockSpec((tn, tk), lambda i, j, k: (j, k)) if tb else pl.BlockSpec((tk, tn), lambda i, j, k: (k, j))
    dn = (((0 if ta else 1,), (1 if tb else 0,)), ((), ()))
    has_res = res is not None
    n_dep = 0 if dep is None else 1

    def body(*refs):
        a_ref, b_ref = refs[:2]
        r_ref = refs[2] if has_res else None
        o_ref = refs[2 + has_res + n_dep]

        def finish(v):
            if alpha != 1.0:
                v = v * alpha
            if has_res:
                v = r_ref[...] + v
            o_ref[...] = v.astype(o_ref.dtype)

        part = lax.dot_general(a_ref[...].astype(BF), b_ref[...].astype(BF), dn, preferred_element_type=F32)
        if nk == 1:
            finish(part)
            return
        acc = refs[-1]
        k = pl.program_id(2)

        @pl.when(k == 0)
        def _():
            acc[...] = part

        @pl.when((k > 0) & (k < nk - 1))
        def _():
            acc[...] += part

        @pl.when(k == nk - 1)
        def _():
            finish(acc[...] + part)

    in_specs = [a_spec, b_spec]
    args = [a, b]
    if has_res:
        in_specs.append(pl.BlockSpec((tm, tn), lambda i, j, k: (i, j)))
        args.append(res)
    if dep is not None:
        in_specs.append(_ANY)
        args.append(dep)
    return pl.pallas_call(
        body, name=name, grid=(M // tm, N // tn, nk),
        in_specs=in_specs, out_specs=pl.BlockSpec((tm, tn), lambda i, j, k: (i, j)),
        out_shape=jax.ShapeDtypeStruct((M, N), out_dtype),
        scratch_shapes=[pltpu.VMEM((tm, tn), F32)] if nk > 1 else [],
        compiler_params=_cp(("parallel", "parallel", "arbitrary")),
    )(*args)


def _mm_rms_bwd(name, a, b, acc_in, h, g, dh_res, *, tb=False):
    M, K = a.shape
    N = b.shape[0] if tb else b.shape[1]
    assert (b.shape[1] if tb else b.shape[0]) == K and h.shape == (M, N), (name, a.shape, b.shape)
    tm = _pick(M, (FFN_TOKEN_TILE, 256, 128))
    dn = (((1,), (1 if tb else 0,)), ((), ()))
    has_acc = acc_in is not None
    g_arr, g_row = _prow(g) if isinstance(g, tuple) else (_prow(g), None)

    def body(*refs):
        a_ref, b_ref = refs[:2]
        c_ref = refs[2] if has_acc else None
        h_ref, g_ref, r_ref, o_ref, dg_ref = refs[2 + has_acc:]
        d = lax.dot_general(a_ref[...].astype(BF), b_ref[...].astype(BF), dn, preferred_element_type=F32)
        if has_acc:
            d = c_ref[...] + d
        x = h_ref[...]
        r = lax.rsqrt(jnp.mean(x * x, axis=-1, keepdims=True) + EPS)
        xhat = x * r
        dxhat = d * g_ref[...]
        o_ref[...] = r_ref[...] + r * (dxhat - xhat * jnp.mean(dxhat * xhat, axis=-1, keepdims=True))

        @pl.when(pl.program_id(0) == 0)
        def _():
            dg_ref[...] = jnp.zeros_like(dg_ref)

        dg_ref[...] += jnp.sum(d * xhat, axis=0, keepdims=True)

    row = pl.BlockSpec((tm, N), lambda i: (i, 0))
    gspec = pl.BlockSpec((1, N), lambda i: (0, 0)) if g_row is None else pl.BlockSpec((None, 1, N), lambda i: (g_row, 0, 0))
    in_specs = [pl.BlockSpec((tm, K), lambda i: (i, 0)), pl.BlockSpec(b.shape, lambda i: (0, 0))]
    args = [a, b]
    if has_acc:
        in_specs.append(row)
        args.append(acc_in)
    return pl.pallas_call(
        body, name=name, grid=(M // tm,),
        in_specs=in_specs + [row, gspec, row], out_specs=[row, pl.BlockSpec((1, N), lambda i: (0, 0))],
        out_shape=[jax.ShapeDtypeStruct((M, N), F32), jax.ShapeDtypeStruct((1, N), F32)],
        compiler_params=_cp(("arbitrary",)),
    )(*args, h, g_arr, dh_res)


def _ew(name, fn, tiled, params, outs, accs=(), tile=256, dep=None):
    tiled = [t if isinstance(t, tuple) else (t, t.shape[1], 0) for t in tiled]
    params = [q if isinstance(q, tuple) else (q, None) for q in params]
    S = tiled[0][0].shape[0]
    T = _pick(S, (tile, 128, 64, 32, 16))
    n_in = len(tiled) + len(params)
    n_dep = 0 if dep is None else 1

    def body(*refs):
        fn(pl.program_id(0) == 0, *refs[:n_in], *refs[n_in + n_dep:])

    in_specs = [pl.BlockSpec((T, w), lambda i, cb=cb: (i, cb)) for _, w, cb in tiled]
    for q, row in params:
        if row is None:
            in_specs.append(pl.BlockSpec(q.shape, lambda i: (0, 0)))
        else:
            in_specs.append(pl.BlockSpec((None, 1, q.shape[2]), lambda i, row=row: (row, 0, 0)))
    args = [t[0] for t in tiled] + [q[0] for q in params]
    if dep is not None:
        in_specs.append(pl.BlockSpec(memory_space=pl.ANY))
        args.append(dep)
    out_specs = [pl.BlockSpec((T, w), lambda i: (i, 0)) for w, _ in outs]
    out_specs += [pl.BlockSpec(shp, lambda i: (0, 0)) for shp, _ in accs]
    out_shape = [jax.ShapeDtypeStruct((S, w), dt) for w, dt in outs]
    out_shape += [jax.ShapeDtypeStruct(shp, dt) for shp, dt in accs]
    res = pl.pallas_call(
        body, name=name, grid=(S // T,), in_specs=in_specs, out_specs=out_specs, out_shape=out_shape,
        compiler_params=_cp(("arbitrary",)),
    )(*args)
    return res


def _prow(g):
    return g if isinstance(g, tuple) else g.reshape(1, -1)


def _rms_fwd(name, h, g, dep=None):
    def fn(first, h_ref, g_ref, o_ref):
        x = h_ref[...]
        r = lax.rsqrt(jnp.mean(x * x, axis=-1, keepdims=True) + EPS)
        o_ref[...] = (x * r * g_ref[...]).astype(o_ref.dtype)

    return _ew(name, fn, [h], [_prow(g)], [(h.shape[1], BF)], dep=dep)[0]


def _rms_bwd(name, dxn, h, g, res):
    D = h.shape[1]

    def fn(first, d_ref, h_ref, r_ref, g_ref, o_ref, dg_ref):
        x = h_ref[...]
        d = d_ref[...].astype(F32)
        r = lax.rsqrt(jnp.mean(x * x, axis=-1, keepdims=True) + EPS)
        xhat = x * r
        dxhat = d * g_ref[...]
        dh = r * (dxhat - xhat * jnp.mean(dxhat * xhat, axis=-1, keepdims=True))
        o_ref[...] = r_ref[...] + dh

        @pl.when(first)
        def _():
            dg_ref[...] = jnp.zeros_like(dg_ref)

        dg_ref[...] += jnp.sum(d * xhat, axis=0, keepdims=True)

    return _ew(name, fn, [dxn, h, res], [_prow(g)], [(D, F32)], [((1, D), F32)])


FFN_TOKEN_TILE = 512


def _ffn_up(name, xn, wgT, wuT, dep=None):
    S, D = xn.shape
    FF = wgT.shape[0]
    tm = _pick(S, (FFN_TOKEN_TILE, 256, 128))
    tn = _pick(FF, MM_TILES)
    n_dep = 0 if dep is None else 1

    def body(x_ref, g_ref, u_ref, *rest):
        a_ref, b_ref, h_ref = rest[n_dep:]
        x = x_ref[...]
        a = _dot_nt(x, g_ref[...])
        b = _dot_nt(x, u_ref[...])
        a_ref[...] = a.astype(BF)
        b_ref[...] = b.astype(BF)
        h_ref[...] = (a * _sigmoid(a) * b).astype(BF)

    wspec = pl.BlockSpec((tn, D), lambda j, i: (j, 0))
    ospec = pl.BlockSpec((tm, tn), lambda j, i: (i, j))
    return pl.pallas_call(
        body, name=name, grid=(FF // tn, S // tm),
        in_specs=[pl.BlockSpec((tm, D), lambda j, i: (i, 0)), wspec, wspec] + ([] if dep is None else [_ANY]),
        out_specs=[ospec] * 3, out_shape=[jax.ShapeDtypeStruct((S, FF), BF)] * 3,
        compiler_params=_cp(("parallel", "arbitrary")),
    )(*([xn, wgT, wuT] + ([] if dep is None else [dep])))


def _ffn_dact(name, dh, wd, a, b, dep=None):
    S, D = dh.shape
    FF = wd.shape[0]
    tm = _pick(S, (FFN_TOKEN_TILE, 256, 128))
    tn = _pick(FF, MM_TILES)
    n_dep = 0 if dep is None else 1

    def body(d_ref, w_ref, a_ref, b_ref, *rest):
        da_ref, db_ref = rest[n_dep:]
        d = 0.5 * _dot_nt(d_ref[...].astype(BF), w_ref[...])
        av = a_ref[...].astype(F32)
        s = _sigmoid(av)
        da_ref[...] = (d * b_ref[...].astype(F32) * (s * (1.0 + av * (1.0 - s)))).astype(BF)
        db_ref[...] = (d * av * s).astype(BF)

    tspec = pl.BlockSpec((tm, tn), lambda j, i: (i, j))
    return pl.pallas_call(
        body, name=name, grid=(FF // tn, S // tm),
        in_specs=[pl.BlockSpec((tm, D), lambda j, i: (i, 0)), pl.BlockSpec((tn, D), lambda j, i: (j, 0)), tspec, tspec]
        + ([] if dep is None else [_ANY]),
        out_specs=[tspec] * 2, out_shape=[jax.ShapeDtypeStruct((S, FF), BF)] * 2,
        compiler_params=_cp(("parallel", "arbitrary")),
    )(*([dh, wd, a, b] + ([] if dep is None else [dep])))


def _merge_fwd(name, ga, gm, ya, ym):
    def fn(first, ga_ref, gm_ref, ya_ref, ym_ref, o_ref):
        o = _sigmoid(ga_ref[...].astype(F32)) * ya_ref[...].astype(F32) + _sigmoid(gm_ref[...].astype(F32)) * ym_ref[...].astype(F32)
        o_ref[...] = o.astype(o_ref.dtype)

    return _ew(name, fn, [ga, gm, ya, ym], [], [(ya.shape[1], BF)])[0]


def _merge_bwd(name, dmerged, ga, gm, ya, ym):
    W = ya.shape[1]

    def fn(first, d_ref, ga_ref, gm_ref, ya_ref, ym_ref, dga_ref, dgm_ref, dya_ref, dym_ref):
        d = d_ref[...].astype(F32)
        sa = _sigmoid(ga_ref[...].astype(F32))
        sm = _sigmoid(gm_ref[...].astype(F32))
        dga_ref[...] = (d * ya_ref[...].astype(F32) * sa * (1.0 - sa)).astype(BF)
        dgm_ref[...] = (d * ym_ref[...].astype(F32) * sm * (1.0 - sm)).astype(BF)
        dya_ref[...] = (d * sa).astype(BF)
        dym_ref[...] = (d * sm).astype(BF)

    return _ew(name, fn, [dmerged, ga, gm, ya, ym], [], [(W, BF)] * 4)


def _gnorm_fwd(name, y, z, w):
    W = y.shape[1]
    gw = W // SSM_GROUPS

    def fn(first, y_ref, z_ref, w_ref, o_ref):
        for g in range(SSM_GROUPS):
            sl = slice(g * gw, (g + 1) * gw)
            zz = z_ref[:, sl].astype(F32)
            t = y_ref[:, sl].astype(F32) * (zz * _sigmoid(zz))
            r = lax.rsqrt(jnp.mean(t * t, axis=-1, keepdims=True) + EPS)
            o_ref[:, sl] = (t * r * w_ref[:, sl]).astype(o_ref.dtype)

    return _ew(name, fn, [y, z], [_prow(w)], [(W, BF)])[0]


def _gnorm_bwd(name, dyn, y, z, w):
    W = y.shape[1]
    gw = W // SSM_GROUPS

    def fn(first, d_ref, y_ref, z_ref, w_ref, dy_ref, dz_ref, dw_ref):
        @pl.when(first)
        def _():
            dw_ref[...] = jnp.zeros_like(dw_ref)

        for g in range(SSM_GROUPS):
            sl = slice(g * gw, (g + 1) * gw)
            zz = z_ref[:, sl].astype(F32)
            yy = y_ref[:, sl].astype(F32)
            d = d_ref[:, sl].astype(F32)
            s = _sigmoid(zz)
            sz = zz * s
            t = yy * sz
            r = lax.rsqrt(jnp.mean(t * t, axis=-1, keepdims=True) + EPS)
            that = t * r
            dthat = d * w_ref[:, sl]
            dt = r * (dthat - that * jnp.mean(dthat * that, axis=-1, keepdims=True))
            dw_ref[:, sl] += jnp.sum(d * that, axis=0, keepdims=True)
            dy_ref[:, sl] = (dt * sz).astype(BF)
            dz_ref[:, sl] = (dt * yy * (s * (1.0 + zz * (1.0 - s)))).astype(BF)

    return _ew(name, fn, [dyn, y, z], [_prow(w)], [(W, BF), (W, BF)], [((1, W), F32)])


def _ple_fwd(name, h, gpre, pp):
    def fn(first, h_ref, g_ref, p_ref, o_ref):
        o_ref[...] = h_ref[...] + _sigmoid(g_ref[...].astype(F32)) * p_ref[...].astype(F32)

    return _ew(name, fn, [h, gpre, pp], [], [(h.shape[1], F32)])[0]


def _ple_bwd(name, dh, gpre, pp, dep=None):
    W = dh.shape[1]

    def fn(first, d_ref, g_ref, p_ref, dg_ref, dp_ref):
        d = d_ref[...]
        s = _sigmoid(g_ref[...].astype(F32))
        dg_ref[...] = (d * p_ref[...].astype(F32) * s * (1.0 - s)).astype(BF)
        dp_ref[...] = (d * s).astype(BF)

    return _ew(name, fn, [dh, gpre, pp], [], [(W, BF), (W, BF)], dep=dep)


def _loss_head(name, h, g, target):
    D = h.shape[1]

    def fn(first, h_ref, t_ref, g_ref, dh_ref, loss_ref, dg_ref):
        x = h_ref[...]
        r = lax.rsqrt(jnp.mean(x * x, axis=-1, keepdims=True) + EPS)
        xhat = x * r
        err = xhat * g_ref[...] - t_ref[...]
        part = 0.5 * jnp.sum(jnp.mean(err * err, axis=-1, keepdims=True), axis=0, keepdims=True)
        dy = err * (1.0 / D)
        dxhat = dy * g_ref[...]
        dh_ref[...] = r * (dxhat - xhat * jnp.mean(dxhat * xhat, axis=-1, keepdims=True))

        @pl.when(first)
        def _():
            loss_ref[...] = jnp.zeros_like(loss_ref)
            dg_ref[...] = jnp.zeros_like(dg_ref)

        loss_ref[...] += jnp.broadcast_to(part, loss_ref.shape)
        dg_ref[...] += jnp.sum(dy * xhat, axis=0, keepdims=True)

    return _ew(name, fn, [h, target], [_prow(g)], [(D, F32)], [((1, LANES), F32), ((1, D), F32)])


def _conv_specs(S, C, offs, l):
    T = _pick(S, (512, 256, 128, 64, 32, 16))
    Ct = [c for c in (512, 256, 128) if C % c == 0 and all(o % c == 0 for o in offs)][0]
    per = T // HALO
    last = S // HALO - 1

    def cur(off=0):
        return pl.BlockSpec((T, Ct), lambda j, i: (i, off // Ct + j))

    def prev(off=0):
        return pl.BlockSpec((HALO, Ct), lambda j, i: (jnp.maximum(i * per - 1, 0), off // Ct + j))

    def nxt(off=0):
        return pl.BlockSpec((HALO, Ct), lambda j, i: (jnp.minimum((i + 1) * per, last), off // Ct + j))

    wspec = pl.BlockSpec((None, 8, Ct), lambda j, i: (l, 0, j))
    return T, Ct, cur, prev, nxt, wspec


def _pad_taps(w):
    return jnp.concatenate([w.astype(F32), jnp.zeros((w.shape[0], 8 - w.shape[1], w.shape[2]), F32)], axis=1)


def _causal(cat, w_ref, K, T, lead):
    views = [(pltpu.roll(cat, K - 1 - k, 0) if k < K - 1 else cat)[lead:lead + T] for k in range(K)]
    out = None
    for k in range(K):
        term = w_ref[k:k + 1, :] * views[k]
        out = term if out is None else out + term
    return out, views


def _anticausal(cat, w_ref, K, T):
    out = None
    rows = cat.shape[0]
    for k in range(K):
        o = K - 1 - k
        term = w_ref[k:k + 1, :] * (pltpu.roll(cat, rows - o, 0) if o else cat)[:T]
        out = term if out is None else out + term
    return out


def _scconv_fwd(name, proj, ob, oc, ox, taps, K, l):
    S = proj.shape[0]
    C = taps.shape[2]
    T, Ct, cur, prev, nxt, wspec = _conv_specs(S, C, (ob, oc, ox), l)

    def body(b_ref, c_ref, x_ref, cp_ref, xp_ref, w_ref, o_ref):
        i = pl.program_id(1)
        q = c_ref[...].astype(F32) * x_ref[...].astype(F32)
        qp = jnp.where(i == 0, 0.0, cp_ref[...].astype(F32) * xp_ref[...].astype(F32))
        cat = jnp.concatenate([qp, q], axis=0)
        o_ref[...] = (b_ref[...].astype(F32) * _causal(cat, w_ref, K, T, HALO)[0]).astype(o_ref.dtype)

    return pl.pallas_call(
        body, name=name, grid=(C // Ct, S // T),
        in_specs=[cur(ob), cur(oc), cur(ox), prev(oc), prev(ox), wspec], out_specs=cur(),
        out_shape=jax.ShapeDtypeStruct((S, C), BF), compiler_params=_cp(("parallel", "arbitrary")),
    )(proj, proj, proj, proj, proj, taps)


def _scconv_bwd(name, dv, proj, ob, oc, ox, taps, K, l):
    S = proj.shape[0]
    C = taps.shape[2]
    T, Ct, cur, prev, nxt, wspec = _conv_specs(S, C, (ob, oc, ox), l)
    n_t = S // T

    def body(d_ref, b_ref, c_ref, x_ref, dn_ref, bn_ref, cp_ref, xp_ref, w_ref, db_ref, dc_ref, dx_ref, dw_ref):
        i = pl.program_id(1)
        c = c_ref[...].astype(F32)
        x = x_ref[...].astype(F32)
        d = d_ref[...].astype(F32)
        q = c * x
        qp = jnp.where(i == 0, 0.0, cp_ref[...].astype(F32) * xp_ref[...].astype(F32))
        catq = jnp.concatenate([qp, q], axis=0)
        cv, q_views = _causal(catq, w_ref, K, T, HALO)
        db_ref[...] = (d * cv).astype(BF)
        dcv = d * b_ref[...].astype(F32)
        dcvn = jnp.where(i == n_t - 1, 0.0, dn_ref[...].astype(F32) * bn_ref[...].astype(F32))
        catd = jnp.concatenate([dcv, dcvn], axis=0)
        dq = _anticausal(catd, w_ref, K, T)
        dc_ref[...] = (dq * x).astype(BF)
        dx_ref[...] = (dq * c).astype(BF)

        @pl.when(i == 0)
        def _():
            dw_ref[...] = jnp.zeros_like(dw_ref)

        for k in range(K):
            dw_ref[k:k + 1, :] += jnp.sum(dcv * q_views[k], axis=0, keepdims=True)

    return pl.pallas_call(
        body, name=name, grid=(C // Ct, n_t),
        in_specs=[cur(), cur(ob), cur(oc), cur(ox), nxt(), nxt(ob), prev(oc), prev(ox), wspec],
        out_specs=[cur(), cur(), cur(), pl.BlockSpec((8, Ct), lambda j, i: (0, j))],
        out_shape=[jax.ShapeDtypeStruct((S, C), BF)] * 3 + [jax.ShapeDtypeStruct((8, C), F32)],
        compiler_params=_cp(("parallel", "arbitrary")),
    )(dv, proj, proj, proj, dv, proj, proj, proj, taps)


def _mconv_fwd(name, proj, ox, taps, K, bias, l):
    S = proj.shape[0]
    C = taps.shape[2]
    T, Ct, cur, prev, nxt, wspec = _conv_specs(S, C, (ox,), l)
    bspec = pl.BlockSpec((None, 1, Ct), lambda j, i: (l, 0, j))

    def body(x_ref, xp_ref, w_ref, b_ref, o_ref):
        i = pl.program_id(1)
        xp = jnp.where(i == 0, 0.0, xp_ref[...].astype(F32))
        cat = jnp.concatenate([xp, x_ref[...].astype(F32)], axis=0)
        pre = _causal(cat, w_ref, K, T, HALO)[0] + b_ref[...]
        o_ref[...] = (pre * _sigmoid(pre)).astype(o_ref.dtype)

    return pl.pallas_call(
        body, name=name, grid=(C // Ct, S // T),
        in_specs=[cur(ox), prev(ox), wspec, bspec], out_specs=cur(),
        out_shape=jax.ShapeDtypeStruct((S, C), BF), compiler_params=_cp(("parallel", "arbitrary")),
    )(proj, proj, taps, bias)


def _mconv_bwd(name, dout, proj, ox, taps, K, bias, l):
    S = proj.shape[0]
    C = taps.shape[2]
    T, Ct, cur, prev, nxt, wspec = _conv_specs(S, C, (ox,), l)
    n_t = S // T
    bspec = pl.BlockSpec((None, 1, Ct), lambda j, i: (l, 0, j))

    def body(d_ref, dn_ref, x_ref, xp_ref, xn_ref, w_ref, b_ref, dx_ref, dw_ref, db_ref):
        i = pl.program_id(1)
        xp = jnp.where(i == 0, 0.0, xp_ref[...].astype(F32))
        cat3 = jnp.concatenate([xp, x_ref[...].astype(F32), xn_ref[...].astype(F32)], axis=0)
        pre, x_views = _causal(cat3, w_ref, K, T + HALO, HALO)
        pre = pre + b_ref[...]
        dn = jnp.where(i == n_t - 1, 0.0, dn_ref[...].astype(F32))
        dext = jnp.concatenate([d_ref[...].astype(F32), dn], axis=0)
        s = _sigmoid(pre)
        dpre = dext * (s * (1.0 + pre * (1.0 - s)))
        dx_ref[...] = _anticausal(dpre, w_ref, K, T).astype(BF)
        dcur = dpre[:T]

        @pl.when(i == 0)
        def _():
            dw_ref[...] = jnp.zeros_like(dw_ref)
            db_ref[...] = jnp.zeros_like(db_ref)

        db_ref[...] += jnp.sum(dcur, axis=0, keepdims=True)
        for k in range(K):
            dw_ref[k:k + 1, :] += jnp.sum(dcur * x_views[k][:T], axis=0, keepdims=True)

    return pl.pallas_call(
        body, name=name, grid=(C // Ct, n_t),
        in_specs=[cur(), nxt(), cur(ox), prev(ox), nxt(ox), wspec, bspec],
        out_specs=[cur(), pl.BlockSpec((8, Ct), lambda j, i: (0, j)), pl.BlockSpec((1, Ct), lambda j, i: (0, j))],
        out_shape=[jax.ShapeDtypeStruct((S, C), BF), jax.ShapeDtypeStruct((8, C), F32), jax.ShapeDtypeStruct((1, C), F32)],
        compiler_params=_cp(("parallel", "arbitrary")),
    )(dout, dout, proj, proj, proj, taps, bias)


def _tri_matmul(tri_bf, v):
    hi = v.astype(BF)
    r1 = v - hi.astype(F32)
    mid = r1.astype(BF)
    lo = (r1 - mid.astype(F32)).astype(BF)
    dot = functools.partial(jnp.dot, preferred_element_type=F32)
    return dot(tri_bf, hi) + dot(tri_bf, mid) + dot(tri_bf, lo)


def _dot_nt(a, b):
    return lax.dot_general(a, b, (((1,), (1,)), ((), ())), preferred_element_type=F32)


def _dot_tn(a, b):
    return lax.dot_general(a, b, (((0,), (0,)), ((), ())), preferred_element_type=F32)


def _dot_nn(a, b):
    return jnp.dot(a, b, preferred_element_type=F32)


def _ssd_chunk_scalars(dtr_ref, par_ref, L):
    row_i = lax.broadcasted_iota(jnp.int32, (L, L), 0)
    col_i = lax.broadcasted_iota(jnp.int32, (L, L), 1)
    tri = row_i >= col_i
    pre = dtr_ref[...] + par_ref[0:1, :]
    dt_all = _softplus(pre)
    A_row = -jnp.exp(par_ref[1:2, :])
    a_all = dt_all * A_row
    acum_all = _tri_matmul(tri.astype(BF), a_all)
    return tri, pre, dt_all, A_row, a_all, acum_all, acum_all.T


def _ssd_dims(xbc, heads):
    S, conv_dim = xbc.shape
    inner = heads * SSM_HEADDIM
    N = (conv_dim - inner) // (2 * SSM_GROUPS)
    gw = inner // SSM_GROUPS
    PP = gw // LANES
    L = min(SSM_CHUNK, S)
    assert N == LANES and gw % LANES == 0 and inner % (SSM_GROUPS * N) == 0 and S % L == 0
    return S, inner, N, gw, PP, L, S // L


def _ssd_params(dt_bias, A_log, Dp):
    depth, H = dt_bias.shape
    rows = jnp.stack([dt_bias, A_log, Dp], axis=1).astype(F32)
    rows = jnp.concatenate([rows, jnp.zeros((depth, 3, LANES - H), F32)], axis=2)
    return jnp.concatenate([rows, jnp.zeros((depth, 5, LANES), F32)], axis=1)


def _ssd_fwd(name, xbc, dt_raw, par, l, heads):
    S, inner, N, gw, PP, L, nc = _ssd_dims(xbc, heads)
    G = SSM_GROUPS

    def body(x_ref, b_ref, c_ref, dtr_ref, par_ref, y_ref, st_out_ref, st_ref):
        @pl.when(pl.program_id(0) == 0)
        def _():
            st_ref[...] = jnp.zeros_like(st_ref)

        tri, pre, dt_all, A_row, a_all, acum_all, acumT = _ssd_chunk_scalars(dtr_ref, par_ref, L)
        lane = lax.broadcasted_iota(jnp.int32, (L, LANES), 1)
        lane1 = lax.broadcasted_iota(jnp.int32, (1, LANES), 1)
        lo = lane < SSM_HEADDIM
        lo1 = lane1 < SSM_HEADDIM
        for g in range(G):
            Bb = b_ref[:, g * N:(g + 1) * N]
            Cb = c_ref[:, g * N:(g + 1) * N]
            BbT = Bb.astype(F32).T.astype(BF)
            Gm = _dot_nt(Cb, Bb)
            for j in range(PP):
                pj = g * PP + j
                h0, h1 = 2 * pj, 2 * pj + 1
                cols = slice(pj * LANES, (pj + 1) * LANES)
                x = x_ref[:, cols].astype(F32)
                dt_l = jnp.where(lo, dt_all[:, h0:h0 + 1], dt_all[:, h1:h1 + 1])
                ac0 = acum_all[:, h0:h0 + 1]
                ac1 = acum_all[:, h1:h1 + 1]
                ac_l = jnp.where(lo, ac0, ac1)
                E0 = jnp.exp(jnp.where(tri, ac0 - acumT[h0:h0 + 1, :], -1e30))
                E1 = jnp.exp(jnp.where(tri, ac1 - acumT[h1:h1 + 1, :], -1e30))
                xd = x * dt_l
                xdb = xd.astype(BF)
                yd = jnp.where(lo, _dot_nn((Gm * E0).astype(BF), xdb), _dot_nn((Gm * E1).astype(BF), xdb))
                prevT = st_ref[pj]
                st_out_ref[0, pj] = prevT
                P = _dot_nn(Cb, prevT.astype(BF))
                D_l = jnp.where(lo1, par_ref[2:3, h0:h0 + 1], par_ref[2:3, h1:h1 + 1])
                y_ref[:, cols] = (yd + P * jnp.exp(ac_l) + D_l * x).astype(y_ref.dtype)
                al0 = ac0[L - 1:L, :]
                al1 = ac1[L - 1:L, :]
                Wm = xd * jnp.exp(jnp.where(lo, al0, al1) - ac_l)
                eal = jnp.where(lo1, jnp.exp(al0), jnp.exp(al1))
                st_ref[pj] = eal * prevT + _dot_nn(BbT, Wm.astype(BF))

    gn = G * N
    return pl.pallas_call(
        body, name=name, grid=(nc,),
        in_specs=[pl.BlockSpec((L, inner), lambda c: (c, 0)), pl.BlockSpec((L, gn), lambda c: (c, inner // gn)),
                  pl.BlockSpec((L, gn), lambda c: (c, inner // gn + 1)),
                  pl.BlockSpec((L, LANES), lambda c: (c, 0)), pl.BlockSpec((None, 8, LANES), lambda c: (l, 0, 0))],
        out_specs=[pl.BlockSpec((L, inner), lambda c: (c, 0)), pl.BlockSpec((1, G * PP, N, LANES), lambda c: (c, 0, 0, 0))],
        out_shape=[jax.ShapeDtypeStruct((S, inner), BF), jax.ShapeDtypeStruct((nc, G * PP, N, LANES), F32)],
        scratch_shapes=[pltpu.VMEM((G * PP, N, LANES), F32)],
        compiler_params=_cp(("arbitrary",)),
    )(xbc, xbc, xbc, dt_raw, par)


def _ssd_bwd(name, dy, xbc, dt_raw, states, par, l, heads):
    S, inner, N, gw, PP, L, nc = _ssd_dims(xbc, heads)
    G = SSM_GROUPS

    def body(dy_ref, x_ref, b_ref, c_ref, dtr_ref, par_ref, st_in_ref, d_ref, ddt_ref, dpar_ref, dst_ref):
        @pl.when(pl.program_id(0) == 0)
        def _():
            dst_ref[...] = jnp.zeros_like(dst_ref)
            dpar_ref[...] = jnp.zeros_like(dpar_ref)

        tri, pre, dt_all, A_row, a_all, acum_all, acumT = _ssd_chunk_scalars(dtr_ref, par_ref, L)
        lane = lax.broadcasted_iota(jnp.int32, (L, LANES), 1)
        lane1 = lax.broadcasted_iota(jnp.int32, (1, LANES), 1)
        rowl = lax.broadcasted_iota(jnp.int32, (L, LANES), 0)
        lo = lane < SSM_HEADDIM
        lo1 = lane1 < SSM_HEADDIM
        triT = lax.broadcasted_iota(jnp.int32, (L, L), 0) <= lax.broadcasted_iota(jnp.int32, (L, L), 1)
        sel_r = lax.broadcasted_iota(jnp.int32, (3 * LANES, LANES), 0)
        sel_c = lax.broadcasted_iota(jnp.int32, (3 * LANES, LANES), 1)
        dac_all = jnp.zeros((L, LANES), F32)
        xds_all = jnp.zeros((L, LANES), F32)
        dD_row = jnp.zeros((1, LANES), F32)

        def half_sums(v):
            return (jnp.sum(jnp.where(lo1, v, 0.0), axis=1, keepdims=True), jnp.sum(jnp.where(lo1, 0.0, v), axis=1, keepdims=True))

        def dot2(v, sel):
            hi = v.astype(BF)
            return _dot_nn(hi, sel) + _dot_nn((v - hi.astype(F32)).astype(BF), sel)

        for pj in range(G * PP):
            g, j = divmod(pj, PP)
            if j == 0:
                Bb = b_ref[:, g * N:(g + 1) * N]
                Cb = c_ref[:, g * N:(g + 1) * N]
                CbT = Cb.astype(F32).T.astype(BF)
                Gm = _dot_nt(Cb, Bb)
                GmT = _dot_nt(Bb, Cb)
                dG = jnp.zeros((L, L), F32)
                dGT = jnp.zeros((L, L), F32)
                dBacc = jnp.zeros((L, N), F32)
                dCacc = jnp.zeros((L, N), F32)
            h0, h1 = 2 * pj, 2 * pj + 1
            to_h0 = (sel_r < LANES) | ((sel_r >= 2 * LANES) & (sel_r < 2 * LANES + SSM_HEADDIM))
            sel3 = jnp.where(sel_c == jnp.where(to_h0, h0, h1), 1.0, 0.0).astype(BF)
            sel1 = sel3[2 * LANES:]
            sl = slice(pj * LANES, (pj + 1) * LANES)
            x = x_ref[:, sl].astype(F32)
            dyv = dy_ref[:, sl].astype(F32)
            dt_l = jnp.where(lo, dt_all[:, h0:h0 + 1], dt_all[:, h1:h1 + 1])
            ac0 = acum_all[:, h0:h0 + 1]
            ac1 = acum_all[:, h1:h1 + 1]
            r0 = acumT[h0:h0 + 1, :]
            r1 = acumT[h1:h1 + 1, :]
            ac_l = jnp.where(lo, ac0, ac1)
            E0 = jnp.exp(jnp.where(tri, ac0 - r0, -1e30))
            E1 = jnp.exp(jnp.where(tri, ac1 - r1, -1e30))
            E0T = jnp.exp(jnp.where(triT, r0 - ac0, -1e30))
            E1T = jnp.exp(jnp.where(triT, r1 - ac1, -1e30))
            xd = x * dt_l
            xdb = xd.astype(BF)
            M0 = Gm * E0
            M1 = Gm * E1
            ea_l = jnp.exp(ac_l)
            al0 = ac0[L - 1:L, :]
            al1 = ac1[L - 1:L, :]
            dte_l = jnp.exp(jnp.where(lo, al0, al1) - ac_l)
            Wm = xd * dte_l
            prevT = st_in_ref[0, pj]
            prevTb = prevT.astype(BF)
            P = _dot_nn(Cb, prevTb)
            D_l = jnp.where(lo1, par_ref[2:3, h0:h0 + 1], par_ref[2:3, h1:h1 + 1])
            dx = D_l * dyv
            dD0, dD1 = half_sums(jnp.sum(dyv * x, axis=0, keepdims=True))
            dyb = dyv.astype(BF)
            dy0b = jnp.where(lo, dyv, 0.0).astype(BF)
            dy1b = jnp.where(lo, 0.0, dyv).astype(BF)
            dM0 = _dot_nt(dy0b, xdb)
            dM1 = _dot_nt(dy1b, xdb)
            dM0T = _dot_nt(xdb, dy0b)
            dM1T = _dot_nt(xdb, dy1b)
            M0T = GmT * E0T
            M1T = GmT * E1T
            dxd = jnp.where(lo, _dot_nn(M0T.astype(BF), dyb), _dot_nn(M1T.astype(BF), dyb))
            dG = dG + dM0 * E0 + dM1 * E1
            dGT = dGT + dM0T * E0T + dM1T * E1T
            z0 = dM0 * M0 - dM0T * M0T
            z1 = dM1 * M1 - dM1T * M1T
            dP = dyv * ea_l
            dPb = dP.astype(BF)
            dCacc = dCacc + _dot_nt(dPb, prevTb)
            dprevT = _dot_nn(CbT, dPb)
            dnewT = dst_ref[pj]
            dnewTb = dnewT.astype(BF)
            e0 = jnp.exp(al0)
            e1 = jnp.exp(al1)
            dprevT = dprevT + jnp.where(lo1, e0, e1) * dnewT
            u0, u1 = half_sums(jnp.sum(dnewT * prevT, axis=0, keepdims=True))
            dW = _dot_nn(Bb, dnewTb)
            dBacc = dBacc + _dot_nt(Wm.astype(BF), dnewTb)
            dxd = dxd + dW * dte_l
            tt = dW * Wm
            t0, t1 = half_sums(jnp.sum(tt, axis=0, keepdims=True))
            dal0 = u0 * e0 + t0
            dal1 = u1 * e1 + t1
            dac_all = dac_all + dot2(jnp.concatenate([z0, z1, dP * P - tt], axis=1), sel3)
            dac_all = dac_all + jnp.where(rowl == L - 1, jnp.where(lane == h0, dal0, 0.0) + jnp.where(lane == h1, dal1, 0.0), 0.0)
            dx = dx + dxd * dt_l
            xds_all = xds_all + dot2(dxd * x, sel1)
            dst_ref[pj] = dprevT
            d_ref[:, sl] = dx.astype(d_ref.dtype)
            dD_row = dD_row + jnp.where(lane1 == h0, dD0, 0.0) + jnp.where(lane1 == h1, dD1, 0.0)
            if j == PP - 1:
                d_ref[:, inner + g * N:inner + (g + 1) * N] = (dBacc + _dot_nn(dGT.astype(BF), Cb)).astype(d_ref.dtype)
                d_ref[:, inner + (G + g) * N:inner + (G + g + 1) * N] = (dCacc + _dot_nn(dG.astype(BF), Bb)).astype(d_ref.dtype)

        row_i = lax.broadcasted_iota(jnp.int32, (L, L), 0)
        col_i = lax.broadcasted_iota(jnp.int32, (L, L), 1)
        da_all = _tri_matmul((row_i <= col_i).astype(BF), dac_all)
        real = lane < heads
        ddt_all = da_all * A_row + xds_all
        draw = jnp.where(real, ddt_all * _sigmoid(pre), 0.0)
        ddt_ref[...] = draw
        dpar_ref[0:1, :] += jnp.sum(draw, axis=0, keepdims=True)
        dpar_ref[1:2, :] += jnp.sum(jnp.where(real, da_all * a_all, 0.0), axis=0, keepdims=True)
        dpar_ref[2:3, :] += dD_row

    gn = G * N
    conv_dim = xbc.shape[1]
    rev = lambda c: nc - 1 - c
    return pl.pallas_call(
        body, name=name, grid=(nc,),
        in_specs=[pl.BlockSpec((L, inner), lambda c: (rev(c), 0)), pl.BlockSpec((L, inner), lambda c: (rev(c), 0)),
                  pl.BlockSpec((L, gn), lambda c: (rev(c), inner // gn)), pl.BlockSpec((L, gn), lambda c: (rev(c), inner // gn + 1)),
                  pl.BlockSpec((L, LANES), lambda c: (rev(c), 0)), pl.BlockSpec((None, 8, LANES), lambda c: (l, 0, 0)),
                  pl.BlockSpec((1, G * PP, N, LANES), lambda c: (rev(c), 0, 0, 0))],
        out_specs=[pl.BlockSpec((L, conv_dim), lambda c: (rev(c), 0)), pl.BlockSpec((L, LANES), lambda c: (rev(c), 0)),
                   pl.BlockSpec((8, LANES), lambda c: (0, 0))],
        out_shape=[jax.ShapeDtypeStruct((S, conv_dim), BF), jax.ShapeDtypeStruct((S, LANES), F32), jax.ShapeDtypeStruct((8, LANES), F32)],
        scratch_shapes=[pltpu.VMEM((G * PP, N, LANES), F32)],
        compiler_params=_cp(("arbitrary",)),
    )(dy, xbc, xbc, xbc, dt_raw, par, states)


def _adamw(g, w, m, v):
    m2 = ADAM_B1 * m + (1.0 - ADAM_B1) * g
    v2 = ADAM_B2 * v + (1.0 - ADAM_B2) * (g * g)
    m_hat = m2 / (1.0 - ADAM_B1 ** ADAM_STEP)
    v_hat = v2 / (1.0 - ADAM_B2 ** ADAM_STEP)
    delta = -ADAM_LR * (m_hat / (jnp.sqrt(v_hat) + ADAM_EPS) + ADAM_WD * w)
    return delta, m2, v2


def _flat_tile(R):
    return _pick(R, (FLAT_ROW_TILE, 1024, 512, 256, 128, 64, 32, 16, 8))


def _sum_adam(name, lands, off, w, m, v):
    depth, r, c = w.shape
    cap = max(16, (4 * 1024 * 1024) // (N_DEV * c * 2))
    row_tiles = [t for t in (512, 256, 128, 64, 32, 16) if r % t == 0 and off % t == 0 and t <= cap]
    if row_tiles:
        tr, tc = row_tiles[0], c
        ob = off // tr
        n_t = r // tr
        spec = pl.BlockSpec((None, tr, c), lambda l, t: (l, t, 0))
        land_specs = [pl.BlockSpec((N_DEV, tr, c), lambda l, t, i=i: (0, jnp.where(l == i, ob + t, ob), 0)) for i in range(depth)]
    else:
        assert off == 0 and lands[0].shape[1] == r and c % LANES == 0
        tc = LANES
        n_t = c // tc
        spec = pl.BlockSpec((None, r, tc), lambda l, t: (l, 0, t))
        land_specs = [pl.BlockSpec((N_DEV, r, tc), lambda l, t, i=i: (0, 0, jnp.where(l == i, t, 0))) for i in range(depth)]

    def body(*refs):
        land_refs = refs[:depth]
        w_ref, m_ref, v_ref, g_ref, d_ref, m2_ref, v2_ref = refs[depth:]
        l = pl.program_id(0)
        for i in range(depth):
            @pl.when(l == i)
            def _(i=i):
                g = land_refs[i][0].astype(F32)
                for k in range(1, N_DEV):
                    g = g + land_refs[i][k].astype(F32)
                g_ref[...] = g
                d_ref[...], m2_ref[...], v2_ref[...] = _adamw(g, w_ref[...], m_ref[...], v_ref[...])

    return pl.pallas_call(
        body, name=name, grid=(depth, n_t),
        in_specs=land_specs + [spec, spec, spec],
        out_specs=[spec] * 4, out_shape=[jax.ShapeDtypeStruct((depth, r, c), F32)] * 4,
        compiler_params=_cp(("arbitrary", "arbitrary")),
    )(*lands, w, m, v)


def _sum8(name, parts):
    R = parts.shape[1]
    TR = _flat_tile(R)

    def body(p_ref, g_ref):
        g = p_ref[0]
        for k in range(1, N_DEV):
            g = g + p_ref[k]
        g_ref[...] = g

    return pl.pallas_call(
        body, name=name, grid=(R // TR,),
        in_specs=[pl.BlockSpec((N_DEV, TR, LANES), lambda i: (0, i, 0))],
        out_specs=pl.BlockSpec((TR, LANES), lambda i: (i, 0)), out_shape=jax.ShapeDtypeStruct((R, LANES), F32),
        compiler_params=_cp(("parallel",)),
    )(parts)


def _adam_flat(name, g, w, m, v):
    R = w.shape[0]
    TR = _flat_tile(R)

    def body(g_ref, w_ref, m_ref, v_ref, d_ref, m2_ref, v2_ref):
        d_ref[...], m2_ref[...], v2_ref[...] = _adamw(g_ref[...], w_ref[...], m_ref[...], v_ref[...])

    spec = pl.BlockSpec((TR, LANES), lambda i: (i, 0))
    return pl.pallas_call(
        body, name=name, grid=(R // TR,), in_specs=[spec] * 4, out_specs=[spec] * 3,
        out_shape=[jax.ShapeDtypeStruct((R, LANES), F32)] * 3, compiler_params=_cp(("parallel",)),
    )(g, w, m, v)


PART_ROWS = 16


def _nrows(shape):
    n = 1
    for s in shape:
        n *= s
    r = -(-n // LANES)
    return -(-r // PART_ROWS) * PART_ROWS


def _as_rows(a):
    n = a.size
    r = _nrows(a.shape)
    f = a.reshape(-1)
    if r * LANES != n:
        f = jnp.concatenate([f, jnp.zeros((r * LANES - n,), a.dtype)])
    return f.reshape(r, LANES)


def _pack(arrs, mult=PART_ROWS):
    cat = jnp.concatenate([_as_rows(a) for a in arrs], axis=0)
    pad = (-cat.shape[0]) % mult
    if pad:
        cat = jnp.concatenate([cat, jnp.zeros((pad, LANES), cat.dtype)], axis=0)
    return cat


def _unpack(flat, shapes):
    lead = flat.shape[:-2]
    out = []
    o = 0
    for shp in shapes:
        n = 1
        for s in shp:
            n *= s
        r = _nrows(shp)
        blk = flat[..., o:o + r, :].reshape(lead + (r * LANES,))
        out.append(blk[..., :n].reshape(lead + tuple(shp)))
        o += r
    return out


def _full_from_shards(st):
    return st.reshape(st.shape[0] * st.shape[1], st.shape[2])


def _shards_from_full(full):
    return full.reshape(N_DEV, full.shape[0] // N_DEV, full.shape[1])


def _ffn_fwd(tag, h, g, wgT, wuT, wd, dep=None):
    xn = _rms_fwd(tag + "_rms", h, g, dep=dep)
    a, b, hmid = _ffn_up(tag + "_up", xn, wgT, wuT)
    hout = _mm(tag + "_down", hmid, wd, out_dtype=F32, res=h, alpha=0.5)
    return hout, (xn, a, b, hmid)


def _ffn_bwd(tag, dh_out, h, g, wgT, wuT, wd, saved, dep=None):
    xn, a, b, hmid = saved
    da, db = _ffn_dact(tag + "_d_act", dh_out, wd, a, b, dep=dep)
    d_wd = _mm(tag + "_d_wd", hmid, dh_out, ta=True, alpha=0.5)
    d_wgT = _mm(tag + "_d_wg", da, xn, ta=True)
    d_wuT = _mm(tag + "_d_wu", db, xn, ta=True)
    dxn = _mm(tag + "_d_xn_g", da, wgT, out_dtype=F32)
    dh, dg = _mm_rms_bwd(tag + "_d_xn_u", db, wuT, dxn, h, g, dh_out)
    return dh, dg, d_wgT, d_wuT, d_wd


SEG_NAMES = ['scb', 'scc', 'scx', 'z', 'xbc', 'dt', 'ga', 'gm']
PERM = ['z', 'scb', 'scc', 'scx', 'ga', 'gm', 'xbc']


def _seg_layout(dims):
    D, inner, conv_dim, H = dims[:4]
    widths = dict(zip(SEG_NAMES, [D, D, D, inner, conv_dim, H, D, D]))
    offs, o = {}, 0
    for n in SEG_NAMES:
        offs[n] = (o, widths[n])
        o += widths[n]
    poffs, o = {}, 0
    for n in PERM:
        poffs[n] = (o, widths[n])
        o += widths[n]
    return offs, poffs


def _perm_w_in(w_inT, dims):
    offs, _ = _seg_layout(dims)
    wp = jnp.concatenate([w_inT[offs[n][0]:offs[n][0] + offs[n][1]] for n in PERM], axis=0)
    o, w = offs['dt']
    wdt = jnp.concatenate([w_inT[o:o + w], jnp.zeros((LANES - w, w_inT.shape[1]), w_inT.dtype)], axis=0)
    return wp, wdt


def _unperm_d_w_in(d_wp, d_wdt, dims):
    offs, poffs = _seg_layout(dims)
    H = dims[3]
    return jnp.concatenate([d_wdt[:H] if n == 'dt' else d_wp[poffs[n][0]:poffs[n][0] + poffs[n][1]] for n in SEG_NAMES], axis=0)


def _mixer_fwd(h, W, dims, dep=None):
    H, Ksc, Km = dims[3:]
    l = W['l']
    _, poffs = _seg_layout(dims)

    def seg(n):
        o, w = poffs[n]
        assert o % w == 0
        return (proj, w, o // w)

    u = _rms_fwd("mix_rms", h, W['mix_norm'], dep=dep)
    proj = _mm("inproj", u, W['w_in_p'], tb=True)
    dt_raw = _mm("inproj_dt", u, W['w_dt'], tb=True, out_dtype=F32)
    v = _scconv_fwd("scconv_f", proj, poffs['scb'][0], poffs['scc'][0], poffs['scx'][0], W['sc_taps'], Ksc, l)
    ya = _mm("sc_out", v, W['sc_w_out'])
    xbc = _mconv_fwd("mconv_f", proj, poffs['xbc'][0], W['m_taps'], Km, W['m_conv_b'], l)
    y, states = _ssd_fwd("ssd_f", xbc, dt_raw, W['ssd_par'], l, H)
    yn = _gnorm_fwd("gnorm_f", y, seg('z'), W['m_norm'])
    ym = _mm("m_out", yn, W['m_w_out'])
    merged = _merge_fwd("merge_f", seg('ga'), seg('gm'), ya, ym)
    hout = _mm("w_o", merged, W['w_o'], out_dtype=F32, res=h)
    return hout, (u, proj, dt_raw, v, ya, xbc, y, states, yn, ym, merged)


def _mixer_bwd(dh_out, h, W, dims, saved, dep=None):
    u, proj, dt_raw, v, ya, xbc, y, states, yn, ym, merged = saved
    H, Ksc, Km = dims[3:]
    l = W['l']
    _, poffs = _seg_layout(dims)

    def seg(n):
        o, w = poffs[n]
        return (proj, w, o // w)

    g = {}
    dmerged = _mm("d_merged", dh_out, W['w_o'], tb=True, dep=dep)
    g['w_o'] = _mm("d_w_o", merged, dh_out, ta=True)
    dga, dgm, dya, dym = _merge_bwd("merge_b", dmerged, seg('ga'), seg('gm'), ya, ym)
    g['sc_w_out'] = _mm("d_sc_w_out", v, dya, ta=True)
    dv = _mm("d_v", dya, W['sc_w_out'], tb=True)
    g['m_w_out'] = _mm("d_m_w_out", yn, dym, ta=True)
    dyn = _mm("d_yn", dym, W['m_w_out'], tb=True)
    dy, dz, d_mnorm = _gnorm_bwd("gnorm_b", dyn, y, seg('z'), W['m_norm'])
    g['m_norm'] = d_mnorm.reshape(-1)
    dxbc_post, ddt, dpar = _ssd_bwd("ssd_b", dy, xbc, dt_raw, states, W['ssd_par'], l, H)
    g['m_dt_bias'] = dpar[0, :H]
    g['m_A_log'] = dpar[1, :H]
    g['m_D'] = dpar[2, :H]
    dxbc, d_mcw, d_mcb = _mconv_bwd("mconv_b", dxbc_post, proj, poffs['xbc'][0], W['m_taps'], Km, W['m_conv_b'], l)
    g['m_conv_w'] = d_mcw[:Km]
    g['m_conv_b'] = d_mcb.reshape(-1)
    dscb, dscc, dscx, d_scw = _scconv_bwd("scconv_b", dv, proj, poffs['scb'][0], poffs['scc'][0], poffs['scx'][0],
                                          W['sc_taps'], Ksc, l)
    g['sc_conv_w'] = d_scw[:Ksc]
    dproj = jnp.concatenate([dz, dscb, dscc, dscx, dga, dgm, dxbc], axis=1)
    du = _mm("d_u_main", dproj, W['w_in_p'], out_dtype=F32)
    dh, dg = _mm_rms_bwd("d_u_dt", ddt, W['w_dt'], du, h, W['mix_norm'], dh_out)
    d_wp = _mm("d_w_in_main", dproj, u, ta=True)
    d_wdt = _mm("d_w_in_dt", ddt, u, ta=True)
    g['w_in'] = _unperm_d_w_in(d_wp, d_wdt, dims)
    g['mix_norm'] = dg.reshape(-1)
    return dh, g


def _ple_layer_fwd(h, p_l, W):
    xn = _rms_fwd("ple_rms", h, W['ple_norm'])
    gpre = _mm("ple_gate", xn, W['ple_w_gate'])
    pp = _mm("ple_proj", p_l, W['ple_w_proj'], tb=True)
    hout = _ple_fwd("ple_f", h, gpre, pp)
    return hout, (xn, gpre, pp)


def _ple_layer_bwd(dh_out, h, p_l, W, saved, dep=None):
    xn, gpre, pp = saved
    g = {}
    dgpre, dpp = _ple_bwd("ple_b", dh_out, gpre, pp, dep=dep)
    g['ple_w_proj'] = _mm("d_ple_proj", dpp, p_l, ta=True)
    g['ple_w_gate'] = _mm("d_ple_gate", xn, dgpre, ta=True)
    dh, dg = _mm_rms_bwd("d_ple_xn", dgpre, W['ple_w_gate'], None, h, W['ple_norm'], dh_out, tb=True)
    g['ple_norm'] = dg.reshape(-1)
    return dh, g


def kernel(x, p, ffn1_norm, ffn1_wg, ffn1_wu, ffn1_wd, mix_norm, w_in, sc_conv_w, sc_w_out, m_conv_w, m_conv_b, m_dt_bias, m_A_log, m_D, m_norm, m_w_out, w_o, ffn2_norm, ffn2_wg, ffn2_wu, ffn2_wd, ple_norm, ple_w_gate, ple_w_proj, final_norm, loss_target, m_ffn1_norm, m_ffn1_wg, m_ffn1_wu, m_ffn1_wd, m_mix_norm, m_w_in, m_sc_conv_w, m_sc_w_out, m_m_conv_w, m_m_conv_b, m_m_dt_bias, m_m_A_log, m_m_D, m_m_norm, m_m_w_out, m_w_o, m_ffn2_norm, m_ffn2_wg, m_ffn2_wu, m_ffn2_wd, m_ple_norm, m_ple_w_gate, m_ple_w_proj, m_final_norm, v_ffn1_norm, v_ffn1_wg, v_ffn1_wu, v_ffn1_wd, v_mix_norm, v_w_in, v_sc_conv_w, v_sc_w_out, v_m_conv_w, v_m_conv_b, v_m_dt_bias, v_m_A_log, v_m_D, v_m_norm, v_m_w_out, v_w_o, v_ffn2_norm, v_ffn2_wg, v_ffn2_wu, v_ffn2_wd, v_ple_norm, v_ple_w_gate, v_ple_w_proj, v_final_norm):
    args = (x, p, ffn1_norm, ffn1_wg, ffn1_wu, ffn1_wd, mix_norm, w_in, sc_conv_w, sc_w_out, m_conv_w, m_conv_b, m_dt_bias, m_A_log, m_D, m_norm, m_w_out, w_o, ffn2_norm, ffn2_wg, ffn2_wu, ffn2_wd, ple_norm, ple_w_gate, ple_w_proj, final_norm, loss_target, m_ffn1_norm, m_ffn1_wg, m_ffn1_wu, m_ffn1_wd, m_mix_norm, m_w_in, m_sc_conv_w, m_sc_w_out, m_m_conv_w, m_m_conv_b, m_m_dt_bias, m_m_A_log, m_m_D, m_m_norm, m_m_w_out, m_w_o, m_ffn2_norm, m_ffn2_wg, m_ffn2_wu, m_ffn2_wd, m_ple_norm, m_ple_w_gate, m_ple_w_proj, m_final_norm, v_ffn1_norm, v_ffn1_wg, v_ffn1_wu, v_ffn1_wd, v_mix_norm, v_w_in, v_sc_conv_w, v_sc_w_out, v_m_conv_w, v_m_conv_b, v_m_dt_bias, v_m_A_log, v_m_D, v_m_norm, v_m_w_out, v_w_o, v_ffn2_norm, v_ffn2_wg, v_ffn2_wu, v_ffn2_wd, v_ple_norm, v_ple_w_gate, v_ple_w_proj, v_final_norm)
    names = ARG_NAMES + ['m_' + n for n in WEIGHTS] + ['v_' + n for n in WEIGHTS]
    A = dict(zip(names, args))
    depth = ffn1_norm.shape[0]
    me = 4 * lax.axis_index("x") + 2 * lax.axis_index("y") + lax.axis_index("c")

    dims = (x.shape[-1], m_norm.shape[1], m_conv_b.shape[1], m_dt_bias.shape[1], sc_conv_w.shape[1], m_conv_w.shape[1])
    kind = dict(BIG)

    def work(n, prefix=''):
        return jnp.swapaxes(A[prefix + n], 1, 2) if kind[n] == 'col' else A[prefix + n]

    wb = {n: work(n).astype(BF) for n, _ in BIG}
    srcs = [[wb[ms[0]] if len(ms) == 1 else jnp.concatenate([wb[n] for n in ms], axis=1) for ms in stage] for stage in STAGES]
    conv_g = _unpack(_exchange("gather_conv_taps", _pack([A[n] for n in CONVW]), True), [A[n].shape for n in CONVW])
    taps = {}
    for n, st in zip(CONVW, conv_g):
        taps[n] = _pad_taps(jnp.transpose(st, (1, 2, 0, 3)).reshape(depth, st.shape[2], N_DEV * st.shape[3]))
    ssd_par = _ssd_params(m_dt_bias, m_A_log, m_D)
    small3 = {n: A[n].reshape(depth, 1, -1) for n in SMALL}

    def stage_weights(W, s, l, lands):
        for ms, land, src in zip(STAGES[s], lands, srcs[s]):
            land = lax.dynamic_update_slice(land, src[l][None], (me, 0, 0))
            off = 0
            for n in ms:
                r = wb[n].shape[1]
                W[n] = _full_from_shards(land if len(ms) == 1 else land[:, off:off + r])
                off += r
        if s == 1:
            W['w_in_p'], W['w_dt'] = _perm_w_in(W.pop('w_in'), dims)

    flight = {}

    via_sibling = {(0, 0), (0, 1)}

    def begin_layer(l, dep):
        for s in range(len(STAGES)):
            rels = NEAR_PEERS if (l, s) in via_sibling else ALL_PEERS
            sems, lands, dep = _xchg_begin(f"gather_begin{l}{'abc'[s]}", srcs[s], l, dep, rels)
            flight[(l, s)] = (sems, lands)
        return dep

    def end_stage(W, l, s, after):
        sems, lands = flight.pop((l, s))
        tag = f"{l}{'abc'[s]}"
        if (l, s) in via_sibling:
            lands = _xchg_end("gather_end" + tag, srcs[s], lands, sems, l, after, NEAR_PEERS)
            rsems, lands = _relay_begin("gather_relay" + tag, lands)
            lands = _relay_end("gather_relayed" + tag, lands, rsems, after)
        else:
            lands = _xchg_end("gather_end" + tag, srcs[s], lands, sems, l, after)
        stage_weights(W, s, l, lands)

    tok = begin_layer(0, taps['sc_conv_w'])
    h = x[0]
    saved = []
    layers = []
    for l in range(depth):
        W = {n: (small3[n], l) for n in SMALL}
        W.update(l=l, sc_taps=taps['sc_conv_w'], m_taps=taps['m_conv_w'], m_conv_b=small3['m_conv_b'], ssd_par=ssd_par)
        layers.append(W)
        end_stage(W, l, 0, tok if l == 0 else h)
        h1, s1 = _ffn_fwd("ffn1", h, W['ffn1_norm'], W['ffn1_wg'], W['ffn1_wu'], W['ffn1_wd'])
        end_stage(W, l, 1, h1)
        tok = begin_layer(l + 1, W['w_dt']) if l + 1 < depth else None
        h2, s2 = _mixer_fwd(h1, W, dims, dep=tok)
        end_stage(W, l, 2, h2)
        h3, s3 = _ffn_fwd("ffn2", h2, W['ffn2_norm'], W['ffn2_wg'], W['ffn2_wu'], W['ffn2_wd'])
        h4, s4 = _ple_layer_fwd(h3, p[l, 0], W)
        saved.append((h, h1, h2, h3, s1, s2, s3, s4))
        h = h4

    dh, loss_row, d_final = _loss_head("loss_head", h, final_norm, loss_target[0])
    loss = lax.psum(loss_row[0, 0], ("x", "y", "c"))

    def send_bufs(g, s):
        return [jnp.concatenate([_shards_from_full(g[n]) for n in ms], axis=1) if len(ms) > 1
                else _shards_from_full(g[ms[0]]) for ms in STAGES[s]]

    grads = [None] * depth
    pending = []

    def send_stage(g, l, s, dep):
        send = send_bufs(g, s)
        sems, lands, tok = _xchg_begin(f"scatter_begin{l}{'abc'[s]}", send, None, dep)
        pending.append((l, s, send, lands, sems))
        return tok

    tok = loss.reshape(1, 1)
    for l in reversed(range(depth)):
        W = layers[l]
        h0, h1, h2, h3, s1, s2, s3, s4 = saved[l]
        g = {}
        dh, g4 = _ple_layer_bwd(dh, h3, p[l, 0], W, s4, dep=tok)
        g.update(g4)
        dh, dg, d_wg, d_wu, d_wd = _ffn_bwd("ffn2", dh, h2, W['ffn2_norm'], W['ffn2_wg'], W['ffn2_wu'], W['ffn2_wd'], s3)
        g.update(ffn2_norm=dg.reshape(-1), ffn2_wg=d_wg, ffn2_wu=d_wu, ffn2_wd=d_wd)
        tok = send_stage(g, l, 2, dh)
        dh, g2 = _mixer_bwd(dh, h1, W, dims, s2, dep=tok)
        g.update(g2)
        tok = send_stage(g, l, 1, dh)
        dh, dg, d_wg, d_wu, d_wd = _ffn_bwd("ffn1", dh, h0, W['ffn1_norm'], W['ffn1_wg'], W['ffn1_wu'], W['ffn1_wd'], s1, dep=tok)
        g.update(ffn1_norm=dg.reshape(-1), ffn1_wg=d_wg, ffn1_wu=d_wu, ffn1_wd=d_wd)
        grads[l] = g
        tok = send_stage(g, l, 0, dh)
    grad_x = dh[None]

    g_lands = [[None] * len(STAGES) for _ in range(depth)]
    big_res = [{}, {}, {}, {}]

    def finish(entries, after):
        for l, s, send, lands, sems in entries:
            got = _xchg_end(f"scatter_end{l}{'abc'[s]}", send, lands, sems, None, after)
            after = got[0]
            g_lands[l][s] = [lax.dynamic_update_slice(o, lax.dynamic_slice_in_dim(b, me, 1, axis=0), (me, 0, 0)) for o, b in zip(got, send)]
        return after

    def adam_stages(stages):
        res = None
        for s in stages:
            for gi, ms in enumerate(STAGES[s]):
                off = 0
                for n in ms:
                    res = _sum_adam("adamw_" + n, [g_lands[l][s][gi] for l in range(depth)], off, work(n), work(n, 'm_'), work(n, 'v_'))
                    for k in range(4):
                        big_res[k][n] = jnp.swapaxes(res[k], 1, 2) if kind[n] == 'col' else res[k]
                    off += wb[n].shape[1]
        return res[0]

    after = finish(pending[:-1], tok)
    after = adam_stages(range(1, len(STAGES)))
    after = finish(pending[-1:], after)
    adam_stages([0])

    small_names = SMALL + CONVW
    small_parts = [jnp.stack([grads[l][n] for l in range(depth)]) for n in small_names] + [d_final.reshape(-1)]
    small_sum = _sum8("sum_small", _exchange("gather_small_grads", _pack(small_parts), True, dep=after))
    sg = dict(zip(small_names + ['final_norm'], _unpack(small_sum, [a.shape for a in small_parts])))
    for n in CONVW:
        c = A[n].shape[-1]
        sg[n] = lax.dynamic_slice_in_dim(sg[n], me * c, c, axis=2)
    s_order = small_names + ['final_norm']
    s_shapes = [sg[n].shape for n in s_order]
    s_out = _adam_flat("adamw_small", _pack([sg[n] for n in s_order]), _pack([A[n] for n in s_order]),
                       _pack([A['m_' + n] for n in s_order]), _pack([A['v_' + n] for n in s_order]))
    small_res = [sg] + [dict(zip(s_order, _unpack(flat, s_shapes))) for flat in s_out]

    outs = [loss, grad_x]
    for k in range(4):
        for n in WEIGHTS:
            outs.append(big_res[k][n] if n in big_res[k] else small_res[k][n])
    return tuple(outs)
```

```python
import functools

import jax
import jax.numpy as jnp
from jax import lax
from jax.experimental import pallas as pl
from jax.experimental.pallas import tpu as pltpu

BF = jnp.bfloat16
F32 = jnp.float32

EPS = 1e-6
N_DEV = 8
LANES = 128
SSM_GROUPS = 4
SSM_HEADDIM = 64
SSM_CHUNK = 128
HALO = 16
VMEM_LIMIT = 56 * 1024 * 1024
FLAT_ROW_TILE = 2048

ADAM_LR = 0.001
ADAM_B1 = 0.9
ADAM_B2 = 0.999
ADAM_EPS = 1e-08
ADAM_WD = 0.01
ADAM_STEP = 10

MESH = pl.DeviceIdType.MESH

ARG_NAMES = ['x', 'p', 'ffn1_norm', 'ffn1_wg', 'ffn1_wu', 'ffn1_wd', 'mix_norm', 'w_in', 'sc_conv_w', 'sc_w_out', 'm_conv_w', 'm_conv_b', 'm_dt_bias', 'm_A_log', 'm_D', 'm_norm', 'm_w_out', 'w_o', 'ffn2_norm', 'ffn2_wg', 'ffn2_wu', 'ffn2_wd', 'ple_norm', 'ple_w_gate', 'ple_w_proj', 'final_norm', 'loss_target']
WEIGHTS = ARG_NAMES[2:26]
BIG = [('ffn1_wg', 'col'), ('ffn1_wu', 'col'), ('ffn1_wd', 'row'), ('w_in', 'col'), ('sc_w_out', 'row'),
       ('m_w_out', 'row'), ('w_o', 'row'), ('ffn2_wg', 'col'), ('ffn2_wu', 'col'), ('ffn2_wd', 'row'),
       ('ple_w_gate', 'row'), ('ple_w_proj', 'col')]
CONVW = ['sc_conv_w', 'm_conv_w']
SMALL = ['ffn1_norm', 'mix_norm', 'm_conv_b', 'm_dt_bias', 'm_A_log', 'm_D', 'm_norm', 'ffn2_norm', 'ple_norm']


def _pick(n, cands):
    for c in cands:
        if n % c == 0:
            return c
    return n


def _cp(sem):
    return pltpu.CompilerParams(dimension_semantics=sem, vmem_limit_bytes=VMEM_LIMIT)


def _sigmoid(x):
    return 1.0 / (1.0 + jnp.exp(-x))


def _softplus(x):
    return jnp.maximum(x, 0.0) + jnp.log(1.0 + jnp.exp(-jnp.abs(x)))


def _exchange(name, x, gather, dep=None):
    slab = x.shape if gather else x.shape[1:]

    def body(x_ref, *rest):
        o_ref, send_sems, recv_sems, local_sem = rest[-4:]
        mx, my, mc = lax.axis_index("x"), lax.axis_index("y"), lax.axis_index("c")
        me = 4 * mx + 2 * my + mc

        def src_for(k):
            return x_ref if gather else x_ref.at[k]

        local = pltpu.make_async_copy(src_for(me), o_ref.at[me], local_sem)
        local.start()
        sends = []
        peers = []
        for r in range(1, N_DEV):
            px = (mx + ((r >> 2) & 1)) % 2
            py = (my + ((r >> 1) & 1)) % 2
            pc = (mc + (r & 1)) % 2
            peer = 4 * px + 2 * py + pc
            peers.append(peer)
            cp = pltpu.make_async_remote_copy(
                src_ref=src_for(peer), dst_ref=o_ref.at[me], send_sem=send_sems.at[r - 1], recv_sem=recv_sems.at[r - 1],
                device_id=(px, py, pc), device_id_type=MESH)
            cp.start()
            sends.append(cp)
        for r in range(1, N_DEV):
            peer = peers[r - 1]
            pltpu.make_async_remote_copy(
                src_ref=src_for(peer), dst_ref=o_ref.at[peer], send_sem=send_sems.at[r - 1], recv_sem=recv_sems.at[r - 1],
                device_id=(mx, my, mc), device_id_type=MESH).wait_recv()
        for cp in sends:
            cp.wait_send()
        local.wait()

    return pl.pallas_call(
        body, name=name,
        out_shape=jax.ShapeDtypeStruct((N_DEV,) + tuple(slab), x.dtype),
        in_specs=[pl.BlockSpec(memory_space=pltpu.HBM)] + ([] if dep is None else [pl.BlockSpec(memory_space=pl.ANY)]),
        out_specs=pl.BlockSpec(memory_space=pltpu.HBM),
        scratch_shapes=[pltpu.SemaphoreType.DMA((N_DEV - 1,)), pltpu.SemaphoreType.DMA((N_DEV - 1,)), pltpu.SemaphoreType.DMA],
    )(*([x] if dep is None else [x, dep]))


STAGES = [[['ffn1_wd'], ['ffn1_wg'], ['ffn1_wu']],
          [['w_in'], ['sc_w_out', 'w_o', 'ple_w_gate', 'm_w_out']],
          [['ffn2_wd'], ['ffn2_wg'], ['ffn2_wu'], ['ple_w_proj']]]
_HBM = pl.BlockSpec(memory_space=pltpu.HBM)
_SEM = pl.BlockSpec(memory_space=pltpu.SEMAPHORE)
_ANY = pl.BlockSpec(memory_space=pl.ANY)
_EFFECT = pltpu.SideEffectType.DATAFLOW_SIDE_EFFECTING


def _peer_list():
    mx, my, mc = lax.axis_index("x"), lax.axis_index("y"), lax.axis_index("c")
    out = []
    for r in range(1, N_DEV):
        px = (mx + ((r >> 2) & 1)) % 2
        py = (my + ((r >> 1) & 1)) % 2
        pc = (mc + (r & 1)) % 2
        out.append((px, py, pc, 4 * px + 2 * py + pc))
    return 4 * mx + 2 * my + mc, out


ALL_PEERS = tuple(range(1, N_DEV))
NEAR_PEERS = (1, 2, 4, 6)
RELAYED = (2, 4, 6)


def _xchg_copy(src_refs, land_refs, send_sems, recv_sems, layer, i, r, peer, dst_slab):
    px, py, pc, pidx = peer
    n = len(src_refs)
    src = src_refs[i].at[layer] if layer is not None else src_refs[i].at[pidx]
    return pltpu.make_async_remote_copy(
        src_ref=src, dst_ref=land_refs[i].at[dst_slab], send_sem=send_sems.at[r * n + i], recv_sem=recv_sems.at[r * n + i],
        device_id=(px, py, pc), device_id_type=MESH)


def _own_copy(src_refs, land_refs, send_sems, layer, i, r, me):
    n = len(src_refs)
    src = src_refs[i].at[layer] if layer is not None else src_refs[i].at[me]
    return pltpu.make_async_copy(src, land_refs[i].at[me], send_sems.at[r * n + i])


def _xchg_begin(name, srcs, layer, dep, rels=ALL_PEERS):
    n = len(srcs)
    slabs = [tuple(s.shape[1:]) for s in srcs]
    ncp = n * (len(rels) + 1)

    def body(*refs):
        src_refs, land_refs = refs[:n], refs[n:2 * n]
        send_sems, recv_sems = refs[2 * n + 1], refs[2 * n + 2]
        token = refs[-1]
        me, peers = _peer_list()
        for ri, r in enumerate(rels):
            for i in range(n):
                _xchg_copy(src_refs, land_refs, send_sems, recv_sems, layer, i, ri, peers[r - 1], me).start()
        for i in range(n):
            _own_copy(src_refs, land_refs, send_sems, layer, i, len(rels), me).start()
        token[...] = jnp.zeros_like(token)

    lands = [pltpu.with_memory_space_constraint(lax.empty((N_DEV,) + sl, s.dtype), pltpu.HBM) for sl, s in zip(slabs, srcs)]
    out = pl.pallas_call(
        body, name=name,
        out_shape=(pltpu.SemaphoreType.DMA((ncp,)), pltpu.SemaphoreType.DMA((ncp,)),
                   *[pltpu.HBM((N_DEV,) + sl, s.dtype) for sl, s in zip(slabs, srcs)], jax.ShapeDtypeStruct((8, LANES), F32)),
        in_specs=[_HBM] * (2 * n) + [_ANY],
        out_specs=(_SEM, _SEM, *[_HBM] * n, pl.BlockSpec(memory_space=pltpu.VMEM)),
        input_output_aliases={n + i: 2 + i for i in range(n)},
        compiler_params=pltpu.CompilerParams(has_side_effects=_EFFECT),
    )(*[pltpu.with_memory_space_constraint(s, pltpu.HBM) for s in srcs], *lands, dep)
    return (out[0], out[1]), list(out[2:2 + n]), out[-1]


def _xchg_end(name, srcs, lands, sems, layer, after, rels=ALL_PEERS):
    n = len(srcs)

    def body(*refs):
        src_refs, land_refs = refs[:n], refs[n:2 * n]
        send_sems, recv_sems = refs[2 * n], refs[2 * n + 1]
        me, peers = _peer_list()
        for ri, r in enumerate(rels):
            for i in range(n):
                cp = _xchg_copy(src_refs, land_refs, send_sems, recv_sems, layer, i, ri, peers[r - 1], peers[r - 1][3])
                cp.wait_send()
                cp.wait_recv()
        for i in range(n):
            _own_copy(src_refs, land_refs, send_sems, layer, i, len(rels), me).wait()

    out = pl.pallas_call(
        body, name=name,
        out_shape=tuple(pltpu.HBM(l.shape, l.dtype) for l in lands),
        in_specs=[_HBM] * (2 * n) + [_SEM, _SEM, _ANY], out_specs=tuple([_HBM] * n),
        input_output_aliases={n + i: i for i in range(n)},
        compiler_params=pltpu.CompilerParams(has_side_effects=_EFFECT),
    )(*[pltpu.with_memory_space_constraint(s, pltpu.HBM) for s in srcs], *lands, sems[0], sems[1], after)
    return list(out)


def _relay_copy(land_refs, send_sems, recv_sems, i, qi, slab, sibling):
    n = len(land_refs)
    return pltpu.make_async_remote_copy(
        src_ref=land_refs[i].at[slab], dst_ref=land_refs[i].at[slab], send_sem=send_sems.at[qi * n + i],
        recv_sem=recv_sems.at[qi * n + i], device_id=sibling[:3], device_id_type=MESH)


def _relay_begin(name, lands):
    n = len(lands)
    ncp = n * len(RELAYED)

    def body(*refs):
        land_refs = refs[:n]
        send_sems, recv_sems = refs[n], refs[n + 1]
        me, peers = _peer_list()
        for qi, q in enumerate(RELAYED):
            for i in range(n):
                _relay_copy(land_refs, send_sems, recv_sems, i, qi, peers[q - 1][3], peers[0]).start()

    out = pl.pallas_call(
        body, name=name,
        out_shape=(pltpu.SemaphoreType.DMA((ncp,)), pltpu.SemaphoreType.DMA((ncp,)), *[pltpu.HBM(l.shape, l.dtype) for l in lands]),
        in_specs=[_HBM] * n, out_specs=(_SEM, _SEM, *[_HBM] * n),
        input_output_aliases={i: 2 + i for i in range(n)},
        compiler_params=pltpu.CompilerParams(has_side_effects=_EFFECT),
    )(*lands)
    return (out[0], out[1]), list(out[2:])


def _relay_end(name, lands, sems, after):
    n = len(lands)

    def body(*refs):
        land_refs = refs[:n]
        send_sems, recv_sems = refs[n], refs[n + 1]
        me, peers = _peer_list()
        for qi, q in enumerate(RELAYED):
            for i in range(n):
                _relay_copy(land_refs, send_sems, recv_sems, i, qi, peers[q - 1][3], peers[0]).wait_send()
                _relay_copy(land_refs, send_sems, recv_sems, i, qi, peers[q][3], peers[0]).wait_recv()

    out = pl.pallas_call(
        body, name=name,
        out_shape=tuple(pltpu.HBM(l.shape, l.dtype) for l in lands),
        in_specs=[_HBM] * n + [_SEM, _SEM, _ANY], out_specs=tuple([_HBM] * n),
        input_output_aliases={i: i for i in range(n)},
        compiler_params=pltpu.CompilerParams(has_side_effects=_EFFECT),
    )(*lands, sems[0], sems[1], after)
    return list(out)


MM_TILES = (1024, 1408, 512, 256, 128)
MM_OPERAND_BYTES = 24 * 1024 * 1024


def _mm(name, a, b, *, ta=False, tb=False, out_dtype=None, res=None, alpha=1.0, dep=None):
    out_dtype = out_dtype or BF
    M, K = (a.shape[1], a.shape[0]) if ta else a.shape
    N = b.shape[0] if tb else b.shape[1]
    assert (b.shape[1] if tb else b.shape[0]) == K, (name, a.shape, b.shape)
    tm = _pick(M, MM_TILES)
    tn = _pick(N, MM_TILES)
    per_k = 2 * (tm * a.dtype.itemsize + tn * b.dtype.itemsize)
    tk = [t for t in sorted({K, 4096, 2816, 2560, 2048, 1408, 1024, 512, 256, 128}, reverse=True)
          if K % t == 0 and (t * per_k <= MM_OPERAND_BYTES or t == 128)][0]
    nk = K // tk
    a_spec = pl.BlockSpec((tk, tm), lambda i, j, k: (k, i)) if ta else pl.BlockSpec((tm, tk), lambda i, j, k: (i, k))
    b_spec = pl.BlockSpec((tn, tk), lambda i, j, k: (j, k)) if tb else pl.BlockSpec((tk, tn), lambda i, j, k: (k, j))
    dn = (((0 if ta else 1,), (1 if tb else 0,)), ((), ()))
    has_res = res is not None
    n_dep = 0 if dep is None else 1

    def body(*refs):
        a_ref, b_ref = refs[:2]
        r_ref = refs[2] if has_res else None
        o_ref = refs[2 + has_res + n_dep]

        def finish(v):
            if alpha != 1.0:
                v = v * alpha
            if has_res:
                v = r_ref[...] + v
            o_ref[...] = v.astype(o_ref.dtype)

        part = lax.dot_general(a_ref[...].astype(BF), b_ref[...].astype(BF), dn, preferred_element_type=F32)
        if nk == 1:
            finish(part)
            return
        acc = refs[-1]
        k = pl.program_id(2)

        @pl.when(k == 0)
        def _():
            acc[...] = part

        @pl.when((k > 0) & (k < nk - 1))
        def _():
            acc[...] += part

        @pl.when(k == nk - 1)
        def _():
            finish(acc[...] + part)

    in_specs = [a_spec, b_spec]
    args = [a, b]
    if has_res:
        in_specs.append(pl.BlockSpec((tm, tn), lambda i, j, k: (i, j)))
        args.append(res)
    if dep is not None:
        in_specs.append(_ANY)
        args.append(dep)
    return pl.pallas_call(
        body, name=name, grid=(M // tm, N // tn, nk),
        in_specs=in_specs, out_specs=pl.BlockSpec((tm, tn), lambda i, j, k: (i, j)),
        out_shape=jax.ShapeDtypeStruct((M, N), out_dtype),
        scratch_shapes=[pltpu.VMEM((tm, tn), F32)] if nk > 1 else [],
        compiler_params=_cp(("parallel", "parallel", "arbitrary")),
    )(*args)


def _mm_rms_bwd(name, a, b, acc_in, h, g, dh_res, *, tb=False):
    M, K = a.shape
    N = b.shape[0] if tb else b.shape[1]
    assert (b.shape[1] if tb else b.shape[0]) == K and h.shape == (M, N), (name, a.shape, b.shape)
    tm = _pick(M, (FFN_TOKEN_TILE, 256, 128))
    dn = (((1,), (1 if tb else 0,)), ((), ()))
    has_acc = acc_in is not None
    g_arr, g_row = _prow(g) if isinstance(g, tuple) else (_prow(g), None)

    def body(*refs):
        a_ref, b_ref = refs[:2]
        c_ref = refs[2] if has_acc else None
        h_ref, g_ref, r_ref, o_ref, dg_ref = refs[2 + has_acc:]
        d = lax.dot_general(a_ref[...].astype(BF), b_ref[...].astype(BF), dn, preferred_element_type=F32)
        if has_acc:
            d = c_ref[...] + d
        x = h_ref[...]
        r = lax.rsqrt(jnp.mean(x * x, axis=-1, keepdims=True) + EPS)
        xhat = x * r
        dxhat = d * g_ref[...]
        o_ref[...] = r_ref[...] + r * (dxhat - xhat * jnp.mean(dxhat * xhat, axis=-1, keepdims=True))

        @pl.when(pl.program_id(0) == 0)
        def _():
            dg_ref[...] = jnp.zeros_like(dg_ref)

        dg_ref[...] += jnp.sum(d * xhat, axis=0, keepdims=True)

    row = pl.BlockSpec((tm, N), lambda i: (i, 0))
    gspec = pl.BlockSpec((1, N), lambda i: (0, 0)) if g_row is None else pl.BlockSpec((None, 1, N), lambda i: (g_row, 0, 0))
    in_specs = [pl.BlockSpec((tm, K), lambda i: (i, 0)), pl.BlockSpec(b.shape, lambda i: (0, 0))]
    args = [a, b]
    if has_acc:
        in_specs.append(row)
        args.append(acc_in)
    return pl.pallas_call(
        body, name=name, grid=(M // tm,),
        in_specs=in_specs + [row, gspec, row], out_specs=[row, pl.BlockSpec((1, N), lambda i: (0, 0))],
        out_shape=[jax.ShapeDtypeStruct((M, N), F32), jax.ShapeDtypeStruct((1, N), F32)],
        compiler_params=_cp(("arbitrary",)),
    )(*args, h, g_arr, dh_res)


def _ew(name, fn, tiled, params, outs, accs=(), tile=256, dep=None):
    tiled = [t if isinstance(t, tuple) else (t, t.shape[1], 0) for t in tiled]
    params = [q if isinstance(q, tuple) else (q, None) for q in params]
    S = tiled[0][0].shape[0]
    T = _pick(S, (tile, 128, 64, 32, 16))
    n_in = len(tiled) + len(params)
    n_dep = 0 if dep is None else 1

    def body(*refs):
        fn(pl.program_id(0) == 0, *refs[:n_in], *refs[n_in + n_dep:])

    in_specs = [pl.BlockSpec((T, w), lambda i, cb=cb: (i, cb)) for _, w, cb in tiled]
    for q, row in params:
        if row is None:
            in_specs.append(pl.BlockSpec(q.shape, lambda i: (0, 0)))
        else:
            in_specs.append(pl.BlockSpec((None, 1, q.shape[2]), lambda i, row=row: (row, 0, 0)))
    args = [t[0] for t in tiled] + [q[0] for q in params]
    if dep is not None:
        in_specs.append(pl.BlockSpec(memory_space=pl.ANY))
        args.append(dep)
    out_specs = [pl.BlockSpec((T, w), lambda i: (i, 0)) for w, _ in outs]
    out_specs += [pl.BlockSpec(shp, lambda i: (0, 0)) for shp, _ in accs]
    out_shape = [jax.ShapeDtypeStruct((S, w), dt) for w, dt in outs]
    out_shape += [jax.ShapeDtypeStruct(shp, dt) for shp, dt in accs]
    res = pl.pallas_call(
        body, name=name, grid=(S // T,), in_specs=in_specs, out_specs=out_specs, out_shape=out_shape,
        compiler_params=_cp(("arbitrary",)),
    )(*args)
    return res


def _prow(g):
    return g if isinstance(g, tuple) else g.reshape(1, -1)


def _rms_fwd(name, h, g, dep=None):
    def fn(first, h_ref, g_ref, o_ref):
        x = h_ref[...]
        r = lax.rsqrt(jnp.mean(x * x, axis=-1, keepdims=True) + EPS)
        o_ref[...] = (x * r * g_ref[...]).astype(o_ref.dtype)

    return _ew(name, fn, [h], [_prow(g)], [(h.shape[1], BF)], dep=dep)[0]


FFN_TOKEN_TILE = 512


def _ffn_up(name, xn, wgT, wuT, dep=None):
    S, D = xn.shape
    FF = wgT.shape[0]
    tm = _pick(S, (FFN_TOKEN_TILE, 256, 128))
    tn = _pick(FF, MM_TILES)
    n_dep = 0 if dep is None else 1

    def body(x_ref, g_ref, u_ref, *rest):
        a_ref, b_ref, h_ref = rest[n_dep:]
        x = x_ref[...]
        a = _dot_nt(x, g_ref[...])
        b = _dot_nt(x, u_ref[...])
        a_ref[...] = a.astype(BF)
        b_ref[...] = b.astype(BF)
        h_ref[...] = (a * _sigmoid(a) * b).astype(BF)

    wspec = pl.BlockSpec((tn, D), lambda j, i: (j, 0))
    ospec = pl.BlockSpec((tm, tn), lambda j, i: (i, j))
    return pl.pallas_call(
        body, name=name, grid=(FF // tn, S // tm),
        in_specs=[pl.BlockSpec((tm, D), lambda j, i: (i, 0)), wspec, wspec] + ([] if dep is None else [_ANY]),
        out_specs=[ospec] * 3, out_shape=[jax.ShapeDtypeStruct((S, FF), BF)] * 3,
        compiler_params=_cp(("parallel", "arbitrary")),
    )(*([xn, wgT, wuT] + ([] if dep is None else [dep])))


def _ffn_dact(name, dh, wd, a, b, dep=None):
    S, D = dh.shape
    FF = wd.shape[0]
    tm = _pick(S, (FFN_TOKEN_TILE, 256, 128))
    tn = _pick(FF, MM_TILES)
    n_dep = 0 if dep is None else 1

    def body(d_ref, w_ref, a_ref, b_ref, *rest):
        da_ref, db_ref = rest[n_dep:]
        d = 0.5 * _dot_nt(d_ref[...].astype(BF), w_ref[...])
        av = a_ref[...].astype(F32)
        s = _sigmoid(av)
        da_ref[...] = (d * b_ref[...].astype(F32) * (s * (1.0 + av * (1.0 - s)))).astype(BF)
        db_ref[...] = (d * av * s).astype(BF)

    tspec = pl.BlockSpec((tm, tn), lambda j, i: (i, j))
    return pl.pallas_call(
        body, name=name, grid=(FF // tn, S // tm),
        in_specs=[pl.BlockSpec((tm, D), lambda j, i: (i, 0)), pl.BlockSpec((tn, D), lambda j, i: (j, 0)), tspec, tspec]
        + ([] if dep is None else [_ANY]),
        out_specs=[tspec] * 2, out_shape=[jax.ShapeDtypeStruct((S, FF), BF)] * 2,
        compiler_params=_cp(("parallel", "arbitrary")),
    )(*([dh, wd, a, b] + ([] if dep is None else [dep])))


def _merge_fwd(name, ga, gm, ya, ym):
    def fn(first, ga_ref, gm_ref, ya_ref, ym_ref, o_ref):
        o = _sigmoid(ga_ref[...].astype(F32)) * ya_ref[...].astype(F32) + _sigmoid(gm_ref[...].astype(F32)) * ym_ref[...].astype(F32)
        o_ref[...] = o.astype(o_ref.dtype)

    return _ew(name, fn, [ga, gm, ya, ym], [], [(ya.shape[1], BF)])[0]


def _merge_bwd(name, dmerged, ga, gm, ya, ym):
    W = ya.shape[1]

    def fn(first, d_ref, ga_ref, gm_ref, ya_ref, ym_ref, dga_ref, dgm_ref, dya_ref, dym_ref):
        d = d_ref[...].astype(F32)
        sa = _sigmoid(ga_ref[...].astype(F32))
        sm = _sigmoid(gm_ref[...].astype(F32))
        dga_ref[...] = (d * ya_ref[...].astype(F32) * sa * (1.0 - sa)).astype(BF)
        dgm_ref[...] = (d * ym_ref[...].astype(F32) * sm * (1.0 - sm)).astype(BF)
        dya_ref[...] = (d * sa).astype(BF)
        dym_ref[...] = (d * sm).astype(BF)

    return _ew(name, fn, [dmerged, ga, gm, ya, ym], [], [(W, BF)] * 4)


def _gnorm_fwd(name, y, z, w):
    W = y.shape[1]
    gw = W // SSM_GROUPS

    def fn(first, y_ref, z_ref, w_ref, o_ref):
        for g in range(SSM_GROUPS):
            sl = slice(g * gw, (g + 1) * gw)
            zz = z_ref[:, sl].astype(F32)
            t = y_ref[:, sl].astype(F32) * (zz * _sigmoid(zz))
            r = lax.rsqrt(jnp.mean(t * t, axis=-1, keepdims=True) + EPS)
            o_ref[:, sl] = (t * r * w_ref[:, sl]).astype(o_ref.dtype)

    return _ew(name, fn, [y, z], [_prow(w)], [(W, BF)])[0]


def _gnorm_bwd(name, dyn, y, z, w):
    W = y.shape[1]
    gw = W // SSM_GROUPS

    def fn(first, d_ref, y_ref, z_ref, w_ref, dy_ref, dz_ref, dw_ref):
        @pl.when(first)
        def _():
            dw_ref[...] = jnp.zeros_like(dw_ref)

        for g in range(SSM_GROUPS):
            sl = slice(g * gw, (g + 1) * gw)
            zz = z_ref[:, sl].astype(F32)
            yy = y_ref[:, sl].astype(F32)
            d = d_ref[:, sl].astype(F32)
            s = _sigmoid(zz)
            sz = zz * s
            t = yy * sz
            r = lax.rsqrt(jnp.mean(t * t, axis=-1, keepdims=True) + EPS)
            that = t * r
            dthat = d * w_ref[:, sl]
            dt = r * (dthat - that * jnp.mean(dthat * that, axis=-1, keepdims=True))
            dw_ref[:, sl] += jnp.sum(d * that, axis=0, keepdims=True)
            dy_ref[:, sl] = (dt * sz).astype(BF)
            dz_ref[:, sl] = (dt * yy * (s * (1.0 + zz * (1.0 - s)))).astype(BF)

    return _ew(name, fn, [dyn, y, z], [_prow(w)], [(W, BF), (W, BF)], [((1, W), F32)])


def _ple_fwd(name, h, gpre, pp):
    def fn(first, h_ref, g_ref, p_ref, o_ref):
        o_ref[...] = h_ref[...] + _sigmoid(g_ref[...].astype(F32)) * p_ref[...].astype(F32)

    return _ew(name, fn, [h, gpre, pp], [], [(h.shape[1], F32)])[0]


def _ple_bwd(name, dh, gpre, pp, dep=None):
    W = dh.shape[1]

    def fn(first, d_ref, g_ref, p_ref, dg_ref, dp_ref):
        d = d_ref[...]
        s = _sigmoid(g_ref[...].astype(F32))
        dg_ref[...] = (d * p_ref[...].astype(F32) * s * (1.0 - s)).astype(BF)
        dp_ref[...] = (d * s).astype(BF)

    return _ew(name, fn, [dh, gpre, pp], [], [(W, BF), (W, BF)], dep=dep)


def _loss_head(name, h, g, target):
    D = h.shape[1]

    def fn(first, h_ref, t_ref, g_ref, dh_ref, loss_ref, dg_ref):
        x = h_ref[...]
        r = lax.rsqrt(jnp.mean(x * x, axis=-1, keepdims=True) + EPS)
        xhat = x * r
        err = xhat * g_ref[...] - t_ref[...]
        part = 0.5 * jnp.sum(jnp.mean(err * err, axis=-1, keepdims=True), axis=0, keepdims=True)
        dy = err * (1.0 / D)
        dxhat = dy * g_ref[...]
        dh_ref[...] = r * (dxhat - xhat * jnp.mean(dxhat * xhat, axis=-1, keepdims=True))

        @pl.when(first)
        def _():
            loss_ref[...] = jnp.zeros_like(loss_ref)
            dg_ref[...] = jnp.zeros_like(dg_ref)

        loss_ref[...] += jnp.broadcast_to(part, loss_ref.shape)
        dg_ref[...] += jnp.sum(dy * xhat, axis=0, keepdims=True)

    return _ew(name, fn, [h, target], [_prow(g)], [(D, F32)], [((1, LANES), F32), ((1, D), F32)])


def _conv_specs(S, C, offs, l):
    T = _pick(S, (512, 256, 128, 64, 32, 16))
    Ct = [c for c in (512, 256, 128) if C % c == 0 and all(o % c == 0 for o in offs)][0]
    per = T // HALO
    last = S // HALO - 1

    def cur(off=0):
        return pl.BlockSpec((T, Ct), lambda j, i: (i, off // Ct + j))

    def prev(off=0):
        return pl.BlockSpec((HALO, Ct), lambda j, i: (jnp.maximum(i * per - 1, 0), off // Ct + j))

    def nxt(off=0):
        return pl.BlockSpec((HALO, Ct), lambda j, i: (jnp.minimum((i + 1) * per, last), off // Ct + j))

    wspec = pl.BlockSpec((None, 8, Ct), lambda j, i: (l, 0, j))
    return T, Ct, cur, prev, nxt, wspec


def _pad_taps(w):
    return jnp.concatenate([w.astype(F32), jnp.zeros((w.shape[0], 8 - w.shape[1], w.shape[2]), F32)], axis=1)


def _causal(cat, w_ref, K, T, lead):
    views = [(pltpu.roll(cat, K - 1 - k, 0) if k < K - 1 else cat)[lead:lead + T] for k in range(K)]
    out = None
    for k in range(K):
        term = w_ref[k:k + 1, :] * views[k]
        out = term if out is None else out + term
    return out, views


def _anticausal(cat, w_ref, K, T):
    out = None
    rows = cat.shape[0]
    for k in range(K):
        o = K - 1 - k
        term = w_ref[k:k + 1, :] * (pltpu.roll(cat, rows - o, 0) if o else cat)[:T]
        out = term if out is None else out + term
    return out


def _scconv_fwd(name, proj, ob, oc, ox, taps, K, l):
    S = proj.shape[0]
    C = taps.shape[2]
    T, Ct, cur, prev, nxt, wspec = _conv_specs(S, C, (ob, oc, ox), l)

    def body(b_ref, c_ref, x_ref, cp_ref, xp_ref, w_ref, o_ref):
        i = pl.program_id(1)
        q = c_ref[...].astype(F32) * x_ref[...].astype(F32)
        qp = jnp.where(i == 0, 0.0, cp_ref[...].astype(F32) * xp_ref[...].astype(F32))
        cat = jnp.concatenate([qp, q], axis=0)
        o_ref[...] = (b_ref[...].astype(F32) * _causal(cat, w_ref, K, T, HALO)[0]).astype(o_ref.dtype)

    return pl.pallas_call(
        body, name=name, grid=(C // Ct, S // T),
        in_specs=[cur(ob), cur(oc), cur(ox), prev(oc), prev(ox), wspec], out_specs=cur(),
        out_shape=jax.ShapeDtypeStruct((S, C), BF), compiler_params=_cp(("parallel", "arbitrary")),
    )(proj, proj, proj, proj, proj, taps)


def _scconv_bwd(name, dv, proj, ob, oc, ox, taps, K, l):
    S = proj.shape[0]
    C = taps.shape[2]
    T, Ct, cur, prev, nxt, wspec = _conv_specs(S, C, (ob, oc, ox), l)
    n_t = S // T

    def body(d_ref, b_ref, c_ref, x_ref, dn_ref, bn_ref, cp_ref, xp_ref, w_ref, db_ref, dc_ref, dx_ref, dw_ref):
        i = pl.program_id(1)
        c = c_ref[...].astype(F32)
        x = x_ref[...].astype(F32)
        d = d_ref[...].astype(F32)
        q = c * x
        qp = jnp.where(i == 0, 0.0, cp_ref[...].astype(F32) * xp_ref[...].astype(F32))
        catq = jnp.concatenate([qp, q], axis=0)
        cv, q_views = _causal(catq, w_ref, K, T, HALO)
        db_ref[...] = (d * cv).astype(BF)
        dcv = d * b_ref[...].astype(F32)
        dcvn = jnp.where(i == n_t - 1, 0.0, dn_ref[...].astype(F32) * bn_ref[...].astype(F32))
        catd = jnp.concatenate([dcv, dcvn], axis=0)
        dq = _anticausal(catd, w_ref, K, T)
        dc_ref[...] = (dq * x).astype(BF)
        dx_ref[...] = (dq * c).astype(BF)

        @pl.when(i == 0)
        def _():
            dw_ref[...] = jnp.zeros_like(dw_ref)

        for k in range(K):
            dw_ref[k:k + 1, :] += jnp.sum(dcv * q_views[k], axis=0, keepdims=True)

    return pl.pallas_call(
        body, name=name, grid=(C // Ct, n_t),
        in_specs=[cur(), cur(ob), cur(oc), cur(ox), nxt(), nxt(ob), prev(oc), prev(ox), wspec],
        out_specs=[cur(), cur(), cur(), pl.BlockSpec((8, Ct), lambda j, i: (0, j))],
        out_shape=[jax.ShapeDtypeStruct((S, C), BF)] * 3 + [jax.ShapeDtypeStruct((8, C), F32)],
        compiler_params=_cp(("parallel", "arbitrary")),
    )(dv, proj, proj, proj, dv, proj, proj, proj, taps)


def _mconv_fwd(name, proj, ox, taps, K, bias, l):
    S = proj.shape[0]
    C = taps.shape[2]
    T, Ct, cur, prev, nxt, wspec = _conv_specs(S, C, (ox,), l)
    bspec = pl.BlockSpec((None, 1, Ct), lambda j, i: (l, 0, j))

    def body(x_ref, xp_ref, w_ref, b_ref, o_ref):
        i = pl.program_id(1)
        xp = jnp.where(i == 0, 0.0, xp_ref[...].astype(F32))
        cat = jnp.concatenate([xp, x_ref[...].astype(F32)], axis=0)
        pre = _causal(cat, w_ref, K, T, HALO)[0] + b_ref[...]
        o_ref[...] = (pre * _sigmoid(pre)).astype(o_ref.dtype)

    return pl.pallas_call(
        body, name=name, grid=(C // Ct, S // T),
        in_specs=[cur(ox), prev(ox), wspec, bspec], out_specs=cur(),
        out_shape=jax.ShapeDtypeStruct((S, C), BF), compiler_params=_cp(("parallel", "arbitrary")),
    )(proj, proj, taps, bias)


def _mconv_bwd(name, dout, proj, ox, taps, K, bias, l):
    S = proj.shape[0]
    C = taps.shape[2]
    T, Ct, cur, prev, nxt, wspec = _conv_specs(S, C, (ox,), l)
    n_t = S // T
    bspec = pl.BlockSpec((None, 1, Ct), lambda j, i: (l, 0, j))

    def body(d_ref, dn_ref, x_ref, xp_ref, xn_ref, w_ref, b_ref, dx_ref, dw_ref, db_ref):
        i = pl.program_id(1)
        xp = jnp.where(i == 0, 0.0, xp_ref[...].astype(F32))
        cat3 = jnp.concatenate([xp, x_ref[...].astype(F32), xn_ref[...].astype(F32)], axis=0)
        pre, x_views = _causal(cat3, w_ref, K, T + HALO, HALO)
        pre = pre + b_ref[...]
        dn = jnp.where(i == n_t - 1, 0.0, dn_ref[...].astype(F32))
        dext = jnp.concatenate([d_ref[...].astype(F32), dn], axis=0)
        s = _sigmoid(pre)
        dpre = dext * (s * (1.0 + pre * (1.0 - s)))
        dx_ref[...] = _anticausal(dpre, w_ref, K, T).astype(BF)
        dcur = dpre[:T]

        @pl.when(i == 0)
        def _():
            dw_ref[...] = jnp.zeros_like(dw_ref)
            db_ref[...] = jnp.zeros_like(db_ref)

        db_ref[...] += jnp.sum(dcur, axis=0, keepdims=True)
        for k in range(K):
            dw_ref[k:k + 1, :] += jnp.sum(dcur * x_views[k][:T], axis=0, keepdims=True)

    return pl.pallas_call(
        body, name=name, grid=(C // Ct, n_t),
        in_specs=[cur(), nxt(), cur(ox), prev(ox), nxt(ox), wspec, bspec],
        out_specs=[cur(), pl.BlockSpec((8, Ct), lambda j, i: (0, j)), pl.BlockSpec((1, Ct), lambda j, i: (0, j))],
        out_shape=[jax.ShapeDtypeStruct((S, C), BF), jax.ShapeDtypeStruct((8, C), F32), jax.ShapeDtypeStruct((1, C), F32)],
        compiler_params=_cp(("parallel", "arbitrary")),
    )(dout, dout, proj, proj, proj, taps, bias)


def _tri_matmul(tri_bf, v):
    hi = v.astype(BF)
    r1 = v - hi.astype(F32)
    mid = r1.astype(BF)
    lo = (r1 - mid.astype(F32)).astype(BF)
    dot = functools.partial(jnp.dot, preferred_element_type=F32)
    return dot(tri_bf, hi) + dot(tri_bf, mid) + dot(tri_bf, lo)


def _dot_nt(a, b):
    return lax.dot_general(a, b, (((1,), (1,)), ((), ())), preferred_element_type=F32)


def _dot_tn(a, b):
    return lax.dot_general(a, b, (((0,), (0,)), ((), ())), preferred_element_type=F32)


def _dot_nn(a, b):
    return jnp.dot(a, b, preferred_element_type=F32)


def _ssd_chunk_scalars(dtr_ref, par_ref, L):
    row_i = lax.broadcasted_iota(jnp.int32, (L, L), 0)
    col_i = lax.broadcasted_iota(jnp.int32, (L, L), 1)
    tri = row_i >= col_i
    pre = dtr_ref[...] + par_ref[0:1, :]
    dt_all = _softplus(pre)
    A_row = -jnp.exp(par_ref[1:2, :])
    a_all = dt_all * A_row
    acum_all = _tri_matmul(tri.astype(BF), a_all)
    return tri, pre, dt_all, A_row, a_all, acum_all, acum_all.T


def _ssd_dims(xbc, heads):
    S, conv_dim = xbc.shape
    inner = heads * SSM_HEADDIM
    N = (conv_dim - inner) // (2 * SSM_GROUPS)
    gw = inner // SSM_GROUPS
    PP = gw // LANES
    L = min(SSM_CHUNK, S)
    assert N == LANES and gw % LANES == 0 and inner % (SSM_GROUPS * N) == 0 and S % L == 0
    return S, inner, N, gw, PP, L, S // L


def _ssd_params(dt_bias, A_log, Dp):
    depth, H = dt_bias.shape
    rows = jnp.stack([dt_bias, A_log, Dp], axis=1).astype(F32)
    rows = jnp.concatenate([rows, jnp.zeros((depth, 3, LANES - H), F32)], axis=2)
    return jnp.concatenate([rows, jnp.zeros((depth, 5, LANES), F32)], axis=1)


def _ssd_fwd(name, xbc, dt_raw, par, l, heads):
    S, inner, N, gw, PP, L, nc = _ssd_dims(xbc, heads)
    G = SSM_GROUPS

    def body(x_ref, b_ref, c_ref, dtr_ref, par_ref, y_ref, st_out_ref, st_ref):
        @pl.when(pl.program_id(0) == 0)
        def _():
            st_ref[...] = jnp.zeros_like(st_ref)

        tri, pre, dt_all, A_row, a_all, acum_all, acumT = _ssd_chunk_scalars(dtr_ref, par_ref, L)
        lane = lax.broadcasted_iota(jnp.int32, (L, LANES), 1)
        lane1 = lax.broadcasted_iota(jnp.int32, (1, LANES), 1)
        lo = lane < SSM_HEADDIM
        lo1 = lane1 < SSM_HEADDIM
        for g in range(G):
            Bb = b_ref[:, g * N:(g + 1) * N]
            Cb = c_ref[:, g * N:(g + 1) * N]
            BbT = Bb.astype(F32).T.astype(BF)
            Gm = _dot_nt(Cb, Bb)
            for j in range(PP):
                pj = g * PP + j
                h0, h1 = 2 * pj, 2 * pj + 1
                cols = slice(pj * LANES, (pj + 1) * LANES)
                x = x_ref[:, cols].astype(F32)
                dt_l = jnp.where(lo, dt_all[:, h0:h0 + 1], dt_all[:, h1:h1 + 1])
                ac0 = acum_all[:, h0:h0 + 1]
                ac1 = acum_all[:, h1:h1 + 1]
                ac_l = jnp.where(lo, ac0, ac1)
                E0 = jnp.exp(jnp.where(tri, ac0 - acumT[h0:h0 + 1, :], -1e30))
                E1 = jnp.exp(jnp.where(tri, ac1 - acumT[h1:h1 + 1, :], -1e30))
                xd = x * dt_l
                xdb = xd.astype(BF)
                yd = jnp.where(lo, _dot_nn((Gm * E0).astype(BF), xdb), _dot_nn((Gm * E1).astype(BF), xdb))
                prevT = st_ref[pj]
                st_out_ref[0, pj] = prevT
                P = _dot_nn(Cb, prevT.astype(BF))
                D_l = jnp.where(lo1, par_ref[2:3, h0:h0 + 1], par_ref[2:3, h1:h1 + 1])
                y_ref[:, cols] = (yd + P * jnp.exp(ac_l) + D_l * x).astype(y_ref.dtype)
                al0 = ac0[L - 1:L, :]
                al1 = ac1[L - 1:L, :]
                Wm = xd * jnp.exp(jnp.where(lo, al0, al1) - ac_l)
                eal = jnp.where(lo1, jnp.exp(al0), jnp.exp(al1))
                st_ref[pj] = eal * prevT + _dot_nn(BbT, Wm.astype(BF))

    gn = G * N
    return pl.pallas_call(
        body, name=name, grid=(nc,),
        in_specs=[pl.BlockSpec((L, inner), lambda c: (c, 0)), pl.BlockSpec((L, gn), lambda c: (c, inner // gn)),
                  pl.BlockSpec((L, gn), lambda c: (c, inner // gn + 1)),
                  pl.BlockSpec((L, LANES), lambda c: (c, 0)), pl.BlockSpec((None, 8, LANES), lambda c: (l, 0, 0))],
        out_specs=[pl.BlockSpec((L, inner), lambda c: (c, 0)), pl.BlockSpec((1, G * PP, N, LANES), lambda c: (c, 0, 0, 0))],
        out_shape=[jax.ShapeDtypeStruct((S, inner), BF), jax.ShapeDtypeStruct((nc, G * PP, N, LANES), F32)],
        scratch_shapes=[pltpu.VMEM((G * PP, N, LANES), F32)],
        compiler_params=_cp(("arbitrary",)),
    )(xbc, xbc, xbc, dt_raw, par)


def _ssd_bwd(name, dy, xbc, dt_raw, states, par, l, heads):
    S, inner, N, gw, PP, L, nc = _ssd_dims(xbc, heads)
    G = SSM_GROUPS

    def body(dy_ref, x_ref, b_ref, c_ref, dtr_ref, par_ref, st_in_ref, d_ref, ddt_ref, dpar_ref, dst_ref):
        @pl.when(pl.program_id(0) == 0)
        def _():
            dst_ref[...] = jnp.zeros_like(dst_ref)
            dpar_ref[...] = jnp.zeros_like(dpar_ref)

        tri, pre, dt_all, A_row, a_all, acum_all, acumT = _ssd_chunk_scalars(dtr_ref, par_ref, L)
        lane = lax.broadcasted_iota(jnp.int32, (L, LANES), 1)
        lane1 = lax.broadcasted_iota(jnp.int32, (1, LANES), 1)
        rowl = lax.broadcasted_iota(jnp.int32, (L, LANES), 0)
        lo = lane < SSM_HEADDIM
        lo1 = lane1 < SSM_HEADDIM
        triT = lax.broadcasted_iota(jnp.int32, (L, L), 0) <= lax.broadcasted_iota(jnp.int32, (L, L), 1)
        sel_r = lax.broadcasted_iota(jnp.int32, (3 * LANES, LANES), 0)
        sel_c = lax.broadcasted_iota(jnp.int32, (3 * LANES, LANES), 1)
        dac_all = jnp.zeros((L, LANES), F32)
        xds_all = jnp.zeros((L, LANES), F32)
        dD_row = jnp.zeros((1, LANES), F32)

        def half_sums(v):
            return (jnp.sum(jnp.where(lo1, v, 0.0), axis=1, keepdims=True), jnp.sum(jnp.where(lo1, 0.0, v), axis=1, keepdims=True))

        def dot2(v, sel):
            hi = v.astype(BF)
            return _dot_nn(hi, sel) + _dot_nn((v - hi.astype(F32)).astype(BF), sel)

        for pj in range(G * PP):
            g, j = divmod(pj, PP)
            if j == 0:
                Bb = b_ref[:, g * N:(g + 1) * N]
                Cb = c_ref[:, g * N:(g + 1) * N]
                CbT = Cb.astype(F32).T.astype(BF)
                Gm = _dot_nt(Cb, Bb)
                GmT = _dot_nt(Bb, Cb)
                dG = jnp.zeros((L, L), F32)
                dGT = jnp.zeros((L, L), F32)
                dBacc = jnp.zeros((L, N), F32)
                dCacc = jnp.zeros((L, N), F32)
            h0, h1 = 2 * pj, 2 * pj + 1
            to_h0 = (sel_r < LANES) | ((sel_r >= 2 * LANES) & (sel_r < 2 * LANES + SSM_HEADDIM))
            sel3 = jnp.where(sel_c == jnp.where(to_h0, h0, h1), 1.0, 0.0).astype(BF)
            sel1 = sel3[2 * LANES:]
            sl = slice(pj * LANES, (pj + 1) * LANES)
            x = x_ref[:, sl].astype(F32)
            dyv = dy_ref[:, sl].astype(F32)
            dt_l = jnp.where(lo, dt_all[:, h0:h0 + 1], dt_all[:, h1:h1 + 1])
            ac0 = acum_all[:, h0:h0 + 1]
            ac1 = acum_all[:, h1:h1 + 1]
            r0 = acumT[h0:h0 + 1, :]
            r1 = acumT[h1:h1 + 1, :]
            ac_l = jnp.where(lo, ac0, ac1)
            E0 = jnp.exp(jnp.where(tri, ac0 - r0, -1e30))
            E1 = jnp.exp(jnp.where(tri, ac1 - r1, -1e30))
            E0T = jnp.exp(jnp.where(triT, r0 - ac0, -1e30))
            E1T = jnp.exp(jnp.where(triT, r1 - ac1, -1e30))
            xd = x * dt_l
            xdb = xd.astype(BF)
            M0 = Gm * E0
            M1 = Gm * E1
            ea_l = jnp.exp(ac_l)
            al0 = ac0[L - 1:L, :]
            al1 = ac1[L - 1:L, :]
            dte_l = jnp.exp(jnp.where(lo, al0, al1) - ac_l)
            Wm = xd * dte_l
            prevT = st_in_ref[0, pj]
            prevTb = prevT.astype(BF)
            P = _dot_nn(Cb, prevTb)
            D_l = jnp.where(lo1, par_ref[2:3, h0:h0 + 1], par_ref[2:3, h1:h1 + 1])
            dx = D_l * dyv
            dD0, dD1 = half_sums(jnp.sum(dyv * x, axis=0, keepdims=True))
            dyb = dyv.astype(BF)
            dy0b = jnp.where(lo, dyv, 0.0).astype(BF)
            dy1b = jnp.where(lo, 0.0, dyv).astype(BF)
            dM0 = _dot_nt(dy0b, xdb)
            dM1 = _dot_nt(dy1b, xdb)
            dM0T = _dot_nt(xdb, dy0b)
            dM1T = _dot_nt(xdb, dy1b)
            M0T = GmT * E0T
            M1T = GmT * E1T
            dxd = jnp.where(lo, _dot_nn(M0T.astype(BF), dyb), _dot_nn(M1T.astype(BF), dyb))
            dG = dG + dM0 * E0 + dM1 * E1
            dGT = dGT + dM0T * E0T + dM1T * E1T
            z0 = dM0 * M0 - dM0T * M0T
            z1 = dM1 * M1 - dM1T * M1T
            dP = dyv * ea_l
            dPb = dP.astype(BF)
            dCacc = dCacc + _dot_nt(dPb, prevTb)
            dprevT = _dot_nn(CbT, dPb)
            dnewT = dst_ref[pj]
            dnewTb = dnewT.astype(BF)
            e0 = jnp.exp(al0)
            e1 = jnp.exp(al1)
            dprevT = dprevT + jnp.where(lo1, e0, e1) * dnewT
            u0, u1 = half_sums(jnp.sum(dnewT * prevT, axis=0, keepdims=True))
            dW = _dot_nn(Bb, dnewTb)
            dBacc = dBacc + _dot_nt(Wm.astype(BF), dnewTb)
            dxd = dxd + dW * dte_l
            tt = dW * Wm
            t0, t1 = half_sums(jnp.sum(tt, axis=0, keepdims=True))
            dal0 = u0 * e0 + t0
            dal1 = u1 * e1 + t1
            dac_all = dac_all + dot2(jnp.concatenate([z0, z1, dP * P - tt], axis=1), sel3)
            dac_all = dac_all + jnp.where(rowl == L - 1, jnp.where(lane == h0, dal0, 0.0) + jnp.where(lane == h1, dal1, 0.0), 0.0)
            dx = dx + dxd * dt_l
            xds_all = xds_all + dot2(dxd * x, sel1)
            dst_ref[pj] = dprevT
            d_ref[:, sl] = dx.astype(d_ref.dtype)
            dD_row = dD_row + jnp.where(lane1 == h0, dD0, 0.0) + jnp.where(lane1 == h1, dD1, 0.0)
            if j == PP - 1:
                d_ref[:, inner + g * N:inner + (g + 1) * N] = (dBacc + _dot_nn(dGT.astype(BF), Cb)).astype(d_ref.dtype)
                d_ref[:, inner + (G + g) * N:inner + (G + g + 1) * N] = (dCacc + _dot_nn(dG.astype(BF), Bb)).astype(d_ref.dtype)

        row_i = lax.broadcasted_iota(jnp.int32, (L, L), 0)
        col_i = lax.broadcasted_iota(jnp.int32, (L, L), 1)
        da_all = _tri_matmul((row_i <= col_i).astype(BF), dac_all)
        real = lane < heads
        ddt_all = da_all * A_row + xds_all
        draw = jnp.where(real, ddt_all * _sigmoid(pre), 0.0)
        ddt_ref[...] = draw
        dpar_ref[0:1, :] += jnp.sum(draw, axis=0, keepdims=True)
        dpar_ref[1:2, :] += jnp.sum(jnp.where(real, da_all * a_all, 0.0), axis=0, keepdims=True)
        dpar_ref[2:3, :] += dD_row

    gn = G * N
    conv_dim = xbc.shape[1]
    rev = lambda c: nc - 1 - c
    return pl.pallas_call(
        body, name=name, grid=(nc,),
        in_specs=[pl.BlockSpec((L, inner), lambda c: (rev(c), 0)), pl.BlockSpec((L, inner), lambda c: (rev(c), 0)),
                  pl.BlockSpec((L, gn), lambda c: (rev(c), inner // gn)), pl.BlockSpec((L, gn), lambda c: (rev(c), inner // gn + 1)),
                  pl.BlockSpec((L, LANES), lambda c: (rev(c), 0)), pl.BlockSpec((None, 8, LANES), lambda c: (l, 0, 0)),
                  pl.BlockSpec((1, G * PP, N, LANES), lambda c: (rev(c), 0, 0, 0))],
        out_specs=[pl.BlockSpec((L, conv_dim), lambda c: (rev(c), 0)), pl.BlockSpec((L, LANES), lambda c: (rev(c), 0)),
                   pl.BlockSpec((8, LANES), lambda c: (0, 0))],
        out_shape=[jax.ShapeDtypeStruct((S, conv_dim), BF), jax.ShapeDtypeStruct((S, LANES), F32), jax.ShapeDtypeStruct((8, LANES), F32)],
        scratch_shapes=[pltpu.VMEM((G * PP, N, LANES), F32)],
        compiler_params=_cp(("arbitrary",)),
    )(dy, xbc, xbc, xbc, dt_raw, par, states)


def _adamw(g, w, m, v):
    m2 = ADAM_B1 * m + (1.0 - ADAM_B1) * g
    v2 = ADAM_B2 * v + (1.0 - ADAM_B2) * (g * g)
    m_hat = m2 / (1.0 - ADAM_B1 ** ADAM_STEP)
    v_hat = v2 / (1.0 - ADAM_B2 ** ADAM_STEP)
    delta = -ADAM_LR * (m_hat / (jnp.sqrt(v_hat) + ADAM_EPS) + ADAM_WD * w)
    return delta, m2, v2


def _flat_tile(R):
    return _pick(R, (FLAT_ROW_TILE, 1024, 512, 256, 128, 64, 32, 16, 8))


def _sum_adam(name, lands, off, w, m, v):
    depth, r, c = w.shape
    cap = max(16, (4 * 1024 * 1024) // (N_DEV * c * 2))
    row_tiles = [t for t in (512, 256, 128, 64, 32, 16) if r % t == 0 and off % t == 0 and t <= cap]
    if row_tiles:
        tr, tc = row_tiles[0], c
        ob = off // tr
        n_t = r // tr
        spec = pl.BlockSpec((None, tr, c), lambda l, t: (l, t, 0))
        land_specs = [pl.BlockSpec((N_DEV, tr, c), lambda l, t, i=i: (0, jnp.where(l == i, ob + t, ob), 0)) for i in range(depth)]
    else:
        assert off == 0 and lands[0].shape[1] == r and c % LANES == 0
        tc = LANES
        n_t = c // tc
        spec = pl.BlockSpec((None, r, tc), lambda l, t: (l, 0, t))
        land_specs = [pl.BlockSpec((N_DEV, r, tc), lambda l, t, i=i: (0, 0, jnp.where(l == i, t, 0))) for i in range(depth)]

    def body(*refs):
        land_refs = refs[:depth]
        w_ref, m_ref, v_ref, g_ref, d_ref, m2_ref, v2_ref = refs[depth:]
        l = pl.program_id(0)
        for i in range(depth):
            @pl.when(l == i)
            def _(i=i):
                g = land_refs[i][0].astype(F32)
                for k in range(1, N_DEV):
                    g = g + land_refs[i][k].astype(F32)
                g_ref[...] = g
                d_ref[...], m2_ref[...], v2_ref[...] = _adamw(g, w_ref[...], m_ref[...], v_ref[...])

    return pl.pallas_call(
        body, name=name, grid=(depth, n_t),
        in_specs=land_specs + [spec, spec, spec],
        out_specs=[spec] * 4, out_shape=[jax.ShapeDtypeStruct((depth, r, c), F32)] * 4,
        compiler_params=_cp(("arbitrary", "arbitrary")),
    )(*lands, w, m, v)


def _sum8(name, parts):
    R = parts.shape[1]
    TR = _flat_tile(R)

    def body(p_ref, g_ref):
        g = p_ref[0]
        for k in range(1, N_DEV):
            g = g + p_ref[k]
        g_ref[...] = g

    return pl.pallas_call(
        body, name=name, grid=(R // TR,),
        in_specs=[pl.BlockSpec((N_DEV, TR, LANES), lambda i: (0, i, 0))],
        out_specs=pl.BlockSpec((TR, LANES), lambda i: (i, 0)), out_shape=jax.ShapeDtypeStruct((R, LANES), F32),
        compiler_params=_cp(("parallel",)),
    )(parts)


def _adam_flat(name, g, w, m, v):
    R = w.shape[0]
    TR = _flat_tile(R)

    def body(g_ref, w_ref, m_ref, v_ref, d_ref, m2_ref, v2_ref):
        d_ref[...], m2_ref[...], v2_ref[...] = _adamw(g_ref[...], w_ref[...], m_ref[...], v_ref[...])

    spec = pl.BlockSpec((TR, LANES), lambda i: (i, 0))
    return pl.pallas_call(
        body, name=name, grid=(R // TR,), in_specs=[spec] * 4, out_specs=[spec] * 3,
        out_shape=[jax.ShapeDtypeStruct((R, LANES), F32)] * 3, compiler_params=_cp(("parallel",)),
    )(g, w, m, v)


PART_ROWS = 16


def _nrows(shape):
    n = 1
    for s in shape:
        n *= s
    r = -(-n // LANES)
    return -(-r // PART_ROWS) * PART_ROWS


def _as_rows(a):
    n = a.size
    r = _nrows(a.shape)
    f = a.reshape(-1)
    if r * LANES != n:
        f = jnp.concatenate([f, jnp.zeros((r * LANES - n,), a.dtype)])
    return f.reshape(r, LANES)


def _pack(arrs, mult=PART_ROWS):
    cat = jnp.concatenate([_as_rows(a) for a in arrs], axis=0)
    pad = (-cat.shape[0]) % mult
    if pad:
        cat = jnp.concatenate([cat, jnp.zeros((pad, LANES), cat.dtype)], axis=0)
    return cat


def _unpack(flat, shapes):
    lead = flat.shape[:-2]
    out = []
    o = 0
    for shp in shapes:
        n = 1
        for s in shp:
            n *= s
        r = _nrows(shp)
        blk = flat[..., o:o + r, :].reshape(lead + (r * LANES,))
        out.append(blk[..., :n].reshape(lead + tuple(shp)))
        o += r
    return out


def _full_from_shards(st):
    return st.reshape(st.shape[0] * st.shape[1], st.shape[2])


def _shards_from_full(full):
    return full.reshape(N_DEV, full.shape[0] // N_DEV, full.shape[1])


def _ffn_fwd(tag, h, g, wgT, wuT, wd, dep=None):
    xn = _rms_fwd(tag + "_rms", h, g, dep=dep)
    a, b, hmid = _ffn_up(tag + "_up", xn, wgT, wuT)
    hout = _mm(tag + "_down", hmid, wd, out_dtype=F32, res=h, alpha=0.5)
    return hout, (xn, a, b, hmid)


def _ffn_bwd(tag, dh_out, h, g, wgT, wuT, wd, saved, dep=None):
    xn, a, b, hmid = saved
    da, db = _ffn_dact(tag + "_d_act", dh_out, wd, a, b, dep=dep)
    d_wd = _mm(tag + "_d_wd", hmid, dh_out, ta=True, alpha=0.5)
    d_wgT = _mm(tag + "_d_wg", da, xn, ta=True)
    d_wuT = _mm(tag + "_d_wu", db, xn, ta=True)
    dxn = _mm(tag + "_d_xn_g", da, wgT, out_dtype=F32)
    dh, dg = _mm_rms_bwd(tag + "_d_xn_u", db, wuT, dxn, h, g, dh_out)
    return dh, dg, d_wgT, d_wuT, d_wd


SEG_NAMES = ['scb', 'scc', 'scx', 'z', 'xbc', 'dt', 'ga', 'gm']
PERM = ['z', 'scb', 'scc', 'scx', 'ga', 'gm', 'xbc']


def _seg_layout(dims):
    D, inner, conv_dim, H = dims[:4]
    widths = dict(zip(SEG_NAMES, [D, D, D, inner, conv_dim, H, D, D]))
    offs, o = {}, 0
    for n in SEG_NAMES:
        offs[n] = (o, widths[n])
        o += widths[n]
    poffs, o = {}, 0
    for n in PERM:
        poffs[n] = (o, widths[n])
        o += widths[n]
    return offs, poffs


def _perm_w_in(w_inT, dims):
    offs, _ = _seg_layout(dims)
    wp = jnp.concatenate([w_inT[offs[n][0]:offs[n][0] + offs[n][1]] for n in PERM], axis=0)
    o, w = offs['dt']
    wdt = jnp.concatenate([w_inT[o:o + w], jnp.zeros((LANES - w, w_inT.shape[1]), w_inT.dtype)], axis=0)
    return wp, wdt


def _unperm_d_w_in(d_wp, d_wdt, dims):
    offs, poffs = _seg_layout(dims)
    H = dims[3]
    return jnp.concatenate([d_wdt[:H] if n == 'dt' else d_wp[poffs[n][0]:poffs[n][0] + poffs[n][1]] for n in SEG_NAMES], axis=0)


def _mixer_fwd(h, W, dims, dep=None):
    H, Ksc, Km = dims[3:]
    l = W['l']
    _, poffs = _seg_layout(dims)

    def seg(n):
        o, w = poffs[n]
        assert o % w == 0
        return (proj, w, o // w)

    u = _rms_fwd("mix_rms", h, W['mix_norm'], dep=dep)
    proj = _mm("inproj", u, W['w_in_p'], tb=True)
    dt_raw = _mm("inproj_dt", u, W['w_dt'], tb=True, out_dtype=F32)
    v = _scconv_fwd("scconv_f", proj, poffs['scb'][0], poffs['scc'][0], poffs['scx'][0], W['sc_taps'], Ksc, l)
    ya = _mm("sc_out", v, W['sc_w_out'])
    xbc = _mconv_fwd("mconv_f", proj, poffs['xbc'][0], W['m_taps'], Km, W['m_conv_b'], l)
    y, states = _ssd_fwd("ssd_f", xbc, dt_raw, W['ssd_par'], l, H)
    yn = _gnorm_fwd("gnorm_f", y, seg('z'), W['m_norm'])
    ym = _mm("m_out", yn, W['m_w_out'])
    merged = _merge_fwd("merge_f", seg('ga'), seg('gm'), ya, ym)
    hout = _mm("w_o", merged, W['w_o'], out_dtype=F32, res=h)
    return hout, (u, proj, dt_raw, v, ya, xbc, y, states, yn, ym, merged)


def _mixer_bwd(dh_out, h, W, dims, saved, dep=None):
    u, proj, dt_raw, v, ya, xbc, y, states, yn, ym, merged = saved
    H, Ksc, Km = dims[3:]
    l = W['l']
    _, poffs = _seg_layout(dims)

    def seg(n):
        o, w = poffs[n]
        return (proj, w, o // w)

    g = {}
    dmerged = _mm("d_merged", dh_out, W['w_o'], tb=True, dep=dep)
    g['w_o'] = _mm("d_w_o", merged, dh_out, ta=True)
    dga, dgm, dya, dym = _merge_bwd("merge_b", dmerged, seg('ga'), seg('gm'), ya, ym)
    g['sc_w_out'] = _mm("d_sc_w_out", v, dya, ta=True)
    dv = _mm("d_v", dya, W['sc_w_out'], tb=True)
    g['m_w_out'] = _mm("d_m_w_out", yn, dym, ta=True)
    dyn = _mm("d_yn", dym, W['m_w_out'], tb=True)
    dy, dz, d_mnorm = _gnorm_bwd("gnorm_b", dyn, y, seg('z'), W['m_norm'])
    g['m_norm'] = d_mnorm.reshape(-1)
    dxbc_post, ddt, dpar = _ssd_bwd("ssd_b", dy, xbc, dt_raw, states, W['ssd_par'], l, H)
    g['m_dt_bias'] = dpar[0, :H]
    g['m_A_log'] = dpar[1, :H]
    g['m_D'] = dpar[2, :H]
    dxbc, d_mcw, d_mcb = _mconv_bwd("mconv_b", dxbc_post, proj, poffs['xbc'][0], W['m_taps'], Km, W['m_conv_b'], l)
    g['m_conv_w'] = d_mcw[:Km]
    g['m_conv_b'] = d_mcb.reshape(-1)
    dscb, dscc, dscx, d_scw = _scconv_bwd("scconv_b", dv, proj, poffs['scb'][0], poffs['scc'][0], poffs['scx'][0],
                                          W['sc_taps'], Ksc, l)
    g['sc_conv_w'] = d_scw[:Ksc]
    dproj = jnp.concatenate([dz, dscb, dscc, dscx, dga, dgm, dxbc], axis=1)
    du = _mm("d_u_main", dproj, W['w_in_p'], out_dtype=F32)
    dh, dg = _mm_rms_bwd("d_u_dt", ddt, W['w_dt'], du, h, W['mix_norm'], dh_out)
    d_wp = _mm("d_w_in_main", dproj, u, ta=True)
    d_wdt = _mm("d_w_in_dt", ddt, u, ta=True)
    g['w_in'] = _unperm_d_w_in(d_wp, d_wdt, dims)
    g['mix_norm'] = dg.reshape(-1)
    return dh, g


def _ple_layer_fwd(h, p_l, W):
    xn = _rms_fwd("ple_rms", h, W['ple_norm'])
    gpre = _mm("ple_gate", xn, W['ple_w_gate'])
    pp = _mm("ple_proj", p_l, W['ple_w_proj'], tb=True)
    hout = _ple_fwd("ple_f", h, gpre, pp)
    return hout, (xn, gpre, pp)


def _ple_layer_bwd(dh_out, h, p_l, W, saved, dep=None):
    xn, gpre, pp = saved
    g = {}
    dgpre, dpp = _ple_bwd("ple_b", dh_out, gpre, pp, dep=dep)
    g['ple_w_proj'] = _mm("d_ple_proj", dpp, p_l, ta=True)
    g['ple_w_gate'] = _mm("d_ple_gate", xn, dgpre, ta=True)
    dh, dg = _mm_rms_bwd("d_ple_xn", dgpre, W['ple_w_gate'], None, h, W['ple_norm'], dh_out, tb=True)
    g['ple_norm'] = dg.reshape(-1)
    return dh, g


def kernel(x, p, ffn1_norm, ffn1_wg, ffn1_wu, ffn1_wd, mix_norm, w_in, sc_conv_w, sc_w_out, m_conv_w, m_conv_b, m_dt_bias, m_A_log, m_D, m_norm, m_w_out, w_o, ffn2_norm, ffn2_wg, ffn2_wu, ffn2_wd, ple_norm, ple_w_gate, ple_w_proj, final_norm, loss_target, m_ffn1_norm, m_ffn1_wg, m_ffn1_wu, m_ffn1_wd, m_mix_norm, m_w_in, m_sc_conv_w, m_sc_w_out, m_m_conv_w, m_m_conv_b, m_m_dt_bias, m_m_A_log, m_m_D, m_m_norm, m_m_w_out, m_w_o, m_ffn2_norm, m_ffn2_wg, m_ffn2_wu, m_ffn2_wd, m_ple_norm, m_ple_w_gate, m_ple_w_proj, m_final_norm, v_ffn1_norm, v_ffn1_wg, v_ffn1_wu, v_ffn1_wd, v_mix_norm, v_w_in, v_sc_conv_w, v_sc_w_out, v_m_conv_w, v_m_conv_b, v_m_dt_bias, v_m_A_log, v_m_D, v_m_norm, v_m_w_out, v_w_o, v_ffn2_norm, v_ffn2_wg, v_ffn2_wu, v_ffn2_wd, v_ple_norm, v_ple_w_gate, v_ple_w_proj, v_final_norm):
    args = (x, p, ffn1_norm, ffn1_wg, ffn1_wu, ffn1_wd, mix_norm, w_in, sc_conv_w, sc_w_out, m_conv_w, m_conv_b, m_dt_bias, m_A_log, m_D, m_norm, m_w_out, w_o, ffn2_norm, ffn2_wg, ffn2_wu, ffn2_wd, ple_norm, ple_w_gate, ple_w_proj, final_norm, loss_target, m_ffn1_norm, m_ffn1_wg, m_ffn1_wu, m_ffn1_wd, m_mix_norm, m_w_in, m_sc_conv_w, m_sc_w_out, m_m_conv_w, m_m_conv_b, m_m_dt_bias, m_m_A_log, m_m_D, m_m_norm, m_m_w_out, m_w_o, m_ffn2_norm, m_ffn2_wg, m_ffn2_wu, m_ffn2_wd, m_ple_norm, m_ple_w_gate, m_ple_w_proj, m_final_norm, v_ffn1_norm, v_ffn1_wg, v_ffn1_wu, v_ffn1_wd, v_mix_norm, v_w_in, v_sc_conv_w, v_sc_w_out, v_m_conv_w, v_m_conv_b, v_m_dt_bias, v_m_A_log, v_m_D, v_m_norm, v_m_w_out, v_w_o, v_ffn2_norm, v_ffn2_wg, v_ffn2_wu, v_ffn2_wd, v_ple_norm, v_ple_w_gate, v_ple_w_proj, v_final_norm)
    names = ARG_NAMES + ['m_' + n for n in WEIGHTS] + ['v_' + n for n in WEIGHTS]
    A = dict(zip(names, args))
    depth = ffn1_norm.shape[0]
    me = 4 * lax.axis_index("x") + 2 * lax.axis_index("y") + lax.axis_index("c")

    dims = (x.shape[-1], m_norm.shape[1], m_conv_b.shape[1], m_dt_bias.shape[1], sc_conv_w.shape[1], m_conv_w.shape[1])
    kind = dict(BIG)

    def work(n, prefix=''):
        return jnp.swapaxes(A[prefix + n], 1, 2) if kind[n] == 'col' else A[prefix + n]

    wb = {n: work(n).astype(BF) for n, _ in BIG}
    srcs = [[wb[ms[0]] if len(ms) == 1 else jnp.concatenate([wb[n] for n in ms], axis=1) for ms in stage] for stage in STAGES]
    conv_g = _unpack(_exchange("gather_conv_taps", _pack([A[n] for n in CONVW]), True), [A[n].shape for n in CONVW])
    taps = {}
    for n, st in zip(CONVW, conv_g):
        taps[n] = _pad_taps(jnp.transpose(st, (1, 2, 0, 3)).reshape(depth, st.shape[2], N_DEV * st.shape[3]))
    ssd_par = _ssd_params(m_dt_bias, m_A_log, m_D)
    small3 = {n: A[n].reshape(depth, 1, -1) for n in SMALL}

    def stage_weights(W, s, l, lands):
        for ms, land in zip(STAGES[s], lands):
            off = 0
            for n in ms:
                r = wb[n].shape[1]
                W[n] = _full_from_shards(land if len(ms) == 1 else land[:, off:off + r])
                off += r
        if s == 1:
            W['w_in_p'], W['w_dt'] = _perm_w_in(W.pop('w_in'), dims)

    flight = {}

    via_sibling = {(0, 0), (0, 1)}

    def begin_layer(l, dep):
        for s in range(len(STAGES)):
            rels = NEAR_PEERS if (l, s) in via_sibling else ALL_PEERS
            sems, lands, dep = _xchg_begin(f"gather_begin{l}{'abc'[s]}", srcs[s], l, dep, rels)
            flight[(l, s)] = (sems, lands)
        return dep

    def end_stage(W, l, s, after):
        sems, lands = flight.pop((l, s))
        tag = f"{l}{'abc'[s]}"
        if (l, s) in via_sibling:
            lands = _xchg_end("gather_end" + tag, srcs[s], lands, sems, l, after, NEAR_PEERS)
            rsems, lands = _relay_begin("gather_relay" + tag, lands)
            lands = _relay_end("gather_relayed" + tag, lands, rsems, after)
        else:
            lands = _xchg_end("gather_end" + tag, srcs[s], lands, sems, l, after)
        stage_weights(W, s, l, lands)

    tok = begin_layer(0, taps['sc_conv_w'])
    h = x[0]
    saved = []
    layers = []
    for l in range(depth):
        W = {n: (small3[n], l) for n in SMALL}
        W.update(l=l, sc_taps=taps['sc_conv_w'], m_taps=taps['m_conv_w'], m_conv_b=small3['m_conv_b'], ssd_par=ssd_par)
        layers.append(W)
        end_stage(W, l, 0, tok if l == 0 else h)
        h1, s1 = _ffn_fwd("ffn1", h, W['ffn1_norm'], W['ffn1_wg'], W['ffn1_wu'], W['ffn1_wd'])
        end_stage(W, l, 1, h1)
        tok = begin_layer(l + 1, W['w_dt']) if l + 1 < depth else None
        h2, s2 = _mixer_fwd(h1, W, dims, dep=tok)
        end_stage(W, l, 2, h2)
        h3, s3 = _ffn_fwd("ffn2", h2, W['ffn2_norm'], W['ffn2_wg'], W['ffn2_wu'], W['ffn2_wd'])
        h4, s4 = _ple_layer_fwd(h3, p[l, 0], W)
        saved.append((h, h1, h2, h3, s1, s2, s3, s4))
        h = h4

    dh, loss_row, d_final = _loss_head("loss_head", h, final_norm, loss_target[0])
    loss = lax.psum(loss_row[0, 0], ("x", "y", "c"))

    def send_bufs(g, s):
        return [jnp.concatenate([_shards_from_full(g[n]) for n in ms], axis=1) if len(ms) > 1
                else _shards_from_full(g[ms[0]]) for ms in STAGES[s]]

    grads = [None] * depth
    pending = []

    def send_stage(g, l, s, dep):
        send = send_bufs(g, s)
        sems, lands, tok = _xchg_begin(f"scatter_begin{l}{'abc'[s]}", send, None, dep)
        pending.append((l, s, send, lands, sems))
        return tok

    tok = loss.reshape(1, 1)
    for l in reversed(range(depth)):
        W = layers[l]
        h0, h1, h2, h3, s1, s2, s3, s4 = saved[l]
        g = {}
        dh, g4 = _ple_layer_bwd(dh, h3, p[l, 0], W, s4, dep=tok)
        g.update(g4)
        dh, dg, d_wg, d_wu, d_wd = _ffn_bwd("ffn2", dh, h2, W['ffn2_norm'], W['ffn2_wg'], W['ffn2_wu'], W['ffn2_wd'], s3)
        g.update(ffn2_norm=dg.reshape(-1), ffn2_wg=d_wg, ffn2_wu=d_wu, ffn2_wd=d_wd)
        tok = send_stage(g, l, 2, dh)
        dh, g2 = _mixer_bwd(dh, h1, W, dims, s2, dep=tok)
        g.update(g2)
        tok = send_stage(g, l, 1, dh)
        dh, dg, d_wg, d_wu, d_wd = _ffn_bwd("ffn1", dh, h0, W['ffn1_norm'], W['ffn1_wg'], W['ffn1_wu'], W['ffn1_wd'], s1, dep=tok)
        g.update(ffn1_norm=dg.reshape(-1), ffn1_wg=d_wg, ffn1_wu=d_wu, ffn1_wd=d_wd)
        grads[l] = g
        tok = send_stage(g, l, 0, dh)
    grad_x = dh[None]

    g_lands = [[None] * len(STAGES) for _ in range(depth)]
    big_res = [{}, {}, {}, {}]

    def finish(entries, after):
        for l, s, send, lands, sems in entries:
            got = _xchg_end(f"scatter_end{l}{'abc'[s]}", send, lands, sems, None, after)
            after = got[0]
            g_lands[l][s] = got
        return after

    def adam_stages(stages):
        res = None
        for s in stages:
            for gi, ms in enumerate(STAGES[s]):
                off = 0
                for n in ms:
                    res = _sum_adam("adamw_" + n, [g_lands[l][s][gi] for l in range(depth)], off, work(n), work(n, 'm_'), work(n, 'v_'))
                    for k in range(4):
                        big_res[k][n] = jnp.swapaxes(res[k], 1, 2) if kind[n] == 'col' else res[k]
                    off += wb[n].shape[1]
        return res[0]

    after = finish(pending[:-1], tok)
    after = adam_stages(range(1, len(STAGES)))
    after = finish(pending[-1:], after)
    adam_stages([0])

    small_names = SMALL + CONVW
    small_parts = [jnp.stack([grads[l][n] for l in range(depth)]) for n in small_names] + [d_final.reshape(-1)]
    small_sum = _sum8("sum_small", _exchange("gather_small_grads", _pack(small_parts), True, dep=after))
    sg = dict(zip(small_names + ['final_norm'], _unpack(small_sum, [a.shape for a in small_parts])))
    for n in CONVW:
        c = A[n].shape[-1]
        sg[n] = lax.dynamic_slice_in_dim(sg[n], me * c, c, axis=2)
    s_order = small_names + ['final_norm']
    s_shapes = [sg[n].shape for n in s_order]
    s_out = _adam_flat("adamw_small", _pack([sg[n] for n in s_order]), _pack([A[n] for n in s_order]),
                       _pack([A['m_' + n] for n in s_order]), _pack([A['v_' + n] for n in s_order]))
    small_res = [sg] + [dict(zip(s_order, _unpack(flat, s_shapes))) for flat in s_out]

    outs = [loss, grad_x]
    for k in range(4):
        for n in WEIGHTS:
            outs.append(big_res[k][n] if n in big_res[k] else small_res[k][n])
    return tuple(outs)
```

```python
import functools

import jax
import jax.numpy as jnp
from jax import lax
from jax.experimental import pallas as pl
from jax.experimental.pallas import tpu as pltpu

BF = jnp.bfloat16
F32 = jnp.float32

EPS = 1e-6
N_DEV = 8
LANES = 128
SSM_GROUPS = 4
SSM_HEADDIM = 64
SSM_CHUNK = 128
HALO = 16
VMEM_LIMIT = 56 * 1024 * 1024
FLAT_ROW_TILE = 2048

ADAM_LR = 0.001
ADAM_B1 = 0.9
ADAM_B2 = 0.999
ADAM_EPS = 1e-08
ADAM_WD = 0.01
ADAM_STEP = 10

MESH = pl.DeviceIdType.MESH

ARG_NAMES = ['x', 'p', 'ffn1_norm', 'ffn1_wg', 'ffn1_wu', 'ffn1_wd', 'mix_norm', 'w_in', 'sc_conv_w', 'sc_w_out', 'm_conv_w', 'm_conv_b', 'm_dt_bias', 'm_A_log', 'm_D', 'm_norm', 'm_w_out', 'w_o', 'ffn2_norm', 'ffn2_wg', 'ffn2_wu', 'ffn2_wd', 'ple_norm', 'ple_w_gate', 'ple_w_proj', 'final_norm', 'loss_target']
WEIGHTS = ARG_NAMES[2:26]
BIG = [('ffn1_wg', 'col'), ('ffn1_wu', 'col'), ('ffn1_wd', 'row'), ('w_in', 'col'), ('sc_w_out', 'row'),
       ('m_w_out', 'row'), ('w_o', 'row'), ('ffn2_wg', 'col'), ('ffn2_wu', 'col'), ('ffn2_wd', 'row'),
       ('ple_w_gate', 'row'), ('ple_w_proj', 'col')]
CONVW = ['sc_conv_w', 'm_conv_w']
SMALL = ['ffn1_norm', 'mix_norm', 'm_conv_b', 'm_dt_bias', 'm_A_log', 'm_D', 'm_norm', 'ffn2_norm', 'ple_norm']


def _pick(n, cands):
    for c in cands:
        if n % c == 0:
            return c
    return n


def _cp(sem):
    return pltpu.CompilerParams(dimension_semantics=sem, vmem_limit_bytes=VMEM_LIMIT)


def _sigmoid(x):
    return 1.0 / (1.0 + jnp.exp(-x))


def _softplus(x):
    return jnp.maximum(x, 0.0) + jnp.log(1.0 + jnp.exp(-jnp.abs(x)))


def _exchange(name, x, gather, dep=None):
    slab = x.shape if gather else x.shape[1:]

    def body(x_ref, *rest):
        o_ref, send_sems, recv_sems, local_sem = rest[-4:]
        mx, my, mc = lax.axis_index("x"), lax.axis_index("y"), lax.axis_index("c")
        me = 4 * mx + 2 * my + mc

        def src_for(k):
            return x_ref if gather else x_ref.at[k]

        local = pltpu.make_async_copy(src_for(me), o_ref.at[me], local_sem)
        local.start()
        sends = []
        peers = []
        for r in range(1, N_DEV):
            px = (mx + ((r >> 2) & 1)) % 2
            py = (my + ((r >> 1) & 1)) % 2
            pc = (mc + (r & 1)) % 2
            peer = 4 * px + 2 * py + pc
            peers.append(peer)
            cp = pltpu.make_async_remote_copy(
                src_ref=src_for(peer), dst_ref=o_ref.at[me], send_sem=send_sems.at[r - 1], recv_sem=recv_sems.at[r - 1],
                device_id=(px, py, pc), device_id_type=MESH)
            cp.start()
            sends.append(cp)
        for r in range(1, N_DEV):
            peer = peers[r - 1]
            pltpu.make_async_remote_copy(
                src_ref=src_for(peer), dst_ref=o_ref.at[peer], send_sem=send_sems.at[r - 1], recv_sem=recv_sems.at[r - 1],
                device_id=(mx, my, mc), device_id_type=MESH).wait_recv()
        for cp in sends:
            cp.wait_send()
        local.wait()

    return pl.pallas_call(
        body, name=name,
        out_shape=jax.ShapeDtypeStruct((N_DEV,) + tuple(slab), x.dtype),
        in_specs=[pl.BlockSpec(memory_space=pltpu.HBM)] + ([] if dep is None else [pl.BlockSpec(memory_space=pl.ANY)]),
        out_specs=pl.BlockSpec(memory_space=pltpu.HBM),
        scratch_shapes=[pltpu.SemaphoreType.DMA((N_DEV - 1,)), pltpu.SemaphoreType.DMA((N_DEV - 1,)), pltpu.SemaphoreType.DMA],
    )(*([x] if dep is None else [x, dep]))


STAGES = [[['ffn1_wd'], ['ffn1_wg'], ['ffn1_wu']],
          [['w_in'], ['sc_w_out', 'w_o', 'ple_w_gate', 'm_w_out']],
          [['ffn2_wd'], ['ffn2_wg'], ['ffn2_wu'], ['ple_w_proj']]]
_HBM = pl.BlockSpec(memory_space=pltpu.HBM)
_SEM = pl.BlockSpec(memory_space=pltpu.SEMAPHORE)
_ANY = pl.BlockSpec(memory_space=pl.ANY)
_EFFECT = pltpu.SideEffectType.DATAFLOW_SIDE_EFFECTING


def _peer_list():
    mx, my, mc = lax.axis_index("x"), lax.axis_index("y"), lax.axis_index("c")
    out = []
    for r in range(1, N_DEV):
        px = (mx + ((r >> 2) & 1)) % 2
        py = (my + ((r >> 1) & 1)) % 2
        pc = (mc + (r & 1)) % 2
        out.append((px, py, pc, 4 * px + 2 * py + pc))
    return 4 * mx + 2 * my + mc, out


ALL_PEERS = tuple(range(1, N_DEV))
NEAR_PEERS = (1, 2, 4, 6)
RELAYED = (2, 4, 6)


def _xchg_copy(src_refs, land_refs, send_sems, recv_sems, layer, i, r, peer, dst_slab):
    px, py, pc, pidx = peer
    n = len(src_refs)
    src = src_refs[i].at[layer] if layer is not None else src_refs[i].at[pidx]
    return pltpu.make_async_remote_copy(
        src_ref=src, dst_ref=land_refs[i].at[dst_slab], send_sem=send_sems.at[r * n + i], recv_sem=recv_sems.at[r * n + i],
        device_id=(px, py, pc), device_id_type=MESH)


def _own_copy(src_refs, land_refs, send_sems, layer, i, r, me):
    n = len(src_refs)
    src = src_refs[i].at[layer] if layer is not None else src_refs[i].at[me]
    return pltpu.make_async_copy(src, land_refs[i].at[me], send_sems.at[r * n + i])


def _xchg_begin(name, srcs, layer, dep, rels=ALL_PEERS):
    n = len(srcs)
    slabs = [tuple(s.shape[1:]) for s in srcs]
    ncp = n * (len(rels) + 1)

    def body(*refs):
        src_refs, land_refs = refs[:n], refs[n:2 * n]
        send_sems, recv_sems = refs[2 * n + 1], refs[2 * n + 2]
        token = refs[-1]
        me, peers = _peer_list()
        for ri, r in enumerate(rels):
            for i in range(n):
                _xchg_copy(src_refs, land_refs, send_sems, recv_sems, layer, i, ri, peers[r - 1], me).start()
        for i in range(n):
            _own_copy(src_refs, land_refs, send_sems, layer, i, len(rels), me).start()
        token[...] = jnp.zeros_like(token)

    lands = [pltpu.with_memory_space_constraint(lax.empty((N_DEV,) + sl, s.dtype), pltpu.HBM) for sl, s in zip(slabs, srcs)]
    out = pl.pallas_call(
        body, name=name,
        out_shape=(pltpu.SemaphoreType.DMA((ncp,)), pltpu.SemaphoreType.DMA((ncp,)),
                   *[pltpu.HBM((N_DEV,) + sl, s.dtype) for sl, s in zip(slabs, srcs)], jax.ShapeDtypeStruct((8, LANES), F32)),
        in_specs=[_HBM] * (2 * n) + [_ANY],
        out_specs=(_SEM, _SEM, *[_HBM] * n, pl.BlockSpec(memory_space=pltpu.VMEM)),
        input_output_aliases={n + i: 2 + i for i in range(n)},
        compiler_params=pltpu.CompilerParams(has_side_effects=_EFFECT),
    )(*[pltpu.with_memory_space_constraint(s, pltpu.HBM) for s in srcs], *lands, dep)
    return (out[0], out[1]), list(out[2:2 + n]), out[-1]


def _xchg_end(name, srcs, lands, sems, layer, after, rels=ALL_PEERS):
    n = len(srcs)

    def body(*refs):
        src_refs, land_refs = refs[:n], refs[n:2 * n]
        send_sems, recv_sems = refs[2 * n], refs[2 * n + 1]
        me, peers = _peer_list()
        for ri, r in enumerate(rels):
            for i in range(n):
                cp = _xchg_copy(src_refs, land_refs, send_sems, recv_sems, layer, i, ri, peers[r - 1], peers[r - 1][3])
                cp.wait_send()
                cp.wait_recv()
        for i in range(n):
            _own_copy(src_refs, land_refs, send_sems, layer, i, len(rels), me).wait()

    out = pl.pallas_call(
        body, name=name,
        out_shape=tuple(pltpu.HBM(l.shape, l.dtype) for l in lands),
        in_specs=[_HBM] * (2 * n) + [_SEM, _SEM, _ANY], out_specs=tuple([_HBM] * n),
        input_output_aliases={n + i: i for i in range(n)},
        compiler_params=pltpu.CompilerParams(has_side_effects=_EFFECT),
    )(*[pltpu.with_memory_space_constraint(s, pltpu.HBM) for s in srcs], *lands, sems[0], sems[1], after)
    return list(out)


def _relay_copy(land_refs, send_sems, recv_sems, i, qi, slab, sibling):
    n = len(land_refs)
    return pltpu.make_async_remote_copy(
        src_ref=land_refs[i].at[slab], dst_ref=land_refs[i].at[slab], send_sem=send_sems.at[qi * n + i],
        recv_sem=recv_sems.at[qi * n + i], device_id=sibling[:3], device_id_type=MESH)


def _relay_begin(name, lands):
    n = len(lands)
    ncp = n * len(RELAYED)

    def body(*refs):
        land_refs = refs[:n]
        send_sems, recv_sems = refs[n], refs[n + 1]
        me, peers = _peer_list()
        for qi, q in enumerate(RELAYED):
            for i in range(n):
                _relay_copy(land_refs, send_sems, recv_sems, i, qi, peers[q - 1][3], peers[0]).start()

    out = pl.pallas_call(
        body, name=name,
        out_shape=(pltpu.SemaphoreType.DMA((ncp,)), pltpu.SemaphoreType.DMA((ncp,)), *[pltpu.HBM(l.shape, l.dtype) for l in lands]),
        in_specs=[_HBM] * n, out_specs=(_SEM, _SEM, *[_HBM] * n),
        input_output_aliases={i: 2 + i for i in range(n)},
        compiler_params=pltpu.CompilerParams(has_side_effects=_EFFECT),
    )(*lands)
    return (out[0], out[1]), list(out[2:])


def _relay_end(name, lands, sems, after):
    n = len(lands)

    def body(*refs):
        land_refs = refs[:n]
        send_sems, recv_sems = refs[n], refs[n + 1]
        me, peers = _peer_list()
        for qi, q in enumerate(RELAYED):
            for i in range(n):
                _relay_copy(land_refs, send_sems, recv_sems, i, qi, peers[q - 1][3], peers[0]).wait_send()
                _relay_copy(land_refs, send_sems, recv_sems, i, qi, peers[q][3], peers[0]).wait_recv()

    out = pl.pallas_call(
        body, name=name,
        out_shape=tuple(pltpu.HBM(l.shape, l.dtype) for l in lands),
        in_specs=[_HBM] * n + [_SEM, _SEM, _ANY], out_specs=tuple([_HBM] * n),
        input_output_aliases={i: i for i in range(n)},
        compiler_params=pltpu.CompilerParams(has_side_effects=_EFFECT),
    )(*lands, sems[0], sems[1], after)
    return list(out)


MM_TILES = (1024, 1408, 512, 256, 128)
MM_OPERAND_BYTES = 24 * 1024 * 1024


def _mm(name, a, b, *, ta=False, tb=False, out_dtype=None, res=None, alpha=1.0, dep=None):
    out_dtype = out_dtype or BF
    M, K = (a.shape[1], a.shape[0]) if ta else a.shape
    N = b.shape[0] if tb else b.shape[1]
    assert (b.shape[1] if tb else b.shape[0]) == K, (name, a.shape, b.shape)
    tm = _pick(M, MM_TILES)
    tn = _pick(N, MM_TILES)
    per_k = 2 * (tm * a.dtype.itemsize + tn * b.dtype.itemsize)
    tk = [t for t in sorted({K, 4096, 2816, 2560, 2048, 1408, 1024, 512, 256, 128}, reverse=True)
          if K % t == 0 and (t * per_k <= MM_OPERAND_BYTES or t == 128)][0]
    nk = K // tk
    a_spec = pl.BlockSpec((tk, tm), lambda i, j, k: (k, i)) if ta else pl.BlockSpec((tm, tk), lambda i, j, k: (i, k))
    b_spec = pl.BlockSpec((tn, tk), lambda i, j, k: (j, k)) if tb else pl.BlockSpec((tk, tn), lambda i, j, k: (k, j))
    dn = (((0 if ta else 1,), (1 if tb else 0,)), ((), ()))
    has_res = res is not None
    n_dep = 0 if dep is None else 1

    def body(*refs):
        a_ref, b_ref = refs[:2]
        r_ref = refs[2] if has_res else None
        o_ref = refs[2 + has_res + n_dep]

        def finish(v):
            if alpha != 1.0:
                v = v * alpha
            if has_res:
                v = r_ref[...] + v
            o_ref[...] = v.astype(o_ref.dtype)

        part = lax.dot_general(a_ref[...].astype(BF), b_ref[...].astype(BF), dn, preferred_element_type=F32)
        if nk == 1:
            finish(part)
            return
        acc = refs[-1]
        k = pl.program_id(2)

        @pl.when(k == 0)
        def _():
            acc[...] = part

        @pl.when((k > 0) & (k < nk - 1))
        def _():
            acc[...] += part

        @pl.when(k == nk - 1)
        def _():
            finish(acc[...] + part)

    in_specs = [a_spec, b_spec]
    args = [a, b]
    if has_res:
        in_specs.append(pl.BlockSpec((tm, tn), lambda i, j, k: (i, j)))
        args.append(res)
    if dep is not None:
        in_specs.append(_ANY)
        args.append(dep)
    return pl.pallas_call(
        body, name=name, grid=(M // tm, N // tn, nk),
        in_specs=in_specs, out_specs=pl.BlockSpec((tm, tn), lambda i, j, k: (i, j)),
        out_shape=jax.ShapeDtypeStruct((M, N), out_dtype),
        scratch_shapes=[pltpu.VMEM((tm, tn), F32)] if nk > 1 else [],
        compiler_params=_cp(("parallel", "parallel", "arbitrary")),
    )(*args)


def _mm_rms_bwd(name, a, b, acc_in, h, g, dh_res, *, tb=False):
    M, K = a.shape
    N = b.shape[0] if tb else b.shape[1]
    assert (b.shape[1] if tb else b.shape[0]) == K and h.shape == (M, N), (name, a.shape, b.shape)
    tm = _pick(M, (FFN_TOKEN_TILE, 256, 128))
    dn = (((1,), (1 if tb else 0,)), ((), ()))
    has_acc = acc_in is not None
    g_arr, g_row = _prow(g) if isinstance(g, tuple) else (_prow(g), None)

    def body(*refs):
        a_ref, b_ref = refs[:2]
        c_ref = refs[2] if has_acc else None
        h_ref, g_ref, r_ref, o_ref, dg_ref = refs[2 + has_acc:]
        d = lax.dot_general(a_ref[...].astype(BF), b_ref[...].astype(BF), dn, preferred_element_type=F32)
        if has_acc:
            d = c_ref[...] + d
        x = h_ref[...]
        r = lax.rsqrt(jnp.mean(x * x, axis=-1, keepdims=True) + EPS)
        xhat = x * r
        dxhat = d * g_ref[...]
        o_ref[...] = r_ref[...] + r * (dxhat - xhat * jnp.mean(dxhat * xhat, axis=-1, keepdims=True))

        @pl.when(pl.program_id(0) == 0)
        def _():
            dg_ref[...] = jnp.zeros_like(dg_ref)

        dg_ref[...] += jnp.sum(d * xhat, axis=0, keepdims=True)

    row = pl.BlockSpec((tm, N), lambda i: (i, 0))
    gspec = pl.BlockSpec((1, N), lambda i: (0, 0)) if g_row is None else pl.BlockSpec((None, 1, N), lambda i: (g_row, 0, 0))
    in_specs = [pl.BlockSpec((tm, K), lambda i: (i, 0)), pl.BlockSpec(b.shape, lambda i: (0, 0))]
    args = [a, b]
    if has_acc:
        in_specs.append(row)
        args.append(acc_in)
    return pl.pallas_call(
        body, name=name, grid=(M // tm,),
        in_specs=in_specs + [row, gspec, row], out_specs=[row, pl.BlockSpec((1, N), lambda i: (0, 0))],
        out_shape=[jax.ShapeDtypeStruct((M, N), F32), jax.ShapeDtypeStruct((1, N), F32)],
        compiler_params=_cp(("arbitrary",)),
    )(*args, h, g_arr, dh_res)


def _ew(name, fn, tiled, params, outs, accs=(), tile=256, dep=None):
    tiled = [t if isinstance(t, tuple) else (t, t.shape[1], 0) for t in tiled]
    params = [q if isinstance(q, tuple) else (q, None) for q in params]
    S = tiled[0][0].shape[0]
    T = _pick(S, (tile, 128, 64, 32, 16))
    n_in = len(tiled) + len(params)
    n_dep = 0 if dep is None else 1

    def body(*refs):
        fn(pl.program_id(0) == 0, *refs[:n_in], *refs[n_in + n_dep:])

    in_specs = [pl.BlockSpec((T, w), lambda i, cb=cb: (i, cb)) for _, w, cb in tiled]
    for q, row in params:
        if row is None:
            in_specs.append(pl.BlockSpec(q.shape, lambda i: (0, 0)))
        else:
            in_specs.append(pl.BlockSpec((None, 1, q.shape[2]), lambda i, row=row: (row, 0, 0)))
    args = [t[0] for t in tiled] + [q[0] for q in params]
    if dep is not None:
        in_specs.append(pl.BlockSpec(memory_space=pl.ANY))
        args.append(dep)
    out_specs = [pl.BlockSpec((T, w), lambda i: (i, 0)) for w, _ in outs]
    out_specs += [pl.BlockSpec(shp, lambda i: (0, 0)) for shp, _ in accs]
    out_shape = [jax.ShapeDtypeStruct((S, w), dt) for w, dt in outs]
    out_shape += [jax.ShapeDtypeStruct(shp, dt) for shp, dt in accs]
    res = pl.pallas_call(
        body, name=name, grid=(S // T,), in_specs=in_specs, out_specs=out_specs, out_shape=out_shape,
        compiler_params=_cp(("arbitrary",)),
    )(*args)
    return res


def _prow(g):
    return g if isinstance(g, tuple) else g.reshape(1, -1)


def _rms_fwd(name, h, g, dep=None):
    def fn(first, h_ref, g_ref, o_ref):
        x = h_ref[...]
        r = lax.rsqrt(jnp.mean(x * x, axis=-1, keepdims=True) + EPS)
        o_ref[...] = (x * r * g_ref[...]).astype(o_ref.dtype)

    return _ew(name, fn, [h], [_prow(g)], [(h.shape[1], BF)], dep=dep)[0]


FFN_TOKEN_TILE = 512


def _ffn_up(name, xn, wgT, wuT, dep=None):
    S, D = xn.shape
    FF = wgT.shape[0]
    tm = _pick(S, (FFN_TOKEN_TILE, 256, 128))
    tn = _pick(FF, MM_TILES)
    n_dep = 0 if dep is None else 1

    def body(x_ref, g_ref, u_ref, *rest):
        a_ref, b_ref, h_ref = rest[n_dep:]
        x = x_ref[...]
        a = _dot_nt(x, g_ref[...])
        b = _dot_nt(x, u_ref[...])
        a_ref[...] = a.astype(BF)
        b_ref[...] = b.astype(BF)
        h_ref[...] = (a * _sigmoid(a) * b).astype(BF)

    wspec = pl.BlockSpec((tn, D), lambda j, i: (j, 0))
    ospec = pl.BlockSpec((tm, tn), lambda j, i: (i, j))
    return pl.pallas_call(
        body, name=name, grid=(FF // tn, S // tm),
        in_specs=[pl.BlockSpec((tm, D), lambda j, i: (i, 0)), wspec, wspec] + ([] if dep is None else [_ANY]),
        out_specs=[ospec] * 3, out_shape=[jax.ShapeDtypeStruct((S, FF), BF)] * 3,
        compiler_params=_cp(("parallel", "arbitrary")),
    )(*([xn, wgT, wuT] + ([] if dep is None else [dep])))


def _ffn_dact(name, dh, wd, a, b, dep=None):
    S, D = dh.shape
    FF = wd.shape[0]
    tm = _pick(S, (FFN_TOKEN_TILE, 256, 128))
    tn = _pick(FF, MM_TILES)
    n_dep = 0 if dep is None else 1

    def body(d_ref, w_ref, a_ref, b_ref, *rest):
        da_ref, db_ref = rest[n_dep:]
        d = 0.5 * _dot_nt(d_ref[...].astype(BF), w_ref[...])
        av = a_ref[...].astype(F32)
        s = _sigmoid(av)
        da_ref[...] = (d * b_ref[...].astype(F32) * (s * (1.0 + av * (1.0 - s)))).astype(BF)
        db_ref[...] = (d * av * s).astype(BF)

    tspec = pl.BlockSpec((tm, tn), lambda j, i: (i, j))
    return pl.pallas_call(
        body, name=name, grid=(FF // tn, S // tm),
        in_specs=[pl.BlockSpec((tm, D), lambda j, i: (i, 0)), pl.BlockSpec((tn, D), lambda j, i: (j, 0)), tspec, tspec]
        + ([] if dep is None else [_ANY]),
        out_specs=[tspec] * 2, out_shape=[jax.ShapeDtypeStruct((S, FF), BF)] * 2,
        compiler_params=_cp(("parallel", "arbitrary")),
    )(*([dh, wd, a, b] + ([] if dep is None else [dep])))


def _merge_fwd(name, ga, gm, ya, ym):
    def fn(first, ga_ref, gm_ref, ya_ref, ym_ref, o_ref):
        o = _sigmoid(ga_ref[...].astype(F32)) * ya_ref[...].astype(F32) + _sigmoid(gm_ref[...].astype(F32)) * ym_ref[...].astype(F32)
        o_ref[...] = o.astype(o_ref.dtype)

    return _ew(name, fn, [ga, gm, ya, ym], [], [(ya.shape[1], BF)])[0]


def _merge_bwd(name, dmerged, ga, gm, ya, ym):
    W = ya.shape[1]

    def fn(first, d_ref, ga_ref, gm_ref, ya_ref, ym_ref, dga_ref, dgm_ref, dya_ref, dym_ref):
        d = d_ref[...].astype(F32)
        sa = _sigmoid(ga_ref[...].astype(F32))
        sm = _sigmoid(gm_ref[...].astype(F32))
        dga_ref[...] = (d * ya_ref[...].astype(F32) * sa * (1.0 - sa)).astype(BF)
        dgm_ref[...] = (d * ym_ref[...].astype(F32) * sm * (1.0 - sm)).astype(BF)
        dya_ref[...] = (d * sa).astype(BF)
        dym_ref[...] = (d * sm).astype(BF)

    return _ew(name, fn, [dmerged, ga, gm, ya, ym], [], [(W, BF)] * 4)


def _gnorm_fwd(name, y, z, w):
    W = y.shape[1]
    gw = W // SSM_GROUPS

    def fn(first, y_ref, z_ref, w_ref, o_ref):
        for g in range(SSM_GROUPS):
            sl = slice(g * gw, (g + 1) * gw)
            zz = z_ref[:, sl].astype(F32)
            t = y_ref[:, sl].astype(F32) * (zz * _sigmoid(zz))
            r = lax.rsqrt(jnp.mean(t * t, axis=-1, keepdims=True) + EPS)
            o_ref[:, sl] = (t * r * w_ref[:, sl]).astype(o_ref.dtype)

    return _ew(name, fn, [y, z], [_prow(w)], [(W, BF)])[0]


def _gnorm_bwd(name, dyn, y, z, w):
    W = y.shape[1]
    gw = W // SSM_GROUPS

    def fn(first, d_ref, y_ref, z_ref, w_ref, dy_ref, dz_ref, dw_ref):
        @pl.when(first)
        def _():
            dw_ref[...] = jnp.zeros_like(dw_ref)

        for g in range(SSM_GROUPS):
            sl = slice(g * gw, (g + 1) * gw)
            zz = z_ref[:, sl].astype(F32)
            yy = y_ref[:, sl].astype(F32)
            d = d_ref[:, sl].astype(F32)
            s = _sigmoid(zz)
            sz = zz * s
            t = yy * sz
            r = lax.rsqrt(jnp.mean(t * t, axis=-1, keepdims=True) + EPS)
            that = t * r
            dthat = d * w_ref[:, sl]
            dt = r * (dthat - that * jnp.mean(dthat * that, axis=-1, keepdims=True))
            dw_ref[:, sl] += jnp.sum(d * that, axis=0, keepdims=True)
            dy_ref[:, sl] = (dt * sz).astype(BF)
            dz_ref[:, sl] = (dt * yy * (s * (1.0 + zz * (1.0 - s)))).astype(BF)

    return _ew(name, fn, [dyn, y, z], [_prow(w)], [(W, BF), (W, BF)], [((1, W), F32)])


def _ple_fwd(name, h, gpre, pp):
    def fn(first, h_ref, g_ref, p_ref, o_ref):
        o_ref[...] = h_ref[...] + _sigmoid(g_ref[...].astype(F32)) * p_ref[...].astype(F32)

    return _ew(name, fn, [h, gpre, pp], [], [(h.shape[1], F32)])[0]


def _ple_bwd(name, dh, gpre, pp, dep=None):
    W = dh.shape[1]

    def fn(first, d_ref, g_ref, p_ref, dg_ref, dp_ref):
        d = d_ref[...]
        s = _sigmoid(g_ref[...].astype(F32))
        dg_ref[...] = (d * p_ref[...].astype(F32) * s * (1.0 - s)).astype(BF)
        dp_ref[...] = (d * s).astype(BF)

    return _ew(name, fn, [dh, gpre, pp], [], [(W, BF), (W, BF)], dep=dep)


def _loss_head(name, h, g, target):
    D = h.shape[1]

    def fn(first, h_ref, t_ref, g_ref, dh_ref, loss_ref, dg_ref):
        x = h_ref[...]
        r = lax.rsqrt(jnp.mean(x * x, axis=-1, keepdims=True) + EPS)
        xhat = x * r
        err = xhat * g_ref[...] - t_ref[...]
        part = 0.5 * jnp.sum(jnp.mean(err * err, axis=-1, keepdims=True), axis=0, keepdims=True)
        dy = err * (1.0 / D)
        dxhat = dy * g_ref[...]
        dh_ref[...] = r * (dxhat - xhat * jnp.mean(dxhat * xhat, axis=-1, keepdims=True))

        @pl.when(first)
        def _():
            loss_ref[...] = jnp.zeros_like(loss_ref)
            dg_ref[...] = jnp.zeros_like(dg_ref)

        loss_ref[...] += jnp.broadcast_to(part, loss_ref.shape)
        dg_ref[...] += jnp.sum(dy * xhat, axis=0, keepdims=True)

    return _ew(name, fn, [h, target], [_prow(g)], [(D, F32)], [((1, LANES), F32), ((1, D), F32)])


def _conv_specs(S, C, offs, l):
    T = _pick(S, (512, 256, 128, 64, 32, 16))
    Ct = [c for c in (512, 256, 128) if C % c == 0 and all(o % c == 0 for o in offs)][0]
    per = T // HALO
    last = S // HALO - 1

    def cur(off=0):
        return pl.BlockSpec((T, Ct), lambda j, i: (i, off // Ct + j))

    def prev(off=0):
        return pl.BlockSpec((HALO, Ct), lambda j, i: (jnp.maximum(i * per - 1, 0), off // Ct + j))

    def nxt(off=0):
        return pl.BlockSpec((HALO, Ct), lambda j, i: (jnp.minimum((i + 1) * per, last), off // Ct + j))

    wspec = pl.BlockSpec((None, 8, Ct), lambda j, i: (l, 0, j))
    return T, Ct, cur, prev, nxt, wspec


def _pad_taps(w):
    return jnp.concatenate([w.astype(F32), jnp.zeros((w.shape[0], 8 - w.shape[1], w.shape[2]), F32)], axis=1)


def _causal(cat, w_ref, K, T, lead):
    views = [(pltpu.roll(cat, K - 1 - k, 0) if k < K - 1 else cat)[lead:lead + T] for k in range(K)]
    out = None
    for k in range(K):
        term = w_ref[k:k + 1, :] * views[k]
        out = term if out is None else out + term
    return out, views


def _anticausal(cat, w_ref, K, T):
    out = None
    rows = cat.shape[0]
    for k in range(K):
        o = K - 1 - k
        term = w_ref[k:k + 1, :] * (pltpu.roll(cat, rows - o, 0) if o else cat)[:T]
        out = term if out is None else out + term
    return out


def _scconv_fwd(name, proj, ob, oc, ox, taps, K, l):
    S = proj.shape[0]
    C = taps.shape[2]
    T, Ct, cur, prev, nxt, wspec = _conv_specs(S, C, (ob, oc, ox), l)

    def body(b_ref, c_ref, x_ref, cp_ref, xp_ref, w_ref, o_ref):
        i = pl.program_id(1)
        q = c_ref[...].astype(F32) * x_ref[...].astype(F32)
        qp = jnp.where(i == 0, 0.0, cp_ref[...].astype(F32) * xp_ref[...].astype(F32))
        cat = jnp.concatenate([qp, q], axis=0)
        o_ref[...] = (b_ref[...].astype(F32) * _causal(cat, w_ref, K, T, HALO)[0]).astype(o_ref.dtype)

    return pl.pallas_call(
        body, name=name, grid=(C // Ct, S // T),
        in_specs=[cur(ob), cur(oc), cur(ox), prev(oc), prev(ox), wspec], out_specs=cur(),
        out_shape=jax.ShapeDtypeStruct((S, C), BF), compiler_params=_cp(("parallel", "arbitrary")),
    )(proj, proj, proj, proj, proj, taps)


def _scconv_bwd(name, dv, proj, ob, oc, ox, taps, K, l):
    S = proj.shape[0]
    C = taps.shape[2]
    T, Ct, cur, prev, nxt, wspec = _conv_specs(S, C, (ob, oc, ox), l)
    n_t = S // T

    def body(d_ref, b_ref, c_ref, x_ref, dn_ref, bn_ref, cp_ref, xp_ref, w_ref, db_ref, dc_ref, dx_ref, dw_ref):
        i = pl.program_id(1)
        c = c_ref[...].astype(F32)
        x = x_ref[...].astype(F32)
        d = d_ref[...].astype(F32)
        q = c * x
        qp = jnp.where(i == 0, 0.0, cp_ref[...].astype(F32) * xp_ref[...].astype(F32))
        catq = jnp.concatenate([qp, q], axis=0)
        cv, q_views = _causal(catq, w_ref, K, T, HALO)
        db_ref[...] = (d * cv).astype(BF)
        dcv = d * b_ref[...].astype(F32)
        dcvn = jnp.where(i == n_t - 1, 0.0, dn_ref[...].astype(F32) * bn_ref[...].astype(F32))
        catd = jnp.concatenate([dcv, dcvn], axis=0)
        dq = _anticausal(catd, w_ref, K, T)
        dc_ref[...] = (dq * x).astype(BF)
        dx_ref[...] = (dq * c).astype(BF)

        @pl.when(i == 0)
        def _():
            dw_ref[...] = jnp.zeros_like(dw_ref)

        for k in range(K):
            dw_ref[k:k + 1, :] += jnp.sum(dcv * q_views[k], axis=0, keepdims=True)

    return pl.pallas_call(
        body, name=name, grid=(C // Ct, n_t),
        in_specs=[cur(), cur(ob), cur(oc), cur(ox), nxt(), nxt(ob), prev(oc), prev(ox), wspec],
        out_specs=[cur(), cur(), cur(), pl.BlockSpec((8, Ct), lambda j, i: (0, j))],
        out_shape=[jax.ShapeDtypeStruct((S, C), BF)] * 3 + [jax.ShapeDtypeStruct((8, C), F32)],
        compiler_params=_cp(("parallel", "arbitrary")),
    )(dv, proj, proj, proj, dv, proj, proj, proj, taps)


def _mconv_fwd(name, proj, ox, taps, K, bias, l):
    S = proj.shape[0]
    C = taps.shape[2]
    T, Ct, cur, prev, nxt, wspec = _conv_specs(S, C, (ox,), l)
    bspec = pl.BlockSpec((None, 1, Ct), lambda j, i: (l, 0, j))

    def body(x_ref, xp_ref, w_ref, b_ref, o_ref):
        i = pl.program_id(1)
        xp = jnp.where(i == 0, 0.0, xp_ref[...].astype(F32))
        cat = jnp.concatenate([xp, x_ref[...].astype(F32)], axis=0)
        pre = _causal(cat, w_ref, K, T, HALO)[0] + b_ref[...]
        o_ref[...] = (pre * _sigmoid(pre)).astype(o_ref.dtype)

    return pl.pallas_call(
        body, name=name, grid=(C // Ct, S // T),
        in_specs=[cur(ox), prev(ox), wspec, bspec], out_specs=cur(),
        out_shape=jax.ShapeDtypeStruct((S, C), BF), compiler_params=_cp(("parallel", "arbitrary")),
    )(proj, proj, taps, bias)


def _mconv_bwd(name, dout, proj, ox, taps, K, bias, l):
    S = proj.shape[0]
    C = taps.shape[2]
    T, Ct, cur, prev, nxt, wspec = _conv_specs(S, C, (ox,), l)
    n_t = S // T
    bspec = pl.BlockSpec((None, 1, Ct), lambda j, i: (l, 0, j))

    def body(d_ref, dn_ref, x_ref, xp_ref, xn_ref, w_ref, b_ref, dx_ref, dw_ref, db_ref):
        i = pl.program_id(1)
        xp = jnp.where(i == 0, 0.0, xp_ref[...].astype(F32))
        cat3 = jnp.concatenate([xp, x_ref[...].astype(F32), xn_ref[...].astype(F32)], axis=0)
        pre, x_views = _causal(cat3, w_ref, K, T + HALO, HALO)
        pre = pre + b_ref[...]
        dn = jnp.where(i == n_t - 1, 0.0, dn_ref[...].astype(F32))
        dext = jnp.concatenate([d_ref[...].astype(F32), dn], axis=0)
        s = _sigmoid(pre)
        dpre = dext * (s * (1.0 + pre * (1.0 - s)))
        dx_ref[...] = _anticausal(dpre, w_ref, K, T).astype(BF)
        dcur = dpre[:T]

        @pl.when(i == 0)
        def _():
            dw_ref[...] = jnp.zeros_like(dw_ref)
            db_ref[...] = jnp.zeros_like(db_ref)

        db_ref[...] += jnp.sum(dcur, axis=0, keepdims=True)
        for k in range(K):
            dw_ref[k:k + 1, :] += jnp.sum(dcur * x_views[k][:T], axis=0, keepdims=True)

    return pl.pallas_call(
        body, name=name, grid=(C // Ct, n_t),
        in_specs=[cur(), nxt(), cur(ox), prev(ox), nxt(ox), wspec, bspec],
        out_specs=[cur(), pl.BlockSpec((8, Ct), lambda j, i: (0, j)), pl.BlockSpec((1, Ct), lambda j, i: (0, j))],
        out_shape=[jax.ShapeDtypeStruct((S, C), BF), jax.ShapeDtypeStruct((8, C), F32), jax.ShapeDtypeStruct((1, C), F32)],
        compiler_params=_cp(("parallel", "arbitrary")),
    )(dout, dout, proj, proj, proj, taps, bias)


def _tri_matmul(tri_bf, v):
    hi = v.astype(BF)
    r1 = v - hi.astype(F32)
    mid = r1.astype(BF)
    lo = (r1 - mid.astype(F32)).astype(BF)
    dot = functools.partial(jnp.dot, preferred_element_type=F32)
    return dot(tri_bf, hi) + dot(tri_bf, mid) + dot(tri_bf, lo)


def _dot_nt(a, b):
    return lax.dot_general(a, b, (((1,), (1,)), ((), ())), preferred_element_type=F32)


def _dot_tn(a, b):
    return lax.dot_general(a, b, (((0,), (0,)), ((), ())), preferred_element_type=F32)


def _dot_nn(a, b):
    return jnp.dot(a, b, preferred_element_type=F32)


def _ssd_chunk_scalars(dtr_ref, par_ref, L):
    row_i = lax.broadcasted_iota(jnp.int32, (L, L), 0)
    col_i = lax.broadcasted_iota(jnp.int32, (L, L), 1)
    tri = row_i >= col_i
    pre = dtr_ref[...] + par_ref[0:1, :]
    dt_all = _softplus(pre)
    A_row = -jnp.exp(par_ref[1:2, :])
    a_all = dt_all * A_row
    acum_all = _tri_matmul(tri.astype(BF), a_all)
    return tri, pre, dt_all, A_row, a_all, acum_all, acum_all.T


def _ssd_dims(xbc, heads):
    S, conv_dim = xbc.shape
    inner = heads * SSM_HEADDIM
    N = (conv_dim - inner) // (2 * SSM_GROUPS)
    gw = inner // SSM_GROUPS
    PP = gw // LANES
    L = min(SSM_CHUNK, S)
    assert N == LANES and gw % LANES == 0 and inner % (SSM_GROUPS * N) == 0 and S % L == 0
    return S, inner, N, gw, PP, L, S // L


def _ssd_params(dt_bias, A_log, Dp):
    depth, H = dt_bias.shape
    rows = jnp.stack([dt_bias, A_log, Dp], axis=1).astype(F32)
    rows = jnp.concatenate([rows, jnp.zeros((depth, 3, LANES - H), F32)], axis=2)
    return jnp.concatenate([rows, jnp.zeros((depth, 5, LANES), F32)], axis=1)


def _ssd_fwd(name, xbc, dt_raw, par, l, heads):
    S, inner, N, gw, PP, L, nc = _ssd_dims(xbc, heads)
    G = SSM_GROUPS

    def body(x_ref, b_ref, c_ref, dtr_ref, par_ref, y_ref, st_out_ref, sc_ref, at_ref, st_ref):
        @pl.when(pl.program_id(0) == 0)
        def _():
            st_ref[...] = jnp.zeros_like(st_ref)

        tri, pre, dt_all, A_row, a_all, acum_all, acumT = _ssd_chunk_scalars(dtr_ref, par_ref, L)
        sc_ref[:, 0:LANES] = dt_all
        sc_ref[:, LANES:2 * LANES] = a_all
        sc_ref[:, 2 * LANES:3 * LANES] = acum_all
        sc_ref[:, 3 * LANES:] = pre
        at_ref[0] = acumT
        lane = lax.broadcasted_iota(jnp.int32, (L, LANES), 1)
        lane1 = lax.broadcasted_iota(jnp.int32, (1, LANES), 1)
        lo = lane < SSM_HEADDIM
        lo1 = lane1 < SSM_HEADDIM
        for g in range(G):
            Bb = b_ref[:, g * N:(g + 1) * N]
            Cb = c_ref[:, g * N:(g + 1) * N]
            BbT = Bb.astype(F32).T.astype(BF)
            Gm = _dot_nt(Cb, Bb)
            for j in range(PP):
                pj = g * PP + j
                h0, h1 = 2 * pj, 2 * pj + 1
                cols = slice(pj * LANES, (pj + 1) * LANES)
                x = x_ref[:, cols].astype(F32)
                dt_l = jnp.where(lo, dt_all[:, h0:h0 + 1], dt_all[:, h1:h1 + 1])
                ac0 = acum_all[:, h0:h0 + 1]
                ac1 = acum_all[:, h1:h1 + 1]
                ac_l = jnp.where(lo, ac0, ac1)
                E0 = jnp.exp(jnp.where(tri, ac0 - acumT[h0:h0 + 1, :], -1e30))
                E1 = jnp.exp(jnp.where(tri, ac1 - acumT[h1:h1 + 1, :], -1e30))
                xd = x * dt_l
                xdb = xd.astype(BF)
                yd = jnp.where(lo, _dot_nn((Gm * E0).astype(BF), xdb), _dot_nn((Gm * E1).astype(BF), xdb))
                prevT = st_ref[pj]
                st_out_ref[0, pj] = prevT
                P = _dot_nn(Cb, prevT.astype(BF))
                D_l = jnp.where(lo1, par_ref[2:3, h0:h0 + 1], par_ref[2:3, h1:h1 + 1])
                y_ref[:, cols] = (yd + P * jnp.exp(ac_l) + D_l * x).astype(y_ref.dtype)
                al0 = ac0[L - 1:L, :]
                al1 = ac1[L - 1:L, :]
                Wm = xd * jnp.exp(jnp.where(lo, al0, al1) - ac_l)
                eal = jnp.where(lo1, jnp.exp(al0), jnp.exp(al1))
                st_ref[pj] = eal * prevT + _dot_nn(BbT, Wm.astype(BF))

    gn = G * N
    return pl.pallas_call(
        body, name=name, grid=(nc,),
        in_specs=[pl.BlockSpec((L, inner), lambda c: (c, 0)), pl.BlockSpec((L, gn), lambda c: (c, inner // gn)),
                  pl.BlockSpec((L, gn), lambda c: (c, inner // gn + 1)),
                  pl.BlockSpec((L, LANES), lambda c: (c, 0)), pl.BlockSpec((None, 8, LANES), lambda c: (l, 0, 0))],
        out_specs=[pl.BlockSpec((L, inner), lambda c: (c, 0)), pl.BlockSpec((1, G * PP, N, LANES), lambda c: (c, 0, 0, 0)),
                   pl.BlockSpec((L, 4 * LANES), lambda c: (c, 0)), pl.BlockSpec((1, LANES, L), lambda c: (c, 0, 0))],
        out_shape=[jax.ShapeDtypeStruct((S, inner), BF), jax.ShapeDtypeStruct((nc, G * PP, N, LANES), F32),
                   jax.ShapeDtypeStruct((S, 4 * LANES), F32), jax.ShapeDtypeStruct((nc, LANES, L), F32)],
        scratch_shapes=[pltpu.VMEM((G * PP, N, LANES), F32)],
        compiler_params=_cp(("arbitrary",)),
    )(xbc, xbc, xbc, dt_raw, par)


def _ssd_bwd(name, dy, xbc, scal, acum_t, states, par, l, heads):
    S, inner, N, gw, PP, L, nc = _ssd_dims(xbc, heads)
    G = SSM_GROUPS

    def body(dy_ref, x_ref, b_ref, c_ref, sc_ref, at_ref, par_ref, st_in_ref, d_ref, ddt_ref, dpar_ref, dst_ref):
        @pl.when(pl.program_id(0) == 0)
        def _():
            dst_ref[...] = jnp.zeros_like(dst_ref)
            dpar_ref[...] = jnp.zeros_like(dpar_ref)

        tri = lax.broadcasted_iota(jnp.int32, (L, L), 0) >= lax.broadcasted_iota(jnp.int32, (L, L), 1)
        dt_all = sc_ref[:, 0:LANES]
        a_all = sc_ref[:, LANES:2 * LANES]
        acum_all = sc_ref[:, 2 * LANES:3 * LANES]
        pre = sc_ref[:, 3 * LANES:]
        acumT = at_ref[0]
        A_row = -jnp.exp(par_ref[1:2, :])
        lane = lax.broadcasted_iota(jnp.int32, (L, LANES), 1)
        lane1 = lax.broadcasted_iota(jnp.int32, (1, LANES), 1)
        rowl = lax.broadcasted_iota(jnp.int32, (L, LANES), 0)
        lo = lane < SSM_HEADDIM
        lo1 = lane1 < SSM_HEADDIM
        triT = lax.broadcasted_iota(jnp.int32, (L, L), 0) <= lax.broadcasted_iota(jnp.int32, (L, L), 1)
        sel_r = lax.broadcasted_iota(jnp.int32, (3 * LANES, LANES), 0)
        sel_c = lax.broadcasted_iota(jnp.int32, (3 * LANES, LANES), 1)
        dac_all = jnp.zeros((L, LANES), F32)
        xds_all = jnp.zeros((L, LANES), F32)
        dD_row = jnp.zeros((1, LANES), F32)

        def half_sums(v):
            return (jnp.sum(jnp.where(lo1, v, 0.0), axis=1, keepdims=True), jnp.sum(jnp.where(lo1, 0.0, v), axis=1, keepdims=True))

        def dot2(v, sel):
            hi = v.astype(BF)
            return _dot_nn(hi, sel) + _dot_nn((v - hi.astype(F32)).astype(BF), sel)

        for pj in range(G * PP):
            g, j = divmod(pj, PP)
            if j == 0:
                Bb = b_ref[:, g * N:(g + 1) * N]
                Cb = c_ref[:, g * N:(g + 1) * N]
                CbT = Cb.astype(F32).T.astype(BF)
                Gm = _dot_nt(Cb, Bb)
                GmT = _dot_nt(Bb, Cb)
                dG = jnp.zeros((L, L), F32)
                dGT = jnp.zeros((L, L), F32)
                dBacc = jnp.zeros((L, N), F32)
                dCacc = jnp.zeros((L, N), F32)
            h0, h1 = 2 * pj, 2 * pj + 1
            to_h0 = (sel_r < LANES) | ((sel_r >= 2 * LANES) & (sel_r < 2 * LANES + SSM_HEADDIM))
            sel3 = jnp.where(sel_c == jnp.where(to_h0, h0, h1), 1.0, 0.0).astype(BF)
            sel1 = sel3[2 * LANES:]
            sl = slice(pj * LANES, (pj + 1) * LANES)
            x = x_ref[:, sl].astype(F32)
            dyv = dy_ref[:, sl].astype(F32)
            dt_l = jnp.where(lo, dt_all[:, h0:h0 + 1], dt_all[:, h1:h1 + 1])
            ac0 = acum_all[:, h0:h0 + 1]
            ac1 = acum_all[:, h1:h1 + 1]
            r0 = acumT[h0:h0 + 1, :]
            r1 = acumT[h1:h1 + 1, :]
            ac_l = jnp.where(lo, ac0, ac1)
            E0 = jnp.exp(jnp.where(tri, ac0 - r0, -1e30))
            E1 = jnp.exp(jnp.where(tri, ac1 - r1, -1e30))
            E0T = jnp.exp(jnp.where(triT, r0 - ac0, -1e30))
            E1T = jnp.exp(jnp.where(triT, r1 - ac1, -1e30))
            xd = x * dt_l
            xdb = xd.astype(BF)
            M0 = Gm * E0
            M1 = Gm * E1
            ea_l = jnp.exp(ac_l)
            al0 = ac0[L - 1:L, :]
            al1 = ac1[L - 1:L, :]
            dte_l = jnp.exp(jnp.where(lo, al0, al1) - ac_l)
            Wm = xd * dte_l
            prevT = st_in_ref[0, pj]
            prevTb = prevT.astype(BF)
            P = _dot_nn(Cb, prevTb)
            D_l = jnp.where(lo1, par_ref[2:3, h0:h0 + 1], par_ref[2:3, h1:h1 + 1])
            dx = D_l * dyv
            dD0, dD1 = half_sums(jnp.sum(dyv * x, axis=0, keepdims=True))
            dyb = dyv.astype(BF)
            dy0b = jnp.where(lo, dyv, 0.0).astype(BF)
            dy1b = jnp.where(lo, 0.0, dyv).astype(BF)
            dM0 = _dot_nt(dy0b, xdb)
            dM1 = _dot_nt(dy1b, xdb)
            dM0T = _dot_nt(xdb, dy0b)
            dM1T = _dot_nt(xdb, dy1b)
            M0T = GmT * E0T
            M1T = GmT * E1T
            dxd = jnp.where(lo, _dot_nn(M0T.astype(BF), dyb), _dot_nn(M1T.astype(BF), dyb))
            dG = dG + dM0 * E0 + dM1 * E1
            dGT = dGT + dM0T * E0T + dM1T * E1T
            z0 = dM0 * M0 - dM0T * M0T
            z1 = dM1 * M1 - dM1T * M1T
            dP = dyv * ea_l
            dPb = dP.astype(BF)
            dCacc = dCacc + _dot_nt(dPb, prevTb)
            dprevT = _dot_nn(CbT, dPb)
            dnewT = dst_ref[pj]
            dnewTb = dnewT.astype(BF)
            e0 = jnp.exp(al0)
            e1 = jnp.exp(al1)
            dprevT = dprevT + jnp.where(lo1, e0, e1) * dnewT
            u0, u1 = half_sums(jnp.sum(dnewT * prevT, axis=0, keepdims=True))
            dW = _dot_nn(Bb, dnewTb)
            dBacc = dBacc + _dot_nt(Wm.astype(BF), dnewTb)
            dxd = dxd + dW * dte_l
            tt = dW * Wm
            t0, t1 = half_sums(jnp.sum(tt, axis=0, keepdims=True))
            dal0 = u0 * e0 + t0
            dal1 = u1 * e1 + t1
            dac_all = dac_all + dot2(jnp.concatenate([z0, z1, dP * P - tt], axis=1), sel3)
            dac_all = dac_all + jnp.where(rowl == L - 1, jnp.where(lane == h0, dal0, 0.0) + jnp.where(lane == h1, dal1, 0.0), 0.0)
            dx = dx + dxd * dt_l
            xds_all = xds_all + dot2(dxd * x, sel1)
            dst_ref[pj] = dprevT
            d_ref[:, sl] = dx.astype(d_ref.dtype)
            dD_row = dD_row + jnp.where(lane1 == h0, dD0, 0.0) + jnp.where(lane1 == h1, dD1, 0.0)
            if j == PP - 1:
                d_ref[:, inner + g * N:inner + (g + 1) * N] = (dBacc + _dot_nn(dGT.astype(BF), Cb)).astype(d_ref.dtype)
                d_ref[:, inner + (G + g) * N:inner + (G + g + 1) * N] = (dCacc + _dot_nn(dG.astype(BF), Bb)).astype(d_ref.dtype)

        row_i = lax.broadcasted_iota(jnp.int32, (L, L), 0)
        col_i = lax.broadcasted_iota(jnp.int32, (L, L), 1)
        da_all = _tri_matmul((row_i <= col_i).astype(BF), dac_all)
        real = lane < heads
        ddt_all = da_all * A_row + xds_all
        draw = jnp.where(real, ddt_all * _sigmoid(pre), 0.0)
        ddt_ref[...] = draw
        dpar_ref[0:1, :] += jnp.sum(draw, axis=0, keepdims=True)
        dpar_ref[1:2, :] += jnp.sum(jnp.where(real, da_all * a_all, 0.0), axis=0, keepdims=True)
        dpar_ref[2:3, :] += dD_row

    gn = G * N
    conv_dim = xbc.shape[1]
    rev = lambda c: nc - 1 - c
    return pl.pallas_call(
        body, name=name, grid=(nc,),
        in_specs=[pl.BlockSpec((L, inner), lambda c: (rev(c), 0)), pl.BlockSpec((L, inner), lambda c: (rev(c), 0)),
                  pl.BlockSpec((L, gn), lambda c: (rev(c), inner // gn)), pl.BlockSpec((L, gn), lambda c: (rev(c), inner // gn + 1)),
                  pl.BlockSpec((L, 4 * LANES), lambda c: (rev(c), 0)), pl.BlockSpec((1, LANES, L), lambda c: (rev(c), 0, 0)),
                  pl.BlockSpec((None, 8, LANES), lambda c: (l, 0, 0)),
                  pl.BlockSpec((1, G * PP, N, LANES), lambda c: (rev(c), 0, 0, 0))],
        out_specs=[pl.BlockSpec((L, conv_dim), lambda c: (rev(c), 0)), pl.BlockSpec((L, LANES), lambda c: (rev(c), 0)),
                   pl.BlockSpec((8, LANES), lambda c: (0, 0))],
        out_shape=[jax.ShapeDtypeStruct((S, conv_dim), BF), jax.ShapeDtypeStruct((S, LANES), F32), jax.ShapeDtypeStruct((8, LANES), F32)],
        scratch_shapes=[pltpu.VMEM((G * PP, N, LANES), F32)],
        compiler_params=_cp(("arbitrary",)),
    )(dy, xbc, xbc, xbc, scal, acum_t, par, states)


def _adamw(g, w, m, v):
    m2 = ADAM_B1 * m + (1.0 - ADAM_B1) * g
    v2 = ADAM_B2 * v + (1.0 - ADAM_B2) * (g * g)
    m_hat = m2 / (1.0 - ADAM_B1 ** ADAM_STEP)
    v_hat = v2 / (1.0 - ADAM_B2 ** ADAM_STEP)
    delta = -ADAM_LR * (m_hat / (jnp.sqrt(v_hat) + ADAM_EPS) + ADAM_WD * w)
    return delta, m2, v2


def _flat_tile(R):
    return _pick(R, (FLAT_ROW_TILE, 1024, 512, 256, 128, 64, 32, 16, 8))


def _sum_adam(name, lands, off, w, m, v):
    depth, r, c = w.shape
    cap = max(16, (4 * 1024 * 1024) // (N_DEV * c * 2))
    row_tiles = [t for t in (512, 256, 128, 64, 32, 16) if r % t == 0 and off % t == 0 and t <= cap]
    if row_tiles:
        tr, tc = row_tiles[0], c
        ob = off // tr
        n_t = r // tr
        spec = pl.BlockSpec((None, tr, c), lambda l, t: (l, t, 0))
        land_specs = [pl.BlockSpec((N_DEV, tr, c), lambda l, t, i=i: (0, jnp.where(l == i, ob + t, ob), 0)) for i in range(depth)]
    else:
        assert off == 0 and lands[0].shape[1] == r and c % LANES == 0
        tc = LANES
        n_t = c // tc
        spec = pl.BlockSpec((None, r, tc), lambda l, t: (l, 0, t))
        land_specs = [pl.BlockSpec((N_DEV, r, tc), lambda l, t, i=i: (0, 0, jnp.where(l == i, t, 0))) for i in range(depth)]

    def body(*refs):
        land_refs = refs[:depth]
        w_ref, m_ref, v_ref, g_ref, d_ref, m2_ref, v2_ref = refs[depth:]
        l = pl.program_id(0)
        for i in range(depth):
            @pl.when(l == i)
            def _(i=i):
                g = land_refs[i][0].astype(F32)
                for k in range(1, N_DEV):
                    g = g + land_refs[i][k].astype(F32)
                g_ref[...] = g
                d_ref[...], m2_ref[...], v2_ref[...] = _adamw(g, w_ref[...], m_ref[...], v_ref[...])

    return pl.pallas_call(
        body, name=name, grid=(depth, n_t),
        in_specs=land_specs + [spec, spec, spec],
        out_specs=[spec] * 4, out_shape=[jax.ShapeDtypeStruct((depth, r, c), F32)] * 4,
        compiler_params=_cp(("arbitrary", "arbitrary")),
    )(*lands, w, m, v)


def _sum8(name, parts):
    R = parts.shape[1]
    TR = _flat_tile(R)

    def body(p_ref, g_ref):
        g = p_ref[0]
        for k in range(1, N_DEV):
            g = g + p_ref[k]
        g_ref[...] = g

    return pl.pallas_call(
        body, name=name, grid=(R // TR,),
        in_specs=[pl.BlockSpec((N_DEV, TR, LANES), lambda i: (0, i, 0))],
        out_specs=pl.BlockSpec((TR, LANES), lambda i: (i, 0)), out_shape=jax.ShapeDtypeStruct((R, LANES), F32),
        compiler_params=_cp(("parallel",)),
    )(parts)


def _adam_flat(name, g, w, m, v):
    R = w.shape[0]
    TR = _flat_tile(R)

    def body(g_ref, w_ref, m_ref, v_ref, d_ref, m2_ref, v2_ref):
        d_ref[...], m2_ref[...], v2_ref[...] = _adamw(g_ref[...], w_ref[...], m_ref[...], v_ref[...])

    spec = pl.BlockSpec((TR, LANES), lambda i: (i, 0))
    return pl.pallas_call(
        body, name=name, grid=(R // TR,), in_specs=[spec] * 4, out_specs=[spec] * 3,
        out_shape=[jax.ShapeDtypeStruct((R, LANES), F32)] * 3, compiler_params=_cp(("parallel",)),
    )(g, w, m, v)


PART_ROWS = 16


def _nrows(shape):
    n = 1
    for s in shape:
        n *= s
    r = -(-n // LANES)
    return -(-r // PART_ROWS) * PART_ROWS


def _as_rows(a):
    n = a.size
    r = _nrows(a.shape)
    f = a.reshape(-1)
    if r * LANES != n:
        f = jnp.concatenate([f, jnp.zeros((r * LANES - n,), a.dtype)])
    return f.reshape(r, LANES)


def _pack(arrs, mult=PART_ROWS):
    cat = jnp.concatenate([_as_rows(a) for a in arrs], axis=0)
    pad = (-cat.shape[0]) % mult
    if pad:
        cat = jnp.concatenate([cat, jnp.zeros((pad, LANES), cat.dtype)], axis=0)
    return cat


def _unpack(flat, shapes):
    lead = flat.shape[:-2]
    out = []
    o = 0
    for shp in shapes:
        n = 1
        for s in shp:
            n *= s
        r = _nrows(shp)
        blk = flat[..., o:o + r, :].reshape(lead + (r * LANES,))
        out.append(blk[..., :n].reshape(lead + tuple(shp)))
        o += r
    return out


def _full_from_shards(st):
    return st.reshape(st.shape[0] * st.shape[1], st.shape[2])


def _shards_from_full(full):
    return full.reshape(N_DEV, full.shape[0] // N_DEV, full.shape[1])


def _ffn_fwd(tag, h, g, wgT, wuT, wd, dep=None):
    xn = _rms_fwd(tag + "_rms", h, g, dep=dep)
    a, b, hmid = _ffn_up(tag + "_up", xn, wgT, wuT)
    hout = _mm(tag + "_down", hmid, wd, out_dtype=F32, res=h, alpha=0.5)
    return hout, (xn, a, b, hmid)


def _ffn_bwd(tag, dh_out, h, g, wgT, wuT, wd, saved, dep=None):
    xn, a, b, hmid = saved
    da, db = _ffn_dact(tag + "_d_act", dh_out, wd, a, b, dep=dep)
    d_wd = _mm(tag + "_d_wd", hmid, dh_out, ta=True, alpha=0.5)
    d_wgT = _mm(tag + "_d_wg", da, xn, ta=True)
    d_wuT = _mm(tag + "_d_wu", db, xn, ta=True)
    dxn = _mm(tag + "_d_xn_g", da, wgT, out_dtype=F32)
    dh, dg = _mm_rms_bwd(tag + "_d_xn_u", db, wuT, dxn, h, g, dh_out)
    return dh, dg, d_wgT, d_wuT, d_wd


SEG_NAMES = ['scb', 'scc', 'scx', 'z', 'xbc', 'dt', 'ga', 'gm']
PERM = ['z', 'scb', 'scc', 'scx', 'ga', 'gm', 'xbc']


def _seg_layout(dims):
    D, inner, conv_dim, H = dims[:4]
    widths = dict(zip(SEG_NAMES, [D, D, D, inner, conv_dim, H, D, D]))
    offs, o = {}, 0
    for n in SEG_NAMES:
        offs[n] = (o, widths[n])
        o += widths[n]
    poffs, o = {}, 0
    for n in PERM:
        poffs[n] = (o, widths[n])
        o += widths[n]
    return offs, poffs


def _perm_w_in(w_inT, dims):
    offs, _ = _seg_layout(dims)
    wp = jnp.concatenate([w_inT[offs[n][0]:offs[n][0] + offs[n][1]] for n in PERM], axis=0)
    o, w = offs['dt']
    wdt = jnp.concatenate([w_inT[o:o + w], jnp.zeros((LANES - w, w_inT.shape[1]), w_inT.dtype)], axis=0)
    return wp, wdt


def _unperm_d_w_in(d_wp, d_wdt, dims):
    offs, poffs = _seg_layout(dims)
    H = dims[3]
    return jnp.concatenate([d_wdt[:H] if n == 'dt' else d_wp[poffs[n][0]:poffs[n][0] + poffs[n][1]] for n in SEG_NAMES], axis=0)


def _mixer_fwd(h, W, dims, dep=None):
    H, Ksc, Km = dims[3:]
    l = W['l']
    _, poffs = _seg_layout(dims)

    def seg(n):
        o, w = poffs[n]
        assert o % w == 0
        return (proj, w, o // w)

    u = _rms_fwd("mix_rms", h, W['mix_norm'], dep=dep)
    proj = _mm("inproj", u, W['w_in_p'], tb=True)
    dt_raw = _mm("inproj_dt", u, W['w_dt'], tb=True, out_dtype=F32)
    v = _scconv_fwd("scconv_f", proj, poffs['scb'][0], poffs['scc'][0], poffs['scx'][0], W['sc_taps'], Ksc, l)
    ya = _mm("sc_out", v, W['sc_w_out'])
    xbc = _mconv_fwd("mconv_f", proj, poffs['xbc'][0], W['m_taps'], Km, W['m_conv_b'], l)
    y, states, scal, acum_t = _ssd_fwd("ssd_f", xbc, dt_raw, W['ssd_par'], l, H)
    yn = _gnorm_fwd("gnorm_f", y, seg('z'), W['m_norm'])
    ym = _mm("m_out", yn, W['m_w_out'])
    merged = _merge_fwd("merge_f", seg('ga'), seg('gm'), ya, ym)
    hout = _mm("w_o", merged, W['w_o'], out_dtype=F32, res=h)
    return hout, (u, proj, (scal, acum_t), v, ya, xbc, y, states, yn, ym, merged)


def _mixer_bwd(dh_out, h, W, dims, saved, dep=None):
    u, proj, (scal, acum_t), v, ya, xbc, y, states, yn, ym, merged = saved
    H, Ksc, Km = dims[3:]
    l = W['l']
    _, poffs = _seg_layout(dims)

    def seg(n):
        o, w = poffs[n]
        return (proj, w, o // w)

    g = {}
    dmerged = _mm("d_merged", dh_out, W['w_o'], tb=True, dep=dep)
    g['w_o'] = _mm("d_w_o", merged, dh_out, ta=True)
    dga, dgm, dya, dym = _merge_bwd("merge_b", dmerged, seg('ga'), seg('gm'), ya, ym)
    g['sc_w_out'] = _mm("d_sc_w_out", v, dya, ta=True)
    dv = _mm("d_v", dya, W['sc_w_out'], tb=True)
    g['m_w_out'] = _mm("d_m_w_out", yn, dym, ta=True)
    dyn = _mm("d_yn", dym, W['m_w_out'], tb=True)
    dy, dz, d_mnorm = _gnorm_bwd("gnorm_b", dyn, y, seg('z'), W['m_norm'])
    g['m_norm'] = d_mnorm.reshape(-1)
    dxbc_post, ddt, dpar = _ssd_bwd("ssd_b", dy, xbc, scal, acum_t, states, W['ssd_par'], l, H)
    g['m_dt_bias'] = dpar[0, :H]
    g['m_A_log'] = dpar[1, :H]
    g['m_D'] = dpar[2, :H]
    dxbc, d_mcw, d_mcb = _mconv_bwd("mconv_b", dxbc_post, proj, poffs['xbc'][0], W['m_taps'], Km, W['m_conv_b'], l)
    g['m_conv_w'] = d_mcw[:Km]
    g['m_conv_b'] = d_mcb.reshape(-1)
    dscb, dscc, dscx, d_scw = _scconv_bwd("scconv_b", dv, proj, poffs['scb'][0], poffs['scc'][0], poffs['scx'][0],
                                          W['sc_taps'], Ksc, l)
    g['sc_conv_w'] = d_scw[:Ksc]
    dproj = jnp.concatenate([dz, dscb, dscc, dscx, dga, dgm, dxbc], axis=1)
    du = _mm("d_u_main", dproj, W['w_in_p'], out_dtype=F32)
    dh, dg = _mm_rms_bwd("d_u_dt", ddt, W['w_dt'], du, h, W['mix_norm'], dh_out)
    d_wp = _mm("d_w_in_main", dproj, u, ta=True)
    d_wdt = _mm("d_w_in_dt", ddt, u, ta=True)
    g['w_in'] = _unperm_d_w_in(d_wp, d_wdt, dims)
    g['mix_norm'] = dg.reshape(-1)
    return dh, g


def _ple_layer_fwd(h, p_l, W):
    xn = _rms_fwd("ple_rms", h, W['ple_norm'])
    gpre = _mm("ple_gate", xn, W['ple_w_gate'])
    pp = _mm("ple_proj", p_l, W['ple_w_proj'], tb=True)
    hout = _ple_fwd("ple_f", h, gpre, pp)
    return hout, (xn, gpre, pp)


def _ple_layer_bwd(dh_out, h, p_l, W, saved, dep=None):
    xn, gpre, pp = saved
    g = {}
    dgpre, dpp = _ple_bwd("ple_b", dh_out, gpre, pp, dep=dep)
    g['ple_w_proj'] = _mm("d_ple_proj", dpp, p_l, ta=True)
    g['ple_w_gate'] = _mm("d_ple_gate", xn, dgpre, ta=True)
    dh, dg = _mm_rms_bwd("d_ple_xn", dgpre, W['ple_w_gate'], None, h, W['ple_norm'], dh_out, tb=True)
    g['ple_norm'] = dg.reshape(-1)
    return dh, g


def kernel(x, p, ffn1_norm, ffn1_wg, ffn1_wu, ffn1_wd, mix_norm, w_in, sc_conv_w, sc_w_out, m_conv_w, m_conv_b, m_dt_bias, m_A_log, m_D, m_norm, m_w_out, w_o, ffn2_norm, ffn2_wg, ffn2_wu, ffn2_wd, ple_norm, ple_w_gate, ple_w_proj, final_norm, loss_target, m_ffn1_norm, m_ffn1_wg, m_ffn1_wu, m_ffn1_wd, m_mix_norm, m_w_in, m_sc_conv_w, m_sc_w_out, m_m_conv_w, m_m_conv_b, m_m_dt_bias, m_m_A_log, m_m_D, m_m_norm, m_m_w_out, m_w_o, m_ffn2_norm, m_ffn2_wg, m_ffn2_wu, m_ffn2_wd, m_ple_norm, m_ple_w_gate, m_ple_w_proj, m_final_norm, v_ffn1_norm, v_ffn1_wg, v_ffn1_wu, v_ffn1_wd, v_mix_norm, v_w_in, v_sc_conv_w, v_sc_w_out, v_m_conv_w, v_m_conv_b, v_m_dt_bias, v_m_A_log, v_m_D, v_m_norm, v_m_w_out, v_w_o, v_ffn2_norm, v_ffn2_wg, v_ffn2_wu, v_ffn2_wd, v_ple_norm, v_ple_w_gate, v_ple_w_proj, v_final_norm):
    args = (x, p, ffn1_norm, ffn1_wg, ffn1_wu, ffn1_wd, mix_norm, w_in, sc_conv_w, sc_w_out, m_conv_w, m_conv_b, m_dt_bias, m_A_log, m_D, m_norm, m_w_out, w_o, ffn2_norm, ffn2_wg, ffn2_wu, ffn2_wd, ple_norm, ple_w_gate, ple_w_proj, final_norm, loss_target, m_ffn1_norm, m_ffn1_wg, m_ffn1_wu, m_ffn1_wd, m_mix_norm, m_w_in, m_sc_conv_w, m_sc_w_out, m_m_conv_w, m_m_conv_b, m_m_dt_bias, m_m_A_log, m_m_D, m_m_norm, m_m_w_out, m_w_o, m_ffn2_norm, m_ffn2_wg, m_ffn2_wu, m_ffn2_wd, m_ple_norm, m_ple_w_gate, m_ple_w_proj, m_final_norm, v_ffn1_norm, v_ffn1_wg, v_ffn1_wu, v_ffn1_wd, v_mix_norm, v_w_in, v_sc_conv_w, v_sc_w_out, v_m_conv_w, v_m_conv_b, v_m_dt_bias, v_m_A_log, v_m_D, v_m_norm, v_m_w_out, v_w_o, v_ffn2_norm, v_ffn2_wg, v_ffn2_wu, v_ffn2_wd, v_ple_norm, v_ple_w_gate, v_ple_w_proj, v_final_norm)
    names = ARG_NAMES + ['m_' + n for n in WEIGHTS] + ['v_' + n for n in WEIGHTS]
    A = dict(zip(names, args))
    depth = ffn1_norm.shape[0]
    me = 4 * lax.axis_index("x") + 2 * lax.axis_index("y") + lax.axis_index("c")

    dims = (x.shape[-1], m_norm.shape[1], m_conv_b.shape[1], m_dt_bias.shape[1], sc_conv_w.shape[1], m_conv_w.shape[1])
    kind = dict(BIG)

    def work(n, prefix=''):
        return jnp.swapaxes(A[prefix + n], 1, 2) if kind[n] == 'col' else A[prefix + n]

    wb = {n: work(n).astype(BF) for n, _ in BIG}
    srcs = [[wb[ms[0]] if len(ms) == 1 else jnp.concatenate([wb[n] for n in ms], axis=1) for ms in stage] for stage in STAGES]
    conv_g = _unpack(_exchange("gather_conv_taps", _pack([A[n] for n in CONVW]), True), [A[n].shape for n in CONVW])
    taps = {}
    for n, st in zip(CONVW, conv_g):
        taps[n] = _pad_taps(jnp.transpose(st, (1, 2, 0, 3)).reshape(depth, st.shape[2], N_DEV * st.shape[3]))
    ssd_par = _ssd_params(m_dt_bias, m_A_log, m_D)
    small3 = {n: A[n].reshape(depth, 1, -1) for n in SMALL}

    def stage_weights(W, s, l, lands):
        for ms, land in zip(STAGES[s], lands):
            off = 0
            for n in ms:
                r = wb[n].shape[1]
                W[n] = _full_from_shards(land if len(ms) == 1 else land[:, off:off + r])
                off += r
        if s == 1:
            W['w_in_p'], W['w_dt'] = _perm_w_in(W.pop('w_in'), dims)

    flight = {}

    via_sibling = {(0, 0), (0, 1)}

    def begin_layer(l, dep):
        for s in range(len(STAGES)):
            rels = NEAR_PEERS if (l, s) in via_sibling else ALL_PEERS
            sems, lands, dep = _xchg_begin(f"gather_begin{l}{'abc'[s]}", srcs[s], l, dep, rels)
            flight[(l, s)] = (sems, lands)
        return dep

    def end_stage(W, l, s, after):
        sems, lands = flight.pop((l, s))
        tag = f"{l}{'abc'[s]}"
        if (l, s) in via_sibling:
            lands = _xchg_end("gather_end" + tag, srcs[s], lands, sems, l, after, NEAR_PEERS)
            rsems, lands = _relay_begin("gather_relay" + tag, lands)
            lands = _relay_end("gather_relayed" + tag, lands, rsems, after)
        else:
            lands = _xchg_end("gather_end" + tag, srcs[s], lands, sems, l, after)
        stage_weights(W, s, l, lands)

    tok = begin_layer(0, taps['sc_conv_w'])
    h = x[0]
    saved = []
    layers = []
    for l in range(depth):
        W = {n: (small3[n], l) for n in SMALL}
        W.update(l=l, sc_taps=taps['sc_conv_w'], m_taps=taps['m_conv_w'], m_conv_b=small3['m_conv_b'], ssd_par=ssd_par)
        layers.append(W)
        end_stage(W, l, 0, tok if l == 0 else h)
        h1, s1 = _ffn_fwd("ffn1", h, W['ffn1_norm'], W['ffn1_wg'], W['ffn1_wu'], W['ffn1_wd'])
        end_stage(W, l, 1, h1)
        tok = begin_layer(l + 1, W['w_dt']) if l + 1 < depth else None
        h2, s2 = _mixer_fwd(h1, W, dims, dep=tok)
        end_stage(W, l, 2, h2)
        h3, s3 = _ffn_fwd("ffn2", h2, W['ffn2_norm'], W['ffn2_wg'], W['ffn2_wu'], W['ffn2_wd'])
        h4, s4 = _ple_layer_fwd(h3, p[l, 0], W)
        saved.append((h, h1, h2, h3, s1, s2, s3, s4))
        h = h4

    dh, loss_row, d_final = _loss_head("loss_head", h, final_norm, loss_target[0])
    loss = lax.psum(loss_row[0, 0], ("x", "y", "c"))

    def send_bufs(g, s):
        return [jnp.concatenate([_shards_from_full(g[n]) for n in ms], axis=1) if len(ms) > 1
                else _shards_from_full(g[ms[0]]) for ms in STAGES[s]]

    grads = [None] * depth
    pending = []

    def send_stage(g, l, s, dep):
        send = send_bufs(g, s)
        sems, lands, tok = _xchg_begin(f"scatter_begin{l}{'abc'[s]}", send, None, dep)
        pending.append((l, s, send, lands, sems))
        return tok

    tok = loss.reshape(1, 1)
    for l in reversed(range(depth)):
        W = layers[l]
        h0, h1, h2, h3, s1, s2, s3, s4 = saved[l]
        g = {}
        dh, g4 = _ple_layer_bwd(dh, h3, p[l, 0], W, s4, dep=tok)
        g.update(g4)
        dh, dg, d_wg, d_wu, d_wd = _ffn_bwd("ffn2", dh, h2, W['ffn2_norm'], W['ffn2_wg'], W['ffn2_wu'], W['ffn2_wd'], s3)
        g.update(ffn2_norm=dg.reshape(-1), ffn2_wg=d_wg, ffn2_wu=d_wu, ffn2_wd=d_wd)
        tok = send_stage(g, l, 2, dh)
        dh, g2 = _mixer_bwd(dh, h1, W, dims, s2, dep=tok)
        g.update(g2)
        tok = send_stage(g, l, 1, dh)
        dh, dg, d_wg, d_wu, d_wd = _ffn_bwd("ffn1", dh, h0, W['ffn1_norm'], W['ffn1_wg'], W['ffn1_wu'], W['ffn1_wd'], s1, dep=tok)
        g.update(ffn1_norm=dg.reshape(-1), ffn1_wg=d_wg, ffn1_wu=d_wu, ffn1_wd=d_wd)
        grads[l] = g
        tok = send_stage(g, l, 0, dh)
    grad_x = dh[None]

    g_lands = [[None] * len(STAGES) for _ in range(depth)]
    big_res = [{}, {}, {}, {}]

    def finish(entries, after):
        for l, s, send, lands, sems in entries:
            got = _xchg_end(f"scatter_end{l}{'abc'[s]}", send, lands, sems, None, after)
            after = got[0]
            g_lands[l][s] = got
        return after

    def adam_stages(stages):
        res = None
        for s in stages:
            for gi, ms in enumerate(STAGES[s]):
                off = 0
                for n in ms:
                    res = _sum_adam("adamw_" + n, [g_lands[l][s][gi] for l in range(depth)], off, work(n), work(n, 'm_'), work(n, 'v_'))
                    for k in range(4):
                        big_res[k][n] = jnp.swapaxes(res[k], 1, 2) if kind[n] == 'col' else res[k]
                    off += wb[n].shape[1]
        return res[0]

    after = finish(pending[:-1], tok)
    after = adam_stages(range(1, len(STAGES)))
    after = finish(pending[-1:], after)
    adam_stages([0])

    small_names = SMALL + CONVW
    small_parts = [jnp.stack([grads[l][n] for l in range(depth)]) for n in small_names] + [d_final.reshape(-1)]
    small_sum = _sum8("sum_small", _exchange("gather_small_grads", _pack(small_parts), True, dep=after))
    sg = dict(zip(small_names + ['final_norm'], _unpack(small_sum, [a.shape for a in small_parts])))
    for n in CONVW:
        c = A[n].shape[-1]
        sg[n] = lax.dynamic_slice_in_dim(sg[n], me * c, c, axis=2)
    s_order = small_names + ['final_norm']
    s_shapes = [sg[n].shape for n in s_order]
    s_out = _adam_flat("adamw_small", _pack([sg[n] for n in s_order]), _pack([A[n] for n in s_order]),
                       _pack([A['m_' + n] for n in s_order]), _pack([A['v_' + n] for n in s_order]))
    small_res = [sg] + [dict(zip(s_order, _unpack(flat, s_shapes))) for flat in s_out]

    outs = [loss, grad_x]
    for k in range(4):
        for n in WEIGHTS:
            outs.append(big_res[k][n] if n in big_res[k] else small_res[k][n])
    return tuple(outs)
```

```python
import functools

import jax
import jax.numpy as jnp
from jax import lax
from jax.experimental import pallas as pl
from jax.experimental.pallas import tpu as pltpu

BF = jnp.bfloat16
F32 = jnp.float32

EPS = 1e-6
N_DEV = 8
LANES = 128
SSM_GROUPS = 4
SSM_HEADDIM = 64
SSM_CHUNK = 128
HALO = 16
VMEM_LIMIT = 56 * 1024 * 1024
FLAT_ROW_TILE = 2048

ADAM_LR = 0.001
ADAM_B1 = 0.9
ADAM_B2 = 0.999
ADAM_EPS = 1e-08
ADAM_WD = 0.01
ADAM_STEP = 10

MESH = pl.DeviceIdType.MESH

ARG_NAMES = ['x', 'p', 'ffn1_norm', 'ffn1_wg', 'ffn1_wu', 'ffn1_wd', 'mix_norm', 'w_in', 'sc_conv_w', 'sc_w_out', 'm_conv_w', 'm_conv_b', 'm_dt_bias', 'm_A_log', 'm_D', 'm_norm', 'm_w_out', 'w_o', 'ffn2_norm', 'ffn2_wg', 'ffn2_wu', 'ffn2_wd', 'ple_norm', 'ple_w_gate', 'ple_w_proj', 'final_norm', 'loss_target']
WEIGHTS = ARG_NAMES[2:26]
BIG = [('ffn1_wg', 'col'), ('ffn1_wu', 'col'), ('ffn1_wd', 'row'), ('w_in', 'col'), ('sc_w_out', 'row'),
       ('m_w_out', 'row'), ('w_o', 'row'), ('ffn2_wg', 'col'), ('ffn2_wu', 'col'), ('ffn2_wd', 'row'),
       ('ple_w_gate', 'row'), ('ple_w_proj', 'col')]
CONVW = ['sc_conv_w', 'm_conv_w']
SMALL = ['ffn1_norm', 'mix_norm', 'm_conv_b', 'm_dt_bias', 'm_A_log', 'm_D', 'm_norm', 'ffn2_norm', 'ple_norm']


def _pick(n, cands):
    for c in cands:
        if n % c == 0:
            return c
    return n


def _cp(sem):
    return pltpu.CompilerParams(dimension_semantics=sem, vmem_limit_bytes=VMEM_LIMIT)


def _sigmoid(x):
    return 1.0 / (1.0 + jnp.exp(-x))


def _softplus(x):
    return jnp.maximum(x, 0.0) + jnp.log(1.0 + jnp.exp(-jnp.abs(x)))


def _exchange(name, x, gather, dep=None):
    slab = x.shape if gather else x.shape[1:]

    def body(x_ref, *rest):
        o_ref, send_sems, recv_sems, local_sem = rest[-4:]
        mx, my, mc = lax.axis_index("x"), lax.axis_index("y"), lax.axis_index("c")
        me = 4 * mx + 2 * my + mc

        def src_for(k):
            return x_ref if gather else x_ref.at[k]

        local = pltpu.make_async_copy(src_for(me), o_ref.at[me], local_sem)
        local.start()
        sends = []
        peers = []
        for r in range(1, N_DEV):
            px = (mx + ((r >> 2) & 1)) % 2
            py = (my + ((r >> 1) & 1)) % 2
            pc = (mc + (r & 1)) % 2
            peer = 4 * px + 2 * py + pc
            peers.append(peer)
            cp = pltpu.make_async_remote_copy(
                src_ref=src_for(peer), dst_ref=o_ref.at[me], send_sem=send_sems.at[r - 1], recv_sem=recv_sems.at[r - 1],
                device_id=(px, py, pc), device_id_type=MESH)
            cp.start()
            sends.append(cp)
        for r in range(1, N_DEV):
            peer = peers[r - 1]
            pltpu.make_async_remote_copy(
                src_ref=src_for(peer), dst_ref=o_ref.at[peer], send_sem=send_sems.at[r - 1], recv_sem=recv_sems.at[r - 1],
                device_id=(mx, my, mc), device_id_type=MESH).wait_recv()
        for cp in sends:
            cp.wait_send()
        local.wait()

    return pl.pallas_call(
        body, name=name,
        out_shape=jax.ShapeDtypeStruct((N_DEV,) + tuple(slab), x.dtype),
        in_specs=[pl.BlockSpec(memory_space=pltpu.HBM)] + ([] if dep is None else [pl.BlockSpec(memory_space=pl.ANY)]),
        out_specs=pl.BlockSpec(memory_space=pltpu.HBM),
        scratch_shapes=[pltpu.SemaphoreType.DMA((N_DEV - 1,)), pltpu.SemaphoreType.DMA((N_DEV - 1,)), pltpu.SemaphoreType.DMA],
    )(*([x] if dep is None else [x, dep]))


STAGES = [[['ffn1_wd'], ['ffn1_wg'], ['ffn1_wu']],
          [['w_in'], ['sc_w_out', 'w_o', 'ple_w_gate', 'm_w_out']],
          [['ffn2_wd'], ['ffn2_wg'], ['ffn2_wu'], ['ple_w_proj']]]
_HBM = pl.BlockSpec(memory_space=pltpu.HBM)
_SEM = pl.BlockSpec(memory_space=pltpu.SEMAPHORE)
_ANY = pl.BlockSpec(memory_space=pl.ANY)
_EFFECT = pltpu.SideEffectType.DATAFLOW_SIDE_EFFECTING


def _peer_list():
    mx, my, mc = lax.axis_index("x"), lax.axis_index("y"), lax.axis_index("c")
    out = []
    for r in range(1, N_DEV):
        px = (mx + ((r >> 2) & 1)) % 2
        py = (my + ((r >> 1) & 1)) % 2
        pc = (mc + (r & 1)) % 2
        out.append((px, py, pc, 4 * px + 2 * py + pc))
    return 4 * mx + 2 * my + mc, out


ALL_PEERS = tuple(range(1, N_DEV))
NEAR_PEERS = (1, 2, 4, 6)
RELAYED = (2, 4, 6)


def _xchg_copy(src_refs, land_refs, send_sems, recv_sems, layer, i, r, peer, dst_slab):
    px, py, pc, pidx = peer
    n = len(src_refs)
    src = src_refs[i].at[layer] if layer is not None else src_refs[i].at[pidx]
    return pltpu.make_async_remote_copy(
        src_ref=src, dst_ref=land_refs[i].at[dst_slab], send_sem=send_sems.at[r * n + i], recv_sem=recv_sems.at[r * n + i],
        device_id=(px, py, pc), device_id_type=MESH)


def _own_copy(src_refs, land_refs, send_sems, layer, i, r, me):
    n = len(src_refs)
    src = src_refs[i].at[layer] if layer is not None else src_refs[i].at[me]
    return pltpu.make_async_copy(src, land_refs[i].at[me], send_sems.at[r * n + i])


def _xchg_begin(name, srcs, layer, dep, rels=ALL_PEERS):
    n = len(srcs)
    slabs = [tuple(s.shape[1:]) for s in srcs]
    ncp = n * (len(rels) + 1)

    def body(*refs):
        src_refs, land_refs = refs[:n], refs[n:2 * n]
        send_sems, recv_sems = refs[2 * n + 1], refs[2 * n + 2]
        token = refs[-1]
        me, peers = _peer_list()
        for ri, r in enumerate(rels):
            for i in range(n):
                _xchg_copy(src_refs, land_refs, send_sems, recv_sems, layer, i, ri, peers[r - 1], me).start()
        for i in range(n):
            _own_copy(src_refs, land_refs, send_sems, layer, i, len(rels), me).start()
        token[...] = jnp.zeros_like(token)

    lands = [pltpu.with_memory_space_constraint(lax.empty((N_DEV,) + sl, s.dtype), pltpu.HBM) for sl, s in zip(slabs, srcs)]
    out = pl.pallas_call(
        body, name=name,
        out_shape=(pltpu.SemaphoreType.DMA((ncp,)), pltpu.SemaphoreType.DMA((ncp,)),
                   *[pltpu.HBM((N_DEV,) + sl, s.dtype) for sl, s in zip(slabs, srcs)], jax.ShapeDtypeStruct((8, LANES), F32)),
        in_specs=[_HBM] * (2 * n) + [_ANY],
        out_specs=(_SEM, _SEM, *[_HBM] * n, pl.BlockSpec(memory_space=pltpu.VMEM)),
        input_output_aliases={n + i: 2 + i for i in range(n)},
        compiler_params=pltpu.CompilerParams(has_side_effects=_EFFECT),
    )(*[pltpu.with_memory_space_constraint(s, pltpu.HBM) for s in srcs], *lands, dep)
    return (out[0], out[1]), list(out[2:2 + n]), out[-1]


def _xchg_end(name, srcs, lands, sems, layer, after, rels=ALL_PEERS):
    n = len(srcs)

    def body(*refs):
        src_refs, land_refs = refs[:n], refs[n:2 * n]
        send_sems, recv_sems = refs[2 * n], refs[2 * n + 1]
        me, peers = _peer_list()
        for ri, r in enumerate(rels):
            for i in range(n):
                cp = _xchg_copy(src_refs, land_refs, send_sems, recv_sems, layer, i, ri, peers[r - 1], peers[r - 1][3])
                cp.wait_send()
                cp.wait_recv()
        for i in range(n):
            _own_copy(src_refs, land_refs, send_sems, layer, i, len(rels), me).wait()

    out = pl.pallas_call(
        body, name=name,
        out_shape=tuple(pltpu.HBM(l.shape, l.dtype) for l in lands),
        in_specs=[_HBM] * (2 * n) + [_SEM, _SEM, _ANY], out_specs=tuple([_HBM] * n),
        input_output_aliases={n + i: i for i in range(n)},
        compiler_params=pltpu.CompilerParams(has_side_effects=_EFFECT),
    )(*[pltpu.with_memory_space_constraint(s, pltpu.HBM) for s in srcs], *lands, sems[0], sems[1], after)
    return list(out)


def _relay_copy(land_refs, send_sems, recv_sems, i, qi, slab, sibling):
    n = len(land_refs)
    return pltpu.make_async_remote_copy(
        src_ref=land_refs[i].at[slab], dst_ref=land_refs[i].at[slab], send_sem=send_sems.at[qi * n + i],
        recv_sem=recv_sems.at[qi * n + i], device_id=sibling[:3], device_id_type=MESH)


def _relay_begin(name, lands):
    n = len(lands)
    ncp = n * len(RELAYED)

    def body(*refs):
        land_refs = refs[:n]
        send_sems, recv_sems = refs[n], refs[n + 1]
        me, peers = _peer_list()
        for qi, q in enumerate(RELAYED):
            for i in range(n):
                _relay_copy(land_refs, send_sems, recv_sems, i, qi, peers[q - 1][3], peers[0]).start()

    out = pl.pallas_call(
        body, name=name,
        out_shape=(pltpu.SemaphoreType.DMA((ncp,)), pltpu.SemaphoreType.DMA((ncp,)), *[pltpu.HBM(l.shape, l.dtype) for l in lands]),
        in_specs=[_HBM] * n, out_specs=(_SEM, _SEM, *[_HBM] * n),
        input_output_aliases={i: 2 + i for i in range(n)},
        compiler_params=pltpu.CompilerParams(has_side_effects=_EFFECT),
    )(*lands)
    return (out[0], out[1]), list(out[2:])


def _relay_end(name, lands, sems, after):
    n = len(lands)

    def body(*refs):
        land_refs = refs[:n]
        send_sems, recv_sems = refs[n], refs[n + 1]
        me, peers = _peer_list()
        for qi, q in enumerate(RELAYED):
            for i in range(n):
                _relay_copy(land_refs, send_sems, recv_sems, i, qi, peers[q - 1][3], peers[0]).wait_send()
                _relay_copy(land_refs, send_sems, recv_sems, i, qi, peers[q][3], peers[0]).wait_recv()

    out = pl.pallas_call(
        body, name=name,
        out_shape=tuple(pltpu.HBM(l.shape, l.dtype) for l in lands),
        in_specs=[_HBM] * n + [_SEM, _SEM, _ANY], out_specs=tuple([_HBM] * n),
        input_output_aliases={i: i for i in range(n)},
        compiler_params=pltpu.CompilerParams(has_side_effects=_EFFECT),
    )(*lands, sems[0], sems[1], after)
    return list(out)


MM_TILES = (1024, 1408, 512, 256, 128)
MM_OPERAND_BYTES = 24 * 1024 * 1024


def _mm(name, a, b, *, ta=False, tb=False, out_dtype=None, res=None, alpha=1.0, dep=None):
    out_dtype = out_dtype or BF
    M, K = (a.shape[1], a.shape[0]) if ta else a.shape
    N = b.shape[0] if tb else b.shape[1]
    assert (b.shape[1] if tb else b.shape[0]) == K, (name, a.shape, b.shape)
    tm = _pick(M, MM_TILES)
    tn = _pick(N, MM_TILES)
    per_k = 2 * (tm * a.dtype.itemsize + tn * b.dtype.itemsize)
    tk = [t for t in sorted({K, 4096, 2816, 2560, 2048, 1408, 1024, 512, 256, 128}, reverse=True)
          if K % t == 0 and (t * per_k <= MM_OPERAND_BYTES or t == 128)][0]
    nk = K // tk
    a_spec = pl.BlockSpec((tk, tm), lambda i, j, k: (k, i)) if ta else pl.BlockSpec((tm, tk), lambda i, j, k: (i, k))
    b_spec = pl.BlockSpec((tn, tk), lambda i, j, k: (j, k)) if tb else pl.BlockSpec((tk, tn), lambda i, j, k: (k, j))
    dn = (((0 if ta else 1,), (1 if tb else 0,)), ((), ()))
    has_res = res is not None
    n_dep = 0 if dep is None else 1

    def body(*refs):
        a_ref, b_ref = refs[:2]
        r_ref = refs[2] if has_res else None
        o_ref = refs[2 + has_res + n_dep]

        def finish(v):
            if alpha != 1.0:
                v = v * alpha
            if has_res:
                v = r_ref[...] + v
            o_ref[...] = v.astype(o_ref.dtype)

        part = lax.dot_general(a_ref[...].astype(BF), b_ref[...].astype(BF), dn, preferred_element_type=F32)
        if nk == 1:
            finish(part)
            return
        acc = refs[-1]
        k = pl.program_id(2)

        @pl.when(k == 0)
        def _():
            acc[...] = part

        @pl.when((k > 0) & (k < nk - 1))
        def _():
            acc[...] += part

        @pl.when(k == nk - 1)
        def _():
            finish(acc[...] + part)

    in_specs = [a_spec, b_spec]
    args = [a, b]
    if has_res:
        in_specs.append(pl.BlockSpec((tm, tn), lambda i, j, k: (i, j)))
        args.append(res)
    if dep is not None:
        in_specs.append(_ANY)
        args.append(dep)
    return pl.pallas_call(
        body, name=name, grid=(M // tm, N // tn, nk),
        in_specs=in_specs, out_specs=pl.BlockSpec((tm, tn), lambda i, j, k: (i, j)),
        out_shape=jax.ShapeDtypeStruct((M, N), out_dtype),
        scratch_shapes=[pltpu.VMEM((tm, tn), F32)] if nk > 1 else [],
        compiler_params=_cp(("parallel", "parallel", "arbitrary")),
    )(*args)


def _mm_rms_bwd(name, a, b, acc_in, h, g, dh_res, *, tb=False):
    M, K = a.shape
    N = b.shape[0] if tb else b.shape[1]
    assert (b.shape[1] if tb else b.shape[0]) == K and h.shape == (M, N), (name, a.shape, b.shape)
    tm = _pick(M, (FFN_TOKEN_TILE, 256, 128))
    dn = (((1,), (1 if tb else 0,)), ((), ()))
    has_acc = acc_in is not None
    g_arr, g_row = _prow(g) if isinstance(g, tuple) else (_prow(g), None)

    def body(*refs):
        a_ref, b_ref = refs[:2]
        c_ref = refs[2] if has_acc else None
        h_ref, g_ref, r_ref, o_ref, dg_ref = refs[2 + has_acc:]
        d = lax.dot_general(a_ref[...].astype(BF), b_ref[...].astype(BF), dn, preferred_element_type=F32)
        if has_acc:
            d = c_ref[...] + d
        x = h_ref[...]
        r = lax.rsqrt(jnp.mean(x * x, axis=-1, keepdims=True) + EPS)
        xhat = x * r
        dxhat = d * g_ref[...]
        o_ref[...] = r_ref[...] + r * (dxhat - xhat * jnp.mean(dxhat * xhat, axis=-1, keepdims=True))

        @pl.when(pl.program_id(0) == 0)
        def _():
            dg_ref[...] = jnp.zeros_like(dg_ref)

        dg_ref[...] += jnp.sum(d * xhat, axis=0, keepdims=True)

    row = pl.BlockSpec((tm, N), lambda i: (i, 0))
    gspec = pl.BlockSpec((1, N), lambda i: (0, 0)) if g_row is None else pl.BlockSpec((None, 1, N), lambda i: (g_row, 0, 0))
    in_specs = [pl.BlockSpec((tm, K), lambda i: (i, 0)), pl.BlockSpec(b.shape, lambda i: (0, 0))]
    args = [a, b]
    if has_acc:
        in_specs.append(row)
        args.append(acc_in)
    return pl.pallas_call(
        body, name=name, grid=(M // tm,),
        in_specs=in_specs + [row, gspec, row], out_specs=[row, pl.BlockSpec((1, N), lambda i: (0, 0))],
        out_shape=[jax.ShapeDtypeStruct((M, N), F32), jax.ShapeDtypeStruct((1, N), F32)],
        compiler_params=_cp(("arbitrary",)),
    )(*args, h, g_arr, dh_res)


def _ew(name, fn, tiled, params, outs, accs=(), tile=512, dep=None):
    tiled = [t if isinstance(t, tuple) else (t, t.shape[1], 0) for t in tiled]
    params = [q if isinstance(q, tuple) else (q, None) for q in params]
    S = tiled[0][0].shape[0]
    T = _pick(S, (tile, 128, 64, 32, 16))
    n_in = len(tiled) + len(params)
    n_dep = 0 if dep is None else 1

    def body(*refs):
        fn(pl.program_id(0) == 0, *refs[:n_in], *refs[n_in + n_dep:])

    in_specs = [pl.BlockSpec((T, w), lambda i, cb=cb: (i, cb)) for _, w, cb in tiled]
    for q, row in params:
        if row is None:
            in_specs.append(pl.BlockSpec(q.shape, lambda i: (0, 0)))
        else:
            in_specs.append(pl.BlockSpec((None, 1, q.shape[2]), lambda i, row=row: (row, 0, 0)))
    args = [t[0] for t in tiled] + [q[0] for q in params]
    if dep is not None:
        in_specs.append(pl.BlockSpec(memory_space=pl.ANY))
        args.append(dep)
    out_specs = [pl.BlockSpec((T, w), lambda i: (i, 0)) for w, _ in outs]
    out_specs += [pl.BlockSpec(shp, lambda i: (0, 0)) for shp, _ in accs]
    out_shape = [jax.ShapeDtypeStruct((S, w), dt) for w, dt in outs]
    out_shape += [jax.ShapeDtypeStruct(shp, dt) for shp, dt in accs]
    res = pl.pallas_call(
        body, name=name, grid=(S // T,), in_specs=in_specs, out_specs=out_specs, out_shape=out_shape,
        compiler_params=_cp(("arbitrary",)),
    )(*args)
    return res


def _prow(g):
    return g if isinstance(g, tuple) else g.reshape(1, -1)


def _rms_fwd(name, h, g, dep=None):
    def fn(first, h_ref, g_ref, o_ref):
        x = h_ref[...]
        r = lax.rsqrt(jnp.mean(x * x, axis=-1, keepdims=True) + EPS)
        o_ref[...] = (x * r * g_ref[...]).astype(o_ref.dtype)

    return _ew(name, fn, [h], [_prow(g)], [(h.shape[1], BF)], dep=dep)[0]


FFN_TOKEN_TILE = 512


def _ffn_up(name, xn, wgT, wuT, dep=None):
    S, D = xn.shape
    FF = wgT.shape[0]
    tm = _pick(S, (FFN_TOKEN_TILE, 256, 128))
    tn = _pick(FF, MM_TILES)
    n_dep = 0 if dep is None else 1

    def body(x_ref, g_ref, u_ref, *rest):
        a_ref, b_ref, h_ref = rest[n_dep:]
        x = x_ref[...]
        a = _dot_nt(x, g_ref[...])
        b = _dot_nt(x, u_ref[...])
        a_ref[...] = a.astype(BF)
        b_ref[...] = b.astype(BF)
        h_ref[...] = (a * _sigmoid(a) * b).astype(BF)

    wspec = pl.BlockSpec((tn, D), lambda j, i: (j, 0))
    ospec = pl.BlockSpec((tm, tn), lambda j, i: (i, j))
    return pl.pallas_call(
        body, name=name, grid=(FF // tn, S // tm),
        in_specs=[pl.BlockSpec((tm, D), lambda j, i: (i, 0)), wspec, wspec] + ([] if dep is None else [_ANY]),
        out_specs=[ospec] * 3, out_shape=[jax.ShapeDtypeStruct((S, FF), BF)] * 3,
        compiler_params=_cp(("parallel", "arbitrary")),
    )(*([xn, wgT, wuT] + ([] if dep is None else [dep])))


def _ffn_dact(name, dh, wd, a, b, dep=None):
    S, D = dh.shape
    FF = wd.shape[0]
    tm = _pick(S, (FFN_TOKEN_TILE, 256, 128))
    tn = _pick(FF, MM_TILES)
    n_dep = 0 if dep is None else 1

    def body(d_ref, w_ref, a_ref, b_ref, *rest):
        da_ref, db_ref = rest[n_dep:]
        d = 0.5 * _dot_nt(d_ref[...].astype(BF), w_ref[...])
        av = a_ref[...].astype(F32)
        s = _sigmoid(av)
        da_ref[...] = (d * b_ref[...].astype(F32) * (s * (1.0 + av * (1.0 - s)))).astype(BF)
        db_ref[...] = (d * av * s).astype(BF)

    tspec = pl.BlockSpec((tm, tn), lambda j, i: (i, j))
    return pl.pallas_call(
        body, name=name, grid=(FF // tn, S // tm),
        in_specs=[pl.BlockSpec((tm, D), lambda j, i: (i, 0)), pl.BlockSpec((tn, D), lambda j, i: (j, 0)), tspec, tspec]
        + ([] if dep is None else [_ANY]),
        out_specs=[tspec] * 2, out_shape=[jax.ShapeDtypeStruct((S, FF), BF)] * 2,
        compiler_params=_cp(("parallel", "arbitrary")),
    )(*([dh, wd, a, b] + ([] if dep is None else [dep])))


def _merge_fwd(name, ga, gm, ya, ym):
    def fn(first, ga_ref, gm_ref, ya_ref, ym_ref, o_ref):
        o = _sigmoid(ga_ref[...].astype(F32)) * ya_ref[...].astype(F32) + _sigmoid(gm_ref[...].astype(F32)) * ym_ref[...].astype(F32)
        o_ref[...] = o.astype(o_ref.dtype)

    return _ew(name, fn, [ga, gm, ya, ym], [], [(ya.shape[1], BF)])[0]


def _merge_bwd(name, dmerged, ga, gm, ya, ym):
    W = ya.shape[1]

    def fn(first, d_ref, ga_ref, gm_ref, ya_ref, ym_ref, dga_ref, dgm_ref, dya_ref, dym_ref):
        d = d_ref[...].astype(F32)
        sa = _sigmoid(ga_ref[...].astype(F32))
        sm = _sigmoid(gm_ref[...].astype(F32))
        dga_ref[...] = (d * ya_ref[...].astype(F32) * sa * (1.0 - sa)).astype(BF)
        dgm_ref[...] = (d * ym_ref[...].astype(F32) * sm * (1.0 - sm)).astype(BF)
        dya_ref[...] = (d * sa).astype(BF)
        dym_ref[...] = (d * sm).astype(BF)

    return _ew(name, fn, [dmerged, ga, gm, ya, ym], [], [(W, BF)] * 4)


def _gnorm_fwd(name, y, z, w):
    W = y.shape[1]
    gw = W // SSM_GROUPS

    def fn(first, y_ref, z_ref, w_ref, o_ref):
        for g in range(SSM_GROUPS):
            sl = slice(g * gw, (g + 1) * gw)
            zz = z_ref[:, sl].astype(F32)
            t = y_ref[:, sl].astype(F32) * (zz * _sigmoid(zz))
            r = lax.rsqrt(jnp.mean(t * t, axis=-1, keepdims=True) + EPS)
            o_ref[:, sl] = (t * r * w_ref[:, sl]).astype(o_ref.dtype)

    return _ew(name, fn, [y, z], [_prow(w)], [(W, BF)])[0]


def _gnorm_bwd(name, dyn, y, z, w):
    W = y.shape[1]
    gw = W // SSM_GROUPS

    def fn(first, d_ref, y_ref, z_ref, w_ref, dy_ref, dz_ref, dw_ref):
        @pl.when(first)
        def _():
            dw_ref[...] = jnp.zeros_like(dw_ref)

        for g in range(SSM_GROUPS):
            sl = slice(g * gw, (g + 1) * gw)
            zz = z_ref[:, sl].astype(F32)
            yy = y_ref[:, sl].astype(F32)
            d = d_ref[:, sl].astype(F32)
            s = _sigmoid(zz)
            sz = zz * s
            t = yy * sz
            r = lax.rsqrt(jnp.mean(t * t, axis=-1, keepdims=True) + EPS)
            that = t * r
            dthat = d * w_ref[:, sl]
            dt = r * (dthat - that * jnp.mean(dthat * that, axis=-1, keepdims=True))
            dw_ref[:, sl] += jnp.sum(d * that, axis=0, keepdims=True)
            dy_ref[:, sl] = (dt * sz).astype(BF)
            dz_ref[:, sl] = (dt * yy * (s * (1.0 + zz * (1.0 - s)))).astype(BF)

    return _ew(name, fn, [dyn, y, z], [_prow(w)], [(W, BF), (W, BF)], [((1, W), F32)])


def _ple_fwd(name, h, gpre, pp):
    def fn(first, h_ref, g_ref, p_ref, o_ref):
        o_ref[...] = h_ref[...] + _sigmoid(g_ref[...].astype(F32)) * p_ref[...].astype(F32)

    return _ew(name, fn, [h, gpre, pp], [], [(h.shape[1], F32)])[0]


def _ple_bwd(name, dh, gpre, pp, dep=None):
    W = dh.shape[1]

    def fn(first, d_ref, g_ref, p_ref, dg_ref, dp_ref):
        d = d_ref[...]
        s = _sigmoid(g_ref[...].astype(F32))
        dg_ref[...] = (d * p_ref[...].astype(F32) * s * (1.0 - s)).astype(BF)
        dp_ref[...] = (d * s).astype(BF)

    return _ew(name, fn, [dh, gpre, pp], [], [(W, BF), (W, BF)], dep=dep)


def _loss_head(name, h, g, target):
    D = h.shape[1]

    def fn(first, h_ref, t_ref, g_ref, dh_ref, loss_ref, dg_ref):
        x = h_ref[...]
        r = lax.rsqrt(jnp.mean(x * x, axis=-1, keepdims=True) + EPS)
        xhat = x * r
        err = xhat * g_ref[...] - t_ref[...]
        part = 0.5 * jnp.sum(jnp.mean(err * err, axis=-1, keepdims=True), axis=0, keepdims=True)
        dy = err * (1.0 / D)
        dxhat = dy * g_ref[...]
        dh_ref[...] = r * (dxhat - xhat * jnp.mean(dxhat * xhat, axis=-1, keepdims=True))

        @pl.when(first)
        def _():
            loss_ref[...] = jnp.zeros_like(loss_ref)
            dg_ref[...] = jnp.zeros_like(dg_ref)

        loss_ref[...] += jnp.broadcast_to(part, loss_ref.shape)
        dg_ref[...] += jnp.sum(dy * xhat, axis=0, keepdims=True)

    return _ew(name, fn, [h, target], [_prow(g)], [(D, F32)], [((1, LANES), F32), ((1, D), F32)])


def _conv_specs(S, C, offs, l):
    T = _pick(S, (512, 256, 128, 64, 32, 16))
    Ct = [c for c in (512, 256, 128) if C % c == 0 and all(o % c == 0 for o in offs)][0]
    per = T // HALO
    last = S // HALO - 1

    def cur(off=0):
        return pl.BlockSpec((T, Ct), lambda j, i: (i, off // Ct + j))

    def prev(off=0):
        return pl.BlockSpec((HALO, Ct), lambda j, i: (jnp.maximum(i * per - 1, 0), off // Ct + j))

    def nxt(off=0):
        return pl.BlockSpec((HALO, Ct), lambda j, i: (jnp.minimum((i + 1) * per, last), off // Ct + j))

    wspec = pl.BlockSpec((None, 8, Ct), lambda j, i: (l, 0, j))
    return T, Ct, cur, prev, nxt, wspec


def _pad_taps(w):
    return jnp.concatenate([w.astype(F32), jnp.zeros((w.shape[0], 8 - w.shape[1], w.shape[2]), F32)], axis=1)


def _causal(cat, w_ref, K, T, lead):
    views = [(pltpu.roll(cat, K - 1 - k, 0) if k < K - 1 else cat)[lead:lead + T] for k in range(K)]
    out = None
    for k in range(K):
        term = w_ref[k:k + 1, :] * views[k]
        out = term if out is None else out + term
    return out, views


def _anticausal(cat, w_ref, K, T):
    out = None
    rows = cat.shape[0]
    for k in range(K):
        o = K - 1 - k
        term = w_ref[k:k + 1, :] * (pltpu.roll(cat, rows - o, 0) if o else cat)[:T]
        out = term if out is None else out + term
    return out


def _scconv_fwd(name, proj, ob, oc, ox, taps, K, l):
    S = proj.shape[0]
    C = taps.shape[2]
    T, Ct, cur, prev, nxt, wspec = _conv_specs(S, C, (ob, oc, ox), l)

    def body(b_ref, c_ref, x_ref, cp_ref, xp_ref, w_ref, o_ref):
        i = pl.program_id(1)
        q = c_ref[...].astype(F32) * x_ref[...].astype(F32)
        qp = jnp.where(i == 0, 0.0, cp_ref[...].astype(F32) * xp_ref[...].astype(F32))
        cat = jnp.concatenate([qp, q], axis=0)
        o_ref[...] = (b_ref[...].astype(F32) * _causal(cat, w_ref, K, T, HALO)[0]).astype(o_ref.dtype)

    return pl.pallas_call(
        body, name=name, grid=(C // Ct, S // T),
        in_specs=[cur(ob), cur(oc), cur(ox), prev(oc), prev(ox), wspec], out_specs=cur(),
        out_shape=jax.ShapeDtypeStruct((S, C), BF), compiler_params=_cp(("parallel", "arbitrary")),
    )(proj, proj, proj, proj, proj, taps)


def _scconv_bwd(name, dv, proj, ob, oc, ox, taps, K, l):
    S = proj.shape[0]
    C = taps.shape[2]
    T, Ct, cur, prev, nxt, wspec = _conv_specs(S, C, (ob, oc, ox), l)
    n_t = S // T

    def body(d_ref, b_ref, c_ref, x_ref, dn_ref, bn_ref, cp_ref, xp_ref, w_ref, db_ref, dc_ref, dx_ref, dw_ref):
        i = pl.program_id(1)
        c = c_ref[...].astype(F32)
        x = x_ref[...].astype(F32)
        d = d_ref[...].astype(F32)
        q = c * x
        qp = jnp.where(i == 0, 0.0, cp_ref[...].astype(F32) * xp_ref[...].astype(F32))
        catq = jnp.concatenate([qp, q], axis=0)
        cv, q_views = _causal(catq, w_ref, K, T, HALO)
        db_ref[...] = (d * cv).astype(BF)
        dcv = d * b_ref[...].astype(F32)
        dcvn = jnp.where(i == n_t - 1, 0.0, dn_ref[...].astype(F32) * bn_ref[...].astype(F32))
        catd = jnp.concatenate([dcv, dcvn], axis=0)
        dq = _anticausal(catd, w_ref, K, T)
        dc_ref[...] = (dq * x).astype(BF)
        dx_ref[...] = (dq * c).astype(BF)

        @pl.when(i == 0)
        def _():
            dw_ref[...] = jnp.zeros_like(dw_ref)

        for k in range(K):
            dw_ref[k:k + 1, :] += jnp.sum(dcv * q_views[k], axis=0, keepdims=True)

    return pl.pallas_call(
        body, name=name, grid=(C // Ct, n_t),
        in_specs=[cur(), cur(ob), cur(oc), cur(ox), nxt(), nxt(ob), prev(oc), prev(ox), wspec],
        out_specs=[cur(), cur(), cur(), pl.BlockSpec((8, Ct), lambda j, i: (0, j))],
        out_shape=[jax.ShapeDtypeStruct((S, C), BF)] * 3 + [jax.ShapeDtypeStruct((8, C), F32)],
        compiler_params=_cp(("parallel", "arbitrary")),
    )(dv, proj, proj, proj, dv, proj, proj, proj, taps)


def _mconv_fwd(name, proj, ox, taps, K, bias, l):
    S = proj.shape[0]
    C = taps.shape[2]
    T, Ct, cur, prev, nxt, wspec = _conv_specs(S, C, (ox,), l)
    bspec = pl.BlockSpec((None, 1, Ct), lambda j, i: (l, 0, j))

    def body(x_ref, xp_ref, w_ref, b_ref, o_ref):
        i = pl.program_id(1)
        xp = jnp.where(i == 0, 0.0, xp_ref[...].astype(F32))
        cat = jnp.concatenate([xp, x_ref[...].astype(F32)], axis=0)
        pre = _causal(cat, w_ref, K, T, HALO)[0] + b_ref[...]
        o_ref[...] = (pre * _sigmoid(pre)).astype(o_ref.dtype)

    return pl.pallas_call(
        body, name=name, grid=(C // Ct, S // T),
        in_specs=[cur(ox), prev(ox), wspec, bspec], out_specs=cur(),
        out_shape=jax.ShapeDtypeStruct((S, C), BF), compiler_params=_cp(("parallel", "arbitrary")),
    )(proj, proj, taps, bias)


def _mconv_bwd(name, dout, proj, ox, taps, K, bias, l):
    S = proj.shape[0]
    C = taps.shape[2]
    T, Ct, cur, prev, nxt, wspec = _conv_specs(S, C, (ox,), l)
    n_t = S // T
    bspec = pl.BlockSpec((None, 1, Ct), lambda j, i: (l, 0, j))

    def body(d_ref, dn_ref, x_ref, xp_ref, xn_ref, w_ref, b_ref, dx_ref, dw_ref, db_ref):
        i = pl.program_id(1)
        xp = jnp.where(i == 0, 0.0, xp_ref[...].astype(F32))
        cat3 = jnp.concatenate([xp, x_ref[...].astype(F32), xn_ref[...].astype(F32)], axis=0)
        pre, x_views = _causal(cat3, w_ref, K, T + HALO, HALO)
        pre = pre + b_ref[...]
        dn = jnp.where(i == n_t - 1, 0.0, dn_ref[...].astype(F32))
        dext = jnp.concatenate([d_ref[...].astype(F32), dn], axis=0)
        s = _sigmoid(pre)
        dpre = dext * (s * (1.0 + pre * (1.0 - s)))
        dx_ref[...] = _anticausal(dpre, w_ref, K, T).astype(BF)
        dcur = dpre[:T]

        @pl.when(i == 0)
        def _():
            dw_ref[...] = jnp.zeros_like(dw_ref)
            db_ref[...] = jnp.zeros_like(db_ref)

        db_ref[...] += jnp.sum(dcur, axis=0, keepdims=True)
        for k in range(K):
            dw_ref[k:k + 1, :] += jnp.sum(dcur * x_views[k][:T], axis=0, keepdims=True)

    return pl.pallas_call(
        body, name=name, grid=(C // Ct, n_t),
        in_specs=[cur(), nxt(), cur(ox), prev(ox), nxt(ox), wspec, bspec],
        out_specs=[cur(), pl.BlockSpec((8, Ct), lambda j, i: (0, j)), pl.BlockSpec((1, Ct), lambda j, i: (0, j))],
        out_shape=[jax.ShapeDtypeStruct((S, C), BF), jax.ShapeDtypeStruct((8, C), F32), jax.ShapeDtypeStruct((1, C), F32)],
        compiler_params=_cp(("parallel", "arbitrary")),
    )(dout, dout, proj, proj, proj, taps, bias)


def _tri_matmul(tri_bf, v):
    hi = v.astype(BF)
    r1 = v - hi.astype(F32)
    mid = r1.astype(BF)
    lo = (r1 - mid.astype(F32)).astype(BF)
    dot = functools.partial(jnp.dot, preferred_element_type=F32)
    return dot(tri_bf, hi) + dot(tri_bf, mid) + dot(tri_bf, lo)


def _dot_nt(a, b):
    return lax.dot_general(a, b, (((1,), (1,)), ((), ())), preferred_element_type=F32)


def _dot_tn(a, b):
    return lax.dot_general(a, b, (((0,), (0,)), ((), ())), preferred_element_type=F32)


def _dot_nn(a, b):
    return jnp.dot(a, b, preferred_element_type=F32)


def _ssd_chunk_scalars(dtr_ref, par_ref, L):
    row_i = lax.broadcasted_iota(jnp.int32, (L, L), 0)
    col_i = lax.broadcasted_iota(jnp.int32, (L, L), 1)
    tri = row_i >= col_i
    pre = dtr_ref[...] + par_ref[0:1, :]
    dt_all = _softplus(pre)
    A_row = -jnp.exp(par_ref[1:2, :])
    a_all = dt_all * A_row
    acum_all = _tri_matmul(tri.astype(BF), a_all)
    return tri, pre, dt_all, A_row, a_all, acum_all, acum_all.T


def _ssd_dims(xbc, heads):
    S, conv_dim = xbc.shape
    inner = heads * SSM_HEADDIM
    N = (conv_dim - inner) // (2 * SSM_GROUPS)
    gw = inner // SSM_GROUPS
    PP = gw // LANES
    L = min(SSM_CHUNK, S)
    assert N == LANES and gw % LANES == 0 and inner % (SSM_GROUPS * N) == 0 and S % L == 0
    return S, inner, N, gw, PP, L, S // L


def _ssd_params(dt_bias, A_log, Dp):
    depth, H = dt_bias.shape
    rows = jnp.stack([dt_bias, A_log, Dp], axis=1).astype(F32)
    rows = jnp.concatenate([rows, jnp.zeros((depth, 3, LANES - H), F32)], axis=2)
    return jnp.concatenate([rows, jnp.zeros((depth, 5, LANES), F32)], axis=1)


def _ssd_fwd(name, xbc, dt_raw, par, l, heads):
    S, inner, N, gw, PP, L, nc = _ssd_dims(xbc, heads)
    G = SSM_GROUPS

    def body(x_ref, b_ref, c_ref, dtr_ref, par_ref, y_ref, st_out_ref, sc_ref, at_ref, st_ref):
        @pl.when(pl.program_id(0) == 0)
        def _():
            st_ref[...] = jnp.zeros_like(st_ref)

        tri, pre, dt_all, A_row, a_all, acum_all, acumT = _ssd_chunk_scalars(dtr_ref, par_ref, L)
        sc_ref[:, 0:LANES] = dt_all
        sc_ref[:, LANES:2 * LANES] = a_all
        sc_ref[:, 2 * LANES:3 * LANES] = acum_all
        sc_ref[:, 3 * LANES:] = pre
        at_ref[0] = acumT
        lane = lax.broadcasted_iota(jnp.int32, (L, LANES), 1)
        lane1 = lax.broadcasted_iota(jnp.int32, (1, LANES), 1)
        lo = lane < SSM_HEADDIM
        lo1 = lane1 < SSM_HEADDIM
        for g in range(G):
            Bb = b_ref[:, g * N:(g + 1) * N]
            Cb = c_ref[:, g * N:(g + 1) * N]
            BbT = Bb.astype(F32).T.astype(BF)
            Gm = _dot_nt(Cb, Bb)
            for j in range(PP):
                pj = g * PP + j
                h0, h1 = 2 * pj, 2 * pj + 1
                cols = slice(pj * LANES, (pj + 1) * LANES)
                x = x_ref[:, cols].astype(F32)
                dt_l = jnp.where(lo, dt_all[:, h0:h0 + 1], dt_all[:, h1:h1 + 1])
                ac0 = acum_all[:, h0:h0 + 1]
                ac1 = acum_all[:, h1:h1 + 1]
                ac_l = jnp.where(lo, ac0, ac1)
                E0 = jnp.exp(jnp.where(tri, ac0 - acumT[h0:h0 + 1, :], -1e30))
                E1 = jnp.exp(jnp.where(tri, ac1 - acumT[h1:h1 + 1, :], -1e30))
                xd = x * dt_l
                xdb = xd.astype(BF)
                yd = jnp.where(lo, _dot_nn((Gm * E0).astype(BF), xdb), _dot_nn((Gm * E1).astype(BF), xdb))
                prevT = st_ref[pj]
                st_out_ref[0, pj] = prevT
                P = _dot_nn(Cb, prevT.astype(BF))
                D_l = jnp.where(lo1, par_ref[2:3, h0:h0 + 1], par_ref[2:3, h1:h1 + 1])
                y_ref[:, cols] = (yd + P * jnp.exp(ac_l) + D_l * x).astype(y_ref.dtype)
                al0 = ac0[L - 1:L, :]
                al1 = ac1[L - 1:L, :]
                Wm = xd * jnp.exp(jnp.where(lo, al0, al1) - ac_l)
                eal = jnp.where(lo1, jnp.exp(al0), jnp.exp(al1))
                st_ref[pj] = eal * prevT + _dot_nn(BbT, Wm.astype(BF))

    gn = G * N
    return pl.pallas_call(
        body, name=name, grid=(nc,),
        in_specs=[pl.BlockSpec((L, inner), lambda c: (c, 0)), pl.BlockSpec((L, gn), lambda c: (c, inner // gn)),
                  pl.BlockSpec((L, gn), lambda c: (c, inner // gn + 1)),
                  pl.BlockSpec((L, LANES), lambda c: (c, 0)), pl.BlockSpec((None, 8, LANES), lambda c: (l, 0, 0))],
        out_specs=[pl.BlockSpec((L, inner), lambda c: (c, 0)), pl.BlockSpec((1, G * PP, N, LANES), lambda c: (c, 0, 0, 0)),
                   pl.BlockSpec((L, 4 * LANES), lambda c: (c, 0)), pl.BlockSpec((1, LANES, L), lambda c: (c, 0, 0))],
        out_shape=[jax.ShapeDtypeStruct((S, inner), BF), jax.ShapeDtypeStruct((nc, G * PP, N, LANES), F32),
                   jax.ShapeDtypeStruct((S, 4 * LANES), F32), jax.ShapeDtypeStruct((nc, LANES, L), F32)],
        scratch_shapes=[pltpu.VMEM((G * PP, N, LANES), F32)],
        compiler_params=_cp(("arbitrary",)),
    )(xbc, xbc, xbc, dt_raw, par)


def _ssd_bwd(name, dy, xbc, scal, acum_t, states, par, l, heads):
    S, inner, N, gw, PP, L, nc = _ssd_dims(xbc, heads)
    G = SSM_GROUPS

    def body(dy_ref, x_ref, b_ref, c_ref, sc_ref, at_ref, par_ref, st_in_ref, d_ref, ddt_ref, dpar_ref, dst_ref):
        @pl.when(pl.program_id(0) == 0)
        def _():
            dst_ref[...] = jnp.zeros_like(dst_ref)
            dpar_ref[...] = jnp.zeros_like(dpar_ref)

        tri = lax.broadcasted_iota(jnp.int32, (L, L), 0) >= lax.broadcasted_iota(jnp.int32, (L, L), 1)
        dt_all = sc_ref[:, 0:LANES]
        a_all = sc_ref[:, LANES:2 * LANES]
        acum_all = sc_ref[:, 2 * LANES:3 * LANES]
        pre = sc_ref[:, 3 * LANES:]
        acumT = at_ref[0]
        A_row = -jnp.exp(par_ref[1:2, :])
        lane = lax.broadcasted_iota(jnp.int32, (L, LANES), 1)
        lane1 = lax.broadcasted_iota(jnp.int32, (1, LANES), 1)
        rowl = lax.broadcasted_iota(jnp.int32, (L, LANES), 0)
        lo = lane < SSM_HEADDIM
        lo1 = lane1 < SSM_HEADDIM
        triT = lax.broadcasted_iota(jnp.int32, (L, L), 0) <= lax.broadcasted_iota(jnp.int32, (L, L), 1)
        sel_r = lax.broadcasted_iota(jnp.int32, (3 * LANES, LANES), 0)
        sel_c = lax.broadcasted_iota(jnp.int32, (3 * LANES, LANES), 1)
        dac_all = jnp.zeros((L, LANES), F32)
        xds_all = jnp.zeros((L, LANES), F32)
        dD_row = jnp.zeros((1, LANES), F32)

        def half_sums(v):
            return (jnp.sum(jnp.where(lo1, v, 0.0), axis=1, keepdims=True), jnp.sum(jnp.where(lo1, 0.0, v), axis=1, keepdims=True))

        def dot2(v, sel):
            hi = v.astype(BF)
            return _dot_nn(hi, sel) + _dot_nn((v - hi.astype(F32)).astype(BF), sel)

        for pj in range(G * PP):
            g, j = divmod(pj, PP)
            if j == 0:
                Bb = b_ref[:, g * N:(g + 1) * N]
                Cb = c_ref[:, g * N:(g + 1) * N]
                CbT = Cb.astype(F32).T.astype(BF)
                Gm = _dot_nt(Cb, Bb)
                GmT = _dot_nt(Bb, Cb)
                dG = jnp.zeros((L, L), F32)
                dGT = jnp.zeros((L, L), F32)
                dBacc = jnp.zeros((L, N), F32)
                dCacc = jnp.zeros((L, N), F32)
            h0, h1 = 2 * pj, 2 * pj + 1
            to_h0 = (sel_r < LANES) | ((sel_r >= 2 * LANES) & (sel_r < 2 * LANES + SSM_HEADDIM))
            sel3 = jnp.where(sel_c == jnp.where(to_h0, h0, h1), 1.0, 0.0).astype(BF)
            sel1 = sel3[2 * LANES:]
            sl = slice(pj * LANES, (pj + 1) * LANES)
            x = x_ref[:, sl].astype(F32)
            dyv = dy_ref[:, sl].astype(F32)
            dt_l = jnp.where(lo, dt_all[:, h0:h0 + 1], dt_all[:, h1:h1 + 1])
            ac0 = acum_all[:, h0:h0 + 1]
            ac1 = acum_all[:, h1:h1 + 1]
            r0 = acumT[h0:h0 + 1, :]
            r1 = acumT[h1:h1 + 1, :]
            ac_l = jnp.where(lo, ac0, ac1)
            E0 = jnp.exp(jnp.where(tri, ac0 - r0, -1e30))
            E1 = jnp.exp(jnp.where(tri, ac1 - r1, -1e30))
            E0T = jnp.exp(jnp.where(triT, r0 - ac0, -1e30))
            E1T = jnp.exp(jnp.where(triT, r1 - ac1, -1e30))
            xd = x * dt_l
            xdb = xd.astype(BF)
            M0 = Gm * E0
            M1 = Gm * E1
            ea_l = jnp.exp(ac_l)
            al0 = ac0[L - 1:L, :]
            al1 = ac1[L - 1:L, :]
            dte_l = jnp.exp(jnp.where(lo, al0, al1) - ac_l)
            Wm = xd * dte_l
            prevT = st_in_ref[0, pj]
            prevTb = prevT.astype(BF)
            P = _dot_nn(Cb, prevTb)
            D_l = jnp.where(lo1, par_ref[2:3, h0:h0 + 1], par_ref[2:3, h1:h1 + 1])
            dx = D_l * dyv
            dD0, dD1 = half_sums(jnp.sum(dyv * x, axis=0, keepdims=True))
            dyb = dyv.astype(BF)
            dy0b = jnp.where(lo, dyv, 0.0).astype(BF)
            dy1b = jnp.where(lo, 0.0, dyv).astype(BF)
            dM0 = _dot_nt(dy0b, xdb)
            dM1 = _dot_nt(dy1b, xdb)
            dM0T = _dot_nt(xdb, dy0b)
            dM1T = _dot_nt(xdb, dy1b)
            M0T = GmT * E0T
            M1T = GmT * E1T
            dxd = jnp.where(lo, _dot_nn(M0T.astype(BF), dyb), _dot_nn(M1T.astype(BF), dyb))
            dG = dG + dM0 * E0 + dM1 * E1
            dGT = dGT + dM0T * E0T + dM1T * E1T
            z0 = dM0 * M0 - dM0T * M0T
            z1 = dM1 * M1 - dM1T * M1T
            dP = dyv * ea_l
            dPb = dP.astype(BF)
            dCacc = dCacc + _dot_nt(dPb, prevTb)
            dprevT = _dot_nn(CbT, dPb)
            dnewT = dst_ref[pj]
            dnewTb = dnewT.astype(BF)
            e0 = jnp.exp(al0)
            e1 = jnp.exp(al1)
            dprevT = dprevT + jnp.where(lo1, e0, e1) * dnewT
            u0, u1 = half_sums(jnp.sum(dnewT * prevT, axis=0, keepdims=True))
            dW = _dot_nn(Bb, dnewTb)
            dBacc = dBacc + _dot_nt(Wm.astype(BF), dnewTb)
            dxd = dxd + dW * dte_l
            tt = dW * Wm
            t0, t1 = half_sums(jnp.sum(tt, axis=0, keepdims=True))
            dal0 = u0 * e0 + t0
            dal1 = u1 * e1 + t1
            dac_all = dac_all + dot2(jnp.concatenate([z0, z1, dP * P - tt], axis=1), sel3)
            dac_all = dac_all + jnp.where(rowl == L - 1, jnp.where(lane == h0, dal0, 0.0) + jnp.where(lane == h1, dal1, 0.0), 0.0)
            dx = dx + dxd * dt_l
            xds_all = xds_all + dot2(dxd * x, sel1)
            dst_ref[pj] = dprevT
            d_ref[:, sl] = dx.astype(d_ref.dtype)
            dD_row = dD_row + jnp.where(lane1 == h0, dD0, 0.0) + jnp.where(lane1 == h1, dD1, 0.0)
            if j == PP - 1:
                d_ref[:, inner + g * N:inner + (g + 1) * N] = (dBacc + _dot_nn(dGT.astype(BF), Cb)).astype(d_ref.dtype)
                d_ref[:, inner + (G + g) * N:inner + (G + g + 1) * N] = (dCacc + _dot_nn(dG.astype(BF), Bb)).astype(d_ref.dtype)

        row_i = lax.broadcasted_iota(jnp.int32, (L, L), 0)
        col_i = lax.broadcasted_iota(jnp.int32, (L, L), 1)
        da_all = _tri_matmul((row_i <= col_i).astype(BF), dac_all)
        real = lane < heads
        ddt_all = da_all * A_row + xds_all
        draw = jnp.where(real, ddt_all * _sigmoid(pre), 0.0)
        ddt_ref[...] = draw
        dpar_ref[0:1, :] += jnp.sum(draw, axis=0, keepdims=True)
        dpar_ref[1:2, :] += jnp.sum(jnp.where(real, da_all * a_all, 0.0), axis=0, keepdims=True)
        dpar_ref[2:3, :] += dD_row

    gn = G * N
    conv_dim = xbc.shape[1]
    rev = lambda c: nc - 1 - c
    return pl.pallas_call(
        body, name=name, grid=(nc,),
        in_specs=[pl.BlockSpec((L, inner), lambda c: (rev(c), 0)), pl.BlockSpec((L, inner), lambda c: (rev(c), 0)),
                  pl.BlockSpec((L, gn), lambda c: (rev(c), inner // gn)), pl.BlockSpec((L, gn), lambda c: (rev(c), inner // gn + 1)),
                  pl.BlockSpec((L, 4 * LANES), lambda c: (rev(c), 0)), pl.BlockSpec((1, LANES, L), lambda c: (rev(c), 0, 0)),
                  pl.BlockSpec((None, 8, LANES), lambda c: (l, 0, 0)),
                  pl.BlockSpec((1, G * PP, N, LANES), lambda c: (rev(c), 0, 0, 0))],
        out_specs=[pl.BlockSpec((L, conv_dim), lambda c: (rev(c), 0)), pl.BlockSpec((L, LANES), lambda c: (rev(c), 0)),
                   pl.BlockSpec((8, LANES), lambda c: (0, 0))],
        out_shape=[jax.ShapeDtypeStruct((S, conv_dim), BF), jax.ShapeDtypeStruct((S, LANES), F32), jax.ShapeDtypeStruct((8, LANES), F32)],
        scratch_shapes=[pltpu.VMEM((G * PP, N, LANES), F32)],
        compiler_params=_cp(("arbitrary",)),
    )(dy, xbc, xbc, xbc, scal, acum_t, par, states)


def _adamw(g, w, m, v):
    m2 = ADAM_B1 * m + (1.0 - ADAM_B1) * g
    v2 = ADAM_B2 * v + (1.0 - ADAM_B2) * (g * g)
    m_hat = m2 / (1.0 - ADAM_B1 ** ADAM_STEP)
    v_hat = v2 / (1.0 - ADAM_B2 ** ADAM_STEP)
    delta = -ADAM_LR * (m_hat / (jnp.sqrt(v_hat) + ADAM_EPS) + ADAM_WD * w)
    return delta, m2, v2


def _flat_tile(R):
    return _pick(R, (FLAT_ROW_TILE, 1024, 512, 256, 128, 64, 32, 16, 8))


def _sum_adam(name, lands, off, w, m, v):
    depth, r, c = w.shape
    cap = max(16, (4 * 1024 * 1024) // (N_DEV * c * 2))
    row_tiles = [t for t in (512, 256, 128, 64, 32, 16) if r % t == 0 and off % t == 0 and t <= cap]
    if row_tiles:
        tr, tc = row_tiles[0], c
        ob = off // tr
        n_t = r // tr
        spec = pl.BlockSpec((None, tr, c), lambda l, t: (l, t, 0))
        land_specs = [pl.BlockSpec((N_DEV, tr, c), lambda l, t, i=i: (0, jnp.where(l == i, ob + t, ob), 0)) for i in range(depth)]
    else:
        assert off == 0 and lands[0].shape[1] == r and c % LANES == 0
        tc = LANES
        n_t = c // tc
        spec = pl.BlockSpec((None, r, tc), lambda l, t: (l, 0, t))
        land_specs = [pl.BlockSpec((N_DEV, r, tc), lambda l, t, i=i: (0, 0, jnp.where(l == i, t, 0))) for i in range(depth)]

    def body(*refs):
        land_refs = refs[:depth]
        w_ref, m_ref, v_ref, g_ref, d_ref, m2_ref, v2_ref = refs[depth:]
        l = pl.program_id(0)
        for i in range(depth):
            @pl.when(l == i)
            def _(i=i):
                g = land_refs[i][0].astype(F32)
                for k in range(1, N_DEV):
                    g = g + land_refs[i][k].astype(F32)
                g_ref[...] = g
                d_ref[...], m2_ref[...], v2_ref[...] = _adamw(g, w_ref[...], m_ref[...], v_ref[...])

    return pl.pallas_call(
        body, name=name, grid=(depth, n_t),
        in_specs=land_specs + [spec, spec, spec],
        out_specs=[spec] * 4, out_shape=[jax.ShapeDtypeStruct((depth, r, c), F32)] * 4,
        compiler_params=_cp(("arbitrary", "arbitrary")),
    )(*lands, w, m, v)


def _sum8(name, parts):
    R = parts.shape[1]
    TR = _flat_tile(R)

    def body(p_ref, g_ref):
        g = p_ref[0]
        for k in range(1, N_DEV):
            g = g + p_ref[k]
        g_ref[...] = g

    return pl.pallas_call(
        body, name=name, grid=(R // TR,),
        in_specs=[pl.BlockSpec((N_DEV, TR, LANES), lambda i: (0, i, 0))],
        out_specs=pl.BlockSpec((TR, LANES), lambda i: (i, 0)), out_shape=jax.ShapeDtypeStruct((R, LANES), F32),
        compiler_params=_cp(("parallel",)),
    )(parts)


def _adam_flat(name, g, w, m, v):
    R = w.shape[0]
    TR = _flat_tile(R)

    def body(g_ref, w_ref, m_ref, v_ref, d_ref, m2_ref, v2_ref):
        d_ref[...], m2_ref[...], v2_ref[...] = _adamw(g_ref[...], w_ref[...], m_ref[...], v_ref[...])

    spec = pl.BlockSpec((TR, LANES), lambda i: (i, 0))
    return pl.pallas_call(
        body, name=name, grid=(R // TR,), in_specs=[spec] * 4, out_specs=[spec] * 3,
        out_shape=[jax.ShapeDtypeStruct((R, LANES), F32)] * 3, compiler_params=_cp(("parallel",)),
    )(g, w, m, v)


PART_ROWS = 16


def _nrows(shape):
    n = 1
    for s in shape:
        n *= s
    r = -(-n // LANES)
    return -(-r // PART_ROWS) * PART_ROWS


def _as_rows(a):
    n = a.size
    r = _nrows(a.shape)
    f = a.reshape(-1)
    if r * LANES != n:
        f = jnp.concatenate([f, jnp.zeros((r * LANES - n,), a.dtype)])
    return f.reshape(r, LANES)


def _pack(arrs, mult=PART_ROWS):
    cat = jnp.concatenate([_as_rows(a) for a in arrs], axis=0)
    pad = (-cat.shape[0]) % mult
    if pad:
        cat = jnp.concatenate([cat, jnp.zeros((pad, LANES), cat.dtype)], axis=0)
    return cat


def _unpack(flat, shapes):
    lead = flat.shape[:-2]
    out = []
    o = 0
    for shp in shapes:
        n = 1
        for s in shp:
            n *= s
        r = _nrows(shp)
        blk = flat[..., o:o + r, :].reshape(lead + (r * LANES,))
        out.append(blk[..., :n].reshape(lead + tuple(shp)))
        o += r
    return out


def _full_from_shards(st):
    return st.reshape(st.shape[0] * st.shape[1], st.shape[2])


def _shards_from_full(full):
    return full.reshape(N_DEV, full.shape[0] // N_DEV, full.shape[1])


def _ffn_fwd(tag, h, g, wgT, wuT, wd, dep=None):
    xn = _rms_fwd(tag + "_rms", h, g, dep=dep)
    a, b, hmid = _ffn_up(tag + "_up", xn, wgT, wuT)
    hout = _mm(tag + "_down", hmid, wd, out_dtype=F32, res=h, alpha=0.5)
    return hout, (xn, a, b, hmid)


def _ffn_bwd(tag, dh_out, h, g, wgT, wuT, wd, saved, dep=None):
    xn, a, b, hmid = saved
    da, db = _ffn_dact(tag + "_d_act", dh_out, wd, a, b, dep=dep)
    d_wd = _mm(tag + "_d_wd", hmid, dh_out, ta=True, alpha=0.5)
    d_wgT = _mm(tag + "_d_wg", da, xn, ta=True)
    d_wuT = _mm(tag + "_d_wu", db, xn, ta=True)
    dxn = _mm(tag + "_d_xn_g", da, wgT, out_dtype=F32)
    dh, dg = _mm_rms_bwd(tag + "_d_xn_u", db, wuT, dxn, h, g, dh_out)
    return dh, dg, d_wgT, d_wuT, d_wd


SEG_NAMES = ['scb', 'scc', 'scx', 'z', 'xbc', 'dt', 'ga', 'gm']
PERM = ['z', 'scb', 'scc', 'scx', 'ga', 'gm', 'xbc']


def _seg_layout(dims):
    D, inner, conv_dim, H = dims[:4]
    widths = dict(zip(SEG_NAMES, [D, D, D, inner, conv_dim, H, D, D]))
    offs, o = {}, 0
    for n in SEG_NAMES:
        offs[n] = (o, widths[n])
        o += widths[n]
    poffs, o = {}, 0
    for n in PERM:
        poffs[n] = (o, widths[n])
        o += widths[n]
    return offs, poffs


def _perm_w_in(w_inT, dims):
    offs, _ = _seg_layout(dims)
    wp = jnp.concatenate([w_inT[offs[n][0]:offs[n][0] + offs[n][1]] for n in PERM], axis=0)
    o, w = offs['dt']
    wdt = jnp.concatenate([w_inT[o:o + w], jnp.zeros((LANES - w, w_inT.shape[1]), w_inT.dtype)], axis=0)
    return wp, wdt


def _unperm_d_w_in(d_wp, d_wdt, dims):
    offs, poffs = _seg_layout(dims)
    H = dims[3]
    return jnp.concatenate([d_wdt[:H] if n == 'dt' else d_wp[poffs[n][0]:poffs[n][0] + poffs[n][1]] for n in SEG_NAMES], axis=0)


def _mixer_fwd(h, W, dims, dep=None):
    H, Ksc, Km = dims[3:]
    l = W['l']
    _, poffs = _seg_layout(dims)

    def seg(n):
        o, w = poffs[n]
        assert o % w == 0
        return (proj, w, o // w)

    u = _rms_fwd("mix_rms", h, W['mix_norm'], dep=dep)
    proj = _mm("inproj", u, W['w_in_p'], tb=True)
    dt_raw = _mm("inproj_dt", u, W['w_dt'], tb=True, out_dtype=F32)
    v = _scconv_fwd("scconv_f", proj, poffs['scb'][0], poffs['scc'][0], poffs['scx'][0], W['sc_taps'], Ksc, l)
    ya = _mm("sc_out", v, W['sc_w_out'])
    xbc = _mconv_fwd("mconv_f", proj, poffs['xbc'][0], W['m_taps'], Km, W['m_conv_b'], l)
    y, states, scal, acum_t = _ssd_fwd("ssd_f", xbc, dt_raw, W['ssd_par'], l, H)
    yn = _gnorm_fwd("gnorm_f", y, seg('z'), W['m_norm'])
    ym = _mm("m_out", yn, W['m_w_out'])
    merged = _merge_fwd("merge_f", seg('ga'), seg('gm'), ya, ym)
    hout = _mm("w_o", merged, W['w_o'], out_dtype=F32, res=h)
    return hout, (u, proj, (scal, acum_t), v, ya, xbc, y, states, yn, ym, merged)


def _mixer_bwd(dh_out, h, W, dims, saved, dep=None):
    u, proj, (scal, acum_t), v, ya, xbc, y, states, yn, ym, merged = saved
    H, Ksc, Km = dims[3:]
    l = W['l']
    _, poffs = _seg_layout(dims)

    def seg(n):
        o, w = poffs[n]
        return (proj, w, o // w)

    g = {}
    dmerged = _mm("d_merged", dh_out, W['w_o'], tb=True, dep=dep)
    g['w_o'] = _mm("d_w_o", merged, dh_out, ta=True)
    dga, dgm, dya, dym = _merge_bwd("merge_b", dmerged, seg('ga'), seg('gm'), ya, ym)
    g['sc_w_out'] = _mm("d_sc_w_out", v, dya, ta=True)
    dv = _mm("d_v", dya, W['sc_w_out'], tb=True)
    g['m_w_out'] = _mm("d_m_w_out", yn, dym, ta=True)
    dyn = _mm("d_yn", dym, W['m_w_out'], tb=True)
    dy, dz, d_mnorm = _gnorm_bwd("gnorm_b", dyn, y, seg('z'), W['m_norm'])
    g['m_norm'] = d_mnorm.reshape(-1)
    dxbc_post, ddt, dpar = _ssd_bwd("ssd_b", dy, xbc, scal, acum_t, states, W['ssd_par'], l, H)
    g['m_dt_bias'] = dpar[0, :H]
    g['m_A_log'] = dpar[1, :H]
    g['m_D'] = dpar[2, :H]
    dxbc, d_mcw, d_mcb = _mconv_bwd("mconv_b", dxbc_post, proj, poffs['xbc'][0], W['m_taps'], Km, W['m_conv_b'], l)
    g['m_conv_w'] = d_mcw[:Km]
    g['m_conv_b'] = d_mcb.reshape(-1)
    dscb, dscc, dscx, d_scw = _scconv_bwd("scconv_b", dv, proj, poffs['scb'][0], poffs['scc'][0], poffs['scx'][0],
                                          W['sc_taps'], Ksc, l)
    g['sc_conv_w'] = d_scw[:Ksc]
    dproj = jnp.concatenate([dz, dscb, dscc, dscx, dga, dgm, dxbc], axis=1)
    du = _mm("d_u_main", dproj, W['w_in_p'], out_dtype=F32)
    dh, dg = _mm_rms_bwd("d_u_dt", ddt, W['w_dt'], du, h, W['mix_norm'], dh_out)
    d_wp = _mm("d_w_in_main", dproj, u, ta=True)
    d_wdt = _mm("d_w_in_dt", ddt, u, ta=True)
    g['w_in'] = _unperm_d_w_in(d_wp, d_wdt, dims)
    g['mix_norm'] = dg.reshape(-1)
    return dh, g


def _ple_layer_fwd(h, p_l, W):
    xn = _rms_fwd("ple_rms", h, W['ple_norm'])
    gpre = _mm("ple_gate", xn, W['ple_w_gate'])
    pp = _mm("ple_proj", p_l, W['ple_w_proj'], tb=True)
    hout = _ple_fwd("ple_f", h, gpre, pp)
    return hout, (xn, gpre, pp)


def _ple_layer_bwd(dh_out, h, p_l, W, saved, dep=None):
    xn, gpre, pp = saved
    g = {}
    dgpre, dpp = _ple_bwd("ple_b", dh_out, gpre, pp, dep=dep)
    g['ple_w_proj'] = _mm("d_ple_proj", dpp, p_l, ta=True)
    g['ple_w_gate'] = _mm("d_ple_gate", xn, dgpre, ta=True)
    dh, dg = _mm_rms_bwd("d_ple_xn", dgpre, W['ple_w_gate'], None, h, W['ple_norm'], dh_out, tb=True)
    g['ple_norm'] = dg.reshape(-1)
    return dh, g


def kernel(x, p, ffn1_norm, ffn1_wg, ffn1_wu, ffn1_wd, mix_norm, w_in, sc_conv_w, sc_w_out, m_conv_w, m_conv_b, m_dt_bias, m_A_log, m_D, m_norm, m_w_out, w_o, ffn2_norm, ffn2_wg, ffn2_wu, ffn2_wd, ple_norm, ple_w_gate, ple_w_proj, final_norm, loss_target, m_ffn1_norm, m_ffn1_wg, m_ffn1_wu, m_ffn1_wd, m_mix_norm, m_w_in, m_sc_conv_w, m_sc_w_out, m_m_conv_w, m_m_conv_b, m_m_dt_bias, m_m_A_log, m_m_D, m_m_norm, m_m_w_out, m_w_o, m_ffn2_norm, m_ffn2_wg, m_ffn2_wu, m_ffn2_wd, m_ple_norm, m_ple_w_gate, m_ple_w_proj, m_final_norm, v_ffn1_norm, v_ffn1_wg, v_ffn1_wu, v_ffn1_wd, v_mix_norm, v_w_in, v_sc_conv_w, v_sc_w_out, v_m_conv_w, v_m_conv_b, v_m_dt_bias, v_m_A_log, v_m_D, v_m_norm, v_m_w_out, v_w_o, v_ffn2_norm, v_ffn2_wg, v_ffn2_wu, v_ffn2_wd, v_ple_norm, v_ple_w_gate, v_ple_w_proj, v_final_norm):
    args = (x, p, ffn1_norm, ffn1_wg, ffn1_wu, ffn1_wd, mix_norm, w_in, sc_conv_w, sc_w_out, m_conv_w, m_conv_b, m_dt_bias, m_A_log, m_D, m_norm, m_w_out, w_o, ffn2_norm, ffn2_wg, ffn2_wu, ffn2_wd, ple_norm, ple_w_gate, ple_w_proj, final_norm, loss_target, m_ffn1_norm, m_ffn1_wg, m_ffn1_wu, m_ffn1_wd, m_mix_norm, m_w_in, m_sc_conv_w, m_sc_w_out, m_m_conv_w, m_m_conv_b, m_m_dt_bias, m_m_A_log, m_m_D, m_m_norm, m_m_w_out, m_w_o, m_ffn2_norm, m_ffn2_wg, m_ffn2_wu, m_ffn2_wd, m_ple_norm, m_ple_w_gate, m_ple_w_proj, m_final_norm, v_ffn1_norm, v_ffn1_wg, v_ffn1_wu, v_ffn1_wd, v_mix_norm, v_w_in, v_sc_conv_w, v_sc_w_out, v_m_conv_w, v_m_conv_b, v_m_dt_bias, v_m_A_log, v_m_D, v_m_norm, v_m_w_out, v_w_o, v_ffn2_norm, v_ffn2_wg, v_ffn2_wu, v_ffn2_wd, v_ple_norm, v_ple_w_gate, v_ple_w_proj, v_final_norm)
    names = ARG_NAMES + ['m_' + n for n in WEIGHTS] + ['v_' + n for n in WEIGHTS]
    A = dict(zip(names, args))
    depth = ffn1_norm.shape[0]
    me = 4 * lax.axis_index("x") + 2 * lax.axis_index("y") + lax.axis_index("c")

    dims = (x.shape[-1], m_norm.shape[1], m_conv_b.shape[1], m_dt_bias.shape[1], sc_conv_w.shape[1], m_conv_w.shape[1])
    kind = dict(BIG)

    def work(n, prefix=''):
        return jnp.swapaxes(A[prefix + n], 1, 2) if kind[n] == 'col' else A[prefix + n]

    wb = {n: work(n).astype(BF) for n, _ in BIG}
    srcs = [[wb[ms[0]] if len(ms) == 1 else jnp.concatenate([wb[n] for n in ms], axis=1) for ms in stage] for stage in STAGES]
    conv_g = _unpack(_exchange("gather_conv_taps", _pack([A[n] for n in CONVW]), True), [A[n].shape for n in CONVW])
    taps = {}
    for n, st in zip(CONVW, conv_g):
        taps[n] = _pad_taps(jnp.transpose(st, (1, 2, 0, 3)).reshape(depth, st.shape[2], N_DEV * st.shape[3]))
    ssd_par = _ssd_params(m_dt_bias, m_A_log, m_D)
    small3 = {n: A[n].reshape(depth, 1, -1) for n in SMALL}

    def stage_weights(W, s, l, lands):
        for ms, land in zip(STAGES[s], lands):
            off = 0
            for n in ms:
                r = wb[n].shape[1]
                W[n] = _full_from_shards(land if len(ms) == 1 else land[:, off:off + r])
                off += r
        if s == 1:
            W['w_in_p'], W['w_dt'] = _perm_w_in(W.pop('w_in'), dims)

    flight = {}

    via_sibling = {(0, 0), (0, 1)}

    def begin_layer(l, dep):
        for s in range(len(STAGES)):
            rels = NEAR_PEERS if (l, s) in via_sibling else ALL_PEERS
            sems, lands, dep = _xchg_begin(f"gather_begin{l}{'abc'[s]}", srcs[s], l, dep, rels)
            flight[(l, s)] = (sems, lands)
        return dep

    def end_stage(W, l, s, after):
        sems, lands = flight.pop((l, s))
        tag = f"{l}{'abc'[s]}"
        if (l, s) in via_sibling:
            lands = _xchg_end("gather_end" + tag, srcs[s], lands, sems, l, after, NEAR_PEERS)
            rsems, lands = _relay_begin("gather_relay" + tag, lands)
            lands = _relay_end("gather_relayed" + tag, lands, rsems, after)
        else:
            lands = _xchg_end("gather_end" + tag, srcs[s], lands, sems, l, after)
        stage_weights(W, s, l, lands)

    tok = begin_layer(0, taps['sc_conv_w'])
    h = x[0]
    saved = []
    layers = []
    for l in range(depth):
        W = {n: (small3[n], l) for n in SMALL}
        W.update(l=l, sc_taps=taps['sc_conv_w'], m_taps=taps['m_conv_w'], m_conv_b=small3['m_conv_b'], ssd_par=ssd_par)
        layers.append(W)
        end_stage(W, l, 0, tok if l == 0 else h)
        h1, s1 = _ffn_fwd("ffn1", h, W['ffn1_norm'], W['ffn1_wg'], W['ffn1_wu'], W['ffn1_wd'])
        end_stage(W, l, 1, h1)
        tok = begin_layer(l + 1, W['w_dt']) if l + 1 < depth else None
        h2, s2 = _mixer_fwd(h1, W, dims, dep=tok)
        end_stage(W, l, 2, h2)
        h3, s3 = _ffn_fwd("ffn2", h2, W['ffn2_norm'], W['ffn2_wg'], W['ffn2_wu'], W['ffn2_wd'])
        h4, s4 = _ple_layer_fwd(h3, p[l, 0], W)
        saved.append((h, h1, h2, h3, s1, s2, s3, s4))
        h = h4

    dh, loss_row, d_final = _loss_head("loss_head", h, final_norm, loss_target[0])
    loss = lax.psum(loss_row[0, 0], ("x", "y", "c"))

    def send_bufs(g, s):
        return [jnp.concatenate([_shards_from_full(g[n]) for n in ms], axis=1) if len(ms) > 1
                else _shards_from_full(g[ms[0]]) for ms in STAGES[s]]

    grads = [None] * depth
    pending = []

    def send_stage(g, l, s, dep):
        send = send_bufs(g, s)
        sems, lands, tok = _xchg_begin(f"scatter_begin{l}{'abc'[s]}", send, None, dep)
        pending.append((l, s, send, lands, sems))
        return tok

    tok = loss.reshape(1, 1)
    for l in reversed(range(depth)):
        W = layers[l]
        h0, h1, h2, h3, s1, s2, s3, s4 = saved[l]
        g = {}
        dh, g4 = _ple_layer_bwd(dh, h3, p[l, 0], W, s4, dep=tok)
        g.update(g4)
        dh, dg, d_wg, d_wu, d_wd = _ffn_bwd("ffn2", dh, h2, W['ffn2_norm'], W['ffn2_wg'], W['ffn2_wu'], W['ffn2_wd'], s3)
        g.update(ffn2_norm=dg.reshape(-1), ffn2_wg=d_wg, ffn2_wu=d_wu, ffn2_wd=d_wd)
        tok = send_stage(g, l, 2, dh)
        dh, g2 = _mixer_bwd(dh, h1, W, dims, s2, dep=tok)
        g.update(g2)
        tok = send_stage(g, l, 1, dh)
        dh, dg, d_wg, d_wu, d_wd = _ffn_bwd("ffn1", dh, h0, W['ffn1_norm'], W['ffn1_wg'], W['ffn1_wu'], W['ffn1_wd'], s1, dep=tok)
        g.update(ffn1_norm=dg.reshape(-1), ffn1_wg=d_wg, ffn1_wu=d_wu, ffn1_wd=d_wd)
        grads[l] = g
        tok = send_stage(g, l, 0, dh)
    grad_x = dh[None]

    g_lands = [[None] * len(STAGES) for _ in range(depth)]
    big_res = [{}, {}, {}, {}]

    def finish(entries, after):
        for l, s, send, lands, sems in entries:
            got = _xchg_end(f"scatter_end{l}{'abc'[s]}", send, lands, sems, None, after)
            after = got[0]
            g_lands[l][s] = got
        return after

    def adam_stages(stages):
        res = None
        for s in stages:
            for gi, ms in enumerate(STAGES[s]):
                off = 0
                for n in ms:
                    res = _sum_adam("adamw_" + n, [g_lands[l][s][gi] for l in range(depth)], off, work(n), work(n, 'm_'), work(n, 'v_'))
                    for k in range(4):
                        big_res[k][n] = jnp.swapaxes(res[k], 1, 2) if kind[n] == 'col' else res[k]
                    off += wb[n].shape[1]
        return res[0]

    after = finish(pending[:-1], tok)
    after = adam_stages(range(1, len(STAGES)))
    after = finish(pending[-1:], after)
    adam_stages([0])

    small_names = SMALL + CONVW
    small_parts = [jnp.stack([grads[l][n] for l in range(depth)]) for n in small_names] + [d_final.reshape(-1)]
    small_sum = _sum8("sum_small", _exchange("gather_small_grads", _pack(small_parts), True, dep=after))
    sg = dict(zip(small_names + ['final_norm'], _unpack(small_sum, [a.shape for a in small_parts])))
    for n in CONVW:
        c = A[n].shape[-1]
        sg[n] = lax.dynamic_slice_in_dim(sg[n], me * c, c, axis=2)
    s_order = small_names + ['final_norm']
    s_shapes = [sg[n].shape for n in s_order]
    s_out = _adam_flat("adamw_small", _pack([sg[n] for n in s_order]), _pack([A[n] for n in s_order]),
                       _pack([A['m_' + n] for n in s_order]), _pack([A['v_' + n] for n in s_order]))
    small_res = [sg] + [dict(zip(s_order, _unpack(flat, s_shapes))) for flat in s_out]

    outs = [loss, grad_x]
    for k in range(4):
        for n in WEIGHTS:
            outs.append(big_res[k][n] if n in big_res[k] else small_res[k][n])
    return tuple(outs)
```
